```python
import jax
import jax.numpy as jnp
from jax import lax
import numpy as np

D_MODEL = 2048
BATCH = 16
SEQ = 256
DEPTH = 2
DEC_BATCH = 2
DEC_SEQ = 1024
PAST_LEN = 512

GRID_W = 64
MIX_W = D_MODEL // 4
CONV_K = 3
NA_DH = 64
NA_HEADS = MIX_W // NA_DH
NA_WIN_R = 8
NA_WIN_C = 16
NA_QBC = 16
NA_KBC = 2 * NA_WIN_C
NA_SCALE = NA_DH ** -0.5
ATT_QBLOCK = 128
ROPE_THETA = 10000.0
RW_DH = 64
RW_HEADS = MIX_W // RW_DH
RW_W_RANK = 64
RW_A_RANK = 64
RW_G_RANK = 128
RW_DECAY_SCALE = 0.606531
RW_GN_EPS = 64e-5
ML_DH = 128
ML_HEADS = MIX_W // ML_DH
ML_CHUNK = 64
N_EXPERTS = 16
N_EXPERT_GROUPS = 4
TOP_K = 2
D_EXPERT = 512
EPS = 1e-6
NEG_INF = -1e30

PROJ_SPLITS = (
    ('cv_b', MIX_W), ('cv_c', MIX_W), ('cv_h', MIX_W),
    ('na_q', MIX_W), ('na_k', MIX_W), ('na_v', MIX_W),
    ('rw_r', MIX_W), ('rw_k', MIX_W), ('rw_v', MIX_W),
    ('rw_wl', RW_W_RANK), ('rw_al', RW_A_RANK), ('rw_gl', RW_G_RANK),
    ('ml_q', MIX_W), ('ml_k', MIX_W), ('ml_v', MIX_W), ('ml_o', MIX_W),
    ('ml_gate', 4 * ML_HEADS),
)
P_IN = sum(w for _, w in PROJ_SPLITS)

kernel_name = 'hybrid_prefix_diffusion_trunk_step'


def rms_norm(x, g, eps=EPS):
    xf = x.astype(jnp.float32)
    y = xf * lax.rsqrt(jnp.mean(xf * xf, axis=-1, keepdims=True) + eps)
    return (y * g.astype(jnp.float32)).astype(x.dtype)


def modulate(x, g, shift, scale):
    return rms_norm(x, g) * (1 + scale) + shift


def adaln(cvec, w_mod, b_mod):
    m = jax.nn.silu(cvec) @ w_mod + b_mod
    return jnp.split(m, 6, axis=-1)


def split_heads(x, n):
    return x.reshape(x.shape[:-1] + (n, x.shape[-1] // n))


def split_proj(p):
    out, o = {}, 0
    for name, w in PROJ_SPLITS:
        out[name] = p[..., o:o + w]
        o += w
    return out


def short_conv_mixer(b, c, h, conv_w):
    T = h.shape[1]
    u = c * h
    up = jnp.pad(u, ((0, 0), (CONV_K // 2, CONV_K // 2), (0, 0)))
    z = sum(up[:, j:j + T] * conv_w[j] for j in range(CONV_K))
    return b * z


def rope_1d(x, pos):
    half = x.shape[-1] // 2
    freq = ROPE_THETA ** (-jnp.arange(half, dtype=jnp.float32) / half)
    ang = pos.astype(jnp.float32)[:, None] * freq
    cos = jnp.cos(ang)[:, None, :].astype(x.dtype)
    sin = jnp.sin(ang)[:, None, :].astype(x.dtype)
    x1, x2 = x[..., :half], x[..., half:]
    return jnp.concatenate([x1 * cos - x2 * sin, x1 * sin + x2 * cos], axis=-1)


def axial_rope_2d(x):
    T, d = x.shape[1], x.shape[-1]
    t = jnp.arange(T)
    return jnp.concatenate([rope_1d(x[..., :d // 2], t // GRID_W),
                            rope_1d(x[..., d // 2:], t % GRID_W)], axis=-1)


def context_attention(q, k, v):
    B, H, L, d = q.shape
    qb = jnp.moveaxis(q.reshape(B, H, L // ATT_QBLOCK, ATT_QBLOCK, d), 2, 0)

    def block(q_i):
        s = jnp.einsum('bhqd,bhkd->bhqk', q_i, k).astype(jnp.float32)
        p = jax.nn.softmax(s, axis=-1).astype(v.dtype)
        return jnp.einsum('bhqk,bhkd->bhqd', p, v)

    o = lax.map(block, qb)
    return jnp.moveaxis(o, 0, 2).reshape(B, H, L, d)


def neighbourhood_attention(q, k, v, k_ctx, v_ctx, rpb):
    B, T, H, d = q.shape
    rows = T // GRID_W
    wr = min(NA_WIN_R, rows)
    n_cb = GRID_W // NA_QBC
    c0 = np.arange(n_cb) * NA_QBC
    kc0 = np.clip(c0 - NA_WIN_C // 2, 0, GRID_W - NA_KBC)
    kcols = kc0[:, None] + np.arange(NA_KBC)
    qcols = c0[:, None] + np.arange(NA_QBC)
    wstart = np.clip(qcols - NA_WIN_C // 2, 0, GRID_W - NA_WIN_C)
    colmask = ((kcols[:, None, :] >= wstart[:, :, None]) &
               (kcols[:, None, :] < wstart[:, :, None] + NA_WIN_C))
    mask = np.broadcast_to(colmask[:, :, None, :], (n_cb, NA_QBC, wr, NA_KBC)).reshape(n_cb, NA_QBC, wr * NA_KBC)
    dc = np.clip(kcols[:, None, :] - qcols[:, :, None], -(NA_WIN_C - 1), NA_WIN_C - 1) + NA_WIN_C - 1
    rpb_c = rpb[:, :, dc]
    kg = k.reshape(B, rows, GRID_W, H, d).transpose(0, 3, 1, 2, 4)
    vg = v.reshape(B, rows, GRID_W, H, d).transpose(0, 3, 1, 2, 4)
    qg = q.reshape(B, rows, n_cb, NA_QBC, H, d).transpose(1, 0, 4, 2, 3, 5)
    nw = wr * NA_KBC

    def row_step(args):
        r, q_r = args
        rs = jnp.clip(r - wr // 2, 0, rows - wr)
        k_r = lax.dynamic_slice_in_dim(kg, rs, wr, axis=2)
        v_r = lax.dynamic_slice_in_dim(vg, rs, wr, axis=2)
        k_b = k_r[:, :, :, kcols].transpose(0, 1, 3, 2, 4, 5).reshape(B, H, n_cb, nw, d)
        v_b = v_r[:, :, :, kcols].transpose(0, 1, 3, 2, 4, 5).reshape(B, H, n_cb, nw, d)
        dr = rs + jnp.arange(wr) - r + NA_WIN_R - 1
        bias = jnp.take(rpb_c, dr, axis=1).transpose(0, 2, 3, 1, 4).reshape(H, n_cb, NA_QBC, nw)
        s_w = jnp.einsum('bhcqd,bhckd->bhcqk', q_r, k_b).astype(jnp.float32) + bias.astype(jnp.float32)
        s_w = jnp.where(mask, s_w, NEG_INF)
        s_c = jnp.einsum('bhcqd,bhld->bhcql', q_r, k_ctx).astype(jnp.float32)
        p = jax.nn.softmax(jnp.concatenate([s_w, s_c], axis=-1), axis=-1).astype(v.dtype)
        return (jnp.einsum('bhcqk,bhckd->bhcqd', p[..., :nw], v_b) +
                jnp.einsum('bhcql,bhld->bhcqd', p[..., nw:], v_ctx))

    o = lax.map(row_step, (jnp.arange(rows), qg))
    return o.transpose(1, 0, 3, 4, 2, 5).reshape(B, T, H * d)


def rwkv7_scan(r, w, k, v, kk, ka, s0, reverse):
    xs = tuple(jnp.swapaxes(t, 0, 1) for t in (r, w, k, v, kk, ka))

    def step(s, inp):
        r_t, w_t, k_t, v_t, kk_t, ka_t = inp
        sa = jnp.einsum('bhvk,bhk->bhv', s, kk_t)
        s = s * w_t[:, :, None, :] - sa[..., None] * ka_t[:, :, None, :] + v_t[..., None] * k_t[:, :, None, :]
        return s, jnp.einsum('bhvk,bhk->bhv', s, r_t)

    s, y = lax.scan(step, s0, xs, reverse=reverse)
    return jnp.swapaxes(y, 0, 1), s


def rwkv7_mixer(u, lw, s0):
    f32 = jnp.float32
    B, T, _ = u['rw_r'].shape
    r = split_heads(u['rw_r'], RW_HEADS).astype(f32)
    k = split_heads(u['rw_k'], RW_HEADS).astype(f32)
    v = split_heads(u['rw_v'], RW_HEADS).astype(f32)
    w_pre = lw['rw_w0'][:, None, None, :] + jnp.einsum('btr,zrw->zbtw', jnp.tanh(u['rw_wl']), lw['rw_w_up'])
    w = split_heads(jnp.exp(-RW_DECAY_SCALE * jax.nn.sigmoid(w_pre.astype(f32))), RW_HEADS)
    a_pre = lw['rw_a0'][:, None, None, :] + jnp.einsum('btr,zrw->zbtw', u['rw_al'], lw['rw_a_up'])
    a = split_heads(jax.nn.sigmoid(a_pre.astype(f32)), RW_HEADS)
    g = jnp.einsum('btr,rw->btw', jax.nn.sigmoid(u['rw_gl']), lw['rw_g_up']).astype(f32)
    kk = k * split_heads(lw['rw_k_k'], RW_HEADS).astype(f32)
    kk = kk * lax.rsqrt(jnp.sum(kk * kk, axis=-1, keepdims=True) + EPS)
    kd = k[None] * (1 + (a - 1) * split_heads(lw['rw_k_a'], RW_HEADS).astype(f32))
    s0 = s0.astype(f32)
    y_f, s_f = rwkv7_scan(r, w[0], kd[0], v, kk, kk * a[0], s0[:, 0], False)
    y_b, s_b = rwkv7_scan(r, w[1], kd[1], v, kk, kk * a[1], s0[:, 1], True)
    y = y_f + y_b
    mu = jnp.mean(y, axis=-1, keepdims=True)
    var = jnp.mean(jnp.square(y - mu), axis=-1, keepdims=True)
    yn = ((y - mu) * lax.rsqrt(var + RW_GN_EPS)).reshape(B, T, MIX_W)
    yn = yn * lw['rw_ln_g'].astype(f32) + lw['rw_ln_b'].astype(f32)
    bonus = jnp.sum(r * k * split_heads(lw['rw_r_k'], RW_HEADS).astype(f32), axis=-1, keepdims=True) * v
    out = (yn + bonus.reshape(B, T, MIX_W)) * g
    return out.astype(u['rw_r'].dtype), jnp.stack([s_f, s_b], axis=1)


def mlstm_chunkwise(q, k, v, ig, fg, c0, n0, m0):
    B, H, T, d = q.shape
    L = ML_CHUNK
    nc = T // L
    ch = lambda x: jnp.moveaxis(x.reshape((B, H, nc, L) + x.shape[3:]), 2, 0)
    causal = np.tril(np.ones((L, L), dtype=bool))

    def step(carry, inp):
        c, n, m = carry
        qc, kc, vc, ic, fc = inp
        b = jnp.cumsum(jax.nn.log_sigmoid(fc), axis=-1)
        a_t = b + m[..., None]
        dmat = jnp.where(causal, b[..., :, None] - b[..., None, :] + ic[..., None, :], NEG_INF)
        m_t = jnp.maximum(a_t, jnp.max(dmat, axis=-1))
        dw = jnp.exp(dmat - m_t[..., None])
        inter = jnp.exp(a_t - m_t)
        s = jnp.einsum('bhtd,bhsd->bhts', qc, kc) * dw
        num = inter[..., None] * jnp.einsum('bhtd,bhed->bhte', qc, c) + jnp.einsum('bhts,bhse->bhte', s, vc)
        den = inter * jnp.einsum('bhtd,bhd->bht', qc, n) + jnp.sum(s, axis=-1)
        h = num / jnp.maximum(jnp.abs(den), jnp.exp(-m_t))[..., None]
        g_s = b[..., -1:] - b + ic
        a_l = b[..., -1] + m
        m_new = jnp.maximum(a_l, jnp.max(g_s, axis=-1))
        decay = jnp.exp(a_l - m_new)
        wgt = jnp.exp(g_s - m_new[..., None])
        c = decay[..., None, None] * c + jnp.einsum('bhs,bhse,bhsd->bhed', wgt, vc, kc)
        n = decay[..., None] * n + jnp.einsum('bhs,bhsd->bhd', wgt, kc)
        return (c, n, m_new), h

    (c, n, m), h = lax.scan(step, (c0, n0, m0), (ch(q), ch(k), ch(v), ch(ig), ch(fg)))
    return jnp.moveaxis(h, 0, 2).reshape(B, H, T, d), c, n, m


def mlstm_mixer(u, lw, c0, n0, m0):
    f32 = jnp.float32
    B, T, _ = u['ml_q'].shape
    to_bhtd = lambda x: jnp.swapaxes(split_heads(x, ML_HEADS), 1, 2).astype(f32)
    q = to_bhtd(u['ml_q']) * (ML_DH ** -0.5)
    k = to_bhtd(u['ml_k'])
    v = to_bhtd(u['ml_v'])
    gates = (u['ml_gate'].reshape(B, T, 2, 2, ML_HEADS) + lw['ml_gate_b']).astype(f32)
    gates = jnp.transpose(gates, (2, 3, 0, 4, 1))
    c0, n0, m0 = c0.astype(f32), n0.astype(f32), m0.astype(f32)
    h_f, c_f, n_f, m_f = mlstm_chunkwise(q, k, v, gates[0, 0], gates[0, 1], c0[:, 0], n0[:, 0], m0[:, 0])
    flip = lambda x: jnp.flip(x, axis=2)
    h_b, c_b, n_b, m_b = mlstm_chunkwise(flip(q), flip(k), flip(v), flip(gates[1, 0]), flip(gates[1, 1]),
                                         c0[:, 1], n0[:, 1], m0[:, 1])
    h = jnp.swapaxes(h_f + flip(h_b), 1, 2)
    h = rms_norm(h, split_heads(lw['ml_norm_g'], ML_HEADS)).reshape(B, T, MIX_W)
    out = jax.nn.sigmoid(u['ml_o'].astype(f32)) * h
    return (out.astype(u['ml_q'].dtype), jnp.stack([c_f, c_b], axis=1),
            jnp.stack([n_f, n_b], axis=1), jnp.stack([m_f, m_b], axis=1))


def token_mixers(h, lw, ctx_cache):
    B, T, _ = h.shape
    u = split_proj(h @ lw['w_in'])
    y_a = short_conv_mixer(u['cv_b'], u['cv_c'], u['cv_h'], lw['conv_w'])
    q = rms_norm(split_heads(u['na_q'], NA_HEADS), lw['na_q_g'])
    k = rms_norm(split_heads(u['na_k'], NA_HEADS), lw['na_k_g'])
    v = split_heads(u['na_v'], NA_HEADS)
    if ctx_cache is None:
        k_ctx = jnp.swapaxes(k, 1, 2)
        v_ctx = jnp.swapaxes(v, 1, 2)
        y_b = context_attention(jnp.swapaxes(q, 1, 2) * NA_SCALE, k_ctx, v_ctx)
        y_b = jnp.swapaxes(y_b, 1, 2).reshape(B, T, MIX_W)
        s_rw0 = jnp.zeros((B, 2, RW_HEADS, RW_DH, RW_DH), jnp.float32)
        c0 = jnp.zeros((B, 2, ML_HEADS, ML_DH, ML_DH), jnp.float32)
        n0 = jnp.zeros((B, 2, ML_HEADS, ML_DH), jnp.float32)
        m0 = jnp.zeros((B, 2, ML_HEADS), jnp.float32)
    else:
        k_ctx, v_ctx, s_rw0, c0, n0, m0 = ctx_cache
        y_b = neighbourhood_attention(axial_rope_2d(q) * NA_SCALE, axial_rope_2d(k), v,
                                      k_ctx.astype(q.dtype), v_ctx.astype(v.dtype), lw['na_rpb'])
    y_c, s_rw = rwkv7_mixer(u, lw, s_rw0)
    y_d, c_m, n_m, m_m = mlstm_mixer(u, lw, c0, n0, m0)
    out = jnp.concatenate([y_a, y_b.astype(h.dtype), y_c, y_d], axis=-1) @ lw['w_out']
    return out, (k_ctx, v_ctx, s_rw, c_m, n_m, m_m)


def moe_ffn(x, router_w, router_b, w1, w3, w2):
    shp = x.shape
    xf = x.reshape(-1, shp[-1])
    f32 = jnp.float32
    scores = jax.nn.softmax((xf @ router_w).astype(f32), axis=-1)
    sel = scores + router_b.astype(f32)
    per = N_EXPERTS // N_EXPERT_GROUPS
    grp_score = jnp.sum(lax.top_k(sel.reshape(-1, N_EXPERT_GROUPS, per), TOP_K)[0], axis=-1)
    grp = jnp.argmax(grp_score, axis=-1)
    in_grp = (jnp.arange(N_EXPERTS) // per)[None, :] == grp[:, None]
    _, idx = lax.top_k(jnp.where(in_grp, sel, NEG_INF), TOP_K)
    wts = jnp.take_along_axis(scores, idx, axis=-1)
    wts = wts / jnp.sum(wts, axis=-1, keepdims=True)
    gates = jnp.sum(jax.nn.one_hot(idx, N_EXPERTS, dtype=f32) * wts[..., None], axis=1)
    h = jax.nn.silu(jnp.einsum('nd,edf->nef', xf, w1)) * jnp.einsum('nd,edf->nef', xf, w3)
    y = jnp.einsum('nef,efd->nd', h * gates[..., None].astype(h.dtype), w2)
    return y.reshape(shp)


def setup_inputs(seed: int = 0) -> dict:
    key = jax.random.key(seed)
    ks = iter(list(jax.random.split(key, 40)))
    f32 = jnp.float32
    nrm = lambda shape, s: jax.random.normal(next(ks), shape, f32) * s
    D, W = D_MODEL, MIX_W
    return {
        'x_prompt': nrm((BATCH, SEQ, D), 1.0),
        'x_sample': nrm((DEC_BATCH, DEC_SEQ, D), 1.0),
        'cache_na_k': nrm((DEC_BATCH, DEPTH, NA_HEADS, PAST_LEN, NA_DH), 1.0),
        'cache_na_v': nrm((DEC_BATCH, DEPTH, NA_HEADS, PAST_LEN, NA_DH), 1.0),
        'state_rwkv': nrm((DEC_BATCH, DEPTH, 2, RW_HEADS, RW_DH, RW_DH), 0.3),
        'state_mlstm_c': nrm((DEC_BATCH, DEPTH, 2, ML_HEADS, ML_DH, ML_DH), 0.1),
        'state_mlstm_n': nrm((DEC_BATCH, DEPTH, 2, ML_HEADS, ML_DH), 0.3),
        'state_mlstm_m': nrm((DEC_BATCH, DEPTH, 2, ML_HEADS), 0.5),
        'c': nrm((DEC_BATCH, D), 1.0),
        'c_ctx': nrm((D,), 1.0),
        'norm1_g': 1.0 + nrm((DEPTH, D), 0.02),
        'norm2_g': 1.0 + nrm((DEPTH, D), 0.02),
        'w_mod': nrm((DEPTH, D, 6 * D), 0.2 * D ** -0.5),
        'b_mod': nrm((DEPTH, 6 * D), 0.01),
        'w_in': nrm((DEPTH, D, P_IN), D ** -0.5),
        'conv_w': nrm((DEPTH, CONV_K, W), 0.5),
        'na_q_g': 1.0 + nrm((DEPTH, NA_DH), 0.02),
        'na_k_g': 1.0 + nrm((DEPTH, NA_DH), 0.02),
        'na_rpb': nrm((DEPTH, NA_HEADS, 2 * NA_WIN_R - 1, 2 * NA_WIN_C - 1), 0.02),
        'rw_w0': nrm((DEPTH, 2, W), 1.0),
        'rw_w_up': nrm((DEPTH, 2, RW_W_RANK, W), 0.1 * RW_W_RANK ** -0.5),
        'rw_a0': nrm((DEPTH, 2, W), 0.1),
        'rw_a_up': nrm((DEPTH, 2, RW_A_RANK, W), 0.1 * RW_A_RANK ** -0.5),
        'rw_g_up': nrm((DEPTH, RW_G_RANK, W), RW_G_RANK ** -0.5),
        'rw_k_k': 0.85 + nrm((DEPTH, W), 0.02),
        'rw_k_a': 1.0 + nrm((DEPTH, W), 0.02),
        'rw_r_k': nrm((DEPTH, W), 0.1),
        'rw_ln_g': 1.0 + nrm((DEPTH, W), 0.02),
        'rw_ln_b': nrm((DEPTH, W), 0.01),
        'ml_gate_b': nrm((DEPTH, 2, 2, ML_HEADS), 0.1) + jnp.array([0.0, 3.0], f32)[None, None, :, None],
        'ml_norm_g': 1.0 + nrm((DEPTH, W), 0.02),
        'w_out': nrm((DEPTH, D, D), D ** -0.5),
        'router_w': nrm((D, N_EXPERTS), D ** -0.5),
        'router_b': nrm((N_EXPERTS,), 0.01),
        'moe_w1': nrm((DEPTH, N_EXPERTS, D, D_EXPERT), D ** -0.5),
        'moe_w3': nrm((DEPTH, N_EXPERTS, D, D_EXPERT), D ** -0.5),
        'moe_w2': nrm((DEPTH, N_EXPERTS, D_EXPERT, D), D_EXPERT ** -0.5),
    }


def reference(x_prompt, x_sample, cache_na_k, cache_na_v, state_rwkv, state_mlstm_c, state_mlstm_n,
              state_mlstm_m, c, c_ctx, norm1_g, norm2_g, w_mod, b_mod, w_in, conv_w, na_q_g, na_k_g,
              na_rpb, rw_w0, rw_w_up, rw_a0, rw_a_up, rw_g_up, rw_k_k, rw_k_a, rw_r_k, rw_ln_g, rw_ln_b,
              ml_gate_b, ml_norm_g, w_out, router_w, router_b, moe_w1, moe_w3, moe_w2):
    xp = x_prompt
    xs = x_sample
    new_k, new_v, new_rw, new_c, new_n, new_m = [], [], [], [], [], []
    for l in range(DEPTH):
        lw = {
            'w_in': w_in[l], 'conv_w': conv_w[l], 'na_q_g': na_q_g[l], 'na_k_g': na_k_g[l],
            'na_rpb': na_rpb[l], 'rw_w0': rw_w0[l], 'rw_w_up': rw_w_up[l], 'rw_a0': rw_a0[l],
            'rw_a_up': rw_a_up[l], 'rw_g_up': rw_g_up[l], 'rw_k_k': rw_k_k[l], 'rw_k_a': rw_k_a[l],
            'rw_r_k': rw_r_k[l], 'rw_ln_g': rw_ln_g[l], 'rw_ln_b': rw_ln_b[l],
            'ml_gate_b': ml_gate_b[l], 'ml_norm_g': ml_norm_g[l], 'w_out': w_out[l],
        }
        sh1, sc1, g1, sh2, sc2, g2 = adaln(c_ctx, w_mod[l], b_mod[l])
        mix, ctx_state = token_mixers(modulate(xp, norm1_g[l], sh1, sc1), lw, None)
        xp = xp + g1 * mix
        xp = xp + g2 * moe_ffn(modulate(xp, norm2_g[l], sh2, sc2), router_w, router_b,
                               moe_w1[l], moe_w3[l], moe_w2[l])
        new_k.append(ctx_state[0])
        new_v.append(ctx_state[1])
        new_rw.append(ctx_state[2])
        new_c.append(ctx_state[3])
        new_n.append(ctx_state[4])
        new_m.append(ctx_state[5])
        sh1, sc1, g1, sh2, sc2, g2 = [t[:, None, :] for t in adaln(c, w_mod[l], b_mod[l])]
        cache_l = (cache_na_k[:, l], cache_na_v[:, l], state_rwkv[:, l],
                   state_mlstm_c[:, l], state_mlstm_n[:, l], state_mlstm_m[:, l])
        mix, _ = token_mixers(modulate(xs, norm1_g[l], sh1, sc1), lw, cache_l)
        xs = xs + g1 * mix
        xs = xs + g2 * moe_ffn(modulate(xs, norm2_g[l], sh2, sc2), router_w, router_b,
                               moe_w1[l], moe_w3[l], moe_w2[l])
    new_na_k = jnp.stack(new_k, axis=1)
    new_na_v = jnp.stack(new_v, axis=1)
    new_rwkv = jnp.stack(new_rw, axis=1)
    new_mlstm_c = jnp.stack(new_c, axis=1)
    new_mlstm_n = jnp.stack(new_n, axis=1)
    new_mlstm_m = jnp.stack(new_m, axis=1)
    return (xp, xs, new_na_k, new_na_v, new_rwkv, new_mlstm_c, new_mlstm_n, new_mlstm_m)
```

```python
import functools

import numpy as np
import jax
import jax.numpy as jnp
from jax import lax
from jax.experimental import pallas as pl
from jax.experimental.pallas import tpu as pltpu

F32 = jnp.float32
BF16 = jnp.bfloat16
HI = lax.Precision.HIGHEST

D_MODEL = 2048
BATCH = 16
SEQ = 256
DEPTH = 2
DEC_BATCH = 2
DEC_SEQ = 1024
PAST_LEN = 512
GRID_W = 64
MIX_W = D_MODEL // 4
CONV_K = 3
NA_DH = 64
NA_HEADS = MIX_W // NA_DH
NA_WIN_R = 8
NA_WIN_C = 16
NA_SCALE = NA_DH ** -0.5
ROPE_THETA = 10000.0
RW_DH = 64
RW_HEADS = MIX_W // RW_DH
RW_W_RANK = 64
RW_A_RANK = 64
RW_G_RANK = 128
RW_DECAY_SCALE = 0.606531
RW_GN_EPS = 64e-5
ML_DH = 128
ML_HEADS = MIX_W // ML_DH
N_EXPERTS = 16
N_EXPERT_GROUPS = 4
D_EXPERT = 512
EPS = 1e-6
NEG_INF = -1e30

N_PROMPT = BATCH * SEQ
N_SAMPLE = DEC_BATCH * DEC_SEQ
N_TOK = N_PROMPT + N_SAMPLE
MOD_ROWS = 8
CHUNK = 64
SUB = 16
P_BLOCKS = 14
P_PAD = P_BLOCKS * MIX_W
(CB_CVB, CB_CVC, CB_CVH, CB_NAQ, CB_NAK, CB_NAV, CB_RWR, CB_RWK, CB_RWV,
 CB_MLQ, CB_MLK, CB_MLV, CB_MLO, CB_SMALL) = range(P_BLOCKS)
SM_WL, SM_AL, SM_GL, SM_GATE = 0, 64, 128, 256
VMEM_LIMIT = 56 * 1024 * 1024


def _mm(a, b, prec=None):
    return jnp.dot(a, b, precision=prec, preferred_element_type=F32)


def _mm_nt(a, b, prec=None):
    return lax.dot_general(a, b, (((1,), (1,)), ((), ())), precision=prec, preferred_element_type=F32)


def _mm_tn(a, b, prec=None):
    return _mm(a.T, b, prec)


def _sigmoid(x):
    return 1.0 / (1.0 + jnp.exp(-x))


def _full(shape):
    n = len(shape)
    return pl.BlockSpec(shape, lambda *_: (0,) * n)


def _params(sem):
    return pltpu.CompilerParams(dimension_semantics=sem, vmem_limit_bytes=VMEM_LIMIT)


def _mod_row(i, tm):
    n_prompt_tiles = N_PROMPT // tm
    tiles_per_sample = DEC_SEQ // tm
    return jnp.where(i < n_prompt_tiles, 0, 1 + (i - n_prompt_tiles) // tiles_per_sample)


def _mod_spec(layer, chunk, tm, tn=D_MODEL, with_j=False):
    if with_j:
        return pl.BlockSpec((None, None, None, 1, tn), lambda i, j: (layer, _mod_row(i, tm), chunk, 0, j))
    return pl.BlockSpec((None, None, None, 1, tn), lambda i, *_: (layer, _mod_row(i, tm), chunk, 0, 0))


def _tri_masks(n):
    t = lax.broadcasted_iota(jnp.int32, (n, n), 0)
    s = lax.broadcasted_iota(jnp.int32, (n, n), 1)
    incl = ((s <= t).astype(F32), (s >= t).astype(F32))
    strict = ((s < t).astype(F32), (s > t).astype(F32))
    return incl, strict, t, s


def _adaln_kernel(cv_ref, w_ref, b_ref, o_ref):
    cv = cv_ref[...]
    o_ref[...] = _mm(cv * _sigmoid(cv), w_ref[...], HI) + b_ref[...]


def _adaln(cvecs, w_mod, b_mod):
    tn = 1024
    n_out = 6 * D_MODEL
    return pl.pallas_call(
        _adaln_kernel,
        out_shape=jax.ShapeDtypeStruct((DEPTH, MOD_ROWS, n_out), F32),
        grid=(DEPTH, n_out // tn),
        in_specs=[_full((MOD_ROWS, D_MODEL)),
                  pl.BlockSpec((None, D_MODEL, tn), lambda l, j: (l, 0, j)),
                  pl.BlockSpec((None, 1, tn), lambda l, j: (l, 0, j))],
        out_specs=pl.BlockSpec((None, MOD_ROWS, tn), lambda l, j: (l, 0, j)),
        compiler_params=_params(("parallel", "parallel")),
        name="adaln",
    )(cvecs, w_mod, b_mod.reshape(DEPTH, 1, n_out))


def _modulated(x, g, sh, sc):
    y = x * lax.rsqrt(jnp.mean(x * x, axis=-1, keepdims=True) + EPS) * g
    return y * (1.0 + sc) + sh


def _inproj_kernel(x_ref, g_ref, sh_ref, sc_ref, w_ref, o_ref, xm_ref):
    @pl.when(pl.program_id(1) == 0)
    def _():
        xm_ref[...] = _modulated(x_ref[...], g_ref[...], sh_ref[...], sc_ref[...]).astype(BF16)

    o_ref[...] = _mm(xm_ref[...], w_ref[...])


def _inproj(x, norm_g, mod, w_in_b, layer):
    tm, tn = 512, 1024
    return pl.pallas_call(
        _inproj_kernel,
        out_shape=jax.ShapeDtypeStruct((N_TOK, P_PAD), F32),
        grid=(N_TOK // tm, P_PAD // tn),
        in_specs=[pl.BlockSpec((tm, D_MODEL), lambda i, j: (i, 0)),
                  pl.BlockSpec((None, 1, D_MODEL), lambda i, j: (layer, 0, 0)),
                  _mod_spec(layer, 0, tm), _mod_spec(layer, 1, tm),
                  pl.BlockSpec((None, D_MODEL, tn), lambda i, j: (layer, 0, j))],
        out_specs=pl.BlockSpec((tm, tn), lambda i, j: (i, j)),
        scratch_shapes=[pltpu.VMEM((tm, D_MODEL), BF16)],
        compiler_params=_params(("parallel", "arbitrary")),
        name="inproj",
    )(x, norm_g.reshape(DEPTH, 1, D_MODEL), mod, mod, w_in_b)


def _conv_mix(b, c, h, w):
    u = c * h
    n = u.shape[0]
    row = lax.broadcasted_iota(jnp.int32, u.shape, 0)
    prev = jnp.where(row == 0, 0.0, pltpu.roll(u, 1, axis=0))
    nxt = jnp.where(row == n - 1, 0.0, pltpu.roll(u, n - 1, axis=0))
    return b * (prev * w[0:1] + u * w[1:2] + nxt * w[2:3])


def _head_rms(x, g):
    return x * lax.rsqrt(jnp.mean(x * x, axis=-1, keepdims=True) + EPS) * g


def _attn_prompt_kernel(cb_ref, cc_ref, ch_ref, q_ref, k_ref, v_ref, cw_ref, qg_ref, kg_ref,
                        ya_ref, yb_ref, nk_ref, nv_ref):
    ya_ref[...] = _conv_mix(cb_ref[...], cc_ref[...], ch_ref[...], cw_ref[...]).astype(ya_ref.dtype)
    for h in range(NA_HEADS):
        sl = slice(h * NA_DH, (h + 1) * NA_DH)
        qn = _head_rms(q_ref[:, sl], qg_ref[...]) * NA_SCALE
        kn = _head_rms(k_ref[:, sl], kg_ref[...])
        vh = v_ref[:, sl]
        s = _mm_nt(qn.astype(BF16), kn.astype(BF16))
        p = jnp.exp(s - jnp.max(s, axis=-1, keepdims=True))
        o = _mm(p.astype(BF16), vh.astype(BF16)) / jnp.sum(p, axis=-1, keepdims=True)
        yb_ref[:, sl] = o.astype(yb_ref.dtype)
        nk_ref[h] = kn
        nv_ref[h] = vh


def _u_spec(rows, col_block, row_off_blocks=0):
    return pl.BlockSpec((rows, MIX_W), lambda b: (b + row_off_blocks, col_block))


def _attn_prompt(u, conv_w, q_g, k_g, layer, n_seq=BATCH, seq=SEQ):
    lw = lambda shape: pl.BlockSpec((None,) + shape, lambda b: (layer,) + (0,) * len(shape))
    y_spec = pl.BlockSpec((seq, MIX_W), lambda b: (b, 0))
    kv_spec = pl.BlockSpec((None, NA_HEADS, seq, NA_DH), lambda b: (b, 0, 0, 0))
    return pl.pallas_call(
        _attn_prompt_kernel,
        out_shape=(jax.ShapeDtypeStruct((n_seq * seq, MIX_W), BF16),) * 2
        + (jax.ShapeDtypeStruct((n_seq, NA_HEADS, seq, NA_DH), F32),) * 2,
        grid=(n_seq,),
        in_specs=[_u_spec(seq, cb) for cb in (CB_CVB, CB_CVC, CB_CVH, CB_NAQ, CB_NAK, CB_NAV)]
        + [lw((CONV_K, MIX_W)), lw((1, NA_DH)), lw((1, NA_DH))],
        out_specs=(y_spec, y_spec, kv_spec, kv_spec),
        compiler_params=_params(("parallel",)),
        name="attn_prompt",
    )(u, u, u, u, u, u, conv_w, q_g.reshape(DEPTH, 1, NA_DH), k_g.reshape(DEPTH, 1, NA_DH))


def _na_kernel(cb_ref, cc_ref, ch_ref, q_ref, k_ref, v_ref, kc_ref, vc_ref, tw_ref, cos_ref, sin_ref,
               perm_ref, cw_ref, qg_ref, kg_ref, ya_ref, yb_ref, qs_ref, ks_ref):
    rows = DEC_SEQ // GRID_W
    wr = min(NA_WIN_R, rows)
    nw = wr * GRID_W
    ya_ref[...] = _conv_mix(cb_ref[...], cc_ref[...], ch_ref[...], cw_ref[...]).astype(ya_ref.dtype)
    cos, sin, perm = cos_ref[...], sin_ref[...], perm_ref[...]

    def rope(x):
        return x * cos + _mm(x, perm, HI) * sin

    for h in range(NA_HEADS):
        sl = slice(h * NA_DH, (h + 1) * NA_DH)
        qs_ref[...] = (rope(_head_rms(q_ref[:, sl], qg_ref[...])) * NA_SCALE).astype(BF16)
        ks_ref[...] = rope(_head_rms(k_ref[:, sl], kg_ref[...])).astype(BF16)
        kch = kc_ref[h].astype(BF16)
        vch = vc_ref[h].astype(BF16)

        def row_step(r, carry):
            rs = jnp.clip(r - wr // 2, 0, rows - wr)
            q0 = pl.multiple_of(r * GRID_W, GRID_W)
            k0 = pl.multiple_of(rs * GRID_W, GRID_W)
            q_r = qs_ref[pl.ds(q0, GRID_W), :]
            s_w = _mm_nt(q_r, ks_ref[pl.ds(k0, nw), :]) + tw_ref[h, r - rs]
            s_c = _mm_nt(q_r, kch)
            m = jnp.maximum(jnp.max(s_w, axis=-1, keepdims=True), jnp.max(s_c, axis=-1, keepdims=True))
            p_w = jnp.exp(s_w - m)
            p_c = jnp.exp(s_c - m)
            den = jnp.sum(p_w, axis=-1, keepdims=True) + jnp.sum(p_c, axis=-1, keepdims=True)
            v_w = v_ref[pl.ds(k0, nw), sl].astype(BF16)
            o = (_mm(p_w.astype(BF16), v_w) + _mm(p_c.astype(BF16), vch)) / den
            yb_ref[pl.ds(q0, GRID_W), sl] = o.astype(yb_ref.dtype)
            return carry

        lax.fori_loop(0, rows, row_step, 0)


def _na_tables(rpb):
    rows = DEC_SEQ // GRID_W
    wr = min(NA_WIN_R, rows)
    qc = np.arange(GRID_W)
    kc = np.arange(GRID_W)
    wstart = np.clip(qc - NA_WIN_C // 2, 0, GRID_W - NA_WIN_C)
    colmask = (kc[None, :] >= wstart[:, None]) & (kc[None, :] < wstart[:, None] + NA_WIN_C)
    dc = np.clip(kc[None, :] - qc[:, None], -(NA_WIN_C - 1), NA_WIN_C - 1) + NA_WIN_C - 1
    p = np.arange(wr)
    j = np.arange(wr)
    dr = j[None, :] - p[:, None] + NA_WIN_R - 1
    bias = rpb[:, :, dr][:, :, :, :, dc]
    bias = jnp.where(colmask[None, None, None, None], bias, NEG_INF)
    tw = bias.transpose(0, 1, 2, 4, 3, 5).reshape(DEPTH, NA_HEADS, wr, GRID_W, wr * GRID_W)
    return tw


def _rope_tables():
    t = np.arange(DEC_SEQ)
    quarter = NA_DH // 4
    freq = ROPE_THETA ** (-np.arange(quarter, dtype=np.float32) / quarter)
    ang_r = (t // GRID_W).astype(np.float32)[:, None] * freq
    ang_c = (t % GRID_W).astype(np.float32)[:, None] * freq
    cos = np.concatenate([np.cos(ang_r), np.cos(ang_r), np.cos(ang_c), np.cos(ang_c)], axis=-1)
    sin = np.concatenate([-np.sin(ang_r), np.sin(ang_r), -np.sin(ang_c), np.sin(ang_c)], axis=-1)
    src = np.concatenate([np.arange(quarter) + quarter, np.arange(quarter),
                          np.arange(quarter) + 3 * quarter, np.arange(quarter) + 2 * quarter])
    perm = np.zeros((NA_DH, NA_DH), np.float32)
    perm[src, np.arange(NA_DH)] = 1.0
    return cos.astype(np.float32), sin.astype(np.float32), perm


def _na_sample(u, cache_k, cache_v, tw, conv_w, q_g, k_g, layer, n_seq=DEC_BATCH, row_off=N_PROMPT // DEC_SEQ):
    cos, sin, perm = _rope_tables()
    lw = lambda shape: pl.BlockSpec((None,) + shape, lambda b: (layer,) + (0,) * len(shape))
    y_spec = pl.BlockSpec((DEC_SEQ, MIX_W), lambda b: (b, 0))
    c_spec = pl.BlockSpec((None, None, NA_HEADS, PAST_LEN, NA_DH), lambda b: (b, layer, 0, 0, 0))
    wr = tw.shape[2]
    return pl.pallas_call(
        _na_kernel,
        out_shape=(jax.ShapeDtypeStruct((n_seq * DEC_SEQ, MIX_W), BF16),) * 2,
        grid=(n_seq,),
        in_specs=[_u_spec(DEC_SEQ, cb, row_off) for cb in (CB_CVB, CB_CVC, CB_CVH, CB_NAQ, CB_NAK, CB_NAV)]
        + [c_spec, c_spec, lw((NA_HEADS, wr, GRID_W, wr * GRID_W)),
           _full((DEC_SEQ, NA_DH)), _full((DEC_SEQ, NA_DH)), _full((NA_DH, NA_DH)),
           lw((CONV_K, MIX_W)), lw((1, NA_DH)), lw((1, NA_DH))],
        out_specs=(y_spec, y_spec),
        scratch_shapes=[pltpu.VMEM((DEC_SEQ, NA_DH), BF16), pltpu.VMEM((DEC_SEQ, NA_DH), BF16)],
        compiler_params=_params(("parallel",)),
        name="na_sample",
    )(u, u, u, u, u, u, cache_k, cache_v, tw, jnp.asarray(cos), jnp.asarray(sin), jnp.asarray(perm),
      conv_w, q_g.reshape(DEPTH, 1, NA_DH), k_g.reshape(DEPTH, 1, NA_DH))


def _seg_ones(width, seg):
    a = lax.broadcasted_iota(jnp.int32, (width, width), 0) // seg
    b = lax.broadcasted_iota(jnp.int32, (width, width), 1) // seg
    return (a == b).astype(F32)


def _rwkv_kernel(*refs, seq, has_s0, emit_state):
    it = iter(refs)
    r_ref, k_ref, v_ref, sm_ref = (next(it) for _ in range(4))
    (w0_ref, wup_ref, a0_ref, aup_ref, gup_ref, kkp_ref, kap_ref, rkp_ref, lng_ref, lnb_ref) = (
        next(it) for _ in range(10))
    s0_ref = next(it) if has_s0 else None
    y_ref = next(it)
    so_ref = next(it) if emit_state else None
    rp_ref, vp_ref, kk_ref, lw_ref, ka_ref, kd_ref, ys_ref, st_ref = (next(it) for _ in range(8))

    n_pairs = RW_HEADS // 2
    pw = 2 * RW_DH
    nc = seq // CHUNK
    seg = _seg_ones(MIX_W, RW_DH)

    r = r_ref[...]
    k = k_ref[...]
    sm = sm_ref[...]
    wl = jnp.tanh(sm[:, SM_WL:SM_WL + RW_W_RANK])
    al = sm[:, SM_AL:SM_AL + RW_A_RANK]
    kk = k * kkp_ref[...]
    kk = kk * lax.rsqrt(_mm(kk * kk, seg, HI) + EPS)
    for hp in range(n_pairs):
        psl = slice(hp * pw, (hp + 1) * pw)
        rp_ref[hp] = r[:, psl]
        vp_ref[hp] = v_ref[:, psl]
        kk_ref[hp] = kk[:, psl]
    for z in range(2):
        w_pre = w0_ref[z:z + 1, :] + _mm(wl, wup_ref[z], HI)
        lw = -RW_DECAY_SCALE * _sigmoid(w_pre)
        a = _sigmoid(a0_ref[z:z + 1, :] + _mm(al, aup_ref[z], HI))
        ka = kk * a
        kd = k * (1.0 + (a - 1.0) * kap_ref[...])
        for hp in range(n_pairs):
            psl = slice(hp * pw, (hp + 1) * pw)
            lw_ref[z * n_pairs + hp] = lw[:, psl]
            ka_ref[z * n_pairs + hp] = ka[:, psl]
            kd_ref[z * n_pairs + hp] = kd[:, psl]
    for i in range(2 * RW_HEADS):
        st_ref[i] = s0_ref[i // RW_HEADS, i % RW_HEADS] if has_s0 else jnp.zeros((RW_DH, RW_DH), F32)

    incl, strict, ti, si = _tri_masks(CHUNK)
    diag_blk = (ti // SUB == si // SUB).astype(F32)
    eye = (ti == si).astype(F32)

    def chunk_pair(ci, hp):
        for z in range(2):
            c = ci if z == 0 else nc - 1 - ci
            rows = pl.ds(pl.multiple_of(c * CHUNK, CHUNK), CHUNK)
            lwc = lw_ref[z * n_pairs + hp, rows, :]
            cum = _mm(incl[z], lwc, HI)
            tot = cum[CHUNK - 1:CHUNK] if z == 0 else cum[0:1]
            e_neg = jnp.exp(-cum)
            dec = jnp.exp(tot - cum)
            e_tot = jnp.exp(tot)
            kac = ka_ref[z * n_pairs + hp, rows, :]
            kdc = kd_ref[z * n_pairs + hp, rows, :]
            rt = rp_ref[hp, rows, :] * jnp.exp(cum)
            kt = kk_ref[hp, rows, :] * jnp.exp(cum - lwc)
            at, kdt = kac * e_neg, kdc * e_neg
            ah, kh = kac * dec, kdc * dec
            vc = vp_ref[hp, rows, :]
            for j in range(2):
                sl = slice(j * RW_DH, (j + 1) * RW_DH)
                si_ = z * RW_HEADS + hp * 2 + j
                s0 = st_ref[si_]
                kt_h, rt_h, v_h = kt[:, sl], rt[:, sl], vc[:, sl]
                low = _mm_nt(kt_h, at[:, sl], HI) * strict[z]
                a_kk = _mm_nt(kt_h, kdt[:, sl], HI) * strict[z]
                a_ra = _mm_nt(rt_h, at[:, sl], HI) * incl[z]
                a_rk = _mm_nt(rt_h, kdt[:, sl], HI) * incl[z]
                ld = low * diag_blk
                lo = low - ld
                l2 = _mm(ld, ld, HI)
                l4 = _mm(l2, l2, HI)
                l8 = _mm(l4, l4, HI)
                tinv = eye - ld
                tinv = tinv + _mm(tinv, l2, HI)
                tinv = tinv + _mm(tinv, l4, HI)
                tinv = tinv + _mm(tinv, l8, HI)
                rhs = -(_mm_nt(kt_h, s0, HI) + _mm(a_kk, v_h, HI))
                cv = _mm(tinv, rhs, HI)
                wm = _mm(tinv, lo, HI)
                uu = cv
                for _ in range(CHUNK // SUB - 1):
                    uu = cv - _mm(wm, uu, HI)
                y = _mm_nt(rt_h, s0, HI) + _mm(a_ra, uu, HI) + _mm(a_rk, v_h, HI)
                ys_ref[z * n_pairs + hp, rows, sl] = y
                st_ref[si_] = s0 * e_tot[:, sl] + _mm_tn(uu, ah[:, sl], HI) + _mm_tn(v_h, kh[:, sl], HI)

    def chunk_step(ci, carry):
        def pair_step(hp, carry2):
            chunk_pair(ci, hp)
            return carry2
        return lax.fori_loop(0, n_pairs, pair_step, carry)

    lax.fori_loop(0, nc, chunk_step, 0)

    if emit_state:
        for i in range(2 * RW_HEADS):
            so_ref[i // RW_HEADS, i % RW_HEADS] = st_ref[i]

    g = _mm(_sigmoid(sm[:, SM_GL:SM_GL + RW_G_RANK]), gup_ref[...], HI)
    for h in range(RW_HEADS):
        hp, j = divmod(h, 2)
        sl = slice(j * RW_DH, (j + 1) * RW_DH)
        hsl = slice(h * RW_DH, (h + 1) * RW_DH)
        y = ys_ref[hp, :, sl] + ys_ref[n_pairs + hp, :, sl]
        mu = jnp.mean(y, axis=-1, keepdims=True)
        var = jnp.mean(jnp.square(y - mu), axis=-1, keepdims=True)
        yn = (y - mu) * lax.rsqrt(var + RW_GN_EPS) * lng_ref[:, hsl] + lnb_ref[:, hsl]
        r_h, k_h, v_h = rp_ref[hp, :, sl], k_ref[:, hsl], vp_ref[hp, :, sl]
        bonus = jnp.sum(r_h * k_h * rkp_ref[:, hsl], axis=-1, keepdims=True) * v_h
        y_ref[:, hsl] = ((yn + bonus) * g[:, hsl]).astype(y_ref.dtype)


def _rwkv(u, p, layer, n_seq, seq, row_off, s0=None, emit_state=False):
    lw = lambda shape: pl.BlockSpec((None,) + shape, lambda b: (layer,) + (0,) * len(shape))
    row = lambda a: a.reshape(DEPTH, 1, MIX_W)
    in_specs = [_u_spec(seq, cb, row_off) for cb in (CB_RWR, CB_RWK, CB_RWV, CB_SMALL)] + [
        lw((2, MIX_W)), lw((2, RW_W_RANK, MIX_W)), lw((2, MIX_W)), lw((2, RW_A_RANK, MIX_W)),
        lw((RW_G_RANK, MIX_W))] + [lw((1, MIX_W))] * 5
    args = [u, u, u, u, p['rw_w0'], p['rw_w_up'], p['rw_a0'], p['rw_a_up'], p['rw_g_up'],
            row(p['rw_k_k']), row(p['rw_k_a']), row(p['rw_r_k']), row(p['rw_ln_g']), row(p['rw_ln_b'])]
    st_shape = (2, RW_HEADS, RW_DH, RW_DH)
    if s0 is not None:
        in_specs.append(pl.BlockSpec((None, None) + st_shape, lambda b: (b, layer, 0, 0, 0, 0)))
        args.append(s0)
    out_shape = [jax.ShapeDtypeStruct((n_seq * seq, MIX_W), BF16)]
    out_specs = [pl.BlockSpec((seq, MIX_W), lambda b: (b, 0))]
    if emit_state:
        out_shape.append(jax.ShapeDtypeStruct((n_seq,) + st_shape, F32))
        out_specs.append(pl.BlockSpec((None,) + st_shape, lambda b: (b, 0, 0, 0, 0)))
    n_pairs = RW_HEADS // 2
    pair = lambda n: pltpu.VMEM((n, seq, 2 * RW_DH), F32)
    return pl.pallas_call(
        functools.partial(_rwkv_kernel, seq=seq, has_s0=s0 is not None, emit_state=emit_state),
        out_shape=tuple(out_shape), grid=(n_seq,), in_specs=in_specs, out_specs=tuple(out_specs),
        scratch_shapes=[pair(n_pairs), pair(n_pairs), pair(n_pairs), pair(2 * n_pairs), pair(2 * n_pairs),
                        pair(2 * n_pairs), pair(2 * n_pairs), pltpu.VMEM((2 * RW_HEADS, RW_DH, RW_DH), F32)],
        compiler_params=_params(("parallel",)),
        name=f"rwkv_{seq}",
    )(*args)


def _log_sigmoid(x):
    return jnp.minimum(x, 0.0) - jnp.log(1.0 + jnp.exp(-jnp.abs(x)))


def _mlstm_kernel(*refs, seq, has_s0, emit_state):
    it = iter(refs)
    q_ref, k_ref, v_ref, o_ref, sm_ref, gr_ref, bc_ref, br_ref, ng_ref = (next(it) for _ in range(9))
    c0_ref, n0_ref, m0_ref = ((next(it), next(it), next(it)) if has_s0 else (None, None, None))
    y_ref = next(it)
    co_ref, no_ref, mo_ref = ((next(it), next(it), next(it)) if emit_state else (None, None, None))
    hs_ref, c_ref, n_ref, m_ref = (next(it) for _ in range(4))

    nc = seq // CHUNK
    n_st = 2 * ML_HEADS
    for i in range(n_st):
        z, h = divmod(i, ML_HEADS)
        c_ref[i] = c0_ref[z, h] if has_s0 else jnp.zeros((ML_DH, ML_DH), F32)
        n_ref[i] = n0_ref[z, h] if has_s0 else jnp.zeros((1, ML_DH), F32)
        m_ref[i] = m0_ref[z, h] if has_s0 else jnp.zeros((1, 1), F32)

    incl, _, ti, si = _tri_masks(CHUNK)
    before = ((si <= ti), (si >= ti))

    def chunk_step(ci, carry):
        for z in range(2):
            c = ci if z == 0 else nc - 1 - ci
            rows = pl.ds(pl.multiple_of(c * CHUNK, CHUNK), CHUNK)
            g0 = SM_GATE + z * 2 * ML_HEADS
            gc = sm_ref[rows, g0:g0 + 2 * ML_HEADS] + bc_ref[:, z * 2 * ML_HEADS:(z + 1) * 2 * ML_HEADS]
            gr = gr_ref[z * 2 * ML_HEADS:(z + 1) * 2 * ML_HEADS, pl.ds(c, 1), :]
            i_cols, f_cols = gc[:, :ML_HEADS], gc[:, ML_HEADS:]
            b_cols = _mm(incl[z], _log_sigmoid(f_cols), HI)
            for h in range(ML_HEADS):
                st = z * ML_HEADS + h
                i_col, b_col = i_cols[:, h:h + 1], b_cols[:, h:h + 1]
                i_row = gr[h] + br_ref[z * 2 * ML_HEADS + h]
                f_row = gr[ML_HEADS + h] + br_ref[z * 2 * ML_HEADS + ML_HEADS + h]
                b_row = _mm(_log_sigmoid(f_row), incl[1 - z], HI)
                b_last = b_col[CHUNK - 1:CHUNK] if z == 0 else b_col[0:1]
                m_old = m_ref[st]
                cm, nv = c_ref[st], n_ref[st]
                hsl = slice(h * ML_DH, (h + 1) * ML_DH)
                qc = q_ref[rows, hsl] * (ML_DH ** -0.5)
                kc = k_ref[rows, hsl]
                vc = v_ref[rows, hsl]
                a_t = b_col + m_old
                dmat = jnp.where(before[z], b_col - b_row + i_row, NEG_INF)
                m_t = jnp.maximum(a_t, jnp.max(dmat, axis=-1, keepdims=True))
                dw = jnp.exp(dmat - m_t)
                inter = jnp.exp(a_t - m_t)
                s = _mm_nt(qc, kc, HI) * dw
                num = inter * _mm_nt(qc, cm, HI) + _mm(s, vc, HI)
                den = inter * jnp.sum(qc * nv, axis=-1, keepdims=True) + jnp.sum(s, axis=-1, keepdims=True)
                hh = num / jnp.maximum(jnp.abs(den), jnp.exp(-m_t))
                hs_ref[z, rows, hsl] = hh
                g_col = b_last - b_col + i_col
                a_l = b_last + m_old
                m_new = jnp.maximum(a_l, jnp.max(g_col, axis=0, keepdims=True))
                decay = jnp.exp(a_l - m_new)
                wgt = jnp.exp(g_col - m_new)
                c_ref[st] = decay * cm + _mm_tn(vc * wgt, kc, HI)
                n_ref[st] = decay * nv + jnp.sum(wgt * kc, axis=0, keepdims=True)
                m_ref[st] = m_new
        return carry

    lax.fori_loop(0, nc, chunk_step, 0)

    if emit_state:
        for i in range(n_st):
            z, h = divmod(i, ML_HEADS)
            co_ref[z, h] = c_ref[i]
            no_ref[z, h] = n_ref[i]
            mo_ref[z, h] = m_ref[i]

    for h in range(ML_HEADS):
        hsl = slice(h * ML_DH, (h + 1) * ML_DH)
        hn = _head_rms(hs_ref[0, :, hsl] + hs_ref[1, :, hsl], ng_ref[:, hsl])
        y_ref[:, hsl] = (_sigmoid(o_ref[:, hsl]) * hn).astype(y_ref.dtype)


def _mlstm(u, gate_rows, p, layer, n_seq, seq, row_off, state=None, emit_state=False):
    lw = lambda shape: pl.BlockSpec((None,) + shape, lambda b: (layer,) + (0,) * len(shape))
    nc = seq // CHUNK
    n_gate = 4 * ML_HEADS
    in_specs = [_u_spec(seq, cb, row_off) for cb in (CB_MLQ, CB_MLK, CB_MLV, CB_MLO, CB_SMALL)] + [
        pl.BlockSpec((n_gate, None, nc, CHUNK), lambda b: (0, b, 0, 0)),
        lw((1, n_gate)), lw((n_gate, 1, 1)), lw((1, MIX_W))]
    args = [u, u, u, u, u, gate_rows, p['ml_gate_b'].reshape(DEPTH, 1, n_gate),
            p['ml_gate_b'].reshape(DEPTH, n_gate, 1, 1), p['ml_norm_g'].reshape(DEPTH, 1, MIX_W)]
    c_shape, n_shape, m_shape = (2, ML_HEADS, ML_DH, ML_DH), (2, ML_HEADS, 1, ML_DH), (2, ML_HEADS, 1, 1)
    if state is not None:
        c0, n0, m0 = state
        for a, shp in ((c0, c_shape), (n0, n_shape), (m0, m_shape)):
            in_specs.append(pl.BlockSpec((None, None) + shp, lambda b: (b, layer, 0, 0, 0, 0)))
            args.append(a.reshape(a.shape[:2] + shp))
    out_shape = [jax.ShapeDtypeStruct((n_seq * seq, MIX_W), BF16)]
    out_specs = [pl.BlockSpec((seq, MIX_W), lambda b: (b, 0))]
    if emit_state:
        for shp in (c_shape, n_shape, m_shape):
            out_shape.append(jax.ShapeDtypeStruct((n_seq,) + shp, F32))
            out_specs.append(pl.BlockSpec((None,) + shp, lambda b: (b, 0, 0, 0, 0)))
    n_st = 2 * ML_HEADS
    return pl.pallas_call(
        functools.partial(_mlstm_kernel, seq=seq, has_s0=state is not None, emit_state=emit_state),
        out_shape=tuple(out_shape), grid=(n_seq,), in_specs=in_specs, out_specs=tuple(out_specs),
        scratch_shapes=[pltpu.VMEM((2, seq, MIX_W), F32), pltpu.VMEM((n_st, ML_DH, ML_DH), F32),
                        pltpu.VMEM((n_st, 1, ML_DH), F32), pltpu.VMEM((n_st, 1, 1), F32)],
        compiler_params=_params(("parallel",)),
        name=f"mlstm_{seq}",
    )(*args)


def _outproj_kernel(ya_ref, yb_ref, yc_ref, yd_ref, w_ref, x_ref, g_ref, o_ref):
    acc = _mm(ya_ref[...], w_ref[0:MIX_W, :])
    for i, y_ref in enumerate((yb_ref, yc_ref, yd_ref), start=1):
        acc += _mm(y_ref[...], w_ref[i * MIX_W:(i + 1) * MIX_W, :])
    o_ref[...] = x_ref[...] + g_ref[...] * acc


def _outproj(ys, w_out_b, x, mod, layer):
    tm, tn = 1024, 1024
    y_spec = pl.BlockSpec((tm, MIX_W), lambda i, j: (i, 0))
    return pl.pallas_call(
        _outproj_kernel,
        out_shape=jax.ShapeDtypeStruct((N_TOK, D_MODEL), F32),
        grid=(N_TOK // tm, D_MODEL // tn),
        in_specs=[y_spec] * 4 + [pl.BlockSpec((None, D_MODEL, tn), lambda i, j: (layer, 0, j)),
                                 pl.BlockSpec((tm, tn), lambda i, j: (i, j)),
                                 _mod_spec(layer, 2, tm, tn, with_j=True)],
        out_specs=pl.BlockSpec((tm, tn), lambda i, j: (i, j)),
        compiler_params=_params(("parallel", "parallel")),
        name="outproj",
    )(*ys, w_out_b, x, mod)


def _router_kernel(x_ref, g_ref, sh_ref, sc_ref, rw_ref, rb_ref, xm_ref, gt_ref):
    xm = _modulated(x_ref[...], g_ref[...], sh_ref[...], sc_ref[...])
    xm_ref[...] = xm.astype(BF16)
    logits = _mm_nt(rw_ref[...], xm, HI)
    ex = jnp.exp(logits - jnp.max(logits, axis=0, keepdims=True))
    scores = ex / jnp.sum(ex, axis=0, keepdims=True)
    sel = scores + rb_ref[...]
    per = N_EXPERTS // N_EXPERT_GROUPS
    s = [sel[e:e + 1, :] for e in range(N_EXPERTS)]
    grp_score = []
    for g in range(N_EXPERT_GROUPS):
        a, b, c, d = s[per * g:per * (g + 1)]
        hi1, lo1, hi2, lo2 = jnp.maximum(a, b), jnp.minimum(a, b), jnp.maximum(c, d), jnp.minimum(c, d)
        grp_score.append(jnp.maximum(hi1, hi2) + jnp.maximum(jnp.minimum(hi1, hi2), jnp.maximum(lo1, lo2)))
    best = functools.reduce(jnp.maximum, grp_score)
    in_grp, taken = [], jnp.zeros_like(best)
    for g in range(N_EXPERT_GROUPS):
        hit = jnp.where(grp_score[g] == best, 1.0, 0.0) * (1.0 - taken)
        in_grp.append(hit)
        taken = taken + hit
    picked = []
    for e in range(N_EXPERTS):
        g = e // per
        rank = jnp.zeros_like(best)
        for o in range(per * g, per * (g + 1)):
            if o < e:
                rank += jnp.where(s[o] >= s[e], 1.0, 0.0)
            elif o > e:
                rank += jnp.where(s[o] > s[e], 1.0, 0.0)
        picked.append(in_grp[g] * jnp.where(rank < 2.0, 1.0, 0.0) * scores[e:e + 1, :])
    total = functools.reduce(lambda x, y: x + y, picked)
    for e in range(N_EXPERTS):
        gt_ref[e:e + 1, :] = picked[e] / total


def _router(x, norm_g, mod, router_wt, router_b, layer):
    tm = 512
    return pl.pallas_call(
        _router_kernel,
        out_shape=(jax.ShapeDtypeStruct((N_TOK, D_MODEL), BF16), jax.ShapeDtypeStruct((N_EXPERTS, N_TOK), F32)),
        grid=(N_TOK // tm,),
        in_specs=[pl.BlockSpec((tm, D_MODEL), lambda i: (i, 0)),
                  pl.BlockSpec((None, 1, D_MODEL), lambda i: (layer, 0, 0)),
                  _mod_spec(layer, 3, tm), _mod_spec(layer, 4, tm),
                  _full((N_EXPERTS, D_MODEL)), _full((N_EXPERTS, 1))],
        out_specs=(pl.BlockSpec((tm, D_MODEL), lambda i: (i, 0)), pl.BlockSpec((N_EXPERTS, tm), lambda i: (0, i))),
        compiler_params=_params(("parallel",)),
        name="router",
    )(x, norm_g.reshape(DEPTH, 1, D_MODEL), mod, mod, router_wt, router_b.reshape(N_EXPERTS, 1))


def _experts_kernel(xm_ref, gates_ref, w1_ref, w3_ref, w2_ref, x_ref, g_ref, o_ref, acc_ref):
    e = pl.program_id(1)

    @pl.when(e == 0)
    def _():
        acc_ref[...] = jnp.zeros_like(acc_ref)

    xm = xm_ref[...]
    h1 = _mm(xm, w1_ref[...])
    h3 = _mm(xm, w3_ref[...])
    gates = gates_ref[...]
    lane = lax.broadcasted_iota(jnp.int32, gates.shape, 1)
    gate = jnp.sum(jnp.where(lane == e, gates, 0.0), axis=-1, keepdims=True)
    hh = h1 * _sigmoid(h1) * h3 * gate
    acc_ref[...] += _mm(hh.astype(BF16), w2_ref[...])

    @pl.when(e == N_EXPERTS - 1)
    def _():
        o_ref[...] = x_ref[...] + g_ref[...] * acc_ref[...]


def _experts(xm, gates, w1_b, w3_b, w2_b, x, mod, layer):
    tm = 512
    row = pl.BlockSpec((tm, D_MODEL), lambda i, e: (i, 0))
    return pl.pallas_call(
        _experts_kernel,
        out_shape=jax.ShapeDtypeStruct((N_TOK, D_MODEL), F32),
        grid=(N_TOK // tm, N_EXPERTS),
        in_specs=[row, pl.BlockSpec((tm, N_EXPERTS), lambda i, e: (i, 0)),
                  pl.BlockSpec((None, None, D_MODEL, D_EXPERT), lambda i, e: (layer, e, 0, 0)),
                  pl.BlockSpec((None, None, D_MODEL, D_EXPERT), lambda i, e: (layer, e, 0, 0)),
                  pl.BlockSpec((None, None, D_EXPERT, D_MODEL), lambda i, e: (layer, e, 0, 0)),
                  row, _mod_spec(layer, 5, tm)],
        out_specs=row,
        scratch_shapes=[pltpu.VMEM((tm, D_MODEL), F32)],
        compiler_params=_params(("parallel", "arbitrary")),
        name="experts",
    )(xm, gates, w1_b, w3_b, w2_b, x, mod)


def _gate_rows(gcols, n_seq, seq):
    return gcols.reshape(n_seq, seq // CHUNK, CHUNK, gcols.shape[-1]).transpose(3, 0, 1, 2)


def kernel(x_prompt, x_sample, cache_na_k, cache_na_v, state_rwkv, state_mlstm_c, state_mlstm_n, state_mlstm_m,
           c, c_ctx, norm1_g, norm2_g, w_mod, b_mod, w_in, conv_w, na_q_g, na_k_g, na_rpb, rw_w0, rw_w_up, rw_a0,
           rw_a_up, rw_g_up, rw_k_k, rw_k_a, rw_r_k, rw_ln_g, rw_ln_b, ml_gate_b, ml_norm_g, w_out, router_w,
           router_b, moe_w1, moe_w3, moe_w2):
    p = dict(rw_w0=rw_w0, rw_w_up=rw_w_up, rw_a0=rw_a0, rw_a_up=rw_a_up, rw_g_up=rw_g_up, rw_k_k=rw_k_k,
             rw_k_a=rw_k_a, rw_r_k=rw_r_k, rw_ln_g=rw_ln_g, rw_ln_b=rw_ln_b, ml_gate_b=ml_gate_b,
             ml_norm_g=ml_norm_g)
    cvecs = jnp.concatenate([c_ctx[None], c, jnp.zeros((MOD_ROWS - 1 - DEC_BATCH, D_MODEL), F32)], axis=0)
    mod = _adaln(cvecs, w_mod, b_mod).reshape(DEPTH, MOD_ROWS, 6, 1, D_MODEL)

    n_wide_a = 9 * MIX_W
    n_narrow = RW_W_RANK + RW_A_RANK + RW_G_RANK
    n_wide_b = 4 * MIX_W
    w_in_b = jnp.concatenate(
        [w_in[..., :n_wide_a], w_in[..., n_wide_a + n_narrow:n_wide_a + n_narrow + n_wide_b],
         w_in[..., n_wide_a:n_wide_a + n_narrow], w_in[..., n_wide_a + n_narrow + n_wide_b:],
         jnp.zeros((DEPTH, D_MODEL, P_PAD - w_in.shape[-1]), F32)], axis=-1).astype(BF16)
    w_out_b = w_out.astype(BF16)
    w1_b, w3_b, w2_b = moe_w1.astype(BF16), moe_w3.astype(BF16), moe_w2.astype(BF16)
    tw = _na_tables(na_rpb)
    router_wt = router_w.T
    sample_row_off = N_PROMPT // DEC_SEQ

    x = jnp.concatenate([x_prompt.reshape(N_PROMPT, D_MODEL), x_sample.reshape(N_SAMPLE, D_MODEL)], axis=0)
    new_k, new_v, new_rw, new_c, new_n, new_m = [], [], [], [], [], []
    for l in range(DEPTH):
        u = _inproj(x, norm1_g, mod, w_in_b, l)
        ya_p, yb_p, nk, nv = _attn_prompt(u, conv_w, na_q_g, na_k_g, l)
        ya_s, yb_s = _na_sample(u, cache_na_k, cache_na_v, tw, conv_w, na_q_g, na_k_g, l)
        yc_p, st = _rwkv(u, p, l, BATCH, SEQ, 0, emit_state=True)
        (yc_s,) = _rwkv(u, p, l, DEC_BATCH, DEC_SEQ, sample_row_off, s0=state_rwkv)
        g0 = CB_SMALL * MIX_W + SM_GATE
        gcols = u[:, g0:g0 + 4 * ML_HEADS]
        yd_p, cm, nm, mm = _mlstm(u, _gate_rows(gcols[:N_PROMPT], BATCH, SEQ), p, l, BATCH, SEQ, 0,
                                  emit_state=True)
        (yd_s,) = _mlstm(u, _gate_rows(gcols[N_PROMPT:], DEC_BATCH, DEC_SEQ), p, l, DEC_BATCH, DEC_SEQ,
                         sample_row_off, state=(state_mlstm_c, state_mlstm_n, state_mlstm_m))
        ys = [jnp.concatenate(pair, axis=0) for pair in ((ya_p, ya_s), (yb_p, yb_s), (yc_p, yc_s), (yd_p, yd_s))]
        x = _outproj(ys, w_out_b, x, mod, l)
        xm, gates_t = _router(x, norm2_g, mod, router_wt, router_b, l)
        x = _experts(xm, gates_t.T, w1_b, w3_b, w2_b, x, mod, l)
        new_k.append(nk)
        new_v.append(nv)
        new_rw.append(st)
        new_c.append(cm)
        new_n.append(nm.reshape(BATCH, 2, ML_HEADS, ML_DH))
        new_m.append(mm.reshape(BATCH, 2, ML_HEADS))
    stack = lambda xs: jnp.stack(xs, axis=1)
    return (x[:N_PROMPT].reshape(BATCH, SEQ, D_MODEL), x[N_PROMPT:].reshape(DEC_BATCH, DEC_SEQ, D_MODEL),
            stack(new_k), stack(new_v), stack(new_rw), stack(new_c), stack(new_n), stack(new_m))
```

```python
import functools

import numpy as np
import jax
import jax.numpy as jnp
from jax import lax
from jax.experimental import pallas as pl
from jax.experimental.pallas import tpu as pltpu

F32 = jnp.float32
BF16 = jnp.bfloat16
HI = lax.Precision.HIGHEST

D_MODEL = 2048
BATCH = 16
SEQ = 256
DEPTH = 2
DEC_BATCH = 2
DEC_SEQ = 1024
PAST_LEN = 512
GRID_W = 64
MIX_W = D_MODEL // 4
CONV_K = 3
NA_DH = 64
NA_HEADS = MIX_W // NA_DH
NA_WIN_R = 8
NA_WIN_C = 16
NA_SCALE = NA_DH ** -0.5
ROPE_THETA = 10000.0
RW_DH = 64
RW_HEADS = MIX_W // RW_DH
RW_W_RANK = 64
RW_A_RANK = 64
RW_G_RANK = 128
RW_DECAY_SCALE = 0.606531
RW_GN_EPS = 64e-5
ML_DH = 128
ML_HEADS = MIX_W // ML_DH
N_EXPERTS = 16
N_EXPERT_GROUPS = 4
D_EXPERT = 512
EPS = 1e-6
NEG_INF = -1e30

N_PROMPT = BATCH * SEQ
N_SAMPLE = DEC_BATCH * DEC_SEQ
N_TOK = N_PROMPT + N_SAMPLE
MOD_ROWS = 8
CHUNK = 64
SUB = 16
P1_CHUNKS = 2
P_BLOCKS = 14
P_PAD = P_BLOCKS * MIX_W
(CB_CVB, CB_CVC, CB_CVH, CB_NAQ, CB_NAK, CB_NAV, CB_RWR, CB_RWK, CB_RWV,
 CB_MLQ, CB_MLK, CB_MLV, CB_MLO, CB_SMALL) = range(P_BLOCKS)
SM_WL, SM_AL, SM_GL, SM_GATE = 0, 64, 128, 256
VMEM_LIMIT = 56 * 1024 * 1024


def _mm(a, b, prec=None):
    return jnp.dot(a, b, precision=prec, preferred_element_type=F32)


def _mm_nt(a, b, prec=None):
    return lax.dot_general(a, b, (((1,), (1,)), ((), ())), precision=prec, preferred_element_type=F32)


def _mm_tn(a, b, prec=None):
    return _mm(a.T, b, prec)


def _sigmoid(x):
    return 1.0 / (1.0 + jnp.exp(-x))


def _full(shape):
    n = len(shape)
    return pl.BlockSpec(shape, lambda *_: (0,) * n)


def _params(sem):
    return pltpu.CompilerParams(dimension_semantics=sem, vmem_limit_bytes=VMEM_LIMIT)


def _mod_row(i, tm):
    n_prompt_tiles = N_PROMPT // tm
    tiles_per_sample = DEC_SEQ // tm
    return jnp.where(i < n_prompt_tiles, 0, 1 + (i - n_prompt_tiles) // tiles_per_sample)


def _mod_spec(layer, chunk, tm, tn=D_MODEL, with_j=False):
    if with_j:
        return pl.BlockSpec((None, None, None, 1, tn), lambda i, j: (layer, _mod_row(i, tm), chunk, 0, j))
    return pl.BlockSpec((None, None, None, 1, tn), lambda i, *_: (layer, _mod_row(i, tm), chunk, 0, 0))


def _tri_masks(n):
    t = lax.broadcasted_iota(jnp.int32, (n, n), 0)
    s = lax.broadcasted_iota(jnp.int32, (n, n), 1)
    incl = ((s <= t).astype(F32), (s >= t).astype(F32))
    strict = ((s < t).astype(F32), (s > t).astype(F32))
    return incl, strict, t, s


def _adaln_kernel(cv_ref, w_ref, b_ref, o_ref):
    cv = cv_ref[...]
    o_ref[...] = _mm(cv * _sigmoid(cv), w_ref[...], HI) + b_ref[...]


def _adaln(cvecs, w_mod, b_mod):
    tn = 1024
    n_out = 6 * D_MODEL
    return pl.pallas_call(
        _adaln_kernel,
        out_shape=jax.ShapeDtypeStruct((DEPTH, MOD_ROWS, n_out), F32),
        grid=(DEPTH, n_out // tn),
        in_specs=[_full((MOD_ROWS, D_MODEL)),
                  pl.BlockSpec((None, D_MODEL, tn), lambda l, j: (l, 0, j)),
                  pl.BlockSpec((None, 1, tn), lambda l, j: (l, 0, j))],
        out_specs=pl.BlockSpec((None, MOD_ROWS, tn), lambda l, j: (l, 0, j)),
        compiler_params=_params(("parallel", "parallel")),
        name="adaln",
    )(cvecs, w_mod, b_mod.reshape(DEPTH, 1, n_out))


def _modulated(x, g, sh, sc):
    y = x * lax.rsqrt(jnp.mean(x * x, axis=-1, keepdims=True) + EPS) * g
    return y * (1.0 + sc) + sh


def _inproj_kernel(x_ref, g_ref, sh_ref, sc_ref, w_ref, o_ref, xm_ref):
    @pl.when(pl.program_id(1) == 0)
    def _():
        xm_ref[...] = _modulated(x_ref[...], g_ref[...], sh_ref[...], sc_ref[...]).astype(BF16)

    o_ref[...] = _mm(xm_ref[...], w_ref[...])


def _inproj(x, norm_g, mod, w_in_b, layer):
    tm, tn = 512, 1024
    return pl.pallas_call(
        _inproj_kernel,
        out_shape=jax.ShapeDtypeStruct((N_TOK, P_PAD), F32),
        grid=(N_TOK // tm, P_PAD // tn),
        in_specs=[pl.BlockSpec((tm, D_MODEL), lambda i, j: (i, 0)),
                  pl.BlockSpec((None, 1, D_MODEL), lambda i, j: (layer, 0, 0)),
                  _mod_spec(layer, 0, tm), _mod_spec(layer, 1, tm),
                  pl.BlockSpec((None, D_MODEL, tn), lambda i, j: (layer, 0, j))],
        out_specs=pl.BlockSpec((tm, tn), lambda i, j: (i, j)),
        scratch_shapes=[pltpu.VMEM((tm, D_MODEL), BF16)],
        compiler_params=_params(("parallel", "arbitrary")),
        name="inproj",
    )(x, norm_g.reshape(DEPTH, 1, D_MODEL), mod, mod, w_in_b)


def _conv_mix(b, c, h, w):
    u = c * h
    n = u.shape[0]
    row = lax.broadcasted_iota(jnp.int32, u.shape, 0)
    prev = jnp.where(row == 0, 0.0, pltpu.roll(u, 1, axis=0))
    nxt = jnp.where(row == n - 1, 0.0, pltpu.roll(u, n - 1, axis=0))
    return b * (prev * w[0:1] + u * w[1:2] + nxt * w[2:3])


def _head_rms(x, g):
    return x * lax.rsqrt(jnp.mean(x * x, axis=-1, keepdims=True) + EPS) * g


def _attn_prompt_kernel(cb_ref, cc_ref, ch_ref, q_ref, k_ref, v_ref, cw_ref, qg_ref, kg_ref,
                        ya_ref, yb_ref, nk_ref, nv_ref):
    ya_ref[...] = _conv_mix(cb_ref[...], cc_ref[...], ch_ref[...], cw_ref[...]).astype(ya_ref.dtype)
    for h in range(NA_HEADS):
        sl = slice(h * NA_DH, (h + 1) * NA_DH)
        qn = _head_rms(q_ref[:, sl], qg_ref[...]) * NA_SCALE
        kn = _head_rms(k_ref[:, sl], kg_ref[...])
        vh = v_ref[:, sl]
        s = _mm_nt(qn.astype(BF16), kn.astype(BF16))
        p = jnp.exp(s - jnp.max(s, axis=-1, keepdims=True))
        o = _mm(p.astype(BF16), vh.astype(BF16)) / jnp.sum(p, axis=-1, keepdims=True)
        yb_ref[:, sl] = o.astype(yb_ref.dtype)
        nk_ref[h] = kn
        nv_ref[h] = vh


def _u_spec(rows, col_block, row_off_blocks=0):
    return pl.BlockSpec((rows, MIX_W), lambda b: (b + row_off_blocks, col_block))


def _attn_prompt(u, conv_w, q_g, k_g, layer, n_seq=BATCH, seq=SEQ):
    lw = lambda shape: pl.BlockSpec((None,) + shape, lambda b: (layer,) + (0,) * len(shape))
    y_spec = pl.BlockSpec((seq, MIX_W), lambda b: (b, 0))
    kv_spec = pl.BlockSpec((None, NA_HEADS, seq, NA_DH), lambda b: (b, 0, 0, 0))
    return pl.pallas_call(
        _attn_prompt_kernel,
        out_shape=(jax.ShapeDtypeStruct((n_seq * seq, MIX_W), BF16),) * 2
        + (jax.ShapeDtypeStruct((n_seq, NA_HEADS, seq, NA_DH), F32),) * 2,
        grid=(n_seq,),
        in_specs=[_u_spec(seq, cb) for cb in (CB_CVB, CB_CVC, CB_CVH, CB_NAQ, CB_NAK, CB_NAV)]
        + [lw((CONV_K, MIX_W)), lw((1, NA_DH)), lw((1, NA_DH))],
        out_specs=(y_spec, y_spec, kv_spec, kv_spec),
        compiler_params=_params(("parallel",)),
        name="attn_prompt",
    )(u, u, u, u, u, u, conv_w, q_g.reshape(DEPTH, 1, NA_DH), k_g.reshape(DEPTH, 1, NA_DH))


def _na_kernel(cb_ref, cc_ref, ch_ref, q_ref, k_ref, v_ref, kc_ref, vc_ref, tw_ref, cos_ref, sin_ref,
               perm_ref, cw_ref, qg_ref, kg_ref, ya_ref, yb_ref, qs_ref, ks_ref):
    rows = DEC_SEQ // GRID_W
    wr = min(NA_WIN_R, rows)
    nw = wr * GRID_W
    ya_ref[...] = _conv_mix(cb_ref[...], cc_ref[...], ch_ref[...], cw_ref[...]).astype(ya_ref.dtype)
    cos, sin, perm = cos_ref[...], sin_ref[...], perm_ref[...]

    def rope(x):
        return x * cos + _mm(x, perm, HI) * sin

    for h in range(NA_HEADS):
        sl = slice(h * NA_DH, (h + 1) * NA_DH)
        qs_ref[...] = (rope(_head_rms(q_ref[:, sl], qg_ref[...])) * NA_SCALE).astype(BF16)
        ks_ref[...] = rope(_head_rms(k_ref[:, sl], kg_ref[...])).astype(BF16)
        kch = kc_ref[h].astype(BF16)
        vch = vc_ref[h].astype(BF16)

        def row_step(r, carry):
            rs = jnp.clip(r - wr // 2, 0, rows - wr)
            q0 = pl.multiple_of(r * GRID_W, GRID_W)
            k0 = pl.multiple_of(rs * GRID_W, GRID_W)
            q_r = qs_ref[pl.ds(q0, GRID_W), :]
            s_w = _mm_nt(q_r, ks_ref[pl.ds(k0, nw), :]) + tw_ref[h, r - rs]
            s_c = _mm_nt(q_r, kch)
            m = jnp.maximum(jnp.max(s_w, axis=-1, keepdims=True), jnp.max(s_c, axis=-1, keepdims=True))
            p_w = jnp.exp(s_w - m)
            p_c = jnp.exp(s_c - m)
            den = jnp.sum(p_w, axis=-1, keepdims=True) + jnp.sum(p_c, axis=-1, keepdims=True)
            v_w = v_ref[pl.ds(k0, nw), sl].astype(BF16)
            o = (_mm(p_w.astype(BF16), v_w) + _mm(p_c.astype(BF16), vch)) / den
            yb_ref[pl.ds(q0, GRID_W), sl] = o.astype(yb_ref.dtype)
            return carry

        lax.fori_loop(0, rows, row_step, 0)


def _na_tables(rpb):
    rows = DEC_SEQ // GRID_W
    wr = min(NA_WIN_R, rows)
    qc = np.arange(GRID_W)
    kc = np.arange(GRID_W)
    wstart = np.clip(qc - NA_WIN_C // 2, 0, GRID_W - NA_WIN_C)
    colmask = (kc[None, :] >= wstart[:, None]) & (kc[None, :] < wstart[:, None] + NA_WIN_C)
    dc = np.clip(kc[None, :] - qc[:, None], -(NA_WIN_C - 1), NA_WIN_C - 1) + NA_WIN_C - 1
    p = np.arange(wr)
    j = np.arange(wr)
    dr = j[None, :] - p[:, None] + NA_WIN_R - 1
    bias = rpb[:, :, dr][:, :, :, :, dc]
    bias = jnp.where(colmask[None, None, None, None], bias, NEG_INF)
    tw = bias.transpose(0, 1, 2, 4, 3, 5).reshape(DEPTH, NA_HEADS, wr, GRID_W, wr * GRID_W)
    return tw


def _rope_tables():
    t = np.arange(DEC_SEQ)
    quarter = NA_DH // 4
    freq = ROPE_THETA ** (-np.arange(quarter, dtype=np.float32) / quarter)
    ang_r = (t // GRID_W).astype(np.float32)[:, None] * freq
    ang_c = (t % GRID_W).astype(np.float32)[:, None] * freq
    cos = np.concatenate([np.cos(ang_r), np.cos(ang_r), np.cos(ang_c), np.cos(ang_c)], axis=-1)
    sin = np.concatenate([-np.sin(ang_r), np.sin(ang_r), -np.sin(ang_c), np.sin(ang_c)], axis=-1)
    src = np.concatenate([np.arange(quarter) + quarter, np.arange(quarter),
                          np.arange(quarter) + 3 * quarter, np.arange(quarter) + 2 * quarter])
    perm = np.zeros((NA_DH, NA_DH), np.float32)
    perm[src, np.arange(NA_DH)] = 1.0
    return cos.astype(np.float32), sin.astype(np.float32), perm


def _na_sample(u, cache_k, cache_v, tw, conv_w, q_g, k_g, layer, n_seq=DEC_BATCH, row_off=N_PROMPT // DEC_SEQ):
    cos, sin, perm = _rope_tables()
    lw = lambda shape: pl.BlockSpec((None,) + shape, lambda b: (layer,) + (0,) * len(shape))
    y_spec = pl.BlockSpec((DEC_SEQ, MIX_W), lambda b: (b, 0))
    c_spec = pl.BlockSpec((None, None, NA_HEADS, PAST_LEN, NA_DH), lambda b: (b, layer, 0, 0, 0))
    wr = tw.shape[2]
    return pl.pallas_call(
        _na_kernel,
        out_shape=(jax.ShapeDtypeStruct((n_seq * DEC_SEQ, MIX_W), BF16),) * 2,
        grid=(n_seq,),
        in_specs=[_u_spec(DEC_SEQ, cb, row_off) for cb in (CB_CVB, CB_CVC, CB_CVH, CB_NAQ, CB_NAK, CB_NAV)]
        + [c_spec, c_spec, lw((NA_HEADS, wr, GRID_W, wr * GRID_W)),
           _full((DEC_SEQ, NA_DH)), _full((DEC_SEQ, NA_DH)), _full((NA_DH, NA_DH)),
           lw((CONV_K, MIX_W)), lw((1, NA_DH)), lw((1, NA_DH))],
        out_specs=(y_spec, y_spec),
        scratch_shapes=[pltpu.VMEM((DEC_SEQ, NA_DH), BF16), pltpu.VMEM((DEC_SEQ, NA_DH), BF16)],
        compiler_params=_params(("parallel",)),
        name="na_sample",
    )(u, u, u, u, u, u, cache_k, cache_v, tw, jnp.asarray(cos), jnp.asarray(sin), jnp.asarray(perm),
      conv_w, q_g.reshape(DEPTH, 1, NA_DH), k_g.reshape(DEPTH, 1, NA_DH))


def _seg_ones(width, seg):
    a = lax.broadcasted_iota(jnp.int32, (width, width), 0) // seg
    b = lax.broadcasted_iota(jnp.int32, (width, width), 1) // seg
    return (a == b).astype(F32)


def _split2(x):
    hi = x.astype(BF16)
    return hi, (x - hi.astype(F32)).astype(BF16)


def _mm3(a, b, nt=False):
    dot = _mm_nt if nt else _mm
    a_hi, a_lo = _split2(a)
    b_hi, b_lo = _split2(b)
    return dot(a_hi, b_lo) + dot(a_lo, b_hi) + dot(a_hi, b_hi)


def _split3(x):
    x0 = x.astype(BF16)
    r1 = x - x0.astype(F32)
    x1 = r1.astype(BF16)
    return x0, x1, (r1 - x1.astype(F32)).astype(BF16)


def _mm_exact_lhs(mask_b, x):
    x0, x1, x2 = _split3(x)
    return _mm(mask_b, x2) + _mm(mask_b, x1) + _mm(mask_b, x0)


def _mm_exact_rhs(x, mask_b):
    x0, x1, x2 = _split3(x)
    return _mm(x2, mask_b) + _mm(x1, mask_b) + _mm(x0, mask_b)


def _rwkv_kernel(*refs, seq, has_s0, emit_state):
    it = iter(refs)
    r_ref, k_ref, v_ref, sm_ref = (next(it) for _ in range(4))
    (w0_ref, wup_ref, a0_ref, aup_ref, gup_ref, kkp_ref, kap_ref, rkp_ref, lng_ref, lnb_ref) = (
        next(it) for _ in range(10))
    s0_ref = next(it) if has_s0 else None
    y_ref = next(it)
    so_ref = next(it) if emit_state else None
    kk_ref, lw_ref, ka_ref, kd_ref, coef_ref, ysp_ref = (next(it) for _ in range(6))

    dh = RW_DH
    pw = 2 * dh
    nc = seq // CHUNK
    seg_b = _seg_ones(pw, dh).astype(BF16)

    r = r_ref[...]
    k = k_ref[...]
    sm = sm_ref[...]
    wl = jnp.tanh(sm[:, SM_WL:SM_WL + RW_W_RANK])
    al = sm[:, SM_AL:SM_AL + RW_A_RANK]
    kk = k * kkp_ref[...]
    kk = kk * lax.rsqrt(_mm_exact_rhs(kk * kk, seg_b) + EPS)
    kk_ref[...] = kk
    for z in range(2):
        lw_ref[z] = -RW_DECAY_SCALE * _sigmoid(w0_ref[z:z + 1, :] + _mm3(wl, wup_ref[z]))
        a = _sigmoid(a0_ref[z:z + 1, :] + _mm3(al, aup_ref[z]))
        ka_ref[z] = kk * a
        kd_ref[z] = k * (1.0 + (a - 1.0) * kap_ref[...])

    incl, strict, ti, si = _tri_masks(CHUNK)
    incl_b = tuple(m.astype(BF16) for m in incl)
    incl2 = tuple(jnp.concatenate([m, m], axis=1) for m in incl)
    diag_blk = (ti // SUB == si // SUB).astype(F32)
    eye = (ti == si).astype(F32)
    zero_blk = jnp.zeros((CHUNK, dh), F32)

    def phase1(c2, carry):
        inst = []
        for cc in range(P1_CHUNKS):
            rows = pl.ds(pl.multiple_of((c2 * P1_CHUNKS + cc) * CHUNK, CHUNK), CHUNK)
            vc, rc, kkc = v_ref[rows, :], r_ref[rows, :], kk_ref[rows, :]
            for z in range(2):
                lwc = lw_ref[z, rows, :]
                cum = _mm_exact_lhs(incl_b[z], lwc)
                tot = cum[CHUNK - 1:CHUNK] if z == 0 else cum[0:1]
                e_neg = jnp.exp(-cum)
                dec = jnp.exp(tot - cum)
                e_tot = jnp.exp(tot)
                kac, kdc = ka_ref[z, rows, :], kd_ref[z, rows, :]
                rt = rc * jnp.exp(cum)
                kt = kkc * jnp.exp(cum - lwc)
                at, kdt, ah, kh = kac * e_neg, kdc * e_neg, kac * dec, kdc * dec
                for j in range(2):
                    sl = slice(j * dh, (j + 1) * dh)
                    inst.append((z, kt[:, sl], rt[:, sl], vc[:, sl], at[:, sl], kdt[:, sl], ah[:, sl], kh[:, sl],
                                 e_tot[:, sl]))
        zs = [i[0] for i in inst]
        kt_h, rt_h, v_h = [i[1] for i in inst], [i[2] for i in inst], [i[3] for i in inst]
        each = lambda f, *xs: [f(*a) for a in zip(*xs)]
        aa = each(lambda i: _mm3(jnp.concatenate([i[1], i[2]], axis=0),
                                 jnp.concatenate([i[4], i[5]], axis=0), nt=True), inst)
        low = each(lambda a, z: a[0:CHUNK, 0:CHUNK] * strict[z], aa, zs)
        a_kk = each(lambda a, z: a[0:CHUNK, CHUNK:] * strict[z], aa, zs)
        a_r = each(lambda a, z: a[CHUNK:, :] * incl2[z], aa, zs)
        akv = each(_mm3, a_kk, v_h)
        ld = each(lambda x: x * diag_blk, low)
        lo = each(lambda x, y: x - y, low, ld)
        l2 = each(_mm3, ld, ld)
        l4 = each(_mm3, l2, l2)
        l8 = each(_mm3, l4, l4)
        td = each(lambda x: eye - x, ld)
        for lp in (l2, l4, l8):
            td = each(lambda t, p: t + _mm3(t, p), td, lp)
        x0 = each(lambda t, a, b, c_: _mm3(t, jnp.concatenate([a, b, c_], axis=1)), td, lo, kt_h, akv)
        wm = each(lambda x: x[:, 0:CHUNK], x0)
        pq0 = each(lambda x: x[:, CHUNK:], x0)
        pq = pq0
        for _ in range(CHUNK // SUB - 1):
            pq = each(lambda p0, w, p: p0 - _mm3(w, p), pq0, wm, pq)
        ryc = each(lambda a, p, v_: _mm3(a, jnp.concatenate(
            [p, jnp.concatenate([zero_blk, -v_], axis=1)], axis=0)), a_r, pq, v_h)
        gh = each(lambda i, p: _mm3(jnp.concatenate([i[6], i[7]], axis=1).T,
                                    jnp.concatenate([p, i[3]], axis=1)), inst, pq)
        pieces = []
        for i, g, ry, rt_ in zip(inst, gh, ryc, rt_h):
            g_t = eye * i[8] - g[0:dh, 0:dh]
            h_t = g[dh:, 2 * dh:] - g[0:dh, dh:2 * dh]
            pieces += [g_t, rt_ - ry[:, 0:dh], h_t, -ry[:, dh:]]
        per_chunk = len(pieces) // P1_CHUNKS
        for cc in range(P1_CHUNKS):
            coef_ref[c2 * P1_CHUNKS + cc] = jnp.concatenate(pieces[cc * per_chunk:(cc + 1) * per_chunk], axis=0)
        return carry

    lax.fori_loop(0, nc // P1_CHUNKS, phase1, 0)

    m_init = tuple((s0_ref[z, j].T if has_s0 else jnp.zeros((dh, dh), F32)) for z in range(2) for j in range(2))

    def phase2(ci, ms):
        new_ms, ys = [], []
        for z in range(2):
            c = ci if z == 0 else nc - 1 - ci
            for j in range(2):
                base = (z * 2 + j) * 4 * dh
                out = (_mm3(coef_ref[c, base:base + 2 * dh, :], ms[z * 2 + j])
                       + coef_ref[c, base + 2 * dh:base + 4 * dh, :])
                new_ms.append(out[0:dh])
                ys.append(out[dh:])
        ysp_ref[ci] = jnp.concatenate(ys, axis=1)
        return tuple(new_ms)

    m_fin = lax.fori_loop(0, nc, phase2, m_init)

    if emit_state:
        for z in range(2):
            for j in range(2):
                so_ref[z, j] = m_fin[z * 2 + j].T

    y = jnp.concatenate([ysp_ref[c, :, 0:pw] + ysp_ref[nc - 1 - c, :, pw:] for c in range(nc)], axis=0)
    mu = _mm_exact_rhs(y, seg_b) * (1.0 / dh)
    yc = y - mu
    var = _mm_exact_rhs(yc * yc, seg_b) * (1.0 / dh)
    yn = yc * lax.rsqrt(var + RW_GN_EPS) * lng_ref[...] + lnb_ref[...]
    v = v_ref[...]
    bonus = _mm_exact_rhs(r * k * rkp_ref[...], seg_b) * v
    g = _mm3(_sigmoid(sm[:, SM_GL:SM_GL + RW_G_RANK]), gup_ref[...])
    y_ref[...] = ((yn + bonus) * g).astype(y_ref.dtype)


def _rwkv(u, p, layer, n_seq, seq, row_off, s0=None, emit_state=False):
    n_pairs = RW_HEADS // 2
    pw = 2 * RW_DH
    bpc = MIX_W // pw
    lw = lambda shape: pl.BlockSpec((None,) + shape, lambda b, hp: (layer,) + (0,) * (len(shape) - 1) + (hp,))
    row = lambda a: a.reshape(DEPTH, 1, MIX_W)
    u_pair = lambda cb: pl.BlockSpec((seq, pw), lambda b, hp: (b + row_off, cb * bpc + hp))
    in_specs = [u_pair(CB_RWR), u_pair(CB_RWK), u_pair(CB_RWV),
                pl.BlockSpec((seq, MIX_W), lambda b, hp: (b + row_off, CB_SMALL)),
                lw((2, pw)), lw((2, RW_W_RANK, pw)), lw((2, pw)), lw((2, RW_A_RANK, pw)),
                lw((RW_G_RANK, pw))] + [lw((1, pw))] * 5
    args = [u, u, u, u, p['rw_w0'], p['rw_w_up'], p['rw_a0'], p['rw_a_up'], p['rw_g_up'],
            row(p['rw_k_k']), row(p['rw_k_a']), row(p['rw_r_k']), row(p['rw_ln_g']), row(p['rw_ln_b'])]
    st_blk = (2, 2, RW_DH, RW_DH)
    if s0 is not None:
        in_specs.append(pl.BlockSpec((None, None) + st_blk, lambda b, hp: (b, layer, 0, hp, 0, 0)))
        args.append(s0)
    out_shape = [jax.ShapeDtypeStruct((n_seq * seq, MIX_W), BF16)]
    out_specs = [pl.BlockSpec((seq, pw), lambda b, hp: (b, hp))]
    if emit_state:
        out_shape.append(jax.ShapeDtypeStruct((n_seq, 2, RW_HEADS, RW_DH, RW_DH), F32))
        out_specs.append(pl.BlockSpec((None,) + st_blk, lambda b, hp: (b, 0, hp, 0, 0)))
    nc = seq // CHUNK
    tok = lambda n: pltpu.VMEM((n, seq, pw) if n else (seq, pw), F32)
    return pl.pallas_call(
        functools.partial(_rwkv_kernel, seq=seq, has_s0=s0 is not None, emit_state=emit_state),
        out_shape=tuple(out_shape), grid=(n_seq, n_pairs), in_specs=in_specs, out_specs=tuple(out_specs),
        scratch_shapes=[tok(0), tok(2), tok(2), tok(2),
                        pltpu.VMEM((nc, 16 * RW_DH, RW_DH), F32), pltpu.VMEM((nc, CHUNK, 2 * pw), F32)],
        compiler_params=_params(("parallel", "parallel")),
        name=f"rwkv_{seq}",
    )(*args)


def _rwkv_kernel_old(*refs, seq, has_s0, emit_state):
    it = iter(refs)
    r_ref, k_ref, v_ref, sm_ref = (next(it) for _ in range(4))
    (w0_ref, wup_ref, a0_ref, aup_ref, gup_ref, kkp_ref, kap_ref, rkp_ref, lng_ref, lnb_ref) = (
        next(it) for _ in range(10))
    s0_ref = next(it) if has_s0 else None
    y_ref = next(it)
    so_ref = next(it) if emit_state else None
    rp_ref, vp_ref, kk_ref, lw_ref, ka_ref, kd_ref, ys_ref, st_ref = (next(it) for _ in range(8))

    n_pairs = RW_HEADS // 2
    pw = 2 * RW_DH
    nc = seq // CHUNK
    seg = _seg_ones(MIX_W, RW_DH)

    r = r_ref[...]
    k = k_ref[...]
    sm = sm_ref[...]
    wl = jnp.tanh(sm[:, SM_WL:SM_WL + RW_W_RANK])
    al = sm[:, SM_AL:SM_AL + RW_A_RANK]
    kk = k * kkp_ref[...]
    kk = kk * lax.rsqrt(_mm(kk * kk, seg, HI) + EPS)
    for hp in range(n_pairs):
        psl = slice(hp * pw, (hp + 1) * pw)
        rp_ref[hp] = r[:, psl]
        vp_ref[hp] = v_ref[:, psl]
        kk_ref[hp] = kk[:, psl]
    for z in range(2):
        w_pre = w0_ref[z:z + 1, :] + _mm(wl, wup_ref[z], HI)
        lw = -RW_DECAY_SCALE * _sigmoid(w_pre)
        a = _sigmoid(a0_ref[z:z + 1, :] + _mm(al, aup_ref[z], HI))
        ka = kk * a
        kd = k * (1.0 + (a - 1.0) * kap_ref[...])
        for hp in range(n_pairs):
            psl = slice(hp * pw, (hp + 1) * pw)
            lw_ref[z * n_pairs + hp] = lw[:, psl]
            ka_ref[z * n_pairs + hp] = ka[:, psl]
            kd_ref[z * n_pairs + hp] = kd[:, psl]
    for i in range(2 * RW_HEADS):
        st_ref[i] = s0_ref[i // RW_HEADS, i % RW_HEADS] if has_s0 else jnp.zeros((RW_DH, RW_DH), F32)

    incl, strict, ti, si = _tri_masks(CHUNK)
    diag_blk = (ti // SUB == si // SUB).astype(F32)
    eye = (ti == si).astype(F32)

    def chunk_pair(ci, hp):
        for z in range(2):
            c = ci if z == 0 else nc - 1 - ci
            rows = pl.ds(pl.multiple_of(c * CHUNK, CHUNK), CHUNK)
            lwc = lw_ref[z * n_pairs + hp, rows, :]
            cum = _mm(incl[z], lwc, HI)
            tot = cum[CHUNK - 1:CHUNK] if z == 0 else cum[0:1]
            e_neg = jnp.exp(-cum)
            dec = jnp.exp(tot - cum)
            e_tot = jnp.exp(tot)
            kac = ka_ref[z * n_pairs + hp, rows, :]
            kdc = kd_ref[z * n_pairs + hp, rows, :]
            rt = rp_ref[hp, rows, :] * jnp.exp(cum)
            kt = kk_ref[hp, rows, :] * jnp.exp(cum - lwc)
            at, kdt = kac * e_neg, kdc * e_neg
            ah, kh = kac * dec, kdc * dec
            vc = vp_ref[hp, rows, :]
            for j in range(2):
                sl = slice(j * RW_DH, (j + 1) * RW_DH)
                si_ = z * RW_HEADS + hp * 2 + j
                s0 = st_ref[si_]
                kt_h, rt_h, v_h = kt[:, sl], rt[:, sl], vc[:, sl]
                low = _mm_nt(kt_h, at[:, sl], HI) * strict[z]
                a_kk = _mm_nt(kt_h, kdt[:, sl], HI) * strict[z]
                a_ra = _mm_nt(rt_h, at[:, sl], HI) * incl[z]
                a_rk = _mm_nt(rt_h, kdt[:, sl], HI) * incl[z]
                ld = low * diag_blk
                lo = low - ld
                l2 = _mm(ld, ld, HI)
                l4 = _mm(l2, l2, HI)
                l8 = _mm(l4, l4, HI)
                tinv = eye - ld
                tinv = tinv + _mm(tinv, l2, HI)
                tinv = tinv + _mm(tinv, l4, HI)
                tinv = tinv + _mm(tinv, l8, HI)
                rhs = -(_mm_nt(kt_h, s0, HI) + _mm(a_kk, v_h, HI))
                cv = _mm(tinv, rhs, HI)
                wm = _mm(tinv, lo, HI)
                uu = cv
                for _ in range(CHUNK // SUB - 1):
                    uu = cv - _mm(wm, uu, HI)
                y = _mm_nt(rt_h, s0, HI) + _mm(a_ra, uu, HI) + _mm(a_rk, v_h, HI)
                ys_ref[z * n_pairs + hp, rows, sl] = y
                st_ref[si_] = s0 * e_tot[:, sl] + _mm_tn(uu, ah[:, sl], HI) + _mm_tn(v_h, kh[:, sl], HI)

    def chunk_step(ci, carry):
        def pair_step(hp, carry2):
            chunk_pair(ci, hp)
            return carry2
        return lax.fori_loop(0, n_pairs, pair_step, carry)

    lax.fori_loop(0, nc, chunk_step, 0)

    if emit_state:
        for i in range(2 * RW_HEADS):
            so_ref[i // RW_HEADS, i % RW_HEADS] = st_ref[i]

    g = _mm(_sigmoid(sm[:, SM_GL:SM_GL + RW_G_RANK]), gup_ref[...], HI)
    for h in range(RW_HEADS):
        hp, j = divmod(h, 2)
        sl = slice(j * RW_DH, (j + 1) * RW_DH)
        hsl = slice(h * RW_DH, (h + 1) * RW_DH)
        y = ys_ref[hp, :, sl] + ys_ref[n_pairs + hp, :, sl]
        mu = jnp.mean(y, axis=-1, keepdims=True)
        var = jnp.mean(jnp.square(y - mu), axis=-1, keepdims=True)
        yn = (y - mu) * lax.rsqrt(var + RW_GN_EPS) * lng_ref[:, hsl] + lnb_ref[:, hsl]
        r_h, k_h, v_h = rp_ref[hp, :, sl], k_ref[:, hsl], vp_ref[hp, :, sl]
        bonus = jnp.sum(r_h * k_h * rkp_ref[:, hsl], axis=-1, keepdims=True) * v_h
        y_ref[:, hsl] = ((yn + bonus) * g[:, hsl]).astype(y_ref.dtype)


def _rwkv_old(u, p, layer, n_seq, seq, row_off, s0=None, emit_state=False):
    lw = lambda shape: pl.BlockSpec((None,) + shape, lambda b: (layer,) + (0,) * len(shape))
    row = lambda a: a.reshape(DEPTH, 1, MIX_W)
    in_specs = [_u_spec(seq, cb, row_off) for cb in (CB_RWR, CB_RWK, CB_RWV, CB_SMALL)] + [
        lw((2, MIX_W)), lw((2, RW_W_RANK, MIX_W)), lw((2, MIX_W)), lw((2, RW_A_RANK, MIX_W)),
        lw((RW_G_RANK, MIX_W))] + [lw((1, MIX_W))] * 5
    args = [u, u, u, u, p['rw_w0'], p['rw_w_up'], p['rw_a0'], p['rw_a_up'], p['rw_g_up'],
            row(p['rw_k_k']), row(p['rw_k_a']), row(p['rw_r_k']), row(p['rw_ln_g']), row(p['rw_ln_b'])]
    st_shape = (2, RW_HEADS, RW_DH, RW_DH)
    if s0 is not None:
        in_specs.append(pl.BlockSpec((None, None) + st_shape, lambda b: (b, layer, 0, 0, 0, 0)))
        args.append(s0)
    out_shape = [jax.ShapeDtypeStruct((n_seq * seq, MIX_W), BF16)]
    out_specs = [pl.BlockSpec((seq, MIX_W), lambda b: (b, 0))]
    if emit_state:
        out_shape.append(jax.ShapeDtypeStruct((n_seq,) + st_shape, F32))
        out_specs.append(pl.BlockSpec((None,) + st_shape, lambda b: (b, 0, 0, 0, 0)))
    n_pairs = RW_HEADS // 2
    pair = lambda n: pltpu.VMEM((n, seq, 2 * RW_DH), F32)
    return pl.pallas_call(
        functools.partial(_rwkv_kernel, seq=seq, has_s0=s0 is not None, emit_state=emit_state),
        out_shape=tuple(out_shape), grid=(n_seq,), in_specs=in_specs, out_specs=tuple(out_specs),
        scratch_shapes=[pair(n_pairs), pair(n_pairs), pair(n_pairs), pair(2 * n_pairs), pair(2 * n_pairs),
                        pair(2 * n_pairs), pair(2 * n_pairs), pltpu.VMEM((2 * RW_HEADS, RW_DH, RW_DH), F32)],
        compiler_params=_params(("parallel",)),
        name=f"rwkv_{seq}",
    )(*args)


def _log_sigmoid(x):
    return jnp.minimum(x, 0.0) - jnp.log(1.0 + jnp.exp(-jnp.abs(x)))


def _mlstm_kernel(*refs, seq, has_s0, emit_state):
    it = iter(refs)
    q_ref, k_ref, v_ref, o_ref, sm_ref, gr_ref, bc_ref, br_ref, ng_ref = (next(it) for _ in range(9))
    c0_ref, n0_ref, m0_ref = ((next(it), next(it), next(it)) if has_s0 else (None, None, None))
    y_ref = next(it)
    co_ref, no_ref, mo_ref = ((next(it), next(it), next(it)) if emit_state else (None, None, None))
    hs_ref, c_ref, n_ref, m_ref = (next(it) for _ in range(4))

    nc = seq // CHUNK
    n_st = 2 * ML_HEADS
    for i in range(n_st):
        z, h = divmod(i, ML_HEADS)
        c_ref[i] = c0_ref[z, h] if has_s0 else jnp.zeros((ML_DH, ML_DH), F32)
        n_ref[i] = n0_ref[z, h] if has_s0 else jnp.zeros((1, ML_DH), F32)
        m_ref[i] = m0_ref[z, h] if has_s0 else jnp.zeros((1, 1), F32)

    incl, _, ti, si = _tri_masks(CHUNK)
    before = ((si <= ti), (si >= ti))

    def chunk_step(ci, carry):
        for z in range(2):
            c = ci if z == 0 else nc - 1 - ci
            rows = pl.ds(pl.multiple_of(c * CHUNK, CHUNK), CHUNK)
            g0 = SM_GATE + z * 2 * ML_HEADS
            gc = sm_ref[rows, g0:g0 + 2 * ML_HEADS] + bc_ref[:, z * 2 * ML_HEADS:(z + 1) * 2 * ML_HEADS]
            gr = gr_ref[z * 2 * ML_HEADS:(z + 1) * 2 * ML_HEADS, pl.ds(c, 1), :]
            i_cols, f_cols = gc[:, :ML_HEADS], gc[:, ML_HEADS:]
            b_cols = _mm(incl[z], _log_sigmoid(f_cols), HI)
            for h in range(ML_HEADS):
                st = z * ML_HEADS + h
                i_col, b_col = i_cols[:, h:h + 1], b_cols[:, h:h + 1]
                i_row = gr[h] + br_ref[z * 2 * ML_HEADS + h]
                f_row = gr[ML_HEADS + h] + br_ref[z * 2 * ML_HEADS + ML_HEADS + h]
                b_row = _mm(_log_sigmoid(f_row), incl[1 - z], HI)
                b_last = b_col[CHUNK - 1:CHUNK] if z == 0 else b_col[0:1]
                m_old = m_ref[st]
                cm, nv = c_ref[st], n_ref[st]
                hsl = slice(h * ML_DH, (h + 1) * ML_DH)
                qc = q_ref[rows, hsl] * (ML_DH ** -0.5)
                kc = k_ref[rows, hsl]
                vc = v_ref[rows, hsl]
                a_t = b_col + m_old
                dmat = jnp.where(before[z], b_col - b_row + i_row, NEG_INF)
                m_t = jnp.maximum(a_t, jnp.max(dmat, axis=-1, keepdims=True))
                dw = jnp.exp(dmat - m_t)
                inter = jnp.exp(a_t - m_t)
                s = _mm_nt(qc, kc, HI) * dw
                num = inter * _mm_nt(qc, cm, HI) + _mm(s, vc, HI)
                den = inter * jnp.sum(qc * nv, axis=-1, keepdims=True) + jnp.sum(s, axis=-1, keepdims=True)
                hh = num / jnp.maximum(jnp.abs(den), jnp.exp(-m_t))
                hs_ref[z, rows, hsl] = hh
                g_col = b_last - b_col + i_col
                a_l = b_last + m_old
                m_new = jnp.maximum(a_l, jnp.max(g_col, axis=0, keepdims=True))
                decay = jnp.exp(a_l - m_new)
                wgt = jnp.exp(g_col - m_new)
                c_ref[st] = decay * cm + _mm_tn(vc * wgt, kc, HI)
                n_ref[st] = decay * nv + jnp.sum(wgt * kc, axis=0, keepdims=True)
                m_ref[st] = m_new
        return carry

    lax.fori_loop(0, nc, chunk_step, 0)

    if emit_state:
        for i in range(n_st):
            z, h = divmod(i, ML_HEADS)
            co_ref[z, h] = c_ref[i]
            no_ref[z, h] = n_ref[i]
            mo_ref[z, h] = m_ref[i]

    for h in range(ML_HEADS):
        hsl = slice(h * ML_DH, (h + 1) * ML_DH)
        hn = _head_rms(hs_ref[0, :, hsl] + hs_ref[1, :, hsl], ng_ref[:, hsl])
        y_ref[:, hsl] = (_sigmoid(o_ref[:, hsl]) * hn).astype(y_ref.dtype)


def _mlstm(u, gate_rows, p, layer, n_seq, seq, row_off, state=None, emit_state=False):
    lw = lambda shape: pl.BlockSpec((None,) + shape, lambda b: (layer,) + (0,) * len(shape))
    nc = seq // CHUNK
    n_gate = 4 * ML_HEADS
    in_specs = [_u_spec(seq, cb, row_off) for cb in (CB_MLQ, CB_MLK, CB_MLV, CB_MLO, CB_SMALL)] + [
        pl.BlockSpec((n_gate, None, nc, CHUNK), lambda b: (0, b, 0, 0)),
        lw((1, n_gate)), lw((n_gate, 1, 1)), lw((1, MIX_W))]
    args = [u, u, u, u, u, gate_rows, p['ml_gate_b'].reshape(DEPTH, 1, n_gate),
            p['ml_gate_b'].reshape(DEPTH, n_gate, 1, 1), p['ml_norm_g'].reshape(DEPTH, 1, MIX_W)]
    c_shape, n_shape, m_shape = (2, ML_HEADS, ML_DH, ML_DH), (2, ML_HEADS, 1, ML_DH), (2, ML_HEADS, 1, 1)
    if state is not None:
        c0, n0, m0 = state
        for a, shp in ((c0, c_shape), (n0, n_shape), (m0, m_shape)):
            in_specs.append(pl.BlockSpec((None, None) + shp, lambda b: (b, layer, 0, 0, 0, 0)))
            args.append(a.reshape(a.shape[:2] + shp))
    out_shape = [jax.ShapeDtypeStruct((n_seq * seq, MIX_W), BF16)]
    out_specs = [pl.BlockSpec((seq, MIX_W), lambda b: (b, 0))]
    if emit_state:
        for shp in (c_shape, n_shape, m_shape):
            out_shape.append(jax.ShapeDtypeStruct((n_seq,) + shp, F32))
            out_specs.append(pl.BlockSpec((None,) + shp, lambda b: (b, 0, 0, 0, 0)))
    n_st = 2 * ML_HEADS
    return pl.pallas_call(
        functools.partial(_mlstm_kernel, seq=seq, has_s0=state is not None, emit_state=emit_state),
        out_shape=tuple(out_shape), grid=(n_seq,), in_specs=in_specs, out_specs=tuple(out_specs),
        scratch_shapes=[pltpu.VMEM((2, seq, MIX_W), F32), pltpu.VMEM((n_st, ML_DH, ML_DH), F32),
                        pltpu.VMEM((n_st, 1, ML_DH), F32), pltpu.VMEM((n_st, 1, 1), F32)],
        compiler_params=_params(("parallel",)),
        name=f"mlstm_{seq}",
    )(*args)


def _outproj_kernel(ya_ref, yb_ref, yc_ref, yd_ref, w_ref, x_ref, g_ref, o_ref):
    acc = _mm(ya_ref[...], w_ref[0:MIX_W, :])
    for i, y_ref in enumerate((yb_ref, yc_ref, yd_ref), start=1):
        acc += _mm(y_ref[...], w_ref[i * MIX_W:(i + 1) * MIX_W, :])
    o_ref[...] = x_ref[...] + g_ref[...] * acc


def _outproj(ys, w_out_b, x, mod, layer):
    tm, tn = 1024, 1024
    y_spec = pl.BlockSpec((tm, MIX_W), lambda i, j: (i, 0))
    return pl.pallas_call(
        _outproj_kernel,
        out_shape=jax.ShapeDtypeStruct((N_TOK, D_MODEL), F32),
        grid=(N_TOK // tm, D_MODEL // tn),
        in_specs=[y_spec] * 4 + [pl.BlockSpec((None, D_MODEL, tn), lambda i, j: (layer, 0, j)),
                                 pl.BlockSpec((tm, tn), lambda i, j: (i, j)),
                                 _mod_spec(layer, 2, tm, tn, with_j=True)],
        out_specs=pl.BlockSpec((tm, tn), lambda i, j: (i, j)),
        compiler_params=_params(("parallel", "parallel")),
        name="outproj",
    )(*ys, w_out_b, x, mod)


def _router_kernel(x_ref, g_ref, sh_ref, sc_ref, rw_ref, rb_ref, xm_ref, gt_ref):
    xm = _modulated(x_ref[...], g_ref[...], sh_ref[...], sc_ref[...])
    xm_ref[...] = xm.astype(BF16)
    logits = _mm_nt(rw_ref[...], xm, HI)
    ex = jnp.exp(logits - jnp.max(logits, axis=0, keepdims=True))
    scores = ex / jnp.sum(ex, axis=0, keepdims=True)
    sel = scores + rb_ref[...]
    per = N_EXPERTS // N_EXPERT_GROUPS
    s = [sel[e:e + 1, :] for e in range(N_EXPERTS)]
    grp_score = []
    for g in range(N_EXPERT_GROUPS):
        a, b, c, d = s[per * g:per * (g + 1)]
        hi1, lo1, hi2, lo2 = jnp.maximum(a, b), jnp.minimum(a, b), jnp.maximum(c, d), jnp.minimum(c, d)
        grp_score.append(jnp.maximum(hi1, hi2) + jnp.maximum(jnp.minimum(hi1, hi2), jnp.maximum(lo1, lo2)))
    best = functools.reduce(jnp.maximum, grp_score)
    in_grp, taken = [], jnp.zeros_like(best)
    for g in range(N_EXPERT_GROUPS):
        hit = jnp.where(grp_score[g] == best, 1.0, 0.0) * (1.0 - taken)
        in_grp.append(hit)
        taken = taken + hit
    picked = []
    for e in range(N_EXPERTS):
        g = e // per
        rank = jnp.zeros_like(best)
        for o in range(per * g, per * (g + 1)):
            if o < e:
                rank += jnp.where(s[o] >= s[e], 1.0, 0.0)
            elif o > e:
                rank += jnp.where(s[o] > s[e], 1.0, 0.0)
        picked.append(in_grp[g] * jnp.where(rank < 2.0, 1.0, 0.0) * scores[e:e + 1, :])
    total = functools.reduce(lambda x, y: x + y, picked)
    for e in range(N_EXPERTS):
        gt_ref[e:e + 1, :] = picked[e] / total


def _router(x, norm_g, mod, router_wt, router_b, layer):
    tm = 512
    return pl.pallas_call(
        _router_kernel,
        out_shape=(jax.ShapeDtypeStruct((N_TOK, D_MODEL), BF16), jax.ShapeDtypeStruct((N_EXPERTS, N_TOK), F32)),
        grid=(N_TOK // tm,),
        in_specs=[pl.BlockSpec((tm, D_MODEL), lambda i: (i, 0)),
                  pl.BlockSpec((None, 1, D_MODEL), lambda i: (layer, 0, 0)),
                  _mod_spec(layer, 3, tm), _mod_spec(layer, 4, tm),
                  _full((N_EXPERTS, D_MODEL)), _full((N_EXPERTS, 1))],
        out_specs=(pl.BlockSpec((tm, D_MODEL), lambda i: (i, 0)), pl.BlockSpec((N_EXPERTS, tm), lambda i: (0, i))),
        compiler_params=_params(("parallel",)),
        name="router",
    )(x, norm_g.reshape(DEPTH, 1, D_MODEL), mod, mod, router_wt, router_b.reshape(N_EXPERTS, 1))


def _experts_kernel(xm_ref, gates_ref, w1_ref, w3_ref, w2_ref, x_ref, g_ref, o_ref, acc_ref):
    e = pl.program_id(1)

    @pl.when(e == 0)
    def _():
        acc_ref[...] = jnp.zeros_like(acc_ref)

    xm = xm_ref[...]
    h1 = _mm(xm, w1_ref[...])
    h3 = _mm(xm, w3_ref[...])
    gates = gates_ref[...]
    lane = lax.broadcasted_iota(jnp.int32, gates.shape, 1)
    gate = jnp.sum(jnp.where(lane == e, gates, 0.0), axis=-1, keepdims=True)
    hh = h1 * _sigmoid(h1) * h3 * gate
    acc_ref[...] += _mm(hh.astype(BF16), w2_ref[...])

    @pl.when(e == N_EXPERTS - 1)
    def _():
        o_ref[...] = x_ref[...] + g_ref[...] * acc_ref[...]


def _experts(xm, gates, w1_b, w3_b, w2_b, x, mod, layer):
    tm = 512
    row = pl.BlockSpec((tm, D_MODEL), lambda i, e: (i, 0))
    return pl.pallas_call(
        _experts_kernel,
        out_shape=jax.ShapeDtypeStruct((N_TOK, D_MODEL), F32),
        grid=(N_TOK // tm, N_EXPERTS),
        in_specs=[row, pl.BlockSpec((tm, N_EXPERTS), lambda i, e: (i, 0)),
                  pl.BlockSpec((None, None, D_MODEL, D_EXPERT), lambda i, e: (layer, e, 0, 0)),
                  pl.BlockSpec((None, None, D_MODEL, D_EXPERT), lambda i, e: (layer, e, 0, 0)),
                  pl.BlockSpec((None, None, D_EXPERT, D_MODEL), lambda i, e: (layer, e, 0, 0)),
                  row, _mod_spec(layer, 5, tm)],
        out_specs=row,
        scratch_shapes=[pltpu.VMEM((tm, D_MODEL), F32)],
        compiler_params=_params(("parallel", "arbitrary")),
        name="experts",
    )(xm, gates, w1_b, w3_b, w2_b, x, mod)


def _gate_rows(gcols, n_seq, seq):
    return gcols.reshape(n_seq, seq // CHUNK, CHUNK, gcols.shape[-1]).transpose(3, 0, 1, 2)


def kernel(x_prompt, x_sample, cache_na_k, cache_na_v, state_rwkv, state_mlstm_c, state_mlstm_n, state_mlstm_m,
           c, c_ctx, norm1_g, norm2_g, w_mod, b_mod, w_in, conv_w, na_q_g, na_k_g, na_rpb, rw_w0, rw_w_up, rw_a0,
           rw_a_up, rw_g_up, rw_k_k, rw_k_a, rw_r_k, rw_ln_g, rw_ln_b, ml_gate_b, ml_norm_g, w_out, router_w,
           router_b, moe_w1, moe_w3, moe_w2):
    p = dict(rw_w0=rw_w0, rw_w_up=rw_w_up, rw_a0=rw_a0, rw_a_up=rw_a_up, rw_g_up=rw_g_up, rw_k_k=rw_k_k,
             rw_k_a=rw_k_a, rw_r_k=rw_r_k, rw_ln_g=rw_ln_g, rw_ln_b=rw_ln_b, ml_gate_b=ml_gate_b,
             ml_norm_g=ml_norm_g)
    cvecs = jnp.concatenate([c_ctx[None], c, jnp.zeros((MOD_ROWS - 1 - DEC_BATCH, D_MODEL), F32)], axis=0)
    mod = _adaln(cvecs, w_mod, b_mod).reshape(DEPTH, MOD_ROWS, 6, 1, D_MODEL)

    n_wide_a = 9 * MIX_W
    n_narrow = RW_W_RANK + RW_A_RANK + RW_G_RANK
    n_wide_b = 4 * MIX_W
    w_in_b = jnp.concatenate(
        [w_in[..., :n_wide_a], w_in[..., n_wide_a + n_narrow:n_wide_a + n_narrow + n_wide_b],
         w_in[..., n_wide_a:n_wide_a + n_narrow], w_in[..., n_wide_a + n_narrow + n_wide_b:],
         jnp.zeros((DEPTH, D_MODEL, P_PAD - w_in.shape[-1]), F32)], axis=-1).astype(BF16)
    w_out_b = w_out.astype(BF16)
    w1_b, w3_b, w2_b = moe_w1.astype(BF16), moe_w3.astype(BF16), moe_w2.astype(BF16)
    tw = _na_tables(na_rpb)
    router_wt = router_w.T
    sample_row_off = N_PROMPT // DEC_SEQ

    x = jnp.concatenate([x_prompt.reshape(N_PROMPT, D_MODEL), x_sample.reshape(N_SAMPLE, D_MODEL)], axis=0)
    new_k, new_v, new_rw, new_c, new_n, new_m = [], [], [], [], [], []
    for l in range(DEPTH):
        u = _inproj(x, norm1_g, mod, w_in_b, l)
        ya_p, yb_p, nk, nv = _attn_prompt(u, conv_w, na_q_g, na_k_g, l)
        ya_s, yb_s = _na_sample(u, cache_na_k, cache_na_v, tw, conv_w, na_q_g, na_k_g, l)
        yc_p, st = _rwkv(u, p, l, BATCH, SEQ, 0, emit_state=True)
        (yc_s,) = _rwkv(u, p, l, DEC_BATCH, DEC_SEQ, sample_row_off, s0=state_rwkv)
        g0 = CB_SMALL * MIX_W + SM_GATE
        gcols = u[:, g0:g0 + 4 * ML_HEADS]
        yd_p, cm, nm, mm = _mlstm(u, _gate_rows(gcols[:N_PROMPT], BATCH, SEQ), p, l, BATCH, SEQ, 0,
                                  emit_state=True)
        (yd_s,) = _mlstm(u, _gate_rows(gcols[N_PROMPT:], DEC_BATCH, DEC_SEQ), p, l, DEC_BATCH, DEC_SEQ,
                         sample_row_off, state=(state_mlstm_c, state_mlstm_n, state_mlstm_m))
        ys = [jnp.concatenate(pair, axis=0) for pair in ((ya_p, ya_s), (yb_p, yb_s), (yc_p, yc_s), (yd_p, yd_s))]
        x = _outproj(ys, w_out_b, x, mod, l)
        xm, gates_t = _router(x, norm2_g, mod, router_wt, router_b, l)
        x = _experts(xm, gates_t.T, w1_b, w3_b, w2_b, x, mod, l)
        new_k.append(nk)
        new_v.append(nv)
        new_rw.append(st)
        new_c.append(cm)
        new_n.append(nm.reshape(BATCH, 2, ML_HEADS, ML_DH))
        new_m.append(mm.reshape(BATCH, 2, ML_HEADS))
    stack = lambda xs: jnp.stack(xs, axis=1)
    return (x[:N_PROMPT].reshape(BATCH, SEQ, D_MODEL), x[N_PROMPT:].reshape(DEC_BATCH, DEC_SEQ, D_MODEL),
            stack(new_k), stack(new_v), stack(new_rw), stack(new_c), stack(new_n), stack(new_m))
```

```python
import functools

import numpy as np
import jax
import jax.numpy as jnp
from jax import lax
from jax.experimental import pallas as pl
from jax.experimental.pallas import tpu as pltpu

F32 = jnp.float32
BF16 = jnp.bfloat16
HI = lax.Precision.HIGHEST

D_MODEL = 2048
BATCH = 16
SEQ = 256
DEPTH = 2
DEC_BATCH = 2
DEC_SEQ = 1024
PAST_LEN = 512
GRID_W = 64
MIX_W = D_MODEL // 4
CONV_K = 3
NA_DH = 64
NA_HEADS = MIX_W // NA_DH
NA_WIN_R = 8
NA_WIN_C = 16
NA_SCALE = NA_DH ** -0.5
ROPE_THETA = 10000.0
RW_DH = 64
RW_HEADS = MIX_W // RW_DH
RW_W_RANK = 64
RW_A_RANK = 64
RW_G_RANK = 128
RW_DECAY_SCALE = 0.606531
RW_GN_EPS = 64e-5
ML_DH = 128
ML_HEADS = MIX_W // ML_DH
N_EXPERTS = 16
N_EXPERT_GROUPS = 4
D_EXPERT = 512
EPS = 1e-6
NEG_INF = -1e30

N_PROMPT = BATCH * SEQ
N_SAMPLE = DEC_BATCH * DEC_SEQ
N_TOK = N_PROMPT + N_SAMPLE
MOD_ROWS = 8
CHUNK = 64
SUB = 16
P1_CHUNKS = 2
P_BLOCKS = 14
P_PAD = P_BLOCKS * MIX_W
(CB_CVB, CB_CVC, CB_CVH, CB_NAQ, CB_NAK, CB_NAV, CB_RWR, CB_RWK, CB_RWV,
 CB_MLQ, CB_MLK, CB_MLV, CB_MLO, CB_SMALL) = range(P_BLOCKS)
SM_WL, SM_AL, SM_GL, SM_GATE = 0, 64, 128, 256
VMEM_LIMIT = 56 * 1024 * 1024


def _mm(a, b, prec=None):
    return jnp.dot(a, b, precision=prec, preferred_element_type=F32)


def _mm_nt(a, b, prec=None):
    return lax.dot_general(a, b, (((1,), (1,)), ((), ())), precision=prec, preferred_element_type=F32)


def _mm_tn(a, b, prec=None):
    return _mm(a.T, b, prec)


def _sigmoid(x):
    return 1.0 / (1.0 + jnp.exp(-x))


def _full(shape):
    n = len(shape)
    return pl.BlockSpec(shape, lambda *_: (0,) * n)


def _params(sem):
    return pltpu.CompilerParams(dimension_semantics=sem, vmem_limit_bytes=VMEM_LIMIT)


def _mod_row(i, tm):
    n_prompt_tiles = N_PROMPT // tm
    tiles_per_sample = DEC_SEQ // tm
    return jnp.where(i < n_prompt_tiles, 0, 1 + (i - n_prompt_tiles) // tiles_per_sample)


def _mod_spec(layer, chunk, tm, tn=D_MODEL, with_j=False):
    if with_j:
        return pl.BlockSpec((None, None, None, 1, tn), lambda i, j: (layer, _mod_row(i, tm), chunk, 0, j))
    return pl.BlockSpec((None, None, None, 1, tn), lambda i, *_: (layer, _mod_row(i, tm), chunk, 0, 0))


def _tri_masks(n):
    t = lax.broadcasted_iota(jnp.int32, (n, n), 0)
    s = lax.broadcasted_iota(jnp.int32, (n, n), 1)
    incl = ((s <= t).astype(F32), (s >= t).astype(F32))
    strict = ((s < t).astype(F32), (s > t).astype(F32))
    return incl, strict, t, s


def _adaln_kernel(cv_ref, w_ref, b_ref, o_ref):
    cv = cv_ref[...]
    o_ref[...] = _mm(cv * _sigmoid(cv), w_ref[...], HI) + b_ref[...]


def _adaln(cvecs, w_mod, b_mod):
    tn = 1024
    n_out = 6 * D_MODEL
    return pl.pallas_call(
        _adaln_kernel,
        out_shape=jax.ShapeDtypeStruct((DEPTH, MOD_ROWS, n_out), F32),
        grid=(DEPTH, n_out // tn),
        in_specs=[_full((MOD_ROWS, D_MODEL)),
                  pl.BlockSpec((None, D_MODEL, tn), lambda l, j: (l, 0, j)),
                  pl.BlockSpec((None, 1, tn), lambda l, j: (l, 0, j))],
        out_specs=pl.BlockSpec((None, MOD_ROWS, tn), lambda l, j: (l, 0, j)),
        compiler_params=_params(("parallel", "parallel")),
        name="adaln",
    )(cvecs, w_mod, b_mod.reshape(DEPTH, 1, n_out))


def _modulated(x, g, sh, sc):
    y = x * lax.rsqrt(jnp.mean(x * x, axis=-1, keepdims=True) + EPS) * g
    return y * (1.0 + sc) + sh


def _inproj_kernel(x_ref, g_ref, sh_ref, sc_ref, w_ref, o_ref, xm_ref):
    @pl.when(pl.program_id(1) == 0)
    def _():
        xm_ref[...] = _modulated(x_ref[...], g_ref[...], sh_ref[...], sc_ref[...]).astype(BF16)

    o_ref[...] = _mm(xm_ref[...], w_ref[...])


def _inproj(x, norm_g, mod, w_in_b, layer):
    tm, tn = 512, 1024
    return pl.pallas_call(
        _inproj_kernel,
        out_shape=jax.ShapeDtypeStruct((N_TOK, P_PAD), F32),
        grid=(N_TOK // tm, P_PAD // tn),
        in_specs=[pl.BlockSpec((tm, D_MODEL), lambda i, j: (i, 0)),
                  pl.BlockSpec((None, 1, D_MODEL), lambda i, j: (layer, 0, 0)),
                  _mod_spec(layer, 0, tm), _mod_spec(layer, 1, tm),
                  pl.BlockSpec((None, D_MODEL, tn), lambda i, j: (layer, 0, j))],
        out_specs=pl.BlockSpec((tm, tn), lambda i, j: (i, j)),
        scratch_shapes=[pltpu.VMEM((tm, D_MODEL), BF16)],
        compiler_params=_params(("parallel", "arbitrary")),
        name="inproj",
    )(x, norm_g.reshape(DEPTH, 1, D_MODEL), mod, mod, w_in_b)


def _conv_mix(b, c, h, w):
    u = c * h
    n = u.shape[0]
    row = lax.broadcasted_iota(jnp.int32, u.shape, 0)
    prev = jnp.where(row == 0, 0.0, pltpu.roll(u, 1, axis=0))
    nxt = jnp.where(row == n - 1, 0.0, pltpu.roll(u, n - 1, axis=0))
    return b * (prev * w[0:1] + u * w[1:2] + nxt * w[2:3])


def _head_rms(x, g):
    return x * lax.rsqrt(jnp.mean(x * x, axis=-1, keepdims=True) + EPS) * g


def _attn_prompt_kernel(cb_ref, cc_ref, ch_ref, q_ref, k_ref, v_ref, cw_ref, qg_ref, kg_ref,
                        ya_ref, yb_ref, nk_ref, nv_ref):
    ya_ref[...] = _conv_mix(cb_ref[...], cc_ref[...], ch_ref[...], cw_ref[...]).astype(ya_ref.dtype)
    for h in range(NA_HEADS):
        sl = slice(h * NA_DH, (h + 1) * NA_DH)
        qn = _head_rms(q_ref[:, sl], qg_ref[...]) * NA_SCALE
        kn = _head_rms(k_ref[:, sl], kg_ref[...])
        vh = v_ref[:, sl]
        s = _mm_nt(qn.astype(BF16), kn.astype(BF16))
        p = jnp.exp(s - jnp.max(s, axis=-1, keepdims=True))
        o = _mm(p.astype(BF16), vh.astype(BF16)) / jnp.sum(p, axis=-1, keepdims=True)
        yb_ref[:, sl] = o.astype(yb_ref.dtype)
        nk_ref[h] = kn
        nv_ref[h] = vh


def _u_spec(rows, col_block, row_off_blocks=0):
    return pl.BlockSpec((rows, MIX_W), lambda b: (b + row_off_blocks, col_block))


def _attn_prompt(u, conv_w, q_g, k_g, layer, n_seq=BATCH, seq=SEQ):
    lw = lambda shape: pl.BlockSpec((None,) + shape, lambda b: (layer,) + (0,) * len(shape))
    y_spec = pl.BlockSpec((seq, MIX_W), lambda b: (b, 0))
    kv_spec = pl.BlockSpec((None, NA_HEADS, seq, NA_DH), lambda b: (b, 0, 0, 0))
    return pl.pallas_call(
        _attn_prompt_kernel,
        out_shape=(jax.ShapeDtypeStruct((n_seq * seq, MIX_W), BF16),) * 2
        + (jax.ShapeDtypeStruct((n_seq, NA_HEADS, seq, NA_DH), F32),) * 2,
        grid=(n_seq,),
        in_specs=[_u_spec(seq, cb) for cb in (CB_CVB, CB_CVC, CB_CVH, CB_NAQ, CB_NAK, CB_NAV)]
        + [lw((CONV_K, MIX_W)), lw((1, NA_DH)), lw((1, NA_DH))],
        out_specs=(y_spec, y_spec, kv_spec, kv_spec),
        compiler_params=_params(("parallel",)),
        name="attn_prompt",
    )(u, u, u, u, u, u, conv_w, q_g.reshape(DEPTH, 1, NA_DH), k_g.reshape(DEPTH, 1, NA_DH))


def _na_kernel(cb_ref, cc_ref, ch_ref, q_ref, k_ref, v_ref, kc_ref, vc_ref, tw_ref, cos_ref, sin_ref,
               perm_ref, cw_ref, qg_ref, kg_ref, ya_ref, yb_ref, qs_ref, ks_ref):
    rows = DEC_SEQ // GRID_W
    wr = min(NA_WIN_R, rows)
    nw = wr * GRID_W
    ya_ref[...] = _conv_mix(cb_ref[...], cc_ref[...], ch_ref[...], cw_ref[...]).astype(ya_ref.dtype)
    cos, sin, perm = cos_ref[...], sin_ref[...], perm_ref[...]

    def rope(x):
        return x * cos + _mm(x, perm, HI) * sin

    for h in range(NA_HEADS):
        sl = slice(h * NA_DH, (h + 1) * NA_DH)
        qs_ref[...] = (rope(_head_rms(q_ref[:, sl], qg_ref[...])) * NA_SCALE).astype(BF16)
        ks_ref[...] = rope(_head_rms(k_ref[:, sl], kg_ref[...])).astype(BF16)
        kch = kc_ref[h].astype(BF16)
        vch = vc_ref[h].astype(BF16)

        def row_step(r, carry):
            rs = jnp.clip(r - wr // 2, 0, rows - wr)
            q0 = pl.multiple_of(r * GRID_W, GRID_W)
            k0 = pl.multiple_of(rs * GRID_W, GRID_W)
            q_r = qs_ref[pl.ds(q0, GRID_W), :]
            s_w = _mm_nt(q_r, ks_ref[pl.ds(k0, nw), :]) + tw_ref[h, r - rs]
            s_c = _mm_nt(q_r, kch)
            m = jnp.maximum(jnp.max(s_w, axis=-1, keepdims=True), jnp.max(s_c, axis=-1, keepdims=True))
            p_w = jnp.exp(s_w - m)
            p_c = jnp.exp(s_c - m)
            den = jnp.sum(p_w, axis=-1, keepdims=True) + jnp.sum(p_c, axis=-1, keepdims=True)
            v_w = v_ref[pl.ds(k0, nw), sl].astype(BF16)
            o = (_mm(p_w.astype(BF16), v_w) + _mm(p_c.astype(BF16), vch)) / den
            yb_ref[pl.ds(q0, GRID_W), sl] = o.astype(yb_ref.dtype)
            return carry

        lax.fori_loop(0, rows, row_step, 0)


def _na_tables(rpb):
    rows = DEC_SEQ // GRID_W
    wr = min(NA_WIN_R, rows)
    qc = np.arange(GRID_W)
    kc = np.arange(GRID_W)
    wstart = np.clip(qc - NA_WIN_C // 2, 0, GRID_W - NA_WIN_C)
    colmask = (kc[None, :] >= wstart[:, None]) & (kc[None, :] < wstart[:, None] + NA_WIN_C)
    dc = np.clip(kc[None, :] - qc[:, None], -(NA_WIN_C - 1), NA_WIN_C - 1) + NA_WIN_C - 1
    p = np.arange(wr)
    j = np.arange(wr)
    dr = j[None, :] - p[:, None] + NA_WIN_R - 1
    bias = rpb[:, :, dr][:, :, :, :, dc]
    bias = jnp.where(colmask[None, None, None, None], bias, NEG_INF)
    tw = bias.transpose(0, 1, 2, 4, 3, 5).reshape(DEPTH, NA_HEADS, wr, GRID_W, wr * GRID_W)
    return tw


def _rope_tables():
    t = np.arange(DEC_SEQ)
    quarter = NA_DH // 4
    freq = ROPE_THETA ** (-np.arange(quarter, dtype=np.float32) / quarter)
    ang_r = (t // GRID_W).astype(np.float32)[:, None] * freq
    ang_c = (t % GRID_W).astype(np.float32)[:, None] * freq
    cos = np.concatenate([np.cos(ang_r), np.cos(ang_r), np.cos(ang_c), np.cos(ang_c)], axis=-1)
    sin = np.concatenate([-np.sin(ang_r), np.sin(ang_r), -np.sin(ang_c), np.sin(ang_c)], axis=-1)
    src = np.concatenate([np.arange(quarter) + quarter, np.arange(quarter),
                          np.arange(quarter) + 3 * quarter, np.arange(quarter) + 2 * quarter])
    perm = np.zeros((NA_DH, NA_DH), np.float32)
    perm[src, np.arange(NA_DH)] = 1.0
    return cos.astype(np.float32), sin.astype(np.float32), perm


def _na_sample(u, cache_k, cache_v, tw, conv_w, q_g, k_g, layer, n_seq=DEC_BATCH, row_off=N_PROMPT // DEC_SEQ):
    cos, sin, perm = _rope_tables()
    lw = lambda shape: pl.BlockSpec((None,) + shape, lambda b: (layer,) + (0,) * len(shape))
    y_spec = pl.BlockSpec((DEC_SEQ, MIX_W), lambda b: (b, 0))
    c_spec = pl.BlockSpec((None, None, NA_HEADS, PAST_LEN, NA_DH), lambda b: (b, layer, 0, 0, 0))
    wr = tw.shape[2]
    return pl.pallas_call(
        _na_kernel,
        out_shape=(jax.ShapeDtypeStruct((n_seq * DEC_SEQ, MIX_W), BF16),) * 2,
        grid=(n_seq,),
        in_specs=[_u_spec(DEC_SEQ, cb, row_off) for cb in (CB_CVB, CB_CVC, CB_CVH, CB_NAQ, CB_NAK, CB_NAV)]
        + [c_spec, c_spec, lw((NA_HEADS, wr, GRID_W, wr * GRID_W)),
           _full((DEC_SEQ, NA_DH)), _full((DEC_SEQ, NA_DH)), _full((NA_DH, NA_DH)),
           lw((CONV_K, MIX_W)), lw((1, NA_DH)), lw((1, NA_DH))],
        out_specs=(y_spec, y_spec),
        scratch_shapes=[pltpu.VMEM((DEC_SEQ, NA_DH), BF16), pltpu.VMEM((DEC_SEQ, NA_DH), BF16)],
        compiler_params=_params(("parallel",)),
        name="na_sample",
    )(u, u, u, u, u, u, cache_k, cache_v, tw, jnp.asarray(cos), jnp.asarray(sin), jnp.asarray(perm),
      conv_w, q_g.reshape(DEPTH, 1, NA_DH), k_g.reshape(DEPTH, 1, NA_DH))


def _seg_ones(width, seg):
    a = lax.broadcasted_iota(jnp.int32, (width, width), 0) // seg
    b = lax.broadcasted_iota(jnp.int32, (width, width), 1) // seg
    return (a == b).astype(F32)


def _split2(x):
    hi = x.astype(BF16)
    return hi, (x - hi.astype(F32)).astype(BF16)


def _mm3(a, b, nt=False):
    dot = _mm_nt if nt else _mm
    a_hi, a_lo = _split2(a)
    b_hi, b_lo = _split2(b)
    return dot(a_hi, b_lo) + dot(a_lo, b_hi) + dot(a_hi, b_hi)


def _split3(x):
    x0 = x.astype(BF16)
    r1 = x - x0.astype(F32)
    x1 = r1.astype(BF16)
    return x0, x1, (r1 - x1.astype(F32)).astype(BF16)


def _mm_exact_lhs(mask_b, x):
    x0, x1, x2 = _split3(x)
    return _mm(mask_b, x2) + _mm(mask_b, x1) + _mm(mask_b, x0)


def _mm_exact_rhs(x, mask_b):
    x0, x1, x2 = _split3(x)
    return _mm(x2, mask_b) + _mm(x1, mask_b) + _mm(x0, mask_b)


def _rwkv_kernel(*refs, seq, has_s0, emit_state):
    it = iter(refs)
    r_ref, k_ref, v_ref, sm_ref = (next(it) for _ in range(4))
    (w0_ref, wup_ref, a0_ref, aup_ref, gup_ref, kkp_ref, kap_ref, rkp_ref, lng_ref, lnb_ref) = (
        next(it) for _ in range(10))
    s0_ref = next(it) if has_s0 else None
    y_ref = next(it)
    so_ref = next(it) if emit_state else None
    kk_ref, lw_ref, ka_ref, kd_ref, coef_ref, ysp_ref = (next(it) for _ in range(6))

    dh = RW_DH
    pw = 2 * dh
    nc = seq // CHUNK
    seg_b = _seg_ones(pw, dh).astype(BF16)

    r = r_ref[...]
    k = k_ref[...]
    sm = sm_ref[...]
    wl = jnp.tanh(sm[:, SM_WL:SM_WL + RW_W_RANK])
    al = sm[:, SM_AL:SM_AL + RW_A_RANK]
    kk = k * kkp_ref[...]
    kk = kk * lax.rsqrt(_mm_exact_rhs(kk * kk, seg_b) + EPS)
    kk_ref[...] = kk
    for z in range(2):
        lw_ref[z] = -RW_DECAY_SCALE * _sigmoid(w0_ref[z:z + 1, :] + _mm3(wl, wup_ref[z]))
        a = _sigmoid(a0_ref[z:z + 1, :] + _mm3(al, aup_ref[z]))
        ka_ref[z] = kk * a
        kd_ref[z] = k * (1.0 + (a - 1.0) * kap_ref[...])

    incl, strict, ti, si = _tri_masks(CHUNK)
    incl_b = tuple(m.astype(BF16) for m in incl)
    incl2 = tuple(jnp.concatenate([m, m], axis=1) for m in incl)
    diag_blk = (ti // SUB == si // SUB).astype(F32)
    eye = (ti == si).astype(F32)
    zero_blk = jnp.zeros((CHUNK, dh), F32)

    def phase1(c2, carry):
        inst = []
        for cc in range(P1_CHUNKS):
            rows = pl.ds(pl.multiple_of((c2 * P1_CHUNKS + cc) * CHUNK, CHUNK), CHUNK)
            vc, rc, kkc = v_ref[rows, :], r_ref[rows, :], kk_ref[rows, :]
            for z in range(2):
                lwc = lw_ref[z, rows, :]
                cum = _mm_exact_lhs(incl_b[z], lwc)
                tot = cum[CHUNK - 1:CHUNK] if z == 0 else cum[0:1]
                e_neg = jnp.exp(-cum)
                dec = jnp.exp(tot - cum)
                e_tot = jnp.exp(tot)
                kac, kdc = ka_ref[z, rows, :], kd_ref[z, rows, :]
                rt = rc * jnp.exp(cum)
                kt = kkc * jnp.exp(cum - lwc)
                at, kdt, ah, kh = kac * e_neg, kdc * e_neg, kac * dec, kdc * dec
                for j in range(2):
                    sl = slice(j * dh, (j + 1) * dh)
                    inst.append((z, kt[:, sl], rt[:, sl], vc[:, sl], at[:, sl], kdt[:, sl], ah[:, sl], kh[:, sl],
                                 e_tot[:, sl]))
        zs = [i[0] for i in inst]
        kt_h, rt_h, v_h = [i[1] for i in inst], [i[2] for i in inst], [i[3] for i in inst]
        each = lambda f, *xs: [f(*a) for a in zip(*xs)]
        aa = each(lambda i: _mm3(jnp.concatenate([i[1], i[2]], axis=0),
                                 jnp.concatenate([i[4], i[5]], axis=0), nt=True), inst)
        low = each(lambda a, z: a[0:CHUNK, 0:CHUNK] * strict[z], aa, zs)
        a_kk = each(lambda a, z: a[0:CHUNK, CHUNK:] * strict[z], aa, zs)
        a_r = each(lambda a, z: a[CHUNK:, :] * incl2[z], aa, zs)
        akv = each(_mm3, a_kk, v_h)
        ld = each(lambda x: x * diag_blk, low)
        lo = each(lambda x, y: x - y, low, ld)
        l2 = each(_mm3, ld, ld)
        l4 = each(_mm3, l2, l2)
        l8 = each(_mm3, l4, l4)
        td = each(lambda x: eye - x, ld)
        for lp in (l2, l4, l8):
            td = each(lambda t, p: t + _mm3(t, p), td, lp)
        x0 = each(lambda t, a, b, c_: _mm3(t, jnp.concatenate([a, b, c_], axis=1)), td, lo, kt_h, akv)
        wm = each(lambda x: x[:, 0:CHUNK], x0)
        pq0 = each(lambda x: x[:, CHUNK:], x0)
        pq = pq0
        for _ in range(CHUNK // SUB - 1):
            pq = each(lambda p0, w, p: p0 - _mm3(w, p), pq0, wm, pq)
        ryc = each(lambda a, p, v_: _mm3(a, jnp.concatenate(
            [p, jnp.concatenate([zero_blk, -v_], axis=1)], axis=0)), a_r, pq, v_h)
        gh = each(lambda i, p: _mm3(jnp.concatenate([i[6], i[7]], axis=1).T,
                                    jnp.concatenate([p, i[3]], axis=1)), inst, pq)
        pieces = []
        for i, g, ry, rt_ in zip(inst, gh, ryc, rt_h):
            g_t = eye * i[8] - g[0:dh, 0:dh]
            h_t = g[dh:, 2 * dh:] - g[0:dh, dh:2 * dh]
            pieces += [g_t, rt_ - ry[:, 0:dh], h_t, -ry[:, dh:]]
        per_chunk = len(pieces) // P1_CHUNKS
        for cc in range(P1_CHUNKS):
            coef_ref[c2 * P1_CHUNKS + cc] = jnp.concatenate(pieces[cc * per_chunk:(cc + 1) * per_chunk], axis=0)
        return carry

    lax.fori_loop(0, nc // P1_CHUNKS, phase1, 0)

    m_init = tuple((s0_ref[z, j].T if has_s0 else jnp.zeros((dh, dh), F32)) for z in range(2) for j in range(2))

    def phase2(ci, ms):
        new_ms, ys = [], []
        for z in range(2):
            c = ci if z == 0 else nc - 1 - ci
            for j in range(2):
                base = (z * 2 + j) * 4 * dh
                out = (_mm3(coef_ref[c, base:base + 2 * dh, :], ms[z * 2 + j])
                       + coef_ref[c, base + 2 * dh:base + 4 * dh, :])
                new_ms.append(out[0:dh])
                ys.append(out[dh:])
        ysp_ref[ci] = jnp.concatenate(ys, axis=1)
        return tuple(new_ms)

    m_fin = lax.fori_loop(0, nc, phase2, m_init)

    if emit_state:
        for z in range(2):
            for j in range(2):
                so_ref[z, j] = m_fin[z * 2 + j].T

    y = jnp.concatenate([ysp_ref[c, :, 0:pw] + ysp_ref[nc - 1 - c, :, pw:] for c in range(nc)], axis=0)
    mu = _mm_exact_rhs(y, seg_b) * (1.0 / dh)
    yc = y - mu
    var = _mm_exact_rhs(yc * yc, seg_b) * (1.0 / dh)
    yn = yc * lax.rsqrt(var + RW_GN_EPS) * lng_ref[...] + lnb_ref[...]
    v = v_ref[...]
    bonus = _mm_exact_rhs(r * k * rkp_ref[...], seg_b) * v
    g = _mm3(_sigmoid(sm[:, SM_GL:SM_GL + RW_G_RANK]), gup_ref[...])
    y_ref[...] = ((yn + bonus) * g).astype(y_ref.dtype)


def _rwkv(u, p, layer, n_seq, seq, row_off, s0=None, emit_state=False):
    n_pairs = RW_HEADS // 2
    pw = 2 * RW_DH
    bpc = MIX_W // pw
    lw = lambda shape: pl.BlockSpec((None,) + shape, lambda b, hp: (layer,) + (0,) * (len(shape) - 1) + (hp,))
    row = lambda a: a.reshape(DEPTH, 1, MIX_W)
    u_pair = lambda cb: pl.BlockSpec((seq, pw), lambda b, hp: (b + row_off, cb * bpc + hp))
    in_specs = [u_pair(CB_RWR), u_pair(CB_RWK), u_pair(CB_RWV),
                pl.BlockSpec((seq, MIX_W), lambda b, hp: (b + row_off, CB_SMALL)),
                lw((2, pw)), lw((2, RW_W_RANK, pw)), lw((2, pw)), lw((2, RW_A_RANK, pw)),
                lw((RW_G_RANK, pw))] + [lw((1, pw))] * 5
    args = [u, u, u, u, p['rw_w0'], p['rw_w_up'], p['rw_a0'], p['rw_a_up'], p['rw_g_up'],
            row(p['rw_k_k']), row(p['rw_k_a']), row(p['rw_r_k']), row(p['rw_ln_g']), row(p['rw_ln_b'])]
    st_blk = (2, 2, RW_DH, RW_DH)
    if s0 is not None:
        in_specs.append(pl.BlockSpec((None, None) + st_blk, lambda b, hp: (b, layer, 0, hp, 0, 0)))
        args.append(s0)
    out_shape = [jax.ShapeDtypeStruct((n_seq * seq, MIX_W), BF16)]
    out_specs = [pl.BlockSpec((seq, pw), lambda b, hp: (b, hp))]
    if emit_state:
        out_shape.append(jax.ShapeDtypeStruct((n_seq, 2, RW_HEADS, RW_DH, RW_DH), F32))
        out_specs.append(pl.BlockSpec((None,) + st_blk, lambda b, hp: (b, 0, hp, 0, 0)))
    nc = seq // CHUNK
    tok = lambda n: pltpu.VMEM((n, seq, pw) if n else (seq, pw), F32)
    return pl.pallas_call(
        functools.partial(_rwkv_kernel, seq=seq, has_s0=s0 is not None, emit_state=emit_state),
        out_shape=tuple(out_shape), grid=(n_seq, n_pairs), in_specs=in_specs, out_specs=tuple(out_specs),
        scratch_shapes=[tok(0), tok(2), tok(2), tok(2),
                        pltpu.VMEM((nc, 16 * RW_DH, RW_DH), F32), pltpu.VMEM((nc, CHUNK, 2 * pw), F32)],
        compiler_params=_params(("parallel", "parallel")),
        name=f"rwkv_{seq}",
    )(*args)


def _rwkv_kernel_old(*refs, seq, has_s0, emit_state):
    it = iter(refs)
    r_ref, k_ref, v_ref, sm_ref = (next(it) for _ in range(4))
    (w0_ref, wup_ref, a0_ref, aup_ref, gup_ref, kkp_ref, kap_ref, rkp_ref, lng_ref, lnb_ref) = (
        next(it) for _ in range(10))
    s0_ref = next(it) if has_s0 else None
    y_ref = next(it)
    so_ref = next(it) if emit_state else None
    rp_ref, vp_ref, kk_ref, lw_ref, ka_ref, kd_ref, ys_ref, st_ref = (next(it) for _ in range(8))

    n_pairs = RW_HEADS // 2
    pw = 2 * RW_DH
    nc = seq // CHUNK
    seg = _seg_ones(MIX_W, RW_DH)

    r = r_ref[...]
    k = k_ref[...]
    sm = sm_ref[...]
    wl = jnp.tanh(sm[:, SM_WL:SM_WL + RW_W_RANK])
    al = sm[:, SM_AL:SM_AL + RW_A_RANK]
    kk = k * kkp_ref[...]
    kk = kk * lax.rsqrt(_mm(kk * kk, seg, HI) + EPS)
    for hp in range(n_pairs):
        psl = slice(hp * pw, (hp + 1) * pw)
        rp_ref[hp] = r[:, psl]
        vp_ref[hp] = v_ref[:, psl]
        kk_ref[hp] = kk[:, psl]
    for z in range(2):
        w_pre = w0_ref[z:z + 1, :] + _mm(wl, wup_ref[z], HI)
        lw = -RW_DECAY_SCALE * _sigmoid(w_pre)
        a = _sigmoid(a0_ref[z:z + 1, :] + _mm(al, aup_ref[z], HI))
        ka = kk * a
        kd = k * (1.0 + (a - 1.0) * kap_ref[...])
        for hp in range(n_pairs):
            psl = slice(hp * pw, (hp + 1) * pw)
            lw_ref[z * n_pairs + hp] = lw[:, psl]
            ka_ref[z * n_pairs + hp] = ka[:, psl]
            kd_ref[z * n_pairs + hp] = kd[:, psl]
    for i in range(2 * RW_HEADS):
        st_ref[i] = s0_ref[i // RW_HEADS, i % RW_HEADS] if has_s0 else jnp.zeros((RW_DH, RW_DH), F32)

    incl, strict, ti, si = _tri_masks(CHUNK)
    diag_blk = (ti // SUB == si // SUB).astype(F32)
    eye = (ti == si).astype(F32)

    def chunk_pair(ci, hp):
        for z in range(2):
            c = ci if z == 0 else nc - 1 - ci
            rows = pl.ds(pl.multiple_of(c * CHUNK, CHUNK), CHUNK)
            lwc = lw_ref[z * n_pairs + hp, rows, :]
            cum = _mm(incl[z], lwc, HI)
            tot = cum[CHUNK - 1:CHUNK] if z == 0 else cum[0:1]
            e_neg = jnp.exp(-cum)
            dec = jnp.exp(tot - cum)
            e_tot = jnp.exp(tot)
            kac = ka_ref[z * n_pairs + hp, rows, :]
            kdc = kd_ref[z * n_pairs + hp, rows, :]
            rt = rp_ref[hp, rows, :] * jnp.exp(cum)
            kt = kk_ref[hp, rows, :] * jnp.exp(cum - lwc)
            at, kdt = kac * e_neg, kdc * e_neg
            ah, kh = kac * dec, kdc * dec
            vc = vp_ref[hp, rows, :]
            for j in range(2):
                sl = slice(j * RW_DH, (j + 1) * RW_DH)
                si_ = z * RW_HEADS + hp * 2 + j
                s0 = st_ref[si_]
                kt_h, rt_h, v_h = kt[:, sl], rt[:, sl], vc[:, sl]
                low = _mm_nt(kt_h, at[:, sl], HI) * strict[z]
                a_kk = _mm_nt(kt_h, kdt[:, sl], HI) * strict[z]
                a_ra = _mm_nt(rt_h, at[:, sl], HI) * incl[z]
                a_rk = _mm_nt(rt_h, kdt[:, sl], HI) * incl[z]
                ld = low * diag_blk
                lo = low - ld
                l2 = _mm(ld, ld, HI)
                l4 = _mm(l2, l2, HI)
                l8 = _mm(l4, l4, HI)
                tinv = eye - ld
                tinv = tinv + _mm(tinv, l2, HI)
                tinv = tinv + _mm(tinv, l4, HI)
                tinv = tinv + _mm(tinv, l8, HI)
                rhs = -(_mm_nt(kt_h, s0, HI) + _mm(a_kk, v_h, HI))
                cv = _mm(tinv, rhs, HI)
                wm = _mm(tinv, lo, HI)
                uu = cv
                for _ in range(CHUNK // SUB - 1):
                    uu = cv - _mm(wm, uu, HI)
                y = _mm_nt(rt_h, s0, HI) + _mm(a_ra, uu, HI) + _mm(a_rk, v_h, HI)
                ys_ref[z * n_pairs + hp, rows, sl] = y
                st_ref[si_] = s0 * e_tot[:, sl] + _mm_tn(uu, ah[:, sl], HI) + _mm_tn(v_h, kh[:, sl], HI)

    def chunk_step(ci, carry):
        def pair_step(hp, carry2):
            chunk_pair(ci, hp)
            return carry2
        return lax.fori_loop(0, n_pairs, pair_step, carry)

    lax.fori_loop(0, nc, chunk_step, 0)

    if emit_state:
        for i in range(2 * RW_HEADS):
            so_ref[i // RW_HEADS, i % RW_HEADS] = st_ref[i]

    g = _mm(_sigmoid(sm[:, SM_GL:SM_GL + RW_G_RANK]), gup_ref[...], HI)
    for h in range(RW_HEADS):
        hp, j = divmod(h, 2)
        sl = slice(j * RW_DH, (j + 1) * RW_DH)
        hsl = slice(h * RW_DH, (h + 1) * RW_DH)
        y = ys_ref[hp, :, sl] + ys_ref[n_pairs + hp, :, sl]
        mu = jnp.mean(y, axis=-1, keepdims=True)
        var = jnp.mean(jnp.square(y - mu), axis=-1, keepdims=True)
        yn = (y - mu) * lax.rsqrt(var + RW_GN_EPS) * lng_ref[:, hsl] + lnb_ref[:, hsl]
        r_h, k_h, v_h = rp_ref[hp, :, sl], k_ref[:, hsl], vp_ref[hp, :, sl]
        bonus = jnp.sum(r_h * k_h * rkp_ref[:, hsl], axis=-1, keepdims=True) * v_h
        y_ref[:, hsl] = ((yn + bonus) * g[:, hsl]).astype(y_ref.dtype)


def _rwkv_old(u, p, layer, n_seq, seq, row_off, s0=None, emit_state=False):
    lw = lambda shape: pl.BlockSpec((None,) + shape, lambda b: (layer,) + (0,) * len(shape))
    row = lambda a: a.reshape(DEPTH, 1, MIX_W)
    in_specs = [_u_spec(seq, cb, row_off) for cb in (CB_RWR, CB_RWK, CB_RWV, CB_SMALL)] + [
        lw((2, MIX_W)), lw((2, RW_W_RANK, MIX_W)), lw((2, MIX_W)), lw((2, RW_A_RANK, MIX_W)),
        lw((RW_G_RANK, MIX_W))] + [lw((1, MIX_W))] * 5
    args = [u, u, u, u, p['rw_w0'], p['rw_w_up'], p['rw_a0'], p['rw_a_up'], p['rw_g_up'],
            row(p['rw_k_k']), row(p['rw_k_a']), row(p['rw_r_k']), row(p['rw_ln_g']), row(p['rw_ln_b'])]
    st_shape = (2, RW_HEADS, RW_DH, RW_DH)
    if s0 is not None:
        in_specs.append(pl.BlockSpec((None, None) + st_shape, lambda b: (b, layer, 0, 0, 0, 0)))
        args.append(s0)
    out_shape = [jax.ShapeDtypeStruct((n_seq * seq, MIX_W), BF16)]
    out_specs = [pl.BlockSpec((seq, MIX_W), lambda b: (b, 0))]
    if emit_state:
        out_shape.append(jax.ShapeDtypeStruct((n_seq,) + st_shape, F32))
        out_specs.append(pl.BlockSpec((None,) + st_shape, lambda b: (b, 0, 0, 0, 0)))
    n_pairs = RW_HEADS // 2
    pair = lambda n: pltpu.VMEM((n, seq, 2 * RW_DH), F32)
    return pl.pallas_call(
        functools.partial(_rwkv_kernel, seq=seq, has_s0=s0 is not None, emit_state=emit_state),
        out_shape=tuple(out_shape), grid=(n_seq,), in_specs=in_specs, out_specs=tuple(out_specs),
        scratch_shapes=[pair(n_pairs), pair(n_pairs), pair(n_pairs), pair(2 * n_pairs), pair(2 * n_pairs),
                        pair(2 * n_pairs), pair(2 * n_pairs), pltpu.VMEM((2 * RW_HEADS, RW_DH, RW_DH), F32)],
        compiler_params=_params(("parallel",)),
        name=f"rwkv_{seq}",
    )(*args)


def _log_sigmoid(x):
    return jnp.minimum(x, 0.0) - jnp.log(1.0 + jnp.exp(-jnp.abs(x)))


def _mlstm_kernel(*refs, seq, has_s0, emit_state):
    it = iter(refs)
    q_ref, k_ref, v_ref, o_ref, sm_ref, gr_ref, bc_ref, br_ref, ng_ref = (next(it) for _ in range(9))
    c0_ref, n0_ref, m0_ref = ((next(it), next(it), next(it)) if has_s0 else (None, None, None))
    y_ref = next(it)
    co_ref, no_ref, mo_ref = ((next(it), next(it), next(it)) if emit_state else (None, None, None))
    hs_ref, c_ref, n_ref, m_ref = (next(it) for _ in range(4))

    nc = seq // CHUNK
    n_st = 2 * ML_HEADS
    for i in range(n_st):
        z, h = divmod(i, ML_HEADS)
        c_ref[i] = c0_ref[z, h] if has_s0 else jnp.zeros((ML_DH, ML_DH), F32)
        n_ref[i] = n0_ref[z, h] if has_s0 else jnp.zeros((1, ML_DH), F32)
        m_ref[i] = m0_ref[z, h] if has_s0 else jnp.zeros((1, 1), F32)

    incl, _, ti, si = _tri_masks(CHUNK)
    before = ((si <= ti), (si >= ti))

    nh = ML_HEADS
    chains = [(z, h) for z in range(2) for h in range(nh)]
    each = lambda f, *xs: [f(*a) for a in zip(*xs)]
    zs = [z for z, _ in chains]

    def chunk_step(ci, carry):
        per_dir = []
        for z in range(2):
            c = ci if z == 0 else nc - 1 - ci
            rows = pl.ds(pl.multiple_of(c * CHUNK, CHUNK), CHUNK)
            g0 = SM_GATE + z * 2 * nh
            gc = sm_ref[rows, g0:g0 + 2 * nh] + bc_ref[:, z * 2 * nh:(z + 1) * 2 * nh]
            gr = gr_ref[c, z * 2 * nh:(z + 1) * 2 * nh, :] + br_ref[z * 2 * nh:(z + 1) * 2 * nh, :]
            b_cols = _mm(incl[z], _log_sigmoid(gc[:, nh:]), HI)
            b_rows = _mm(_log_sigmoid(gr[nh:]), incl[1 - z], HI)
            per_dir.append((rows, gc[:, :nh], b_cols, gr[:nh], b_rows))
        rows = [per_dir[z][0] for z, _ in chains]
        hsl = [slice(h * ML_DH, (h + 1) * ML_DH) for _, h in chains]
        i_col = [per_dir[z][1][:, h:h + 1] for z, h in chains]
        b_col = [per_dir[z][2][:, h:h + 1] for z, h in chains]
        i_row = [per_dir[z][3][h:h + 1] for z, h in chains]
        b_row = [per_dir[z][4][h:h + 1] for z, h in chains]
        b_last = each(lambda b, z: b[CHUNK - 1:CHUNK] if z == 0 else b[0:1], b_col, zs)
        m_old = [m_ref[i] for i in range(n_st)]
        cm = [c_ref[i] for i in range(n_st)]
        nv = [n_ref[i] for i in range(n_st)]
        qc = each(lambda r, s: q_ref[r, s] * (ML_DH ** -0.5), rows, hsl)
        kc = each(lambda r, s: k_ref[r, s], rows, hsl)
        vc = each(lambda r, s: v_ref[r, s], rows, hsl)
        qk = each(lambda q, k: _mm_nt(q, k, HI), qc, kc)
        qcm = each(lambda q, c_: _mm_nt(q, c_, HI), qc, cm)
        a_t = each(lambda b, m: b + m, b_col, m_old)
        dmat = each(lambda bc, brw, ir, z: jnp.where(before[z], bc - brw + ir, NEG_INF), b_col, b_row, i_row, zs)
        m_t = each(lambda a, d: jnp.maximum(a, jnp.max(d, axis=-1, keepdims=True)), a_t, dmat)
        s = each(lambda x, d, m: x * jnp.exp(d - m), qk, dmat, m_t)
        inter = each(lambda a, m: jnp.exp(a - m), a_t, m_t)
        sv = each(lambda s_, v: _mm(s_, v, HI), s, vc)
        g_col = each(lambda bl, bc, ic: bl - bc + ic, b_last, b_col, i_col)
        a_l = each(lambda bl, m: bl + m, b_last, m_old)
        m_new = each(lambda a, g: jnp.maximum(a, jnp.max(g, axis=0, keepdims=True)), a_l, g_col)
        wgt = each(lambda g, m: jnp.exp(g - m), g_col, m_new)
        vk = each(lambda v, w, k: _mm_tn(v * w, k, HI), vc, wgt, kc)
        decay = each(lambda a, m: jnp.exp(a - m), a_l, m_new)
        num = each(lambda i_, x, y: i_ * x + y, inter, qcm, sv)
        den = each(lambda i_, q, n_, s_: i_ * jnp.sum(q * n_, axis=-1, keepdims=True)
                   + jnp.sum(s_, axis=-1, keepdims=True), inter, qc, nv, s)
        hh = each(lambda n_, d, m: n_ / jnp.maximum(jnp.abs(d), jnp.exp(-m)), num, den, m_t)
        for z in range(2):
            hs_ref[z, per_dir[z][0], :] = jnp.concatenate(hh[z * nh:(z + 1) * nh], axis=1)
        for i in range(n_st):
            c_ref[i] = decay[i] * cm[i] + vk[i]
            n_ref[i] = decay[i] * nv[i] + jnp.sum(wgt[i] * kc[i], axis=0, keepdims=True)
            m_ref[i] = m_new[i]
        return carry

    lax.fori_loop(0, nc, chunk_step, 0)

    if emit_state:
        for i in range(n_st):
            z, h = divmod(i, ML_HEADS)
            co_ref[z, h] = c_ref[i]
            no_ref[z, h] = n_ref[i]
            mo_ref[z, h] = m_ref[i]

    for h in range(ML_HEADS):
        hsl = slice(h * ML_DH, (h + 1) * ML_DH)
        hn = _head_rms(hs_ref[0, :, hsl] + hs_ref[1, :, hsl], ng_ref[:, hsl])
        y_ref[:, hsl] = (_sigmoid(o_ref[:, hsl]) * hn).astype(y_ref.dtype)


def _mlstm(u, gate_rows, p, layer, n_seq, seq, row_off, state=None, emit_state=False):
    lw = lambda shape: pl.BlockSpec((None,) + shape, lambda b: (layer,) + (0,) * len(shape))
    nc = seq // CHUNK
    n_gate = 4 * ML_HEADS
    in_specs = [_u_spec(seq, cb, row_off) for cb in (CB_MLQ, CB_MLK, CB_MLV, CB_MLO, CB_SMALL)] + [
        pl.BlockSpec((None, nc, n_gate, CHUNK), lambda b: (b, 0, 0, 0)),
        lw((1, n_gate)), lw((n_gate, 1)), lw((1, MIX_W))]
    args = [u, u, u, u, u, gate_rows, p['ml_gate_b'].reshape(DEPTH, 1, n_gate),
            p['ml_gate_b'].reshape(DEPTH, n_gate, 1), p['ml_norm_g'].reshape(DEPTH, 1, MIX_W)]
    c_shape, n_shape, m_shape = (2, ML_HEADS, ML_DH, ML_DH), (2, ML_HEADS, 1, ML_DH), (2, ML_HEADS, 1, 1)
    if state is not None:
        c0, n0, m0 = state
        for a, shp in ((c0, c_shape), (n0, n_shape), (m0, m_shape)):
            in_specs.append(pl.BlockSpec((None, None) + shp, lambda b: (b, layer, 0, 0, 0, 0)))
            args.append(a.reshape(a.shape[:2] + shp))
    out_shape = [jax.ShapeDtypeStruct((n_seq * seq, MIX_W), BF16)]
    out_specs = [pl.BlockSpec((seq, MIX_W), lambda b: (b, 0))]
    if emit_state:
        for shp in (c_shape, n_shape, m_shape):
            out_shape.append(jax.ShapeDtypeStruct((n_seq,) + shp, F32))
            out_specs.append(pl.BlockSpec((None,) + shp, lambda b: (b, 0, 0, 0, 0)))
    n_st = 2 * ML_HEADS
    return pl.pallas_call(
        functools.partial(_mlstm_kernel, seq=seq, has_s0=state is not None, emit_state=emit_state),
        out_shape=tuple(out_shape), grid=(n_seq,), in_specs=in_specs, out_specs=tuple(out_specs),
        scratch_shapes=[pltpu.VMEM((2, seq, MIX_W), F32), pltpu.VMEM((n_st, ML_DH, ML_DH), F32),
                        pltpu.VMEM((n_st, 1, ML_DH), F32), pltpu.VMEM((n_st, 1, 1), F32)],
        compiler_params=_params(("parallel",)),
        name=f"mlstm_{seq}",
    )(*args)


def _outproj_kernel(ya_ref, yb_ref, yc_ref, yd_ref, w_ref, x_ref, g_ref, o_ref):
    acc = _mm(ya_ref[...], w_ref[0:MIX_W, :])
    for i, y_ref in enumerate((yb_ref, yc_ref, yd_ref), start=1):
        acc += _mm(y_ref[...], w_ref[i * MIX_W:(i + 1) * MIX_W, :])
    o_ref[...] = x_ref[...] + g_ref[...] * acc


def _outproj(ys, w_out_b, x, mod, layer):
    tm, tn = 1024, 1024
    y_spec = pl.BlockSpec((tm, MIX_W), lambda i, j: (i, 0))
    return pl.pallas_call(
        _outproj_kernel,
        out_shape=jax.ShapeDtypeStruct((N_TOK, D_MODEL), F32),
        grid=(N_TOK // tm, D_MODEL // tn),
        in_specs=[y_spec] * 4 + [pl.BlockSpec((None, D_MODEL, tn), lambda i, j: (layer, 0, j)),
                                 pl.BlockSpec((tm, tn), lambda i, j: (i, j)),
                                 _mod_spec(layer, 2, tm, tn, with_j=True)],
        out_specs=pl.BlockSpec((tm, tn), lambda i, j: (i, j)),
        compiler_params=_params(("parallel", "parallel")),
        name="outproj",
    )(*ys, w_out_b, x, mod)


def _router_kernel(x_ref, g_ref, sh_ref, sc_ref, rw_ref, rb_ref, xm_ref, gt_ref, ids_ref):
    xm = _modulated(x_ref[...], g_ref[...], sh_ref[...], sc_ref[...])
    xm_ref[...] = xm
    logits = _mm_nt(rw_ref[...], xm, HI)
    ex = jnp.exp(logits - jnp.max(logits, axis=0, keepdims=True))
    scores = ex / jnp.sum(ex, axis=0, keepdims=True)
    sel = scores + rb_ref[...]
    per = N_EXPERTS // N_EXPERT_GROUPS
    s = [sel[e:e + 1, :] for e in range(N_EXPERTS)]
    grp_score = []
    for g in range(N_EXPERT_GROUPS):
        a, b, c, d = s[per * g:per * (g + 1)]
        hi1, lo1, hi2, lo2 = jnp.maximum(a, b), jnp.minimum(a, b), jnp.maximum(c, d), jnp.minimum(c, d)
        grp_score.append(jnp.maximum(hi1, hi2) + jnp.maximum(jnp.minimum(hi1, hi2), jnp.maximum(lo1, lo2)))
    best = functools.reduce(jnp.maximum, grp_score)
    in_grp, taken = [], jnp.zeros_like(best)
    for g in range(N_EXPERT_GROUPS):
        hit = jnp.where(grp_score[g] == best, 1.0, 0.0) * (1.0 - taken)
        in_grp.append(hit)
        taken = taken + hit
    picked, flag = [], []
    for e in range(N_EXPERTS):
        g = e // per
        rank = jnp.zeros_like(best)
        for o in range(per * g, per * (g + 1)):
            if o < e:
                rank += jnp.where(s[o] >= s[e], 1.0, 0.0)
            elif o > e:
                rank += jnp.where(s[o] > s[e], 1.0, 0.0)
        flag.append(in_grp[g] * jnp.where(rank < 2.0, 1.0, 0.0))
        picked.append(flag[e] * scores[e:e + 1, :])
    total = functools.reduce(lambda x, y: x + y, picked)
    for e in range(N_EXPERTS):
        gt_ref[e:e + 1, :] = picked[e] / total
    lo_id = functools.reduce(jnp.minimum, [jnp.where(flag[e] > 0.0, float(e), float(N_EXPERTS)) for e in range(N_EXPERTS)])
    hi_id = functools.reduce(jnp.maximum, [jnp.where(flag[e] > 0.0, float(e), -1.0) for e in range(N_EXPERTS)])
    ids_ref[0:1, :] = lo_id
    ids_ref[1:2, :] = hi_id
    ids_ref[2:, :] = jnp.zeros((ids_ref.shape[0] - 2,) + lo_id.shape[1:], F32)


def _router(x, norm_g, mod, router_wt, router_b, layer):
    tm = 512
    return pl.pallas_call(
        _router_kernel,
        out_shape=(jax.ShapeDtypeStruct((N_TOK, D_MODEL), F32), jax.ShapeDtypeStruct((N_EXPERTS, N_TOK), F32),
                   jax.ShapeDtypeStruct((8, N_TOK), F32)),
        grid=(N_TOK // tm,),
        in_specs=[pl.BlockSpec((tm, D_MODEL), lambda i: (i, 0)),
                  pl.BlockSpec((None, 1, D_MODEL), lambda i: (layer, 0, 0)),
                  _mod_spec(layer, 3, tm), _mod_spec(layer, 4, tm),
                  _full((N_EXPERTS, D_MODEL)), _full((N_EXPERTS, 1))],
        out_specs=(pl.BlockSpec((tm, D_MODEL), lambda i: (i, 0)), pl.BlockSpec((N_EXPERTS, tm), lambda i: (0, i)),
                   pl.BlockSpec((8, tm), lambda i: (0, i))),
        compiler_params=_params(("parallel",)),
        name="router",
    )(x, norm_g.reshape(DEPTH, 1, D_MODEL), mod, mod, router_wt, router_b.reshape(N_EXPERTS, 1))


def _experts_kernel(xm_ref, gates_ref, w1_ref, w3_ref, w2_ref, x_ref, g_ref, o_ref, acc_ref):
    e = pl.program_id(1)

    @pl.when(e == 0)
    def _():
        acc_ref[...] = jnp.zeros_like(acc_ref)

    xm = xm_ref[...]
    h1 = _mm(xm, w1_ref[...])
    h3 = _mm(xm, w3_ref[...])
    gates = gates_ref[...]
    lane = lax.broadcasted_iota(jnp.int32, gates.shape, 1)
    gate = jnp.sum(jnp.where(lane == e, gates, 0.0), axis=-1, keepdims=True)
    hh = h1 * _sigmoid(h1) * h3 * gate
    acc_ref[...] += _mm(hh.astype(BF16), w2_ref[...])

    @pl.when(e == N_EXPERTS - 1)
    def _():
        o_ref[...] = x_ref[...] + g_ref[...] * acc_ref[...]


def _experts(xm, gates, w1_b, w3_b, w2_b, x, mod, layer):
    tm = 512
    row = pl.BlockSpec((tm, D_MODEL), lambda i, e: (i, 0))
    return pl.pallas_call(
        _experts_kernel,
        out_shape=jax.ShapeDtypeStruct((N_TOK, D_MODEL), F32),
        grid=(N_TOK // tm, N_EXPERTS),
        in_specs=[row, pl.BlockSpec((tm, N_EXPERTS), lambda i, e: (i, 0)),
                  pl.BlockSpec((None, None, D_MODEL, D_EXPERT), lambda i, e: (layer, e, 0, 0)),
                  pl.BlockSpec((None, None, D_MODEL, D_EXPERT), lambda i, e: (layer, e, 0, 0)),
                  pl.BlockSpec((None, None, D_EXPERT, D_MODEL), lambda i, e: (layer, e, 0, 0)),
                  row, _mod_spec(layer, 5, tm)],
        out_specs=row,
        scratch_shapes=[pltpu.VMEM((tm, D_MODEL), F32)],
        compiler_params=_params(("parallel", "arbitrary")),
        name="experts",
    )(xm, gates, w1_b, w3_b, w2_b, x, mod)


MOE_TM = 512
N_PAIRS = N_EXPERT_GROUPS * 6
MOE_STEPS = 2 * (N_PAIRS + N_TOK // MOE_TM - 1)


def _gather_rows(src_hbm, dst_ref, idx_ref, base, n, sem):
    def issue(r, carry):
        pltpu.make_async_copy(src_hbm.at[pl.ds(idx_ref[base + r], 1), :], dst_ref.at[pl.ds(r, 1), :], sem).start()
        return carry

    def wait(r, carry):
        pltpu.make_async_copy(src_hbm.at[pl.ds(0, 1), :], dst_ref.at[pl.ds(r, 1), :], sem).wait()
        return carry

    lax.fori_loop(0, n, issue, 0, unroll=8)
    lax.fori_loop(0, n, wait, 0, unroll=8)


def _moe_plan(ids):
    i32 = jnp.int32
    lo, hi = ids[0].astype(i32), ids[1].astype(i32)
    src = jnp.argsort(lo * N_EXPERTS + hi).astype(i32)
    pos = jnp.zeros((N_TOK,), i32).at[src].set(jnp.arange(N_TOK, dtype=i32))
    n_tiles = N_TOK // MOE_TM
    ex = jnp.arange(N_EXPERTS, dtype=i32)
    lo_s, hi_s = lo[src].reshape(n_tiles, MOE_TM, 1), hi[src].reshape(n_tiles, MOE_TM, 1)
    used = ((lo_s == ex).any(axis=1) | (hi_s == ex).any(axis=1)).reshape(-1)
    n_valid = jnp.sum(used).astype(i32)
    idx = jnp.nonzero(used, size=MOE_STEPS, fill_value=0)[0].astype(i32)
    valid = jnp.arange(MOE_STEPS, dtype=i32) < n_valid
    idx = jnp.where(valid, idx, idx[jnp.maximum(n_valid - 1, 0)])
    tile, exp = idx // N_EXPERTS, idx % N_EXPERTS
    first = valid & (tile != jnp.concatenate([jnp.full((1,), -1, i32), tile[:-1]]))
    return src, pos, tile, exp, first.astype(i32), valid.astype(i32)


def _moe_kernel(tile_ref, exp_ref, first_ref, valid_ref, src_ref, xm_hbm, gates_ref, w1_ref, w3_ref, w2_ref,
                o_ref, xs_ref, xb_ref, sem):
    s = pl.program_id(0)

    @pl.when(first_ref[s] == 1)
    def _():
        _gather_rows(xm_hbm, xs_ref, src_ref, tile_ref[s] * MOE_TM, MOE_TM, sem)
        xb_ref[...] = xs_ref[...].astype(BF16)
        o_ref[...] = jnp.zeros_like(o_ref)

    @pl.when(valid_ref[s] == 1)
    def _():
        xb = xb_ref[...]
        h1 = _mm(xb, w1_ref[...])
        h3 = _mm(xb, w3_ref[...])
        gates = gates_ref[...]
        lane = lax.broadcasted_iota(jnp.int32, gates.shape, 1)
        gate = jnp.sum(jnp.where(lane == exp_ref[s], gates, 0.0), axis=-1, keepdims=True)
        hh = h1 * _sigmoid(h1) * h3 * gate
        o_ref[...] += _mm(hh.astype(BF16), w2_ref[...])


def _moe(xm, gates_sorted, plan, w1_b, w3_b, w2_b, layer):
    src, _, tile, exp, first, valid = plan
    w_in = pl.BlockSpec((None, None, D_MODEL, D_EXPERT), lambda s, t, e, f, v, i: (layer, e[s], 0, 0))
    grid_spec = pltpu.PrefetchScalarGridSpec(
        num_scalar_prefetch=5, grid=(MOE_STEPS,),
        in_specs=[pl.BlockSpec(memory_space=pl.ANY),
                  pl.BlockSpec((MOE_TM, N_EXPERTS), lambda s, t, e, f, v, i: (t[s], 0)),
                  w_in, w_in,
                  pl.BlockSpec((None, None, D_EXPERT, D_MODEL), lambda s, t, e, f, v, i: (layer, e[s], 0, 0))],
        out_specs=pl.BlockSpec((MOE_TM, D_MODEL), lambda s, t, e, f, v, i: (t[s], 0)),
        scratch_shapes=[pltpu.VMEM((MOE_TM, D_MODEL), F32), pltpu.VMEM((MOE_TM, D_MODEL), BF16),
                        pltpu.SemaphoreType.DMA(())])
    return pl.pallas_call(
        _moe_kernel, out_shape=jax.ShapeDtypeStruct((N_TOK, D_MODEL), F32), grid_spec=grid_spec,
        compiler_params=_params(("arbitrary",)), name="moe",
    )(tile, exp, first, valid, src, xm, gates_sorted, w1_b, w3_b, w2_b)


def _combine_kernel(pos_ref, acc_hbm, x_ref, g_ref, o_ref, buf_ref, sem):
    _gather_rows(acc_hbm, buf_ref, pos_ref, pl.program_id(0) * MOE_TM, MOE_TM, sem)
    o_ref[...] = x_ref[...] + g_ref[...] * buf_ref[...]


def _combine(acc_sorted, pos, x, mod, layer):
    tm = MOE_TM
    row = pl.BlockSpec((tm, D_MODEL), lambda i, p: (i, 0))
    grid_spec = pltpu.PrefetchScalarGridSpec(
        num_scalar_prefetch=1, grid=(N_TOK // tm,),
        in_specs=[pl.BlockSpec(memory_space=pl.ANY), row,
                  pl.BlockSpec((None, None, None, 1, D_MODEL), lambda i, p: (layer, _mod_row(i, tm), 5, 0, 0))],
        out_specs=row,
        scratch_shapes=[pltpu.VMEM((tm, D_MODEL), F32), pltpu.SemaphoreType.DMA(())])
    return pl.pallas_call(
        _combine_kernel, out_shape=jax.ShapeDtypeStruct((N_TOK, D_MODEL), F32), grid_spec=grid_spec,
        compiler_params=_params(("arbitrary",)), name="combine",
    )(pos, acc_sorted, x, mod)


def _gate_rows(gcols, n_seq, seq):
    return gcols.reshape(n_seq, seq // CHUNK, CHUNK, gcols.shape[-1]).transpose(0, 1, 3, 2)


def kernel(x_prompt, x_sample, cache_na_k, cache_na_v, state_rwkv, state_mlstm_c, state_mlstm_n, state_mlstm_m,
           c, c_ctx, norm1_g, norm2_g, w_mod, b_mod, w_in, conv_w, na_q_g, na_k_g, na_rpb, rw_w0, rw_w_up, rw_a0,
           rw_a_up, rw_g_up, rw_k_k, rw_k_a, rw_r_k, rw_ln_g, rw_ln_b, ml_gate_b, ml_norm_g, w_out, router_w,
           router_b, moe_w1, moe_w3, moe_w2):
    p = dict(rw_w0=rw_w0, rw_w_up=rw_w_up, rw_a0=rw_a0, rw_a_up=rw_a_up, rw_g_up=rw_g_up, rw_k_k=rw_k_k,
             rw_k_a=rw_k_a, rw_r_k=rw_r_k, rw_ln_g=rw_ln_g, rw_ln_b=rw_ln_b, ml_gate_b=ml_gate_b,
             ml_norm_g=ml_norm_g)
    cvecs = jnp.concatenate([c_ctx[None], c, jnp.zeros((MOD_ROWS - 1 - DEC_BATCH, D_MODEL), F32)], axis=0)
    mod = _adaln(cvecs, w_mod, b_mod).reshape(DEPTH, MOD_ROWS, 6, 1, D_MODEL)

    n_wide_a = 9 * MIX_W
    n_narrow = RW_W_RANK + RW_A_RANK + RW_G_RANK
    n_wide_b = 4 * MIX_W
    w_in_b = jnp.concatenate(
        [w_in[..., :n_wide_a], w_in[..., n_wide_a + n_narrow:n_wide_a + n_narrow + n_wide_b],
         w_in[..., n_wide_a:n_wide_a + n_narrow], w_in[..., n_wide_a + n_narrow + n_wide_b:],
         jnp.zeros((DEPTH, D_MODEL, P_PAD - w_in.shape[-1]), F32)], axis=-1).astype(BF16)
    w_out_b = w_out.astype(BF16)
    w1_b, w3_b, w2_b = moe_w1.astype(BF16), moe_w3.astype(BF16), moe_w2.astype(BF16)
    tw = _na_tables(na_rpb)
    router_wt = router_w.T
    sample_row_off = N_PROMPT // DEC_SEQ

    x = jnp.concatenate([x_prompt.reshape(N_PROMPT, D_MODEL), x_sample.reshape(N_SAMPLE, D_MODEL)], axis=0)
    new_k, new_v, new_rw, new_c, new_n, new_m = [], [], [], [], [], []
    for l in range(DEPTH):
        u = _inproj(x, norm1_g, mod, w_in_b, l)
        ya_p, yb_p, nk, nv = _attn_prompt(u, conv_w, na_q_g, na_k_g, l)
        ya_s, yb_s = _na_sample(u, cache_na_k, cache_na_v, tw, conv_w, na_q_g, na_k_g, l)
        yc_p, st = _rwkv(u, p, l, BATCH, SEQ, 0, emit_state=True)
        (yc_s,) = _rwkv(u, p, l, DEC_BATCH, DEC_SEQ, sample_row_off, s0=state_rwkv)
        g0 = CB_SMALL * MIX_W + SM_GATE
        gcols = u[:, g0:g0 + 4 * ML_HEADS]
        yd_p, cm, nm, mm = _mlstm(u, _gate_rows(gcols[:N_PROMPT], BATCH, SEQ), p, l, BATCH, SEQ, 0,
                                  emit_state=True)
        (yd_s,) = _mlstm(u, _gate_rows(gcols[N_PROMPT:], DEC_BATCH, DEC_SEQ), p, l, DEC_BATCH, DEC_SEQ,
                         sample_row_off, state=(state_mlstm_c, state_mlstm_n, state_mlstm_m))
        ys = [jnp.concatenate(pair, axis=0) for pair in ((ya_p, ya_s), (yb_p, yb_s), (yc_p, yc_s), (yd_p, yd_s))]
        x = _outproj(ys, w_out_b, x, mod, l)
        xm, gates_t, ids = _router(x, norm2_g, mod, router_wt, router_b, l)
        plan = _moe_plan(ids)
        acc_sorted = _moe(xm, gates_t.T[plan[0]], plan, w1_b, w3_b, w2_b, l)
        x = _combine(acc_sorted, plan[1], x, mod, l)
        new_k.append(nk)
        new_v.append(nv)
        new_rw.append(st)
        new_c.append(cm)
        new_n.append(nm.reshape(BATCH, 2, ML_HEADS, ML_DH))
        new_m.append(mm.reshape(BATCH, 2, ML_HEADS))
    stack = lambda xs: jnp.stack(xs, axis=1)
    return (x[:N_PROMPT].reshape(BATCH, SEQ, D_MODEL), x[N_PROMPT:].reshape(DEC_BATCH, DEC_SEQ, D_MODEL),
            stack(new_k), stack(new_v), stack(new_rw), stack(new_c), stack(new_n), stack(new_m))
```

```python
import functools

import numpy as np
import jax
import jax.numpy as jnp
from jax import lax
from jax.experimental import pallas as pl
from jax.experimental.pallas import tpu as pltpu

F32 = jnp.float32
BF16 = jnp.bfloat16
HI = lax.Precision.HIGHEST

D_MODEL = 2048
BATCH = 16
SEQ = 256
DEPTH = 2
DEC_BATCH = 2
DEC_SEQ = 1024
PAST_LEN = 512
GRID_W = 64
MIX_W = D_MODEL // 4
CONV_K = 3
NA_DH = 64
NA_HEADS = MIX_W // NA_DH
NA_WIN_R = 8
NA_WIN_C = 16
NA_SCALE = NA_DH ** -0.5
ROPE_THETA = 10000.0
RW_DH = 64
RW_HEADS = MIX_W // RW_DH
RW_W_RANK = 64
RW_A_RANK = 64
RW_G_RANK = 128
RW_DECAY_SCALE = 0.606531
RW_GN_EPS = 64e-5
ML_DH = 128
ML_HEADS = MIX_W // ML_DH
N_EXPERTS = 16
N_EXPERT_GROUPS = 4
D_EXPERT = 512
EPS = 1e-6
NEG_INF = -1e30

N_PROMPT = BATCH * SEQ
N_SAMPLE = DEC_BATCH * DEC_SEQ
N_TOK = N_PROMPT + N_SAMPLE
MOD_ROWS = 8
CHUNK = 64
SUB = 16
P1_CHUNKS = 4
P_BLOCKS = 14
P_PAD = P_BLOCKS * MIX_W
(CB_CVB, CB_CVC, CB_CVH, CB_NAQ, CB_NAK, CB_NAV, CB_RWR, CB_RWK, CB_RWV,
 CB_MLQ, CB_MLK, CB_MLV, CB_MLO, CB_SMALL) = range(P_BLOCKS)
SM_WL, SM_AL, SM_GL, SM_GATE = 0, 64, 128, 256
VMEM_LIMIT = 56 * 1024 * 1024


def _mm(a, b, prec=None):
    return jnp.dot(a, b, precision=prec, preferred_element_type=F32)


def _mm_nt(a, b, prec=None):
    return lax.dot_general(a, b, (((1,), (1,)), ((), ())), precision=prec, preferred_element_type=F32)


def _mm_tn(a, b, prec=None):
    return _mm(a.T, b, prec)


def _sigmoid(x):
    return 1.0 / (1.0 + jnp.exp(-x))


def _full(shape):
    n = len(shape)
    return pl.BlockSpec(shape, lambda *_: (0,) * n)


def _params(sem):
    return pltpu.CompilerParams(dimension_semantics=sem, vmem_limit_bytes=VMEM_LIMIT)


def _mod_row(i, tm):
    n_prompt_tiles = N_PROMPT // tm
    tiles_per_sample = DEC_SEQ // tm
    return jnp.where(i < n_prompt_tiles, 0, 1 + (i - n_prompt_tiles) // tiles_per_sample)


def _mod_spec(layer, chunk, tm, tn=D_MODEL, with_j=False):
    if with_j:
        return pl.BlockSpec((None, None, None, 1, tn), lambda i, j: (layer, _mod_row(i, tm), chunk, 0, j))
    return pl.BlockSpec((None, None, None, 1, tn), lambda i, *_: (layer, _mod_row(i, tm), chunk, 0, 0))


def _tri_masks(n):
    t = lax.broadcasted_iota(jnp.int32, (n, n), 0)
    s = lax.broadcasted_iota(jnp.int32, (n, n), 1)
    incl = ((s <= t).astype(F32), (s >= t).astype(F32))
    strict = ((s < t).astype(F32), (s > t).astype(F32))
    return incl, strict, t, s


def _adaln_kernel(cv_ref, w_ref, b_ref, o_ref):
    cv = cv_ref[...]
    o_ref[...] = _mm(cv * _sigmoid(cv), w_ref[...], HI) + b_ref[...]


def _adaln(cvecs, w_mod, b_mod):
    tn = 1024
    n_out = 6 * D_MODEL
    return pl.pallas_call(
        _adaln_kernel,
        out_shape=jax.ShapeDtypeStruct((DEPTH, MOD_ROWS, n_out), F32),
        grid=(DEPTH, n_out // tn),
        in_specs=[_full((MOD_ROWS, D_MODEL)),
                  pl.BlockSpec((None, D_MODEL, tn), lambda l, j: (l, 0, j)),
                  pl.BlockSpec((None, 1, tn), lambda l, j: (l, 0, j))],
        out_specs=pl.BlockSpec((None, MOD_ROWS, tn), lambda l, j: (l, 0, j)),
        compiler_params=_params(("parallel", "parallel")),
        name="adaln",
    )(cvecs, w_mod, b_mod.reshape(DEPTH, 1, n_out))


def _modulated(x, g, sh, sc):
    y = x * lax.rsqrt(jnp.mean(x * x, axis=-1, keepdims=True) + EPS) * g
    return y * (1.0 + sc) + sh


def _inproj_kernel(x_ref, g_ref, sh_ref, sc_ref, w_ref, o_ref, xm_ref):
    @pl.when(pl.program_id(1) == 0)
    def _():
        xm_ref[...] = _modulated(x_ref[...], g_ref[...], sh_ref[...], sc_ref[...]).astype(BF16)

    w = w_ref[...]
    n_cols = jnp.where(pl.program_id(1) == P_PAD // INPROJ_TN - 1, P_IN % INPROJ_TN, INPROJ_TN)
    w = jnp.where(lax.broadcasted_iota(jnp.int32, w.shape, 1) < n_cols, w, 0.0)
    o_ref[...] = _mm(xm_ref[...], w.astype(BF16))


INPROJ_TN = 256
P_IN = 13 * MIX_W + RW_W_RANK + RW_A_RANK + RW_G_RANK + 4 * ML_HEADS


def _inproj_src_block(j):
    a = 9 * MIX_W // INPROJ_TN
    n = (RW_W_RANK + RW_A_RANK + RW_G_RANK) // INPROJ_TN
    b = 4 * MIX_W // INPROJ_TN
    return jnp.where(j < a, j, jnp.where(j < a + b, j + n, jnp.where(j < a + b + n, j - b, j)))


def _inproj(x, norm_g, mod, w_in, layer):
    tm, tn = 1024, INPROJ_TN
    return pl.pallas_call(
        _inproj_kernel,
        out_shape=jax.ShapeDtypeStruct((N_TOK, P_PAD), F32),
        grid=(N_TOK // tm, P_PAD // tn),
        in_specs=[pl.BlockSpec((tm, D_MODEL), lambda i, j: (i, 0)),
                  pl.BlockSpec((None, 1, D_MODEL), lambda i, j: (layer, 0, 0)),
                  _mod_spec(layer, 0, tm), _mod_spec(layer, 1, tm),
                  pl.BlockSpec((None, D_MODEL, tn), lambda i, j: (layer, 0, _inproj_src_block(j)))],
        out_specs=pl.BlockSpec((tm, tn), lambda i, j: (i, j)),
        scratch_shapes=[pltpu.VMEM((tm, D_MODEL), BF16)],
        compiler_params=_params(("parallel", "arbitrary")),
        name="inproj",
    )(x, norm_g.reshape(DEPTH, 1, D_MODEL), mod, mod, w_in)


def _conv_mix(b, c, h, w):
    u = c * h
    n = u.shape[0]
    row = lax.broadcasted_iota(jnp.int32, u.shape, 0)
    prev = jnp.where(row == 0, 0.0, pltpu.roll(u, 1, axis=0))
    nxt = jnp.where(row == n - 1, 0.0, pltpu.roll(u, n - 1, axis=0))
    return b * (prev * w[0:1] + u * w[1:2] + nxt * w[2:3])


def _head_rms(x, g):
    return x * lax.rsqrt(jnp.mean(x * x, axis=-1, keepdims=True) + EPS) * g


def _attn_prompt_kernel(cb_ref, cc_ref, ch_ref, q_ref, k_ref, v_ref, cw_ref, qg_ref, kg_ref,
                        ya_ref, yb_ref, nk_ref, nv_ref):
    ya_ref[...] = _conv_mix(cb_ref[...], cc_ref[...], ch_ref[...], cw_ref[...]).astype(ya_ref.dtype)
    for h in range(NA_HEADS):
        sl = slice(h * NA_DH, (h + 1) * NA_DH)
        qn = _head_rms(q_ref[:, sl], qg_ref[...]) * NA_SCALE
        kn = _head_rms(k_ref[:, sl], kg_ref[...])
        vh = v_ref[:, sl]
        s = _mm_nt(qn.astype(BF16), kn.astype(BF16))
        p = jnp.exp(s - jnp.max(s, axis=-1, keepdims=True))
        o = _mm(p.astype(BF16), vh.astype(BF16)) / jnp.sum(p, axis=-1, keepdims=True)
        yb_ref[:, sl] = o.astype(yb_ref.dtype)
        nk_ref[h] = kn
        nv_ref[h] = vh


def _u_spec(rows, col_block, row_off_blocks=0):
    return pl.BlockSpec((rows, MIX_W), lambda b: (b + row_off_blocks, col_block))


def _attn_prompt(u, conv_w, q_g, k_g, layer, n_seq=BATCH, seq=SEQ):
    lw = lambda shape: pl.BlockSpec((None,) + shape, lambda b: (layer,) + (0,) * len(shape))
    y_spec = pl.BlockSpec((seq, MIX_W), lambda b: (b, 0))
    kv_spec = pl.BlockSpec((None, NA_HEADS, seq, NA_DH), lambda b: (b, 0, 0, 0))
    return pl.pallas_call(
        _attn_prompt_kernel,
        out_shape=(jax.ShapeDtypeStruct((n_seq * seq, MIX_W), BF16),) * 2
        + (jax.ShapeDtypeStruct((n_seq, NA_HEADS, seq, NA_DH), F32),) * 2,
        grid=(n_seq,),
        in_specs=[_u_spec(seq, cb) for cb in (CB_CVB, CB_CVC, CB_CVH, CB_NAQ, CB_NAK, CB_NAV)]
        + [lw((CONV_K, MIX_W)), lw((1, NA_DH)), lw((1, NA_DH))],
        out_specs=(y_spec, y_spec, kv_spec, kv_spec),
        compiler_params=_params(("parallel",)),
        name="attn_prompt",
    )(u, u, u, u, u, u, conv_w, q_g.reshape(DEPTH, 1, NA_DH), k_g.reshape(DEPTH, 1, NA_DH))


def _na_kernel(cb_ref, cc_ref, ch_ref, q_ref, k_ref, v_ref, kc_ref, vc_ref, tw_ref, cos_ref, sin_ref,
               perm_ref, cw_ref, qg_ref, kg_ref, ya_ref, yb_ref, qs_ref, ks_ref, tws_ref):
    rows = DEC_SEQ // GRID_W
    wr = min(NA_WIN_R, rows)
    nw = wr * GRID_W
    ya_ref[...] = _conv_mix(cb_ref[...], cc_ref[...], ch_ref[...], cw_ref[...]).astype(ya_ref.dtype)
    cos, sin, perm = cos_ref[...], sin_ref[...], perm_ref[...]

    def rope(x):
        return x * cos + _mm(x, perm, HI) * sin

    for h in range(NA_HEADS):
        sl = slice(h * NA_DH, (h + 1) * NA_DH)
        qs_ref[...] = (rope(_head_rms(q_ref[:, sl], qg_ref[...])) * NA_SCALE).astype(BF16)
        ks_ref[...] = rope(_head_rms(k_ref[:, sl], kg_ref[...])).astype(BF16)
        kch = kc_ref[h].astype(BF16)
        vch = vc_ref[h].astype(BF16)
        for p in range(wr):
            tws_ref[p] = jnp.concatenate([tw_ref[h, j - p + NA_WIN_R - 1] for j in range(wr)], axis=1)

        def row_step(r, carry):
            rs = jnp.clip(r - wr // 2, 0, rows - wr)
            q0 = pl.multiple_of(r * GRID_W, GRID_W)
            k0 = pl.multiple_of(rs * GRID_W, GRID_W)
            q_r = qs_ref[pl.ds(q0, GRID_W), :]
            s_w = _mm_nt(q_r, ks_ref[pl.ds(k0, nw), :]) + tws_ref[r - rs]
            s_c = _mm_nt(q_r, kch)
            m = jnp.maximum(jnp.max(s_w, axis=-1, keepdims=True), jnp.max(s_c, axis=-1, keepdims=True))
            p_w = jnp.exp(s_w - m)
            p_c = jnp.exp(s_c - m)
            den = jnp.sum(p_w, axis=-1, keepdims=True) + jnp.sum(p_c, axis=-1, keepdims=True)
            v_w = v_ref[pl.ds(k0, nw), sl].astype(BF16)
            o = (_mm(p_w.astype(BF16), v_w) + _mm(p_c.astype(BF16), vch)) / den
            yb_ref[pl.ds(q0, GRID_W), sl] = o.astype(yb_ref.dtype)
            return carry

        lax.fori_loop(0, rows, row_step, 0)


def _na_tables(rpb):
    rows = DEC_SEQ // GRID_W
    wr = min(NA_WIN_R, rows)
    qc = np.arange(GRID_W)
    kc = np.arange(GRID_W)
    wstart = np.clip(qc - NA_WIN_C // 2, 0, GRID_W - NA_WIN_C)
    colmask = (kc[None, :] >= wstart[:, None]) & (kc[None, :] < wstart[:, None] + NA_WIN_C)
    dc = np.clip(kc[None, :] - qc[:, None], -(NA_WIN_C - 1), NA_WIN_C - 1) + NA_WIN_C - 1
    bias = rpb[:, :, :, dc]
    return jnp.where(colmask[None, None, None], bias, NEG_INF)


def _rope_tables():
    t = np.arange(DEC_SEQ)
    quarter = NA_DH // 4
    freq = ROPE_THETA ** (-np.arange(quarter, dtype=np.float32) / quarter)
    ang_r = (t // GRID_W).astype(np.float32)[:, None] * freq
    ang_c = (t % GRID_W).astype(np.float32)[:, None] * freq
    cos = np.concatenate([np.cos(ang_r), np.cos(ang_r), np.cos(ang_c), np.cos(ang_c)], axis=-1)
    sin = np.concatenate([-np.sin(ang_r), np.sin(ang_r), -np.sin(ang_c), np.sin(ang_c)], axis=-1)
    src = np.concatenate([np.arange(quarter) + quarter, np.arange(quarter),
                          np.arange(quarter) + 3 * quarter, np.arange(quarter) + 2 * quarter])
    perm = np.zeros((NA_DH, NA_DH), np.float32)
    perm[src, np.arange(NA_DH)] = 1.0
    return cos.astype(np.float32), sin.astype(np.float32), perm


def _na_sample(u, cache_k, cache_v, tw, conv_w, q_g, k_g, layer, n_seq=DEC_BATCH, row_off=N_PROMPT // DEC_SEQ):
    cos, sin, perm = _rope_tables()
    lw = lambda shape: pl.BlockSpec((None,) + shape, lambda b: (layer,) + (0,) * len(shape))
    y_spec = pl.BlockSpec((DEC_SEQ, MIX_W), lambda b: (b, 0))
    c_spec = pl.BlockSpec((None, None, NA_HEADS, PAST_LEN, NA_DH), lambda b: (b, layer, 0, 0, 0))
    wr = min(NA_WIN_R, DEC_SEQ // GRID_W)
    n_off = 2 * NA_WIN_R - 1
    return pl.pallas_call(
        _na_kernel,
        out_shape=(jax.ShapeDtypeStruct((n_seq * DEC_SEQ, MIX_W), BF16),) * 2,
        grid=(n_seq,),
        in_specs=[_u_spec(DEC_SEQ, cb, row_off) for cb in (CB_CVB, CB_CVC, CB_CVH, CB_NAQ, CB_NAK, CB_NAV)]
        + [c_spec, c_spec, lw((NA_HEADS, n_off, GRID_W, GRID_W)),
           _full((DEC_SEQ, NA_DH)), _full((DEC_SEQ, NA_DH)), _full((NA_DH, NA_DH)),
           lw((CONV_K, MIX_W)), lw((1, NA_DH)), lw((1, NA_DH))],
        out_specs=(y_spec, y_spec),
        scratch_shapes=[pltpu.VMEM((DEC_SEQ, NA_DH), BF16), pltpu.VMEM((DEC_SEQ, NA_DH), BF16),
                        pltpu.VMEM((wr, GRID_W, wr * GRID_W), F32)],
        compiler_params=_params(("parallel",)),
        name="na_sample",
    )(u, u, u, u, u, u, cache_k, cache_v, tw, jnp.asarray(cos), jnp.asarray(sin), jnp.asarray(perm),
      conv_w, q_g.reshape(DEPTH, 1, NA_DH), k_g.reshape(DEPTH, 1, NA_DH))


def _seg_ones(width, seg):
    a = lax.broadcasted_iota(jnp.int32, (width, width), 0) // seg
    b = lax.broadcasted_iota(jnp.int32, (width, width), 1) // seg
    return (a == b).astype(F32)


def _split2(x):
    hi = x.astype(BF16)
    return hi, (x - hi.astype(F32)).astype(BF16)


def _mm1(a, b):
    return _mm(a.astype(BF16), b.astype(BF16))


def _mm3(a, b, nt=False):
    dot = _mm_nt if nt else _mm
    a_hi, a_lo = _split2(a)
    b_hi, b_lo = _split2(b)
    return dot(a_hi, b_lo) + dot(a_lo, b_hi) + dot(a_hi, b_hi)


def _split3(x):
    x0 = x.astype(BF16)
    r1 = x - x0.astype(F32)
    x1 = r1.astype(BF16)
    return x0, x1, (r1 - x1.astype(F32)).astype(BF16)


def _mm_exact_lhs(mask_b, x):
    x0, x1, x2 = _split3(x)
    return _mm(mask_b, x2) + _mm(mask_b, x1) + _mm(mask_b, x0)


def _mm_exact_rhs(x, mask_b):
    x0, x1, x2 = _split3(x)
    return _mm(x2, mask_b) + _mm(x1, mask_b) + _mm(x0, mask_b)


def _rwkv_kernel(*refs, seq, has_s0, emit_state):
    it = iter(refs)
    r_ref, k_ref, v_ref, sm_ref = (next(it) for _ in range(4))
    (w0_ref, wup_ref, a0_ref, aup_ref, gup_ref, kkp_ref, kap_ref, rkp_ref, lng_ref, lnb_ref) = (
        next(it) for _ in range(10))
    mask_ref = next(it)
    s0_ref = next(it) if has_s0 else None
    y_ref = next(it)
    so_ref = next(it) if emit_state else None
    kk_ref, lw_ref, ka_ref, kd_ref, coef_ref, ysp_ref = (next(it) for _ in range(6))

    dh = RW_DH
    pw = 2 * dh
    nc = seq // CHUNK
    seg_b = _seg_ones(pw, dh).astype(BF16)

    r = r_ref[...]
    k = k_ref[...]
    sm = sm_ref[...]
    wl = jnp.tanh(sm[:, SM_WL:SM_WL + RW_W_RANK])
    al = sm[:, SM_AL:SM_AL + RW_A_RANK]
    kk = k * kkp_ref[...]
    kk = kk * lax.rsqrt(_mm_exact_rhs(kk * kk, seg_b) + EPS)
    kk_ref[...] = kk
    for z in range(2):
        lw_ref[z] = -RW_DECAY_SCALE * _sigmoid(w0_ref[z:z + 1, :] + _mm3(wl, wup_ref[z]))
        a = _sigmoid(a0_ref[z:z + 1, :] + _mm3(al, aup_ref[z]))
        ka_ref[z] = kk * a
        kd_ref[z] = k * (1.0 + (a - 1.0) * kap_ref[...])

    def phase1(c2, carry):
        incl, strict, ti, si = _tri_masks(CHUNK)
        incl_b = (mask_ref[0], mask_ref[1])
        incl2 = tuple(jnp.concatenate([m, m], axis=1) for m in incl)
        diag_blk = (ti // SUB == si // SUB).astype(F32)
        eye = (ti == si).astype(F32)
        zero_blk = jnp.zeros((CHUNK, dh), F32)
        inst = []
        for cc in range(P1_CHUNKS):
            rows = pl.ds(pl.multiple_of((c2 * P1_CHUNKS + cc) * CHUNK, CHUNK), CHUNK)
            vc, rc, kkc = v_ref[rows, :], r_ref[rows, :], kk_ref[rows, :]
            for z in range(2):
                lwc = lw_ref[z, rows, :]
                cum = _mm_exact_lhs(incl_b[z], lwc)
                tot = cum[CHUNK - 1:CHUNK] if z == 0 else cum[0:1]
                e_neg = jnp.exp(-cum)
                dec = jnp.exp(tot - cum)
                e_tot = jnp.exp(tot)
                kac, kdc = ka_ref[z, rows, :], kd_ref[z, rows, :]
                rt = rc * jnp.exp(cum)
                kt = kkc * jnp.exp(cum - lwc)
                at, kdt, ah, kh = kac * e_neg, kdc * e_neg, kac * dec, kdc * dec
                for j in range(2):
                    sl = slice(j * dh, (j + 1) * dh)
                    inst.append((z, kt[:, sl], rt[:, sl], vc[:, sl], at[:, sl], kdt[:, sl], ah[:, sl], kh[:, sl],
                                 e_tot[:, sl]))
        zs = [i[0] for i in inst]
        kt_h, rt_h, v_h = [i[1] for i in inst], [i[2] for i in inst], [i[3] for i in inst]
        each = lambda f, *xs: [f(*a) for a in zip(*xs)]
        aa = each(lambda i: _mm3(jnp.concatenate([i[1], i[2]], axis=0),
                                 jnp.concatenate([i[4], i[5]], axis=0), nt=True), inst)
        low = each(lambda a, z: a[0:CHUNK, 0:CHUNK] * strict[z], aa, zs)
        a_kk = each(lambda a, z: a[0:CHUNK, CHUNK:] * strict[z], aa, zs)
        a_r = each(lambda a, z: a[CHUNK:, :] * incl2[z], aa, zs)
        akv = each(_mm1, a_kk, v_h)
        ld = each(lambda x: x * diag_blk, low)
        lo = each(lambda x, y: x - y, low, ld)
        l2 = each(_mm1, ld, ld)
        l4 = each(_mm1, l2, l2)
        l8 = each(_mm1, l4, l4)
        td = each(lambda x: eye - x, ld)
        for lp in (l2, l4, l8):
            td = each(lambda t, p: t + _mm1(t, p), td, lp)
        x0 = each(lambda t, a, b, c_: _mm1(t, jnp.concatenate([a, b, c_], axis=1)), td, lo, kt_h, akv)
        wm = each(lambda x: x[:, 0:CHUNK], x0)
        pq0 = each(lambda x: x[:, CHUNK:], x0)
        pq = pq0
        for _ in range(CHUNK // SUB - 1):
            pq = each(lambda p0, w, p: p0 - _mm1(w, p), pq0, wm, pq)
        ryc = each(lambda a, p, v_: _mm3(a, jnp.concatenate(
            [p, jnp.concatenate([zero_blk, -v_], axis=1)], axis=0)), a_r, pq, v_h)
        gh = each(lambda i, p: _mm3(jnp.concatenate([i[6], i[7]], axis=1).T,
                                    jnp.concatenate([p, i[3]], axis=1)), inst, pq)
        pieces = []
        for i, g, ry, rt_ in zip(inst, gh, ryc, rt_h):
            g_t = eye * i[8] - g[0:dh, 0:dh]
            h_t = g[dh:, 2 * dh:] - g[0:dh, dh:2 * dh]
            pieces += [g_t, rt_ - ry[:, 0:dh], h_t, -ry[:, dh:]]
        per_chunk = len(pieces) // P1_CHUNKS
        for cc in range(P1_CHUNKS):
            coef_ref[c2 * P1_CHUNKS + cc] = jnp.concatenate(pieces[cc * per_chunk:(cc + 1) * per_chunk], axis=0)
        return carry

    lax.fori_loop(0, nc // P1_CHUNKS, phase1, 0)

    m_init = tuple((s0_ref[z, j].T if has_s0 else jnp.zeros((dh, dh), F32)) for z in range(2) for j in range(2))

    def phase2(ci, ms):
        new_ms, ys = [], []
        for z in range(2):
            c = ci if z == 0 else nc - 1 - ci
            for j in range(2):
                base = (z * 2 + j) * 4 * dh
                out = (_mm3(coef_ref[c, base:base + 2 * dh, :], ms[z * 2 + j])
                       + coef_ref[c, base + 2 * dh:base + 4 * dh, :])
                new_ms.append(out[0:dh])
                ys.append(out[dh:])
        ysp_ref[ci] = jnp.concatenate(ys, axis=1)
        return tuple(new_ms)

    m_fin = lax.fori_loop(0, nc, phase2, m_init)

    if emit_state:
        for z in range(2):
            for j in range(2):
                so_ref[z, j] = m_fin[z * 2 + j].T

    y = jnp.concatenate([ysp_ref[c, :, 0:pw] + ysp_ref[nc - 1 - c, :, pw:] for c in range(nc)], axis=0)
    mu = _mm_exact_rhs(y, seg_b) * (1.0 / dh)
    yc = y - mu
    var = _mm_exact_rhs(yc * yc, seg_b) * (1.0 / dh)
    yn = yc * lax.rsqrt(var + RW_GN_EPS) * lng_ref[...] + lnb_ref[...]
    v = v_ref[...]
    bonus = _mm_exact_rhs(r * k * rkp_ref[...], seg_b) * v
    g = _mm3(_sigmoid(sm[:, SM_GL:SM_GL + RW_G_RANK]), gup_ref[...])
    y_ref[...] = ((yn + bonus) * g).astype(y_ref.dtype)


def _rwkv(u, p, layer, n_seq, seq, row_off, s0=None, emit_state=False):
    n_pairs = RW_HEADS // 2
    pw = 2 * RW_DH
    bpc = MIX_W // pw
    lw = lambda shape: pl.BlockSpec((None,) + shape, lambda b, hp: (layer,) + (0,) * (len(shape) - 1) + (hp,))
    row = lambda a: a.reshape(DEPTH, 1, MIX_W)
    u_pair = lambda cb: pl.BlockSpec((seq, pw), lambda b, hp: (b + row_off, cb * bpc + hp))
    in_specs = [u_pair(CB_RWR), u_pair(CB_RWK), u_pair(CB_RWV),
                pl.BlockSpec((seq, MIX_W), lambda b, hp: (b + row_off, CB_SMALL)),
                lw((2, pw)), lw((2, RW_W_RANK, pw)), lw((2, pw)), lw((2, RW_A_RANK, pw)),
                lw((RW_G_RANK, pw))] + [lw((1, pw))] * 5 + [_full((2, CHUNK, CHUNK))]
    t_idx = np.arange(CHUNK)
    incl_masks = np.stack([t_idx[None, :] <= t_idx[:, None], t_idx[None, :] >= t_idx[:, None]])
    args = [u, u, u, u, p['rw_w0'], p['rw_w_up'], p['rw_a0'], p['rw_a_up'], p['rw_g_up'],
            row(p['rw_k_k']), row(p['rw_k_a']), row(p['rw_r_k']), row(p['rw_ln_g']), row(p['rw_ln_b']),
            jnp.asarray(incl_masks, BF16)]
    st_blk = (2, 2, RW_DH, RW_DH)
    if s0 is not None:
        in_specs.append(pl.BlockSpec((None, None) + st_blk, lambda b, hp: (b, layer, 0, hp, 0, 0)))
        args.append(s0)
    out_shape = [jax.ShapeDtypeStruct((n_seq * seq, MIX_W), BF16)]
    out_specs = [pl.BlockSpec((seq, pw), lambda b, hp: (b, hp))]
    if emit_state:
        out_shape.append(jax.ShapeDtypeStruct((n_seq, 2, RW_HEADS, RW_DH, RW_DH), F32))
        out_specs.append(pl.BlockSpec((None,) + st_blk, lambda b, hp: (b, 0, hp, 0, 0)))
    nc = seq // CHUNK
    tok = lambda n: pltpu.VMEM((n, seq, pw) if n else (seq, pw), F32)
    return pl.pallas_call(
        functools.partial(_rwkv_kernel, seq=seq, has_s0=s0 is not None, emit_state=emit_state),
        out_shape=tuple(out_shape), grid=(n_seq, n_pairs), in_specs=in_specs, out_specs=tuple(out_specs),
        scratch_shapes=[tok(0), tok(2), tok(2), tok(2),
                        pltpu.VMEM((nc, 16 * RW_DH, RW_DH), F32), pltpu.VMEM((nc, CHUNK, 2 * pw), F32)],
        compiler_params=_params(("parallel", "parallel")),
        name=f"rwkv_{seq}",
    )(*args)


def _rwkv_kernel_old(*refs, seq, has_s0, emit_state):
    it = iter(refs)
    r_ref, k_ref, v_ref, sm_ref = (next(it) for _ in range(4))
    (w0_ref, wup_ref, a0_ref, aup_ref, gup_ref, kkp_ref, kap_ref, rkp_ref, lng_ref, lnb_ref) = (
        next(it) for _ in range(10))
    s0_ref = next(it) if has_s0 else None
    y_ref = next(it)
    so_ref = next(it) if emit_state else None
    rp_ref, vp_ref, kk_ref, lw_ref, ka_ref, kd_ref, ys_ref, st_ref = (next(it) for _ in range(8))

    n_pairs = RW_HEADS // 2
    pw = 2 * RW_DH
    nc = seq // CHUNK
    seg = _seg_ones(MIX_W, RW_DH)

    r = r_ref[...]
    k = k_ref[...]
    sm = sm_ref[...]
    wl = jnp.tanh(sm[:, SM_WL:SM_WL + RW_W_RANK])
    al = sm[:, SM_AL:SM_AL + RW_A_RANK]
    kk = k * kkp_ref[...]
    kk = kk * lax.rsqrt(_mm(kk * kk, seg, HI) + EPS)
    for hp in range(n_pairs):
        psl = slice(hp * pw, (hp + 1) * pw)
        rp_ref[hp] = r[:, psl]
        vp_ref[hp] = v_ref[:, psl]
        kk_ref[hp] = kk[:, psl]
    for z in range(2):
        w_pre = w0_ref[z:z + 1, :] + _mm(wl, wup_ref[z], HI)
        lw = -RW_DECAY_SCALE * _sigmoid(w_pre)
        a = _sigmoid(a0_ref[z:z + 1, :] + _mm(al, aup_ref[z], HI))
        ka = kk * a
        kd = k * (1.0 + (a - 1.0) * kap_ref[...])
        for hp in range(n_pairs):
            psl = slice(hp * pw, (hp + 1) * pw)
            lw_ref[z * n_pairs + hp] = lw[:, psl]
            ka_ref[z * n_pairs + hp] = ka[:, psl]
            kd_ref[z * n_pairs + hp] = kd[:, psl]
    for i in range(2 * RW_HEADS):
        st_ref[i] = s0_ref[i // RW_HEADS, i % RW_HEADS] if has_s0 else jnp.zeros((RW_DH, RW_DH), F32)

    incl, strict, ti, si = _tri_masks(CHUNK)
    diag_blk = (ti // SUB == si // SUB).astype(F32)
    eye = (ti == si).astype(F32)

    def chunk_pair(ci, hp):
        for z in range(2):
            c = ci if z == 0 else nc - 1 - ci
            rows = pl.ds(pl.multiple_of(c * CHUNK, CHUNK), CHUNK)
            lwc = lw_ref[z * n_pairs + hp, rows, :]
            cum = _mm(incl[z], lwc, HI)
            tot = cum[CHUNK - 1:CHUNK] if z == 0 else cum[0:1]
            e_neg = jnp.exp(-cum)
            dec = jnp.exp(tot - cum)
            e_tot = jnp.exp(tot)
            kac = ka_ref[z * n_pairs + hp, rows, :]
            kdc = kd_ref[z * n_pairs + hp, rows, :]
            rt = rp_ref[hp, rows, :] * jnp.exp(cum)
            kt = kk_ref[hp, rows, :] * jnp.exp(cum - lwc)
            at, kdt = kac * e_neg, kdc * e_neg
            ah, kh = kac * dec, kdc * dec
            vc = vp_ref[hp, rows, :]
            for j in range(2):
                sl = slice(j * RW_DH, (j + 1) * RW_DH)
                si_ = z * RW_HEADS + hp * 2 + j
                s0 = st_ref[si_]
                kt_h, rt_h, v_h = kt[:, sl], rt[:, sl], vc[:, sl]
                low = _mm_nt(kt_h, at[:, sl], HI) * strict[z]
                a_kk = _mm_nt(kt_h, kdt[:, sl], HI) * strict[z]
                a_ra = _mm_nt(rt_h, at[:, sl], HI) * incl[z]
                a_rk = _mm_nt(rt_h, kdt[:, sl], HI) * incl[z]
                ld = low * diag_blk
                lo = low - ld
                l2 = _mm(ld, ld, HI)
                l4 = _mm(l2, l2, HI)
                l8 = _mm(l4, l4, HI)
                tinv = eye - ld
                tinv = tinv + _mm(tinv, l2, HI)
                tinv = tinv + _mm(tinv, l4, HI)
                tinv = tinv + _mm(tinv, l8, HI)
                rhs = -(_mm_nt(kt_h, s0, HI) + _mm(a_kk, v_h, HI))
                cv = _mm(tinv, rhs, HI)
                wm = _mm(tinv, lo, HI)
                uu = cv
                for _ in range(CHUNK // SUB - 1):
                    uu = cv - _mm(wm, uu, HI)
                y = _mm_nt(rt_h, s0, HI) + _mm(a_ra, uu, HI) + _mm(a_rk, v_h, HI)
                ys_ref[z * n_pairs + hp, rows, sl] = y
                st_ref[si_] = s0 * e_tot[:, sl] + _mm_tn(uu, ah[:, sl], HI) + _mm_tn(v_h, kh[:, sl], HI)

    def chunk_step(ci, carry):
        def pair_step(hp, carry2):
            chunk_pair(ci, hp)
            return carry2
        return lax.fori_loop(0, n_pairs, pair_step, carry)

    lax.fori_loop(0, nc, chunk_step, 0)

    if emit_state:
        for i in range(2 * RW_HEADS):
            so_ref[i // RW_HEADS, i % RW_HEADS] = st_ref[i]

    g = _mm(_sigmoid(sm[:, SM_GL:SM_GL + RW_G_RANK]), gup_ref[...], HI)
    for h in range(RW_HEADS):
        hp, j = divmod(h, 2)
        sl = slice(j * RW_DH, (j + 1) * RW_DH)
        hsl = slice(h * RW_DH, (h + 1) * RW_DH)
        y = ys_ref[hp, :, sl] + ys_ref[n_pairs + hp, :, sl]
        mu = jnp.mean(y, axis=-1, keepdims=True)
        var = jnp.mean(jnp.square(y - mu), axis=-1, keepdims=True)
        yn = (y - mu) * lax.rsqrt(var + RW_GN_EPS) * lng_ref[:, hsl] + lnb_ref[:, hsl]
        r_h, k_h, v_h = rp_ref[hp, :, sl], k_ref[:, hsl], vp_ref[hp, :, sl]
        bonus = jnp.sum(r_h * k_h * rkp_ref[:, hsl], axis=-1, keepdims=True) * v_h
        y_ref[:, hsl] = ((yn + bonus) * g[:, hsl]).astype(y_ref.dtype)


def _rwkv_old(u, p, layer, n_seq, seq, row_off, s0=None, emit_state=False):
    lw = lambda shape: pl.BlockSpec((None,) + shape, lambda b: (layer,) + (0,) * len(shape))
    row = lambda a: a.reshape(DEPTH, 1, MIX_W)
    in_specs = [_u_spec(seq, cb, row_off) for cb in (CB_RWR, CB_RWK, CB_RWV, CB_SMALL)] + [
        lw((2, MIX_W)), lw((2, RW_W_RANK, MIX_W)), lw((2, MIX_W)), lw((2, RW_A_RANK, MIX_W)),
        lw((RW_G_RANK, MIX_W))] + [lw((1, MIX_W))] * 5
    args = [u, u, u, u, p['rw_w0'], p['rw_w_up'], p['rw_a0'], p['rw_a_up'], p['rw_g_up'],
            row(p['rw_k_k']), row(p['rw_k_a']), row(p['rw_r_k']), row(p['rw_ln_g']), row(p['rw_ln_b'])]
    st_shape = (2, RW_HEADS, RW_DH, RW_DH)
    if s0 is not None:
        in_specs.append(pl.BlockSpec((None, None) + st_shape, lambda b: (b, layer, 0, 0, 0, 0)))
        args.append(s0)
    out_shape = [jax.ShapeDtypeStruct((n_seq * seq, MIX_W), BF16)]
    out_specs = [pl.BlockSpec((seq, MIX_W), lambda b: (b, 0))]
    if emit_state:
        out_shape.append(jax.ShapeDtypeStruct((n_seq,) + st_shape, F32))
        out_specs.append(pl.BlockSpec((None,) + st_shape, lambda b: (b, 0, 0, 0, 0)))
    n_pairs = RW_HEADS // 2
    pair = lambda n: pltpu.VMEM((n, seq, 2 * RW_DH), F32)
    return pl.pallas_call(
        functools.partial(_rwkv_kernel, seq=seq, has_s0=s0 is not None, emit_state=emit_state),
        out_shape=tuple(out_shape), grid=(n_seq,), in_specs=in_specs, out_specs=tuple(out_specs),
        scratch_shapes=[pair(n_pairs), pair(n_pairs), pair(n_pairs), pair(2 * n_pairs), pair(2 * n_pairs),
                        pair(2 * n_pairs), pair(2 * n_pairs), pltpu.VMEM((2 * RW_HEADS, RW_DH, RW_DH), F32)],
        compiler_params=_params(("parallel",)),
        name=f"rwkv_{seq}",
    )(*args)


def _log_sigmoid(x):
    return jnp.minimum(x, 0.0) - jnp.log(1.0 + jnp.exp(-jnp.abs(x)))


def _mlstm_kernel(*refs, seq, has_s0, emit_state):
    it = iter(refs)
    q_ref, k_ref, v_ref, o_ref, sm_ref, gr_ref, bc_ref, br_ref, ng_ref = (next(it) for _ in range(9))
    c0_ref, n0_ref, m0_ref = ((next(it), next(it), next(it)) if has_s0 else (None, None, None))
    y_ref = next(it)
    co_ref, no_ref, mo_ref = ((next(it), next(it), next(it)) if emit_state else (None, None, None))
    hs_ref, c_ref, n_ref, m_ref = (next(it) for _ in range(4))

    nc = seq // CHUNK
    n_st = 2 * ML_HEADS
    for i in range(n_st):
        z, h = divmod(i, ML_HEADS)
        c_ref[i] = c0_ref[z, h] if has_s0 else jnp.zeros((ML_DH, ML_DH), F32)
        n_ref[i] = n0_ref[z, h] if has_s0 else jnp.zeros((1, ML_DH), F32)
        m_ref[i] = m0_ref[z, h] if has_s0 else jnp.zeros((1, 1), F32)

    incl, _, ti, si = _tri_masks(CHUNK)
    before = ((si <= ti), (si >= ti))

    nh = ML_HEADS
    chains = [(z, h) for z in range(2) for h in range(nh)]
    each = lambda f, *xs: [f(*a) for a in zip(*xs)]
    zs = [z for z, _ in chains]

    def chunk_step(ci, carry):
        per_dir = []
        for z in range(2):
            c = ci if z == 0 else nc - 1 - ci
            rows = pl.ds(pl.multiple_of(c * CHUNK, CHUNK), CHUNK)
            g0 = SM_GATE + z * 2 * nh
            gc = sm_ref[rows, g0:g0 + 2 * nh] + bc_ref[:, z * 2 * nh:(z + 1) * 2 * nh]
            gr = gr_ref[c, z * 2 * nh:(z + 1) * 2 * nh, :] + br_ref[z * 2 * nh:(z + 1) * 2 * nh, :]
            b_cols = _mm(incl[z], _log_sigmoid(gc[:, nh:]), HI)
            b_rows = _mm(_log_sigmoid(gr[nh:]), incl[1 - z], HI)
            per_dir.append((rows, gc[:, :nh], b_cols, gr[:nh], b_rows))
        rows = [per_dir[z][0] for z, _ in chains]
        hsl = [slice(h * ML_DH, (h + 1) * ML_DH) for _, h in chains]
        i_col = [per_dir[z][1][:, h:h + 1] for z, h in chains]
        b_col = [per_dir[z][2][:, h:h + 1] for z, h in chains]
        i_row = [per_dir[z][3][h:h + 1] for z, h in chains]
        b_row = [per_dir[z][4][h:h + 1] for z, h in chains]
        b_last = each(lambda b, z: b[CHUNK - 1:CHUNK] if z == 0 else b[0:1], b_col, zs)
        m_old = [m_ref[i] for i in range(n_st)]
        cm = [c_ref[i] for i in range(n_st)]
        nv = [n_ref[i] for i in range(n_st)]
        qc = each(lambda r, s: q_ref[r, s] * (ML_DH ** -0.5), rows, hsl)
        kc = each(lambda r, s: k_ref[r, s], rows, hsl)
        vc = each(lambda r, s: v_ref[r, s], rows, hsl)
        qk = each(lambda q, k: _mm_nt(q, k, HI), qc, kc)
        qcm = each(lambda q, c_: _mm_nt(q, c_, HI), qc, cm)
        a_t = each(lambda b, m: b + m, b_col, m_old)
        dmat = each(lambda bc, brw, ir, z: jnp.where(before[z], bc - brw + ir, NEG_INF), b_col, b_row, i_row, zs)
        m_t = each(lambda a, d: jnp.maximum(a, jnp.max(d, axis=-1, keepdims=True)), a_t, dmat)
        s = each(lambda x, d, m: x * jnp.exp(d - m), qk, dmat, m_t)
        inter = each(lambda a, m: jnp.exp(a - m), a_t, m_t)
        sv = each(lambda s_, v: _mm(s_, v, HI), s, vc)
        g_col = each(lambda bl, bc, ic: bl - bc + ic, b_last, b_col, i_col)
        a_l = each(lambda bl, m: bl + m, b_last, m_old)
        m_new = each(lambda a, g: jnp.maximum(a, jnp.max(g, axis=0, keepdims=True)), a_l, g_col)
        wgt = each(lambda g, m: jnp.exp(g - m), g_col, m_new)
        vk = each(lambda v, w, k: _mm_tn(v * w, k, HI), vc, wgt, kc)
        decay = each(lambda a, m: jnp.exp(a - m), a_l, m_new)
        num = each(lambda i_, x, y: i_ * x + y, inter, qcm, sv)
        den = each(lambda i_, q, n_, s_: i_ * jnp.sum(q * n_, axis=-1, keepdims=True)
                   + jnp.sum(s_, axis=-1, keepdims=True), inter, qc, nv, s)
        hh = each(lambda n_, d, m: n_ / jnp.maximum(jnp.abs(d), jnp.exp(-m)), num, den, m_t)
        for z in range(2):
            hs_ref[z, per_dir[z][0], :] = jnp.concatenate(hh[z * nh:(z + 1) * nh], axis=1)
        for i in range(n_st):
            c_ref[i] = decay[i] * cm[i] + vk[i]
            n_ref[i] = decay[i] * nv[i] + jnp.sum(wgt[i] * kc[i], axis=0, keepdims=True)
            m_ref[i] = m_new[i]
        return carry

    lax.fori_loop(0, nc, chunk_step, 0)

    if emit_state:
        for i in range(n_st):
            z, h = divmod(i, ML_HEADS)
            co_ref[z, h] = c_ref[i]
            no_ref[z, h] = n_ref[i]
            mo_ref[z, h] = m_ref[i]

    for h in range(ML_HEADS):
        hsl = slice(h * ML_DH, (h + 1) * ML_DH)
        hn = _head_rms(hs_ref[0, :, hsl] + hs_ref[1, :, hsl], ng_ref[:, hsl])
        y_ref[:, hsl] = (_sigmoid(o_ref[:, hsl]) * hn).astype(y_ref.dtype)


def _mlstm(u, gate_rows, p, layer, n_seq, seq, row_off, state=None, emit_state=False):
    lw = lambda shape: pl.BlockSpec((None,) + shape, lambda b: (layer,) + (0,) * len(shape))
    nc = seq // CHUNK
    n_gate = 4 * ML_HEADS
    in_specs = [_u_spec(seq, cb, row_off) for cb in (CB_MLQ, CB_MLK, CB_MLV, CB_MLO, CB_SMALL)] + [
        pl.BlockSpec((None, nc, n_gate, CHUNK), lambda b: (b, 0, 0, 0)),
        lw((1, n_gate)), lw((n_gate, 1)), lw((1, MIX_W))]
    args = [u, u, u, u, u, gate_rows, p['ml_gate_b'].reshape(DEPTH, 1, n_gate),
            p['ml_gate_b'].reshape(DEPTH, n_gate, 1), p['ml_norm_g'].reshape(DEPTH, 1, MIX_W)]
    c_shape, n_shape, m_shape = (2, ML_HEADS, ML_DH, ML_DH), (2, ML_HEADS, 1, ML_DH), (2, ML_HEADS, 1, 1)
    if state is not None:
        c0, n0, m0 = state
        for a, shp in ((c0, c_shape), (n0, n_shape), (m0, m_shape)):
            in_specs.append(pl.BlockSpec((None, None) + shp, lambda b: (b, layer, 0, 0, 0, 0)))
            args.append(a.reshape(a.shape[:2] + shp))
    out_shape = [jax.ShapeDtypeStruct((n_seq * seq, MIX_W), BF16)]
    out_specs = [pl.BlockSpec((seq, MIX_W), lambda b: (b, 0))]
    if emit_state:
        for shp in (c_shape, n_shape, m_shape):
            out_shape.append(jax.ShapeDtypeStruct((n_seq,) + shp, F32))
            out_specs.append(pl.BlockSpec((None,) + shp, lambda b: (b, 0, 0, 0, 0)))
    n_st = 2 * ML_HEADS
    return pl.pallas_call(
        functools.partial(_mlstm_kernel, seq=seq, has_s0=state is not None, emit_state=emit_state),
        out_shape=tuple(out_shape), grid=(n_seq,), in_specs=in_specs, out_specs=tuple(out_specs),
        scratch_shapes=[pltpu.VMEM((2, seq, MIX_W), F32), pltpu.VMEM((n_st, ML_DH, ML_DH), F32),
                        pltpu.VMEM((n_st, 1, ML_DH), F32), pltpu.VMEM((n_st, 1, 1), F32)],
        compiler_params=_params(("parallel",)),
        name=f"mlstm_{seq}",
    )(*args)


def _outproj_kernel(ya_ref, yb_ref, yc_ref, yd_ref, w_ref, x_ref, g_ref, o_ref):
    acc = _mm(ya_ref[...], w_ref[0:MIX_W, :].astype(BF16))
    for i, y_ref in enumerate((yb_ref, yc_ref, yd_ref), start=1):
        acc += _mm(y_ref[...], w_ref[i * MIX_W:(i + 1) * MIX_W, :].astype(BF16))
    o_ref[...] = x_ref[...] + g_ref[...] * acc


def _outproj(ys, w_out_b, x, mod, layer):
    tm, tn = 1024, 1024
    y_spec = pl.BlockSpec((tm, MIX_W), lambda i, j: (i, 0))
    return pl.pallas_call(
        _outproj_kernel,
        out_shape=jax.ShapeDtypeStruct((N_TOK, D_MODEL), F32),
        grid=(N_TOK // tm, D_MODEL // tn),
        in_specs=[y_spec] * 4 + [pl.BlockSpec((None, D_MODEL, tn), lambda i, j: (layer, 0, j)),
                                 pl.BlockSpec((tm, tn), lambda i, j: (i, j)),
                                 _mod_spec(layer, 2, tm, tn, with_j=True)],
        out_specs=pl.BlockSpec((tm, tn), lambda i, j: (i, j)),
        compiler_params=_params(("parallel", "parallel")),
        name="outproj",
    )(*ys, w_out_b, x, mod)


def _router_kernel(x_ref, g_ref, sh_ref, sc_ref, rw_ref, rb_ref, xm_ref, gt_ref, ids_ref):
    xm = _modulated(x_ref[...], g_ref[...], sh_ref[...], sc_ref[...])
    xm_ref[...] = xm
    logits = _mm_nt(rw_ref[...], xm, HI)
    ex = jnp.exp(logits - jnp.max(logits, axis=0, keepdims=True))
    scores = ex / jnp.sum(ex, axis=0, keepdims=True)
    sel = scores + rb_ref[...]
    per = N_EXPERTS // N_EXPERT_GROUPS
    s = [sel[e:e + 1, :] for e in range(N_EXPERTS)]
    grp_score = []
    for g in range(N_EXPERT_GROUPS):
        a, b, c, d = s[per * g:per * (g + 1)]
        hi1, lo1, hi2, lo2 = jnp.maximum(a, b), jnp.minimum(a, b), jnp.maximum(c, d), jnp.minimum(c, d)
        grp_score.append(jnp.maximum(hi1, hi2) + jnp.maximum(jnp.minimum(hi1, hi2), jnp.maximum(lo1, lo2)))
    best = functools.reduce(jnp.maximum, grp_score)
    in_grp, taken = [], jnp.zeros_like(best)
    for g in range(N_EXPERT_GROUPS):
        hit = jnp.where(grp_score[g] == best, 1.0, 0.0) * (1.0 - taken)
        in_grp.append(hit)
        taken = taken + hit
    picked, flag = [], []
    for e in range(N_EXPERTS):
        g = e // per
        rank = jnp.zeros_like(best)
        for o in range(per * g, per * (g + 1)):
            if o < e:
                rank += jnp.where(s[o] >= s[e], 1.0, 0.0)
            elif o > e:
                rank += jnp.where(s[o] > s[e], 1.0, 0.0)
        flag.append(in_grp[g] * jnp.where(rank < 2.0, 1.0, 0.0))
        picked.append(flag[e] * scores[e:e + 1, :])
    total = functools.reduce(lambda x, y: x + y, picked)
    for e in range(N_EXPERTS):
        gt_ref[e:e + 1, :] = picked[e] / total
    lo_id = functools.reduce(jnp.minimum, [jnp.where(flag[e] > 0.0, float(e), float(N_EXPERTS)) for e in range(N_EXPERTS)])
    hi_id = functools.reduce(jnp.maximum, [jnp.where(flag[e] > 0.0, float(e), -1.0) for e in range(N_EXPERTS)])
    ids_ref[0:1, :] = lo_id
    ids_ref[1:2, :] = hi_id
    ids_ref[2:, :] = jnp.zeros((ids_ref.shape[0] - 2,) + lo_id.shape[1:], F32)


def _router(x, norm_g, mod, router_wt, router_b, layer):
    tm = 512
    return pl.pallas_call(
        _router_kernel,
        out_shape=(jax.ShapeDtypeStruct((N_TOK, D_MODEL), F32), jax.ShapeDtypeStruct((N_EXPERTS, N_TOK), F32),
                   jax.ShapeDtypeStruct((8, N_TOK), F32)),
        grid=(N_TOK // tm,),
        in_specs=[pl.BlockSpec((tm, D_MODEL), lambda i: (i, 0)),
                  pl.BlockSpec((None, 1, D_MODEL), lambda i: (layer, 0, 0)),
                  _mod_spec(layer, 3, tm), _mod_spec(layer, 4, tm),
                  _full((N_EXPERTS, D_MODEL)), _full((N_EXPERTS, 1))],
        out_specs=(pl.BlockSpec((tm, D_MODEL), lambda i: (i, 0)), pl.BlockSpec((N_EXPERTS, tm), lambda i: (0, i)),
                   pl.BlockSpec((8, tm), lambda i: (0, i))),
        compiler_params=_params(("parallel",)),
        name="router",
    )(x, norm_g.reshape(DEPTH, 1, D_MODEL), mod, mod, router_wt, router_b.reshape(N_EXPERTS, 1))


def _experts_kernel(xm_ref, gates_ref, w1_ref, w3_ref, w2_ref, x_ref, g_ref, o_ref, acc_ref):
    e = pl.program_id(1)

    @pl.when(e == 0)
    def _():
        acc_ref[...] = jnp.zeros_like(acc_ref)

    xm = xm_ref[...]
    h1 = _mm(xm, w1_ref[...])
    h3 = _mm(xm, w3_ref[...])
    gates = gates_ref[...]
    lane = lax.broadcasted_iota(jnp.int32, gates.shape, 1)
    gate = jnp.sum(jnp.where(lane == e, gates, 0.0), axis=-1, keepdims=True)
    hh = h1 * _sigmoid(h1) * h3 * gate
    acc_ref[...] += _mm(hh.astype(BF16), w2_ref[...])

    @pl.when(e == N_EXPERTS - 1)
    def _():
        o_ref[...] = x_ref[...] + g_ref[...] * acc_ref[...]


def _experts(xm, gates, w1_b, w3_b, w2_b, x, mod, layer):
    tm = 512
    row = pl.BlockSpec((tm, D_MODEL), lambda i, e: (i, 0))
    return pl.pallas_call(
        _experts_kernel,
        out_shape=jax.ShapeDtypeStruct((N_TOK, D_MODEL), F32),
        grid=(N_TOK // tm, N_EXPERTS),
        in_specs=[row, pl.BlockSpec((tm, N_EXPERTS), lambda i, e: (i, 0)),
                  pl.BlockSpec((None, None, D_MODEL, D_EXPERT), lambda i, e: (layer, e, 0, 0)),
                  pl.BlockSpec((None, None, D_MODEL, D_EXPERT), lambda i, e: (layer, e, 0, 0)),
                  pl.BlockSpec((None, None, D_EXPERT, D_MODEL), lambda i, e: (layer, e, 0, 0)),
                  row, _mod_spec(layer, 5, tm)],
        out_specs=row,
        scratch_shapes=[pltpu.VMEM((tm, D_MODEL), F32)],
        compiler_params=_params(("parallel", "arbitrary")),
        name="experts",
    )(xm, gates, w1_b, w3_b, w2_b, x, mod)


MOE_TM = 512
N_PAIRS = N_EXPERT_GROUPS * 6
MOE_STEPS = 2 * (N_PAIRS + N_TOK // MOE_TM - 1)


def _gather_rows(src_hbm, dst_ref, idx_ref, base, n, sem):
    def issue(r, carry):
        pltpu.make_async_copy(src_hbm.at[pl.ds(idx_ref[base + r], 1), :], dst_ref.at[pl.ds(r, 1), :], sem).start()
        return carry

    def wait(r, carry):
        pltpu.make_async_copy(src_hbm.at[pl.ds(0, 1), :], dst_ref.at[pl.ds(r, 1), :], sem).wait()
        return carry

    lax.fori_loop(0, n, issue, 0, unroll=8)
    lax.fori_loop(0, n, wait, 0, unroll=8)


def _moe_plan(ids):
    i32 = jnp.int32
    lo, hi = ids[0].astype(i32), ids[1].astype(i32)
    src = jnp.argsort(lo * N_EXPERTS + hi).astype(i32)
    pos = jnp.zeros((N_TOK,), i32).at[src].set(jnp.arange(N_TOK, dtype=i32))
    n_tiles = N_TOK // MOE_TM
    ex = jnp.arange(N_EXPERTS, dtype=i32)
    lo_s, hi_s = lo[src].reshape(n_tiles, MOE_TM, 1), hi[src].reshape(n_tiles, MOE_TM, 1)
    used = ((lo_s == ex).any(axis=1) | (hi_s == ex).any(axis=1)).reshape(-1)
    n_valid = jnp.sum(used).astype(i32)
    idx = jnp.nonzero(used, size=MOE_STEPS, fill_value=0)[0].astype(i32)
    valid = jnp.arange(MOE_STEPS, dtype=i32) < n_valid
    idx = jnp.where(valid, idx, idx[jnp.maximum(n_valid - 1, 0)])
    tile, exp = idx // N_EXPERTS, idx % N_EXPERTS
    first = valid & (tile != jnp.concatenate([jnp.full((1,), -1, i32), tile[:-1]]))
    return src, pos, tile, exp, first.astype(i32), valid.astype(i32)


def _moe_kernel(tile_ref, exp_ref, first_ref, valid_ref, src_ref, xm_hbm, gates_ref, w1_ref, w3_ref, w2_ref,
                o_ref, xs_ref, xb_ref, sem):
    s = pl.program_id(0)

    @pl.when(first_ref[s] == 1)
    def _():
        _gather_rows(xm_hbm, xs_ref, src_ref, tile_ref[s] * MOE_TM, MOE_TM, sem)
        xb_ref[...] = xs_ref[...].astype(BF16)
        o_ref[...] = jnp.zeros_like(o_ref)

    @pl.when(valid_ref[s] == 1)
    def _():
        xb = xb_ref[...]
        h1 = _mm(xb, w1_ref[...].astype(BF16))
        h3 = _mm(xb, w3_ref[...].astype(BF16))
        gates = gates_ref[...]
        lane = lax.broadcasted_iota(jnp.int32, gates.shape, 1)
        gate = jnp.sum(jnp.where(lane == exp_ref[s], gates, 0.0), axis=-1, keepdims=True)
        hh = h1 * _sigmoid(h1) * h3 * gate
        o_ref[...] += _mm(hh.astype(BF16), w2_ref[...].astype(BF16))


def _moe(xm, gates_sorted, plan, w1_b, w3_b, w2_b, layer):
    src, _, tile, exp, first, valid = plan
    w_in = pl.BlockSpec((None, None, D_MODEL, D_EXPERT), lambda s, t, e, f, v, i: (layer, e[s], 0, 0))
    grid_spec = pltpu.PrefetchScalarGridSpec(
        num_scalar_prefetch=5, grid=(MOE_STEPS,),
        in_specs=[pl.BlockSpec(memory_space=pl.ANY),
                  pl.BlockSpec((MOE_TM, N_EXPERTS), lambda s, t, e, f, v, i: (t[s], 0)),
                  w_in, w_in,
                  pl.BlockSpec((None, None, D_EXPERT, D_MODEL), lambda s, t, e, f, v, i: (layer, e[s], 0, 0))],
        out_specs=pl.BlockSpec((MOE_TM, D_MODEL), lambda s, t, e, f, v, i: (t[s], 0)),
        scratch_shapes=[pltpu.VMEM((MOE_TM, D_MODEL), F32), pltpu.VMEM((MOE_TM, D_MODEL), BF16),
                        pltpu.SemaphoreType.DMA(())])
    return pl.pallas_call(
        _moe_kernel, out_shape=jax.ShapeDtypeStruct((N_TOK, D_MODEL), F32), grid_spec=grid_spec,
        compiler_params=_params(("arbitrary",)), name="moe",
    )(tile, exp, first, valid, src, xm, gates_sorted, w1_b, w3_b, w2_b)


def _combine_kernel(pos_ref, acc_hbm, x_ref, g_ref, o_ref, buf_ref, sem):
    _gather_rows(acc_hbm, buf_ref, pos_ref, pl.program_id(0) * MOE_TM, MOE_TM, sem)
    o_ref[...] = x_ref[...] + g_ref[...] * buf_ref[...]


def _combine(acc_sorted, pos, x, mod, layer):
    tm = MOE_TM
    row = pl.BlockSpec((tm, D_MODEL), lambda i, p: (i, 0))
    grid_spec = pltpu.PrefetchScalarGridSpec(
        num_scalar_prefetch=1, grid=(N_TOK // tm,),
        in_specs=[pl.BlockSpec(memory_space=pl.ANY), row,
                  pl.BlockSpec((None, None, None, 1, D_MODEL), lambda i, p: (layer, _mod_row(i, tm), 5, 0, 0))],
        out_specs=row,
        scratch_shapes=[pltpu.VMEM((tm, D_MODEL), F32), pltpu.SemaphoreType.DMA(())])
    return pl.pallas_call(
        _combine_kernel, out_shape=jax.ShapeDtypeStruct((N_TOK, D_MODEL), F32), grid_spec=grid_spec,
        compiler_params=_params(("arbitrary",)), name="combine",
    )(pos, acc_sorted, x, mod)


def _gate_rows(gcols, n_seq, seq):
    return gcols.reshape(n_seq, seq // CHUNK, CHUNK, gcols.shape[-1]).transpose(0, 1, 3, 2)


def kernel(x_prompt, x_sample, cache_na_k, cache_na_v, state_rwkv, state_mlstm_c, state_mlstm_n, state_mlstm_m,
           c, c_ctx, norm1_g, norm2_g, w_mod, b_mod, w_in, conv_w, na_q_g, na_k_g, na_rpb, rw_w0, rw_w_up, rw_a0,
           rw_a_up, rw_g_up, rw_k_k, rw_k_a, rw_r_k, rw_ln_g, rw_ln_b, ml_gate_b, ml_norm_g, w_out, router_w,
           router_b, moe_w1, moe_w3, moe_w2):
    p = dict(rw_w0=rw_w0, rw_w_up=rw_w_up, rw_a0=rw_a0, rw_a_up=rw_a_up, rw_g_up=rw_g_up, rw_k_k=rw_k_k,
             rw_k_a=rw_k_a, rw_r_k=rw_r_k, rw_ln_g=rw_ln_g, rw_ln_b=rw_ln_b, ml_gate_b=ml_gate_b,
             ml_norm_g=ml_norm_g)
    cvecs = jnp.concatenate([c_ctx[None], c, jnp.zeros((MOD_ROWS - 1 - DEC_BATCH, D_MODEL), F32)], axis=0)
    mod = _adaln(cvecs, w_mod, b_mod).reshape(DEPTH, MOD_ROWS, 6, 1, D_MODEL)

    assert w_in.shape[-1] == P_IN
    w_in_b, w_out_b, w1_b, w3_b, w2_b = w_in, w_out, moe_w1, moe_w3, moe_w2
    tw = _na_tables(na_rpb)
    router_wt = router_w.T
    sample_row_off = N_PROMPT // DEC_SEQ

    x = jnp.concatenate([x_prompt.reshape(N_PROMPT, D_MODEL), x_sample.reshape(N_SAMPLE, D_MODEL)], axis=0)
    new_k, new_v, new_rw, new_c, new_n, new_m = [], [], [], [], [], []
    for l in range(DEPTH):
        u = _inproj(x, norm1_g, mod, w_in_b, l)
        ya_p, yb_p, nk, nv = _attn_prompt(u, conv_w, na_q_g, na_k_g, l)
        ya_s, yb_s = _na_sample(u, cache_na_k, cache_na_v, tw, conv_w, na_q_g, na_k_g, l)
        yc_p, st = _rwkv(u, p, l, BATCH, SEQ, 0, emit_state=True)
        (yc_s,) = _rwkv(u, p, l, DEC_BATCH, DEC_SEQ, sample_row_off, s0=state_rwkv)
        g0 = CB_SMALL * MIX_W + SM_GATE
        gcols = u[:, g0:g0 + 4 * ML_HEADS]
        yd_p, cm, nm, mm = _mlstm(u, _gate_rows(gcols[:N_PROMPT], BATCH, SEQ), p, l, BATCH, SEQ, 0,
                                  emit_state=True)
        (yd_s,) = _mlstm(u, _gate_rows(gcols[N_PROMPT:], DEC_BATCH, DEC_SEQ), p, l, DEC_BATCH, DEC_SEQ,
                         sample_row_off, state=(state_mlstm_c, state_mlstm_n, state_mlstm_m))
        ys = [jnp.concatenate(pair, axis=0) for pair in ((ya_p, ya_s), (yb_p, yb_s), (yc_p, yc_s), (yd_p, yd_s))]
        x = _outproj(ys, w_out_b, x, mod, l)
        xm, gates_t, ids = _router(x, norm2_g, mod, router_wt, router_b, l)
        plan = _moe_plan(ids)
        acc_sorted = _moe(xm, gates_t.T[plan[0]], plan, w1_b, w3_b, w2_b, l)
        x = _combine(acc_sorted, plan[1], x, mod, l)
        new_k.append(nk)
        new_v.append(nv)
        new_rw.append(st)
        new_c.append(cm)
        new_n.append(nm.reshape(BATCH, 2, ML_HEADS, ML_DH))
        new_m.append(mm.reshape(BATCH, 2, ML_HEADS))
    stack = lambda xs: jnp.stack(xs, axis=1)
    return (x[:N_PROMPT].reshape(BATCH, SEQ, D_MODEL), x[N_PROMPT:].reshape(DEC_BATCH, DEC_SEQ, D_MODEL),
            stack(new_k), stack(new_v), stack(new_rw), stack(new_c), stack(new_n), stack(new_m))
```

```python
import functools

import numpy as np
import jax
import jax.numpy as jnp
from jax import lax
from jax.experimental import pallas as pl
from jax.experimental.pallas import tpu as pltpu

F32 = jnp.float32
BF16 = jnp.bfloat16
HI = lax.Precision.HIGHEST

D_MODEL = 2048
BATCH = 16
SEQ = 256
DEPTH = 2
DEC_BATCH = 2
DEC_SEQ = 1024
PAST_LEN = 512
GRID_W = 64
MIX_W = D_MODEL // 4
CONV_K = 3
NA_DH = 64
NA_HEADS = MIX_W // NA_DH
NA_WIN_R = 8
NA_WIN_C = 16
NA_SCALE = NA_DH ** -0.5
ROPE_THETA = 10000.0
RW_DH = 64
RW_HEADS = MIX_W // RW_DH
RW_W_RANK = 64
RW_A_RANK = 64
RW_G_RANK = 128
RW_DECAY_SCALE = 0.606531
RW_GN_EPS = 64e-5
ML_DH = 128
ML_HEADS = MIX_W // ML_DH
N_EXPERTS = 16
N_EXPERT_GROUPS = 4
D_EXPERT = 512
EPS = 1e-6
NEG_INF = -1e30

N_PROMPT = BATCH * SEQ
N_SAMPLE = DEC_BATCH * DEC_SEQ
N_TOK = N_PROMPT + N_SAMPLE
MOD_ROWS = 8
CHUNK = 64
SUB = 16
P1_CHUNKS = 4
P_BLOCKS = 15
P_PAD = P_BLOCKS * MIX_W
(CB_CVB, CB_CVC, CB_CVH, CB_NAQ, CB_NAK, CB_NAV, CB_RWR, CB_RWK, CB_RWV,
 CB_MLQ, CB_MLK, CB_MLV, CB_MLO, CB_SMALL, CB_GATE) = range(P_BLOCKS)
SM_WL, SM_AL, SM_GL = 0, 64, 128
SM_GATE = MIX_W - 4 * ML_HEADS
VMEM_LIMIT = 56 * 1024 * 1024


def _mm(a, b, prec=None):
    return jnp.dot(a, b, precision=prec, preferred_element_type=F32)


def _mm_nt(a, b, prec=None):
    return lax.dot_general(a, b, (((1,), (1,)), ((), ())), precision=prec, preferred_element_type=F32)


def _mm_tn(a, b, prec=None):
    return _mm(a.T, b, prec)


def _sigmoid(x):
    return 1.0 / (1.0 + jnp.exp(-x))


def _full(shape):
    n = len(shape)
    return pl.BlockSpec(shape, lambda *_: (0,) * n)


def _params(sem):
    return pltpu.CompilerParams(dimension_semantics=sem, vmem_limit_bytes=VMEM_LIMIT)


def _mod_row(i, tm):
    n_prompt_tiles = N_PROMPT // tm
    tiles_per_sample = DEC_SEQ // tm
    return jnp.where(i < n_prompt_tiles, 0, 1 + (i - n_prompt_tiles) // tiles_per_sample)


def _mod_spec(layer, chunk, tm, tn=D_MODEL, with_j=False):
    if with_j:
        return pl.BlockSpec((None, None, None, 1, tn), lambda i, j: (layer, _mod_row(i, tm), chunk, 0, j))
    return pl.BlockSpec((None, None, None, 1, tn), lambda i, *_: (layer, _mod_row(i, tm), chunk, 0, 0))


def _tri_masks(n):
    t = lax.broadcasted_iota(jnp.int32, (n, n), 0)
    s = lax.broadcasted_iota(jnp.int32, (n, n), 1)
    incl = ((s <= t).astype(F32), (s >= t).astype(F32))
    strict = ((s < t).astype(F32), (s > t).astype(F32))
    return incl, strict, t, s


def _adaln_kernel(cv_ref, w_ref, b_ref, o_ref):
    cv = cv_ref[...]
    o_ref[...] = _mm(cv * _sigmoid(cv), w_ref[...], HI) + b_ref[...]


def _adaln(cvecs, w_mod, b_mod):
    tn = 1024
    n_out = 6 * D_MODEL
    return pl.pallas_call(
        _adaln_kernel,
        out_shape=jax.ShapeDtypeStruct((DEPTH, MOD_ROWS, n_out), F32),
        grid=(DEPTH, n_out // tn),
        in_specs=[_full((MOD_ROWS, D_MODEL)),
                  pl.BlockSpec((None, D_MODEL, tn), lambda l, j: (l, 0, j)),
                  pl.BlockSpec((None, 1, tn), lambda l, j: (l, 0, j))],
        out_specs=pl.BlockSpec((None, MOD_ROWS, tn), lambda l, j: (l, 0, j)),
        compiler_params=_params(("parallel", "parallel")),
        name="adaln",
    )(cvecs, w_mod, b_mod.reshape(DEPTH, 1, n_out))


def _modulated(x, g, sh, sc):
    y = x * lax.rsqrt(jnp.mean(x * x, axis=-1, keepdims=True) + EPS) * g
    return y * (1.0 + sc) + sh


def _inproj_kernel(x_ref, g_ref, sh_ref, sc_ref, w_ref, o_ref, xm_ref):
    @pl.when(pl.program_id(1) == 0)
    def _():
        xm_ref[...] = _modulated(x_ref[...], g_ref[...], sh_ref[...], sc_ref[...]).astype(BF16)

    o_ref[...] = _mm_nt(xm_ref[...], w_ref[0].astype(BF16))


P_IN = 13 * MIX_W + RW_W_RANK + RW_A_RANK + RW_G_RANK + 4 * ML_HEADS


def _inproj_src_row(j):
    a = 9
    narrow = RW_W_RANK + RW_A_RANK + RW_G_RANK
    g = 16
    return g * jnp.where(j < a, j * (MIX_W // g),
                         jnp.where(j < CB_SMALL, j * (MIX_W // g) + narrow // g,
                                   jnp.where(j == CB_SMALL, a * MIX_W // g, (P_IN - MIX_W) // g)))


def _inproj(x, norm_g, mod, w_in_t, layer):
    tm, tn = 1024, MIX_W
    return pl.pallas_call(
        _inproj_kernel,
        out_shape=jax.ShapeDtypeStruct((N_TOK, P_PAD), F32),
        grid=(N_TOK // tm, P_BLOCKS),
        in_specs=[pl.BlockSpec((tm, D_MODEL), lambda i, j: (i, 0)),
                  pl.BlockSpec((None, 1, D_MODEL), lambda i, j: (layer, 0, 0)),
                  _mod_spec(layer, 0, tm), _mod_spec(layer, 1, tm),
                  pl.BlockSpec((pl.Element(1), pl.Element(tn), pl.Element(D_MODEL)),
                               lambda i, j: (layer, _inproj_src_row(j), 0))],
        out_specs=pl.BlockSpec((tm, tn), lambda i, j: (i, j)),
        scratch_shapes=[pltpu.VMEM((tm, D_MODEL), BF16)],
        compiler_params=_params(("parallel", "arbitrary")),
        name="inproj",
    )(x, norm_g.reshape(DEPTH, 1, D_MODEL), mod, mod, w_in_t)


def _conv_mix(b, c, h, w):
    u = c * h
    n = u.shape[0]
    row = lax.broadcasted_iota(jnp.int32, u.shape, 0)
    prev = jnp.where(row == 0, 0.0, pltpu.roll(u, 1, axis=0))
    nxt = jnp.where(row == n - 1, 0.0, pltpu.roll(u, n - 1, axis=0))
    return b * (prev * w[0:1] + u * w[1:2] + nxt * w[2:3])


def _head_rms(x, g):
    return x * lax.rsqrt(jnp.mean(x * x, axis=-1, keepdims=True) + EPS) * g


def _attn_prompt_kernel(cb_ref, cc_ref, ch_ref, q_ref, k_ref, v_ref, cw_ref, qg_ref, kg_ref,
                        ya_ref, yb_ref, nk_ref, nv_ref):
    ya_ref[...] = _conv_mix(cb_ref[...], cc_ref[...], ch_ref[...], cw_ref[...]).astype(ya_ref.dtype)
    sls = [slice(h * NA_DH, (h + 1) * NA_DH) for h in range(NA_HEADS)]
    qn = [_head_rms(q_ref[:, sl], qg_ref[...]) * NA_SCALE for sl in sls]
    kn = [_head_rms(k_ref[:, sl], kg_ref[...]) for sl in sls]
    vh = [v_ref[:, sl] for sl in sls]
    s = [_mm_nt(q.astype(BF16), k.astype(BF16)) for q, k in zip(qn, kn)]
    p = [jnp.exp(x - jnp.max(x, axis=-1, keepdims=True)) for x in s]
    o = [_mm(x.astype(BF16), v.astype(BF16)) / jnp.sum(x, axis=-1, keepdims=True) for x, v in zip(p, vh)]
    yb_ref[...] = jnp.concatenate(o, axis=1).astype(yb_ref.dtype)
    for h in range(NA_HEADS):
        nk_ref[h] = kn[h]
        nv_ref[h] = vh[h]


def _u_spec(rows, col_block, row_off_blocks=0):
    return pl.BlockSpec((rows, MIX_W), lambda b: (b + row_off_blocks, col_block))


def _attn_prompt(u, conv_w, q_g, k_g, layer, n_seq=BATCH, seq=SEQ):
    lw = lambda shape: pl.BlockSpec((None,) + shape, lambda b: (layer,) + (0,) * len(shape))
    y_spec = pl.BlockSpec((seq, MIX_W), lambda b: (b, 0))
    kv_spec = pl.BlockSpec((None, NA_HEADS, seq, NA_DH), lambda b: (b, 0, 0, 0))
    return pl.pallas_call(
        _attn_prompt_kernel,
        out_shape=(jax.ShapeDtypeStruct((n_seq * seq, MIX_W), BF16),) * 2
        + (jax.ShapeDtypeStruct((n_seq, NA_HEADS, seq, NA_DH), F32),) * 2,
        grid=(n_seq,),
        in_specs=[_u_spec(seq, cb) for cb in (CB_CVB, CB_CVC, CB_CVH, CB_NAQ, CB_NAK, CB_NAV)]
        + [lw((CONV_K, MIX_W)), lw((1, NA_DH)), lw((1, NA_DH))],
        out_specs=(y_spec, y_spec, kv_spec, kv_spec),
        compiler_params=_params(("parallel",)),
        name="attn_prompt",
    )(u, u, u, u, u, u, conv_w, q_g.reshape(DEPTH, 1, NA_DH), k_g.reshape(DEPTH, 1, NA_DH))


def _na_kernel(cb_ref, cc_ref, ch_ref, q_ref, k_ref, v_ref, kc_ref, vc_ref, tw_ref, cos_ref, sin_ref,
               perm_ref, cw_ref, qg_ref, kg_ref, ya_ref, yb_ref, qs_ref, ks_ref, tws_ref):
    rows = DEC_SEQ // GRID_W
    wr = min(NA_WIN_R, rows)
    nw = wr * GRID_W
    ya_ref[...] = _conv_mix(cb_ref[...], cc_ref[...], ch_ref[...], cw_ref[...]).astype(ya_ref.dtype)
    cos, sin, perm = cos_ref[...], sin_ref[...], perm_ref[...]

    def rope(x):
        return x * cos + _mm(x, perm, HI) * sin

    for h in range(NA_HEADS):
        sl = slice(h * NA_DH, (h + 1) * NA_DH)
        qs_ref[...] = (rope(_head_rms(q_ref[:, sl], qg_ref[...])) * NA_SCALE).astype(BF16)
        ks_ref[...] = rope(_head_rms(k_ref[:, sl], kg_ref[...])).astype(BF16)
        kch = kc_ref[h].astype(BF16)
        vch = vc_ref[h].astype(BF16)
        for p in range(wr):
            tws_ref[p] = jnp.concatenate([tw_ref[h, j - p + NA_WIN_R - 1] for j in range(wr)], axis=1)

        def row_step(r, carry):
            rs = jnp.clip(r - wr // 2, 0, rows - wr)
            q0 = pl.multiple_of(r * GRID_W, GRID_W)
            k0 = pl.multiple_of(rs * GRID_W, GRID_W)
            q_r = qs_ref[pl.ds(q0, GRID_W), :]
            s_w = _mm_nt(q_r, ks_ref[pl.ds(k0, nw), :]) + tws_ref[r - rs]
            s_c = _mm_nt(q_r, kch)
            m = jnp.maximum(jnp.max(s_w, axis=-1, keepdims=True), jnp.max(s_c, axis=-1, keepdims=True))
            p_w = jnp.exp(s_w - m)
            p_c = jnp.exp(s_c - m)
            den = jnp.sum(p_w, axis=-1, keepdims=True) + jnp.sum(p_c, axis=-1, keepdims=True)
            v_w = v_ref[pl.ds(k0, nw), sl].astype(BF16)
            o = (_mm(p_w.astype(BF16), v_w) + _mm(p_c.astype(BF16), vch)) / den
            yb_ref[pl.ds(q0, GRID_W), sl] = o.astype(yb_ref.dtype)
            return carry

        lax.fori_loop(0, rows, row_step, 0)


def _na_tables(rpb):
    rows = DEC_SEQ // GRID_W
    wr = min(NA_WIN_R, rows)
    qc = np.arange(GRID_W)
    kc = np.arange(GRID_W)
    wstart = np.clip(qc - NA_WIN_C // 2, 0, GRID_W - NA_WIN_C)
    colmask = (kc[None, :] >= wstart[:, None]) & (kc[None, :] < wstart[:, None] + NA_WIN_C)
    dc = np.clip(kc[None, :] - qc[:, None], -(NA_WIN_C - 1), NA_WIN_C - 1) + NA_WIN_C - 1
    bias = rpb[:, :, :, dc]
    return jnp.where(colmask[None, None, None], bias, NEG_INF)


def _rope_tables():
    t = np.arange(DEC_SEQ)
    quarter = NA_DH // 4
    freq = ROPE_THETA ** (-np.arange(quarter, dtype=np.float32) / quarter)
    ang_r = (t // GRID_W).astype(np.float32)[:, None] * freq
    ang_c = (t % GRID_W).astype(np.float32)[:, None] * freq
    cos = np.concatenate([np.cos(ang_r), np.cos(ang_r), np.cos(ang_c), np.cos(ang_c)], axis=-1)
    sin = np.concatenate([-np.sin(ang_r), np.sin(ang_r), -np.sin(ang_c), np.sin(ang_c)], axis=-1)
    src = np.concatenate([np.arange(quarter) + quarter, np.arange(quarter),
                          np.arange(quarter) + 3 * quarter, np.arange(quarter) + 2 * quarter])
    perm = np.zeros((NA_DH, NA_DH), np.float32)
    perm[src, np.arange(NA_DH)] = 1.0
    return cos.astype(np.float32), sin.astype(np.float32), perm


def _na_sample(u, cache_k, cache_v, tw, conv_w, q_g, k_g, layer, n_seq=DEC_BATCH, row_off=N_PROMPT // DEC_SEQ):
    cos, sin, perm = _rope_tables()
    lw = lambda shape: pl.BlockSpec((None,) + shape, lambda b: (layer,) + (0,) * len(shape))
    y_spec = pl.BlockSpec((DEC_SEQ, MIX_W), lambda b: (b, 0))
    c_spec = pl.BlockSpec((None, None, NA_HEADS, PAST_LEN, NA_DH), lambda b: (b, layer, 0, 0, 0))
    wr = min(NA_WIN_R, DEC_SEQ // GRID_W)
    n_off = 2 * NA_WIN_R - 1
    return pl.pallas_call(
        _na_kernel,
        out_shape=(jax.ShapeDtypeStruct((n_seq * DEC_SEQ, MIX_W), BF16),) * 2,
        grid=(n_seq,),
        in_specs=[_u_spec(DEC_SEQ, cb, row_off) for cb in (CB_CVB, CB_CVC, CB_CVH, CB_NAQ, CB_NAK, CB_NAV)]
        + [c_spec, c_spec, lw((NA_HEADS, n_off, GRID_W, GRID_W)),
           _full((DEC_SEQ, NA_DH)), _full((DEC_SEQ, NA_DH)), _full((NA_DH, NA_DH)),
           lw((CONV_K, MIX_W)), lw((1, NA_DH)), lw((1, NA_DH))],
        out_specs=(y_spec, y_spec),
        scratch_shapes=[pltpu.VMEM((DEC_SEQ, NA_DH), BF16), pltpu.VMEM((DEC_SEQ, NA_DH), BF16),
                        pltpu.VMEM((wr, GRID_W, wr * GRID_W), F32)],
        compiler_params=_params(("parallel",)),
        name="na_sample",
    )(u, u, u, u, u, u, cache_k, cache_v, tw, jnp.asarray(cos), jnp.asarray(sin), jnp.asarray(perm),
      conv_w, q_g.reshape(DEPTH, 1, NA_DH), k_g.reshape(DEPTH, 1, NA_DH))


def _seg_ones(width, seg):
    a = lax.broadcasted_iota(jnp.int32, (width, width), 0) // seg
    b = lax.broadcasted_iota(jnp.int32, (width, width), 1) // seg
    return (a == b).astype(F32)


def _split2(x):
    hi = x.astype(BF16)
    return hi, (x - hi.astype(F32)).astype(BF16)


def _mm1(a, b):
    return _mm(a.astype(BF16), b.astype(BF16))


def _mm3(a, b, nt=False):
    dot = _mm_nt if nt else _mm
    a_hi, a_lo = _split2(a)
    b_hi, b_lo = _split2(b)
    return dot(a_hi, b_lo) + dot(a_lo, b_hi) + dot(a_hi, b_hi)


def _split3(x):
    x0 = x.astype(BF16)
    r1 = x - x0.astype(F32)
    x1 = r1.astype(BF16)
    return x0, x1, (r1 - x1.astype(F32)).astype(BF16)


def _mm_exact_lhs(mask_b, x):
    x0, x1, x2 = _split3(x)
    return _mm(mask_b, x2) + _mm(mask_b, x1) + _mm(mask_b, x0)


def _mm_exact_rhs(x, mask_b):
    x0, x1, x2 = _split3(x)
    return _mm(x2, mask_b) + _mm(x1, mask_b) + _mm(x0, mask_b)


def _rwkv_kernel(*refs, seq, has_s0, emit_state):
    it = iter(refs)
    r_ref, k_ref, v_ref, sm_ref = (next(it) for _ in range(4))
    (w0_ref, wup_ref, a0_ref, aup_ref, gup_ref, kkp_ref, kap_ref, rkp_ref, lng_ref, lnb_ref) = (
        next(it) for _ in range(10))
    mask_ref = next(it)
    s0_ref = next(it) if has_s0 else None
    y_ref = next(it)
    so_ref = next(it) if emit_state else None
    kk_ref, lw_ref, ka_ref, kd_ref, coef_ref, ysp_ref = (next(it) for _ in range(6))

    dh = RW_DH
    pw = 2 * dh
    nc = seq // CHUNK
    seg_b = _seg_ones(pw, dh).astype(BF16)

    r = r_ref[...]
    k = k_ref[...]
    sm = sm_ref[...]
    wl = jnp.tanh(sm[:, SM_WL:SM_WL + RW_W_RANK])
    al = sm[:, SM_AL:SM_AL + RW_A_RANK]
    kk = k * kkp_ref[...]
    kk = kk * lax.rsqrt(_mm_exact_rhs(kk * kk, seg_b) + EPS)
    kk_ref[...] = kk
    for z in range(2):
        lw_ref[z] = -RW_DECAY_SCALE * _sigmoid(w0_ref[z:z + 1, :] + _mm3(wl, wup_ref[z]))
        a = _sigmoid(a0_ref[z:z + 1, :] + _mm3(al, aup_ref[z]))
        ka_ref[z] = kk * a
        kd_ref[z] = k * (1.0 + (a - 1.0) * kap_ref[...])

    def phase1(c2, carry):
        incl, strict, ti, si = _tri_masks(CHUNK)
        incl_b = (mask_ref[0], mask_ref[1])
        incl2 = tuple(jnp.concatenate([m, m], axis=1) for m in incl)
        diag_blk = (ti // SUB == si // SUB).astype(F32)
        eye = (ti == si).astype(F32)
        zero_blk = jnp.zeros((CHUNK, dh), F32)
        inst = []
        for cc in range(P1_CHUNKS):
            rows = pl.ds(pl.multiple_of((c2 * P1_CHUNKS + cc) * CHUNK, CHUNK), CHUNK)
            vc, rc, kkc = v_ref[rows, :], r_ref[rows, :], kk_ref[rows, :]
            for z in range(2):
                lwc = lw_ref[z, rows, :]
                cum = _mm_exact_lhs(incl_b[z], lwc)
                tot = cum[CHUNK - 1:CHUNK] if z == 0 else cum[0:1]
                e_neg = jnp.exp(-cum)
                dec = jnp.exp(tot - cum)
                e_tot = jnp.exp(tot)
                kac, kdc = ka_ref[z, rows, :], kd_ref[z, rows, :]
                rt = rc * jnp.exp(cum)
                kt = kkc * jnp.exp(cum - lwc)
                at, kdt, ah, kh = kac * e_neg, kdc * e_neg, kac * dec, kdc * dec
                for j in range(2):
                    sl = slice(j * dh, (j + 1) * dh)
                    inst.append((z, kt[:, sl], rt[:, sl], vc[:, sl], at[:, sl], kdt[:, sl], ah[:, sl], kh[:, sl],
                                 e_tot[:, sl]))
        zs = [i[0] for i in inst]
        kt_h, rt_h, v_h = [i[1] for i in inst], [i[2] for i in inst], [i[3] for i in inst]
        each = lambda f, *xs: [f(*a) for a in zip(*xs)]
        aa = each(lambda i: _mm3(jnp.concatenate([i[1], i[2]], axis=0),
                                 jnp.concatenate([i[4], i[5]], axis=0), nt=True), inst)
        low = each(lambda a, z: a[0:CHUNK, 0:CHUNK] * strict[z], aa, zs)
        a_kk = each(lambda a, z: a[0:CHUNK, CHUNK:] * strict[z], aa, zs)
        a_r = each(lambda a, z: a[CHUNK:, :] * incl2[z], aa, zs)
        akv = each(_mm1, a_kk, v_h)
        ld = each(lambda x: x * diag_blk, low)
        lo = each(lambda x, y: x - y, low, ld)
        l2 = each(_mm1, ld, ld)
        l4 = each(_mm1, l2, l2)
        l8 = each(_mm1, l4, l4)
        td = each(lambda x: eye - x, ld)
        for lp in (l2, l4, l8):
            td = each(lambda t, p: t + _mm1(t, p), td, lp)
        x0 = each(lambda t, a, b, c_: _mm1(t, jnp.concatenate([a, b, c_], axis=1)), td, lo, kt_h, akv)
        wm = each(lambda x: x[:, 0:CHUNK], x0)
        pq0 = each(lambda x: x[:, CHUNK:], x0)
        pq = pq0
        for _ in range(CHUNK // SUB - 1):
            pq = each(lambda p0, w, p: p0 - _mm1(w, p), pq0, wm, pq)
        ryc = each(lambda a, p, v_: _mm3(a, jnp.concatenate(
            [p, jnp.concatenate([zero_blk, -v_], axis=1)], axis=0)), a_r, pq, v_h)
        gh = each(lambda i, p: _mm3(jnp.concatenate([i[6], i[7]], axis=1).T,
                                    jnp.concatenate([p, i[3]], axis=1)), inst, pq)
        pieces = []
        for i, g, ry, rt_ in zip(inst, gh, ryc, rt_h):
            g_t = eye * i[8] - g[0:dh, 0:dh]
            h_t = g[dh:, 2 * dh:] - g[0:dh, dh:2 * dh]
            pieces += [g_t, rt_ - ry[:, 0:dh], h_t, -ry[:, dh:]]
        per_chunk = len(pieces) // P1_CHUNKS
        for cc in range(P1_CHUNKS):
            coef_ref[c2 * P1_CHUNKS + cc] = jnp.concatenate(pieces[cc * per_chunk:(cc + 1) * per_chunk], axis=0)
        return carry

    lax.fori_loop(0, nc // P1_CHUNKS, phase1, 0)

    m_init = tuple((s0_ref[z, j].T if has_s0 else jnp.zeros((dh, dh), F32)) for z in range(2) for j in range(2))

    def phase2(ci, ms):
        new_ms, ys = [], []
        for z in range(2):
            c = ci if z == 0 else nc - 1 - ci
            for j in range(2):
                base = (z * 2 + j) * 4 * dh
                out = (_mm3(coef_ref[c, base:base + 2 * dh, :], ms[z * 2 + j])
                       + coef_ref[c, base + 2 * dh:base + 4 * dh, :])
                new_ms.append(out[0:dh])
                ys.append(out[dh:])
        ysp_ref[ci] = jnp.concatenate(ys, axis=1)
        return tuple(new_ms)

    m_fin = lax.fori_loop(0, nc, phase2, m_init)

    if emit_state:
        for z in range(2):
            for j in range(2):
                so_ref[z, j] = m_fin[z * 2 + j].T

    y = jnp.concatenate([ysp_ref[c, :, 0:pw] + ysp_ref[nc - 1 - c, :, pw:] for c in range(nc)], axis=0)
    mu = _mm_exact_rhs(y, seg_b) * (1.0 / dh)
    yc = y - mu
    var = _mm_exact_rhs(yc * yc, seg_b) * (1.0 / dh)
    yn = yc * lax.rsqrt(var + RW_GN_EPS) * lng_ref[...] + lnb_ref[...]
    v = v_ref[...]
    bonus = _mm_exact_rhs(r * k * rkp_ref[...], seg_b) * v
    g = _mm3(_sigmoid(sm[:, SM_GL:SM_GL + RW_G_RANK]), gup_ref[...])
    y_ref[...] = ((yn + bonus) * g).astype(y_ref.dtype)


def _rwkv(u, p, layer, n_seq, seq, row_off, s0=None, emit_state=False):
    n_pairs = RW_HEADS // 2
    pw = 2 * RW_DH
    bpc = MIX_W // pw
    lw = lambda shape: pl.BlockSpec((None,) + shape, lambda b, hp: (layer,) + (0,) * (len(shape) - 1) + (hp,))
    row = lambda a: a.reshape(DEPTH, 1, MIX_W)
    u_pair = lambda cb: pl.BlockSpec((seq, pw), lambda b, hp: (b + row_off, cb * bpc + hp))
    in_specs = [u_pair(CB_RWR), u_pair(CB_RWK), u_pair(CB_RWV),
                pl.BlockSpec((seq, MIX_W), lambda b, hp: (b + row_off, CB_SMALL)),
                lw((2, pw)), lw((2, RW_W_RANK, pw)), lw((2, pw)), lw((2, RW_A_RANK, pw)),
                lw((RW_G_RANK, pw))] + [lw((1, pw))] * 5 + [_full((2, CHUNK, CHUNK))]
    t_idx = np.arange(CHUNK)
    incl_masks = np.stack([t_idx[None, :] <= t_idx[:, None], t_idx[None, :] >= t_idx[:, None]])
    args = [u, u, u, u, p['rw_w0'], p['rw_w_up'], p['rw_a0'], p['rw_a_up'], p['rw_g_up'],
            row(p['rw_k_k']), row(p['rw_k_a']), row(p['rw_r_k']), row(p['rw_ln_g']), row(p['rw_ln_b']),
            jnp.asarray(incl_masks, BF16)]
    st_blk = (2, 2, RW_DH, RW_DH)
    if s0 is not None:
        in_specs.append(pl.BlockSpec((None, None) + st_blk, lambda b, hp: (b, layer, 0, hp, 0, 0)))
        args.append(s0)
    out_shape = [jax.ShapeDtypeStruct((n_seq * seq, MIX_W), BF16)]
    out_specs = [pl.BlockSpec((seq, pw), lambda b, hp: (b, hp))]
    if emit_state:
        out_shape.append(jax.ShapeDtypeStruct((n_seq, 2, RW_HEADS, RW_DH, RW_DH), F32))
        out_specs.append(pl.BlockSpec((None,) + st_blk, lambda b, hp: (b, 0, hp, 0, 0)))
    nc = seq // CHUNK
    tok = lambda n: pltpu.VMEM((n, seq, pw) if n else (seq, pw), F32)
    return pl.pallas_call(
        functools.partial(_rwkv_kernel, seq=seq, has_s0=s0 is not None, emit_state=emit_state),
        out_shape=tuple(out_shape), grid=(n_seq, n_pairs), in_specs=in_specs, out_specs=tuple(out_specs),
        scratch_shapes=[tok(0), tok(2), tok(2), tok(2),
                        pltpu.VMEM((nc, 16 * RW_DH, RW_DH), F32), pltpu.VMEM((nc, CHUNK, 2 * pw), F32)],
        compiler_params=_params(("parallel", "parallel")),
        name=f"rwkv_{seq}",
    )(*args)


def _rwkv_kernel_old(*refs, seq, has_s0, emit_state):
    it = iter(refs)
    r_ref, k_ref, v_ref, sm_ref = (next(it) for _ in range(4))
    (w0_ref, wup_ref, a0_ref, aup_ref, gup_ref, kkp_ref, kap_ref, rkp_ref, lng_ref, lnb_ref) = (
        next(it) for _ in range(10))
    s0_ref = next(it) if has_s0 else None
    y_ref = next(it)
    so_ref = next(it) if emit_state else None
    rp_ref, vp_ref, kk_ref, lw_ref, ka_ref, kd_ref, ys_ref, st_ref = (next(it) for _ in range(8))

    n_pairs = RW_HEADS // 2
    pw = 2 * RW_DH
    nc = seq // CHUNK
    seg = _seg_ones(MIX_W, RW_DH)

    r = r_ref[...]
    k = k_ref[...]
    sm = sm_ref[...]
    wl = jnp.tanh(sm[:, SM_WL:SM_WL + RW_W_RANK])
    al = sm[:, SM_AL:SM_AL + RW_A_RANK]
    kk = k * kkp_ref[...]
    kk = kk * lax.rsqrt(_mm(kk * kk, seg, HI) + EPS)
    for hp in range(n_pairs):
        psl = slice(hp * pw, (hp + 1) * pw)
        rp_ref[hp] = r[:, psl]
        vp_ref[hp] = v_ref[:, psl]
        kk_ref[hp] = kk[:, psl]
    for z in range(2):
        w_pre = w0_ref[z:z + 1, :] + _mm(wl, wup_ref[z], HI)
        lw = -RW_DECAY_SCALE * _sigmoid(w_pre)
        a = _sigmoid(a0_ref[z:z + 1, :] + _mm(al, aup_ref[z], HI))
        ka = kk * a
        kd = k * (1.0 + (a - 1.0) * kap_ref[...])
        for hp in range(n_pairs):
            psl = slice(hp * pw, (hp + 1) * pw)
            lw_ref[z * n_pairs + hp] = lw[:, psl]
            ka_ref[z * n_pairs + hp] = ka[:, psl]
            kd_ref[z * n_pairs + hp] = kd[:, psl]
    for i in range(2 * RW_HEADS):
        st_ref[i] = s0_ref[i // RW_HEADS, i % RW_HEADS] if has_s0 else jnp.zeros((RW_DH, RW_DH), F32)

    incl, strict, ti, si = _tri_masks(CHUNK)
    diag_blk = (ti // SUB == si // SUB).astype(F32)
    eye = (ti == si).astype(F32)

    def chunk_pair(ci, hp):
        for z in range(2):
            c = ci if z == 0 else nc - 1 - ci
            rows = pl.ds(pl.multiple_of(c * CHUNK, CHUNK), CHUNK)
            lwc = lw_ref[z * n_pairs + hp, rows, :]
            cum = _mm(incl[z], lwc, HI)
            tot = cum[CHUNK - 1:CHUNK] if z == 0 else cum[0:1]
            e_neg = jnp.exp(-cum)
            dec = jnp.exp(tot - cum)
            e_tot = jnp.exp(tot)
            kac = ka_ref[z * n_pairs + hp, rows, :]
            kdc = kd_ref[z * n_pairs + hp, rows, :]
            rt = rp_ref[hp, rows, :] * jnp.exp(cum)
            kt = kk_ref[hp, rows, :] * jnp.exp(cum - lwc)
            at, kdt = kac * e_neg, kdc * e_neg
            ah, kh = kac * dec, kdc * dec
            vc = vp_ref[hp, rows, :]
            for j in range(2):
                sl = slice(j * RW_DH, (j + 1) * RW_DH)
                si_ = z * RW_HEADS + hp * 2 + j
                s0 = st_ref[si_]
                kt_h, rt_h, v_h = kt[:, sl], rt[:, sl], vc[:, sl]
                low = _mm_nt(kt_h, at[:, sl], HI) * strict[z]
                a_kk = _mm_nt(kt_h, kdt[:, sl], HI) * strict[z]
                a_ra = _mm_nt(rt_h, at[:, sl], HI) * incl[z]
                a_rk = _mm_nt(rt_h, kdt[:, sl], HI) * incl[z]
                ld = low * diag_blk
                lo = low - ld
                l2 = _mm(ld, ld, HI)
                l4 = _mm(l2, l2, HI)
                l8 = _mm(l4, l4, HI)
                tinv = eye - ld
                tinv = tinv + _mm(tinv, l2, HI)
                tinv = tinv + _mm(tinv, l4, HI)
                tinv = tinv + _mm(tinv, l8, HI)
                rhs = -(_mm_nt(kt_h, s0, HI) + _mm(a_kk, v_h, HI))
                cv = _mm(tinv, rhs, HI)
                wm = _mm(tinv, lo, HI)
                uu = cv
                for _ in range(CHUNK // SUB - 1):
                    uu = cv - _mm(wm, uu, HI)
                y = _mm_nt(rt_h, s0, HI) + _mm(a_ra, uu, HI) + _mm(a_rk, v_h, HI)
                ys_ref[z * n_pairs + hp, rows, sl] = y
                st_ref[si_] = s0 * e_tot[:, sl] + _mm_tn(uu, ah[:, sl], HI) + _mm_tn(v_h, kh[:, sl], HI)

    def chunk_step(ci, carry):
        def pair_step(hp, carry2):
            chunk_pair(ci, hp)
            return carry2
        return lax.fori_loop(0, n_pairs, pair_step, carry)

    lax.fori_loop(0, nc, chunk_step, 0)

    if emit_state:
        for i in range(2 * RW_HEADS):
            so_ref[i // RW_HEADS, i % RW_HEADS] = st_ref[i]

    g = _mm(_sigmoid(sm[:, SM_GL:SM_GL + RW_G_RANK]), gup_ref[...], HI)
    for h in range(RW_HEADS):
        hp, j = divmod(h, 2)
        sl = slice(j * RW_DH, (j + 1) * RW_DH)
        hsl = slice(h * RW_DH, (h + 1) * RW_DH)
        y = ys_ref[hp, :, sl] + ys_ref[n_pairs + hp, :, sl]
        mu = jnp.mean(y, axis=-1, keepdims=True)
        var = jnp.mean(jnp.square(y - mu), axis=-1, keepdims=True)
        yn = (y - mu) * lax.rsqrt(var + RW_GN_EPS) * lng_ref[:, hsl] + lnb_ref[:, hsl]
        r_h, k_h, v_h = rp_ref[hp, :, sl], k_ref[:, hsl], vp_ref[hp, :, sl]
        bonus = jnp.sum(r_h * k_h * rkp_ref[:, hsl], axis=-1, keepdims=True) * v_h
        y_ref[:, hsl] = ((yn + bonus) * g[:, hsl]).astype(y_ref.dtype)


def _rwkv_old(u, p, layer, n_seq, seq, row_off, s0=None, emit_state=False):
    lw = lambda shape: pl.BlockSpec((None,) + shape, lambda b: (layer,) + (0,) * len(shape))
    row = lambda a: a.reshape(DEPTH, 1, MIX_W)
    in_specs = [_u_spec(seq, cb, row_off) for cb in (CB_RWR, CB_RWK, CB_RWV, CB_SMALL)] + [
        lw((2, MIX_W)), lw((2, RW_W_RANK, MIX_W)), lw((2, MIX_W)), lw((2, RW_A_RANK, MIX_W)),
        lw((RW_G_RANK, MIX_W))] + [lw((1, MIX_W))] * 5
    args = [u, u, u, u, p['rw_w0'], p['rw_w_up'], p['rw_a0'], p['rw_a_up'], p['rw_g_up'],
            row(p['rw_k_k']), row(p['rw_k_a']), row(p['rw_r_k']), row(p['rw_ln_g']), row(p['rw_ln_b'])]
    st_shape = (2, RW_HEADS, RW_DH, RW_DH)
    if s0 is not None:
        in_specs.append(pl.BlockSpec((None, None) + st_shape, lambda b: (b, layer, 0, 0, 0, 0)))
        args.append(s0)
    out_shape = [jax.ShapeDtypeStruct((n_seq * seq, MIX_W), BF16)]
    out_specs = [pl.BlockSpec((seq, MIX_W), lambda b: (b, 0))]
    if emit_state:
        out_shape.append(jax.ShapeDtypeStruct((n_seq,) + st_shape, F32))
        out_specs.append(pl.BlockSpec((None,) + st_shape, lambda b: (b, 0, 0, 0, 0)))
    n_pairs = RW_HEADS // 2
    pair = lambda n: pltpu.VMEM((n, seq, 2 * RW_DH), F32)
    return pl.pallas_call(
        functools.partial(_rwkv_kernel, seq=seq, has_s0=s0 is not None, emit_state=emit_state),
        out_shape=tuple(out_shape), grid=(n_seq,), in_specs=in_specs, out_specs=tuple(out_specs),
        scratch_shapes=[pair(n_pairs), pair(n_pairs), pair(n_pairs), pair(2 * n_pairs), pair(2 * n_pairs),
                        pair(2 * n_pairs), pair(2 * n_pairs), pltpu.VMEM((2 * RW_HEADS, RW_DH, RW_DH), F32)],
        compiler_params=_params(("parallel",)),
        name=f"rwkv_{seq}",
    )(*args)


def _log_sigmoid(x):
    return jnp.minimum(x, 0.0) - jnp.log(1.0 + jnp.exp(-jnp.abs(x)))


def _mlstm_kernel(*refs, seq, has_s0, emit_state):
    it = iter(refs)
    q_ref, k_ref, v_ref, o_ref, sm_ref, gr_ref, bc_ref, br_ref, ng_ref = (next(it) for _ in range(9))
    c0_ref, n0_ref, m0_ref = ((next(it), next(it), next(it)) if has_s0 else (None, None, None))
    y_ref = next(it)
    co_ref, no_ref, mo_ref = ((next(it), next(it), next(it)) if emit_state else (None, None, None))
    hs_ref, c_ref, n_ref, m_ref = (next(it) for _ in range(4))

    nc = seq // CHUNK
    n_st = 2 * ML_HEADS
    for i in range(n_st):
        z, h = divmod(i, ML_HEADS)
        c_ref[i] = c0_ref[z, h] if has_s0 else jnp.zeros((ML_DH, ML_DH), F32)
        n_ref[i] = n0_ref[z, h] if has_s0 else jnp.zeros((1, ML_DH), F32)
        m_ref[i] = m0_ref[z, h] if has_s0 else jnp.zeros((1, 1), F32)

    incl, _, ti, si = _tri_masks(CHUNK)
    before = ((si <= ti), (si >= ti))

    nh = ML_HEADS
    chains = [(z, h) for z in range(2) for h in range(nh)]
    each = lambda f, *xs: [f(*a) for a in zip(*xs)]
    zs = [z for z, _ in chains]

    def chunk_step(ci, carry):
        per_dir = []
        for z in range(2):
            c = ci if z == 0 else nc - 1 - ci
            rows = pl.ds(pl.multiple_of(c * CHUNK, CHUNK), CHUNK)
            g0 = SM_GATE + z * 2 * nh
            gc = sm_ref[rows, g0:g0 + 2 * nh] + bc_ref[:, z * 2 * nh:(z + 1) * 2 * nh]
            gr = gr_ref[c, z * 2 * nh:(z + 1) * 2 * nh, :] + br_ref[z * 2 * nh:(z + 1) * 2 * nh, :]
            b_cols = _mm(incl[z], _log_sigmoid(gc[:, nh:]), HI)
            b_rows = _mm(_log_sigmoid(gr[nh:]), incl[1 - z], HI)
            per_dir.append((rows, gc[:, :nh], b_cols, gr[:nh], b_rows))
        rows = [per_dir[z][0] for z, _ in chains]
        hsl = [slice(h * ML_DH, (h + 1) * ML_DH) for _, h in chains]
        i_col = [per_dir[z][1][:, h:h + 1] for z, h in chains]
        b_col = [per_dir[z][2][:, h:h + 1] for z, h in chains]
        i_row = [per_dir[z][3][h:h + 1] for z, h in chains]
        b_row = [per_dir[z][4][h:h + 1] for z, h in chains]
        b_last = each(lambda b, z: b[CHUNK - 1:CHUNK] if z == 0 else b[0:1], b_col, zs)
        m_old = [m_ref[i] for i in range(n_st)]
        cm = [c_ref[i] for i in range(n_st)]
        nv = [n_ref[i] for i in range(n_st)]
        qc = each(lambda r, s: q_ref[r, s] * (ML_DH ** -0.5), rows, hsl)
        kc = each(lambda r, s: k_ref[r, s], rows, hsl)
        vc = each(lambda r, s: v_ref[r, s], rows, hsl)
        qk = each(lambda q, k: _mm_nt(q.astype(BF16), k.astype(BF16)), qc, kc)
        qcm = each(lambda q, c_: _mm_nt(q.astype(BF16), c_.astype(BF16)), qc, cm)
        a_t = each(lambda b, m: b + m, b_col, m_old)
        dmat = each(lambda bc, brw, ir, z: jnp.where(before[z], bc - brw + ir, NEG_INF), b_col, b_row, i_row, zs)
        m_t = each(lambda a, d: jnp.maximum(a, jnp.max(d, axis=-1, keepdims=True)), a_t, dmat)
        s = each(lambda x, d, m: x * jnp.exp(d - m), qk, dmat, m_t)
        inter = each(lambda a, m: jnp.exp(a - m), a_t, m_t)
        sv = each(_mm1, s, vc)
        g_col = each(lambda bl, bc, ic: bl - bc + ic, b_last, b_col, i_col)
        a_l = each(lambda bl, m: bl + m, b_last, m_old)
        m_new = each(lambda a, g: jnp.maximum(a, jnp.max(g, axis=0, keepdims=True)), a_l, g_col)
        wgt = each(lambda g, m: jnp.exp(g - m), g_col, m_new)
        vk = each(lambda v, w, k: _mm1((v * w).T, k), vc, wgt, kc)
        decay = each(lambda a, m: jnp.exp(a - m), a_l, m_new)
        num = each(lambda i_, x, y: i_ * x + y, inter, qcm, sv)
        den = each(lambda i_, q, n_, s_: i_ * jnp.sum(q * n_, axis=-1, keepdims=True)
                   + jnp.sum(s_, axis=-1, keepdims=True), inter, qc, nv, s)
        hh = each(lambda n_, d, m: n_ / jnp.maximum(jnp.abs(d), jnp.exp(-m)), num, den, m_t)
        for z in range(2):
            hs_ref[z, per_dir[z][0], :] = jnp.concatenate(hh[z * nh:(z + 1) * nh], axis=1)
        for i in range(n_st):
            c_ref[i] = decay[i] * cm[i] + vk[i]
            n_ref[i] = decay[i] * nv[i] + jnp.sum(wgt[i] * kc[i], axis=0, keepdims=True)
            m_ref[i] = m_new[i]
        return carry

    lax.fori_loop(0, nc, chunk_step, 0)

    if emit_state:
        for i in range(n_st):
            z, h = divmod(i, ML_HEADS)
            co_ref[z, h] = c_ref[i]
            no_ref[z, h] = n_ref[i]
            mo_ref[z, h] = m_ref[i]

    for h in range(ML_HEADS):
        hsl = slice(h * ML_DH, (h + 1) * ML_DH)
        hn = _head_rms(hs_ref[0, :, hsl] + hs_ref[1, :, hsl], ng_ref[:, hsl])
        y_ref[:, hsl] = (_sigmoid(o_ref[:, hsl]) * hn).astype(y_ref.dtype)


def _mlstm(u, gate_rows, p, layer, n_seq, seq, row_off, state=None, emit_state=False):
    lw = lambda shape: pl.BlockSpec((None,) + shape, lambda b: (layer,) + (0,) * len(shape))
    nc = seq // CHUNK
    n_gate = 4 * ML_HEADS
    in_specs = [_u_spec(seq, cb, row_off) for cb in (CB_MLQ, CB_MLK, CB_MLV, CB_MLO, CB_GATE)] + [
        pl.BlockSpec((None, nc, n_gate, CHUNK), lambda b: (b, 0, 0, 0)),
        lw((1, n_gate)), lw((n_gate, 1)), lw((1, MIX_W))]
    args = [u, u, u, u, u, gate_rows, p['ml_gate_b'].reshape(DEPTH, 1, n_gate),
            p['ml_gate_b'].reshape(DEPTH, n_gate, 1), p['ml_norm_g'].reshape(DEPTH, 1, MIX_W)]
    c_shape, n_shape, m_shape = (2, ML_HEADS, ML_DH, ML_DH), (2, ML_HEADS, 1, ML_DH), (2, ML_HEADS, 1, 1)
    if state is not None:
        c0, n0, m0 = state
        for a, shp in ((c0, c_shape), (n0, n_shape), (m0, m_shape)):
            in_specs.append(pl.BlockSpec((None, None) + shp, lambda b: (b, layer, 0, 0, 0, 0)))
            args.append(a.reshape(a.shape[:2] + shp))
    out_shape = [jax.ShapeDtypeStruct((n_seq * seq, MIX_W), BF16)]
    out_specs = [pl.BlockSpec((seq, MIX_W), lambda b: (b, 0))]
    if emit_state:
        for shp in (c_shape, n_shape, m_shape):
            out_shape.append(jax.ShapeDtypeStruct((n_seq,) + shp, F32))
            out_specs.append(pl.BlockSpec((None,) + shp, lambda b: (b, 0, 0, 0, 0)))
    n_st = 2 * ML_HEADS
    return pl.pallas_call(
        functools.partial(_mlstm_kernel, seq=seq, has_s0=state is not None, emit_state=emit_state),
        out_shape=tuple(out_shape), grid=(n_seq,), in_specs=in_specs, out_specs=tuple(out_specs),
        scratch_shapes=[pltpu.VMEM((2, seq, MIX_W), F32), pltpu.VMEM((n_st, ML_DH, ML_DH), F32),
                        pltpu.VMEM((n_st, 1, ML_DH), F32), pltpu.VMEM((n_st, 1, 1), F32)],
        compiler_params=_params(("parallel",)),
        name=f"mlstm_{seq}",
    )(*args)


def _outproj_kernel(ya_ref, yb_ref, yc_ref, yd_ref, w_ref, x_ref, g_ref, o_ref):
    acc = _mm(ya_ref[...], w_ref[0:MIX_W, :].astype(BF16))
    for i, y_ref in enumerate((yb_ref, yc_ref, yd_ref), start=1):
        acc += _mm(y_ref[...], w_ref[i * MIX_W:(i + 1) * MIX_W, :].astype(BF16))
    o_ref[...] = x_ref[...] + g_ref[...] * acc


def _outproj(ys, w_out_b, x, mod, layer):
    tm, tn = 1024, 1024
    y_spec = pl.BlockSpec((tm, MIX_W), lambda i, j: (i, 0))
    return pl.pallas_call(
        _outproj_kernel,
        out_shape=jax.ShapeDtypeStruct((N_TOK, D_MODEL), F32),
        grid=(N_TOK // tm, D_MODEL // tn),
        in_specs=[y_spec] * 4 + [pl.BlockSpec((None, D_MODEL, tn), lambda i, j: (layer, 0, j)),
                                 pl.BlockSpec((tm, tn), lambda i, j: (i, j)),
                                 _mod_spec(layer, 2, tm, tn, with_j=True)],
        out_specs=pl.BlockSpec((tm, tn), lambda i, j: (i, j)),
        compiler_params=_params(("parallel", "parallel")),
        name="outproj",
    )(*ys, w_out_b, x, mod)


def _router_kernel(x_ref, g_ref, sh_ref, sc_ref, rw_ref, rb_ref, xm_ref, gt_ref, ids_ref):
    xm = _modulated(x_ref[...], g_ref[...], sh_ref[...], sc_ref[...])
    xm_ref[...] = xm
    logits = _mm_nt(rw_ref[...], xm, HI)
    ex = jnp.exp(logits - jnp.max(logits, axis=0, keepdims=True))
    scores = ex / jnp.sum(ex, axis=0, keepdims=True)
    sel = scores + rb_ref[...]
    per = N_EXPERTS // N_EXPERT_GROUPS
    s = [sel[e:e + 1, :] for e in range(N_EXPERTS)]
    grp_score = []
    for g in range(N_EXPERT_GROUPS):
        a, b, c, d = s[per * g:per * (g + 1)]
        hi1, lo1, hi2, lo2 = jnp.maximum(a, b), jnp.minimum(a, b), jnp.maximum(c, d), jnp.minimum(c, d)
        grp_score.append(jnp.maximum(hi1, hi2) + jnp.maximum(jnp.minimum(hi1, hi2), jnp.maximum(lo1, lo2)))
    best = functools.reduce(jnp.maximum, grp_score)
    in_grp, taken = [], jnp.zeros_like(best)
    for g in range(N_EXPERT_GROUPS):
        hit = jnp.where(grp_score[g] == best, 1.0, 0.0) * (1.0 - taken)
        in_grp.append(hit)
        taken = taken + hit
    picked, flag = [], []
    for e in range(N_EXPERTS):
        g = e // per
        rank = jnp.zeros_like(best)
        for o in range(per * g, per * (g + 1)):
            if o < e:
                rank += jnp.where(s[o] >= s[e], 1.0, 0.0)
            elif o > e:
                rank += jnp.where(s[o] > s[e], 1.0, 0.0)
        flag.append(in_grp[g] * jnp.where(rank < 2.0, 1.0, 0.0))
        picked.append(flag[e] * scores[e:e + 1, :])
    total = functools.reduce(lambda x, y: x + y, picked)
    for e in range(N_EXPERTS):
        gt_ref[e:e + 1, :] = picked[e] / total
    lo_id = functools.reduce(jnp.minimum, [jnp.where(flag[e] > 0.0, float(e), float(N_EXPERTS)) for e in range(N_EXPERTS)])
    hi_id = functools.reduce(jnp.maximum, [jnp.where(flag[e] > 0.0, float(e), -1.0) for e in range(N_EXPERTS)])
    ids_ref[0:1, :] = lo_id
    ids_ref[1:2, :] = hi_id
    ids_ref[2:, :] = jnp.zeros((ids_ref.shape[0] - 2,) + lo_id.shape[1:], F32)


def _router(x, norm_g, mod, router_wt, router_b, layer):
    tm = 512
    return pl.pallas_call(
        _router_kernel,
        out_shape=(jax.ShapeDtypeStruct((N_TOK, D_MODEL), F32), jax.ShapeDtypeStruct((N_EXPERTS, N_TOK), F32),
                   jax.ShapeDtypeStruct((8, N_TOK), F32)),
        grid=(N_TOK // tm,),
        in_specs=[pl.BlockSpec((tm, D_MODEL), lambda i: (i, 0)),
                  pl.BlockSpec((None, 1, D_MODEL), lambda i: (layer, 0, 0)),
                  _mod_spec(layer, 3, tm), _mod_spec(layer, 4, tm),
                  _full((N_EXPERTS, D_MODEL)), _full((N_EXPERTS, 1))],
        out_specs=(pl.BlockSpec((tm, D_MODEL), lambda i: (i, 0)), pl.BlockSpec((N_EXPERTS, tm), lambda i: (0, i)),
                   pl.BlockSpec((8, tm), lambda i: (0, i))),
        compiler_params=_params(("parallel",)),
        name="router",
    )(x, norm_g.reshape(DEPTH, 1, D_MODEL), mod, mod, router_wt, router_b.reshape(N_EXPERTS, 1))


def _experts_kernel(xm_ref, gates_ref, w1_ref, w3_ref, w2_ref, x_ref, g_ref, o_ref, acc_ref):
    e = pl.program_id(1)

    @pl.when(e == 0)
    def _():
        acc_ref[...] = jnp.zeros_like(acc_ref)

    xm = xm_ref[...]
    h1 = _mm(xm, w1_ref[...])
    h3 = _mm(xm, w3_ref[...])
    gates = gates_ref[...]
    lane = lax.broadcasted_iota(jnp.int32, gates.shape, 1)
    gate = jnp.sum(jnp.where(lane == e, gates, 0.0), axis=-1, keepdims=True)
    hh = h1 * _sigmoid(h1) * h3 * gate
    acc_ref[...] += _mm(hh.astype(BF16), w2_ref[...])

    @pl.when(e == N_EXPERTS - 1)
    def _():
        o_ref[...] = x_ref[...] + g_ref[...] * acc_ref[...]


def _experts(xm, gates, w1_b, w3_b, w2_b, x, mod, layer):
    tm = 512
    row = pl.BlockSpec((tm, D_MODEL), lambda i, e: (i, 0))
    return pl.pallas_call(
        _experts_kernel,
        out_shape=jax.ShapeDtypeStruct((N_TOK, D_MODEL), F32),
        grid=(N_TOK // tm, N_EXPERTS),
        in_specs=[row, pl.BlockSpec((tm, N_EXPERTS), lambda i, e: (i, 0)),
                  pl.BlockSpec((None, None, D_MODEL, D_EXPERT), lambda i, e: (layer, e, 0, 0)),
                  pl.BlockSpec((None, None, D_MODEL, D_EXPERT), lambda i, e: (layer, e, 0, 0)),
                  pl.BlockSpec((None, None, D_EXPERT, D_MODEL), lambda i, e: (layer, e, 0, 0)),
                  row, _mod_spec(layer, 5, tm)],
        out_specs=row,
        scratch_shapes=[pltpu.VMEM((tm, D_MODEL), F32)],
        compiler_params=_params(("parallel", "arbitrary")),
        name="experts",
    )(xm, gates, w1_b, w3_b, w2_b, x, mod)


MOE_TM = 512
N_PAIRS = N_EXPERT_GROUPS * 6
MOE_STEPS = 2 * (N_PAIRS + N_TOK // MOE_TM - 1)


def _gather_rows(src_hbm, dst_ref, idx_ref, base, n, sem):
    def issue(r, carry):
        pltpu.make_async_copy(src_hbm.at[pl.ds(idx_ref[base + r], 1), :], dst_ref.at[pl.ds(r, 1), :], sem).start()
        return carry

    def wait(r, carry):
        pltpu.make_async_copy(src_hbm.at[pl.ds(0, 1), :], dst_ref.at[pl.ds(r, 1), :], sem).wait()
        return carry

    lax.fori_loop(0, n, issue, 0, unroll=8)
    lax.fori_loop(0, n, wait, 0, unroll=8)


def _moe_plan(ids):
    i32 = jnp.int32
    lo, hi = ids[0].astype(i32), ids[1].astype(i32)
    src = jnp.argsort(lo * N_EXPERTS + hi).astype(i32)
    pos = jnp.argsort(src).astype(i32)
    n_tiles = N_TOK // MOE_TM
    ex = jnp.arange(N_EXPERTS, dtype=i32)
    lo_s, hi_s = lo[src].reshape(n_tiles, MOE_TM, 1), hi[src].reshape(n_tiles, MOE_TM, 1)
    used = ((lo_s == ex).any(axis=1) | (hi_s == ex).any(axis=1)).reshape(-1)
    n_valid = jnp.sum(used).astype(i32)
    idx = jnp.nonzero(used, size=MOE_STEPS, fill_value=0)[0].astype(i32)
    valid = jnp.arange(MOE_STEPS, dtype=i32) < n_valid
    idx = jnp.where(valid, idx, idx[jnp.maximum(n_valid - 1, 0)])
    tile, exp = idx // N_EXPERTS, idx % N_EXPERTS
    first = valid & (tile != jnp.concatenate([jnp.full((1,), -1, i32), tile[:-1]]))
    return src, pos, tile, exp, first.astype(i32), valid.astype(i32)


def _moe_kernel(tile_ref, exp_ref, first_ref, valid_ref, src_ref, xm_hbm, gates_ref, w1_ref, w3_ref, w2_ref,
                o_ref, xs_ref, xb_ref, sem):
    s = pl.program_id(0)

    @pl.when(first_ref[s] == 1)
    def _():
        _gather_rows(xm_hbm, xs_ref, src_ref, tile_ref[s] * MOE_TM, MOE_TM, sem)
        xb_ref[...] = xs_ref[...].astype(BF16)
        o_ref[...] = jnp.zeros_like(o_ref)

    @pl.when(valid_ref[s] == 1)
    def _():
        xb = xb_ref[...]
        h1 = _mm(xb, w1_ref[...].astype(BF16))
        h3 = _mm(xb, w3_ref[...].astype(BF16))
        gates = gates_ref[...]
        lane = lax.broadcasted_iota(jnp.int32, gates.shape, 1)
        gate = jnp.sum(jnp.where(lane == exp_ref[s], gates, 0.0), axis=-1, keepdims=True)
        hh = h1 * _sigmoid(h1) * h3 * gate
        o_ref[...] += _mm(hh.astype(BF16), w2_ref[...].astype(BF16))


def _moe(xm, gates_sorted, plan, w1_b, w3_b, w2_b, layer):
    src, _, tile, exp, first, valid = plan
    w_in = pl.BlockSpec((None, None, D_MODEL, D_EXPERT), lambda s, t, e, f, v, i: (layer, e[s], 0, 0))
    grid_spec = pltpu.PrefetchScalarGridSpec(
        num_scalar_prefetch=5, grid=(MOE_STEPS,),
        in_specs=[pl.BlockSpec(memory_space=pl.ANY),
                  pl.BlockSpec((MOE_TM, N_EXPERTS), lambda s, t, e, f, v, i: (t[s], 0)),
                  w_in, w_in,
                  pl.BlockSpec((None, None, D_EXPERT, D_MODEL), lambda s, t, e, f, v, i: (layer, e[s], 0, 0))],
        out_specs=pl.BlockSpec((MOE_TM, D_MODEL), lambda s, t, e, f, v, i: (t[s], 0)),
        scratch_shapes=[pltpu.VMEM((MOE_TM, D_MODEL), F32), pltpu.VMEM((MOE_TM, D_MODEL), BF16),
                        pltpu.SemaphoreType.DMA(())])
    return pl.pallas_call(
        _moe_kernel, out_shape=jax.ShapeDtypeStruct((N_TOK, D_MODEL), F32), grid_spec=grid_spec,
        compiler_params=_params(("arbitrary",)), name="moe",
    )(tile, exp, first, valid, src, xm, gates_sorted, w1_b, w3_b, w2_b)


def _combine_kernel(pos_ref, acc_hbm, x_ref, g_ref, o_ref, buf_ref, sem):
    _gather_rows(acc_hbm, buf_ref, pos_ref, pl.program_id(0) * MOE_TM, MOE_TM, sem)
    o_ref[...] = x_ref[...] + g_ref[...] * buf_ref[...]


def _combine(acc_sorted, pos, x, mod, layer):
    tm = MOE_TM
    row = pl.BlockSpec((tm, D_MODEL), lambda i, p: (i, 0))
    grid_spec = pltpu.PrefetchScalarGridSpec(
        num_scalar_prefetch=1, grid=(N_TOK // tm,),
        in_specs=[pl.BlockSpec(memory_space=pl.ANY), row,
                  pl.BlockSpec((None, None, None, 1, D_MODEL), lambda i, p: (layer, _mod_row(i, tm), 5, 0, 0))],
        out_specs=row,
        scratch_shapes=[pltpu.VMEM((tm, D_MODEL), F32), pltpu.SemaphoreType.DMA(())])
    return pl.pallas_call(
        _combine_kernel, out_shape=jax.ShapeDtypeStruct((N_TOK, D_MODEL), F32), grid_spec=grid_spec,
        compiler_params=_params(("arbitrary",)), name="combine",
    )(pos, acc_sorted, x, mod)


def _gate_rows(gcols, n_seq, seq):
    return gcols.reshape(n_seq, seq // CHUNK, CHUNK, gcols.shape[-1]).transpose(0, 1, 3, 2)


def kernel(x_prompt, x_sample, cache_na_k, cache_na_v, state_rwkv, state_mlstm_c, state_mlstm_n, state_mlstm_m,
           c, c_ctx, norm1_g, norm2_g, w_mod, b_mod, w_in, conv_w, na_q_g, na_k_g, na_rpb, rw_w0, rw_w_up, rw_a0,
           rw_a_up, rw_g_up, rw_k_k, rw_k_a, rw_r_k, rw_ln_g, rw_ln_b, ml_gate_b, ml_norm_g, w_out, router_w,
           router_b, moe_w1, moe_w3, moe_w2):
    p = dict(rw_w0=rw_w0, rw_w_up=rw_w_up, rw_a0=rw_a0, rw_a_up=rw_a_up, rw_g_up=rw_g_up, rw_k_k=rw_k_k,
             rw_k_a=rw_k_a, rw_r_k=rw_r_k, rw_ln_g=rw_ln_g, rw_ln_b=rw_ln_b, ml_gate_b=ml_gate_b,
             ml_norm_g=ml_norm_g)
    cvecs = jnp.concatenate([c_ctx[None], c, jnp.zeros((MOD_ROWS - 1 - DEC_BATCH, D_MODEL), F32)], axis=0)
    mod = _adaln(cvecs, w_mod, b_mod).reshape(DEPTH, MOD_ROWS, 6, 1, D_MODEL)

    assert w_in.shape[-1] == P_IN
    w_in_b, w_out_b, w1_b, w3_b, w2_b = jnp.swapaxes(w_in, 1, 2), w_out, moe_w1, moe_w3, moe_w2
    tw = _na_tables(na_rpb)
    router_wt = router_w.T
    sample_row_off = N_PROMPT // DEC_SEQ

    x = jnp.concatenate([x_prompt.reshape(N_PROMPT, D_MODEL), x_sample.reshape(N_SAMPLE, D_MODEL)], axis=0)
    new_k, new_v, new_rw, new_c, new_n, new_m = [], [], [], [], [], []
    for l in range(DEPTH):
        u = _inproj(x, norm1_g, mod, w_in_b, l)
        ya_p, yb_p, nk, nv = _attn_prompt(u, conv_w, na_q_g, na_k_g, l)
        ya_s, yb_s = _na_sample(u, cache_na_k, cache_na_v, tw, conv_w, na_q_g, na_k_g, l)
        yc_p, st = _rwkv(u, p, l, BATCH, SEQ, 0, emit_state=True)
        (yc_s,) = _rwkv(u, p, l, DEC_BATCH, DEC_SEQ, sample_row_off, s0=state_rwkv)
        g0 = CB_GATE * MIX_W + SM_GATE
        gcols = u[:, g0:g0 + 4 * ML_HEADS]
        yd_p, cm, nm, mm = _mlstm(u, _gate_rows(gcols[:N_PROMPT], BATCH, SEQ), p, l, BATCH, SEQ, 0,
                                  emit_state=True)
        (yd_s,) = _mlstm(u, _gate_rows(gcols[N_PROMPT:], DEC_BATCH, DEC_SEQ), p, l, DEC_BATCH, DEC_SEQ,
                         sample_row_off, state=(state_mlstm_c, state_mlstm_n, state_mlstm_m))
        ys = [jnp.concatenate(pair, axis=0) for pair in ((ya_p, ya_s), (yb_p, yb_s), (yc_p, yc_s), (yd_p, yd_s))]
        x = _outproj(ys, w_out_b, x, mod, l)
        xm, gates_t, ids = _router(x, norm2_g, mod, router_wt, router_b, l)
        plan = _moe_plan(ids)
        acc_sorted = _moe(xm, gates_t.T[plan[0]], plan, w1_b, w3_b, w2_b, l)
        x = _combine(acc_sorted, plan[1], x, mod, l)
        new_k.append(nk)
        new_v.append(nv)
        new_rw.append(st)
        new_c.append(cm)
        new_n.append(nm.reshape(BATCH, 2, ML_HEADS, ML_DH))
        new_m.append(mm.reshape(BATCH, 2, ML_HEADS))
    stack = lambda xs: jnp.stack(xs, axis=1)
    return (x[:N_PROMPT].reshape(BATCH, SEQ, D_MODEL), x[N_PROMPT:].reshape(DEC_BATCH, DEC_SEQ, D_MODEL),
            stack(new_k), stack(new_v), stack(new_rw), stack(new_c), stack(new_n), stack(new_m))
```

```python
import functools

import numpy as np
import jax
import jax.numpy as jnp
from jax import lax
from jax.experimental import pallas as pl
from jax.experimental.pallas import tpu as pltpu

F32 = jnp.float32
BF16 = jnp.bfloat16
HI = lax.Precision.HIGHEST

D_MODEL = 2048
BATCH = 16
SEQ = 256
DEPTH = 2
DEC_BATCH = 2
DEC_SEQ = 1024
PAST_LEN = 512
GRID_W = 64
MIX_W = D_MODEL // 4
CONV_K = 3
NA_DH = 64
NA_HEADS = MIX_W // NA_DH
NA_WIN_R = 8
NA_WIN_C = 16
NA_SCALE = NA_DH ** -0.5
ROPE_THETA = 10000.0
RW_DH = 64
RW_HEADS = MIX_W // RW_DH
RW_W_RANK = 64
RW_A_RANK = 64
RW_G_RANK = 128
RW_DECAY_SCALE = 0.606531
RW_GN_EPS = 64e-5
ML_DH = 128
ML_HEADS = MIX_W // ML_DH
N_EXPERTS = 16
N_EXPERT_GROUPS = 4
D_EXPERT = 512
EPS = 1e-6
NEG_INF = -1e30

N_PROMPT = BATCH * SEQ
N_SAMPLE = DEC_BATCH * DEC_SEQ
N_TOK = N_PROMPT + N_SAMPLE
MOD_ROWS = 8
CHUNK = 64
SUB = 16
P1_CHUNKS = 4
NA_ROWS_PER_TRIP = 4
P_BLOCKS = 15
P_PAD = P_BLOCKS * MIX_W
(CB_CVB, CB_CVC, CB_CVH, CB_NAQ, CB_NAK, CB_NAV, CB_RWR, CB_RWK, CB_RWV,
 CB_MLQ, CB_MLK, CB_MLV, CB_MLO, CB_SMALL, CB_GATE) = range(P_BLOCKS)
SM_WL, SM_AL, SM_GL = 0, 64, 128
SM_GATE = MIX_W - 4 * ML_HEADS
VMEM_LIMIT = 56 * 1024 * 1024


def _mm(a, b, prec=None):
    return jnp.dot(a, b, precision=prec, preferred_element_type=F32)


def _mm_nt(a, b, prec=None):
    return lax.dot_general(a, b, (((1,), (1,)), ((), ())), precision=prec, preferred_element_type=F32)


def _mm_tn(a, b, prec=None):
    return _mm(a.T, b, prec)


def _sigmoid(x):
    return 1.0 / (1.0 + jnp.exp(-x))


def _full(shape):
    n = len(shape)
    return pl.BlockSpec(shape, lambda *_: (0,) * n)


def _params(sem):
    return pltpu.CompilerParams(dimension_semantics=sem, vmem_limit_bytes=VMEM_LIMIT)


def _mod_row(i, tm):
    n_prompt_tiles = N_PROMPT // tm
    tiles_per_sample = DEC_SEQ // tm
    return jnp.where(i < n_prompt_tiles, 0, 1 + (i - n_prompt_tiles) // tiles_per_sample)


def _mod_spec(layer, chunk, tm, tn=D_MODEL, with_j=False):
    if with_j:
        return pl.BlockSpec((None, None, None, 1, tn), lambda i, j: (layer, _mod_row(i, tm), chunk, 0, j))
    return pl.BlockSpec((None, None, None, 1, tn), lambda i, *_: (layer, _mod_row(i, tm), chunk, 0, 0))


def _tri_masks(n):
    t = lax.broadcasted_iota(jnp.int32, (n, n), 0)
    s = lax.broadcasted_iota(jnp.int32, (n, n), 1)
    incl = ((s <= t).astype(F32), (s >= t).astype(F32))
    strict = ((s < t).astype(F32), (s > t).astype(F32))
    return incl, strict, t, s


def _adaln_kernel(cv_ref, w_ref, b_ref, o_ref):
    cv = cv_ref[...]
    o_ref[...] = _mm(cv * _sigmoid(cv), w_ref[...], HI) + b_ref[...]


def _adaln(cvecs, w_mod, b_mod):
    tn = 1024
    n_out = 6 * D_MODEL
    return pl.pallas_call(
        _adaln_kernel,
        out_shape=jax.ShapeDtypeStruct((DEPTH, MOD_ROWS, n_out), F32),
        grid=(DEPTH, n_out // tn),
        in_specs=[_full((MOD_ROWS, D_MODEL)),
                  pl.BlockSpec((None, D_MODEL, tn), lambda l, j: (l, 0, j)),
                  pl.BlockSpec((None, 1, tn), lambda l, j: (l, 0, j))],
        out_specs=pl.BlockSpec((None, MOD_ROWS, tn), lambda l, j: (l, 0, j)),
        compiler_params=_params(("parallel", "parallel")),
        name="adaln",
    )(cvecs, w_mod, b_mod.reshape(DEPTH, 1, n_out))


def _modulated(x, g, sh, sc):
    y = x * lax.rsqrt(jnp.mean(x * x, axis=-1, keepdims=True) + EPS) * g
    return y * (1.0 + sc) + sh


def _inproj_kernel(x_ref, g_ref, sh_ref, sc_ref, w_ref, o_ref, xm_ref):
    @pl.when(pl.program_id(1) == 0)
    def _():
        xm_ref[...] = _modulated(x_ref[...], g_ref[...], sh_ref[...], sc_ref[...]).astype(BF16)

    o_ref[...] = _mm_nt(xm_ref[...], w_ref[0].astype(BF16))


P_IN = 13 * MIX_W + RW_W_RANK + RW_A_RANK + RW_G_RANK + 4 * ML_HEADS


def _inproj_src_row(j):
    a = 9
    narrow = RW_W_RANK + RW_A_RANK + RW_G_RANK
    g = 16
    return g * jnp.where(j < a, j * (MIX_W // g),
                         jnp.where(j < CB_SMALL, j * (MIX_W // g) + narrow // g,
                                   jnp.where(j == CB_SMALL, a * MIX_W // g, (P_IN - MIX_W) // g)))


def _inproj(x, norm_g, mod, w_in_t, layer):
    tm, tn = 1024, MIX_W
    return pl.pallas_call(
        _inproj_kernel,
        out_shape=jax.ShapeDtypeStruct((N_TOK, P_PAD), F32),
        grid=(N_TOK // tm, P_BLOCKS),
        in_specs=[pl.BlockSpec((tm, D_MODEL), lambda i, j: (i, 0)),
                  pl.BlockSpec((None, 1, D_MODEL), lambda i, j: (layer, 0, 0)),
                  _mod_spec(layer, 0, tm), _mod_spec(layer, 1, tm),
                  pl.BlockSpec((pl.Element(1), pl.Element(tn), pl.Element(D_MODEL)),
                               lambda i, j: (layer, _inproj_src_row(j), 0))],
        out_specs=pl.BlockSpec((tm, tn), lambda i, j: (i, j)),
        scratch_shapes=[pltpu.VMEM((tm, D_MODEL), BF16)],
        compiler_params=_params(("parallel", "arbitrary")),
        name="inproj",
    )(x, norm_g.reshape(DEPTH, 1, D_MODEL), mod, mod, w_in_t)


def _conv_mix(b, c, h, w):
    u = c * h
    n = u.shape[0]
    row = lax.broadcasted_iota(jnp.int32, u.shape, 0)
    prev = jnp.where(row == 0, 0.0, pltpu.roll(u, 1, axis=0))
    nxt = jnp.where(row == n - 1, 0.0, pltpu.roll(u, n - 1, axis=0))
    return b * (prev * w[0:1] + u * w[1:2] + nxt * w[2:3])


def _head_rms(x, g):
    return x * lax.rsqrt(jnp.mean(x * x, axis=-1, keepdims=True) + EPS) * g


def _attn_prompt_kernel(cb_ref, cc_ref, ch_ref, q_ref, k_ref, v_ref, cw_ref, qg_ref, kg_ref,
                        ya_ref, yb_ref, nk_ref, nv_ref):
    ya_ref[...] = _conv_mix(cb_ref[...], cc_ref[...], ch_ref[...], cw_ref[...]).astype(ya_ref.dtype)
    sls = [slice(h * NA_DH, (h + 1) * NA_DH) for h in range(NA_HEADS)]
    qn = [_head_rms(q_ref[:, sl], qg_ref[...]) * NA_SCALE for sl in sls]
    kn = [_head_rms(k_ref[:, sl], kg_ref[...]) for sl in sls]
    vh = [v_ref[:, sl] for sl in sls]
    s = [_mm_nt(q.astype(BF16), k.astype(BF16)) for q, k in zip(qn, kn)]
    p = [jnp.exp(x - jnp.max(x, axis=-1, keepdims=True)) for x in s]
    o = [_mm(x.astype(BF16), v.astype(BF16)) / jnp.sum(x, axis=-1, keepdims=True) for x, v in zip(p, vh)]
    yb_ref[...] = jnp.concatenate(o, axis=1).astype(yb_ref.dtype)
    for h in range(NA_HEADS):
        nk_ref[h] = kn[h]
        nv_ref[h] = vh[h]


def _u_spec(rows, col_block, row_off_blocks=0):
    return pl.BlockSpec((rows, MIX_W), lambda b: (b + row_off_blocks, col_block))


def _carry_through(kernel, n_inputs, prev, n_plain_out, n_state_out, layer):
    n_prev = len(prev)

    def body(*refs):
        ins, rest = refs[:n_inputs], list(refs[n_inputs + n_prev:])
        if not n_prev:
            for i in range(n_plain_out, n_plain_out + n_state_out):
                full = rest[i]
                for d in range(DEPTH):
                    if d != layer:
                        full[d] = jnp.zeros(full.shape[1:], full.dtype)
                rest[i] = full.at[layer]
        return kernel(*ins, *rest)

    return body, [pl.BlockSpec(memory_space=pl.ANY)] * n_prev


def _state_spec(shape, layer, first, index):
    def index_map(*g):
        b, *tail = index(*g)
        return (b, 0 if first else layer, *tail)
    return pl.BlockSpec((None, DEPTH if first else None) + tuple(shape), index_map)


def _attn_prompt(u, conv_w, q_g, k_g, layer, prev=(), n_seq=BATCH, seq=SEQ):
    lw = lambda shape: pl.BlockSpec((None,) + shape, lambda b: (layer,) + (0,) * len(shape))
    y_spec = pl.BlockSpec((seq, MIX_W), lambda b: (b, 0))
    kv_spec = _state_spec((NA_HEADS, seq, NA_DH), layer, not prev, lambda b: (b, 0, 0, 0))
    in_specs = ([_u_spec(seq, cb) for cb in (CB_CVB, CB_CVC, CB_CVH, CB_NAQ, CB_NAK, CB_NAV)]
                + [lw((CONV_K, MIX_W)), lw((1, NA_DH)), lw((1, NA_DH))])
    body, prev_specs = _carry_through(_attn_prompt_kernel, len(in_specs), prev, 2, 2, layer)
    return pl.pallas_call(
        body,
        out_shape=(jax.ShapeDtypeStruct((n_seq * seq, MIX_W), BF16),) * 2
        + (jax.ShapeDtypeStruct((n_seq, DEPTH, NA_HEADS, seq, NA_DH), F32),) * 2,
        grid=(n_seq,),
        in_specs=in_specs + prev_specs,
        out_specs=(y_spec, y_spec, kv_spec, kv_spec),
        input_output_aliases={len(in_specs) + i: 2 + i for i in range(len(prev))},
        compiler_params=_params(("parallel",)),
        name="attn_prompt",
    )(u, u, u, u, u, u, conv_w, q_g.reshape(DEPTH, 1, NA_DH), k_g.reshape(DEPTH, 1, NA_DH), *prev)


def _na_kernel(cb_ref, cc_ref, ch_ref, q_ref, k_ref, v_ref, kc_ref, vc_ref, tw_ref, cos_ref, sin_ref,
               perm_ref, cw_ref, qg_ref, kg_ref, ya_ref, yb_ref, qs_ref, ks_ref, tws_ref):
    rows = DEC_SEQ // GRID_W
    wr = min(NA_WIN_R, rows)
    nw = wr * GRID_W
    ya_ref[...] = _conv_mix(cb_ref[...], cc_ref[...], ch_ref[...], cw_ref[...]).astype(ya_ref.dtype)
    cos, sin, perm_b = cos_ref[...], sin_ref[...], perm_ref[...].astype(BF16)

    def rope(x):
        hi, lo = _split2(x)
        return x * cos + (_mm(hi, perm_b) + _mm(lo, perm_b)) * sin

    for h in range(NA_HEADS):
        sl = slice(h * NA_DH, (h + 1) * NA_DH)
        qs_ref[...] = (rope(_head_rms(q_ref[:, sl], qg_ref[...])) * NA_SCALE).astype(BF16)
        ks_ref[...] = rope(_head_rms(k_ref[:, sl], kg_ref[...])).astype(BF16)
        kch = kc_ref[h].astype(BF16)
        vch = vc_ref[h].astype(BF16)
        for p in range(wr):
            tws_ref[p] = jnp.concatenate([tw_ref[h, j - p + NA_WIN_R - 1] for j in range(wr)], axis=1)

        def rows_step(t, carry):
            rr = [t * NA_ROWS_PER_TRIP + i for i in range(NA_ROWS_PER_TRIP)]
            rs = [jnp.clip(r - wr // 2, 0, rows - wr) for r in rr]
            q0 = [pl.multiple_of(r * GRID_W, GRID_W) for r in rr]
            k0 = [pl.multiple_of(x * GRID_W, GRID_W) for x in rs]
            q_r = [qs_ref[pl.ds(x, GRID_W), :] for x in q0]
            s_w = [_mm_nt(q, ks_ref[pl.ds(k, nw), :]) + tws_ref[r - x] for q, k, r, x in zip(q_r, k0, rr, rs)]
            s_c = [_mm_nt(q, kch) for q in q_r]
            m = [jnp.maximum(jnp.max(a, axis=-1, keepdims=True), jnp.max(b, axis=-1, keepdims=True))
                 for a, b in zip(s_w, s_c)]
            p_w = [jnp.exp(a - x) for a, x in zip(s_w, m)]
            p_c = [jnp.exp(b - x) for b, x in zip(s_c, m)]
            den = [jnp.sum(a, axis=-1, keepdims=True) + jnp.sum(b, axis=-1, keepdims=True) for a, b in zip(p_w, p_c)]
            v_w = [v_ref[pl.ds(k, nw), sl].astype(BF16) for k in k0]
            o = [(_mm(a.astype(BF16), v) + _mm(b.astype(BF16), vch)) / d for a, b, v, d in zip(p_w, p_c, v_w, den)]
            for x, val in zip(q0, o):
                yb_ref[pl.ds(x, GRID_W), sl] = val.astype(yb_ref.dtype)
            return carry

        lax.fori_loop(0, rows // NA_ROWS_PER_TRIP, rows_step, 0)


def _na_tables(rpb):
    rows = DEC_SEQ // GRID_W
    wr = min(NA_WIN_R, rows)
    qc = np.arange(GRID_W)
    kc = np.arange(GRID_W)
    wstart = np.clip(qc - NA_WIN_C // 2, 0, GRID_W - NA_WIN_C)
    colmask = (kc[None, :] >= wstart[:, None]) & (kc[None, :] < wstart[:, None] + NA_WIN_C)
    dc = np.clip(kc[None, :] - qc[:, None], -(NA_WIN_C - 1), NA_WIN_C - 1) + NA_WIN_C - 1
    bias = rpb[:, :, :, dc]
    return jnp.where(colmask[None, None, None], bias, NEG_INF)


def _rope_tables():
    t = np.arange(DEC_SEQ)
    quarter = NA_DH // 4
    freq = ROPE_THETA ** (-np.arange(quarter, dtype=np.float32) / quarter)
    ang_r = (t // GRID_W).astype(np.float32)[:, None] * freq
    ang_c = (t % GRID_W).astype(np.float32)[:, None] * freq
    cos = np.concatenate([np.cos(ang_r), np.cos(ang_r), np.cos(ang_c), np.cos(ang_c)], axis=-1)
    sin = np.concatenate([-np.sin(ang_r), np.sin(ang_r), -np.sin(ang_c), np.sin(ang_c)], axis=-1)
    src = np.concatenate([np.arange(quarter) + quarter, np.arange(quarter),
                          np.arange(quarter) + 3 * quarter, np.arange(quarter) + 2 * quarter])
    perm = np.zeros((NA_DH, NA_DH), np.float32)
    perm[src, np.arange(NA_DH)] = 1.0
    return cos.astype(np.float32), sin.astype(np.float32), perm


def _na_sample(u, cache_k, cache_v, tw, conv_w, q_g, k_g, layer, n_seq=DEC_BATCH, row_off=N_PROMPT // DEC_SEQ):
    cos, sin, perm = _rope_tables()
    lw = lambda shape: pl.BlockSpec((None,) + shape, lambda b: (layer,) + (0,) * len(shape))
    y_spec = pl.BlockSpec((DEC_SEQ, MIX_W), lambda b: (b, 0))
    c_spec = pl.BlockSpec((None, None, NA_HEADS, PAST_LEN, NA_DH), lambda b: (b, layer, 0, 0, 0))
    wr = min(NA_WIN_R, DEC_SEQ // GRID_W)
    n_off = 2 * NA_WIN_R - 1
    return pl.pallas_call(
        _na_kernel,
        out_shape=(jax.ShapeDtypeStruct((n_seq * DEC_SEQ, MIX_W), BF16),) * 2,
        grid=(n_seq,),
        in_specs=[_u_spec(DEC_SEQ, cb, row_off) for cb in (CB_CVB, CB_CVC, CB_CVH, CB_NAQ, CB_NAK, CB_NAV)]
        + [c_spec, c_spec, lw((NA_HEADS, n_off, GRID_W, GRID_W)),
           _full((DEC_SEQ, NA_DH)), _full((DEC_SEQ, NA_DH)), _full((NA_DH, NA_DH)),
           lw((CONV_K, MIX_W)), lw((1, NA_DH)), lw((1, NA_DH))],
        out_specs=(y_spec, y_spec),
        scratch_shapes=[pltpu.VMEM((DEC_SEQ, NA_DH), BF16), pltpu.VMEM((DEC_SEQ, NA_DH), BF16),
                        pltpu.VMEM((wr, GRID_W, wr * GRID_W), F32)],
        compiler_params=_params(("parallel",)),
        name="na_sample",
    )(u, u, u, u, u, u, cache_k, cache_v, tw, jnp.asarray(cos), jnp.asarray(sin), jnp.asarray(perm),
      conv_w, q_g.reshape(DEPTH, 1, NA_DH), k_g.reshape(DEPTH, 1, NA_DH))


def _seg_ones(width, seg):
    a = lax.broadcasted_iota(jnp.int32, (width, width), 0) // seg
    b = lax.broadcasted_iota(jnp.int32, (width, width), 1) // seg
    return (a == b).astype(F32)


def _split2(x):
    hi = x.astype(BF16)
    return hi, (x - hi.astype(F32)).astype(BF16)


def _mm1(a, b):
    return _mm(a.astype(BF16), b.astype(BF16))


def _mm3(a, b, nt=False):
    dot = _mm_nt if nt else _mm
    a_hi, a_lo = _split2(a)
    b_hi, b_lo = _split2(b)
    return dot(a_hi, b_lo) + dot(a_lo, b_hi) + dot(a_hi, b_hi)


def _split3(x):
    x0 = x.astype(BF16)
    r1 = x - x0.astype(F32)
    x1 = r1.astype(BF16)
    return x0, x1, (r1 - x1.astype(F32)).astype(BF16)


def _mm_exact_lhs(mask_b, x):
    x0, x1, x2 = _split3(x)
    return _mm(mask_b, x2) + _mm(mask_b, x1) + _mm(mask_b, x0)


def _mm_exact_rhs(x, mask_b):
    x0, x1, x2 = _split3(x)
    return _mm(x2, mask_b) + _mm(x1, mask_b) + _mm(x0, mask_b)


def _rwkv_kernel(*refs, seq, has_s0, emit_state):
    it = iter(refs)
    r_ref, k_ref, v_ref, sm_ref = (next(it) for _ in range(4))
    (w0_ref, wup_ref, a0_ref, aup_ref, gup_ref, kkp_ref, kap_ref, rkp_ref, lng_ref, lnb_ref) = (
        next(it) for _ in range(10))
    mask_ref = next(it)
    s0_ref = next(it) if has_s0 else None
    y_ref = next(it)
    so_ref = next(it) if emit_state else None
    kk_ref, lw_ref, ka_ref, kd_ref, coef_ref, ysp_ref = (next(it) for _ in range(6))

    dh = RW_DH
    pw = 2 * dh
    nc = seq // CHUNK
    seg_b = _seg_ones(pw, dh).astype(BF16)

    r = r_ref[...]
    k = k_ref[...]
    sm = sm_ref[...]
    wl = jnp.tanh(sm[:, SM_WL:SM_WL + RW_W_RANK])
    al = sm[:, SM_AL:SM_AL + RW_A_RANK]
    kk = k * kkp_ref[...]
    kk = kk * lax.rsqrt(_mm_exact_rhs(kk * kk, seg_b) + EPS)
    kk_ref[...] = kk
    for z in range(2):
        lw_ref[z] = -RW_DECAY_SCALE * _sigmoid(w0_ref[z:z + 1, :] + _mm3(wl, wup_ref[z]))
        a = _sigmoid(a0_ref[z:z + 1, :] + _mm3(al, aup_ref[z]))
        ka_ref[z] = kk * a
        kd_ref[z] = k * (1.0 + (a - 1.0) * kap_ref[...])

    def phase1(c2, carry):
        incl, strict, ti, si = _tri_masks(CHUNK)
        incl_b = (mask_ref[0], mask_ref[1])
        incl2 = tuple(jnp.concatenate([m, m], axis=1) for m in incl)
        diag_blk = (ti // SUB == si // SUB).astype(F32)
        eye = (ti == si).astype(F32)
        zero_blk = jnp.zeros((CHUNK, dh), F32)
        inst = []
        for cc in range(P1_CHUNKS):
            rows = pl.ds(pl.multiple_of((c2 * P1_CHUNKS + cc) * CHUNK, CHUNK), CHUNK)
            vc, rc, kkc = v_ref[rows, :], r_ref[rows, :], kk_ref[rows, :]
            for z in range(2):
                lwc = lw_ref[z, rows, :]
                cum = _mm_exact_lhs(incl_b[z], lwc)
                tot = cum[CHUNK - 1:CHUNK] if z == 0 else cum[0:1]
                e_neg = jnp.exp(-cum)
                dec = jnp.exp(tot - cum)
                e_tot = jnp.exp(tot)
                kac, kdc = ka_ref[z, rows, :], kd_ref[z, rows, :]
                rt = rc * jnp.exp(cum)
                kt = kkc * jnp.exp(cum - lwc)
                at, kdt, ah, kh = kac * e_neg, kdc * e_neg, kac * dec, kdc * dec
                for j in range(2):
                    sl = slice(j * dh, (j + 1) * dh)
                    inst.append((z, kt[:, sl], rt[:, sl], vc[:, sl], at[:, sl], kdt[:, sl], ah[:, sl], kh[:, sl],
                                 e_tot[:, sl]))
        zs = [i[0] for i in inst]
        kt_h, rt_h, v_h = [i[1] for i in inst], [i[2] for i in inst], [i[3] for i in inst]
        each = lambda f, *xs: [f(*a) for a in zip(*xs)]
        aa = each(lambda i: _mm3(jnp.concatenate([i[1], i[2]], axis=0),
                                 jnp.concatenate([i[4], i[5]], axis=0), nt=True), inst)
        low = each(lambda a, z: a[0:CHUNK, 0:CHUNK] * strict[z], aa, zs)
        a_kk = each(lambda a, z: a[0:CHUNK, CHUNK:] * strict[z], aa, zs)
        a_r = each(lambda a, z: a[CHUNK:, :] * incl2[z], aa, zs)
        akv = each(_mm1, a_kk, v_h)
        ld = each(lambda x: x * diag_blk, low)
        lo = each(lambda x, y: x - y, low, ld)
        l2 = each(_mm1, ld, ld)
        l4 = each(_mm1, l2, l2)
        l8 = each(_mm1, l4, l4)
        td = each(lambda x: eye - x, ld)
        for lp in (l2, l4, l8):
            td = each(lambda t, p: t + _mm1(t, p), td, lp)
        x0 = each(lambda t, a, b, c_: _mm1(t, jnp.concatenate([a, b, c_], axis=1)), td, lo, kt_h, akv)
        wm = each(lambda x: x[:, 0:CHUNK], x0)
        pq0 = each(lambda x: x[:, CHUNK:], x0)
        pq = pq0
        for _ in range(CHUNK // SUB - 1):
            pq = each(lambda p0, w, p: p0 - _mm1(w, p), pq0, wm, pq)
        ryc = each(lambda a, p, v_: _mm3(a, jnp.concatenate(
            [p, jnp.concatenate([zero_blk, -v_], axis=1)], axis=0)), a_r, pq, v_h)
        gh = each(lambda i, p: _mm3(jnp.concatenate([i[6], i[7]], axis=1).T,
                                    jnp.concatenate([p, i[3]], axis=1)), inst, pq)
        pieces = []
        for i, g, ry, rt_ in zip(inst, gh, ryc, rt_h):
            g_t = eye * i[8] - g[0:dh, 0:dh]
            h_t = g[dh:, 2 * dh:] - g[0:dh, dh:2 * dh]
            pieces += [g_t, rt_ - ry[:, 0:dh], h_t, -ry[:, dh:]]
        per_chunk = len(pieces) // P1_CHUNKS
        for cc in range(P1_CHUNKS):
            coef_ref[c2 * P1_CHUNKS + cc] = jnp.concatenate(pieces[cc * per_chunk:(cc + 1) * per_chunk], axis=0)
        return carry

    lax.fori_loop(0, nc // P1_CHUNKS, phase1, 0)

    m_init = tuple((s0_ref[z, j].T if has_s0 else jnp.zeros((dh, dh), F32)) for z in range(2) for j in range(2))

    def phase2(ci, ms):
        new_ms, ys = [], []
        for z in range(2):
            c = ci if z == 0 else nc - 1 - ci
            for j in range(2):
                base = (z * 2 + j) * 4 * dh
                out = (_mm3(coef_ref[c, base:base + 2 * dh, :], ms[z * 2 + j])
                       + coef_ref[c, base + 2 * dh:base + 4 * dh, :])
                new_ms.append(out[0:dh])
                ys.append(out[dh:])
        ysp_ref[ci] = jnp.concatenate(ys, axis=1)
        return tuple(new_ms)

    m_fin = lax.fori_loop(0, nc, phase2, m_init)

    if emit_state:
        for z in range(2):
            for j in range(2):
                so_ref[z, j] = m_fin[z * 2 + j].T

    y = jnp.concatenate([ysp_ref[c, :, 0:pw] + ysp_ref[nc - 1 - c, :, pw:] for c in range(nc)], axis=0)
    mu = _mm_exact_rhs(y, seg_b) * (1.0 / dh)
    yc = y - mu
    var = _mm_exact_rhs(yc * yc, seg_b) * (1.0 / dh)
    yn = yc * lax.rsqrt(var + RW_GN_EPS) * lng_ref[...] + lnb_ref[...]
    v = v_ref[...]
    bonus = _mm_exact_rhs(r * k * rkp_ref[...], seg_b) * v
    g = _mm3(_sigmoid(sm[:, SM_GL:SM_GL + RW_G_RANK]), gup_ref[...])
    y_ref[...] = ((yn + bonus) * g).astype(y_ref.dtype)


def _rwkv(u, p, layer, n_seq, seq, row_off, s0=None, emit_state=False, prev=()):
    n_pairs = RW_HEADS // 2
    pw = 2 * RW_DH
    bpc = MIX_W // pw
    lw = lambda shape: pl.BlockSpec((None,) + shape, lambda b, hp: (layer,) + (0,) * (len(shape) - 1) + (hp,))
    row = lambda a: a.reshape(DEPTH, 1, MIX_W)
    u_pair = lambda cb: pl.BlockSpec((seq, pw), lambda b, hp: (b + row_off, cb * bpc + hp))
    in_specs = [u_pair(CB_RWR), u_pair(CB_RWK), u_pair(CB_RWV),
                pl.BlockSpec((seq, MIX_W), lambda b, hp: (b + row_off, CB_SMALL)),
                lw((2, pw)), lw((2, RW_W_RANK, pw)), lw((2, pw)), lw((2, RW_A_RANK, pw)),
                lw((RW_G_RANK, pw))] + [lw((1, pw))] * 5 + [_full((2, CHUNK, CHUNK))]
    t_idx = np.arange(CHUNK)
    incl_masks = np.stack([t_idx[None, :] <= t_idx[:, None], t_idx[None, :] >= t_idx[:, None]])
    args = [u, u, u, u, p['rw_w0'], p['rw_w_up'], p['rw_a0'], p['rw_a_up'], p['rw_g_up'],
            row(p['rw_k_k']), row(p['rw_k_a']), row(p['rw_r_k']), row(p['rw_ln_g']), row(p['rw_ln_b']),
            jnp.asarray(incl_masks, BF16)]
    st_blk = (2, 2, RW_DH, RW_DH)
    if s0 is not None:
        in_specs.append(pl.BlockSpec((None, None) + st_blk, lambda b, hp: (b, layer, 0, hp, 0, 0)))
        args.append(s0)
    out_shape = [jax.ShapeDtypeStruct((n_seq * seq, MIX_W), BF16)]
    out_specs = [pl.BlockSpec((seq, pw), lambda b, hp: (b, hp))]
    if emit_state:
        out_shape.append(jax.ShapeDtypeStruct((n_seq, DEPTH, 2, RW_HEADS, RW_DH, RW_DH), F32))
        out_specs.append(_state_spec(st_blk, layer, not prev, lambda b, hp: (b, 0, hp, 0, 0)))
    nc = seq // CHUNK
    tok = lambda n: pltpu.VMEM((n, seq, pw) if n else (seq, pw), F32)
    body, prev_specs = _carry_through(
        functools.partial(_rwkv_kernel, seq=seq, has_s0=s0 is not None, emit_state=emit_state), len(in_specs), prev,
        1, 1 if emit_state else 0, layer)
    return pl.pallas_call(
        body,
        out_shape=tuple(out_shape), grid=(n_seq, n_pairs), in_specs=in_specs + prev_specs,
        out_specs=tuple(out_specs),
        input_output_aliases={len(in_specs) + i: 1 + i for i in range(len(prev))},
        scratch_shapes=[tok(0), tok(2), tok(2), tok(2),
                        pltpu.VMEM((nc, 16 * RW_DH, RW_DH), F32), pltpu.VMEM((nc, CHUNK, 2 * pw), F32)],
        compiler_params=_params(("parallel", "parallel")),
        name=f"rwkv_{seq}",
    )(*args, *prev)


def _rwkv_kernel_old(*refs, seq, has_s0, emit_state):
    it = iter(refs)
    r_ref, k_ref, v_ref, sm_ref = (next(it) for _ in range(4))
    (w0_ref, wup_ref, a0_ref, aup_ref, gup_ref, kkp_ref, kap_ref, rkp_ref, lng_ref, lnb_ref) = (
        next(it) for _ in range(10))
    s0_ref = next(it) if has_s0 else None
    y_ref = next(it)
    so_ref = next(it) if emit_state else None
    rp_ref, vp_ref, kk_ref, lw_ref, ka_ref, kd_ref, ys_ref, st_ref = (next(it) for _ in range(8))

    n_pairs = RW_HEADS // 2
    pw = 2 * RW_DH
    nc = seq // CHUNK
    seg = _seg_ones(MIX_W, RW_DH)

    r = r_ref[...]
    k = k_ref[...]
    sm = sm_ref[...]
    wl = jnp.tanh(sm[:, SM_WL:SM_WL + RW_W_RANK])
    al = sm[:, SM_AL:SM_AL + RW_A_RANK]
    kk = k * kkp_ref[...]
    kk = kk * lax.rsqrt(_mm(kk * kk, seg, HI) + EPS)
    for hp in range(n_pairs):
        psl = slice(hp * pw, (hp + 1) * pw)
        rp_ref[hp] = r[:, psl]
        vp_ref[hp] = v_ref[:, psl]
        kk_ref[hp] = kk[:, psl]
    for z in range(2):
        w_pre = w0_ref[z:z + 1, :] + _mm(wl, wup_ref[z], HI)
        lw = -RW_DECAY_SCALE * _sigmoid(w_pre)
        a = _sigmoid(a0_ref[z:z + 1, :] + _mm(al, aup_ref[z], HI))
        ka = kk * a
        kd = k * (1.0 + (a - 1.0) * kap_ref[...])
        for hp in range(n_pairs):
            psl = slice(hp * pw, (hp + 1) * pw)
            lw_ref[z * n_pairs + hp] = lw[:, psl]
            ka_ref[z * n_pairs + hp] = ka[:, psl]
            kd_ref[z * n_pairs + hp] = kd[:, psl]
    for i in range(2 * RW_HEADS):
        st_ref[i] = s0_ref[i // RW_HEADS, i % RW_HEADS] if has_s0 else jnp.zeros((RW_DH, RW_DH), F32)

    incl, strict, ti, si = _tri_masks(CHUNK)
    diag_blk = (ti // SUB == si // SUB).astype(F32)
    eye = (ti == si).astype(F32)

    def chunk_pair(ci, hp):
        for z in range(2):
            c = ci if z == 0 else nc - 1 - ci
            rows = pl.ds(pl.multiple_of(c * CHUNK, CHUNK), CHUNK)
            lwc = lw_ref[z * n_pairs + hp, rows, :]
            cum = _mm(incl[z], lwc, HI)
            tot = cum[CHUNK - 1:CHUNK] if z == 0 else cum[0:1]
            e_neg = jnp.exp(-cum)
            dec = jnp.exp(tot - cum)
            e_tot = jnp.exp(tot)
            kac = ka_ref[z * n_pairs + hp, rows, :]
            kdc = kd_ref[z * n_pairs + hp, rows, :]
            rt = rp_ref[hp, rows, :] * jnp.exp(cum)
            kt = kk_ref[hp, rows, :] * jnp.exp(cum - lwc)
            at, kdt = kac * e_neg, kdc * e_neg
            ah, kh = kac * dec, kdc * dec
            vc = vp_ref[hp, rows, :]
            for j in range(2):
                sl = slice(j * RW_DH, (j + 1) * RW_DH)
                si_ = z * RW_HEADS + hp * 2 + j
                s0 = st_ref[si_]
                kt_h, rt_h, v_h = kt[:, sl], rt[:, sl], vc[:, sl]
                low = _mm_nt(kt_h, at[:, sl], HI) * strict[z]
                a_kk = _mm_nt(kt_h, kdt[:, sl], HI) * strict[z]
                a_ra = _mm_nt(rt_h, at[:, sl], HI) * incl[z]
                a_rk = _mm_nt(rt_h, kdt[:, sl], HI) * incl[z]
                ld = low * diag_blk
                lo = low - ld
                l2 = _mm(ld, ld, HI)
                l4 = _mm(l2, l2, HI)
                l8 = _mm(l4, l4, HI)
                tinv = eye - ld
                tinv = tinv + _mm(tinv, l2, HI)
                tinv = tinv + _mm(tinv, l4, HI)
                tinv = tinv + _mm(tinv, l8, HI)
                rhs = -(_mm_nt(kt_h, s0, HI) + _mm(a_kk, v_h, HI))
                cv = _mm(tinv, rhs, HI)
                wm = _mm(tinv, lo, HI)
                uu = cv
                for _ in range(CHUNK // SUB - 1):
                    uu = cv - _mm(wm, uu, HI)
                y = _mm_nt(rt_h, s0, HI) + _mm(a_ra, uu, HI) + _mm(a_rk, v_h, HI)
                ys_ref[z * n_pairs + hp, rows, sl] = y
                st_ref[si_] = s0 * e_tot[:, sl] + _mm_tn(uu, ah[:, sl], HI) + _mm_tn(v_h, kh[:, sl], HI)

    def chunk_step(ci, carry):
        def pair_step(hp, carry2):
            chunk_pair(ci, hp)
            return carry2
        return lax.fori_loop(0, n_pairs, pair_step, carry)

    lax.fori_loop(0, nc, chunk_step, 0)

    if emit_state:
        for i in range(2 * RW_HEADS):
            so_ref[i // RW_HEADS, i % RW_HEADS] = st_ref[i]

    g = _mm(_sigmoid(sm[:, SM_GL:SM_GL + RW_G_RANK]), gup_ref[...], HI)
    for h in range(RW_HEADS):
        hp, j = divmod(h, 2)
        sl = slice(j * RW_DH, (j + 1) * RW_DH)
        hsl = slice(h * RW_DH, (h + 1) * RW_DH)
        y = ys_ref[hp, :, sl] + ys_ref[n_pairs + hp, :, sl]
        mu = jnp.mean(y, axis=-1, keepdims=True)
        var = jnp.mean(jnp.square(y - mu), axis=-1, keepdims=True)
        yn = (y - mu) * lax.rsqrt(var + RW_GN_EPS) * lng_ref[:, hsl] + lnb_ref[:, hsl]
        r_h, k_h, v_h = rp_ref[hp, :, sl], k_ref[:, hsl], vp_ref[hp, :, sl]
        bonus = jnp.sum(r_h * k_h * rkp_ref[:, hsl], axis=-1, keepdims=True) * v_h
        y_ref[:, hsl] = ((yn + bonus) * g[:, hsl]).astype(y_ref.dtype)


def _rwkv_old(u, p, layer, n_seq, seq, row_off, s0=None, emit_state=False):
    lw = lambda shape: pl.BlockSpec((None,) + shape, lambda b: (layer,) + (0,) * len(shape))
    row = lambda a: a.reshape(DEPTH, 1, MIX_W)
    in_specs = [_u_spec(seq, cb, row_off) for cb in (CB_RWR, CB_RWK, CB_RWV, CB_SMALL)] + [
        lw((2, MIX_W)), lw((2, RW_W_RANK, MIX_W)), lw((2, MIX_W)), lw((2, RW_A_RANK, MIX_W)),
        lw((RW_G_RANK, MIX_W))] + [lw((1, MIX_W))] * 5
    args = [u, u, u, u, p['rw_w0'], p['rw_w_up'], p['rw_a0'], p['rw_a_up'], p['rw_g_up'],
            row(p['rw_k_k']), row(p['rw_k_a']), row(p['rw_r_k']), row(p['rw_ln_g']), row(p['rw_ln_b'])]
    st_shape = (2, RW_HEADS, RW_DH, RW_DH)
    if s0 is not None:
        in_specs.append(pl.BlockSpec((None, None) + st_shape, lambda b: (b, layer, 0, 0, 0, 0)))
        args.append(s0)
    out_shape = [jax.ShapeDtypeStruct((n_seq * seq, MIX_W), BF16)]
    out_specs = [pl.BlockSpec((seq, MIX_W), lambda b: (b, 0))]
    if emit_state:
        out_shape.append(jax.ShapeDtypeStruct((n_seq,) + st_shape, F32))
        out_specs.append(pl.BlockSpec((None,) + st_shape, lambda b: (b, 0, 0, 0, 0)))
    n_pairs = RW_HEADS // 2
    pair = lambda n: pltpu.VMEM((n, seq, 2 * RW_DH), F32)
    return pl.pallas_call(
        functools.partial(_rwkv_kernel, seq=seq, has_s0=s0 is not None, emit_state=emit_state),
        out_shape=tuple(out_shape), grid=(n_seq,), in_specs=in_specs, out_specs=tuple(out_specs),
        scratch_shapes=[pair(n_pairs), pair(n_pairs), pair(n_pairs), pair(2 * n_pairs), pair(2 * n_pairs),
                        pair(2 * n_pairs), pair(2 * n_pairs), pltpu.VMEM((2 * RW_HEADS, RW_DH, RW_DH), F32)],
        compiler_params=_params(("parallel",)),
        name=f"rwkv_{seq}",
    )(*args)


def _log_sigmoid(x):
    return jnp.minimum(x, 0.0) - jnp.log(1.0 + jnp.exp(-jnp.abs(x)))


def _mlstm_kernel(*refs, seq, has_s0, emit_state):
    it = iter(refs)
    q_ref, k_ref, v_ref, o_ref, sm_ref, gr_ref, bc_ref, br_ref, ng_ref = (next(it) for _ in range(9))
    c0_ref, n0_ref, m0_ref = ((next(it), next(it), next(it)) if has_s0 else (None, None, None))
    y_ref = next(it)
    co_ref, no_ref, mo_ref = ((next(it), next(it), next(it)) if emit_state else (None, None, None))
    hs_ref, c_ref, n_ref, m_ref = (next(it) for _ in range(4))

    nc = seq // CHUNK
    n_st = 2 * ML_HEADS
    for i in range(n_st):
        z, h = divmod(i, ML_HEADS)
        c_ref[i] = c0_ref[z, h] if has_s0 else jnp.zeros((ML_DH, ML_DH), F32)
        n_ref[i] = n0_ref[z, h] if has_s0 else jnp.zeros((1, ML_DH), F32)
        m_ref[i] = m0_ref[z, h] if has_s0 else jnp.zeros((1, 1), F32)

    incl, _, ti, si = _tri_masks(CHUNK)
    before = ((si <= ti), (si >= ti))

    nh = ML_HEADS
    chains = [(z, h) for z in range(2) for h in range(nh)]
    each = lambda f, *xs: [f(*a) for a in zip(*xs)]
    zs = [z for z, _ in chains]

    def chunk_step(ci, carry):
        per_dir = []
        for z in range(2):
            c = ci if z == 0 else nc - 1 - ci
            rows = pl.ds(pl.multiple_of(c * CHUNK, CHUNK), CHUNK)
            g0 = SM_GATE + z * 2 * nh
            gc = sm_ref[rows, g0:g0 + 2 * nh] + bc_ref[:, z * 2 * nh:(z + 1) * 2 * nh]
            gr = gr_ref[c, z * 2 * nh:(z + 1) * 2 * nh, :] + br_ref[z * 2 * nh:(z + 1) * 2 * nh, :]
            b_cols = _mm(incl[z], _log_sigmoid(gc[:, nh:]), HI)
            b_rows = _mm(_log_sigmoid(gr[nh:]), incl[1 - z], HI)
            per_dir.append((rows, gc[:, :nh], b_cols, gr[:nh], b_rows))
        rows = [per_dir[z][0] for z, _ in chains]
        hsl = [slice(h * ML_DH, (h + 1) * ML_DH) for _, h in chains]
        i_col = [per_dir[z][1][:, h:h + 1] for z, h in chains]
        b_col = [per_dir[z][2][:, h:h + 1] for z, h in chains]
        i_row = [per_dir[z][3][h:h + 1] for z, h in chains]
        b_row = [per_dir[z][4][h:h + 1] for z, h in chains]
        b_last = each(lambda b, z: b[CHUNK - 1:CHUNK] if z == 0 else b[0:1], b_col, zs)
        m_old = [m_ref[i] for i in range(n_st)]
        cm = [c_ref[i] for i in range(n_st)]
        nv = [n_ref[i] for i in range(n_st)]
        qc = each(lambda r, s: q_ref[r, s] * (ML_DH ** -0.5), rows, hsl)
        kc = each(lambda r, s: k_ref[r, s], rows, hsl)
        vc = each(lambda r, s: v_ref[r, s], rows, hsl)
        qk = each(lambda q, k: _mm_nt(q.astype(BF16), k.astype(BF16)), qc, kc)
        qcm = each(lambda q, c_: _mm_nt(q.astype(BF16), c_.astype(BF16)), qc, cm)
        a_t = each(lambda b, m: b + m, b_col, m_old)
        dmat = each(lambda bc, brw, ir, z: jnp.where(before[z], bc - brw + ir, NEG_INF), b_col, b_row, i_row, zs)
        m_t = each(lambda a, d: jnp.maximum(a, jnp.max(d, axis=-1, keepdims=True)), a_t, dmat)
        s = each(lambda x, d, m: x * jnp.exp(d - m), qk, dmat, m_t)
        inter = each(lambda a, m: jnp.exp(a - m), a_t, m_t)
        sv = each(_mm1, s, vc)
        g_col = each(lambda bl, bc, ic: bl - bc + ic, b_last, b_col, i_col)
        a_l = each(lambda bl, m: bl + m, b_last, m_old)
        m_new = each(lambda a, g: jnp.maximum(a, jnp.max(g, axis=0, keepdims=True)), a_l, g_col)
        wgt = each(lambda g, m: jnp.exp(g - m), g_col, m_new)
        vk = each(lambda v, w, k: _mm1((v * w).T, k), vc, wgt, kc)
        decay = each(lambda a, m: jnp.exp(a - m), a_l, m_new)
        num = each(lambda i_, x, y: i_ * x + y, inter, qcm, sv)
        den = each(lambda i_, q, n_, s_: i_ * jnp.sum(q * n_, axis=-1, keepdims=True)
                   + jnp.sum(s_, axis=-1, keepdims=True), inter, qc, nv, s)
        hh = each(lambda n_, d, m: n_ / jnp.maximum(jnp.abs(d), jnp.exp(-m)), num, den, m_t)
        for z in range(2):
            hs_ref[z, per_dir[z][0], :] = jnp.concatenate(hh[z * nh:(z + 1) * nh], axis=1)
        for i in range(n_st):
            c_ref[i] = decay[i] * cm[i] + vk[i]
            n_ref[i] = decay[i] * nv[i] + jnp.sum(wgt[i] * kc[i], axis=0, keepdims=True)
            m_ref[i] = m_new[i]
        return carry

    lax.fori_loop(0, nc, chunk_step, 0)

    if emit_state:
        for i in range(n_st):
            z, h = divmod(i, ML_HEADS)
            co_ref[z, h] = c_ref[i]
            no_ref[z, h] = n_ref[i]
            mo_ref[z, h] = m_ref[i]

    for h in range(ML_HEADS):
        hsl = slice(h * ML_DH, (h + 1) * ML_DH)
        hn = _head_rms(hs_ref[0, :, hsl] + hs_ref[1, :, hsl], ng_ref[:, hsl])
        y_ref[:, hsl] = (_sigmoid(o_ref[:, hsl]) * hn).astype(y_ref.dtype)


def _mlstm(u, gate_rows, p, layer, n_seq, seq, row_off, state=None, emit_state=False, prev=()):
    lw = lambda shape: pl.BlockSpec((None,) + shape, lambda b: (layer,) + (0,) * len(shape))
    nc = seq // CHUNK
    n_gate = 4 * ML_HEADS
    in_specs = [_u_spec(seq, cb, row_off) for cb in (CB_MLQ, CB_MLK, CB_MLV, CB_MLO, CB_GATE)] + [
        pl.BlockSpec((None, nc, n_gate, CHUNK), lambda b: (b, 0, 0, 0)),
        lw((1, n_gate)), lw((n_gate, 1)), lw((1, MIX_W))]
    args = [u, u, u, u, u, gate_rows, p['ml_gate_b'].reshape(DEPTH, 1, n_gate),
            p['ml_gate_b'].reshape(DEPTH, n_gate, 1), p['ml_norm_g'].reshape(DEPTH, 1, MIX_W)]
    c_shape, n_shape, m_shape = (2, ML_HEADS, ML_DH, ML_DH), (2, ML_HEADS, 1, ML_DH), (2, ML_HEADS, 1, 1)
    if state is not None:
        c0, n0, m0 = state
        for a, shp in ((c0, c_shape), (n0, n_shape), (m0, m_shape)):
            in_specs.append(pl.BlockSpec((None, None) + shp, lambda b: (b, layer, 0, 0, 0, 0)))
            args.append(a.reshape(a.shape[:2] + shp))
    out_shape = [jax.ShapeDtypeStruct((n_seq * seq, MIX_W), BF16)]
    out_specs = [pl.BlockSpec((seq, MIX_W), lambda b: (b, 0))]
    if emit_state:
        for shp in (c_shape, n_shape, m_shape):
            out_shape.append(jax.ShapeDtypeStruct((n_seq, DEPTH) + shp, F32))
            out_specs.append(_state_spec(shp, layer, not prev, lambda b: (b, 0, 0, 0, 0)))
    n_st = 2 * ML_HEADS
    body, prev_specs = _carry_through(
        functools.partial(_mlstm_kernel, seq=seq, has_s0=state is not None, emit_state=emit_state),
        len(in_specs), prev, 1, 3 if emit_state else 0, layer)
    return pl.pallas_call(
        body,
        out_shape=tuple(out_shape), grid=(n_seq,), in_specs=in_specs + prev_specs, out_specs=tuple(out_specs),
        input_output_aliases={len(in_specs) + i: 1 + i for i in range(len(prev))},
        scratch_shapes=[pltpu.VMEM((2, seq, MIX_W), F32), pltpu.VMEM((n_st, ML_DH, ML_DH), F32),
                        pltpu.VMEM((n_st, 1, ML_DH), F32), pltpu.VMEM((n_st, 1, 1), F32)],
        compiler_params=_params(("parallel",)),
        name=f"mlstm_{seq}",
    )(*args, *prev)


def _outproj_kernel(ya_ref, yb_ref, yc_ref, yd_ref, w_ref, x_ref, g_ref, o_ref):
    acc = _mm(ya_ref[...], w_ref[0:MIX_W, :].astype(BF16))
    for i, y_ref in enumerate((yb_ref, yc_ref, yd_ref), start=1):
        acc += _mm(y_ref[...], w_ref[i * MIX_W:(i + 1) * MIX_W, :].astype(BF16))
    o_ref[...] = x_ref[...] + g_ref[...] * acc


def _outproj(ys, w_out_b, x, mod, layer):
    tm, tn = 1024, 1024
    y_spec = pl.BlockSpec((tm, MIX_W), lambda i, j: (i, 0))
    return pl.pallas_call(
        _outproj_kernel,
        out_shape=jax.ShapeDtypeStruct((N_TOK, D_MODEL), F32),
        grid=(N_TOK // tm, D_MODEL // tn),
        in_specs=[y_spec] * 4 + [pl.BlockSpec((None, D_MODEL, tn), lambda i, j: (layer, 0, j)),
                                 pl.BlockSpec((tm, tn), lambda i, j: (i, j)),
                                 _mod_spec(layer, 2, tm, tn, with_j=True)],
        out_specs=pl.BlockSpec((tm, tn), lambda i, j: (i, j)),
        compiler_params=_params(("parallel", "parallel")),
        name="outproj",
    )(*ys, w_out_b, x, mod)


def _router_kernel(x_ref, g_ref, sh_ref, sc_ref, rw_ref, rb_ref, xm_ref, gt_ref, ids_ref):
    xm = _modulated(x_ref[...], g_ref[...], sh_ref[...], sc_ref[...])
    xm_ref[...] = xm
    logits = _mm_nt(rw_ref[...], xm, HI)
    ex = jnp.exp(logits - jnp.max(logits, axis=0, keepdims=True))
    scores = ex / jnp.sum(ex, axis=0, keepdims=True)
    sel = scores + rb_ref[...]
    per = N_EXPERTS // N_EXPERT_GROUPS
    s = [sel[e:e + 1, :] for e in range(N_EXPERTS)]
    grp_score = []
    for g in range(N_EXPERT_GROUPS):
        a, b, c, d = s[per * g:per * (g + 1)]
        hi1, lo1, hi2, lo2 = jnp.maximum(a, b), jnp.minimum(a, b), jnp.maximum(c, d), jnp.minimum(c, d)
        grp_score.append(jnp.maximum(hi1, hi2) + jnp.maximum(jnp.minimum(hi1, hi2), jnp.maximum(lo1, lo2)))
    best = functools.reduce(jnp.maximum, grp_score)
    in_grp, taken = [], jnp.zeros_like(best)
    for g in range(N_EXPERT_GROUPS):
        hit = jnp.where(grp_score[g] == best, 1.0, 0.0) * (1.0 - taken)
        in_grp.append(hit)
        taken = taken + hit
    picked, flag = [], []
    for e in range(N_EXPERTS):
        g = e // per
        rank = jnp.zeros_like(best)
        for o in range(per * g, per * (g + 1)):
            if o < e:
                rank += jnp.where(s[o] >= s[e], 1.0, 0.0)
            elif o > e:
                rank += jnp.where(s[o] > s[e], 1.0, 0.0)
        flag.append(in_grp[g] * jnp.where(rank < 2.0, 1.0, 0.0))
        picked.append(flag[e] * scores[e:e + 1, :])
    total = functools.reduce(lambda x, y: x + y, picked)
    for e in range(N_EXPERTS):
        gt_ref[e:e + 1, :] = picked[e] / total
    lo_id = functools.reduce(jnp.minimum, [jnp.where(flag[e] > 0.0, float(e), float(N_EXPERTS)) for e in range(N_EXPERTS)])
    hi_id = functools.reduce(jnp.maximum, [jnp.where(flag[e] > 0.0, float(e), -1.0) for e in range(N_EXPERTS)])
    ids_ref[0:1, :] = lo_id
    ids_ref[1:2, :] = hi_id
    ids_ref[2:, :] = jnp.zeros((ids_ref.shape[0] - 2,) + lo_id.shape[1:], F32)


def _router(x, norm_g, mod, router_wt, router_b, layer):
    tm = 512
    return pl.pallas_call(
        _router_kernel,
        out_shape=(jax.ShapeDtypeStruct((N_TOK, D_MODEL), F32), jax.ShapeDtypeStruct((N_EXPERTS, N_TOK), F32),
                   jax.ShapeDtypeStruct((8, N_TOK), F32)),
        grid=(N_TOK // tm,),
        in_specs=[pl.BlockSpec((tm, D_MODEL), lambda i: (i, 0)),
                  pl.BlockSpec((None, 1, D_MODEL), lambda i: (layer, 0, 0)),
                  _mod_spec(layer, 3, tm), _mod_spec(layer, 4, tm),
                  _full((N_EXPERTS, D_MODEL)), _full((N_EXPERTS, 1))],
        out_specs=(pl.BlockSpec((tm, D_MODEL), lambda i: (i, 0)), pl.BlockSpec((N_EXPERTS, tm), lambda i: (0, i)),
                   pl.BlockSpec((8, tm), lambda i: (0, i))),
        compiler_params=_params(("parallel",)),
        name="router",
    )(x, norm_g.reshape(DEPTH, 1, D_MODEL), mod, mod, router_wt, router_b.reshape(N_EXPERTS, 1))


def _experts_kernel(xm_ref, gates_ref, w1_ref, w3_ref, w2_ref, x_ref, g_ref, o_ref, acc_ref):
    e = pl.program_id(1)

    @pl.when(e == 0)
    def _():
        acc_ref[...] = jnp.zeros_like(acc_ref)

    xm = xm_ref[...]
    h1 = _mm(xm, w1_ref[...])
    h3 = _mm(xm, w3_ref[...])
    gates = gates_ref[...]
    lane = lax.broadcasted_iota(jnp.int32, gates.shape, 1)
    gate = jnp.sum(jnp.where(lane == e, gates, 0.0), axis=-1, keepdims=True)
    hh = h1 * _sigmoid(h1) * h3 * gate
    acc_ref[...] += _mm(hh.astype(BF16), w2_ref[...])

    @pl.when(e == N_EXPERTS - 1)
    def _():
        o_ref[...] = x_ref[...] + g_ref[...] * acc_ref[...]


def _experts(xm, gates, w1_b, w3_b, w2_b, x, mod, layer):
    tm = 512
    row = pl.BlockSpec((tm, D_MODEL), lambda i, e: (i, 0))
    return pl.pallas_call(
        _experts_kernel,
        out_shape=jax.ShapeDtypeStruct((N_TOK, D_MODEL), F32),
        grid=(N_TOK // tm, N_EXPERTS),
        in_specs=[row, pl.BlockSpec((tm, N_EXPERTS), lambda i, e: (i, 0)),
                  pl.BlockSpec((None, None, D_MODEL, D_EXPERT), lambda i, e: (layer, e, 0, 0)),
                  pl.BlockSpec((None, None, D_MODEL, D_EXPERT), lambda i, e: (layer, e, 0, 0)),
                  pl.BlockSpec((None, None, D_EXPERT, D_MODEL), lambda i, e: (layer, e, 0, 0)),
                  row, _mod_spec(layer, 5, tm)],
        out_specs=row,
        scratch_shapes=[pltpu.VMEM((tm, D_MODEL), F32)],
        compiler_params=_params(("parallel", "arbitrary")),
        name="experts",
    )(xm, gates, w1_b, w3_b, w2_b, x, mod)


MOE_TM = 512
N_PAIRS = N_EXPERT_GROUPS * 6
MOE_STEPS = 2 * (N_PAIRS + N_TOK // MOE_TM - 1)


def _gather_rows(src_hbm, dst_ref, idx_ref, base, n, sem):
    def issue(r, carry):
        pltpu.make_async_copy(src_hbm.at[pl.ds(idx_ref[base + r], 1), :], dst_ref.at[pl.ds(r, 1), :], sem).start()
        return carry

    def wait(r, carry):
        pltpu.make_async_copy(src_hbm.at[pl.ds(0, 1), :], dst_ref.at[pl.ds(r, 1), :], sem).wait()
        return carry

    lax.fori_loop(0, n, issue, 0, unroll=8)
    lax.fori_loop(0, n, wait, 0, unroll=8)


def _moe_plan(ids):
    i32 = jnp.int32
    lo, hi = ids[0].astype(i32), ids[1].astype(i32)
    src = jnp.argsort(lo * N_EXPERTS + hi).astype(i32)
    pos = jnp.argsort(src).astype(i32)
    n_tiles = N_TOK // MOE_TM
    ex = jnp.arange(N_EXPERTS, dtype=i32)
    lo_s, hi_s = lo[src].reshape(n_tiles, MOE_TM, 1), hi[src].reshape(n_tiles, MOE_TM, 1)
    used = ((lo_s == ex).any(axis=1) | (hi_s == ex).any(axis=1)).reshape(-1)
    n_valid = jnp.sum(used).astype(i32)
    idx = jnp.nonzero(used, size=MOE_STEPS, fill_value=0)[0].astype(i32)
    valid = jnp.arange(MOE_STEPS, dtype=i32) < n_valid
    idx = jnp.where(valid, idx, idx[jnp.maximum(n_valid - 1, 0)])
    tile, exp = idx // N_EXPERTS, idx % N_EXPERTS
    first = valid & (tile != jnp.concatenate([jnp.full((1,), -1, i32), tile[:-1]]))
    return src, pos, tile, exp, first.astype(i32), valid.astype(i32)


def _moe_kernel(tile_ref, exp_ref, first_ref, valid_ref, src_ref, xm_hbm, gates_ref, w1_ref, w3_ref, w2_ref,
                o_ref, xs_ref, xb_ref, sem):
    s = pl.program_id(0)

    @pl.when(first_ref[s] == 1)
    def _():
        _gather_rows(xm_hbm, xs_ref, src_ref, tile_ref[s] * MOE_TM, MOE_TM, sem)
        xb_ref[...] = xs_ref[...].astype(BF16)
        o_ref[...] = jnp.zeros_like(o_ref)

    @pl.when(valid_ref[s] == 1)
    def _():
        xb = xb_ref[...]
        h1 = _mm(xb, w1_ref[...].astype(BF16))
        h3 = _mm(xb, w3_ref[...].astype(BF16))
        gates = gates_ref[...]
        lane = lax.broadcasted_iota(jnp.int32, gates.shape, 1)
        gate = jnp.sum(jnp.where(lane == exp_ref[s], gates, 0.0), axis=-1, keepdims=True)
        hh = h1 * _sigmoid(h1) * h3 * gate
        o_ref[...] += _mm(hh.astype(BF16), w2_ref[...].astype(BF16))


def _moe(xm, gates_sorted, plan, w1_b, w3_b, w2_b, layer):
    src, _, tile, exp, first, valid = plan
    w_in = pl.BlockSpec((None, None, D_MODEL, D_EXPERT), lambda s, t, e, f, v, i: (layer, e[s], 0, 0))
    grid_spec = pltpu.PrefetchScalarGridSpec(
        num_scalar_prefetch=5, grid=(MOE_STEPS,),
        in_specs=[pl.BlockSpec(memory_space=pl.ANY),
                  pl.BlockSpec((MOE_TM, N_EXPERTS), lambda s, t, e, f, v, i: (t[s], 0)),
                  w_in, w_in,
                  pl.BlockSpec((None, None, D_EXPERT, D_MODEL), lambda s, t, e, f, v, i: (layer, e[s], 0, 0))],
        out_specs=pl.BlockSpec((MOE_TM, D_MODEL), lambda s, t, e, f, v, i: (t[s], 0)),
        scratch_shapes=[pltpu.VMEM((MOE_TM, D_MODEL), F32), pltpu.VMEM((MOE_TM, D_MODEL), BF16),
                        pltpu.SemaphoreType.DMA(())])
    return pl.pallas_call(
        _moe_kernel, out_shape=jax.ShapeDtypeStruct((N_TOK, D_MODEL), F32), grid_spec=grid_spec,
        compiler_params=_params(("arbitrary",)), name="moe",
    )(tile, exp, first, valid, src, xm, gates_sorted, w1_b, w3_b, w2_b)


def _combine_kernel(pos_ref, acc_hbm, x_ref, g_ref, o_ref, buf_ref, sem):
    _gather_rows(acc_hbm, buf_ref, pos_ref, pl.program_id(0) * MOE_TM, MOE_TM, sem)
    o_ref[...] = x_ref[...] + g_ref[...] * buf_ref[...]


def _combine(acc_sorted, pos, x, mod, layer):
    tm = MOE_TM
    row = pl.BlockSpec((tm, D_MODEL), lambda i, p: (i, 0))
    grid_spec = pltpu.PrefetchScalarGridSpec(
        num_scalar_prefetch=1, grid=(N_TOK // tm,),
        in_specs=[pl.BlockSpec(memory_space=pl.ANY), row,
                  pl.BlockSpec((None, None, None, 1, D_MODEL), lambda i, p: (layer, _mod_row(i, tm), 5, 0, 0))],
        out_specs=row,
        scratch_shapes=[pltpu.VMEM((tm, D_MODEL), F32), pltpu.SemaphoreType.DMA(())])
    return pl.pallas_call(
        _combine_kernel, out_shape=jax.ShapeDtypeStruct((N_TOK, D_MODEL), F32), grid_spec=grid_spec,
        compiler_params=_params(("arbitrary",)), name="combine",
    )(pos, acc_sorted, x, mod)


def _gate_rows(gcols, n_seq, seq):
    return gcols.reshape(n_seq, seq // CHUNK, CHUNK, gcols.shape[-1]).transpose(0, 1, 3, 2)


def kernel(x_prompt, x_sample, cache_na_k, cache_na_v, state_rwkv, state_mlstm_c, state_mlstm_n, state_mlstm_m,
           c, c_ctx, norm1_g, norm2_g, w_mod, b_mod, w_in, conv_w, na_q_g, na_k_g, na_rpb, rw_w0, rw_w_up, rw_a0,
           rw_a_up, rw_g_up, rw_k_k, rw_k_a, rw_r_k, rw_ln_g, rw_ln_b, ml_gate_b, ml_norm_g, w_out, router_w,
           router_b, moe_w1, moe_w3, moe_w2):
    p = dict(rw_w0=rw_w0, rw_w_up=rw_w_up, rw_a0=rw_a0, rw_a_up=rw_a_up, rw_g_up=rw_g_up, rw_k_k=rw_k_k,
             rw_k_a=rw_k_a, rw_r_k=rw_r_k, rw_ln_g=rw_ln_g, rw_ln_b=rw_ln_b, ml_gate_b=ml_gate_b,
             ml_norm_g=ml_norm_g)
    cvecs = jnp.concatenate([c_ctx[None], c, jnp.zeros((MOD_ROWS - 1 - DEC_BATCH, D_MODEL), F32)], axis=0)
    mod = _adaln(cvecs, w_mod, b_mod).reshape(DEPTH, MOD_ROWS, 6, 1, D_MODEL)

    assert w_in.shape[-1] == P_IN
    w_in_b, w_out_b, w1_b, w3_b, w2_b = jnp.swapaxes(w_in, 1, 2), w_out, moe_w1, moe_w3, moe_w2
    tw = _na_tables(na_rpb)
    router_wt = router_w.T
    sample_row_off = N_PROMPT // DEC_SEQ

    x = jnp.concatenate([x_prompt.reshape(N_PROMPT, D_MODEL), x_sample.reshape(N_SAMPLE, D_MODEL)], axis=0)
    kv_prev, rw_prev, ml_prev = (), (), ()
    for l in range(DEPTH):
        u = _inproj(x, norm1_g, mod, w_in_b, l)
        ya_p, yb_p, *kv_prev = _attn_prompt(u, conv_w, na_q_g, na_k_g, l, prev=tuple(kv_prev))
        ya_s, yb_s = _na_sample(u, cache_na_k, cache_na_v, tw, conv_w, na_q_g, na_k_g, l)
        yc_p, *rw_prev = _rwkv(u, p, l, BATCH, SEQ, 0, emit_state=True, prev=tuple(rw_prev))
        (yc_s,) = _rwkv(u, p, l, DEC_BATCH, DEC_SEQ, sample_row_off, s0=state_rwkv)
        g0 = CB_GATE * MIX_W + SM_GATE
        gcols = u[:, g0:g0 + 4 * ML_HEADS]
        yd_p, *ml_prev = _mlstm(u, _gate_rows(gcols[:N_PROMPT], BATCH, SEQ), p, l, BATCH, SEQ, 0,
                                emit_state=True, prev=tuple(ml_prev))
        (yd_s,) = _mlstm(u, _gate_rows(gcols[N_PROMPT:], DEC_BATCH, DEC_SEQ), p, l, DEC_BATCH, DEC_SEQ,
                         sample_row_off, state=(state_mlstm_c, state_mlstm_n, state_mlstm_m))
        ys = [jnp.concatenate(pair, axis=0) for pair in ((ya_p, ya_s), (yb_p, yb_s), (yc_p, yc_s), (yd_p, yd_s))]
        x = _outproj(ys, w_out_b, x, mod, l)
        xm, gates_t, ids = _router(x, norm2_g, mod, router_wt, router_b, l)
        plan = _moe_plan(ids)
        acc_sorted = _moe(xm, gates_t.T[plan[0]], plan, w1_b, w3_b, w2_b, l)
        x = _combine(acc_sorted, plan[1], x, mod, l)
    new_c, new_n, new_m = ml_prev
    return (x[:N_PROMPT].reshape(BATCH, SEQ, D_MODEL), x[N_PROMPT:].reshape(DEC_BATCH, DEC_SEQ, D_MODEL),
            kv_prev[0], kv_prev[1], rw_prev[0], new_c, new_n.reshape(BATCH, DEPTH, 2, ML_HEADS, ML_DH),
            new_m.reshape(BATCH, DEPTH, 2, ML_HEADS))
```

```python
import functools

import numpy as np
import jax
import jax.numpy as jnp
from jax import lax
from jax.experimental import pallas as pl
from jax.experimental.pallas import tpu as pltpu

F32 = jnp.float32
BF16 = jnp.bfloat16
HI = lax.Precision.HIGHEST

D_MODEL = 2048
BATCH = 16
SEQ = 256
DEPTH = 2
DEC_BATCH = 2
DEC_SEQ = 1024
PAST_LEN = 512
GRID_W = 64
MIX_W = D_MODEL // 4
CONV_K = 3
NA_DH = 64
NA_HEADS = MIX_W // NA_DH
NA_WIN_R = 8
NA_WIN_C = 16
NA_SCALE = NA_DH ** -0.5
ROPE_THETA = 10000.0
RW_DH = 64
RW_HEADS = MIX_W // RW_DH
RW_W_RANK = 64
RW_A_RANK = 64
RW_G_RANK = 128
RW_DECAY_SCALE = 0.606531
RW_GN_EPS = 64e-5
ML_DH = 128
ML_HEADS = MIX_W // ML_DH
N_EXPERTS = 16
N_EXPERT_GROUPS = 4
D_EXPERT = 512
EPS = 1e-6
NEG_INF = -1e30

N_PROMPT = BATCH * SEQ
N_SAMPLE = DEC_BATCH * DEC_SEQ
N_TOK = N_PROMPT + N_SAMPLE
MOD_ROWS = 8
CHUNK = 64
SUB = 16
P1_CHUNKS = 4
NA_ROWS_PER_TRIP = 4
P_BLOCKS = 15
P_PAD = P_BLOCKS * MIX_W
(CB_CVB, CB_CVC, CB_CVH, CB_NAQ, CB_NAK, CB_NAV, CB_RWR, CB_RWK, CB_RWV,
 CB_MLQ, CB_MLK, CB_MLV, CB_MLO, CB_SMALL, CB_GATE) = range(P_BLOCKS)
SM_WL, SM_AL, SM_GL = 0, 64, 128
SM_GATE = MIX_W - 4 * ML_HEADS
VMEM_LIMIT = 56 * 1024 * 1024


def _mm(a, b, prec=None):
    return jnp.dot(a, b, precision=prec, preferred_element_type=F32)


def _mm_nt(a, b, prec=None):
    return lax.dot_general(a, b, (((1,), (1,)), ((), ())), precision=prec, preferred_element_type=F32)


def _mm_tn(a, b, prec=None):
    return _mm(a.T, b, prec)


def _sigmoid(x):
    return 1.0 / (1.0 + jnp.exp(-x))


def _full(shape):
    n = len(shape)
    return pl.BlockSpec(shape, lambda *_: (0,) * n)


def _params(sem):
    return pltpu.CompilerParams(dimension_semantics=sem, vmem_limit_bytes=VMEM_LIMIT)


def _mod_row(i, tm):
    n_prompt_tiles = N_PROMPT // tm
    tiles_per_sample = DEC_SEQ // tm
    return jnp.where(i < n_prompt_tiles, 0, 1 + (i - n_prompt_tiles) // tiles_per_sample)


def _mod_spec(layer, chunk, tm, tn=D_MODEL, with_j=False):
    if with_j:
        return pl.BlockSpec((None, None, None, 1, tn), lambda i, j: (layer, _mod_row(i, tm), chunk, 0, j))
    return pl.BlockSpec((None, None, None, 1, tn), lambda i, *_: (layer, _mod_row(i, tm), chunk, 0, 0))


def _tri_masks(n):
    t = lax.broadcasted_iota(jnp.int32, (n, n), 0)
    s = lax.broadcasted_iota(jnp.int32, (n, n), 1)
    incl = ((s <= t).astype(F32), (s >= t).astype(F32))
    strict = ((s < t).astype(F32), (s > t).astype(F32))
    return incl, strict, t, s


def _adaln_kernel(cv_ref, w_ref, b_ref, o_ref):
    cv = cv_ref[...]
    o_ref[...] = _mm(cv * _sigmoid(cv), w_ref[...], HI) + b_ref[...]


def _adaln(cvecs, w_mod, b_mod):
    tn = 1024
    n_out = 6 * D_MODEL
    return pl.pallas_call(
        _adaln_kernel,
        out_shape=jax.ShapeDtypeStruct((DEPTH, MOD_ROWS, n_out), F32),
        grid=(DEPTH, n_out // tn),
        in_specs=[_full((MOD_ROWS, D_MODEL)),
                  pl.BlockSpec((None, D_MODEL, tn), lambda l, j: (l, 0, j)),
                  pl.BlockSpec((None, 1, tn), lambda l, j: (l, 0, j))],
        out_specs=pl.BlockSpec((None, MOD_ROWS, tn), lambda l, j: (l, 0, j)),
        compiler_params=_params(("parallel", "parallel")),
        name="adaln",
    )(cvecs, w_mod, b_mod.reshape(DEPTH, 1, n_out))


def _modulated(x, g, sh, sc):
    y = x * lax.rsqrt(jnp.mean(x * x, axis=-1, keepdims=True) + EPS) * g
    return y * (1.0 + sc) + sh


def _inproj_kernel(x_ref, g_ref, sh_ref, sc_ref, w_ref, o_ref, xm_ref):
    @pl.when(pl.program_id(1) == 0)
    def _():
        xm_ref[...] = _modulated(x_ref[...], g_ref[...], sh_ref[...], sc_ref[...]).astype(BF16)

    o_ref[...] = _mm_nt(xm_ref[...], w_ref[0].astype(BF16))


P_IN = 13 * MIX_W + RW_W_RANK + RW_A_RANK + RW_G_RANK + 4 * ML_HEADS


def _inproj_src_row(j):
    a = 9
    narrow = RW_W_RANK + RW_A_RANK + RW_G_RANK
    g = 16
    return g * jnp.where(j < a, j * (MIX_W // g),
                         jnp.where(j < CB_SMALL, j * (MIX_W // g) + narrow // g,
                                   jnp.where(j == CB_SMALL, a * MIX_W // g, (P_IN - MIX_W) // g)))


def _inproj(x, norm_g, mod, w_in_t, layer):
    tm, tn = 1024, MIX_W
    return pl.pallas_call(
        _inproj_kernel,
        out_shape=jax.ShapeDtypeStruct((N_TOK, P_PAD), F32),
        grid=(N_TOK // tm, P_BLOCKS),
        in_specs=[pl.BlockSpec((tm, D_MODEL), lambda i, j: (i, 0)),
                  pl.BlockSpec((None, 1, D_MODEL), lambda i, j: (layer, 0, 0)),
                  _mod_spec(layer, 0, tm), _mod_spec(layer, 1, tm),
                  pl.BlockSpec((pl.Element(1), pl.Element(tn), pl.Element(D_MODEL)),
                               lambda i, j: (layer, _inproj_src_row(j), 0))],
        out_specs=pl.BlockSpec((tm, tn), lambda i, j: (i, j)),
        scratch_shapes=[pltpu.VMEM((tm, D_MODEL), BF16)],
        compiler_params=_params(("parallel", "arbitrary")),
        name="inproj",
    )(x, norm_g.reshape(DEPTH, 1, D_MODEL), mod, mod, w_in_t)


def _conv_mix(b, c, h, w):
    u = c * h
    n = u.shape[0]
    row = lax.broadcasted_iota(jnp.int32, u.shape, 0)
    prev = jnp.where(row == 0, 0.0, pltpu.roll(u, 1, axis=0))
    nxt = jnp.where(row == n - 1, 0.0, pltpu.roll(u, n - 1, axis=0))
    return b * (prev * w[0:1] + u * w[1:2] + nxt * w[2:3])


def _head_rms(x, g):
    return x * lax.rsqrt(jnp.mean(x * x, axis=-1, keepdims=True) + EPS) * g


def _attn_prompt_kernel(cb_ref, cc_ref, ch_ref, q_ref, k_ref, v_ref, cw_ref, qg_ref, kg_ref,
                        ya_ref, yb_ref, nk_ref, nv_ref):
    ya_ref[...] = _conv_mix(cb_ref[...], cc_ref[...], ch_ref[...], cw_ref[...]).astype(ya_ref.dtype)
    sls = [slice(h * NA_DH, (h + 1) * NA_DH) for h in range(NA_HEADS)]
    qn = [_head_rms(q_ref[:, sl], qg_ref[...]) * NA_SCALE for sl in sls]
    kn = [_head_rms(k_ref[:, sl], kg_ref[...]) for sl in sls]
    vh = [v_ref[:, sl] for sl in sls]
    s = [_mm_nt(q.astype(BF16), k.astype(BF16)) for q, k in zip(qn, kn)]
    p = [jnp.exp(x - jnp.max(x, axis=-1, keepdims=True)) for x in s]
    o = [_mm(x.astype(BF16), v.astype(BF16)) / jnp.sum(x, axis=-1, keepdims=True) for x, v in zip(p, vh)]
    yb_ref[...] = jnp.concatenate(o, axis=1).astype(yb_ref.dtype)
    for h in range(NA_HEADS):
        nk_ref[h] = kn[h]
        nv_ref[h] = vh[h]


def _u_spec(rows, col_block, row_off_blocks=0):
    return pl.BlockSpec((rows, MIX_W), lambda b: (b + row_off_blocks, col_block))


def _carry_through(kernel, n_inputs, prev, n_plain_out, n_state_out, layer):
    n_prev = len(prev)

    def body(*refs):
        ins, rest = refs[:n_inputs], list(refs[n_inputs + n_prev:])
        if not n_prev:
            for i in range(n_plain_out, n_plain_out + n_state_out):
                full = rest[i]
                for d in range(DEPTH):
                    if d != layer:
                        full[d] = jnp.zeros(full.shape[1:], full.dtype)
                rest[i] = full.at[layer]
        return kernel(*ins, *rest)

    return body, [pl.BlockSpec(memory_space=pl.ANY)] * n_prev


def _state_spec(shape, layer, first, index):
    def index_map(*g):
        b, *tail = index(*g)
        return (b, 0 if first else layer, *tail)
    return pl.BlockSpec((None, DEPTH if first else None) + tuple(shape), index_map)


def _attn_prompt(u, conv_w, q_g, k_g, layer, prev=(), n_seq=BATCH, seq=SEQ):
    lw = lambda shape: pl.BlockSpec((None,) + shape, lambda b: (layer,) + (0,) * len(shape))
    y_spec = pl.BlockSpec((seq, MIX_W), lambda b: (b, 0))
    kv_spec = _state_spec((NA_HEADS, seq, NA_DH), layer, not prev, lambda b: (b, 0, 0, 0))
    in_specs = ([_u_spec(seq, cb) for cb in (CB_CVB, CB_CVC, CB_CVH, CB_NAQ, CB_NAK, CB_NAV)]
                + [lw((CONV_K, MIX_W)), lw((1, NA_DH)), lw((1, NA_DH))])
    body, prev_specs = _carry_through(_attn_prompt_kernel, len(in_specs), prev, 2, 2, layer)
    return pl.pallas_call(
        body,
        out_shape=(jax.ShapeDtypeStruct((n_seq * seq, MIX_W), BF16),) * 2
        + (jax.ShapeDtypeStruct((n_seq, DEPTH, NA_HEADS, seq, NA_DH), F32),) * 2,
        grid=(n_seq,),
        in_specs=in_specs + prev_specs,
        out_specs=(y_spec, y_spec, kv_spec, kv_spec),
        input_output_aliases={len(in_specs) + i: 2 + i for i in range(len(prev))},
        compiler_params=_params(("parallel",)),
        name="attn_prompt",
    )(u, u, u, u, u, u, conv_w, q_g.reshape(DEPTH, 1, NA_DH), k_g.reshape(DEPTH, 1, NA_DH), *prev)


def _na_kernel(cb_ref, cc_ref, ch_ref, q_ref, k_ref, v_ref, kc_ref, vc_ref, tw_ref, cos_ref, sin_ref,
               perm_ref, cw_ref, qg_ref, kg_ref, ya_ref, yb_ref, qs_ref, ks_ref, tws_ref):
    rows = DEC_SEQ // GRID_W
    wr = min(NA_WIN_R, rows)
    nw = wr * GRID_W
    ya_ref[...] = _conv_mix(cb_ref[...], cc_ref[...], ch_ref[...], cw_ref[...]).astype(ya_ref.dtype)
    cos, sin, perm_b = cos_ref[...], sin_ref[...], perm_ref[...].astype(BF16)

    def rope(x):
        hi, lo = _split2(x)
        return x * cos + (_mm(hi, perm_b) + _mm(lo, perm_b)) * sin

    for h in range(NA_HEADS):
        sl = slice(h * NA_DH, (h + 1) * NA_DH)
        qs_ref[...] = (rope(_head_rms(q_ref[:, sl], qg_ref[...])) * NA_SCALE).astype(BF16)
        ks_ref[...] = rope(_head_rms(k_ref[:, sl], kg_ref[...])).astype(BF16)
        kch = kc_ref[h].astype(BF16)
        vch = vc_ref[h].astype(BF16)
        for p in range(wr):
            tws_ref[p] = jnp.concatenate([tw_ref[h, j - p + NA_WIN_R - 1] for j in range(wr)], axis=1)

        def rows_step(t, carry):
            rr = [t * NA_ROWS_PER_TRIP + i for i in range(NA_ROWS_PER_TRIP)]
            rs = [jnp.clip(r - wr // 2, 0, rows - wr) for r in rr]
            q0 = [pl.multiple_of(r * GRID_W, GRID_W) for r in rr]
            k0 = [pl.multiple_of(x * GRID_W, GRID_W) for x in rs]
            q_r = [qs_ref[pl.ds(x, GRID_W), :] for x in q0]
            s_w = [_mm_nt(q, ks_ref[pl.ds(k, nw), :]) + tws_ref[r - x] for q, k, r, x in zip(q_r, k0, rr, rs)]
            s_c = [_mm_nt(q, kch) for q in q_r]
            m = [jnp.maximum(jnp.max(a, axis=-1, keepdims=True), jnp.max(b, axis=-1, keepdims=True))
                 for a, b in zip(s_w, s_c)]
            p_w = [jnp.exp(a - x) for a, x in zip(s_w, m)]
            p_c = [jnp.exp(b - x) for b, x in zip(s_c, m)]
            den = [jnp.sum(a, axis=-1, keepdims=True) + jnp.sum(b, axis=-1, keepdims=True) for a, b in zip(p_w, p_c)]
            v_w = [v_ref[pl.ds(k, nw), sl].astype(BF16) for k in k0]
            o = [(_mm(a.astype(BF16), v) + _mm(b.astype(BF16), vch)) / d for a, b, v, d in zip(p_w, p_c, v_w, den)]
            for x, val in zip(q0, o):
                yb_ref[pl.ds(x, GRID_W), sl] = val.astype(yb_ref.dtype)
            return carry

        lax.fori_loop(0, rows // NA_ROWS_PER_TRIP, rows_step, 0)


def _na_tables(rpb):
    rows = DEC_SEQ // GRID_W
    wr = min(NA_WIN_R, rows)
    qc = np.arange(GRID_W)
    kc = np.arange(GRID_W)
    wstart = np.clip(qc - NA_WIN_C // 2, 0, GRID_W - NA_WIN_C)
    colmask = (kc[None, :] >= wstart[:, None]) & (kc[None, :] < wstart[:, None] + NA_WIN_C)
    dc = np.clip(kc[None, :] - qc[:, None], -(NA_WIN_C - 1), NA_WIN_C - 1) + NA_WIN_C - 1
    bias = rpb[:, :, :, dc]
    return jnp.where(colmask[None, None, None], bias, NEG_INF)


def _rope_tables():
    t = np.arange(DEC_SEQ)
    quarter = NA_DH // 4
    freq = ROPE_THETA ** (-np.arange(quarter, dtype=np.float32) / quarter)
    ang_r = (t // GRID_W).astype(np.float32)[:, None] * freq
    ang_c = (t % GRID_W).astype(np.float32)[:, None] * freq
    cos = np.concatenate([np.cos(ang_r), np.cos(ang_r), np.cos(ang_c), np.cos(ang_c)], axis=-1)
    sin = np.concatenate([-np.sin(ang_r), np.sin(ang_r), -np.sin(ang_c), np.sin(ang_c)], axis=-1)
    src = np.concatenate([np.arange(quarter) + quarter, np.arange(quarter),
                          np.arange(quarter) + 3 * quarter, np.arange(quarter) + 2 * quarter])
    perm = np.zeros((NA_DH, NA_DH), np.float32)
    perm[src, np.arange(NA_DH)] = 1.0
    return cos.astype(np.float32), sin.astype(np.float32), perm


def _na_sample(u, cache_k, cache_v, tw, conv_w, q_g, k_g, layer, n_seq=DEC_BATCH, row_off=N_PROMPT // DEC_SEQ):
    cos, sin, perm = _rope_tables()
    lw = lambda shape: pl.BlockSpec((None,) + shape, lambda b: (layer,) + (0,) * len(shape))
    y_spec = pl.BlockSpec((DEC_SEQ, MIX_W), lambda b: (b, 0))
    c_spec = pl.BlockSpec((None, None, NA_HEADS, PAST_LEN, NA_DH), lambda b: (b, layer, 0, 0, 0))
    wr = min(NA_WIN_R, DEC_SEQ // GRID_W)
    n_off = 2 * NA_WIN_R - 1
    return pl.pallas_call(
        _na_kernel,
        out_shape=(jax.ShapeDtypeStruct((n_seq * DEC_SEQ, MIX_W), BF16),) * 2,
        grid=(n_seq,),
        in_specs=[_u_spec(DEC_SEQ, cb, row_off) for cb in (CB_CVB, CB_CVC, CB_CVH, CB_NAQ, CB_NAK, CB_NAV)]
        + [c_spec, c_spec, lw((NA_HEADS, n_off, GRID_W, GRID_W)),
           _full((DEC_SEQ, NA_DH)), _full((DEC_SEQ, NA_DH)), _full((NA_DH, NA_DH)),
           lw((CONV_K, MIX_W)), lw((1, NA_DH)), lw((1, NA_DH))],
        out_specs=(y_spec, y_spec),
        scratch_shapes=[pltpu.VMEM((DEC_SEQ, NA_DH), BF16), pltpu.VMEM((DEC_SEQ, NA_DH), BF16),
                        pltpu.VMEM((wr, GRID_W, wr * GRID_W), F32)],
        compiler_params=_params(("parallel",)),
        name="na_sample",
    )(u, u, u, u, u, u, cache_k, cache_v, tw, jnp.asarray(cos), jnp.asarray(sin), jnp.asarray(perm),
      conv_w, q_g.reshape(DEPTH, 1, NA_DH), k_g.reshape(DEPTH, 1, NA_DH))


def _seg_ones(width, seg):
    a = lax.broadcasted_iota(jnp.int32, (width, width), 0) // seg
    b = lax.broadcasted_iota(jnp.int32, (width, width), 1) // seg
    return (a == b).astype(F32)


def _split2(x):
    hi = x.astype(BF16)
    return hi, (x - hi.astype(F32)).astype(BF16)


def _mm1(a, b):
    return _mm(a.astype(BF16), b.astype(BF16))


def _mm3(a, b, nt=False):
    dot = _mm_nt if nt else _mm
    a_hi, a_lo = _split2(a)
    b_hi, b_lo = _split2(b)
    return dot(a_hi, b_lo) + dot(a_lo, b_hi) + dot(a_hi, b_hi)


def _split3(x):
    x0 = x.astype(BF16)
    r1 = x - x0.astype(F32)
    x1 = r1.astype(BF16)
    return x0, x1, (r1 - x1.astype(F32)).astype(BF16)


def _mm_exact_lhs(mask_b, x):
    x0, x1, x2 = _split3(x)
    return _mm(mask_b, x2) + _mm(mask_b, x1) + _mm(mask_b, x0)


def _mm_exact_rhs(x, mask_b):
    x0, x1, x2 = _split3(x)
    return _mm(x2, mask_b) + _mm(x1, mask_b) + _mm(x0, mask_b)


def _rwkv_kernel(*refs, seq, has_s0, emit_state):
    it = iter(refs)
    r_ref, k_ref, v_ref, sm_ref = (next(it) for _ in range(4))
    (w0_ref, wup_ref, a0_ref, aup_ref, gup_ref, kkp_ref, kap_ref, rkp_ref, lng_ref, lnb_ref) = (
        next(it) for _ in range(10))
    mask_ref = next(it)
    s0_ref = next(it) if has_s0 else None
    y_ref = next(it)
    so_ref = next(it) if emit_state else None
    kk_ref, lw_ref, ka_ref, kd_ref, coef_ref, ysp_ref = (next(it) for _ in range(6))

    dh = RW_DH
    pw = 2 * dh
    nc = seq // CHUNK
    seg_b = _seg_ones(pw, dh).astype(BF16)

    r = r_ref[...]
    k = k_ref[...]
    sm = sm_ref[...]
    wl = jnp.tanh(sm[:, SM_WL:SM_WL + RW_W_RANK])
    al = sm[:, SM_AL:SM_AL + RW_A_RANK]
    kk = k * kkp_ref[...]
    kk = kk * lax.rsqrt(_mm_exact_rhs(kk * kk, seg_b) + EPS)
    kk_ref[...] = kk
    for z in range(2):
        lw_ref[z] = -RW_DECAY_SCALE * _sigmoid(w0_ref[z:z + 1, :] + _mm3(wl, wup_ref[z]))
        a = _sigmoid(a0_ref[z:z + 1, :] + _mm3(al, aup_ref[z]))
        ka_ref[z] = kk * a
        kd_ref[z] = k * (1.0 + (a - 1.0) * kap_ref[...])

    cat = jnp.concatenate

    def phase1(c2, carry):
        incl, strict, ti, si = _tri_masks(CHUNK)
        incl_b = (mask_ref[0], mask_ref[1])
        strict2 = tuple(cat([m, m], axis=1) for m in strict)
        incl2 = tuple(cat([m, m], axis=1) for m in incl)
        diag_blk = (ti // SUB == si // SUB).astype(F32)
        eye = (ti == si).astype(F32)
        zero_pair = jnp.zeros((CHUNK, pw), F32)
        head_of_lane = lax.broadcasted_iota(jnp.int32, (CHUNK, pw), 1) // dh
        head_of_lane2 = (lax.broadcasted_iota(jnp.int32, (CHUNK, 2 * pw), 1) // dh) % 2
        r_pw = lax.broadcasted_iota(jnp.int32, (pw, pw), 0)
        c_pw = lax.broadcasted_iota(jnp.int32, (pw, pw), 1)
        same_head = r_pw // dh == c_pw // dh
        eye_pw = (r_pw == c_pw).astype(F32)
        chains = []
        for cc in range(P1_CHUNKS):
            rows = pl.ds(pl.multiple_of((c2 * P1_CHUNKS + cc) * CHUNK, CHUNK), CHUNK)
            vc, rc, kkc = v_ref[rows, :], r_ref[rows, :], kk_ref[rows, :]
            for z in range(2):
                lwc = lw_ref[z, rows, :]
                cum = _mm_exact_lhs(incl_b[z], lwc)
                tot = cum[CHUNK - 1:CHUNK] if z == 0 else cum[0:1]
                e_neg = jnp.exp(-cum)
                dec = jnp.exp(tot - cum)
                kac, kdc = ka_ref[z, rows, :], kd_ref[z, rows, :]
                chains.append((z, kkc * jnp.exp(cum - lwc), rc * jnp.exp(cum), vc, kac * e_neg, kdc * e_neg,
                               kac * dec, kdc * dec, jnp.exp(tot)))
        heads = [(q, j) for q in range(len(chains)) for j in range(2)]
        zq = [ch[0] for ch in chains]
        ymat = [cat([ch[4], ch[5]], axis=0) for ch in chains]
        vz = [cat([zero_pair, ch[3]], axis=0) for ch in chains]
        nvz = [cat([zero_pair, -ch[3]], axis=1) for ch in chains]
        x_in = [cat([jnp.where(head_of_lane == j, chains[q][1], 0.0),
                     jnp.where(head_of_lane == j, chains[q][2], 0.0)], axis=0) for q, j in heads]
        aa = [_mm_nt(x.astype(BF16), ymat[q].astype(BF16)) for x, (q, _) in zip(x_in, heads)]
        top = [a[0:CHUNK] * strict2[zq[q]] for a, (q, _) in zip(aa, heads)]
        a_r = [a[CHUNK:] * incl2[zq[q]] for a, (q, _) in zip(aa, heads)]
        akv = [_mm1(t, vz[q]) for t, (q, _) in zip(top, heads)]
        low = [t[:, 0:CHUNK] for t in top]
        ld = [x * diag_blk for x in low]
        lo = [x - y for x, y in zip(low, ld)]
        l2 = [_mm1(x, x) for x in ld]
        l4 = [_mm1(x, x) for x in l2]
        l8 = [_mm1(x, x) for x in l4]
        td = [eye - x for x in ld]
        for lp in (l2, l4, l8):
            td = [t + _mm1(t, p_) for t, p_ in zip(td, lp)]
        x0 = [_mm1(t, cat([chains[q][1], a, l], axis=1)) for t, a, l, (q, _) in zip(td, akv, lo, heads)]
        pq0 = [x[:, 0:2 * pw] for x in x0]
        wm = [x[:, 2 * pw:] for x in x0]
        pq = pq0
        for _ in range(CHUNK // SUB - 1):
            pq = [p0 - _mm1(w, p_) for p0, w, p_ in zip(pq0, wm, pq)]
        ryc = [_mm1(a, cat([p_, nvz[q]], axis=0)) for a, p_, (q, _) in zip(a_r, pq, heads)]
        for q, ch in enumerate(chains):
            pq_m = jnp.where(head_of_lane2 == 0, pq[2 * q], pq[2 * q + 1])
            ryc_m = jnp.where(head_of_lane2 == 0, ryc[2 * q], ryc[2 * q + 1])
            g1 = _mm1(ch[6].T, pq_m)
            g2 = _mm1(ch[7].T, ch[3])
            g_t = eye_pw * ch[8] - jnp.where(same_head, g1[:, 0:pw], 0.0)
            h_t = jnp.where(same_head, g2 - g1[:, pw:], 0.0)
            coef_ref[c2 * P1_CHUNKS + q // 2, q % 2] = cat([g_t, ch[2] - ryc_m[:, 0:pw], h_t, -ryc_m[:, pw:]], axis=0)
        return carry

    lax.fori_loop(0, nc // P1_CHUNKS, phase1, 0)

    def block_diag(a, b):
        zero = jnp.zeros((dh, dh), F32)
        return cat([cat([a, zero], axis=1), cat([zero, b], axis=1)], axis=0)

    m_init = tuple((block_diag(s0_ref[z, 0].T, s0_ref[z, 1].T) if has_s0 else jnp.zeros((pw, pw), F32))
                   for z in range(2))
    n_lhs = pw + CHUNK

    def phase2(ci, ms):
        new_ms, ys = [], []
        for z in range(2):
            c = ci if z == 0 else nc - 1 - ci
            out = _mm3(coef_ref[c, z, 0:n_lhs, :], ms[z]) + coef_ref[c, z, n_lhs:, :]
            new_ms.append(out[0:pw])
            ys.append(out[pw:])
        ysp_ref[ci] = cat(ys, axis=1)
        return tuple(new_ms)

    m_fin = lax.fori_loop(0, nc, phase2, m_init)

    if emit_state:
        for z in range(2):
            for j in range(2):
                so_ref[z, j] = m_fin[z][j * dh:(j + 1) * dh, j * dh:(j + 1) * dh].T

    y = cat([ysp_ref[c, :, 0:pw] + ysp_ref[nc - 1 - c, :, pw:] for c in range(nc)], axis=0)
    mu = _mm_exact_rhs(y, seg_b) * (1.0 / dh)
    yc = y - mu
    var = _mm_exact_rhs(yc * yc, seg_b) * (1.0 / dh)
    yn = yc * lax.rsqrt(var + RW_GN_EPS) * lng_ref[...] + lnb_ref[...]
    bonus = _mm_exact_rhs(r * k * rkp_ref[...], seg_b) * v_ref[...]
    g = _mm3(_sigmoid(sm[:, SM_GL:SM_GL + RW_G_RANK]), gup_ref[...])
    y_ref[...] = ((yn + bonus) * g).astype(y_ref.dtype)
    return

    def phase1(c2, carry):
        incl, strict, ti, si = _tri_masks(CHUNK)
        incl_b = (mask_ref[0], mask_ref[1])
        incl2 = tuple(jnp.concatenate([m, m], axis=1) for m in incl)
        diag_blk = (ti // SUB == si // SUB).astype(F32)
        eye = (ti == si).astype(F32)
        zero_blk = jnp.zeros((CHUNK, dh), F32)
        inst = []
        for cc in range(P1_CHUNKS):
            rows = pl.ds(pl.multiple_of((c2 * P1_CHUNKS + cc) * CHUNK, CHUNK), CHUNK)
            vc, rc, kkc = v_ref[rows, :], r_ref[rows, :], kk_ref[rows, :]
            for z in range(2):
                lwc = lw_ref[z, rows, :]
                cum = _mm_exact_lhs(incl_b[z], lwc)
                tot = cum[CHUNK - 1:CHUNK] if z == 0 else cum[0:1]
                e_neg = jnp.exp(-cum)
                dec = jnp.exp(tot - cum)
                e_tot = jnp.exp(tot)
                kac, kdc = ka_ref[z, rows, :], kd_ref[z, rows, :]
                rt = rc * jnp.exp(cum)
                kt = kkc * jnp.exp(cum - lwc)
                at, kdt, ah, kh = kac * e_neg, kdc * e_neg, kac * dec, kdc * dec
                for j in range(2):
                    sl = slice(j * dh, (j + 1) * dh)
                    inst.append((z, kt[:, sl], rt[:, sl], vc[:, sl], at[:, sl], kdt[:, sl], ah[:, sl], kh[:, sl],
                                 e_tot[:, sl]))
        zs = [i[0] for i in inst]
        kt_h, rt_h, v_h = [i[1] for i in inst], [i[2] for i in inst], [i[3] for i in inst]
        each = lambda f, *xs: [f(*a) for a in zip(*xs)]
        aa = each(lambda i: _mm3(jnp.concatenate([i[1], i[2]], axis=0),
                                 jnp.concatenate([i[4], i[5]], axis=0), nt=True), inst)
        low = each(lambda a, z: a[0:CHUNK, 0:CHUNK] * strict[z], aa, zs)
        a_kk = each(lambda a, z: a[0:CHUNK, CHUNK:] * strict[z], aa, zs)
        a_r = each(lambda a, z: a[CHUNK:, :] * incl2[z], aa, zs)
        akv = each(_mm1, a_kk, v_h)
        ld = each(lambda x: x * diag_blk, low)
        lo = each(lambda x, y: x - y, low, ld)
        l2 = each(_mm1, ld, ld)
        l4 = each(_mm1, l2, l2)
        l8 = each(_mm1, l4, l4)
        td = each(lambda x: eye - x, ld)
        for lp in (l2, l4, l8):
            td = each(lambda t, p: t + _mm1(t, p), td, lp)
        x0 = each(lambda t, a, b, c_: _mm1(t, jnp.concatenate([a, b, c_], axis=1)), td, lo, kt_h, akv)
        wm = each(lambda x: x[:, 0:CHUNK], x0)
        pq0 = each(lambda x: x[:, CHUNK:], x0)
        pq = pq0
        for _ in range(CHUNK // SUB - 1):
            pq = each(lambda p0, w, p: p0 - _mm1(w, p), pq0, wm, pq)
        ryc = each(lambda a, p, v_: _mm3(a, jnp.concatenate(
            [p, jnp.concatenate([zero_blk, -v_], axis=1)], axis=0)), a_r, pq, v_h)
        gh = each(lambda i, p: _mm3(jnp.concatenate([i[6], i[7]], axis=1).T,
                                    jnp.concatenate([p, i[3]], axis=1)), inst, pq)
        pieces = []
        for i, g, ry, rt_ in zip(inst, gh, ryc, rt_h):
            g_t = eye * i[8] - g[0:dh, 0:dh]
            h_t = g[dh:, 2 * dh:] - g[0:dh, dh:2 * dh]
            pieces += [g_t, rt_ - ry[:, 0:dh], h_t, -ry[:, dh:]]
        per_chunk = len(pieces) // P1_CHUNKS
        for cc in range(P1_CHUNKS):
            coef_ref[c2 * P1_CHUNKS + cc] = jnp.concatenate(pieces[cc * per_chunk:(cc + 1) * per_chunk], axis=0)
        return carry

    lax.fori_loop(0, nc // P1_CHUNKS, phase1, 0)

    m_init = tuple((s0_ref[z, j].T if has_s0 else jnp.zeros((dh, dh), F32)) for z in range(2) for j in range(2))

    def phase2(ci, ms):
        new_ms, ys = [], []
        for z in range(2):
            c = ci if z == 0 else nc - 1 - ci
            for j in range(2):
                base = (z * 2 + j) * 4 * dh
                out = (_mm3(coef_ref[c, base:base + 2 * dh, :], ms[z * 2 + j])
                       + coef_ref[c, base + 2 * dh:base + 4 * dh, :])
                new_ms.append(out[0:dh])
                ys.append(out[dh:])
        ysp_ref[ci] = jnp.concatenate(ys, axis=1)
        return tuple(new_ms)

    m_fin = lax.fori_loop(0, nc, phase2, m_init)

    if emit_state:
        for z in range(2):
            for j in range(2):
                so_ref[z, j] = m_fin[z * 2 + j].T

    y = jnp.concatenate([ysp_ref[c, :, 0:pw] + ysp_ref[nc - 1 - c, :, pw:] for c in range(nc)], axis=0)
    mu = _mm_exact_rhs(y, seg_b) * (1.0 / dh)
    yc = y - mu
    var = _mm_exact_rhs(yc * yc, seg_b) * (1.0 / dh)
    yn = yc * lax.rsqrt(var + RW_GN_EPS) * lng_ref[...] + lnb_ref[...]
    v = v_ref[...]
    bonus = _mm_exact_rhs(r * k * rkp_ref[...], seg_b) * v
    g = _mm3(_sigmoid(sm[:, SM_GL:SM_GL + RW_G_RANK]), gup_ref[...])
    y_ref[...] = ((yn + bonus) * g).astype(y_ref.dtype)


def _rwkv(u, p, layer, n_seq, seq, row_off, s0=None, emit_state=False, prev=()):
    n_pairs = RW_HEADS // 2
    pw = 2 * RW_DH
    bpc = MIX_W // pw
    lw = lambda shape: pl.BlockSpec((None,) + shape, lambda b, hp: (layer,) + (0,) * (len(shape) - 1) + (hp,))
    row = lambda a: a.reshape(DEPTH, 1, MIX_W)
    u_pair = lambda cb: pl.BlockSpec((seq, pw), lambda b, hp: (b + row_off, cb * bpc + hp))
    in_specs = [u_pair(CB_RWR), u_pair(CB_RWK), u_pair(CB_RWV),
                pl.BlockSpec((seq, MIX_W), lambda b, hp: (b + row_off, CB_SMALL)),
                lw((2, pw)), lw((2, RW_W_RANK, pw)), lw((2, pw)), lw((2, RW_A_RANK, pw)),
                lw((RW_G_RANK, pw))] + [lw((1, pw))] * 5 + [_full((2, CHUNK, CHUNK))]
    t_idx = np.arange(CHUNK)
    incl_masks = np.stack([t_idx[None, :] <= t_idx[:, None], t_idx[None, :] >= t_idx[:, None]])
    args = [u, u, u, u, p['rw_w0'], p['rw_w_up'], p['rw_a0'], p['rw_a_up'], p['rw_g_up'],
            row(p['rw_k_k']), row(p['rw_k_a']), row(p['rw_r_k']), row(p['rw_ln_g']), row(p['rw_ln_b']),
            jnp.asarray(incl_masks, BF16)]
    st_blk = (2, 2, RW_DH, RW_DH)
    if s0 is not None:
        in_specs.append(pl.BlockSpec((None, None) + st_blk, lambda b, hp: (b, layer, 0, hp, 0, 0)))
        args.append(s0)
    out_shape = [jax.ShapeDtypeStruct((n_seq * seq, MIX_W), BF16)]
    out_specs = [pl.BlockSpec((seq, pw), lambda b, hp: (b, hp))]
    if emit_state:
        out_shape.append(jax.ShapeDtypeStruct((n_seq, DEPTH, 2, RW_HEADS, RW_DH, RW_DH), F32))
        out_specs.append(_state_spec(st_blk, layer, not prev, lambda b, hp: (b, 0, hp, 0, 0)))
    nc = seq // CHUNK
    tok = lambda n: pltpu.VMEM((n, seq, pw) if n else (seq, pw), F32)
    body, prev_specs = _carry_through(
        functools.partial(_rwkv_kernel, seq=seq, has_s0=s0 is not None, emit_state=emit_state), len(in_specs), prev,
        1, 1 if emit_state else 0, layer)
    return pl.pallas_call(
        body,
        out_shape=tuple(out_shape), grid=(n_seq, n_pairs), in_specs=in_specs + prev_specs,
        out_specs=tuple(out_specs),
        input_output_aliases={len(in_specs) + i: 1 + i for i in range(len(prev))},
        scratch_shapes=[tok(0), tok(2), tok(2), tok(2),
                        pltpu.VMEM((nc, 2, 2 * (pw + CHUNK), pw), F32), pltpu.VMEM((nc, CHUNK, 2 * pw), F32)],
        compiler_params=_params(("parallel", "parallel")),
        name=f"rwkv_{seq}",
    )(*args, *prev)


def _rwkv_kernel_old(*refs, seq, has_s0, emit_state):
    it = iter(refs)
    r_ref, k_ref, v_ref, sm_ref = (next(it) for _ in range(4))
    (w0_ref, wup_ref, a0_ref, aup_ref, gup_ref, kkp_ref, kap_ref, rkp_ref, lng_ref, lnb_ref) = (
        next(it) for _ in range(10))
    s0_ref = next(it) if has_s0 else None
    y_ref = next(it)
    so_ref = next(it) if emit_state else None
    rp_ref, vp_ref, kk_ref, lw_ref, ka_ref, kd_ref, ys_ref, st_ref = (next(it) for _ in range(8))

    n_pairs = RW_HEADS // 2
    pw = 2 * RW_DH
    nc = seq // CHUNK
    seg = _seg_ones(MIX_W, RW_DH)

    r = r_ref[...]
    k = k_ref[...]
    sm = sm_ref[...]
    wl = jnp.tanh(sm[:, SM_WL:SM_WL + RW_W_RANK])
    al = sm[:, SM_AL:SM_AL + RW_A_RANK]
    kk = k * kkp_ref[...]
    kk = kk * lax.rsqrt(_mm(kk * kk, seg, HI) + EPS)
    for hp in range(n_pairs):
        psl = slice(hp * pw, (hp + 1) * pw)
        rp_ref[hp] = r[:, psl]
        vp_ref[hp] = v_ref[:, psl]
        kk_ref[hp] = kk[:, psl]
    for z in range(2):
        w_pre = w0_ref[z:z + 1, :] + _mm(wl, wup_ref[z], HI)
        lw = -RW_DECAY_SCALE * _sigmoid(w_pre)
        a = _sigmoid(a0_ref[z:z + 1, :] + _mm(al, aup_ref[z], HI))
        ka = kk * a
        kd = k * (1.0 + (a - 1.0) * kap_ref[...])
        for hp in range(n_pairs):
            psl = slice(hp * pw, (hp + 1) * pw)
            lw_ref[z * n_pairs + hp] = lw[:, psl]
            ka_ref[z * n_pairs + hp] = ka[:, psl]
            kd_ref[z * n_pairs + hp] = kd[:, psl]
    for i in range(2 * RW_HEADS):
        st_ref[i] = s0_ref[i // RW_HEADS, i % RW_HEADS] if has_s0 else jnp.zeros((RW_DH, RW_DH), F32)

    incl, strict, ti, si = _tri_masks(CHUNK)
    diag_blk = (ti // SUB == si // SUB).astype(F32)
    eye = (ti == si).astype(F32)

    def chunk_pair(ci, hp):
        for z in range(2):
            c = ci if z == 0 else nc - 1 - ci
            rows = pl.ds(pl.multiple_of(c * CHUNK, CHUNK), CHUNK)
            lwc = lw_ref[z * n_pairs + hp, rows, :]
            cum = _mm(incl[z], lwc, HI)
            tot = cum[CHUNK - 1:CHUNK] if z == 0 else cum[0:1]
            e_neg = jnp.exp(-cum)
            dec = jnp.exp(tot - cum)
            e_tot = jnp.exp(tot)
            kac = ka_ref[z * n_pairs + hp, rows, :]
            kdc = kd_ref[z * n_pairs + hp, rows, :]
            rt = rp_ref[hp, rows, :] * jnp.exp(cum)
            kt = kk_ref[hp, rows, :] * jnp.exp(cum - lwc)
            at, kdt = kac * e_neg, kdc * e_neg
            ah, kh = kac * dec, kdc * dec
            vc = vp_ref[hp, rows, :]
            for j in range(2):
                sl = slice(j * RW_DH, (j + 1) * RW_DH)
                si_ = z * RW_HEADS + hp * 2 + j
                s0 = st_ref[si_]
                kt_h, rt_h, v_h = kt[:, sl], rt[:, sl], vc[:, sl]
                low = _mm_nt(kt_h, at[:, sl], HI) * strict[z]
                a_kk = _mm_nt(kt_h, kdt[:, sl], HI) * strict[z]
                a_ra = _mm_nt(rt_h, at[:, sl], HI) * incl[z]
                a_rk = _mm_nt(rt_h, kdt[:, sl], HI) * incl[z]
                ld = low * diag_blk
                lo = low - ld
                l2 = _mm(ld, ld, HI)
                l4 = _mm(l2, l2, HI)
                l8 = _mm(l4, l4, HI)
                tinv = eye - ld
                tinv = tinv + _mm(tinv, l2, HI)
                tinv = tinv + _mm(tinv, l4, HI)
                tinv = tinv + _mm(tinv, l8, HI)
                rhs = -(_mm_nt(kt_h, s0, HI) + _mm(a_kk, v_h, HI))
                cv = _mm(tinv, rhs, HI)
                wm = _mm(tinv, lo, HI)
                uu = cv
                for _ in range(CHUNK // SUB - 1):
                    uu = cv - _mm(wm, uu, HI)
                y = _mm_nt(rt_h, s0, HI) + _mm(a_ra, uu, HI) + _mm(a_rk, v_h, HI)
                ys_ref[z * n_pairs + hp, rows, sl] = y
                st_ref[si_] = s0 * e_tot[:, sl] + _mm_tn(uu, ah[:, sl], HI) + _mm_tn(v_h, kh[:, sl], HI)

    def chunk_step(ci, carry):
        def pair_step(hp, carry2):
            chunk_pair(ci, hp)
            return carry2
        return lax.fori_loop(0, n_pairs, pair_step, carry)

    lax.fori_loop(0, nc, chunk_step, 0)

    if emit_state:
        for i in range(2 * RW_HEADS):
            so_ref[i // RW_HEADS, i % RW_HEADS] = st_ref[i]

    g = _mm(_sigmoid(sm[:, SM_GL:SM_GL + RW_G_RANK]), gup_ref[...], HI)
    for h in range(RW_HEADS):
        hp, j = divmod(h, 2)
        sl = slice(j * RW_DH, (j + 1) * RW_DH)
        hsl = slice(h * RW_DH, (h + 1) * RW_DH)
        y = ys_ref[hp, :, sl] + ys_ref[n_pairs + hp, :, sl]
        mu = jnp.mean(y, axis=-1, keepdims=True)
        var = jnp.mean(jnp.square(y - mu), axis=-1, keepdims=True)
        yn = (y - mu) * lax.rsqrt(var + RW_GN_EPS) * lng_ref[:, hsl] + lnb_ref[:, hsl]
        r_h, k_h, v_h = rp_ref[hp, :, sl], k_ref[:, hsl], vp_ref[hp, :, sl]
        bonus = jnp.sum(r_h * k_h * rkp_ref[:, hsl], axis=-1, keepdims=True) * v_h
        y_ref[:, hsl] = ((yn + bonus) * g[:, hsl]).astype(y_ref.dtype)


def _rwkv_old(u, p, layer, n_seq, seq, row_off, s0=None, emit_state=False):
    lw = lambda shape: pl.BlockSpec((None,) + shape, lambda b: (layer,) + (0,) * len(shape))
    row = lambda a: a.reshape(DEPTH, 1, MIX_W)
    in_specs = [_u_spec(seq, cb, row_off) for cb in (CB_RWR, CB_RWK, CB_RWV, CB_SMALL)] + [
        lw((2, MIX_W)), lw((2, RW_W_RANK, MIX_W)), lw((2, MIX_W)), lw((2, RW_A_RANK, MIX_W)),
        lw((RW_G_RANK, MIX_W))] + [lw((1, MIX_W))] * 5
    args = [u, u, u, u, p['rw_w0'], p['rw_w_up'], p['rw_a0'], p['rw_a_up'], p['rw_g_up'],
            row(p['rw_k_k']), row(p['rw_k_a']), row(p['rw_r_k']), row(p['rw_ln_g']), row(p['rw_ln_b'])]
    st_shape = (2, RW_HEADS, RW_DH, RW_DH)
    if s0 is not None:
        in_specs.append(pl.BlockSpec((None, None) + st_shape, lambda b: (b, layer, 0, 0, 0, 0)))
        args.append(s0)
    out_shape = [jax.ShapeDtypeStruct((n_seq * seq, MIX_W), BF16)]
    out_specs = [pl.BlockSpec((seq, MIX_W), lambda b: (b, 0))]
    if emit_state:
        out_shape.append(jax.ShapeDtypeStruct((n_seq,) + st_shape, F32))
        out_specs.append(pl.BlockSpec((None,) + st_shape, lambda b: (b, 0, 0, 0, 0)))
    n_pairs = RW_HEADS // 2
    pair = lambda n: pltpu.VMEM((n, seq, 2 * RW_DH), F32)
    return pl.pallas_call(
        functools.partial(_rwkv_kernel, seq=seq, has_s0=s0 is not None, emit_state=emit_state),
        out_shape=tuple(out_shape), grid=(n_seq,), in_specs=in_specs, out_specs=tuple(out_specs),
        scratch_shapes=[pair(n_pairs), pair(n_pairs), pair(n_pairs), pair(2 * n_pairs), pair(2 * n_pairs),
                        pair(2 * n_pairs), pair(2 * n_pairs), pltpu.VMEM((2 * RW_HEADS, RW_DH, RW_DH), F32)],
        compiler_params=_params(("parallel",)),
        name=f"rwkv_{seq}",
    )(*args)


def _log_sigmoid(x):
    return jnp.minimum(x, 0.0) - jnp.log(1.0 + jnp.exp(-jnp.abs(x)))


def _mlstm_kernel(*refs, seq, has_s0, emit_state):
    it = iter(refs)
    q_ref, k_ref, v_ref, o_ref, sm_ref, gr_ref, bc_ref, br_ref, ng_ref = (next(it) for _ in range(9))
    c0_ref, n0_ref, m0_ref = ((next(it), next(it), next(it)) if has_s0 else (None, None, None))
    y_ref = next(it)
    co_ref, no_ref, mo_ref = ((next(it), next(it), next(it)) if emit_state else (None, None, None))
    hs_ref, c_ref, n_ref, m_ref = (next(it) for _ in range(4))

    nc = seq // CHUNK
    n_st = 2 * ML_HEADS
    for i in range(n_st):
        z, h = divmod(i, ML_HEADS)
        c_ref[i] = c0_ref[z, h] if has_s0 else jnp.zeros((ML_DH, ML_DH), F32)
        n_ref[i] = n0_ref[z, h] if has_s0 else jnp.zeros((1, ML_DH), F32)
        m_ref[i] = m0_ref[z, h] if has_s0 else jnp.zeros((1, 1), F32)

    incl, _, ti, si = _tri_masks(CHUNK)
    before = ((si <= ti), (si >= ti))

    nh = ML_HEADS
    chains = [(z, h) for z in range(2) for h in range(nh)]
    each = lambda f, *xs: [f(*a) for a in zip(*xs)]
    zs = [z for z, _ in chains]

    def chunk_step(ci, carry):
        per_dir = []
        for z in range(2):
            c = ci if z == 0 else nc - 1 - ci
            rows = pl.ds(pl.multiple_of(c * CHUNK, CHUNK), CHUNK)
            g0 = SM_GATE + z * 2 * nh
            gc = sm_ref[rows, g0:g0 + 2 * nh] + bc_ref[:, z * 2 * nh:(z + 1) * 2 * nh]
            gr = gr_ref[c, z * 2 * nh:(z + 1) * 2 * nh, :] + br_ref[z * 2 * nh:(z + 1) * 2 * nh, :]
            b_cols = _mm(incl[z], _log_sigmoid(gc[:, nh:]), HI)
            b_rows = _mm(_log_sigmoid(gr[nh:]), incl[1 - z], HI)
            per_dir.append((rows, gc[:, :nh], b_cols, gr[:nh], b_rows))
        rows = [per_dir[z][0] for z, _ in chains]
        hsl = [slice(h * ML_DH, (h + 1) * ML_DH) for _, h in chains]
        i_col = [per_dir[z][1][:, h:h + 1] for z, h in chains]
        b_col = [per_dir[z][2][:, h:h + 1] for z, h in chains]
        i_row = [per_dir[z][3][h:h + 1] for z, h in chains]
        b_row = [per_dir[z][4][h:h + 1] for z, h in chains]
        b_last = each(lambda b, z: b[CHUNK - 1:CHUNK] if z == 0 else b[0:1], b_col, zs)
        m_old = [m_ref[i] for i in range(n_st)]
        cm = [c_ref[i] for i in range(n_st)]
        nv = [n_ref[i] for i in range(n_st)]
        qc = each(lambda r, s: q_ref[r, s] * (ML_DH ** -0.5), rows, hsl)
        kc = each(lambda r, s: k_ref[r, s], rows, hsl)
        vc = each(lambda r, s: v_ref[r, s], rows, hsl)
        qk = each(lambda q, k: _mm_nt(q.astype(BF16), k.astype(BF16)), qc, kc)
        qcm = each(lambda q, c_: _mm_nt(q.astype(BF16), c_.astype(BF16)), qc, cm)
        a_t = each(lambda b, m: b + m, b_col, m_old)
        dmat = each(lambda bc, brw, ir, z: jnp.where(before[z], bc - brw + ir, NEG_INF), b_col, b_row, i_row, zs)
        m_t = each(lambda a, d: jnp.maximum(a, jnp.max(d, axis=-1, keepdims=True)), a_t, dmat)
        s = each(lambda x, d, m: x * jnp.exp(d - m), qk, dmat, m_t)
        inter = each(lambda a, m: jnp.exp(a - m), a_t, m_t)
        sv = each(_mm1, s, vc)
        g_col = each(lambda bl, bc, ic: bl - bc + ic, b_last, b_col, i_col)
        a_l = each(lambda bl, m: bl + m, b_last, m_old)
        m_new = each(lambda a, g: jnp.maximum(a, jnp.max(g, axis=0, keepdims=True)), a_l, g_col)
        wgt = each(lambda g, m: jnp.exp(g - m), g_col, m_new)
        vk = each(lambda v, w, k: _mm1((v * w).T, k), vc, wgt, kc)
        decay = each(lambda a, m: jnp.exp(a - m), a_l, m_new)
        num = each(lambda i_, x, y: i_ * x + y, inter, qcm, sv)
        den = each(lambda i_, q, n_, s_: i_ * jnp.sum(q * n_, axis=-1, keepdims=True)
                   + jnp.sum(s_, axis=-1, keepdims=True), inter, qc, nv, s)
        hh = each(lambda n_, d, m: n_ / jnp.maximum(jnp.abs(d), jnp.exp(-m)), num, den, m_t)
        for z in range(2):
            hs_ref[z, per_dir[z][0], :] = jnp.concatenate(hh[z * nh:(z + 1) * nh], axis=1)
        for i in range(n_st):
            c_ref[i] = decay[i] * cm[i] + vk[i]
            n_ref[i] = decay[i] * nv[i] + jnp.sum(wgt[i] * kc[i], axis=0, keepdims=True)
            m_ref[i] = m_new[i]
        return carry

    lax.fori_loop(0, nc, chunk_step, 0)

    if emit_state:
        for i in range(n_st):
            z, h = divmod(i, ML_HEADS)
            co_ref[z, h] = c_ref[i]
            no_ref[z, h] = n_ref[i]
            mo_ref[z, h] = m_ref[i]

    for h in range(ML_HEADS):
        hsl = slice(h * ML_DH, (h + 1) * ML_DH)
        hn = _head_rms(hs_ref[0, :, hsl] + hs_ref[1, :, hsl], ng_ref[:, hsl])
        y_ref[:, hsl] = (_sigmoid(o_ref[:, hsl]) * hn).astype(y_ref.dtype)


def _mlstm(u, gate_rows, p, layer, n_seq, seq, row_off, state=None, emit_state=False, prev=()):
    lw = lambda shape: pl.BlockSpec((None,) + shape, lambda b: (layer,) + (0,) * len(shape))
    nc = seq // CHUNK
    n_gate = 4 * ML_HEADS
    in_specs = [_u_spec(seq, cb, row_off) for cb in (CB_MLQ, CB_MLK, CB_MLV, CB_MLO, CB_GATE)] + [
        pl.BlockSpec((None, nc, n_gate, CHUNK), lambda b: (b, 0, 0, 0)),
        lw((1, n_gate)), lw((n_gate, 1)), lw((1, MIX_W))]
    args = [u, u, u, u, u, gate_rows, p['ml_gate_b'].reshape(DEPTH, 1, n_gate),
            p['ml_gate_b'].reshape(DEPTH, n_gate, 1), p['ml_norm_g'].reshape(DEPTH, 1, MIX_W)]
    c_shape, n_shape, m_shape = (2, ML_HEADS, ML_DH, ML_DH), (2, ML_HEADS, 1, ML_DH), (2, ML_HEADS, 1, 1)
    if state is not None:
        c0, n0, m0 = state
        for a, shp in ((c0, c_shape), (n0, n_shape), (m0, m_shape)):
            in_specs.append(pl.BlockSpec((None, None) + shp, lambda b: (b, layer, 0, 0, 0, 0)))
            args.append(a.reshape(a.shape[:2] + shp))
    out_shape = [jax.ShapeDtypeStruct((n_seq * seq, MIX_W), BF16)]
    out_specs = [pl.BlockSpec((seq, MIX_W), lambda b: (b, 0))]
    if emit_state:
        for shp in (c_shape, n_shape, m_shape):
            out_shape.append(jax.ShapeDtypeStruct((n_seq, DEPTH) + shp, F32))
            out_specs.append(_state_spec(shp, layer, not prev, lambda b: (b, 0, 0, 0, 0)))
    n_st = 2 * ML_HEADS
    body, prev_specs = _carry_through(
        functools.partial(_mlstm_kernel, seq=seq, has_s0=state is not None, emit_state=emit_state),
        len(in_specs), prev, 1, 3 if emit_state else 0, layer)
    return pl.pallas_call(
        body,
        out_shape=tuple(out_shape), grid=(n_seq,), in_specs=in_specs + prev_specs, out_specs=tuple(out_specs),
        input_output_aliases={len(in_specs) + i: 1 + i for i in range(len(prev))},
        scratch_shapes=[pltpu.VMEM((2, seq, MIX_W), F32), pltpu.VMEM((n_st, ML_DH, ML_DH), F32),
                        pltpu.VMEM((n_st, 1, ML_DH), F32), pltpu.VMEM((n_st, 1, 1), F32)],
        compiler_params=_params(("parallel",)),
        name=f"mlstm_{seq}",
    )(*args, *prev)


def _outproj_kernel(ya_ref, yb_ref, yc_ref, yd_ref, w_ref, x_ref, g_ref, o_ref):
    acc = _mm(ya_ref[...], w_ref[0:MIX_W, :].astype(BF16))
    for i, y_ref in enumerate((yb_ref, yc_ref, yd_ref), start=1):
        acc += _mm(y_ref[...], w_ref[i * MIX_W:(i + 1) * MIX_W, :].astype(BF16))
    o_ref[...] = x_ref[...] + g_ref[...] * acc


def _outproj(ys, w_out_b, x, mod, layer):
    tm, tn = 1024, 1024
    y_spec = pl.BlockSpec((tm, MIX_W), lambda i, j: (i, 0))
    return pl.pallas_call(
        _outproj_kernel,
        out_shape=jax.ShapeDtypeStruct((N_TOK, D_MODEL), F32),
        grid=(N_TOK // tm, D_MODEL // tn),
        in_specs=[y_spec] * 4 + [pl.BlockSpec((None, D_MODEL, tn), lambda i, j: (layer, 0, j)),
                                 pl.BlockSpec((tm, tn), lambda i, j: (i, j)),
                                 _mod_spec(layer, 2, tm, tn, with_j=True)],
        out_specs=pl.BlockSpec((tm, tn), lambda i, j: (i, j)),
        compiler_params=_params(("parallel", "parallel")),
        name="outproj",
    )(*ys, w_out_b, x, mod)


def _router_kernel(x_ref, g_ref, sh_ref, sc_ref, rw_ref, rb_ref, xm_ref, gt_ref, ids_ref):
    xm = _modulated(x_ref[...], g_ref[...], sh_ref[...], sc_ref[...])
    xm_ref[...] = xm
    logits = _mm_nt(rw_ref[...], xm, HI)
    ex = jnp.exp(logits - jnp.max(logits, axis=0, keepdims=True))
    scores = ex / jnp.sum(ex, axis=0, keepdims=True)
    sel = scores + rb_ref[...]
    per = N_EXPERTS // N_EXPERT_GROUPS
    s = [sel[e:e + 1, :] for e in range(N_EXPERTS)]
    grp_score = []
    for g in range(N_EXPERT_GROUPS):
        a, b, c, d = s[per * g:per * (g + 1)]
        hi1, lo1, hi2, lo2 = jnp.maximum(a, b), jnp.minimum(a, b), jnp.maximum(c, d), jnp.minimum(c, d)
        grp_score.append(jnp.maximum(hi1, hi2) + jnp.maximum(jnp.minimum(hi1, hi2), jnp.maximum(lo1, lo2)))
    best = functools.reduce(jnp.maximum, grp_score)
    in_grp, taken = [], jnp.zeros_like(best)
    for g in range(N_EXPERT_GROUPS):
        hit = jnp.where(grp_score[g] == best, 1.0, 0.0) * (1.0 - taken)
        in_grp.append(hit)
        taken = taken + hit
    picked, flag = [], []
    for e in range(N_EXPERTS):
        g = e // per
        rank = jnp.zeros_like(best)
        for o in range(per * g, per * (g + 1)):
            if o < e:
                rank += jnp.where(s[o] >= s[e], 1.0, 0.0)
            elif o > e:
                rank += jnp.where(s[o] > s[e], 1.0, 0.0)
        flag.append(in_grp[g] * jnp.where(rank < 2.0, 1.0, 0.0))
        picked.append(flag[e] * scores[e:e + 1, :])
    total = functools.reduce(lambda x, y: x + y, picked)
    for e in range(N_EXPERTS):
        gt_ref[e:e + 1, :] = picked[e] / total
    lo_id = functools.reduce(jnp.minimum, [jnp.where(flag[e] > 0.0, float(e), float(N_EXPERTS)) for e in range(N_EXPERTS)])
    hi_id = functools.reduce(jnp.maximum, [jnp.where(flag[e] > 0.0, float(e), -1.0) for e in range(N_EXPERTS)])
    ids_ref[0:1, :] = lo_id
    ids_ref[1:2, :] = hi_id
    ids_ref[2:, :] = jnp.zeros((ids_ref.shape[0] - 2,) + lo_id.shape[1:], F32)


def _router(x, norm_g, mod, router_wt, router_b, layer):
    tm = 512
    return pl.pallas_call(
        _router_kernel,
        out_shape=(jax.ShapeDtypeStruct((N_TOK, D_MODEL), F32), jax.ShapeDtypeStruct((N_EXPERTS, N_TOK), F32),
                   jax.ShapeDtypeStruct((8, N_TOK), F32)),
        grid=(N_TOK // tm,),
        in_specs=[pl.BlockSpec((tm, D_MODEL), lambda i: (i, 0)),
                  pl.BlockSpec((None, 1, D_MODEL), lambda i: (layer, 0, 0)),
                  _mod_spec(layer, 3, tm), _mod_spec(layer, 4, tm),
                  _full((N_EXPERTS, D_MODEL)), _full((N_EXPERTS, 1))],
        out_specs=(pl.BlockSpec((tm, D_MODEL), lambda i: (i, 0)), pl.BlockSpec((N_EXPERTS, tm), lambda i: (0, i)),
                   pl.BlockSpec((8, tm), lambda i: (0, i))),
        compiler_params=_params(("parallel",)),
        name="router",
    )(x, norm_g.reshape(DEPTH, 1, D_MODEL), mod, mod, router_wt, router_b.reshape(N_EXPERTS, 1))


def _experts_kernel(xm_ref, gates_ref, w1_ref, w3_ref, w2_ref, x_ref, g_ref, o_ref, acc_ref):
    e = pl.program_id(1)

    @pl.when(e == 0)
    def _():
        acc_ref[...] = jnp.zeros_like(acc_ref)

    xm = xm_ref[...]
    h1 = _mm(xm, w1_ref[...])
    h3 = _mm(xm, w3_ref[...])
    gates = gates_ref[...]
    lane = lax.broadcasted_iota(jnp.int32, gates.shape, 1)
    gate = jnp.sum(jnp.where(lane == e, gates, 0.0), axis=-1, keepdims=True)
    hh = h1 * _sigmoid(h1) * h3 * gate
    acc_ref[...] += _mm(hh.astype(BF16), w2_ref[...])

    @pl.when(e == N_EXPERTS - 1)
    def _():
        o_ref[...] = x_ref[...] + g_ref[...] * acc_ref[...]


def _experts(xm, gates, w1_b, w3_b, w2_b, x, mod, layer):
    tm = 512
    row = pl.BlockSpec((tm, D_MODEL), lambda i, e: (i, 0))
    return pl.pallas_call(
        _experts_kernel,
        out_shape=jax.ShapeDtypeStruct((N_TOK, D_MODEL), F32),
        grid=(N_TOK // tm, N_EXPERTS),
        in_specs=[row, pl.BlockSpec((tm, N_EXPERTS), lambda i, e: (i, 0)),
                  pl.BlockSpec((None, None, D_MODEL, D_EXPERT), lambda i, e: (layer, e, 0, 0)),
                  pl.BlockSpec((None, None, D_MODEL, D_EXPERT), lambda i, e: (layer, e, 0, 0)),
                  pl.BlockSpec((None, None, D_EXPERT, D_MODEL), lambda i, e: (layer, e, 0, 0)),
                  row, _mod_spec(layer, 5, tm)],
        out_specs=row,
        scratch_shapes=[pltpu.VMEM((tm, D_MODEL), F32)],
        compiler_params=_params(("parallel", "arbitrary")),
        name="experts",
    )(xm, gates, w1_b, w3_b, w2_b, x, mod)


MOE_TM = 512
N_PAIRS = N_EXPERT_GROUPS * 6
MOE_STEPS = 2 * (N_PAIRS + N_TOK // MOE_TM - 1)


def _gather_rows(src_hbm, dst_ref, idx_ref, base, n, sem):
    def issue(r, carry):
        pltpu.make_async_copy(src_hbm.at[pl.ds(idx_ref[base + r], 1), :], dst_ref.at[pl.ds(r, 1), :], sem).start()
        return carry

    def wait(r, carry):
        pltpu.make_async_copy(src_hbm.at[pl.ds(0, 1), :], dst_ref.at[pl.ds(r, 1), :], sem).wait()
        return carry

    lax.fori_loop(0, n, issue, 0, unroll=8)
    lax.fori_loop(0, n, wait, 0, unroll=8)


def _moe_plan(ids):
    i32 = jnp.int32
    lo, hi = ids[0].astype(i32), ids[1].astype(i32)
    src = jnp.argsort(lo * N_EXPERTS + hi).astype(i32)
    pos = jnp.argsort(src).astype(i32)
    n_tiles = N_TOK // MOE_TM
    ex = jnp.arange(N_EXPERTS, dtype=i32)
    lo_s, hi_s = lo[src].reshape(n_tiles, MOE_TM, 1), hi[src].reshape(n_tiles, MOE_TM, 1)
    used = ((lo_s == ex).any(axis=1) | (hi_s == ex).any(axis=1)).reshape(-1)
    n_valid = jnp.sum(used).astype(i32)
    idx = jnp.nonzero(used, size=MOE_STEPS, fill_value=0)[0].astype(i32)
    valid = jnp.arange(MOE_STEPS, dtype=i32) < n_valid
    idx = jnp.where(valid, idx, idx[jnp.maximum(n_valid - 1, 0)])
    tile, exp = idx // N_EXPERTS, idx % N_EXPERTS
    first = valid & (tile != jnp.concatenate([jnp.full((1,), -1, i32), tile[:-1]]))
    return src, pos, tile, exp, first.astype(i32), valid.astype(i32)


def _moe_kernel(tile_ref, exp_ref, first_ref, valid_ref, src_ref, xm_hbm, gates_ref, w1_ref, w3_ref, w2_ref,
                o_ref, xs_ref, xb_ref, sem):
    s = pl.program_id(0)

    @pl.when(first_ref[s] == 1)
    def _():
        _gather_rows(xm_hbm, xs_ref, src_ref, tile_ref[s] * MOE_TM, MOE_TM, sem)
        xb_ref[...] = xs_ref[...].astype(BF16)
        o_ref[...] = jnp.zeros_like(o_ref)

    @pl.when(valid_ref[s] == 1)
    def _():
        xb = xb_ref[...]
        h1 = _mm(xb, w1_ref[...].astype(BF16))
        h3 = _mm(xb, w3_ref[...].astype(BF16))
        gates = gates_ref[...]
        lane = lax.broadcasted_iota(jnp.int32, gates.shape, 1)
        gate = jnp.sum(jnp.where(lane == exp_ref[s], gates, 0.0), axis=-1, keepdims=True)
        hh = h1 * _sigmoid(h1) * h3 * gate
        o_ref[...] += _mm(hh.astype(BF16), w2_ref[...].astype(BF16))


def _moe(xm, gates_sorted, plan, w1_b, w3_b, w2_b, layer):
    src, _, tile, exp, first, valid = plan
    w_in = pl.BlockSpec((None, None, D_MODEL, D_EXPERT), lambda s, t, e, f, v, i: (layer, e[s], 0, 0))
    grid_spec = pltpu.PrefetchScalarGridSpec(
        num_scalar_prefetch=5, grid=(MOE_STEPS,),
        in_specs=[pl.BlockSpec(memory_space=pl.ANY),
                  pl.BlockSpec((MOE_TM, N_EXPERTS), lambda s, t, e, f, v, i: (t[s], 0)),
                  w_in, w_in,
                  pl.BlockSpec((None, None, D_EXPERT, D_MODEL), lambda s, t, e, f, v, i: (layer, e[s], 0, 0))],
        out_specs=pl.BlockSpec((MOE_TM, D_MODEL), lambda s, t, e, f, v, i: (t[s], 0)),
        scratch_shapes=[pltpu.VMEM((MOE_TM, D_MODEL), F32), pltpu.VMEM((MOE_TM, D_MODEL), BF16),
                        pltpu.SemaphoreType.DMA(())])
    return pl.pallas_call(
        _moe_kernel, out_shape=jax.ShapeDtypeStruct((N_TOK, D_MODEL), F32), grid_spec=grid_spec,
        compiler_params=_params(("arbitrary",)), name="moe",
    )(tile, exp, first, valid, src, xm, gates_sorted, w1_b, w3_b, w2_b)


def _combine_kernel(pos_ref, acc_hbm, x_ref, g_ref, o_ref, buf_ref, sem):
    _gather_rows(acc_hbm, buf_ref, pos_ref, pl.program_id(0) * MOE_TM, MOE_TM, sem)
    o_ref[...] = x_ref[...] + g_ref[...] * buf_ref[...]


def _combine(acc_sorted, pos, x, mod, layer):
    tm = MOE_TM
    row = pl.BlockSpec((tm, D_MODEL), lambda i, p: (i, 0))
    grid_spec = pltpu.PrefetchScalarGridSpec(
        num_scalar_prefetch=1, grid=(N_TOK // tm,),
        in_specs=[pl.BlockSpec(memory_space=pl.ANY), row,
                  pl.BlockSpec((None, None, None, 1, D_MODEL), lambda i, p: (layer, _mod_row(i, tm), 5, 0, 0))],
        out_specs=row,
        scratch_shapes=[pltpu.VMEM((tm, D_MODEL), F32), pltpu.SemaphoreType.DMA(())])
    return pl.pallas_call(
        _combine_kernel, out_shape=jax.ShapeDtypeStruct((N_TOK, D_MODEL), F32), grid_spec=grid_spec,
        compiler_params=_params(("arbitrary",)), name="combine",
    )(pos, acc_sorted, x, mod)


def _gate_rows(gcols, n_seq, seq):
    return gcols.reshape(n_seq, seq // CHUNK, CHUNK, gcols.shape[-1]).transpose(0, 1, 3, 2)


def kernel(x_prompt, x_sample, cache_na_k, cache_na_v, state_rwkv, state_mlstm_c, state_mlstm_n, state_mlstm_m,
           c, c_ctx, norm1_g, norm2_g, w_mod, b_mod, w_in, conv_w, na_q_g, na_k_g, na_rpb, rw_w0, rw_w_up, rw_a0,
           rw_a_up, rw_g_up, rw_k_k, rw_k_a, rw_r_k, rw_ln_g, rw_ln_b, ml_gate_b, ml_norm_g, w_out, router_w,
           router_b, moe_w1, moe_w3, moe_w2):
    p = dict(rw_w0=rw_w0, rw_w_up=rw_w_up, rw_a0=rw_a0, rw_a_up=rw_a_up, rw_g_up=rw_g_up, rw_k_k=rw_k_k,
             rw_k_a=rw_k_a, rw_r_k=rw_r_k, rw_ln_g=rw_ln_g, rw_ln_b=rw_ln_b, ml_gate_b=ml_gate_b,
             ml_norm_g=ml_norm_g)
    cvecs = jnp.concatenate([c_ctx[None], c, jnp.zeros((MOD_ROWS - 1 - DEC_BATCH, D_MODEL), F32)], axis=0)
    mod = _adaln(cvecs, w_mod, b_mod).reshape(DEPTH, MOD_ROWS, 6, 1, D_MODEL)

    assert w_in.shape[-1] == P_IN
    w_in_b, w_out_b, w1_b, w3_b, w2_b = jnp.swapaxes(w_in, 1, 2), w_out, moe_w1, moe_w3, moe_w2
    tw = _na_tables(na_rpb)
    router_wt = router_w.T
    sample_row_off = N_PROMPT // DEC_SEQ

    x = jnp.concatenate([x_prompt.reshape(N_PROMPT, D_MODEL), x_sample.reshape(N_SAMPLE, D_MODEL)], axis=0)
    kv_prev, rw_prev, ml_prev = (), (), ()
    for l in range(DEPTH):
        u = _inproj(x, norm1_g, mod, w_in_b, l)
        ya_p, yb_p, *kv_prev = _attn_prompt(u, conv_w, na_q_g, na_k_g, l, prev=tuple(kv_prev))
        ya_s, yb_s = _na_sample(u, cache_na_k, cache_na_v, tw, conv_w, na_q_g, na_k_g, l)
        yc_p, *rw_prev = _rwkv(u, p, l, BATCH, SEQ, 0, emit_state=True, prev=tuple(rw_prev))
        (yc_s,) = _rwkv(u, p, l, DEC_BATCH, DEC_SEQ, sample_row_off, s0=state_rwkv)
        g0 = CB_GATE * MIX_W + SM_GATE
        gcols = u[:, g0:g0 + 4 * ML_HEADS]
        yd_p, *ml_prev = _mlstm(u, _gate_rows(gcols[:N_PROMPT], BATCH, SEQ), p, l, BATCH, SEQ, 0,
                                emit_state=True, prev=tuple(ml_prev))
        (yd_s,) = _mlstm(u, _gate_rows(gcols[N_PROMPT:], DEC_BATCH, DEC_SEQ), p, l, DEC_BATCH, DEC_SEQ,
                         sample_row_off, state=(state_mlstm_c, state_mlstm_n, state_mlstm_m))
        ys = [jnp.concatenate(pair, axis=0) for pair in ((ya_p, ya_s), (yb_p, yb_s), (yc_p, yc_s), (yd_p, yd_s))]
        x = _outproj(ys, w_out_b, x, mod, l)
        xm, gates_t, ids = _router(x, norm2_g, mod, router_wt, router_b, l)
        plan = _moe_plan(ids)
        acc_sorted = _moe(xm, gates_t.T[plan[0]], plan, w1_b, w3_b, w2_b, l)
        x = _combine(acc_sorted, plan[1], x, mod, l)
    new_c, new_n, new_m = ml_prev
    return (x[:N_PROMPT].reshape(BATCH, SEQ, D_MODEL), x[N_PROMPT:].reshape(DEC_BATCH, DEC_SEQ, D_MODEL),
            kv_prev[0], kv_prev[1], rw_prev[0], new_c, new_n.reshape(BATCH, DEPTH, 2, ML_HEADS, ML_DH),
            new_m.reshape(BATCH, DEPTH, 2, ML_HEADS))
```

```python
import functools

import numpy as np
import jax
import jax.numpy as jnp
from jax import lax
from jax.experimental import pallas as pl
from jax.experimental.pallas import tpu as pltpu

F32 = jnp.float32
BF16 = jnp.bfloat16
HI = lax.Precision.HIGHEST

D_MODEL = 2048
BATCH = 16
SEQ = 256
DEPTH = 2
DEC_BATCH = 2
DEC_SEQ = 1024
PAST_LEN = 512
GRID_W = 64
MIX_W = D_MODEL // 4
CONV_K = 3
NA_DH = 64
NA_HEADS = MIX_W // NA_DH
NA_WIN_R = 8
NA_WIN_C = 16
NA_SCALE = NA_DH ** -0.5
ROPE_THETA = 10000.0
RW_DH = 64
RW_HEADS = MIX_W // RW_DH
RW_W_RANK = 64
RW_A_RANK = 64
RW_G_RANK = 128
RW_DECAY_SCALE = 0.606531
RW_GN_EPS = 64e-5
ML_DH = 128
ML_HEADS = MIX_W // ML_DH
N_EXPERTS = 16
N_EXPERT_GROUPS = 4
D_EXPERT = 512
EPS = 1e-6
NEG_INF = -1e30

N_PROMPT = BATCH * SEQ
N_SAMPLE = DEC_BATCH * DEC_SEQ
N_TOK = N_PROMPT + N_SAMPLE
MOD_ROWS = 8
CHUNK = 64
SUB = 16
P1_CHUNKS = 4
NA_ROWS_PER_TRIP = 4
P_IN = 13 * MIX_W + RW_W_RANK + RW_A_RANK + RW_G_RANK + 4 * ML_HEADS
P_BLOCKS = 15
P_PAD = P_BLOCKS * MIX_W
(CB_CVB, CB_CVC, CB_CVH, CB_NAQ, CB_NAK, CB_NAV, CB_RWR, CB_RWK, CB_RWV,
 CB_MLQ, CB_MLK, CB_MLV, CB_MLO, CB_SMALL, CB_GATE) = range(P_BLOCKS)
SM_WL, SM_AL, SM_GL = 0, 64, 128
SM_GATE = MIX_W - 4 * ML_HEADS
VMEM_LIMIT = 56 * 1024 * 1024


def _mm(a, b, prec=None):
    return jnp.dot(a, b, precision=prec, preferred_element_type=F32)


def _mm_nt(a, b, prec=None):
    return lax.dot_general(a, b, (((1,), (1,)), ((), ())), precision=prec, preferred_element_type=F32)


def _sigmoid(x):
    return 1.0 / (1.0 + jnp.exp(-x))


def _full(shape):
    n = len(shape)
    return pl.BlockSpec(shape, lambda *_: (0,) * n)


def _params(sem):
    return pltpu.CompilerParams(dimension_semantics=sem, vmem_limit_bytes=VMEM_LIMIT)


def _mod_row(i, tm):
    n_prompt_tiles = N_PROMPT // tm
    tiles_per_sample = DEC_SEQ // tm
    return jnp.where(i < n_prompt_tiles, 0, 1 + (i - n_prompt_tiles) // tiles_per_sample)


def _mod_spec(layer, chunk, tm, tn=D_MODEL, with_j=False):
    if with_j:
        return pl.BlockSpec((None, None, None, 1, tn), lambda i, j: (layer, _mod_row(i, tm), chunk, 0, j))
    return pl.BlockSpec((None, None, None, 1, tn), lambda i, *_: (layer, _mod_row(i, tm), chunk, 0, 0))


def _tri_masks(n):
    t = lax.broadcasted_iota(jnp.int32, (n, n), 0)
    s = lax.broadcasted_iota(jnp.int32, (n, n), 1)
    incl = ((s <= t).astype(F32), (s >= t).astype(F32))
    strict = ((s < t).astype(F32), (s > t).astype(F32))
    return incl, strict, t, s


def _split2(x):
    hi = x.astype(BF16)
    return hi, (x - hi.astype(F32)).astype(BF16)


def _mm1(a, b):
    return _mm(a.astype(BF16), b.astype(BF16))


def _mm3(a, b):
    a_hi, a_lo = _split2(a)
    b_hi, b_lo = _split2(b)
    return _mm(a_hi, b_lo) + _mm(a_lo, b_hi) + _mm(a_hi, b_hi)


def _split3(x):
    x0 = x.astype(BF16)
    r1 = x - x0.astype(F32)
    x1 = r1.astype(BF16)
    return x0, x1, (r1 - x1.astype(F32)).astype(BF16)


def _mm_exact_lhs(mask_b, x):
    x0, x1, x2 = _split3(x)
    return _mm(mask_b, x2) + _mm(mask_b, x1) + _mm(mask_b, x0)


def _mm_exact_rhs(x, mask_b):
    x0, x1, x2 = _split3(x)
    return _mm(x2, mask_b) + _mm(x1, mask_b) + _mm(x0, mask_b)


def _adaln_kernel(cv_ref, w_ref, b_ref, o_ref):
    cv = cv_ref[...]
    hi, lo = _split2(cv * _sigmoid(cv))
    w = w_ref[...].astype(BF16)
    o_ref[...] = _mm(lo, w) + _mm(hi, w) + b_ref[...]


def _adaln(cvecs, w_mod, b_mod):
    tn = 1024
    n_out = 6 * D_MODEL
    return pl.pallas_call(
        _adaln_kernel,
        out_shape=jax.ShapeDtypeStruct((DEPTH, MOD_ROWS, n_out), F32),
        grid=(DEPTH, n_out // tn),
        in_specs=[_full((MOD_ROWS, D_MODEL)),
                  pl.BlockSpec((None, D_MODEL, tn), lambda l, j: (l, 0, j)),
                  pl.BlockSpec((None, 1, tn), lambda l, j: (l, 0, j))],
        out_specs=pl.BlockSpec((None, MOD_ROWS, tn), lambda l, j: (l, 0, j)),
        compiler_params=_params(("parallel", "parallel")),
        name="adaln",
    )(cvecs, w_mod, b_mod.reshape(DEPTH, 1, n_out))


def _modulated(x, g, sh, sc):
    y = x * lax.rsqrt(jnp.mean(x * x, axis=-1, keepdims=True) + EPS) * g
    return y * (1.0 + sc) + sh


def _inproj_kernel(x_ref, g_ref, sh_ref, sc_ref, w_ref, o_ref, xm_ref):
    @pl.when(pl.program_id(1) == 0)
    def _():
        xm_ref[...] = _modulated(x_ref[...], g_ref[...], sh_ref[...], sc_ref[...]).astype(BF16)

    o_ref[...] = _mm_nt(xm_ref[...], w_ref[0].astype(BF16))


def _inproj_src_row(j):
    a = 9
    narrow = RW_W_RANK + RW_A_RANK + RW_G_RANK
    g = 16
    return g * jnp.where(j < a, j * (MIX_W // g),
                         jnp.where(j < CB_SMALL, j * (MIX_W // g) + narrow // g,
                                   jnp.where(j == CB_SMALL, a * MIX_W // g, (P_IN - MIX_W) // g)))


def _inproj(x, norm_g, mod, w_in_t, layer):
    tm, tn = 1024, MIX_W
    return pl.pallas_call(
        _inproj_kernel,
        out_shape=jax.ShapeDtypeStruct((N_TOK, P_PAD), F32),
        grid=(N_TOK // tm, P_BLOCKS),
        in_specs=[pl.BlockSpec((tm, D_MODEL), lambda i, j: (i, 0)),
                  pl.BlockSpec((None, 1, D_MODEL), lambda i, j: (layer, 0, 0)),
                  _mod_spec(layer, 0, tm), _mod_spec(layer, 1, tm),
                  pl.BlockSpec((pl.Element(1), pl.Element(tn), pl.Element(D_MODEL)),
                               lambda i, j: (layer, _inproj_src_row(j), 0))],
        out_specs=pl.BlockSpec((tm, tn), lambda i, j: (i, j)),
        scratch_shapes=[pltpu.VMEM((tm, D_MODEL), BF16)],
        compiler_params=_params(("parallel", "arbitrary")),
        name="inproj",
    )(x, norm_g.reshape(DEPTH, 1, D_MODEL), mod, mod, w_in_t)


def _conv_mix(b, c, h, w):
    u = c * h
    n = u.shape[0]
    row = lax.broadcasted_iota(jnp.int32, u.shape, 0)
    prev = jnp.where(row == 0, 0.0, pltpu.roll(u, 1, axis=0))
    nxt = jnp.where(row == n - 1, 0.0, pltpu.roll(u, n - 1, axis=0))
    return b * (prev * w[0:1] + u * w[1:2] + nxt * w[2:3])


def _head_rms(x, g):
    return x * lax.rsqrt(jnp.mean(x * x, axis=-1, keepdims=True) + EPS) * g


def _attn_prompt_kernel(cb_ref, cc_ref, ch_ref, q_ref, k_ref, v_ref, cw_ref, qg_ref, kg_ref,
                        ya_ref, yb_ref, nk_ref, nv_ref):
    ya_ref[...] = _conv_mix(cb_ref[...], cc_ref[...], ch_ref[...], cw_ref[...]).astype(ya_ref.dtype)
    sls = [slice(h * NA_DH, (h + 1) * NA_DH) for h in range(NA_HEADS)]
    qn = [_head_rms(q_ref[:, sl], qg_ref[...]) * NA_SCALE for sl in sls]
    kn = [_head_rms(k_ref[:, sl], kg_ref[...]) for sl in sls]
    vh = [v_ref[:, sl] for sl in sls]
    s = [_mm_nt(q.astype(BF16), k.astype(BF16)) for q, k in zip(qn, kn)]
    p = [jnp.exp(x - jnp.max(x, axis=-1, keepdims=True)) for x in s]
    o = [_mm(x.astype(BF16), v.astype(BF16)) / jnp.sum(x, axis=-1, keepdims=True) for x, v in zip(p, vh)]
    yb_ref[...] = jnp.concatenate(o, axis=1).astype(yb_ref.dtype)
    for h in range(NA_HEADS):
        nk_ref[h] = kn[h]
        nv_ref[h] = vh[h]


def _u_spec(rows, col_block, row_off_blocks=0):
    return pl.BlockSpec((rows, MIX_W), lambda b: (b + row_off_blocks, col_block))


def _carry_through(kernel, n_inputs, prev, n_plain_out, n_state_out, layer):
    n_prev = len(prev)

    def body(*refs):
        ins, rest = refs[:n_inputs], list(refs[n_inputs + n_prev:])
        if not n_prev:
            for i in range(n_plain_out, n_plain_out + n_state_out):
                full = rest[i]
                for d in range(DEPTH):
                    if d != layer:
                        full[d] = jnp.zeros(full.shape[1:], full.dtype)
                rest[i] = full.at[layer]
        return kernel(*ins, *rest)

    return body, [pl.BlockSpec(memory_space=pl.ANY)] * n_prev


def _state_spec(shape, layer, first, index):
    def index_map(*g):
        b, *tail = index(*g)
        return (b, 0 if first else layer, *tail)
    return pl.BlockSpec((None, DEPTH if first else None) + tuple(shape), index_map)


def _attn_prompt(u, conv_w, q_g, k_g, layer, prev=(), n_seq=BATCH, seq=SEQ):
    lw = lambda shape: pl.BlockSpec((None,) + shape, lambda b: (layer,) + (0,) * len(shape))
    y_spec = pl.BlockSpec((seq, MIX_W), lambda b: (b, 0))
    kv_spec = _state_spec((NA_HEADS, seq, NA_DH), layer, not prev, lambda b: (b, 0, 0, 0))
    in_specs = ([_u_spec(seq, cb) for cb in (CB_CVB, CB_CVC, CB_CVH, CB_NAQ, CB_NAK, CB_NAV)]
                + [lw((CONV_K, MIX_W)), lw((1, NA_DH)), lw((1, NA_DH))])
    body, prev_specs = _carry_through(_attn_prompt_kernel, len(in_specs), prev, 2, 2, layer)
    return pl.pallas_call(
        body,
        out_shape=(jax.ShapeDtypeStruct((n_seq * seq, MIX_W), BF16),) * 2
        + (jax.ShapeDtypeStruct((n_seq, DEPTH, NA_HEADS, seq, NA_DH), F32),) * 2,
        grid=(n_seq,),
        in_specs=in_specs + prev_specs,
        out_specs=(y_spec, y_spec, kv_spec, kv_spec),
        input_output_aliases={len(in_specs) + i: 2 + i for i in range(len(prev))},
        compiler_params=_params(("parallel",)),
        name="attn_prompt",
    )(u, u, u, u, u, u, conv_w, q_g.reshape(DEPTH, 1, NA_DH), k_g.reshape(DEPTH, 1, NA_DH), *prev)


def _na_kernel(cb_ref, cc_ref, ch_ref, q_ref, k_ref, v_ref, kc_ref, vc_ref, tw_ref, cos_ref, sin_ref,
               perm_ref, cw_ref, qg_ref, kg_ref, ya_ref, yb_ref, qs_ref, ks_ref, tws_ref):
    rows = DEC_SEQ // GRID_W
    wr = min(NA_WIN_R, rows)
    nw = wr * GRID_W
    ya_ref[...] = _conv_mix(cb_ref[...], cc_ref[...], ch_ref[...], cw_ref[...]).astype(ya_ref.dtype)
    cos, sin, perm_b = cos_ref[...], sin_ref[...], perm_ref[...].astype(BF16)

    def rope(x):
        hi, lo = _split2(x)
        return x * cos + (_mm(hi, perm_b) + _mm(lo, perm_b)) * sin

    for h in range(NA_HEADS):
        sl = slice(h * NA_DH, (h + 1) * NA_DH)
        qs_ref[...] = (rope(_head_rms(q_ref[:, sl], qg_ref[...])) * NA_SCALE).astype(BF16)
        ks_ref[...] = rope(_head_rms(k_ref[:, sl], kg_ref[...])).astype(BF16)
        kch = kc_ref[h].astype(BF16)
        vch = vc_ref[h].astype(BF16)
        for p in range(wr):
            tws_ref[p] = jnp.concatenate([tw_ref[h, j - p + NA_WIN_R - 1] for j in range(wr)], axis=1)

        def rows_step(t, carry):
            rr = [t * NA_ROWS_PER_TRIP + i for i in range(NA_ROWS_PER_TRIP)]
            rs = [jnp.clip(r - wr // 2, 0, rows - wr) for r in rr]
            q0 = [pl.multiple_of(r * GRID_W, GRID_W) for r in rr]
            k0 = [pl.multiple_of(x * GRID_W, GRID_W) for x in rs]
            q_r = [qs_ref[pl.ds(x, GRID_W), :] for x in q0]
            s_w = [_mm_nt(q, ks_ref[pl.ds(k, nw), :]) + tws_ref[r - x] for q, k, r, x in zip(q_r, k0, rr, rs)]
            s_c = [_mm_nt(q, kch) for q in q_r]
            m = [jnp.maximum(jnp.max(a, axis=-1, keepdims=True), jnp.max(b, axis=-1, keepdims=True))
                 for a, b in zip(s_w, s_c)]
            p_w = [jnp.exp(a - x) for a, x in zip(s_w, m)]
            p_c = [jnp.exp(b - x) for b, x in zip(s_c, m)]
            den = [jnp.sum(a, axis=-1, keepdims=True) + jnp.sum(b, axis=-1, keepdims=True) for a, b in zip(p_w, p_c)]
            v_w = [v_ref[pl.ds(k, nw), sl].astype(BF16) for k in k0]
            o = [(_mm(a.astype(BF16), v) + _mm(b.astype(BF16), vch)) / d for a, b, v, d in zip(p_w, p_c, v_w, den)]
            for x, val in zip(q0, o):
                yb_ref[pl.ds(x, GRID_W), sl] = val.astype(yb_ref.dtype)
            return carry

        lax.fori_loop(0, rows // NA_ROWS_PER_TRIP, rows_step, 0)


def _na_tables(rpb):
    qc = np.arange(GRID_W)
    kc = np.arange(GRID_W)
    wstart = np.clip(qc - NA_WIN_C // 2, 0, GRID_W - NA_WIN_C)
    colmask = (kc[None, :] >= wstart[:, None]) & (kc[None, :] < wstart[:, None] + NA_WIN_C)
    dc = np.clip(kc[None, :] - qc[:, None], -(NA_WIN_C - 1), NA_WIN_C - 1) + NA_WIN_C - 1
    return jnp.where(colmask[None, None, None], rpb[:, :, :, dc], NEG_INF)


def _rope_tables():
    t = np.arange(DEC_SEQ)
    quarter = NA_DH // 4
    freq = ROPE_THETA ** (-np.arange(quarter, dtype=np.float32) / quarter)
    ang_r = (t // GRID_W).astype(np.float32)[:, None] * freq
    ang_c = (t % GRID_W).astype(np.float32)[:, None] * freq
    cos = np.concatenate([np.cos(ang_r), np.cos(ang_r), np.cos(ang_c), np.cos(ang_c)], axis=-1)
    sin = np.concatenate([-np.sin(ang_r), np.sin(ang_r), -np.sin(ang_c), np.sin(ang_c)], axis=-1)
    src = np.concatenate([np.arange(quarter) + quarter, np.arange(quarter),
                          np.arange(quarter) + 3 * quarter, np.arange(quarter) + 2 * quarter])
    perm = np.zeros((NA_DH, NA_DH), np.float32)
    perm[src, np.arange(NA_DH)] = 1.0
    return cos.astype(np.float32), sin.astype(np.float32), perm


def _na_sample(u, cache_k, cache_v, tw, conv_w, q_g, k_g, layer, n_seq=DEC_BATCH, row_off=N_PROMPT // DEC_SEQ):
    cos, sin, perm = _rope_tables()
    lw = lambda shape: pl.BlockSpec((None,) + shape, lambda b: (layer,) + (0,) * len(shape))
    y_spec = pl.BlockSpec((DEC_SEQ, MIX_W), lambda b: (b, 0))
    c_spec = pl.BlockSpec((None, None, NA_HEADS, PAST_LEN, NA_DH), lambda b: (b, layer, 0, 0, 0))
    wr = min(NA_WIN_R, DEC_SEQ // GRID_W)
    n_off = 2 * NA_WIN_R - 1
    return pl.pallas_call(
        _na_kernel,
        out_shape=(jax.ShapeDtypeStruct((n_seq * DEC_SEQ, MIX_W), BF16),) * 2,
        grid=(n_seq,),
        in_specs=[_u_spec(DEC_SEQ, cb, row_off) for cb in (CB_CVB, CB_CVC, CB_CVH, CB_NAQ, CB_NAK, CB_NAV)]
        + [c_spec, c_spec, lw((NA_HEADS, n_off, GRID_W, GRID_W)),
           _full((DEC_SEQ, NA_DH)), _full((DEC_SEQ, NA_DH)), _full((NA_DH, NA_DH)),
           lw((CONV_K, MIX_W)), lw((1, NA_DH)), lw((1, NA_DH))],
        out_specs=(y_spec, y_spec),
        scratch_shapes=[pltpu.VMEM((DEC_SEQ, NA_DH), BF16), pltpu.VMEM((DEC_SEQ, NA_DH), BF16),
                        pltpu.VMEM((wr, GRID_W, wr * GRID_W), F32)],
        compiler_params=_params(("parallel",)),
        name="na_sample",
    )(u, u, u, u, u, u, cache_k, cache_v, tw, jnp.asarray(cos), jnp.asarray(sin), jnp.asarray(perm),
      conv_w, q_g.reshape(DEPTH, 1, NA_DH), k_g.reshape(DEPTH, 1, NA_DH))


def _seg_ones(width, seg):
    a = lax.broadcasted_iota(jnp.int32, (width, width), 0) // seg
    b = lax.broadcasted_iota(jnp.int32, (width, width), 1) // seg
    return (a == b).astype(F32)


def _rwkv_kernel(*refs, seq, has_s0, emit_state):
    it = iter(refs)
    r_ref, k_ref, v_ref, sm_ref = (next(it) for _ in range(4))
    (w0_ref, wup_ref, a0_ref, aup_ref, gup_ref, kkp_ref, kap_ref, rkp_ref, lng_ref, lnb_ref) = (
        next(it) for _ in range(10))
    mask_ref = next(it)
    s0_ref = next(it) if has_s0 else None
    y_ref = next(it)
    so_ref = next(it) if emit_state else None
    kk_ref, lw_ref, ka_ref, kd_ref, coef_ref, ysp_ref = (next(it) for _ in range(6))

    dh = RW_DH
    pw = 2 * dh
    nc = seq // CHUNK
    seg_b = _seg_ones(pw, dh).astype(BF16)

    r = r_ref[...]
    k = k_ref[...]
    sm = sm_ref[...]
    wl = jnp.tanh(sm[:, SM_WL:SM_WL + RW_W_RANK])
    al = sm[:, SM_AL:SM_AL + RW_A_RANK]
    kk = k * kkp_ref[...]
    kk = kk * lax.rsqrt(_mm_exact_rhs(kk * kk, seg_b) + EPS)
    kk_ref[...] = kk
    for z in range(2):
        lw_ref[z] = -RW_DECAY_SCALE * _sigmoid(w0_ref[z:z + 1, :] + _mm3(wl, wup_ref[z]))
        a = _sigmoid(a0_ref[z:z + 1, :] + _mm3(al, aup_ref[z]))
        ka_ref[z] = kk * a
        kd_ref[z] = k * (1.0 + (a - 1.0) * kap_ref[...])

    cat = jnp.concatenate

    def phase1(c2, carry):
        incl, strict, ti, si = _tri_masks(CHUNK)
        incl_b = (mask_ref[0], mask_ref[1])
        strict2 = tuple(cat([m, m], axis=1) for m in strict)
        incl2 = tuple(cat([m, m], axis=1) for m in incl)
        diag_blk = (ti // SUB == si // SUB).astype(F32)
        eye = (ti == si).astype(F32)
        zero_pair = jnp.zeros((CHUNK, pw), F32)
        head_of_lane = lax.broadcasted_iota(jnp.int32, (CHUNK, pw), 1) // dh
        head_of_lane2 = (lax.broadcasted_iota(jnp.int32, (CHUNK, 2 * pw), 1) // dh) % 2
        r_pw = lax.broadcasted_iota(jnp.int32, (pw, pw), 0)
        c_pw = lax.broadcasted_iota(jnp.int32, (pw, pw), 1)
        same_head = r_pw // dh == c_pw // dh
        eye_pw = (r_pw == c_pw).astype(F32)
        chains = []
        for cc in range(P1_CHUNKS):
            rows = pl.ds(pl.multiple_of((c2 * P1_CHUNKS + cc) * CHUNK, CHUNK), CHUNK)
            vc, rc, kkc = v_ref[rows, :], r_ref[rows, :], kk_ref[rows, :]
            for z in range(2):
                lwc = lw_ref[z, rows, :]
                cum = _mm_exact_lhs(incl_b[z], lwc)
                tot = cum[CHUNK - 1:CHUNK] if z == 0 else cum[0:1]
                e_neg = jnp.exp(-cum)
                dec = jnp.exp(tot - cum)
                kac, kdc = ka_ref[z, rows, :], kd_ref[z, rows, :]
                chains.append((z, kkc * jnp.exp(cum - lwc), rc * jnp.exp(cum), vc, kac * e_neg, kdc * e_neg,
                               kac * dec, kdc * dec, jnp.exp(tot)))
        heads = [(q, j) for q in range(len(chains)) for j in range(2)]
        zq = [ch[0] for ch in chains]
        ymat = [cat([ch[4], ch[5]], axis=0).astype(BF16) for ch in chains]
        vz = [cat([zero_pair, ch[3]], axis=0) for ch in chains]
        nvz = [cat([zero_pair, -ch[3]], axis=1) for ch in chains]
        x_in = [cat([jnp.where(head_of_lane == j, chains[q][1], 0.0),
                     jnp.where(head_of_lane == j, chains[q][2], 0.0)], axis=0) for q, j in heads]
        aa = [_mm_nt(x.astype(BF16), ymat[q]) for x, (q, _) in zip(x_in, heads)]
        top = [a[0:CHUNK] * strict2[zq[q]] for a, (q, _) in zip(aa, heads)]
        a_r = [a[CHUNK:] * incl2[zq[q]] for a, (q, _) in zip(aa, heads)]
        akv = [_mm1(t, vz[q]) for t, (q, _) in zip(top, heads)]
        low = [t[:, 0:CHUNK] for t in top]
        ld = [x * diag_blk for x in low]
        lo = [x - y for x, y in zip(low, ld)]
        l2 = [_mm1(x, x) for x in ld]
        l4 = [_mm1(x, x) for x in l2]
        l8 = [_mm1(x, x) for x in l4]
        td = [eye - x for x in ld]
        for lp in (l2, l4, l8):
            td = [t + _mm1(t, p_) for t, p_ in zip(td, lp)]
        x0 = [_mm1(t, cat([chains[q][1], a, l], axis=1)) for t, a, l, (q, _) in zip(td, akv, lo, heads)]
        pq0 = [x[:, 0:2 * pw] for x in x0]
        wm = [x[:, 2 * pw:] for x in x0]
        pq = pq0
        for _ in range(CHUNK // SUB - 1):
            pq = [p0 - _mm1(w, p_) for p0, w, p_ in zip(pq0, wm, pq)]
        ryc = [_mm1(a, cat([p_, nvz[q]], axis=0)) for a, p_, (q, _) in zip(a_r, pq, heads)]
        for q, ch in enumerate(chains):
            pq_m = jnp.where(head_of_lane2 == 0, pq[2 * q], pq[2 * q + 1])
            ryc_m = jnp.where(head_of_lane2 == 0, ryc[2 * q], ryc[2 * q + 1])
            g1 = _mm1(ch[6].T, pq_m)
            g2 = _mm1(ch[7].T, ch[3])
            g_t = eye_pw * ch[8] - jnp.where(same_head, g1[:, 0:pw], 0.0)
            h_t = jnp.where(same_head, g2 - g1[:, pw:], 0.0)
            coef_ref[c2 * P1_CHUNKS + q // 2, q % 2] = cat(
                [g_t, ch[2] - ryc_m[:, 0:pw], h_t, -ryc_m[:, pw:]], axis=0)
        return carry

    lax.fori_loop(0, nc // P1_CHUNKS, phase1, 0)

    def block_diag(a, b):
        zero = jnp.zeros((dh, dh), F32)
        return cat([cat([a, zero], axis=1), cat([zero, b], axis=1)], axis=0)

    m_init = tuple((block_diag(s0_ref[z, 0].T, s0_ref[z, 1].T) if has_s0 else jnp.zeros((pw, pw), F32))
                   for z in range(2))
    n_lhs = pw + CHUNK

    def phase2(ci, ms):
        new_ms, ys = [], []
        for z in range(2):
            c = ci if z == 0 else nc - 1 - ci
            out = _mm3(coef_ref[c, z, 0:n_lhs, :], ms[z]) + coef_ref[c, z, n_lhs:, :]
            new_ms.append(out[0:pw])
            ys.append(out[pw:])
        ysp_ref[ci] = cat(ys, axis=1)
        return tuple(new_ms)

    m_fin = lax.fori_loop(0, nc, phase2, m_init)

    if emit_state:
        for z in range(2):
            for j in range(2):
                so_ref[z, j] = m_fin[z][j * dh:(j + 1) * dh, j * dh:(j + 1) * dh].T

    y = cat([ysp_ref[c, :, 0:pw] + ysp_ref[nc - 1 - c, :, pw:] for c in range(nc)], axis=0)
    mu = _mm_exact_rhs(y, seg_b) * (1.0 / dh)
    yc = y - mu
    var = _mm_exact_rhs(yc * yc, seg_b) * (1.0 / dh)
    yn = yc * lax.rsqrt(var + RW_GN_EPS) * lng_ref[...] + lnb_ref[...]
    bonus = _mm_exact_rhs(r * k * rkp_ref[...], seg_b) * v_ref[...]
    g = _mm3(_sigmoid(sm[:, SM_GL:SM_GL + RW_G_RANK]), gup_ref[...])
    y_ref[...] = ((yn + bonus) * g).astype(y_ref.dtype)


def _rwkv(u, p, layer, n_seq, seq, row_off, s0=None, emit_state=False, prev=()):
    n_pairs = RW_HEADS // 2
    pw = 2 * RW_DH
    bpc = MIX_W // pw
    lw = lambda shape: pl.BlockSpec((None,) + shape, lambda b, hp: (layer,) + (0,) * (len(shape) - 1) + (hp,))
    row = lambda a: a.reshape(DEPTH, 1, MIX_W)
    u_pair = lambda cb: pl.BlockSpec((seq, pw), lambda b, hp: (b + row_off, cb * bpc + hp))
    in_specs = [u_pair(CB_RWR), u_pair(CB_RWK), u_pair(CB_RWV),
                pl.BlockSpec((seq, MIX_W), lambda b, hp: (b + row_off, CB_SMALL)),
                lw((2, pw)), lw((2, RW_W_RANK, pw)), lw((2, pw)), lw((2, RW_A_RANK, pw)),
                lw((RW_G_RANK, pw))] + [lw((1, pw))] * 5 + [_full((2, CHUNK, CHUNK))]
    t_idx = np.arange(CHUNK)
    incl_masks = np.stack([t_idx[None, :] <= t_idx[:, None], t_idx[None, :] >= t_idx[:, None]])
    args = [u, u, u, u, p['rw_w0'], p['rw_w_up'], p['rw_a0'], p['rw_a_up'], p['rw_g_up'],
            row(p['rw_k_k']), row(p['rw_k_a']), row(p['rw_r_k']), row(p['rw_ln_g']), row(p['rw_ln_b']),
            jnp.asarray(incl_masks, BF16)]
    st_blk = (2, 2, RW_DH, RW_DH)
    if s0 is not None:
        in_specs.append(pl.BlockSpec((None, None) + st_blk, lambda b, hp: (b, layer, 0, hp, 0, 0)))
        args.append(s0)
    out_shape = [jax.ShapeDtypeStruct((n_seq * seq, MIX_W), BF16)]
    out_specs = [pl.BlockSpec((seq, pw), lambda b, hp: (b, hp))]
    if emit_state:
        out_shape.append(jax.ShapeDtypeStruct((n_seq, DEPTH, 2, RW_HEADS, RW_DH, RW_DH), F32))
        out_specs.append(_state_spec(st_blk, layer, not prev, lambda b, hp: (b, 0, hp, 0, 0)))
    nc = seq // CHUNK
    tok = lambda n: pltpu.VMEM((n, seq, pw) if n else (seq, pw), F32)
    body, prev_specs = _carry_through(
        functools.partial(_rwkv_kernel, seq=seq, has_s0=s0 is not None, emit_state=emit_state), len(in_specs), prev,
        1, 1 if emit_state else 0, layer)
    return pl.pallas_call(
        body,
        out_shape=tuple(out_shape), grid=(n_seq, n_pairs), in_specs=in_specs + prev_specs,
        out_specs=tuple(out_specs),
        input_output_aliases={len(in_specs) + i: 1 + i for i in range(len(prev))},
        scratch_shapes=[tok(0), tok(2), tok(2), tok(2),
                        pltpu.VMEM((nc, 2, 2 * (pw + CHUNK), pw), F32), pltpu.VMEM((nc, CHUNK, 2 * pw), F32)],
        compiler_params=_params(("parallel", "parallel")),
        name=f"rwkv_{seq}",
    )(*args, *prev)


def _log_sigmoid(x):
    return jnp.minimum(x, 0.0) - jnp.log(1.0 + jnp.exp(-jnp.abs(x)))


def _mlstm_kernel(*refs, seq, has_s0, emit_state):
    it = iter(refs)
    q_ref, k_ref, v_ref, o_ref, sm_ref, gr_ref, bc_ref, br_ref, ng_ref = (next(it) for _ in range(9))
    c0_ref, n0_ref, m0_ref = ((next(it), next(it), next(it)) if has_s0 else (None, None, None))
    y_ref = next(it)
    co_ref, no_ref, mo_ref = ((next(it), next(it), next(it)) if emit_state else (None, None, None))
    hs_ref, c_ref, n_ref, m_ref = (next(it) for _ in range(4))

    nc = seq // CHUNK
    n_st = 2 * ML_HEADS
    for i in range(n_st):
        z, h = divmod(i, ML_HEADS)
        c_ref[i] = c0_ref[z, h] if has_s0 else jnp.zeros((ML_DH, ML_DH), F32)
        n_ref[i] = n0_ref[z, h] if has_s0 else jnp.zeros((1, ML_DH), F32)
        m_ref[i] = m0_ref[z, h] if has_s0 else jnp.zeros((1, 1), F32)

    incl, _, ti, si = _tri_masks(CHUNK)
    before = ((si <= ti), (si >= ti))

    nh = ML_HEADS
    chains = [(z, h) for z in range(2) for h in range(nh)]
    each = lambda f, *xs: [f(*a) for a in zip(*xs)]
    zs = [z for z, _ in chains]

    def chunk_step(ci, carry):
        per_dir = []
        for z in range(2):
            c = ci if z == 0 else nc - 1 - ci
            rows = pl.ds(pl.multiple_of(c * CHUNK, CHUNK), CHUNK)
            g0 = SM_GATE + z * 2 * nh
            gc = sm_ref[rows, g0:g0 + 2 * nh] + bc_ref[:, z * 2 * nh:(z + 1) * 2 * nh]
            gr = gr_ref[c, z * 2 * nh:(z + 1) * 2 * nh, :] + br_ref[z * 2 * nh:(z + 1) * 2 * nh, :]
            b_cols = _mm(incl[z], _log_sigmoid(gc[:, nh:]), HI)
            b_rows = _mm(_log_sigmoid(gr[nh:]), incl[1 - z], HI)
            per_dir.append((rows, gc[:, :nh], b_cols, gr[:nh], b_rows))
        rows = [per_dir[z][0] for z, _ in chains]
        hsl = [slice(h * ML_DH, (h + 1) * ML_DH) for _, h in chains]
        i_col = [per_dir[z][1][:, h:h + 1] for z, h in chains]
        b_col = [per_dir[z][2][:, h:h + 1] for z, h in chains]
        i_row = [per_dir[z][3][h:h + 1] for z, h in chains]
        b_row = [per_dir[z][4][h:h + 1] for z, h in chains]
        b_last = each(lambda b, z: b[CHUNK - 1:CHUNK] if z == 0 else b[0:1], b_col, zs)
        m_old = [m_ref[i] for i in range(n_st)]
        cm = [c_ref[i] for i in range(n_st)]
        nv = [n_ref[i] for i in range(n_st)]
        qc = each(lambda r, s: q_ref[r, s] * (ML_DH ** -0.5), rows, hsl)
        kc = each(lambda r, s: k_ref[r, s], rows, hsl)
        vc = each(lambda r, s: v_ref[r, s], rows, hsl)
        qk = each(lambda q, k: _mm_nt(q.astype(BF16), k.astype(BF16)), qc, kc)
        qcm = each(lambda q, c_: _mm_nt(q.astype(BF16), c_.astype(BF16)), qc, cm)
        a_t = each(lambda b, m: b + m, b_col, m_old)
        dmat = each(lambda bc, brw, ir, z: jnp.where(before[z], bc - brw + ir, NEG_INF), b_col, b_row, i_row, zs)
        m_t = each(lambda a, d: jnp.maximum(a, jnp.max(d, axis=-1, keepdims=True)), a_t, dmat)
        s = each(lambda x, d, m: x * jnp.exp(d - m), qk, dmat, m_t)
        inter = each(lambda a, m: jnp.exp(a - m), a_t, m_t)
        sv = each(_mm1, s, vc)
        g_col = each(lambda bl, bc, ic: bl - bc + ic, b_last, b_col, i_col)
        a_l = each(lambda bl, m: bl + m, b_last, m_old)
        m_new = each(lambda a, g: jnp.maximum(a, jnp.max(g, axis=0, keepdims=True)), a_l, g_col)
        wgt = each(lambda g, m: jnp.exp(g - m), g_col, m_new)
        vk = each(lambda v, w, k: _mm1((v * w).T, k), vc, wgt, kc)
        decay = each(lambda a, m: jnp.exp(a - m), a_l, m_new)
        num = each(lambda i_, x, y: i_ * x + y, inter, qcm, sv)
        den = each(lambda i_, q, n_, s_: i_ * jnp.sum(q * n_, axis=-1, keepdims=True)
                   + jnp.sum(s_, axis=-1, keepdims=True), inter, qc, nv, s)
        hh = each(lambda n_, d, m: n_ / jnp.maximum(jnp.abs(d), jnp.exp(-m)), num, den, m_t)
        for z in range(2):
            hs_ref[z, per_dir[z][0], :] = jnp.concatenate(hh[z * nh:(z + 1) * nh], axis=1)
        for i in range(n_st):
            c_ref[i] = decay[i] * cm[i] + vk[i]
            n_ref[i] = decay[i] * nv[i] + jnp.sum(wgt[i] * kc[i], axis=0, keepdims=True)
            m_ref[i] = m_new[i]
        return carry

    lax.fori_loop(0, nc, chunk_step, 0)

    if emit_state:
        for i in range(n_st):
            z, h = divmod(i, ML_HEADS)
            co_ref[z, h] = c_ref[i]
            no_ref[z, h] = n_ref[i]
            mo_ref[z, h] = m_ref[i]

    for h in range(ML_HEADS):
        hsl = slice(h * ML_DH, (h + 1) * ML_DH)
        hn = _head_rms(hs_ref[0, :, hsl] + hs_ref[1, :, hsl], ng_ref[:, hsl])
        y_ref[:, hsl] = (_sigmoid(o_ref[:, hsl]) * hn).astype(y_ref.dtype)


def _mlstm(u, gate_rows, p, layer, n_seq, seq, row_off, state=None, emit_state=False, prev=()):
    lw = lambda shape: pl.BlockSpec((None,) + shape, lambda b: (layer,) + (0,) * len(shape))
    nc = seq // CHUNK
    n_gate = 4 * ML_HEADS
    in_specs = [_u_spec(seq, cb, row_off) for cb in (CB_MLQ, CB_MLK, CB_MLV, CB_MLO, CB_GATE)] + [
        pl.BlockSpec((None, nc, n_gate, CHUNK), lambda b: (b, 0, 0, 0)),
        lw((1, n_gate)), lw((n_gate, 1)), lw((1, MIX_W))]
    args = [u, u, u, u, u, gate_rows, p['ml_gate_b'].reshape(DEPTH, 1, n_gate),
            p['ml_gate_b'].reshape(DEPTH, n_gate, 1), p['ml_norm_g'].reshape(DEPTH, 1, MIX_W)]
    c_shape, n_shape, m_shape = (2, ML_HEADS, ML_DH, ML_DH), (2, ML_HEADS, 1, ML_DH), (2, ML_HEADS, 1, 1)
    if state is not None:
        c0, n0, m0 = state
        for a, shp in ((c0, c_shape), (n0, n_shape), (m0, m_shape)):
            in_specs.append(pl.BlockSpec((None, None) + shp, lambda b: (b, layer, 0, 0, 0, 0)))
            args.append(a.reshape(a.shape[:2] + shp))
    out_shape = [jax.ShapeDtypeStruct((n_seq * seq, MIX_W), BF16)]
    out_specs = [pl.BlockSpec((seq, MIX_W), lambda b: (b, 0))]
    if emit_state:
        for shp in (c_shape, n_shape, m_shape):
            out_shape.append(jax.ShapeDtypeStruct((n_seq, DEPTH) + shp, F32))
            out_specs.append(_state_spec(shp, layer, not prev, lambda b: (b, 0, 0, 0, 0)))
    n_st = 2 * ML_HEADS
    body, prev_specs = _carry_through(
        functools.partial(_mlstm_kernel, seq=seq, has_s0=state is not None, emit_state=emit_state),
        len(in_specs), prev, 1, 3 if emit_state else 0, layer)
    return pl.pallas_call(
        body,
        out_shape=tuple(out_shape), grid=(n_seq,), in_specs=in_specs + prev_specs, out_specs=tuple(out_specs),
        input_output_aliases={len(in_specs) + i: 1 + i for i in range(len(prev))},
        scratch_shapes=[pltpu.VMEM((2, seq, MIX_W), F32), pltpu.VMEM((n_st, ML_DH, ML_DH), F32),
                        pltpu.VMEM((n_st, 1, ML_DH), F32), pltpu.VMEM((n_st, 1, 1), F32)],
        compiler_params=_params(("parallel",)),
        name=f"mlstm_{seq}",
    )(*args, *prev)


def _outproj_kernel(ya_ref, yb_ref, yc_ref, yd_ref, w_ref, x_ref, g_ref, o_ref):
    acc = _mm(ya_ref[...], w_ref[0:MIX_W, :].astype(BF16))
    for i, y_ref in enumerate((yb_ref, yc_ref, yd_ref), start=1):
        acc += _mm(y_ref[...], w_ref[i * MIX_W:(i + 1) * MIX_W, :].astype(BF16))
    o_ref[...] = x_ref[...] + g_ref[...] * acc


def _outproj(ys, w_out, x, mod, layer):
    tm, tn = 1024, 1024
    y_spec = pl.BlockSpec((tm, MIX_W), lambda i, j: (i, 0))
    return pl.pallas_call(
        _outproj_kernel,
        out_shape=jax.ShapeDtypeStruct((N_TOK, D_MODEL), F32),
        grid=(N_TOK // tm, D_MODEL // tn),
        in_specs=[y_spec] * 4 + [pl.BlockSpec((None, D_MODEL, tn), lambda i, j: (layer, 0, j)),
                                 pl.BlockSpec((tm, tn), lambda i, j: (i, j)),
                                 _mod_spec(layer, 2, tm, tn, with_j=True)],
        out_specs=pl.BlockSpec((tm, tn), lambda i, j: (i, j)),
        compiler_params=_params(("parallel", "parallel")),
        name="outproj",
    )(*ys, w_out, x, mod)


def _router_kernel(x_ref, g_ref, sh_ref, sc_ref, rw_ref, rb_ref, xm_ref, gt_ref, ids_ref):
    xm = _modulated(x_ref[...], g_ref[...], sh_ref[...], sc_ref[...])
    xm_ref[...] = xm
    logits = _mm_nt(rw_ref[...], xm, HI)
    ex = jnp.exp(logits - jnp.max(logits, axis=0, keepdims=True))
    scores = ex / jnp.sum(ex, axis=0, keepdims=True)
    sel = scores + rb_ref[...]
    per = N_EXPERTS // N_EXPERT_GROUPS
    s = [sel[e:e + 1, :] for e in range(N_EXPERTS)]
    grp_score = []
    for g in range(N_EXPERT_GROUPS):
        a, b, c, d = s[per * g:per * (g + 1)]
        hi1, lo1, hi2, lo2 = jnp.maximum(a, b), jnp.minimum(a, b), jnp.maximum(c, d), jnp.minimum(c, d)
        grp_score.append(jnp.maximum(hi1, hi2) + jnp.maximum(jnp.minimum(hi1, hi2), jnp.maximum(lo1, lo2)))
    best = functools.reduce(jnp.maximum, grp_score)
    in_grp, taken = [], jnp.zeros_like(best)
    for g in range(N_EXPERT_GROUPS):
        hit = jnp.where(grp_score[g] == best, 1.0, 0.0) * (1.0 - taken)
        in_grp.append(hit)
        taken = taken + hit
    picked, flag = [], []
    for e in range(N_EXPERTS):
        g = e // per
        rank = jnp.zeros_like(best)
        for o in range(per * g, per * (g + 1)):
            if o < e:
                rank += jnp.where(s[o] >= s[e], 1.0, 0.0)
            elif o > e:
                rank += jnp.where(s[o] > s[e], 1.0, 0.0)
        flag.append(in_grp[g] * jnp.where(rank < 2.0, 1.0, 0.0))
        picked.append(flag[e] * scores[e:e + 1, :])
    total = functools.reduce(lambda x, y: x + y, picked)
    for e in range(N_EXPERTS):
        gt_ref[e:e + 1, :] = picked[e] / total
    lo_id = functools.reduce(
        jnp.minimum, [jnp.where(flag[e] > 0.0, float(e), float(N_EXPERTS)) for e in range(N_EXPERTS)])
    hi_id = functools.reduce(jnp.maximum, [jnp.where(flag[e] > 0.0, float(e), -1.0) for e in range(N_EXPERTS)])
    ids_ref[0:1, :] = lo_id
    ids_ref[1:2, :] = hi_id
    ids_ref[2:, :] = jnp.zeros((ids_ref.shape[0] - 2,) + lo_id.shape[1:], F32)


def _router(x, norm_g, mod, router_wt, router_b, layer):
    tm = 512
    return pl.pallas_call(
        _router_kernel,
        out_shape=(jax.ShapeDtypeStruct((N_TOK, D_MODEL), F32), jax.ShapeDtypeStruct((N_EXPERTS, N_TOK), F32),
                   jax.ShapeDtypeStruct((8, N_TOK), F32)),
        grid=(N_TOK // tm,),
        in_specs=[pl.BlockSpec((tm, D_MODEL), lambda i: (i, 0)),
                  pl.BlockSpec((None, 1, D_MODEL), lambda i: (layer, 0, 0)),
                  _mod_spec(layer, 3, tm), _mod_spec(layer, 4, tm),
                  _full((N_EXPERTS, D_MODEL)), _full((N_EXPERTS, 1))],
        out_specs=(pl.BlockSpec((tm, D_MODEL), lambda i: (i, 0)), pl.BlockSpec((N_EXPERTS, tm), lambda i: (0, i)),
                   pl.BlockSpec((8, tm), lambda i: (0, i))),
        compiler_params=_params(("parallel",)),
        name="router",
    )(x, norm_g.reshape(DEPTH, 1, D_MODEL), mod, mod, router_wt, router_b.reshape(N_EXPERTS, 1))


MOE_TM = 512
N_PAIRS = N_EXPERT_GROUPS * 6
MOE_STEPS = 2 * (N_PAIRS + N_TOK // MOE_TM - 1)


def _gather_rows(src_hbm, dst_ref, idx_ref, base, n, sem):
    def issue(r, carry):
        pltpu.make_async_copy(src_hbm.at[pl.ds(idx_ref[base + r], 1), :], dst_ref.at[pl.ds(r, 1), :], sem).start()
        return carry

    def wait(r, carry):
        pltpu.make_async_copy(src_hbm.at[pl.ds(0, 1), :], dst_ref.at[pl.ds(r, 1), :], sem).wait()
        return carry

    lax.fori_loop(0, n, issue, 0, unroll=8)
    lax.fori_loop(0, n, wait, 0, unroll=8)


def _moe_plan(ids):
    i32 = jnp.int32
    lo, hi = ids[0].astype(i32), ids[1].astype(i32)
    src = jnp.argsort(lo * N_EXPERTS + hi).astype(i32)
    pos = jnp.argsort(src).astype(i32)
    n_tiles = N_TOK // MOE_TM
    ex = jnp.arange(N_EXPERTS, dtype=i32)
    lo_s, hi_s = lo[src].reshape(n_tiles, MOE_TM, 1), hi[src].reshape(n_tiles, MOE_TM, 1)
    used = ((lo_s == ex).any(axis=1) | (hi_s == ex).any(axis=1)).reshape(-1)
    n_valid = jnp.sum(used).astype(i32)
    idx = jnp.nonzero(used, size=MOE_STEPS, fill_value=0)[0].astype(i32)
    valid = jnp.arange(MOE_STEPS, dtype=i32) < n_valid
    idx = jnp.where(valid, idx, idx[jnp.maximum(n_valid - 1, 0)])
    tile, exp = idx // N_EXPERTS, idx % N_EXPERTS
    first = valid & (tile != jnp.concatenate([jnp.full((1,), -1, i32), tile[:-1]]))
    return src, pos, tile, exp, first.astype(i32), valid.astype(i32)


def _moe_kernel(tile_ref, exp_ref, first_ref, valid_ref, src_ref, xm_hbm, gates_ref, w1_ref, w3_ref, w2_ref,
                o_ref, xs_ref, xb_ref, sem):
    s = pl.program_id(0)

    @pl.when(first_ref[s] == 1)
    def _():
        _gather_rows(xm_hbm, xs_ref, src_ref, tile_ref[s] * MOE_TM, MOE_TM, sem)
        xb_ref[...] = xs_ref[...].astype(BF16)
        o_ref[...] = jnp.zeros_like(o_ref)

    @pl.when(valid_ref[s] == 1)
    def _():
        xb = xb_ref[...]
        h1 = _mm(xb, w1_ref[...].astype(BF16))
        h3 = _mm(xb, w3_ref[...].astype(BF16))
        gates = gates_ref[...]
        lane = lax.broadcasted_iota(jnp.int32, gates.shape, 1)
        gate = jnp.sum(jnp.where(lane == exp_ref[s], gates, 0.0), axis=-1, keepdims=True)
        hh = h1 * _sigmoid(h1) * h3 * gate
        o_ref[...] += _mm(hh.astype(BF16), w2_ref[...].astype(BF16))


def _moe(xm, gates_sorted, plan, w1, w3, w2, layer):
    src, _, tile, exp, first, valid = plan
    w_in = pl.BlockSpec((None, None, D_MODEL, D_EXPERT), lambda s, t, e, f, v, i: (layer, e[s], 0, 0))
    grid_spec = pltpu.PrefetchScalarGridSpec(
        num_scalar_prefetch=5, grid=(MOE_STEPS,),
        in_specs=[pl.BlockSpec(memory_space=pl.ANY),
                  pl.BlockSpec((MOE_TM, N_EXPERTS), lambda s, t, e, f, v, i: (t[s], 0)),
                  w_in, w_in,
                  pl.BlockSpec((None, None, D_EXPERT, D_MODEL), lambda s, t, e, f, v, i: (layer, e[s], 0, 0))],
        out_specs=pl.BlockSpec((MOE_TM, D_MODEL), lambda s, t, e, f, v, i: (t[s], 0)),
        scratch_shapes=[pltpu.VMEM((MOE_TM, D_MODEL), F32), pltpu.VMEM((MOE_TM, D_MODEL), BF16),
                        pltpu.SemaphoreType.DMA(())])
    return pl.pallas_call(
        _moe_kernel, out_shape=jax.ShapeDtypeStruct((N_TOK, D_MODEL), F32), grid_spec=grid_spec,
        compiler_params=_params(("arbitrary",)), name="moe",
    )(tile, exp, first, valid, src, xm, gates_sorted, w1, w3, w2)


def _combine_kernel(pos_ref, acc_hbm, x_ref, g_ref, o_ref, buf_ref, sem):
    _gather_rows(acc_hbm, buf_ref, pos_ref, pl.program_id(0) * MOE_TM, MOE_TM, sem)
    o_ref[...] = x_ref[...] + g_ref[...] * buf_ref[...]


def _combine(acc_sorted, pos, x, mod, layer):
    tm = MOE_TM
    row = pl.BlockSpec((tm, D_MODEL), lambda i, p: (i, 0))
    grid_spec = pltpu.PrefetchScalarGridSpec(
        num_scalar_prefetch=1, grid=(N_TOK // tm,),
        in_specs=[pl.BlockSpec(memory_space=pl.ANY), row,
                  pl.BlockSpec((None, None, None, 1, D_MODEL), lambda i, p: (layer, _mod_row(i, tm), 5, 0, 0))],
        out_specs=row,
        scratch_shapes=[pltpu.VMEM((tm, D_MODEL), F32), pltpu.SemaphoreType.DMA(())])
    return pl.pallas_call(
        _combine_kernel, out_shape=jax.ShapeDtypeStruct((N_TOK, D_MODEL), F32), grid_spec=grid_spec,
        compiler_params=_params(("arbitrary",)), name="combine",
    )(pos, acc_sorted, x, mod)


def _gate_rows(gcols, n_seq, seq):
    return gcols.reshape(n_seq, seq // CHUNK, CHUNK, gcols.shape[-1]).transpose(0, 1, 3, 2)


def kernel(x_prompt, x_sample, cache_na_k, cache_na_v, state_rwkv, state_mlstm_c, state_mlstm_n, state_mlstm_m,
           c, c_ctx, norm1_g, norm2_g, w_mod, b_mod, w_in, conv_w, na_q_g, na_k_g, na_rpb, rw_w0, rw_w_up, rw_a0,
           rw_a_up, rw_g_up, rw_k_k, rw_k_a, rw_r_k, rw_ln_g, rw_ln_b, ml_gate_b, ml_norm_g, w_out, router_w,
           router_b, moe_w1, moe_w3, moe_w2):
    p = dict(rw_w0=rw_w0, rw_w_up=rw_w_up, rw_a0=rw_a0, rw_a_up=rw_a_up, rw_g_up=rw_g_up, rw_k_k=rw_k_k,
             rw_k_a=rw_k_a, rw_r_k=rw_r_k, rw_ln_g=rw_ln_g, rw_ln_b=rw_ln_b, ml_gate_b=ml_gate_b,
             ml_norm_g=ml_norm_g)
    cvecs = jnp.concatenate([c_ctx[None], c, jnp.zeros((MOD_ROWS - 1 - DEC_BATCH, D_MODEL), F32)], axis=0)
    mod = _adaln(cvecs, w_mod, b_mod).reshape(DEPTH, MOD_ROWS, 6, 1, D_MODEL)

    assert w_in.shape[-1] == P_IN
    w_in_t = jnp.swapaxes(w_in, 1, 2)
    tw = _na_tables(na_rpb)
    router_wt = router_w.T
    sample_row_off = N_PROMPT // DEC_SEQ

    x = jnp.concatenate([x_prompt.reshape(N_PROMPT, D_MODEL), x_sample.reshape(N_SAMPLE, D_MODEL)], axis=0)
    kv_prev, rw_prev, ml_prev = (), (), ()
    for l in range(DEPTH):
        u = _inproj(x, norm1_g, mod, w_in_t, l)
        ya_p, yb_p, *kv_prev = _attn_prompt(u, conv_w, na_q_g, na_k_g, l, prev=tuple(kv_prev))
        ya_s, yb_s = _na_sample(u, cache_na_k, cache_na_v, tw, conv_w, na_q_g, na_k_g, l)
        yc_p, *rw_prev = _rwkv(u, p, l, BATCH, SEQ, 0, emit_state=True, prev=tuple(rw_prev))
        (yc_s,) = _rwkv(u, p, l, DEC_BATCH, DEC_SEQ, sample_row_off, s0=state_rwkv)
        g0 = CB_GATE * MIX_W + SM_GATE
        gcols = u[:, g0:g0 + 4 * ML_HEADS]
        yd_p, *ml_prev = _mlstm(u, _gate_rows(gcols[:N_PROMPT], BATCH, SEQ), p, l, BATCH, SEQ, 0,
                                emit_state=True, prev=tuple(ml_prev))
        (yd_s,) = _mlstm(u, _gate_rows(gcols[N_PROMPT:], DEC_BATCH, DEC_SEQ), p, l, DEC_BATCH, DEC_SEQ,
                         sample_row_off, state=(state_mlstm_c, state_mlstm_n, state_mlstm_m))
        ys = [jnp.concatenate(pair, axis=0) for pair in ((ya_p, ya_s), (yb_p, yb_s), (yc_p, yc_s), (yd_p, yd_s))]
        x = _outproj(ys, w_out, x, mod, l)
        xm, gates_t, ids = _router(x, norm2_g, mod, router_wt, router_b, l)
        plan = _moe_plan(ids)
        acc_sorted = _moe(xm, gates_t.T[plan[0]], plan, moe_w1, moe_w3, moe_w2, l)
        x = _combine(acc_sorted, plan[1], x, mod, l)
    new_c, new_n, new_m = ml_prev
    return (x[:N_PROMPT].reshape(BATCH, SEQ, D_MODEL), x[N_PROMPT:].reshape(DEC_BATCH, DEC_SEQ, D_MODEL),
            kv_prev[0], kv_prev[1], rw_prev[0], new_c, new_n.reshape(BATCH, DEPTH, 2, ML_HEADS, ML_DH),
            new_m.reshape(BATCH, DEPTH, 2, ML_HEADS))
```

```python
import functools

import numpy as np
import jax
import jax.numpy as jnp
from jax import lax
from jax.experimental import pallas as pl
from jax.experimental.pallas import tpu as pltpu

F32 = jnp.float32
BF16 = jnp.bfloat16
HI = lax.Precision.HIGHEST

D_MODEL = 2048
BATCH = 16
SEQ = 256
DEPTH = 2
DEC_BATCH = 2
DEC_SEQ = 1024
PAST_LEN = 512
GRID_W = 64
MIX_W = D_MODEL // 4
CONV_K = 3
NA_DH = 64
NA_HEADS = MIX_W // NA_DH
NA_WIN_R = 8
NA_WIN_C = 16
NA_SCALE = NA_DH ** -0.5
ROPE_THETA = 10000.0
RW_DH = 64
RW_HEADS = MIX_W // RW_DH
RW_W_RANK = 64
RW_A_RANK = 64
RW_G_RANK = 128
RW_DECAY_SCALE = 0.606531
RW_GN_EPS = 64e-5
ML_DH = 128
ML_HEADS = MIX_W // ML_DH
N_EXPERTS = 16
N_EXPERT_GROUPS = 4
D_EXPERT = 512
EPS = 1e-6
NEG_INF = -1e30

N_PROMPT = BATCH * SEQ
N_SAMPLE = DEC_BATCH * DEC_SEQ
N_TOK = N_PROMPT + N_SAMPLE
MOD_ROWS = 8
CHUNK = 64
SUB = 16
P1_CHUNKS = 4
NA_ROWS_PER_TRIP = 4
P_IN = 13 * MIX_W + RW_W_RANK + RW_A_RANK + RW_G_RANK + 4 * ML_HEADS
P_BLOCKS = 15
P_PAD = P_BLOCKS * MIX_W
(CB_CVB, CB_CVC, CB_CVH, CB_NAQ, CB_NAK, CB_NAV, CB_RWR, CB_RWK, CB_RWV,
 CB_MLQ, CB_MLK, CB_MLV, CB_MLO, CB_SMALL, CB_GATE) = range(P_BLOCKS)
SM_WL, SM_AL, SM_GL = 0, 64, 128
SM_GATE = MIX_W - 4 * ML_HEADS
VMEM_LIMIT = 56 * 1024 * 1024


def _mm(a, b, prec=None):
    return jnp.dot(a, b, precision=prec, preferred_element_type=F32)


def _mm_nt(a, b, prec=None):
    return lax.dot_general(a, b, (((1,), (1,)), ((), ())), precision=prec, preferred_element_type=F32)


def _sigmoid(x):
    return 1.0 / (1.0 + jnp.exp(-x))


def _full(shape):
    n = len(shape)
    return pl.BlockSpec(shape, lambda *_: (0,) * n)


def _params(sem):
    return pltpu.CompilerParams(dimension_semantics=sem, vmem_limit_bytes=VMEM_LIMIT)


def _mod_row(i, tm):
    n_prompt_tiles = N_PROMPT // tm
    tiles_per_sample = DEC_SEQ // tm
    return jnp.where(i < n_prompt_tiles, 0, 1 + (i - n_prompt_tiles) // tiles_per_sample)


def _mod_spec(layer, chunk, tm, tn=D_MODEL, with_j=False):
    if with_j:
        return pl.BlockSpec((None, None, None, 1, tn), lambda i, j: (layer, _mod_row(i, tm), chunk, 0, j))
    return pl.BlockSpec((None, None, None, 1, tn), lambda i, *_: (layer, _mod_row(i, tm), chunk, 0, 0))


def _tri_masks(n):
    t = lax.broadcasted_iota(jnp.int32, (n, n), 0)
    s = lax.broadcasted_iota(jnp.int32, (n, n), 1)
    incl = ((s <= t).astype(F32), (s >= t).astype(F32))
    strict = ((s < t).astype(F32), (s > t).astype(F32))
    return incl, strict, t, s


def _split2(x):
    hi = x.astype(BF16)
    return hi, (x - hi.astype(F32)).astype(BF16)


def _mm1(a, b):
    return _mm(a.astype(BF16), b.astype(BF16))


def _mm3(a, b):
    a_hi, a_lo = _split2(a)
    b_hi, b_lo = _split2(b)
    return _mm(a_hi, b_lo) + _mm(a_lo, b_hi) + _mm(a_hi, b_hi)


def _split3(x):
    x0 = x.astype(BF16)
    r1 = x - x0.astype(F32)
    x1 = r1.astype(BF16)
    return x0, x1, (r1 - x1.astype(F32)).astype(BF16)


def _mm_exact_lhs(mask_b, x):
    x0, x1, x2 = _split3(x)
    return _mm(mask_b, x2) + _mm(mask_b, x1) + _mm(mask_b, x0)


def _mm_exact_rhs(x, mask_b):
    x0, x1, x2 = _split3(x)
    return _mm(x2, mask_b) + _mm(x1, mask_b) + _mm(x0, mask_b)


def _adaln_kernel(cv_ref, w_ref, b_ref, o_ref):
    cv = cv_ref[...]
    o_ref[...] = _mm(cv * _sigmoid(cv), w_ref[...], HI) + b_ref[...]


def _adaln(cvecs, w_mod, b_mod):
    tn = 1024
    n_out = 6 * D_MODEL
    return pl.pallas_call(
        _adaln_kernel,
        out_shape=jax.ShapeDtypeStruct((DEPTH, MOD_ROWS, n_out), F32),
        grid=(DEPTH, n_out // tn),
        in_specs=[_full((MOD_ROWS, D_MODEL)),
                  pl.BlockSpec((None, D_MODEL, tn), lambda l, j: (l, 0, j)),
                  pl.BlockSpec((None, 1, tn), lambda l, j: (l, 0, j))],
        out_specs=pl.BlockSpec((None, MOD_ROWS, tn), lambda l, j: (l, 0, j)),
        compiler_params=_params(("parallel", "parallel")),
        name="adaln",
    )(cvecs, w_mod, b_mod.reshape(DEPTH, 1, n_out))


def _modulated(x, g, sh, sc):
    y = x * lax.rsqrt(jnp.mean(x * x, axis=-1, keepdims=True) + EPS) * g
    return y * (1.0 + sc) + sh


def _inproj_kernel(x_ref, g_ref, sh_ref, sc_ref, w_ref, o_ref, xm_ref):
    @pl.when(pl.program_id(1) == 0)
    def _():
        xm_ref[...] = _modulated(x_ref[...], g_ref[...], sh_ref[...], sc_ref[...]).astype(BF16)

    o_ref[...] = _mm_nt(xm_ref[...], w_ref[0].astype(BF16))


def _inproj_src_row(j):
    a = 9
    narrow = RW_W_RANK + RW_A_RANK + RW_G_RANK
    g = 16
    return g * jnp.where(j < a, j * (MIX_W // g),
                         jnp.where(j < CB_SMALL, j * (MIX_W // g) + narrow // g,
                                   jnp.where(j == CB_SMALL, a * MIX_W // g, (P_IN - MIX_W) // g)))


def _inproj(x, norm_g, mod, w_in_t, layer):
    tm, tn = 1024, MIX_W
    return pl.pallas_call(
        _inproj_kernel,
        out_shape=jax.ShapeDtypeStruct((N_TOK, P_PAD), F32),
        grid=(N_TOK // tm, P_BLOCKS),
        in_specs=[pl.BlockSpec((tm, D_MODEL), lambda i, j: (i, 0)),
                  pl.BlockSpec((None, 1, D_MODEL), lambda i, j: (layer, 0, 0)),
                  _mod_spec(layer, 0, tm), _mod_spec(layer, 1, tm),
                  pl.BlockSpec((pl.Element(1), pl.Element(tn), pl.Element(D_MODEL)),
                               lambda i, j: (layer, _inproj_src_row(j), 0))],
        out_specs=pl.BlockSpec((tm, tn), lambda i, j: (i, j)),
        scratch_shapes=[pltpu.VMEM((tm, D_MODEL), BF16)],
        compiler_params=_params(("parallel", "arbitrary")),
        name="inproj",
    )(x, norm_g.reshape(DEPTH, 1, D_MODEL), mod, mod, w_in_t)


def _conv_mix(b, c, h, w):
    u = c * h
    n = u.shape[0]
    row = lax.broadcasted_iota(jnp.int32, u.shape, 0)
    prev = jnp.where(row == 0, 0.0, pltpu.roll(u, 1, axis=0))
    nxt = jnp.where(row == n - 1, 0.0, pltpu.roll(u, n - 1, axis=0))
    return b * (prev * w[0:1] + u * w[1:2] + nxt * w[2:3])


def _head_rms(x, g):
    return x * lax.rsqrt(jnp.mean(x * x, axis=-1, keepdims=True) + EPS) * g


def _attn_prompt_kernel(cb_ref, cc_ref, ch_ref, q_ref, k_ref, v_ref, cw_ref, qg_ref, kg_ref,
                        ya_ref, yb_ref, nk_ref, nv_ref):
    ya_ref[...] = _conv_mix(cb_ref[...], cc_ref[...], ch_ref[...], cw_ref[...]).astype(ya_ref.dtype)
    sls = [slice(h * NA_DH, (h + 1) * NA_DH) for h in range(NA_HEADS)]
    qn = [_head_rms(q_ref[:, sl], qg_ref[...]) * NA_SCALE for sl in sls]
    kn = [_head_rms(k_ref[:, sl], kg_ref[...]) for sl in sls]
    vh = [v_ref[:, sl] for sl in sls]
    s = [_mm_nt(q.astype(BF16), k.astype(BF16)) for q, k in zip(qn, kn)]
    p = [jnp.exp(x - jnp.max(x, axis=-1, keepdims=True)) for x in s]
    o = [_mm(x.astype(BF16), v.astype(BF16)) / jnp.sum(x, axis=-1, keepdims=True) for x, v in zip(p, vh)]
    yb_ref[...] = jnp.concatenate(o, axis=1).astype(yb_ref.dtype)
    for h in range(NA_HEADS):
        nk_ref[h] = kn[h]
        nv_ref[h] = vh[h]


def _u_spec(rows, col_block, row_off_blocks=0):
    return pl.BlockSpec((rows, MIX_W), lambda b: (b + row_off_blocks, col_block))


def _carry_through(kernel, n_inputs, prev, n_plain_out, n_state_out, layer):
    n_prev = len(prev)

    def body(*refs):
        ins, rest = refs[:n_inputs], list(refs[n_inputs + n_prev:])
        if not n_prev:
            for i in range(n_plain_out, n_plain_out + n_state_out):
                full = rest[i]
                for d in range(DEPTH):
                    if d != layer:
                        full[d] = jnp.zeros(full.shape[1:], full.dtype)
                rest[i] = full.at[layer]
        return kernel(*ins, *rest)

    return body, [pl.BlockSpec(memory_space=pl.ANY)] * n_prev


def _state_spec(shape, layer, first, index):
    def index_map(*g):
        b, *tail = index(*g)
        return (b, 0 if first else layer, *tail)
    return pl.BlockSpec((None, DEPTH if first else None) + tuple(shape), index_map)


def _attn_prompt(u, conv_w, q_g, k_g, layer, prev=(), n_seq=BATCH, seq=SEQ):
    lw = lambda shape: pl.BlockSpec((None,) + shape, lambda b: (layer,) + (0,) * len(shape))
    y_spec = pl.BlockSpec((seq, MIX_W), lambda b: (b, 0))
    kv_spec = _state_spec((NA_HEADS, seq, NA_DH), layer, not prev, lambda b: (b, 0, 0, 0))
    in_specs = ([_u_spec(seq, cb) for cb in (CB_CVB, CB_CVC, CB_CVH, CB_NAQ, CB_NAK, CB_NAV)]
                + [lw((CONV_K, MIX_W)), lw((1, NA_DH)), lw((1, NA_DH))])
    body, prev_specs = _carry_through(_attn_prompt_kernel, len(in_specs), prev, 2, 2, layer)
    return pl.pallas_call(
        body,
        out_shape=(jax.ShapeDtypeStruct((n_seq * seq, MIX_W), BF16),) * 2
        + (jax.ShapeDtypeStruct((n_seq, DEPTH, NA_HEADS, seq, NA_DH), F32),) * 2,
        grid=(n_seq,),
        in_specs=in_specs + prev_specs,
        out_specs=(y_spec, y_spec, kv_spec, kv_spec),
        input_output_aliases={len(in_specs) + i: 2 + i for i in range(len(prev))},
        compiler_params=_params(("parallel",)),
        name="attn_prompt",
    )(u, u, u, u, u, u, conv_w, q_g.reshape(DEPTH, 1, NA_DH), k_g.reshape(DEPTH, 1, NA_DH), *prev)


def _na_kernel(cb_ref, cc_ref, ch_ref, q_ref, k_ref, v_ref, kc_ref, vc_ref, tw_ref, cos_ref, sin_ref,
               perm_ref, cw_ref, qg_ref, kg_ref, ya_ref, yb_ref, qs_ref, ks_ref, tws_ref):
    rows = DEC_SEQ // GRID_W
    wr = min(NA_WIN_R, rows)
    nw = wr * GRID_W
    ya_ref[...] = _conv_mix(cb_ref[...], cc_ref[...], ch_ref[...], cw_ref[...]).astype(ya_ref.dtype)
    cos, sin, perm_b = cos_ref[...], sin_ref[...], perm_ref[...].astype(BF16)

    def rope(x):
        hi, lo = _split2(x)
        return x * cos + (_mm(hi, perm_b) + _mm(lo, perm_b)) * sin

    for h in range(NA_HEADS):
        sl = slice(h * NA_DH, (h + 1) * NA_DH)
        qs_ref[...] = (rope(_head_rms(q_ref[:, sl], qg_ref[...])) * NA_SCALE).astype(BF16)
        ks_ref[...] = rope(_head_rms(k_ref[:, sl], kg_ref[...])).astype(BF16)
        kch = kc_ref[h].astype(BF16)
        vch = vc_ref[h].astype(BF16)
        for p in range(wr):
            tws_ref[p] = jnp.concatenate([tw_ref[h, j - p + NA_WIN_R - 1] for j in range(wr)], axis=1)

        def rows_step(t, carry):
            rr = [t * NA_ROWS_PER_TRIP + i for i in range(NA_ROWS_PER_TRIP)]
            rs = [jnp.clip(r - wr // 2, 0, rows - wr) for r in rr]
            q0 = [pl.multiple_of(r * GRID_W, GRID_W) for r in rr]
            k0 = [pl.multiple_of(x * GRID_W, GRID_W) for x in rs]
            q_r = [qs_ref[pl.ds(x, GRID_W), :] for x in q0]
            s_w = [_mm_nt(q, ks_ref[pl.ds(k, nw), :]) + tws_ref[r - x] for q, k, r, x in zip(q_r, k0, rr, rs)]
            s_c = [_mm_nt(q, kch) for q in q_r]
            m = [jnp.maximum(jnp.max(a, axis=-1, keepdims=True), jnp.max(b, axis=-1, keepdims=True))
                 for a, b in zip(s_w, s_c)]
            p_w = [jnp.exp(a - x) for a, x in zip(s_w, m)]
            p_c = [jnp.exp(b - x) for b, x in zip(s_c, m)]
            den = [jnp.sum(a, axis=-1, keepdims=True) + jnp.sum(b, axis=-1, keepdims=True) for a, b in zip(p_w, p_c)]
            v_w = [v_ref[pl.ds(k, nw), sl].astype(BF16) for k in k0]
            o = [(_mm(a.astype(BF16), v) + _mm(b.astype(BF16), vch)) / d for a, b, v, d in zip(p_w, p_c, v_w, den)]
            for x, val in zip(q0, o):
                yb_ref[pl.ds(x, GRID_W), sl] = val.astype(yb_ref.dtype)
            return carry

        lax.fori_loop(0, rows // NA_ROWS_PER_TRIP, rows_step, 0)


def _na_tables(rpb):
    qc = np.arange(GRID_W)
    kc = np.arange(GRID_W)
    wstart = np.clip(qc - NA_WIN_C // 2, 0, GRID_W - NA_WIN_C)
    colmask = (kc[None, :] >= wstart[:, None]) & (kc[None, :] < wstart[:, None] + NA_WIN_C)
    dc = np.clip(kc[None, :] - qc[:, None], -(NA_WIN_C - 1), NA_WIN_C - 1) + NA_WIN_C - 1
    pick = (np.arange(2 * NA_WIN_C - 1)[:, None] == dc.reshape(1, -1)).astype(np.float32)
    bias = jnp.einsum('lhrd,dn->lhrn', rpb, jnp.asarray(pick), precision=HI)
    return jnp.where(colmask[None, None, None], bias.reshape(rpb.shape[:3] + dc.shape), NEG_INF)


def _rope_tables():
    t = np.arange(DEC_SEQ)
    quarter = NA_DH // 4
    freq = ROPE_THETA ** (-np.arange(quarter, dtype=np.float32) / quarter)
    ang_r = (t // GRID_W).astype(np.float32)[:, None] * freq
    ang_c = (t % GRID_W).astype(np.float32)[:, None] * freq
    cos = np.concatenate([np.cos(ang_r), np.cos(ang_r), np.cos(ang_c), np.cos(ang_c)], axis=-1)
    sin = np.concatenate([-np.sin(ang_r), np.sin(ang_r), -np.sin(ang_c), np.sin(ang_c)], axis=-1)
    src = np.concatenate([np.arange(quarter) + quarter, np.arange(quarter),
                          np.arange(quarter) + 3 * quarter, np.arange(quarter) + 2 * quarter])
    perm = np.zeros((NA_DH, NA_DH), np.float32)
    perm[src, np.arange(NA_DH)] = 1.0
    return cos.astype(np.float32), sin.astype(np.float32), perm


def _na_sample(u, cache_k, cache_v, tw, conv_w, q_g, k_g, layer, n_seq=DEC_BATCH, row_off=N_PROMPT // DEC_SEQ):
    cos, sin, perm = _rope_tables()
    lw = lambda shape: pl.BlockSpec((None,) + shape, lambda b: (layer,) + (0,) * len(shape))
    y_spec = pl.BlockSpec((DEC_SEQ, MIX_W), lambda b: (b, 0))
    c_spec = pl.BlockSpec((None, None, NA_HEADS, PAST_LEN, NA_DH), lambda b: (b, layer, 0, 0, 0))
    wr = min(NA_WIN_R, DEC_SEQ // GRID_W)
    n_off = 2 * NA_WIN_R - 1
    return pl.pallas_call(
        _na_kernel,
        out_shape=(jax.ShapeDtypeStruct((n_seq * DEC_SEQ, MIX_W), BF16),) * 2,
        grid=(n_seq,),
        in_specs=[_u_spec(DEC_SEQ, cb, row_off) for cb in (CB_CVB, CB_CVC, CB_CVH, CB_NAQ, CB_NAK, CB_NAV)]
        + [c_spec, c_spec, lw((NA_HEADS, n_off, GRID_W, GRID_W)),
           _full((DEC_SEQ, NA_DH)), _full((DEC_SEQ, NA_DH)), _full((NA_DH, NA_DH)),
           lw((CONV_K, MIX_W)), lw((1, NA_DH)), lw((1, NA_DH))],
        out_specs=(y_spec, y_spec),
        scratch_shapes=[pltpu.VMEM((DEC_SEQ, NA_DH), BF16), pltpu.VMEM((DEC_SEQ, NA_DH), BF16),
                        pltpu.VMEM((wr, GRID_W, wr * GRID_W), F32)],
        compiler_params=_params(("parallel",)),
        name="na_sample",
    )(u, u, u, u, u, u, cache_k, cache_v, tw, jnp.asarray(cos), jnp.asarray(sin), jnp.asarray(perm),
      conv_w, q_g.reshape(DEPTH, 1, NA_DH), k_g.reshape(DEPTH, 1, NA_DH))


def _seg_ones(width, seg):
    a = lax.broadcasted_iota(jnp.int32, (width, width), 0) // seg
    b = lax.broadcasted_iota(jnp.int32, (width, width), 1) // seg
    return (a == b).astype(F32)


def _rwkv_kernel(*refs, seq, has_s0, emit_state):
    it = iter(refs)
    r_ref, k_ref, v_ref, sm_ref = (next(it) for _ in range(4))
    (w0_ref, wup_ref, a0_ref, aup_ref, gup_ref, kkp_ref, kap_ref, rkp_ref, lng_ref, lnb_ref) = (
        next(it) for _ in range(10))
    mask_ref = next(it)
    s0_ref = next(it) if has_s0 else None
    y_ref = next(it)
    so_ref = next(it) if emit_state else None
    kk_ref, lw_ref, ka_ref, kd_ref, coef_ref, ysp_ref = (next(it) for _ in range(6))

    dh = RW_DH
    pw = 2 * dh
    nc = seq // CHUNK
    seg_b = _seg_ones(pw, dh).astype(BF16)

    r = r_ref[...]
    k = k_ref[...]
    sm = sm_ref[...]
    wl = jnp.tanh(sm[:, SM_WL:SM_WL + RW_W_RANK])
    al = sm[:, SM_AL:SM_AL + RW_A_RANK]
    kk = k * kkp_ref[...]
    kk = kk * lax.rsqrt(_mm_exact_rhs(kk * kk, seg_b) + EPS)
    kk_ref[...] = kk
    for z in range(2):
        lw_ref[z] = -RW_DECAY_SCALE * _sigmoid(w0_ref[z:z + 1, :] + _mm3(wl, wup_ref[z]))
        a = _sigmoid(a0_ref[z:z + 1, :] + _mm3(al, aup_ref[z]))
        ka_ref[z] = kk * a
        kd_ref[z] = k * (1.0 + (a - 1.0) * kap_ref[...])

    cat = jnp.concatenate

    def phase1(c2, carry):
        incl, strict, ti, si = _tri_masks(CHUNK)
        incl_b = (mask_ref[0], mask_ref[1])
        strict2 = tuple(cat([m, m], axis=1) for m in strict)
        incl2 = tuple(cat([m, m], axis=1) for m in incl)
        diag_blk = (ti // SUB == si // SUB).astype(F32)
        eye = (ti == si).astype(F32)
        zero_pair = jnp.zeros((CHUNK, pw), F32)
        head_of_lane = lax.broadcasted_iota(jnp.int32, (CHUNK, pw), 1) // dh
        head_of_lane2 = (lax.broadcasted_iota(jnp.int32, (CHUNK, 2 * pw), 1) // dh) % 2
        r_pw = lax.broadcasted_iota(jnp.int32, (pw, pw), 0)
        c_pw = lax.broadcasted_iota(jnp.int32, (pw, pw), 1)
        same_head = r_pw // dh == c_pw // dh
        eye_pw = (r_pw == c_pw).astype(F32)
        chains = []
        for cc in range(P1_CHUNKS):
            rows = pl.ds(pl.multiple_of((c2 * P1_CHUNKS + cc) * CHUNK, CHUNK), CHUNK)
            vc, rc, kkc = v_ref[rows, :], r_ref[rows, :], kk_ref[rows, :]
            for z in range(2):
                lwc = lw_ref[z, rows, :]
                cum = _mm_exact_lhs(incl_b[z], lwc)
                tot = cum[CHUNK - 1:CHUNK] if z == 0 else cum[0:1]
                e_neg = jnp.exp(-cum)
                dec = jnp.exp(tot - cum)
                kac, kdc = ka_ref[z, rows, :], kd_ref[z, rows, :]
                chains.append((z, kkc * jnp.exp(cum - lwc), rc * jnp.exp(cum), vc, kac * e_neg, kdc * e_neg,
                               kac * dec, kdc * dec, jnp.exp(tot)))
        heads = [(q, j) for q in range(len(chains)) for j in range(2)]
        zq = [ch[0] for ch in chains]
        ymat = [cat([ch[4], ch[5]], axis=0).astype(BF16) for ch in chains]
        vz = [cat([zero_pair, ch[3]], axis=0) for ch in chains]
        nvz = [cat([zero_pair, -ch[3]], axis=1) for ch in chains]
        x_in = [cat([jnp.where(head_of_lane == j, chains[q][1], 0.0),
                     jnp.where(head_of_lane == j, chains[q][2], 0.0)], axis=0) for q, j in heads]
        aa = [_mm_nt(x.astype(BF16), ymat[q]) for x, (q, _) in zip(x_in, heads)]
        top = [a[0:CHUNK] * strict2[zq[q]] for a, (q, _) in zip(aa, heads)]
        a_r = [a[CHUNK:] * incl2[zq[q]] for a, (q, _) in zip(aa, heads)]
        akv = [_mm1(t, vz[q]) for t, (q, _) in zip(top, heads)]
        low = [t[:, 0:CHUNK] for t in top]
        ld = [x * diag_blk for x in low]
        lo = [x - y for x, y in zip(low, ld)]
        l2 = [_mm1(x, x) for x in ld]
        l4 = [_mm1(x, x) for x in l2]
        l8 = [_mm1(x, x) for x in l4]
        td = [eye - x for x in ld]
        for lp in (l2, l4, l8):
            td = [t + _mm1(t, p_) for t, p_ in zip(td, lp)]
        x0 = [_mm1(t, cat([chains[q][1], a, l], axis=1)) for t, a, l, (q, _) in zip(td, akv, lo, heads)]
        pq0 = [x[:, 0:2 * pw] for x in x0]
        wm = [x[:, 2 * pw:] for x in x0]
        pq = pq0
        for _ in range(CHUNK // SUB - 1):
            pq = [p0 - _mm1(w, p_) for p0, w, p_ in zip(pq0, wm, pq)]
        ryc = [_mm1(a, cat([p_, nvz[q]], axis=0)) for a, p_, (q, _) in zip(a_r, pq, heads)]
        for q, ch in enumerate(chains):
            pq_m = jnp.where(head_of_lane2 == 0, pq[2 * q], pq[2 * q + 1])
            ryc_m = jnp.where(head_of_lane2 == 0, ryc[2 * q], ryc[2 * q + 1])
            g1 = _mm1(ch[6].T, pq_m)
            g2 = _mm1(ch[7].T, ch[3])
            g_t = eye_pw * ch[8] - jnp.where(same_head, g1[:, 0:pw], 0.0)
            h_t = jnp.where(same_head, g2 - g1[:, pw:], 0.0)
            coef_ref[c2 * P1_CHUNKS + q // 2, q % 2] = cat(
                [g_t, ch[2] - ryc_m[:, 0:pw], h_t, -ryc_m[:, pw:]], axis=0)
        return carry

    lax.fori_loop(0, nc // P1_CHUNKS, phase1, 0)

    def block_diag(a, b):
        zero = jnp.zeros((dh, dh), F32)
        return cat([cat([a, zero], axis=1), cat([zero, b], axis=1)], axis=0)

    m_init = tuple((block_diag(s0_ref[z, 0].T, s0_ref[z, 1].T) if has_s0 else jnp.zeros((pw, pw), F32))
                   for z in range(2))
    n_lhs = pw + CHUNK

    def phase2(ci, ms):
        new_ms, ys = [], []
        for z in range(2):
            c = ci if z == 0 else nc - 1 - ci
            out = _mm3(coef_ref[c, z, 0:n_lhs, :], ms[z]) + coef_ref[c, z, n_lhs:, :]
            new_ms.append(out[0:pw])
            ys.append(out[pw:])
        ysp_ref[ci] = cat(ys, axis=1)
        return tuple(new_ms)

    m_fin = lax.fori_loop(0, nc, phase2, m_init)

    if emit_state:
        for z in range(2):
            for j in range(2):
                so_ref[z, j] = m_fin[z][j * dh:(j + 1) * dh, j * dh:(j + 1) * dh].T

    y = cat([ysp_ref[c, :, 0:pw] + ysp_ref[nc - 1 - c, :, pw:] for c in range(nc)], axis=0)
    mu = _mm_exact_rhs(y, seg_b) * (1.0 / dh)
    yc = y - mu
    var = _mm_exact_rhs(yc * yc, seg_b) * (1.0 / dh)
    yn = yc * lax.rsqrt(var + RW_GN_EPS) * lng_ref[...] + lnb_ref[...]
    bonus = _mm_exact_rhs(r * k * rkp_ref[...], seg_b) * v_ref[...]
    g = _mm3(_sigmoid(sm[:, SM_GL:SM_GL + RW_G_RANK]), gup_ref[...])
    y_ref[...] = ((yn + bonus) * g).astype(y_ref.dtype)


def _rwkv(u, p, layer, n_seq, seq, row_off, s0=None, emit_state=False, prev=()):
    n_pairs = RW_HEADS // 2
    pw = 2 * RW_DH
    bpc = MIX_W // pw
    lw = lambda shape: pl.BlockSpec((None,) + shape, lambda b, hp: (layer,) + (0,) * (len(shape) - 1) + (hp,))
    row = lambda a: a.reshape(DEPTH, 1, MIX_W)
    u_pair = lambda cb: pl.BlockSpec((seq, pw), lambda b, hp: (b + row_off, cb * bpc + hp))
    in_specs = [u_pair(CB_RWR), u_pair(CB_RWK), u_pair(CB_RWV),
                pl.BlockSpec((seq, MIX_W), lambda b, hp: (b + row_off, CB_SMALL)),
                lw((2, pw)), lw((2, RW_W_RANK, pw)), lw((2, pw)), lw((2, RW_A_RANK, pw)),
                lw((RW_G_RANK, pw))] + [lw((1, pw))] * 5 + [_full((2, CHUNK, CHUNK))]
    t_idx = np.arange(CHUNK)
    incl_masks = np.stack([t_idx[None, :] <= t_idx[:, None], t_idx[None, :] >= t_idx[:, None]])
    args = [u, u, u, u, p['rw_w0'], p['rw_w_up'], p['rw_a0'], p['rw_a_up'], p['rw_g_up'],
            row(p['rw_k_k']), row(p['rw_k_a']), row(p['rw_r_k']), row(p['rw_ln_g']), row(p['rw_ln_b']),
            jnp.asarray(incl_masks, BF16)]
    st_blk = (2, 2, RW_DH, RW_DH)
    if s0 is not None:
        in_specs.append(pl.BlockSpec((None, None) + st_blk, lambda b, hp: (b, layer, 0, hp, 0, 0)))
        args.append(s0)
    out_shape = [jax.ShapeDtypeStruct((n_seq * seq, MIX_W), BF16)]
    out_specs = [pl.BlockSpec((seq, pw), lambda b, hp: (b, hp))]
    if emit_state:
        out_shape.append(jax.ShapeDtypeStruct((n_seq, DEPTH, 2, RW_HEADS, RW_DH, RW_DH), F32))
        out_specs.append(_state_spec(st_blk, layer, not prev, lambda b, hp: (b, 0, hp, 0, 0)))
    nc = seq // CHUNK
    tok = lambda n: pltpu.VMEM((n, seq, pw) if n else (seq, pw), F32)
    body, prev_specs = _carry_through(
        functools.partial(_rwkv_kernel, seq=seq, has_s0=s0 is not None, emit_state=emit_state), len(in_specs), prev,
        1, 1 if emit_state else 0, layer)
    return pl.pallas_call(
        body,
        out_shape=tuple(out_shape), grid=(n_seq, n_pairs), in_specs=in_specs + prev_specs,
        out_specs=tuple(out_specs),
        input_output_aliases={len(in_specs) + i: 1 + i for i in range(len(prev))},
        scratch_shapes=[tok(0), tok(2), tok(2), tok(2),
                        pltpu.VMEM((nc, 2, 2 * (pw + CHUNK), pw), F32), pltpu.VMEM((nc, CHUNK, 2 * pw), F32)],
        compiler_params=_params(("parallel", "parallel")),
        name=f"rwkv_{seq}",
    )(*args, *prev)


def _log_sigmoid(x):
    return jnp.minimum(x, 0.0) - jnp.log(1.0 + jnp.exp(-jnp.abs(x)))


def _mlstm_kernel(*refs, seq, has_s0, emit_state):
    it = iter(refs)
    q_ref, k_ref, v_ref, o_ref, sm_ref, gr_ref, bc_ref, br_ref, ng_ref = (next(it) for _ in range(9))
    c0_ref, n0_ref, m0_ref = ((next(it), next(it), next(it)) if has_s0 else (None, None, None))
    y_ref = next(it)
    co_ref, no_ref, mo_ref = ((next(it), next(it), next(it)) if emit_state else (None, None, None))
    hs_ref, c_ref, n_ref, m_ref = (next(it) for _ in range(4))

    nc = seq // CHUNK
    n_st = 2 * ML_HEADS
    for i in range(n_st):
        z, h = divmod(i, ML_HEADS)
        c_ref[i] = c0_ref[z, h] if has_s0 else jnp.zeros((ML_DH, ML_DH), F32)
        n_ref[i] = n0_ref[z, h] if has_s0 else jnp.zeros((1, ML_DH), F32)
        m_ref[i] = m0_ref[z, h] if has_s0 else jnp.zeros((1, 1), F32)

    incl, _, ti, si = _tri_masks(CHUNK)
    before = ((si <= ti), (si >= ti))

    nh = ML_HEADS
    chains = [(z, h) for z in range(2) for h in range(nh)]
    each = lambda f, *xs: [f(*a) for a in zip(*xs)]
    zs = [z for z, _ in chains]

    def chunk_step(ci, carry):
        per_dir = []
        for z in range(2):
            c = ci if z == 0 else nc - 1 - ci
            rows = pl.ds(pl.multiple_of(c * CHUNK, CHUNK), CHUNK)
            g0 = SM_GATE + z * 2 * nh
            gc = sm_ref[rows, g0:g0 + 2 * nh] + bc_ref[:, z * 2 * nh:(z + 1) * 2 * nh]
            gr = gr_ref[c, z * 2 * nh:(z + 1) * 2 * nh, :] + br_ref[z * 2 * nh:(z + 1) * 2 * nh, :]
            b_cols = _mm(incl[z], _log_sigmoid(gc[:, nh:]), HI)
            b_rows = _mm(_log_sigmoid(gr[nh:]), incl[1 - z], HI)
            per_dir.append((rows, gc[:, :nh], b_cols, gr[:nh], b_rows))
        rows = [per_dir[z][0] for z, _ in chains]
        hsl = [slice(h * ML_DH, (h + 1) * ML_DH) for _, h in chains]
        i_col = [per_dir[z][1][:, h:h + 1] for z, h in chains]
        b_col = [per_dir[z][2][:, h:h + 1] for z, h in chains]
        i_row = [per_dir[z][3][h:h + 1] for z, h in chains]
        b_row = [per_dir[z][4][h:h + 1] for z, h in chains]
        b_last = each(lambda b, z: b[CHUNK - 1:CHUNK] if z == 0 else b[0:1], b_col, zs)
        m_old = [m_ref[i] for i in range(n_st)]
        cm = [c_ref[i] for i in range(n_st)]
        nv = [n_ref[i] for i in range(n_st)]
        qc = each(lambda r, s: q_ref[r, s] * (ML_DH ** -0.5), rows, hsl)
        kc = each(lambda r, s: k_ref[r, s], rows, hsl)
        vc = each(lambda r, s: v_ref[r, s], rows, hsl)
        qk = each(lambda q, k: _mm_nt(q.astype(BF16), k.astype(BF16)), qc, kc)
        qcm = each(lambda q, c_: _mm_nt(q.astype(BF16), c_.astype(BF16)), qc, cm)
        a_t = each(lambda b, m: b + m, b_col, m_old)
        dmat = each(lambda bc, brw, ir, z: jnp.where(before[z], bc - brw + ir, NEG_INF), b_col, b_row, i_row, zs)
        m_t = each(lambda a, d: jnp.maximum(a, jnp.max(d, axis=-1, keepdims=True)), a_t, dmat)
        s = each(lambda x, d, m: x * jnp.exp(d - m), qk, dmat, m_t)
        inter = each(lambda a, m: jnp.exp(a - m), a_t, m_t)
        sv = each(_mm1, s, vc)
        g_col = each(lambda bl, bc, ic: bl - bc + ic, b_last, b_col, i_col)
        a_l = each(lambda bl, m: bl + m, b_last, m_old)
        m_new = each(lambda a, g: jnp.maximum(a, jnp.max(g, axis=0, keepdims=True)), a_l, g_col)
        wgt = each(lambda g, m: jnp.exp(g - m), g_col, m_new)
        vk = each(lambda v, w, k: _mm1((v * w).T, k), vc, wgt, kc)
        decay = each(lambda a, m: jnp.exp(a - m), a_l, m_new)
        num = each(lambda i_, x, y: i_ * x + y, inter, qcm, sv)
        den = each(lambda i_, q, n_, s_: i_ * jnp.sum(q * n_, axis=-1, keepdims=True)
                   + jnp.sum(s_, axis=-1, keepdims=True), inter, qc, nv, s)
        hh = each(lambda n_, d, m: n_ / jnp.maximum(jnp.abs(d), jnp.exp(-m)), num, den, m_t)
        for z in range(2):
            hs_ref[z, per_dir[z][0], :] = jnp.concatenate(hh[z * nh:(z + 1) * nh], axis=1)
        for i in range(n_st):
            c_ref[i] = decay[i] * cm[i] + vk[i]
            n_ref[i] = decay[i] * nv[i] + jnp.sum(wgt[i] * kc[i], axis=0, keepdims=True)
            m_ref[i] = m_new[i]
        return carry

    lax.fori_loop(0, nc, chunk_step, 0)

    if emit_state:
        for i in range(n_st):
            z, h = divmod(i, ML_HEADS)
            co_ref[z, h] = c_ref[i]
            no_ref[z, h] = n_ref[i]
            mo_ref[z, h] = m_ref[i]

    for h in range(ML_HEADS):
        hsl = slice(h * ML_DH, (h + 1) * ML_DH)
        hn = _head_rms(hs_ref[0, :, hsl] + hs_ref[1, :, hsl], ng_ref[:, hsl])
        y_ref[:, hsl] = (_sigmoid(o_ref[:, hsl]) * hn).astype(y_ref.dtype)


def _mlstm(u, gate_rows, p, layer, n_seq, seq, row_off, state=None, emit_state=False, prev=()):
    lw = lambda shape: pl.BlockSpec((None,) + shape, lambda b: (layer,) + (0,) * len(shape))
    nc = seq // CHUNK
    n_gate = 4 * ML_HEADS
    in_specs = [_u_spec(seq, cb, row_off) for cb in (CB_MLQ, CB_MLK, CB_MLV, CB_MLO, CB_GATE)] + [
        pl.BlockSpec((None, nc, n_gate, CHUNK), lambda b: (b, 0, 0, 0)),
        lw((1, n_gate)), lw((n_gate, 1)), lw((1, MIX_W))]
    args = [u, u, u, u, u, gate_rows, p['ml_gate_b'].reshape(DEPTH, 1, n_gate),
            p['ml_gate_b'].reshape(DEPTH, n_gate, 1), p['ml_norm_g'].reshape(DEPTH, 1, MIX_W)]
    c_shape, n_shape, m_shape = (2, ML_HEADS, ML_DH, ML_DH), (2, ML_HEADS, 1, ML_DH), (2, ML_HEADS, 1, 1)
    if state is not None:
        c0, n0, m0 = state
        for a, shp in ((c0, c_shape), (n0, n_shape), (m0, m_shape)):
            in_specs.append(pl.BlockSpec((None, None) + shp, lambda b: (b, layer, 0, 0, 0, 0)))
            args.append(a.reshape(a.shape[:2] + shp))
    out_shape = [jax.ShapeDtypeStruct((n_seq * seq, MIX_W), BF16)]
    out_specs = [pl.BlockSpec((seq, MIX_W), lambda b: (b, 0))]
    if emit_state:
        for shp in (c_shape, n_shape, m_shape):
            out_shape.append(jax.ShapeDtypeStruct((n_seq, DEPTH) + shp, F32))
            out_specs.append(_state_spec(shp, layer, not prev, lambda b: (b, 0, 0, 0, 0)))
    n_st = 2 * ML_HEADS
    body, prev_specs = _carry_through(
        functools.partial(_mlstm_kernel, seq=seq, has_s0=state is not None, emit_state=emit_state),
        len(in_specs), prev, 1, 3 if emit_state else 0, layer)
    return pl.pallas_call(
        body,
        out_shape=tuple(out_shape), grid=(n_seq,), in_specs=in_specs + prev_specs, out_specs=tuple(out_specs),
        input_output_aliases={len(in_specs) + i: 1 + i for i in range(len(prev))},
        scratch_shapes=[pltpu.VMEM((2, seq, MIX_W), F32), pltpu.VMEM((n_st, ML_DH, ML_DH), F32),
                        pltpu.VMEM((n_st, 1, ML_DH), F32), pltpu.VMEM((n_st, 1, 1), F32)],
        compiler_params=_params(("parallel",)),
        name=f"mlstm_{seq}",
    )(*args, *prev)


def _outproj_kernel(*refs, n_prompt_tiles):
    n_mix = 4
    yp_refs, ys_refs = refs[:n_mix], refs[n_mix:2 * n_mix]
    w_ref, x_ref, g_ref, o_ref = refs[2 * n_mix:]
    is_prompt = pl.program_id(0) < n_prompt_tiles
    acc = None
    for i, (yp_ref, ys_ref) in enumerate(zip(yp_refs, ys_refs)):
        y = jnp.where(is_prompt, yp_ref[...], ys_ref[...])
        part = _mm(y, w_ref[i * MIX_W:(i + 1) * MIX_W, :].astype(BF16))
        acc = part if acc is None else acc + part
    o_ref[...] = x_ref[...] + g_ref[...] * acc


def _outproj(ys_prompt, ys_sample, w_out, x, mod, layer):
    tm, tn = 1024, 1024
    n_prompt_tiles = N_PROMPT // tm
    yp_spec = pl.BlockSpec((tm, MIX_W), lambda i, j: (jnp.minimum(i, n_prompt_tiles - 1), 0))
    ys_spec = pl.BlockSpec((tm, MIX_W), lambda i, j: (jnp.maximum(i - n_prompt_tiles, 0), 0))
    return pl.pallas_call(
        functools.partial(_outproj_kernel, n_prompt_tiles=n_prompt_tiles),
        out_shape=jax.ShapeDtypeStruct((N_TOK, D_MODEL), F32),
        grid=(N_TOK // tm, D_MODEL // tn),
        in_specs=[yp_spec] * 4 + [ys_spec] * 4 + [pl.BlockSpec((None, D_MODEL, tn), lambda i, j: (layer, 0, j)),
                                                   pl.BlockSpec((tm, tn), lambda i, j: (i, j)),
                                                   _mod_spec(layer, 2, tm, tn, with_j=True)],
        out_specs=pl.BlockSpec((tm, tn), lambda i, j: (i, j)),
        compiler_params=_params(("parallel", "parallel")),
        name="outproj",
    )(*ys_prompt, *ys_sample, w_out, x, mod)


def _router_kernel(x_ref, g_ref, sh_ref, sc_ref, rw_ref, rb_ref, xm_ref, gt_ref, ids_ref):
    xm = _modulated(x_ref[...], g_ref[...], sh_ref[...], sc_ref[...])
    xm_ref[...] = xm
    logits = _mm_nt(rw_ref[...], xm, HI)
    ex = jnp.exp(logits - jnp.max(logits, axis=0, keepdims=True))
    scores = ex / jnp.sum(ex, axis=0, keepdims=True)
    sel = scores + rb_ref[...]
    per = N_EXPERTS // N_EXPERT_GROUPS
    s = [sel[e:e + 1, :] for e in range(N_EXPERTS)]
    grp_score = []
    for g in range(N_EXPERT_GROUPS):
        a, b, c, d = s[per * g:per * (g + 1)]
        hi1, lo1, hi2, lo2 = jnp.maximum(a, b), jnp.minimum(a, b), jnp.maximum(c, d), jnp.minimum(c, d)
        grp_score.append(jnp.maximum(hi1, hi2) + jnp.maximum(jnp.minimum(hi1, hi2), jnp.maximum(lo1, lo2)))
    best = functools.reduce(jnp.maximum, grp_score)
    in_grp, taken = [], jnp.zeros_like(best)
    for g in range(N_EXPERT_GROUPS):
        hit = jnp.where(grp_score[g] == best, 1.0, 0.0) * (1.0 - taken)
        in_grp.append(hit)
        taken = taken + hit
    picked, flag = [], []
    for e in range(N_EXPERTS):
        g = e // per
        rank = jnp.zeros_like(best)
        for o in range(per * g, per * (g + 1)):
            if o < e:
                rank += jnp.where(s[o] >= s[e], 1.0, 0.0)
            elif o > e:
                rank += jnp.where(s[o] > s[e], 1.0, 0.0)
        flag.append(in_grp[g] * jnp.where(rank < 2.0, 1.0, 0.0))
        picked.append(flag[e] * scores[e:e + 1, :])
    total = functools.reduce(lambda x, y: x + y, picked)
    for e in range(N_EXPERTS):
        gt_ref[e:e + 1, :] = picked[e] / total
    lo_id = functools.reduce(
        jnp.minimum, [jnp.where(flag[e] > 0.0, float(e), float(N_EXPERTS)) for e in range(N_EXPERTS)])
    hi_id = functools.reduce(jnp.maximum, [jnp.where(flag[e] > 0.0, float(e), -1.0) for e in range(N_EXPERTS)])
    ids_ref[0:1, :] = lo_id
    ids_ref[1:2, :] = hi_id
    ids_ref[2:, :] = jnp.zeros((ids_ref.shape[0] - 2,) + lo_id.shape[1:], F32)


def _router(x, norm_g, mod, router_wt, router_b, layer):
    tm = 512
    return pl.pallas_call(
        _router_kernel,
        out_shape=(jax.ShapeDtypeStruct((N_TOK, D_MODEL), F32), jax.ShapeDtypeStruct((N_EXPERTS, N_TOK), F32),
                   jax.ShapeDtypeStruct((8, N_TOK), F32)),
        grid=(N_TOK // tm,),
        in_specs=[pl.BlockSpec((tm, D_MODEL), lambda i: (i, 0)),
                  pl.BlockSpec((None, 1, D_MODEL), lambda i: (layer, 0, 0)),
                  _mod_spec(layer, 3, tm), _mod_spec(layer, 4, tm),
                  _full((N_EXPERTS, D_MODEL)), _full((N_EXPERTS, 1))],
        out_specs=(pl.BlockSpec((tm, D_MODEL), lambda i: (i, 0)), pl.BlockSpec((N_EXPERTS, tm), lambda i: (0, i)),
                   pl.BlockSpec((8, tm), lambda i: (0, i))),
        compiler_params=_params(("parallel",)),
        name="router",
    )(x, norm_g.reshape(DEPTH, 1, D_MODEL), mod, mod, router_wt, router_b.reshape(N_EXPERTS, 1))


MOE_TM = 512
N_PAIRS = N_EXPERT_GROUPS * 6
MOE_STEPS = 2 * (N_PAIRS + N_TOK // MOE_TM - 1)


def _gather_rows(src_hbm, dst_ref, idx_ref, base, n, sem):
    def issue(r, carry):
        pltpu.make_async_copy(src_hbm.at[pl.ds(idx_ref[base + r], 1), :], dst_ref.at[pl.ds(r, 1), :], sem).start()
        return carry

    def wait(r, carry):
        pltpu.make_async_copy(src_hbm.at[pl.ds(0, 1), :], dst_ref.at[pl.ds(r, 1), :], sem).wait()
        return carry

    lax.fori_loop(0, n, issue, 0, unroll=8)
    lax.fori_loop(0, n, wait, 0, unroll=8)


def _moe_plan(ids):
    i32 = jnp.int32
    lo, hi = ids[0].astype(i32), ids[1].astype(i32)
    src = jnp.argsort(lo * N_EXPERTS + hi).astype(i32)
    pos = jnp.argsort(src).astype(i32)
    n_tiles = N_TOK // MOE_TM
    ex = jnp.arange(N_EXPERTS, dtype=i32)
    lo_s, hi_s = lo[src].reshape(n_tiles, MOE_TM, 1), hi[src].reshape(n_tiles, MOE_TM, 1)
    used = ((lo_s == ex).any(axis=1) | (hi_s == ex).any(axis=1)).reshape(-1)
    n_valid = jnp.sum(used).astype(i32)
    idx = jnp.nonzero(used, size=MOE_STEPS, fill_value=0)[0].astype(i32)
    valid = jnp.arange(MOE_STEPS, dtype=i32) < n_valid
    idx = jnp.where(valid, idx, idx[jnp.maximum(n_valid - 1, 0)])
    tile, exp = idx // N_EXPERTS, idx % N_EXPERTS
    first = valid & (tile != jnp.concatenate([jnp.full((1,), -1, i32), tile[:-1]]))
    return src, pos, tile, exp, first.astype(i32), valid.astype(i32)


def _moe_kernel(tile_ref, exp_ref, first_ref, valid_ref, src_ref, xm_hbm, gates_ref, w1_ref, w3_ref, w2_ref,
                o_ref, xs_ref, xb_ref, sem):
    s = pl.program_id(0)

    @pl.when(first_ref[s] == 1)
    def _():
        _gather_rows(xm_hbm, xs_ref, src_ref, tile_ref[s] * MOE_TM, MOE_TM, sem)
        xb_ref[...] = xs_ref[...].astype(BF16)
        o_ref[...] = jnp.zeros_like(o_ref)

    @pl.when(valid_ref[s] == 1)
    def _():
        xb = xb_ref[...]
        h1 = _mm(xb, w1_ref[...].astype(BF16))
        h3 = _mm(xb, w3_ref[...].astype(BF16))
        gates = gates_ref[...]
        lane = lax.broadcasted_iota(jnp.int32, gates.shape, 1)
        gate = jnp.sum(jnp.where(lane == exp_ref[s], gates, 0.0), axis=-1, keepdims=True)
        hh = h1 * _sigmoid(h1) * h3 * gate
        o_ref[...] += _mm(hh.astype(BF16), w2_ref[...].astype(BF16))


def _moe(xm, gates_sorted, plan, w1, w3, w2, layer):
    src, _, tile, exp, first, valid = plan
    w_in = pl.BlockSpec((None, None, D_MODEL, D_EXPERT), lambda s, t, e, f, v, i: (layer, e[s], 0, 0))
    grid_spec = pltpu.PrefetchScalarGridSpec(
        num_scalar_prefetch=5, grid=(MOE_STEPS,),
        in_specs=[pl.BlockSpec(memory_space=pl.ANY),
                  pl.BlockSpec((MOE_TM, N_EXPERTS), lambda s, t, e, f, v, i: (t[s], 0)),
                  w_in, w_in,
                  pl.BlockSpec((None, None, D_EXPERT, D_MODEL), lambda s, t, e, f, v, i: (layer, e[s], 0, 0))],
        out_specs=pl.BlockSpec((MOE_TM, D_MODEL), lambda s, t, e, f, v, i: (t[s], 0)),
        scratch_shapes=[pltpu.VMEM((MOE_TM, D_MODEL), F32), pltpu.VMEM((MOE_TM, D_MODEL), BF16),
                        pltpu.SemaphoreType.DMA(())])
    return pl.pallas_call(
        _moe_kernel, out_shape=jax.ShapeDtypeStruct((N_TOK, D_MODEL), F32), grid_spec=grid_spec,
        compiler_params=_params(("arbitrary",)), name="moe",
    )(tile, exp, first, valid, src, xm, gates_sorted, w1, w3, w2)


def _combine_kernel(pos_ref, acc_hbm, x_ref, g_ref, *refs, split):
    i = pl.program_id(0)
    *out_refs, buf_ref, sem = refs
    _gather_rows(acc_hbm, buf_ref, pos_ref, i * MOE_TM, MOE_TM, sem)
    val = x_ref[...] + g_ref[...] * buf_ref[...]
    if not split:
        out_refs[0][...] = val
        return
    n_prompt_tiles = N_PROMPT // MOE_TM

    @pl.when(i < n_prompt_tiles)
    def _():
        out_refs[0][...] = val

    @pl.when(i >= n_prompt_tiles)
    def _():
        out_refs[1][...] = val


def _combine(acc_sorted, pos, x, mod, layer, split):
    tm = MOE_TM
    n_prompt_tiles = N_PROMPT // tm
    row = pl.BlockSpec((tm, D_MODEL), lambda i, p: (i, 0))
    if split:
        out_shape = (jax.ShapeDtypeStruct((N_PROMPT, D_MODEL), F32), jax.ShapeDtypeStruct((N_SAMPLE, D_MODEL), F32))
        out_specs = (pl.BlockSpec((tm, D_MODEL), lambda i, p: (jnp.minimum(i, n_prompt_tiles - 1), 0)),
                     pl.BlockSpec((tm, D_MODEL), lambda i, p: (jnp.maximum(i - n_prompt_tiles, 0), 0)))
    else:
        out_shape, out_specs = jax.ShapeDtypeStruct((N_TOK, D_MODEL), F32), row
    grid_spec = pltpu.PrefetchScalarGridSpec(
        num_scalar_prefetch=1, grid=(N_TOK // tm,),
        in_specs=[pl.BlockSpec(memory_space=pl.ANY), row,
                  pl.BlockSpec((None, None, None, 1, D_MODEL), lambda i, p: (layer, _mod_row(i, tm), 5, 0, 0))],
        out_specs=out_specs,
        scratch_shapes=[pltpu.VMEM((tm, D_MODEL), F32), pltpu.SemaphoreType.DMA(())])
    return pl.pallas_call(
        functools.partial(_combine_kernel, split=split), out_shape=out_shape, grid_spec=grid_spec,
        compiler_params=_params(("arbitrary",)), name="combine",
    )(pos, acc_sorted, x, mod)


def _gate_rows(gcols, n_seq, seq):
    return gcols.reshape(n_seq, seq // CHUNK, CHUNK, gcols.shape[-1]).transpose(0, 1, 3, 2)


def kernel(x_prompt, x_sample, cache_na_k, cache_na_v, state_rwkv, state_mlstm_c, state_mlstm_n, state_mlstm_m,
           c, c_ctx, norm1_g, norm2_g, w_mod, b_mod, w_in, conv_w, na_q_g, na_k_g, na_rpb, rw_w0, rw_w_up, rw_a0,
           rw_a_up, rw_g_up, rw_k_k, rw_k_a, rw_r_k, rw_ln_g, rw_ln_b, ml_gate_b, ml_norm_g, w_out, router_w,
           router_b, moe_w1, moe_w3, moe_w2):
    p = dict(rw_w0=rw_w0, rw_w_up=rw_w_up, rw_a0=rw_a0, rw_a_up=rw_a_up, rw_g_up=rw_g_up, rw_k_k=rw_k_k,
             rw_k_a=rw_k_a, rw_r_k=rw_r_k, rw_ln_g=rw_ln_g, rw_ln_b=rw_ln_b, ml_gate_b=ml_gate_b,
             ml_norm_g=ml_norm_g)
    cvecs = jnp.concatenate([c_ctx[None], c, jnp.zeros((MOD_ROWS - 1 - DEC_BATCH, D_MODEL), F32)], axis=0)
    mod = _adaln(cvecs, w_mod, b_mod).reshape(DEPTH, MOD_ROWS, 6, 1, D_MODEL)

    assert w_in.shape[-1] == P_IN
    w_in_t = jnp.swapaxes(w_in, 1, 2)
    tw = _na_tables(na_rpb)
    router_wt = router_w.T
    sample_row_off = N_PROMPT // DEC_SEQ

    x = jnp.concatenate([x_prompt.reshape(N_PROMPT, D_MODEL), x_sample.reshape(N_SAMPLE, D_MODEL)], axis=0)
    kv_prev, rw_prev, ml_prev = (), (), ()
    for l in range(DEPTH):
        u = _inproj(x, norm1_g, mod, w_in_t, l)
        ya_p, yb_p, *kv_prev = _attn_prompt(u, conv_w, na_q_g, na_k_g, l, prev=tuple(kv_prev))
        ya_s, yb_s = _na_sample(u, cache_na_k, cache_na_v, tw, conv_w, na_q_g, na_k_g, l)
        yc_p, *rw_prev = _rwkv(u, p, l, BATCH, SEQ, 0, emit_state=True, prev=tuple(rw_prev))
        (yc_s,) = _rwkv(u, p, l, DEC_BATCH, DEC_SEQ, sample_row_off, s0=state_rwkv)
        g0 = CB_GATE * MIX_W + SM_GATE
        gcols = u[:, g0:g0 + 4 * ML_HEADS]
        yd_p, *ml_prev = _mlstm(u, _gate_rows(gcols[:N_PROMPT], BATCH, SEQ), p, l, BATCH, SEQ, 0,
                                emit_state=True, prev=tuple(ml_prev))
        (yd_s,) = _mlstm(u, _gate_rows(gcols[N_PROMPT:], DEC_BATCH, DEC_SEQ), p, l, DEC_BATCH, DEC_SEQ,
                         sample_row_off, state=(state_mlstm_c, state_mlstm_n, state_mlstm_m))
        x = _outproj((ya_p, yb_p, yc_p, yd_p), (ya_s, yb_s, yc_s, yd_s), w_out, x, mod, l)
        xm, gates_t, ids = _router(x, norm2_g, mod, router_wt, router_b, l)
        plan = _moe_plan(ids)
        acc_sorted = _moe(xm, gates_t.T[plan[0]], plan, moe_w1, moe_w3, moe_w2, l)
        x = _combine(acc_sorted, plan[1], x, mod, l, split=l == DEPTH - 1)
    new_c, new_n, new_m = ml_prev
    return (x[0].reshape(BATCH, SEQ, D_MODEL), x[1].reshape(DEC_BATCH, DEC_SEQ, D_MODEL),
            kv_prev[0], kv_prev[1], rw_prev[0], new_c, new_n.reshape(BATCH, DEPTH, 2, ML_HEADS, ML_DH),
            new_m.reshape(BATCH, DEPTH, 2, ML_HEADS))
```

```python
import functools

import numpy as np
import jax
import jax.numpy as jnp
from jax import lax
from jax.experimental import pallas as pl
from jax.experimental.pallas import tpu as pltpu

F32 = jnp.float32
BF16 = jnp.bfloat16
HI = lax.Precision.HIGHEST

D_MODEL = 2048
BATCH = 16
SEQ = 256
DEPTH = 2
DEC_BATCH = 2
DEC_SEQ = 1024
PAST_LEN = 512
GRID_W = 64
MIX_W = D_MODEL // 4
CONV_K = 3
NA_DH = 64
NA_HEADS = MIX_W // NA_DH
NA_WIN_R = 8
NA_WIN_C = 16
NA_SCALE = NA_DH ** -0.5
ROPE_THETA = 10000.0
RW_DH = 64
RW_HEADS = MIX_W // RW_DH
RW_W_RANK = 64
RW_A_RANK = 64
RW_G_RANK = 128
RW_DECAY_SCALE = 0.606531
RW_GN_EPS = 64e-5
ML_DH = 128
ML_HEADS = MIX_W // ML_DH
N_EXPERTS = 16
N_EXPERT_GROUPS = 4
D_EXPERT = 512
EPS = 1e-6
NEG_INF = -1e30

N_PROMPT = BATCH * SEQ
N_SAMPLE = DEC_BATCH * DEC_SEQ
N_TOK = N_PROMPT + N_SAMPLE
MOD_ROWS = 8
CHUNK = 64
SUB = 16
P1_CHUNKS = 4
NA_ROWS_PER_TRIP = 4
P_IN = 13 * MIX_W + RW_W_RANK + RW_A_RANK + RW_G_RANK + 4 * ML_HEADS
P_BLOCKS = 15
P_PAD = P_BLOCKS * MIX_W
(CB_CVB, CB_CVC, CB_CVH, CB_NAQ, CB_NAK, CB_NAV, CB_RWR, CB_RWK, CB_RWV,
 CB_MLQ, CB_MLK, CB_MLV, CB_MLO, CB_SMALL, CB_GATE) = range(P_BLOCKS)
SM_WL, SM_AL, SM_GL = 0, 64, 128
SM_GATE = MIX_W - 4 * ML_HEADS
VMEM_LIMIT = 56 * 1024 * 1024


def _mm(a, b, prec=None):
    return jnp.dot(a, b, precision=prec, preferred_element_type=F32)


def _mm_nt(a, b, prec=None):
    return lax.dot_general(a, b, (((1,), (1,)), ((), ())), precision=prec, preferred_element_type=F32)


def _sigmoid(x):
    return 1.0 / (1.0 + jnp.exp(-x))


def _full(shape):
    n = len(shape)
    return pl.BlockSpec(shape, lambda *_: (0,) * n)


def _params(sem):
    return pltpu.CompilerParams(dimension_semantics=sem, vmem_limit_bytes=VMEM_LIMIT)


def _mod_row(i, tm):
    n_prompt_tiles = N_PROMPT // tm
    tiles_per_sample = DEC_SEQ // tm
    return jnp.where(i < n_prompt_tiles, 0, 1 + (i - n_prompt_tiles) // tiles_per_sample)


def _mod_spec(layer, chunk, tm, tn=D_MODEL, with_j=False):
    if with_j:
        return pl.BlockSpec((None, None, None, 1, tn), lambda i, j: (layer, _mod_row(i, tm), chunk, 0, j))
    return pl.BlockSpec((None, None, None, 1, tn), lambda i, *_: (layer, _mod_row(i, tm), chunk, 0, 0))


def _tri_masks(n):
    t = lax.broadcasted_iota(jnp.int32, (n, n), 0)
    s = lax.broadcasted_iota(jnp.int32, (n, n), 1)
    incl = ((s <= t).astype(F32), (s >= t).astype(F32))
    strict = ((s < t).astype(F32), (s > t).astype(F32))
    return incl, strict, t, s


def _split2(x):
    hi = x.astype(BF16)
    return hi, (x - hi.astype(F32)).astype(BF16)


def _mm1(a, b):
    return _mm(a.astype(BF16), b.astype(BF16))


def _mm3(a, b):
    a_hi, a_lo = _split2(a)
    b_hi, b_lo = _split2(b)
    return _mm(a_hi, b_lo) + _mm(a_lo, b_hi) + _mm(a_hi, b_hi)


def _split3(x):
    x0 = x.astype(BF16)
    r1 = x - x0.astype(F32)
    x1 = r1.astype(BF16)
    return x0, x1, (r1 - x1.astype(F32)).astype(BF16)


def _mm_exact_lhs(mask_b, x):
    x0, x1, x2 = _split3(x)
    return _mm(mask_b, x2) + _mm(mask_b, x1) + _mm(mask_b, x0)


def _mm_exact_rhs(x, mask_b):
    x0, x1, x2 = _split3(x)
    return _mm(x2, mask_b) + _mm(x1, mask_b) + _mm(x0, mask_b)


def _adaln_kernel(cv_ref, w_ref, b_ref, o_ref):
    cv = cv_ref[...]
    a0, a1, a2 = _split3(cv * _sigmoid(cv))
    w_hi, w_lo = _split2(w_ref[...])
    o_ref[...] = ((_mm(a2, w_hi) + _mm(a1, w_lo)) + (_mm(a1, w_hi) + _mm(a0, w_lo)) + _mm(a0, w_hi)) + b_ref[...]


def _adaln(cvecs, w_mod, b_mod):
    tn = 1024
    n_out = 6 * D_MODEL
    return pl.pallas_call(
        _adaln_kernel,
        out_shape=jax.ShapeDtypeStruct((DEPTH, MOD_ROWS, n_out), F32),
        grid=(DEPTH, n_out // tn),
        in_specs=[_full((MOD_ROWS, D_MODEL)),
                  pl.BlockSpec((None, D_MODEL, tn), lambda l, j: (l, 0, j)),
                  pl.BlockSpec((None, 1, tn), lambda l, j: (l, 0, j))],
        out_specs=pl.BlockSpec((None, MOD_ROWS, tn), lambda l, j: (l, 0, j)),
        compiler_params=_params(("parallel", "parallel")),
        name="adaln",
    )(cvecs, w_mod, b_mod.reshape(DEPTH, 1, n_out))


def _modulated(x, g, sh, sc):
    y = x * lax.rsqrt(jnp.mean(x * x, axis=-1, keepdims=True) + EPS) * g
    return y * (1.0 + sc) + sh


def _inproj_kernel(x_ref, g_ref, sh_ref, sc_ref, w_ref, o_ref, xm_ref):
    @pl.when(pl.program_id(1) == 0)
    def _():
        xm_ref[...] = _modulated(x_ref[...], g_ref[...], sh_ref[...], sc_ref[...]).astype(BF16)

    o_ref[...] = _mm_nt(xm_ref[...], w_ref[0].astype(BF16))


def _inproj_src_row(j):
    a = 9
    narrow = RW_W_RANK + RW_A_RANK + RW_G_RANK
    g = 16
    return g * jnp.where(j < a, j * (MIX_W // g),
                         jnp.where(j < CB_SMALL, j * (MIX_W // g) + narrow // g,
                                   jnp.where(j == CB_SMALL, a * MIX_W // g, (P_IN - MIX_W) // g)))


def _inproj(x, norm_g, mod, w_in_t, layer):
    tm, tn = 1024, MIX_W
    return pl.pallas_call(
        _inproj_kernel,
        out_shape=jax.ShapeDtypeStruct((N_TOK, P_PAD), F32),
        grid=(N_TOK // tm, P_BLOCKS),
        in_specs=[pl.BlockSpec((tm, D_MODEL), lambda i, j: (i, 0)),
                  pl.BlockSpec((None, 1, D_MODEL), lambda i, j: (layer, 0, 0)),
                  _mod_spec(layer, 0, tm), _mod_spec(layer, 1, tm),
                  pl.BlockSpec((pl.Element(1), pl.Element(tn), pl.Element(D_MODEL)),
                               lambda i, j: (layer, _inproj_src_row(j), 0))],
        out_specs=pl.BlockSpec((tm, tn), lambda i, j: (i, j)),
        scratch_shapes=[pltpu.VMEM((tm, D_MODEL), BF16)],
        compiler_params=_params(("parallel", "arbitrary")),
        name="inproj",
    )(x, norm_g.reshape(DEPTH, 1, D_MODEL), mod, mod, w_in_t)


def _conv_mix(b, c, h, w):
    u = c * h
    n = u.shape[0]
    row = lax.broadcasted_iota(jnp.int32, u.shape, 0)
    prev = jnp.where(row == 0, 0.0, pltpu.roll(u, 1, axis=0))
    nxt = jnp.where(row == n - 1, 0.0, pltpu.roll(u, n - 1, axis=0))
    return b * (prev * w[0:1] + u * w[1:2] + nxt * w[2:3])


def _head_rms(x, g):
    return x * lax.rsqrt(jnp.mean(x * x, axis=-1, keepdims=True) + EPS) * g


def _attn_prompt_kernel(cb_ref, cc_ref, ch_ref, q_ref, k_ref, v_ref, cw_ref, qg_ref, kg_ref,
                        ya_ref, yb_ref, nk_ref, nv_ref):
    ya_ref[...] = _conv_mix(cb_ref[...], cc_ref[...], ch_ref[...], cw_ref[...]).astype(ya_ref.dtype)
    sls = [slice(h * NA_DH, (h + 1) * NA_DH) for h in range(NA_HEADS)]
    qn = [_head_rms(q_ref[:, sl], qg_ref[...]) * NA_SCALE for sl in sls]
    kn = [_head_rms(k_ref[:, sl], kg_ref[...]) for sl in sls]
    vh = [v_ref[:, sl] for sl in sls]
    s = [_mm_nt(q.astype(BF16), k.astype(BF16)) for q, k in zip(qn, kn)]
    p = [jnp.exp(x - jnp.max(x, axis=-1, keepdims=True)) for x in s]
    o = [_mm(x.astype(BF16), v.astype(BF16)) / jnp.sum(x, axis=-1, keepdims=True) for x, v in zip(p, vh)]
    yb_ref[...] = jnp.concatenate(o, axis=1).astype(yb_ref.dtype)
    for h in range(NA_HEADS):
        nk_ref[h] = kn[h]
        nv_ref[h] = vh[h]


def _u_spec(rows, col_block, row_off_blocks=0):
    return pl.BlockSpec((rows, MIX_W), lambda b: (b + row_off_blocks, col_block))


def _carry_through(kernel, n_inputs, prev, n_plain_out, n_state_out, layer):
    n_prev = len(prev)

    def body(*refs):
        ins, rest = refs[:n_inputs], list(refs[n_inputs + n_prev:])
        if not n_prev:
            for i in range(n_plain_out, n_plain_out + n_state_out):
                full = rest[i]
                for d in range(DEPTH):
                    if d != layer:
                        full[d] = jnp.zeros(full.shape[1:], full.dtype)
                rest[i] = full.at[layer]
        return kernel(*ins, *rest)

    return body, [pl.BlockSpec(memory_space=pl.ANY)] * n_prev


def _state_spec(shape, layer, first, index):
    def index_map(*g):
        b, *tail = index(*g)
        return (b, 0 if first else layer, *tail)
    return pl.BlockSpec((None, DEPTH if first else None) + tuple(shape), index_map)


def _attn_prompt(u, conv_w, q_g, k_g, layer, prev=(), n_seq=BATCH, seq=SEQ):
    lw = lambda shape: pl.BlockSpec((None,) + shape, lambda b: (layer,) + (0,) * len(shape))
    y_spec = pl.BlockSpec((seq, MIX_W), lambda b: (b, 0))
    kv_spec = _state_spec((NA_HEADS, seq, NA_DH), layer, not prev, lambda b: (b, 0, 0, 0))
    in_specs = ([_u_spec(seq, cb) for cb in (CB_CVB, CB_CVC, CB_CVH, CB_NAQ, CB_NAK, CB_NAV)]
                + [lw((CONV_K, MIX_W)), lw((1, NA_DH)), lw((1, NA_DH))])
    body, prev_specs = _carry_through(_attn_prompt_kernel, len(in_specs), prev, 2, 2, layer)
    return pl.pallas_call(
        body,
        out_shape=(jax.ShapeDtypeStruct((n_seq * seq, MIX_W), BF16),) * 2
        + (jax.ShapeDtypeStruct((n_seq, DEPTH, NA_HEADS, seq, NA_DH), F32),) * 2,
        grid=(n_seq,),
        in_specs=in_specs + prev_specs,
        out_specs=(y_spec, y_spec, kv_spec, kv_spec),
        input_output_aliases={len(in_specs) + i: 2 + i for i in range(len(prev))},
        compiler_params=_params(("parallel",)),
        name="attn_prompt",
    )(u, u, u, u, u, u, conv_w, q_g.reshape(DEPTH, 1, NA_DH), k_g.reshape(DEPTH, 1, NA_DH), *prev)


def _na_kernel(cb_ref, cc_ref, ch_ref, q_ref, k_ref, v_ref, kc_ref, vc_ref, tw_ref, cos_ref, sin_ref,
               perm_ref, cw_ref, qg_ref, kg_ref, ya_ref, yb_ref, qs_ref, ks_ref, tws_ref):
    rows = DEC_SEQ // GRID_W
    wr = min(NA_WIN_R, rows)
    nw = wr * GRID_W
    ya_ref[...] = _conv_mix(cb_ref[...], cc_ref[...], ch_ref[...], cw_ref[...]).astype(ya_ref.dtype)
    cos, sin, perm_b = cos_ref[...], sin_ref[...], perm_ref[...].astype(BF16)

    def rope(x):
        hi, lo = _split2(x)
        return x * cos + (_mm(hi, perm_b) + _mm(lo, perm_b)) * sin

    for h in range(NA_HEADS):
        sl = slice(h * NA_DH, (h + 1) * NA_DH)
        qs_ref[...] = (rope(_head_rms(q_ref[:, sl], qg_ref[...])) * NA_SCALE).astype(BF16)
        ks_ref[...] = rope(_head_rms(k_ref[:, sl], kg_ref[...])).astype(BF16)
        kch = kc_ref[h].astype(BF16)
        vch = vc_ref[h].astype(BF16)
        for p in range(wr):
            tws_ref[p] = jnp.concatenate([tw_ref[h, j - p + NA_WIN_R - 1] for j in range(wr)], axis=1)

        def rows_step(t, carry):
            rr = [t * NA_ROWS_PER_TRIP + i for i in range(NA_ROWS_PER_TRIP)]
            rs = [jnp.clip(r - wr // 2, 0, rows - wr) for r in rr]
            q0 = [pl.multiple_of(r * GRID_W, GRID_W) for r in rr]
            k0 = [pl.multiple_of(x * GRID_W, GRID_W) for x in rs]
            q_r = [qs_ref[pl.ds(x, GRID_W), :] for x in q0]
            s_w = [_mm_nt(q, ks_ref[pl.ds(k, nw), :]) + tws_ref[r - x] for q, k, r, x in zip(q_r, k0, rr, rs)]
            s_c = [_mm_nt(q, kch) for q in q_r]
            m = [jnp.maximum(jnp.max(a, axis=-1, keepdims=True), jnp.max(b, axis=-1, keepdims=True))
                 for a, b in zip(s_w, s_c)]
            p_w = [jnp.exp(a - x) for a, x in zip(s_w, m)]
            p_c = [jnp.exp(b - x) for b, x in zip(s_c, m)]
            den = [jnp.sum(a, axis=-1, keepdims=True) + jnp.sum(b, axis=-1, keepdims=True) for a, b in zip(p_w, p_c)]
            v_w = [v_ref[pl.ds(k, nw), sl].astype(BF16) for k in k0]
            o = [(_mm(a.astype(BF16), v) + _mm(b.astype(BF16), vch)) / d for a, b, v, d in zip(p_w, p_c, v_w, den)]
            for x, val in zip(q0, o):
                yb_ref[pl.ds(x, GRID_W), sl] = val.astype(yb_ref.dtype)
            return carry

        lax.fori_loop(0, rows // NA_ROWS_PER_TRIP, rows_step, 0)


def _na_tables(rpb):
    qc = np.arange(GRID_W)
    kc = np.arange(GRID_W)
    wstart = np.clip(qc - NA_WIN_C // 2, 0, GRID_W - NA_WIN_C)
    colmask = (kc[None, :] >= wstart[:, None]) & (kc[None, :] < wstart[:, None] + NA_WIN_C)
    dc = np.clip(kc[None, :] - qc[:, None], -(NA_WIN_C - 1), NA_WIN_C - 1) + NA_WIN_C - 1
    pick = (np.arange(2 * NA_WIN_C - 1)[:, None] == dc.reshape(1, -1)).astype(np.float32)
    bias = jnp.einsum('lhrd,dn->lhrn', rpb, jnp.asarray(pick), precision=HI)
    return jnp.where(colmask[None, None, None], bias.reshape(rpb.shape[:3] + dc.shape), NEG_INF)


def _rope_tables():
    t = np.arange(DEC_SEQ)
    quarter = NA_DH // 4
    freq = ROPE_THETA ** (-np.arange(quarter, dtype=np.float32) / quarter)
    ang_r = (t // GRID_W).astype(np.float32)[:, None] * freq
    ang_c = (t % GRID_W).astype(np.float32)[:, None] * freq
    cos = np.concatenate([np.cos(ang_r), np.cos(ang_r), np.cos(ang_c), np.cos(ang_c)], axis=-1)
    sin = np.concatenate([-np.sin(ang_r), np.sin(ang_r), -np.sin(ang_c), np.sin(ang_c)], axis=-1)
    src = np.concatenate([np.arange(quarter) + quarter, np.arange(quarter),
                          np.arange(quarter) + 3 * quarter, np.arange(quarter) + 2 * quarter])
    perm = np.zeros((NA_DH, NA_DH), np.float32)
    perm[src, np.arange(NA_DH)] = 1.0
    return cos.astype(np.float32), sin.astype(np.float32), perm


def _na_sample(u, cache_k, cache_v, tw, conv_w, q_g, k_g, layer, n_seq=DEC_BATCH, row_off=N_PROMPT // DEC_SEQ):
    cos, sin, perm = _rope_tables()
    lw = lambda shape: pl.BlockSpec((None,) + shape, lambda b: (layer,) + (0,) * len(shape))
    y_spec = pl.BlockSpec((DEC_SEQ, MIX_W), lambda b: (b, 0))
    c_spec = pl.BlockSpec((None, None, NA_HEADS, PAST_LEN, NA_DH), lambda b: (b, layer, 0, 0, 0))
    wr = min(NA_WIN_R, DEC_SEQ // GRID_W)
    n_off = 2 * NA_WIN_R - 1
    return pl.pallas_call(
        _na_kernel,
        out_shape=(jax.ShapeDtypeStruct((n_seq * DEC_SEQ, MIX_W), BF16),) * 2,
        grid=(n_seq,),
        in_specs=[_u_spec(DEC_SEQ, cb, row_off) for cb in (CB_CVB, CB_CVC, CB_CVH, CB_NAQ, CB_NAK, CB_NAV)]
        + [c_spec, c_spec, lw((NA_HEADS, n_off, GRID_W, GRID_W)),
           _full((DEC_SEQ, NA_DH)), _full((DEC_SEQ, NA_DH)), _full((NA_DH, NA_DH)),
           lw((CONV_K, MIX_W)), lw((1, NA_DH)), lw((1, NA_DH))],
        out_specs=(y_spec, y_spec),
        scratch_shapes=[pltpu.VMEM((DEC_SEQ, NA_DH), BF16), pltpu.VMEM((DEC_SEQ, NA_DH), BF16),
                        pltpu.VMEM((wr, GRID_W, wr * GRID_W), F32)],
        compiler_params=_params(("parallel",)),
        name="na_sample",
    )(u, u, u, u, u, u, cache_k, cache_v, tw, jnp.asarray(cos), jnp.asarray(sin), jnp.asarray(perm),
      conv_w, q_g.reshape(DEPTH, 1, NA_DH), k_g.reshape(DEPTH, 1, NA_DH))


def _seg_ones(width, seg):
    a = lax.broadcasted_iota(jnp.int32, (width, width), 0) // seg
    b = lax.broadcasted_iota(jnp.int32, (width, width), 1) // seg
    return (a == b).astype(F32)


def _rwkv_kernel(*refs, seq, has_s0, emit_state):
    it = iter(refs)
    r_ref, k_ref, v_ref, sm_ref = (next(it) for _ in range(4))
    (w0_ref, wup_ref, a0_ref, aup_ref, gup_ref, kkp_ref, kap_ref, rkp_ref, lng_ref, lnb_ref) = (
        next(it) for _ in range(10))
    mask_ref = next(it)
    s0_ref = next(it) if has_s0 else None
    y_ref = next(it)
    so_ref = next(it) if emit_state else None
    kk_ref, lw_ref, ka_ref, kd_ref, coef_ref, ysp_ref = (next(it) for _ in range(6))

    dh = RW_DH
    pw = 2 * dh
    nc = seq // CHUNK
    seg_b = _seg_ones(pw, dh).astype(BF16)

    r = r_ref[...]
    k = k_ref[...]
    sm = sm_ref[...]
    wl = jnp.tanh(sm[:, SM_WL:SM_WL + RW_W_RANK])
    al = sm[:, SM_AL:SM_AL + RW_A_RANK]
    kk = k * kkp_ref[...]
    kk = kk * lax.rsqrt(_mm_exact_rhs(kk * kk, seg_b) + EPS)
    kk_ref[...] = kk
    for z in range(2):
        lw_ref[z] = -RW_DECAY_SCALE * _sigmoid(w0_ref[z:z + 1, :] + _mm3(wl, wup_ref[z]))
        a = _sigmoid(a0_ref[z:z + 1, :] + _mm3(al, aup_ref[z]))
        ka_ref[z] = kk * a
        kd_ref[z] = k * (1.0 + (a - 1.0) * kap_ref[...])

    cat = jnp.concatenate

    def phase1(c2, carry):
        incl, strict, ti, si = _tri_masks(CHUNK)
        incl_b = (mask_ref[0], mask_ref[1])
        strict2 = tuple(cat([m, m], axis=1) for m in strict)
        incl2 = tuple(cat([m, m], axis=1) for m in incl)
        diag_blk = (ti // SUB == si // SUB).astype(F32)
        eye = (ti == si).astype(F32)
        zero_pair = jnp.zeros((CHUNK, pw), F32)
        head_of_lane = lax.broadcasted_iota(jnp.int32, (CHUNK, pw), 1) // dh
        head_of_lane2 = (lax.broadcasted_iota(jnp.int32, (CHUNK, 2 * pw), 1) // dh) % 2
        r_pw = lax.broadcasted_iota(jnp.int32, (pw, pw), 0)
        c_pw = lax.broadcasted_iota(jnp.int32, (pw, pw), 1)
        same_head = r_pw // dh == c_pw // dh
        eye_pw = (r_pw == c_pw).astype(F32)
        chains = []
        for cc in range(P1_CHUNKS):
            rows = pl.ds(pl.multiple_of((c2 * P1_CHUNKS + cc) * CHUNK, CHUNK), CHUNK)
            vc, rc, kkc = v_ref[rows, :], r_ref[rows, :], kk_ref[rows, :]
            for z in range(2):
                lwc = lw_ref[z, rows, :]
                cum = _mm_exact_lhs(incl_b[z], lwc)
                tot = cum[CHUNK - 1:CHUNK] if z == 0 else cum[0:1]
                e_neg = jnp.exp(-cum)
                dec = jnp.exp(tot - cum)
                kac, kdc = ka_ref[z, rows, :], kd_ref[z, rows, :]
                chains.append((z, kkc * jnp.exp(cum - lwc), rc * jnp.exp(cum), vc, kac * e_neg, kdc * e_neg,
                               kac * dec, kdc * dec, jnp.exp(tot)))
        heads = [(q, j) for q in range(len(chains)) for j in range(2)]
        zq = [ch[0] for ch in chains]
        ymat = [cat([ch[4], ch[5]], axis=0).astype(BF16) for ch in chains]
        vz = [cat([zero_pair, ch[3]], axis=0) for ch in chains]
        nvz = [cat([zero_pair, -ch[3]], axis=1) for ch in chains]
        x_in = [cat([jnp.where(head_of_lane == j, chains[q][1], 0.0),
                     jnp.where(head_of_lane == j, chains[q][2], 0.0)], axis=0) for q, j in heads]
        aa = [_mm_nt(x.astype(BF16), ymat[q]) for x, (q, _) in zip(x_in, heads)]
        top = [a[0:CHUNK] * strict2[zq[q]] for a, (q, _) in zip(aa, heads)]
        a_r = [a[CHUNK:] * incl2[zq[q]] for a, (q, _) in zip(aa, heads)]
        akv = [_mm1(t, vz[q]) for t, (q, _) in zip(top, heads)]
        low = [t[:, 0:CHUNK] for t in top]
        ld = [x * diag_blk for x in low]
        lo = [x - y for x, y in zip(low, ld)]
        l2 = [_mm1(x, x) for x in ld]
        l4 = [_mm1(x, x) for x in l2]
        l8 = [_mm1(x, x) for x in l4]
        td = [eye - x for x in ld]
        for lp in (l2, l4, l8):
            td = [t + _mm1(t, p_) for t, p_ in zip(td, lp)]
        x0 = [_mm1(t, cat([chains[q][1], a, l], axis=1)) for t, a, l, (q, _) in zip(td, akv, lo, heads)]
        pq0 = [x[:, 0:2 * pw] for x in x0]
        wm = [x[:, 2 * pw:] for x in x0]
        pq = pq0
        for _ in range(CHUNK // SUB - 1):
            pq = [p0 - _mm1(w, p_) for p0, w, p_ in zip(pq0, wm, pq)]
        ryc = [_mm1(a, cat([p_, nvz[q]], axis=0)) for a, p_, (q, _) in zip(a_r, pq, heads)]
        for q, ch in enumerate(chains):
            pq_m = jnp.where(head_of_lane2 == 0, pq[2 * q], pq[2 * q + 1])
            ryc_m = jnp.where(head_of_lane2 == 0, ryc[2 * q], ryc[2 * q + 1])
            g1 = _mm1(ch[6].T, pq_m)
            g2 = _mm1(ch[7].T, ch[3])
            g_t = eye_pw * ch[8] - jnp.where(same_head, g1[:, 0:pw], 0.0)
            h_t = jnp.where(same_head, g2 - g1[:, pw:], 0.0)
            coef_ref[c2 * P1_CHUNKS + q // 2, q % 2] = cat(
                [g_t, ch[2] - ryc_m[:, 0:pw], h_t, -ryc_m[:, pw:]], axis=0)
        return carry

    lax.fori_loop(0, nc // P1_CHUNKS, phase1, 0)

    def block_diag(a, b):
        zero = jnp.zeros((dh, dh), F32)
        return cat([cat([a, zero], axis=1), cat([zero, b], axis=1)], axis=0)

    m_init = tuple((block_diag(s0_ref[z, 0].T, s0_ref[z, 1].T) if has_s0 else jnp.zeros((pw, pw), F32))
                   for z in range(2))
    n_lhs = pw + CHUNK

    def phase2(ci, ms):
        new_ms, ys = [], []
        for z in range(2):
            c = ci if z == 0 else nc - 1 - ci
            out = _mm3(coef_ref[c, z, 0:n_lhs, :], ms[z]) + coef_ref[c, z, n_lhs:, :]
            new_ms.append(out[0:pw])
            ys.append(out[pw:])
        ysp_ref[ci] = cat(ys, axis=1)
        return tuple(new_ms)

    m_fin = lax.fori_loop(0, nc, phase2, m_init)

    if emit_state:
        for z in range(2):
            for j in range(2):
                so_ref[z, j] = m_fin[z][j * dh:(j + 1) * dh, j * dh:(j + 1) * dh].T

    y = cat([ysp_ref[c, :, 0:pw] + ysp_ref[nc - 1 - c, :, pw:] for c in range(nc)], axis=0)
    mu = _mm_exact_rhs(y, seg_b) * (1.0 / dh)
    yc = y - mu
    var = _mm_exact_rhs(yc * yc, seg_b) * (1.0 / dh)
    yn = yc * lax.rsqrt(var + RW_GN_EPS) * lng_ref[...] + lnb_ref[...]
    bonus = _mm_exact_rhs(r * k * rkp_ref[...], seg_b) * v_ref[...]
    g = _mm3(_sigmoid(sm[:, SM_GL:SM_GL + RW_G_RANK]), gup_ref[...])
    y_ref[...] = ((yn + bonus) * g).astype(y_ref.dtype)


def _rwkv(u, p, layer, n_seq, seq, row_off, s0=None, emit_state=False, prev=()):
    n_pairs = RW_HEADS // 2
    pw = 2 * RW_DH
    bpc = MIX_W // pw
    lw = lambda shape: pl.BlockSpec((None,) + shape, lambda b, hp: (layer,) + (0,) * (len(shape) - 1) + (hp,))
    row = lambda a: a.reshape(DEPTH, 1, MIX_W)
    u_pair = lambda cb: pl.BlockSpec((seq, pw), lambda b, hp: (b + row_off, cb * bpc + hp))
    in_specs = [u_pair(CB_RWR), u_pair(CB_RWK), u_pair(CB_RWV),
                pl.BlockSpec((seq, MIX_W), lambda b, hp: (b + row_off, CB_SMALL)),
                lw((2, pw)), lw((2, RW_W_RANK, pw)), lw((2, pw)), lw((2, RW_A_RANK, pw)),
                lw((RW_G_RANK, pw))] + [lw((1, pw))] * 5 + [_full((2, CHUNK, CHUNK))]
    t_idx = np.arange(CHUNK)
    incl_masks = np.stack([t_idx[None, :] <= t_idx[:, None], t_idx[None, :] >= t_idx[:, None]])
    args = [u, u, u, u, p['rw_w0'], p['rw_w_up'], p['rw_a0'], p['rw_a_up'], p['rw_g_up'],
            row(p['rw_k_k']), row(p['rw_k_a']), row(p['rw_r_k']), row(p['rw_ln_g']), row(p['rw_ln_b']),
            jnp.asarray(incl_masks, BF16)]
    st_blk = (2, 2, RW_DH, RW_DH)
    if s0 is not None:
        in_specs.append(pl.BlockSpec((None, None) + st_blk, lambda b, hp: (b, layer, 0, hp, 0, 0)))
        args.append(s0)
    out_shape = [jax.ShapeDtypeStruct((n_seq * seq, MIX_W), BF16)]
    out_specs = [pl.BlockSpec((seq, pw), lambda b, hp: (b, hp))]
    if emit_state:
        out_shape.append(jax.ShapeDtypeStruct((n_seq, DEPTH, 2, RW_HEADS, RW_DH, RW_DH), F32))
        out_specs.append(_state_spec(st_blk, layer, not prev, lambda b, hp: (b, 0, hp, 0, 0)))
    nc = seq // CHUNK
    tok = lambda n: pltpu.VMEM((n, seq, pw) if n else (seq, pw), F32)
    body, prev_specs = _carry_through(
        functools.partial(_rwkv_kernel, seq=seq, has_s0=s0 is not None, emit_state=emit_state), len(in_specs), prev,
        1, 1 if emit_state else 0, layer)
    return pl.pallas_call(
        body,
        out_shape=tuple(out_shape), grid=(n_seq, n_pairs), in_specs=in_specs + prev_specs,
        out_specs=tuple(out_specs),
        input_output_aliases={len(in_specs) + i: 1 + i for i in range(len(prev))},
        scratch_shapes=[tok(0), tok(2), tok(2), tok(2),
                        pltpu.VMEM((nc, 2, 2 * (pw + CHUNK), pw), F32), pltpu.VMEM((nc, CHUNK, 2 * pw), F32)],
        compiler_params=_params(("parallel", "parallel")),
        name=f"rwkv_{seq}",
    )(*args, *prev)


def _log_sigmoid(x):
    return jnp.minimum(x, 0.0) - jnp.log(1.0 + jnp.exp(-jnp.abs(x)))


def _mlstm_kernel(*refs, seq, has_s0, emit_state):
    it = iter(refs)
    q_ref, k_ref, v_ref, o_ref, sm_ref, gr_ref, bc_ref, br_ref, ng_ref = (next(it) for _ in range(9))
    c0_ref, n0_ref, m0_ref = ((next(it), next(it), next(it)) if has_s0 else (None, None, None))
    y_ref = next(it)
    co_ref, no_ref, mo_ref = ((next(it), next(it), next(it)) if emit_state else (None, None, None))
    hs_ref, c_ref, n_ref, m_ref = (next(it) for _ in range(4))

    nc = seq // CHUNK
    n_st = 2 * ML_HEADS
    for i in range(n_st):
        z, h = divmod(i, ML_HEADS)
        c_ref[i] = c0_ref[z, h] if has_s0 else jnp.zeros((ML_DH, ML_DH), F32)
        n_ref[i] = n0_ref[z, h] if has_s0 else jnp.zeros((1, ML_DH), F32)
        m_ref[i] = m0_ref[z, h] if has_s0 else jnp.zeros((1, 1), F32)

    incl, _, ti, si = _tri_masks(CHUNK)
    before = ((si <= ti), (si >= ti))

    nh = ML_HEADS
    chains = [(z, h) for z in range(2) for h in range(nh)]
    each = lambda f, *xs: [f(*a) for a in zip(*xs)]
    zs = [z for z, _ in chains]

    def chunk_step(ci, carry):
        per_dir = []
        for z in range(2):
            c = ci if z == 0 else nc - 1 - ci
            rows = pl.ds(pl.multiple_of(c * CHUNK, CHUNK), CHUNK)
            g0 = SM_GATE + z * 2 * nh
            gc = sm_ref[rows, g0:g0 + 2 * nh] + bc_ref[:, z * 2 * nh:(z + 1) * 2 * nh]
            gr = gr_ref[c, z * 2 * nh:(z + 1) * 2 * nh, :] + br_ref[z * 2 * nh:(z + 1) * 2 * nh, :]
            b_cols = _mm(incl[z], _log_sigmoid(gc[:, nh:]), HI)
            b_rows = _mm(_log_sigmoid(gr[nh:]), incl[1 - z], HI)
            per_dir.append((rows, gc[:, :nh], b_cols, gr[:nh], b_rows))
        rows = [per_dir[z][0] for z, _ in chains]
        hsl = [slice(h * ML_DH, (h + 1) * ML_DH) for _, h in chains]
        i_col = [per_dir[z][1][:, h:h + 1] for z, h in chains]
        b_col = [per_dir[z][2][:, h:h + 1] for z, h in chains]
        i_row = [per_dir[z][3][h:h + 1] for z, h in chains]
        b_row = [per_dir[z][4][h:h + 1] for z, h in chains]
        b_last = each(lambda b, z: b[CHUNK - 1:CHUNK] if z == 0 else b[0:1], b_col, zs)
        m_old = [m_ref[i] for i in range(n_st)]
        cm = [c_ref[i] for i in range(n_st)]
        nv = [n_ref[i] for i in range(n_st)]
        qc = each(lambda r, s: q_ref[r, s] * (ML_DH ** -0.5), rows, hsl)
        kc = each(lambda r, s: k_ref[r, s], rows, hsl)
        vc = each(lambda r, s: v_ref[r, s], rows, hsl)
        qk = each(lambda q, k: _mm_nt(q.astype(BF16), k.astype(BF16)), qc, kc)
        qcm = each(lambda q, c_: _mm_nt(q.astype(BF16), c_.astype(BF16)), qc, cm)
        a_t = each(lambda b, m: b + m, b_col, m_old)
        dmat = each(lambda bc, brw, ir, z: jnp.where(before[z], bc - brw + ir, NEG_INF), b_col, b_row, i_row, zs)
        m_t = each(lambda a, d: jnp.maximum(a, jnp.max(d, axis=-1, keepdims=True)), a_t, dmat)
        s = each(lambda x, d, m: x * jnp.exp(d - m), qk, dmat, m_t)
        inter = each(lambda a, m: jnp.exp(a - m), a_t, m_t)
        sv = each(_mm1, s, vc)
        g_col = each(lambda bl, bc, ic: bl - bc + ic, b_last, b_col, i_col)
        a_l = each(lambda bl, m: bl + m, b_last, m_old)
        m_new = each(lambda a, g: jnp.maximum(a, jnp.max(g, axis=0, keepdims=True)), a_l, g_col)
        wgt = each(lambda g, m: jnp.exp(g - m), g_col, m_new)
        vk = each(lambda v, w, k: _mm1((v * w).T, k), vc, wgt, kc)
        decay = each(lambda a, m: jnp.exp(a - m), a_l, m_new)
        num = each(lambda i_, x, y: i_ * x + y, inter, qcm, sv)
        den = each(lambda i_, q, n_, s_: i_ * jnp.sum(q * n_, axis=-1, keepdims=True)
                   + jnp.sum(s_, axis=-1, keepdims=True), inter, qc, nv, s)
        hh = each(lambda n_, d, m: n_ / jnp.maximum(jnp.abs(d), jnp.exp(-m)), num, den, m_t)
        for z in range(2):
            hs_ref[z, per_dir[z][0], :] = jnp.concatenate(hh[z * nh:(z + 1) * nh], axis=1)
        for i in range(n_st):
            c_ref[i] = decay[i] * cm[i] + vk[i]
            n_ref[i] = decay[i] * nv[i] + jnp.sum(wgt[i] * kc[i], axis=0, keepdims=True)
            m_ref[i] = m_new[i]
        return carry

    lax.fori_loop(0, nc, chunk_step, 0)

    if emit_state:
        for i in range(n_st):
            z, h = divmod(i, ML_HEADS)
            co_ref[z, h] = c_ref[i]
            no_ref[z, h] = n_ref[i]
            mo_ref[z, h] = m_ref[i]

    for h in range(ML_HEADS):
        hsl = slice(h * ML_DH, (h + 1) * ML_DH)
        hn = _head_rms(hs_ref[0, :, hsl] + hs_ref[1, :, hsl], ng_ref[:, hsl])
        y_ref[:, hsl] = (_sigmoid(o_ref[:, hsl]) * hn).astype(y_ref.dtype)


def _mlstm(u, gate_rows, p, layer, n_seq, seq, row_off, state=None, emit_state=False, prev=()):
    lw = lambda shape: pl.BlockSpec((None,) + shape, lambda b: (layer,) + (0,) * len(shape))
    nc = seq // CHUNK
    n_gate = 4 * ML_HEADS
    in_specs = [_u_spec(seq, cb, row_off) for cb in (CB_MLQ, CB_MLK, CB_MLV, CB_MLO, CB_GATE)] + [
        pl.BlockSpec((None, nc, n_gate, CHUNK), lambda b: (b, 0, 0, 0)),
        lw((1, n_gate)), lw((n_gate, 1)), lw((1, MIX_W))]
    args = [u, u, u, u, u, gate_rows, p['ml_gate_b'].reshape(DEPTH, 1, n_gate),
            p['ml_gate_b'].reshape(DEPTH, n_gate, 1), p['ml_norm_g'].reshape(DEPTH, 1, MIX_W)]
    c_shape, n_shape, m_shape = (2, ML_HEADS, ML_DH, ML_DH), (2, ML_HEADS, 1, ML_DH), (2, ML_HEADS, 1, 1)
    if state is not None:
        c0, n0, m0 = state
        for a, shp in ((c0, c_shape), (n0, n_shape), (m0, m_shape)):
            in_specs.append(pl.BlockSpec((None, None) + shp, lambda b: (b, layer, 0, 0, 0, 0)))
            args.append(a.reshape(a.shape[:2] + shp))
    out_shape = [jax.ShapeDtypeStruct((n_seq * seq, MIX_W), BF16)]
    out_specs = [pl.BlockSpec((seq, MIX_W), lambda b: (b, 0))]
    if emit_state:
        for shp in (c_shape, n_shape, m_shape):
            out_shape.append(jax.ShapeDtypeStruct((n_seq, DEPTH) + shp, F32))
            out_specs.append(_state_spec(shp, layer, not prev, lambda b: (b, 0, 0, 0, 0)))
    n_st = 2 * ML_HEADS
    body, prev_specs = _carry_through(
        functools.partial(_mlstm_kernel, seq=seq, has_s0=state is not None, emit_state=emit_state),
        len(in_specs), prev, 1, 3 if emit_state else 0, layer)
    return pl.pallas_call(
        body,
        out_shape=tuple(out_shape), grid=(n_seq,), in_specs=in_specs + prev_specs, out_specs=tuple(out_specs),
        input_output_aliases={len(in_specs) + i: 1 + i for i in range(len(prev))},
        scratch_shapes=[pltpu.VMEM((2, seq, MIX_W), F32), pltpu.VMEM((n_st, ML_DH, ML_DH), F32),
                        pltpu.VMEM((n_st, 1, ML_DH), F32), pltpu.VMEM((n_st, 1, 1), F32)],
        compiler_params=_params(("parallel",)),
        name=f"mlstm_{seq}",
    )(*args, *prev)


def _outproj_kernel(*refs, n_prompt_tiles):
    n_mix = 4
    yp_refs, ys_refs = refs[:n_mix], refs[n_mix:2 * n_mix]
    w_ref, x_ref, g_ref, o_ref = refs[2 * n_mix:]
    is_prompt = pl.program_id(0) < n_prompt_tiles
    acc = None
    for i, (yp_ref, ys_ref) in enumerate(zip(yp_refs, ys_refs)):
        y = jnp.where(is_prompt, yp_ref[...], ys_ref[...])
        part = _mm(y, w_ref[i * MIX_W:(i + 1) * MIX_W, :].astype(BF16))
        acc = part if acc is None else acc + part
    o_ref[...] = x_ref[...] + g_ref[...] * acc


def _outproj(ys_prompt, ys_sample, w_out, x, mod, layer):
    tm, tn = 1024, 1024
    n_prompt_tiles = N_PROMPT // tm
    yp_spec = pl.BlockSpec((tm, MIX_W), lambda i, j: (jnp.minimum(i, n_prompt_tiles - 1), 0))
    ys_spec = pl.BlockSpec((tm, MIX_W), lambda i, j: (jnp.maximum(i - n_prompt_tiles, 0), 0))
    return pl.pallas_call(
        functools.partial(_outproj_kernel, n_prompt_tiles=n_prompt_tiles),
        out_shape=jax.ShapeDtypeStruct((N_TOK, D_MODEL), F32),
        grid=(N_TOK // tm, D_MODEL // tn),
        in_specs=[yp_spec] * 4 + [ys_spec] * 4 + [pl.BlockSpec((None, D_MODEL, tn), lambda i, j: (layer, 0, j)),
                                                   pl.BlockSpec((tm, tn), lambda i, j: (i, j)),
                                                   _mod_spec(layer, 2, tm, tn, with_j=True)],
        out_specs=pl.BlockSpec((tm, tn), lambda i, j: (i, j)),
        compiler_params=_params(("parallel", "parallel")),
        name="outproj",
    )(*ys_prompt, *ys_sample, w_out, x, mod)


def _router_kernel(x_ref, g_ref, sh_ref, sc_ref, rw_ref, rb_ref, xm_ref, gt_ref, ids_ref):
    xm = _modulated(x_ref[...], g_ref[...], sh_ref[...], sc_ref[...])
    xm_ref[...] = xm
    logits = _mm_nt(rw_ref[...], xm, HI)
    ex = jnp.exp(logits - jnp.max(logits, axis=0, keepdims=True))
    scores = ex / jnp.sum(ex, axis=0, keepdims=True)
    sel = scores + rb_ref[...]
    per = N_EXPERTS // N_EXPERT_GROUPS
    s = [sel[e:e + 1, :] for e in range(N_EXPERTS)]
    grp_score = []
    for g in range(N_EXPERT_GROUPS):
        a, b, c, d = s[per * g:per * (g + 1)]
        hi1, lo1, hi2, lo2 = jnp.maximum(a, b), jnp.minimum(a, b), jnp.maximum(c, d), jnp.minimum(c, d)
        grp_score.append(jnp.maximum(hi1, hi2) + jnp.maximum(jnp.minimum(hi1, hi2), jnp.maximum(lo1, lo2)))
    best = functools.reduce(jnp.maximum, grp_score)
    in_grp, taken = [], jnp.zeros_like(best)
    for g in range(N_EXPERT_GROUPS):
        hit = jnp.where(grp_score[g] == best, 1.0, 0.0) * (1.0 - taken)
        in_grp.append(hit)
        taken = taken + hit
    picked, flag = [], []
    for e in range(N_EXPERTS):
        g = e // per
        rank = jnp.zeros_like(best)
        for o in range(per * g, per * (g + 1)):
            if o < e:
                rank += jnp.where(s[o] >= s[e], 1.0, 0.0)
            elif o > e:
                rank += jnp.where(s[o] > s[e], 1.0, 0.0)
        flag.append(in_grp[g] * jnp.where(rank < 2.0, 1.0, 0.0))
        picked.append(flag[e] * scores[e:e + 1, :])
    total = functools.reduce(lambda x, y: x + y, picked)
    for e in range(N_EXPERTS):
        gt_ref[e:e + 1, :] = picked[e] / total
    lo_id = functools.reduce(
        jnp.minimum, [jnp.where(flag[e] > 0.0, float(e), float(N_EXPERTS)) for e in range(N_EXPERTS)])
    hi_id = functools.reduce(jnp.maximum, [jnp.where(flag[e] > 0.0, float(e), -1.0) for e in range(N_EXPERTS)])
    ids_ref[0:1, :] = lo_id
    ids_ref[1:2, :] = hi_id
    ids_ref[2:, :] = jnp.zeros((ids_ref.shape[0] - 2,) + lo_id.shape[1:], F32)


def _router(x, norm_g, mod, router_wt, router_b, layer):
    tm = 512
    return pl.pallas_call(
        _router_kernel,
        out_shape=(jax.ShapeDtypeStruct((N_TOK, D_MODEL), F32), jax.ShapeDtypeStruct((N_EXPERTS, N_TOK), F32),
                   jax.ShapeDtypeStruct((8, N_TOK), F32)),
        grid=(N_TOK // tm,),
        in_specs=[pl.BlockSpec((tm, D_MODEL), lambda i: (i, 0)),
                  pl.BlockSpec((None, 1, D_MODEL), lambda i: (layer, 0, 0)),
                  _mod_spec(layer, 3, tm), _mod_spec(layer, 4, tm),
                  _full((N_EXPERTS, D_MODEL)), _full((N_EXPERTS, 1))],
        out_specs=(pl.BlockSpec((tm, D_MODEL), lambda i: (i, 0)), pl.BlockSpec((N_EXPERTS, tm), lambda i: (0, i)),
                   pl.BlockSpec((8, tm), lambda i: (0, i))),
        compiler_params=_params(("parallel",)),
        name="router",
    )(x, norm_g.reshape(DEPTH, 1, D_MODEL), mod, mod, router_wt, router_b.reshape(N_EXPERTS, 1))


MOE_TM = 512
N_PAIRS = N_EXPERT_GROUPS * 6
MOE_STEPS = 2 * (N_PAIRS + N_TOK // MOE_TM - 1)


def _row_copy(src_hbm, src_row, dst_ref, slot, r, sems):
    return pltpu.make_async_copy(src_hbm.at[pl.ds(src_row, 1), :], dst_ref.at[slot, pl.ds(r, 1), :], sems.at[slot])


def _gather_start(src_hbm, dst_ref, slot, idx_ref, base, n, sems):
    def issue(r, carry):
        _row_copy(src_hbm, idx_ref[base + r], dst_ref, slot, r, sems).start()
        return carry

    lax.fori_loop(0, n, issue, 0, unroll=8)


def _gather_wait(src_hbm, dst_ref, slot, n, sems):
    def wait(r, carry):
        _row_copy(src_hbm, 0, dst_ref, slot, r, sems).wait()
        return carry

    lax.fori_loop(0, n, wait, 0, unroll=8)


def _gather_tile(src_hbm, dst_ref, idx_ref, tile, n_tiles, sems):
    slot = tile % 2

    @pl.when(tile == 0)
    def _():
        _gather_start(src_hbm, dst_ref, slot, idx_ref, 0, MOE_TM, sems)

    _gather_wait(src_hbm, dst_ref, slot, MOE_TM, sems)

    @pl.when(tile + 1 < n_tiles)
    def _():
        _gather_start(src_hbm, dst_ref, 1 - slot, idx_ref, (tile + 1) * MOE_TM, MOE_TM, sems)

    return slot


def _moe_plan(ids):
    i32 = jnp.int32
    lo, hi = ids[0].astype(i32), ids[1].astype(i32)
    src = jnp.argsort(lo * N_EXPERTS + hi).astype(i32)
    pos = jnp.argsort(src).astype(i32)
    n_tiles = N_TOK // MOE_TM
    ex = jnp.arange(N_EXPERTS, dtype=i32)
    lo_s, hi_s = lo[src].reshape(n_tiles, MOE_TM, 1), hi[src].reshape(n_tiles, MOE_TM, 1)
    used = ((lo_s == ex).any(axis=1) | (hi_s == ex).any(axis=1)).reshape(-1)
    n_valid = jnp.sum(used).astype(i32)
    idx = jnp.nonzero(used, size=MOE_STEPS, fill_value=0)[0].astype(i32)
    valid = jnp.arange(MOE_STEPS, dtype=i32) < n_valid
    idx = jnp.where(valid, idx, idx[jnp.maximum(n_valid - 1, 0)])
    tile, exp = idx // N_EXPERTS, idx % N_EXPERTS
    first = valid & (tile != jnp.concatenate([jnp.full((1,), -1, i32), tile[:-1]]))
    return src, pos, tile, exp, first.astype(i32), valid.astype(i32)


def _moe_kernel(tile_ref, exp_ref, first_ref, valid_ref, src_ref, xm_hbm, gates_ref, w1_ref, w3_ref, w2_ref,
                o_ref, xs_ref, xb_ref, sem):
    s = pl.program_id(0)

    @pl.when(first_ref[s] == 1)
    def _():
        slot = _gather_tile(xm_hbm, xs_ref, src_ref, tile_ref[s], N_TOK // MOE_TM, sem)
        xb_ref[...] = xs_ref[slot].astype(BF16)
        o_ref[...] = jnp.zeros_like(o_ref)

    @pl.when(valid_ref[s] == 1)
    def _():
        xb = xb_ref[...]
        h1 = _mm(xb, w1_ref[...].astype(BF16))
        h3 = _mm(xb, w3_ref[...].astype(BF16))
        gates = gates_ref[...]
        lane = lax.broadcasted_iota(jnp.int32, gates.shape, 1)
        gate = jnp.sum(jnp.where(lane == exp_ref[s], gates, 0.0), axis=-1, keepdims=True)
        hh = h1 * _sigmoid(h1) * h3 * gate
        o_ref[...] += _mm(hh.astype(BF16), w2_ref[...].astype(BF16))


def _moe(xm, gates_sorted, plan, w1, w3, w2, layer):
    src, _, tile, exp, first, valid = plan
    w_in = pl.BlockSpec((None, None, D_MODEL, D_EXPERT), lambda s, t, e, f, v, i: (layer, e[s], 0, 0))
    grid_spec = pltpu.PrefetchScalarGridSpec(
        num_scalar_prefetch=5, grid=(MOE_STEPS,),
        in_specs=[pl.BlockSpec(memory_space=pl.ANY),
                  pl.BlockSpec((MOE_TM, N_EXPERTS), lambda s, t, e, f, v, i: (t[s], 0)),
                  w_in, w_in,
                  pl.BlockSpec((None, None, D_EXPERT, D_MODEL), lambda s, t, e, f, v, i: (layer, e[s], 0, 0))],
        out_specs=pl.BlockSpec((MOE_TM, D_MODEL), lambda s, t, e, f, v, i: (t[s], 0)),
        scratch_shapes=[pltpu.VMEM((2, MOE_TM, D_MODEL), F32), pltpu.VMEM((MOE_TM, D_MODEL), BF16),
                        pltpu.SemaphoreType.DMA((2,))])
    return pl.pallas_call(
        _moe_kernel, out_shape=jax.ShapeDtypeStruct((N_TOK, D_MODEL), F32), grid_spec=grid_spec,
        compiler_params=_params(("arbitrary",)), name="moe",
    )(tile, exp, first, valid, src, xm, gates_sorted, w1, w3, w2)


def _combine_kernel(pos_ref, acc_hbm, x_ref, g_ref, *refs, split):
    i = pl.program_id(0)
    *out_refs, buf_ref, sem = refs
    slot = _gather_tile(acc_hbm, buf_ref, pos_ref, i, N_TOK // MOE_TM, sem)
    val = x_ref[...] + g_ref[...] * buf_ref[slot]
    if not split:
        out_refs[0][...] = val
        return
    n_prompt_tiles = N_PROMPT // MOE_TM

    @pl.when(i < n_prompt_tiles)
    def _():
        out_refs[0][...] = val

    @pl.when(i >= n_prompt_tiles)
    def _():
        out_refs[1][...] = val


def _combine(acc_sorted, pos, x, mod, layer, split):
    tm = MOE_TM
    n_prompt_tiles = N_PROMPT // tm
    row = pl.BlockSpec((tm, D_MODEL), lambda i, p: (i, 0))
    if split:
        out_shape = (jax.ShapeDtypeStruct((N_PROMPT, D_MODEL), F32), jax.ShapeDtypeStruct((N_SAMPLE, D_MODEL), F32))
        out_specs = (pl.BlockSpec((tm, D_MODEL), lambda i, p: (jnp.minimum(i, n_prompt_tiles - 1), 0)),
                     pl.BlockSpec((tm, D_MODEL), lambda i, p: (jnp.maximum(i - n_prompt_tiles, 0), 0)))
    else:
        out_shape, out_specs = jax.ShapeDtypeStruct((N_TOK, D_MODEL), F32), row
    grid_spec = pltpu.PrefetchScalarGridSpec(
        num_scalar_prefetch=1, grid=(N_TOK // tm,),
        in_specs=[pl.BlockSpec(memory_space=pl.ANY), row,
                  pl.BlockSpec((None, None, None, 1, D_MODEL), lambda i, p: (layer, _mod_row(i, tm), 5, 0, 0))],
        out_specs=out_specs,
        scratch_shapes=[pltpu.VMEM((2, tm, D_MODEL), F32), pltpu.SemaphoreType.DMA((2,))])
    return pl.pallas_call(
        functools.partial(_combine_kernel, split=split), out_shape=out_shape, grid_spec=grid_spec,
        compiler_params=_params(("arbitrary",)), name="combine",
    )(pos, acc_sorted, x, mod)


def _gate_rows(gcols, n_seq, seq):
    return gcols.reshape(n_seq, seq // CHUNK, CHUNK, gcols.shape[-1]).transpose(0, 1, 3, 2)


def kernel(x_prompt, x_sample, cache_na_k, cache_na_v, state_rwkv, state_mlstm_c, state_mlstm_n, state_mlstm_m,
           c, c_ctx, norm1_g, norm2_g, w_mod, b_mod, w_in, conv_w, na_q_g, na_k_g, na_rpb, rw_w0, rw_w_up, rw_a0,
           rw_a_up, rw_g_up, rw_k_k, rw_k_a, rw_r_k, rw_ln_g, rw_ln_b, ml_gate_b, ml_norm_g, w_out, router_w,
           router_b, moe_w1, moe_w3, moe_w2):
    p = dict(rw_w0=rw_w0, rw_w_up=rw_w_up, rw_a0=rw_a0, rw_a_up=rw_a_up, rw_g_up=rw_g_up, rw_k_k=rw_k_k,
             rw_k_a=rw_k_a, rw_r_k=rw_r_k, rw_ln_g=rw_ln_g, rw_ln_b=rw_ln_b, ml_gate_b=ml_gate_b,
             ml_norm_g=ml_norm_g)
    cvecs = jnp.concatenate([c_ctx[None], c, jnp.zeros((MOD_ROWS - 1 - DEC_BATCH, D_MODEL), F32)], axis=0)
    mod = _adaln(cvecs, w_mod, b_mod).reshape(DEPTH, MOD_ROWS, 6, 1, D_MODEL)

    assert w_in.shape[-1] == P_IN
    w_in_t = jnp.swapaxes(w_in, 1, 2)
    tw = _na_tables(na_rpb)
    router_wt = router_w.T
    sample_row_off = N_PROMPT // DEC_SEQ

    x = jnp.concatenate([x_prompt.reshape(N_PROMPT, D_MODEL), x_sample.reshape(N_SAMPLE, D_MODEL)], axis=0)
    kv_prev, rw_prev, ml_prev = (), (), ()
    for l in range(DEPTH):
        u = _inproj(x, norm1_g, mod, w_in_t, l)
        ya_p, yb_p, *kv_prev = _attn_prompt(u, conv_w, na_q_g, na_k_g, l, prev=tuple(kv_prev))
        ya_s, yb_s = _na_sample(u, cache_na_k, cache_na_v, tw, conv_w, na_q_g, na_k_g, l)
        yc_p, *rw_prev = _rwkv(u, p, l, BATCH, SEQ, 0, emit_state=True, prev=tuple(rw_prev))
        (yc_s,) = _rwkv(u, p, l, DEC_BATCH, DEC_SEQ, sample_row_off, s0=state_rwkv)
        g0 = CB_GATE * MIX_W + SM_GATE
        gcols = u[:, g0:g0 + 4 * ML_HEADS]
        yd_p, *ml_prev = _mlstm(u, _gate_rows(gcols[:N_PROMPT], BATCH, SEQ), p, l, BATCH, SEQ, 0,
                                emit_state=True, prev=tuple(ml_prev))
        (yd_s,) = _mlstm(u, _gate_rows(gcols[N_PROMPT:], DEC_BATCH, DEC_SEQ), p, l, DEC_BATCH, DEC_SEQ,
                         sample_row_off, state=(state_mlstm_c, state_mlstm_n, state_mlstm_m))
        x = _outproj((ya_p, yb_p, yc_p, yd_p), (ya_s, yb_s, yc_s, yd_s), w_out, x, mod, l)
        xm, gates_t, ids = _router(x, norm2_g, mod, router_wt, router_b, l)
        plan = _moe_plan(ids)
        acc_sorted = _moe(xm, gates_t.T[plan[0]], plan, moe_w1, moe_w3, moe_w2, l)
        x = _combine(acc_sorted, plan[1], x, mod, l, split=l == DEPTH - 1)
    new_c, new_n, new_m = ml_prev
    return (x[0].reshape(BATCH, SEQ, D_MODEL), x[1].reshape(DEC_BATCH, DEC_SEQ, D_MODEL),
            kv_prev[0], kv_prev[1], rw_prev[0], new_c, new_n.reshape(BATCH, DEPTH, 2, ML_HEADS, ML_DH),
            new_m.reshape(BATCH, DEPTH, 2, ML_HEADS))
```

```python
import functools

import numpy as np
import jax
import jax.numpy as jnp
from jax import lax
from jax.experimental import pallas as pl
from jax.experimental.pallas import tpu as pltpu

F32 = jnp.float32
BF16 = jnp.bfloat16
HI = lax.Precision.HIGHEST

D_MODEL = 2048
BATCH = 16
SEQ = 256
DEPTH = 2
DEC_BATCH = 2
DEC_SEQ = 1024
PAST_LEN = 512
GRID_W = 64
MIX_W = D_MODEL // 4
CONV_K = 3
NA_DH = 64
NA_HEADS = MIX_W // NA_DH
NA_WIN_R = 8
NA_WIN_C = 16
NA_SCALE = NA_DH ** -0.5
ROPE_THETA = 10000.0
RW_DH = 64
RW_HEADS = MIX_W // RW_DH
RW_W_RANK = 64
RW_A_RANK = 64
RW_G_RANK = 128
RW_DECAY_SCALE = 0.606531
RW_GN_EPS = 64e-5
ML_DH = 128
ML_HEADS = MIX_W // ML_DH
N_EXPERTS = 16
N_EXPERT_GROUPS = 4
D_EXPERT = 512
EPS = 1e-6
NEG_INF = -1e30

N_PROMPT = BATCH * SEQ
N_SAMPLE = DEC_BATCH * DEC_SEQ
N_TOK = N_PROMPT + N_SAMPLE
MOD_ROWS = 8
CHUNK = 64
SUB = 16
P1_CHUNKS = 4
NA_ROWS_PER_TRIP = 4
P_IN = 13 * MIX_W + RW_W_RANK + RW_A_RANK + RW_G_RANK + 4 * ML_HEADS
P_BLOCKS = 15
P_PAD = P_BLOCKS * MIX_W
(CB_CVB, CB_CVC, CB_CVH, CB_NAQ, CB_NAK, CB_NAV, CB_RWR, CB_RWK, CB_RWV,
 CB_MLQ, CB_MLK, CB_MLV, CB_MLO, CB_SMALL, CB_GATE) = range(P_BLOCKS)
SM_WL, SM_AL, SM_GL = 0, 64, 128
SM_GATE = MIX_W - 4 * ML_HEADS
VMEM_LIMIT = 56 * 1024 * 1024


def _mm(a, b, prec=None):
    return jnp.dot(a, b, precision=prec, preferred_element_type=F32)


def _mm_nt(a, b, prec=None):
    return lax.dot_general(a, b, (((1,), (1,)), ((), ())), precision=prec, preferred_element_type=F32)


def _sigmoid(x):
    return 1.0 / (1.0 + jnp.exp(-x))


def _full(shape):
    n = len(shape)
    return pl.BlockSpec(shape, lambda *_: (0,) * n)


def _params(sem):
    return pltpu.CompilerParams(dimension_semantics=sem, vmem_limit_bytes=VMEM_LIMIT)


def _mod_row(i, tm):
    n_prompt_tiles = N_PROMPT // tm
    tiles_per_sample = DEC_SEQ // tm
    return jnp.where(i < n_prompt_tiles, 0, 1 + (i - n_prompt_tiles) // tiles_per_sample)


def _mod_spec(layer, chunk, tm, tn=D_MODEL, with_j=False):
    if with_j:
        return pl.BlockSpec((None, None, None, 1, tn), lambda i, j: (layer, _mod_row(i, tm), chunk, 0, j))
    return pl.BlockSpec((None, None, None, 1, tn), lambda i, *_: (layer, _mod_row(i, tm), chunk, 0, 0))


def _tri_masks(n):
    t = lax.broadcasted_iota(jnp.int32, (n, n), 0)
    s = lax.broadcasted_iota(jnp.int32, (n, n), 1)
    incl = ((s <= t).astype(F32), (s >= t).astype(F32))
    strict = ((s < t).astype(F32), (s > t).astype(F32))
    return incl, strict, t, s


def _split2(x):
    hi = x.astype(BF16)
    return hi, (x - hi.astype(F32)).astype(BF16)


def _mm1(a, b):
    return _mm(a.astype(BF16), b.astype(BF16))


def _mm3(a, b):
    a_hi, a_lo = _split2(a)
    b_hi, b_lo = _split2(b)
    return _mm(a_hi, b_lo) + _mm(a_lo, b_hi) + _mm(a_hi, b_hi)


def _split3(x):
    x0 = x.astype(BF16)
    r1 = x - x0.astype(F32)
    x1 = r1.astype(BF16)
    return x0, x1, (r1 - x1.astype(F32)).astype(BF16)


def _mm_exact_lhs(mask_b, x):
    x0, x1, x2 = _split3(x)
    return _mm(mask_b, x2) + _mm(mask_b, x1) + _mm(mask_b, x0)


def _mm_exact_rhs(x, mask_b):
    x0, x1, x2 = _split3(x)
    return _mm(x2, mask_b) + _mm(x1, mask_b) + _mm(x0, mask_b)


def _adaln_kernel(cv_ref, w_ref, b_ref, o_ref):
    cv = cv_ref[...]
    a0, a1, a2 = _split3(cv * _sigmoid(cv))
    w_hi, w_lo = _split2(w_ref[...])
    o_ref[...] = ((_mm(a2, w_hi) + _mm(a1, w_lo)) + (_mm(a1, w_hi) + _mm(a0, w_lo)) + _mm(a0, w_hi)) + b_ref[...]


def _adaln(cvecs, w_mod, b_mod):
    tn = 1024
    n_out = 6 * D_MODEL
    return pl.pallas_call(
        _adaln_kernel,
        out_shape=jax.ShapeDtypeStruct((DEPTH, MOD_ROWS, n_out), F32),
        grid=(DEPTH, n_out // tn),
        in_specs=[_full((MOD_ROWS, D_MODEL)),
                  pl.BlockSpec((None, D_MODEL, tn), lambda l, j: (l, 0, j)),
                  pl.BlockSpec((None, 1, tn), lambda l, j: (l, 0, j))],
        out_specs=pl.BlockSpec((None, MOD_ROWS, tn), lambda l, j: (l, 0, j)),
        compiler_params=_params(("parallel", "parallel")),
        name="adaln",
    )(cvecs, w_mod, b_mod.reshape(DEPTH, 1, n_out))


def _modulated(x, g, sh, sc):
    y = x * lax.rsqrt(jnp.mean(x * x, axis=-1, keepdims=True) + EPS) * g
    return y * (1.0 + sc) + sh


def _premod_kernel(x_ref, g_ref, sh_ref, sc_ref, o_ref):
    o_ref[...] = _modulated(x_ref[...], g_ref[...], sh_ref[...], sc_ref[...]).astype(BF16)


def _premod(x, norm_g, mod, layer):
    tm = 512
    row = pl.BlockSpec((tm, D_MODEL), lambda i: (i, 0))
    return pl.pallas_call(
        _premod_kernel,
        out_shape=jax.ShapeDtypeStruct((N_TOK, D_MODEL), BF16),
        grid=(N_TOK // tm,),
        in_specs=[row, pl.BlockSpec((None, 1, D_MODEL), lambda i: (layer, 0, 0)),
                  _mod_spec(layer, 0, tm), _mod_spec(layer, 1, tm)],
        out_specs=row,
        compiler_params=_params(("parallel",)),
        name="premod",
    )(x, norm_g.reshape(DEPTH, 1, D_MODEL), mod, mod)


def _inproj_kernel(xm_ref, w_ref, o_ref):
    o_ref[...] = _mm_nt(xm_ref[...], w_ref[0].astype(BF16))


def _inproj_src_row(j):
    a = 9
    narrow = RW_W_RANK + RW_A_RANK + RW_G_RANK
    g = 16
    return g * jnp.where(j < a, j * (MIX_W // g),
                         jnp.where(j < CB_SMALL, j * (MIX_W // g) + narrow // g,
                                   jnp.where(j == CB_SMALL, a * MIX_W // g, (P_IN - MIX_W) // g)))


def _inproj(xm, w_in_t, layer):
    tm, tn = 2048, MIX_W
    return pl.pallas_call(
        _inproj_kernel,
        out_shape=jax.ShapeDtypeStruct((N_TOK, P_PAD), F32),
        grid=(N_TOK // tm, P_BLOCKS),
        in_specs=[pl.BlockSpec((tm, D_MODEL), lambda i, j: (i, 0)),
                  pl.BlockSpec((pl.Element(1), pl.Element(tn), pl.Element(D_MODEL)),
                               lambda i, j: (layer, _inproj_src_row(j), 0))],
        out_specs=pl.BlockSpec((tm, tn), lambda i, j: (i, j)),
        compiler_params=_params(("parallel", "parallel")),
        name="inproj",
    )(xm, w_in_t)


def _conv_mix(b, c, h, w):
    u = c * h
    n = u.shape[0]
    row = lax.broadcasted_iota(jnp.int32, u.shape, 0)
    prev = jnp.where(row == 0, 0.0, pltpu.roll(u, 1, axis=0))
    nxt = jnp.where(row == n - 1, 0.0, pltpu.roll(u, n - 1, axis=0))
    return b * (prev * w[0:1] + u * w[1:2] + nxt * w[2:3])


def _head_rms(x, g):
    return x * lax.rsqrt(jnp.mean(x * x, axis=-1, keepdims=True) + EPS) * g


def _attn_prompt_kernel(cb_ref, cc_ref, ch_ref, q_ref, k_ref, v_ref, cw_ref, qg_ref, kg_ref,
                        ya_ref, yb_ref, nk_ref, nv_ref):
    ya_ref[...] = _conv_mix(cb_ref[...], cc_ref[...], ch_ref[...], cw_ref[...]).astype(ya_ref.dtype)
    sls = [slice(h * NA_DH, (h + 1) * NA_DH) for h in range(NA_HEADS)]
    qn = [_head_rms(q_ref[:, sl], qg_ref[...]) * NA_SCALE for sl in sls]
    kn = [_head_rms(k_ref[:, sl], kg_ref[...]) for sl in sls]
    vh = [v_ref[:, sl] for sl in sls]
    s = [_mm_nt(q.astype(BF16), k.astype(BF16)) for q, k in zip(qn, kn)]
    p = [jnp.exp(x - jnp.max(x, axis=-1, keepdims=True)) for x in s]
    o = [_mm(x.astype(BF16), v.astype(BF16)) / jnp.sum(x, axis=-1, keepdims=True) for x, v in zip(p, vh)]
    yb_ref[...] = jnp.concatenate(o, axis=1).astype(yb_ref.dtype)
    for h in range(NA_HEADS):
        nk_ref[h] = kn[h]
        nv_ref[h] = vh[h]


def _u_spec(rows, col_block, row_off_blocks=0):
    return pl.BlockSpec((rows, MIX_W), lambda b: (b + row_off_blocks, col_block))


def _carry_through(kernel, n_inputs, prev, n_plain_out, n_state_out, layer):
    n_prev = len(prev)

    def body(*refs):
        ins, rest = refs[:n_inputs], list(refs[n_inputs + n_prev:])
        if not n_prev:
            for i in range(n_plain_out, n_plain_out + n_state_out):
                full = rest[i]
                for d in range(DEPTH):
                    if d != layer:
                        full[d] = jnp.zeros(full.shape[1:], full.dtype)
                rest[i] = full.at[layer]
        return kernel(*ins, *rest)

    return body, [pl.BlockSpec(memory_space=pl.ANY)] * n_prev


def _state_spec(shape, layer, first, index):
    def index_map(*g):
        b, *tail = index(*g)
        return (b, 0 if first else layer, *tail)
    return pl.BlockSpec((None, DEPTH if first else None) + tuple(shape), index_map)


def _attn_prompt(u, conv_w, q_g, k_g, layer, prev=(), n_seq=BATCH, seq=SEQ):
    lw = lambda shape: pl.BlockSpec((None,) + shape, lambda b: (layer,) + (0,) * len(shape))
    y_spec = pl.BlockSpec((seq, MIX_W), lambda b: (b, 0))
    kv_spec = _state_spec((NA_HEADS, seq, NA_DH), layer, not prev, lambda b: (b, 0, 0, 0))
    in_specs = ([_u_spec(seq, cb) for cb in (CB_CVB, CB_CVC, CB_CVH, CB_NAQ, CB_NAK, CB_NAV)]
                + [lw((CONV_K, MIX_W)), lw((1, NA_DH)), lw((1, NA_DH))])
    body, prev_specs = _carry_through(_attn_prompt_kernel, len(in_specs), prev, 2, 2, layer)
    return pl.pallas_call(
        body,
        out_shape=(jax.ShapeDtypeStruct((n_seq * seq, MIX_W), BF16),) * 2
        + (jax.ShapeDtypeStruct((n_seq, DEPTH, NA_HEADS, seq, NA_DH), F32),) * 2,
        grid=(n_seq,),
        in_specs=in_specs + prev_specs,
        out_specs=(y_spec, y_spec, kv_spec, kv_spec),
        input_output_aliases={len(in_specs) + i: 2 + i for i in range(len(prev))},
        compiler_params=_params(("parallel",)),
        name="attn_prompt",
    )(u, u, u, u, u, u, conv_w, q_g.reshape(DEPTH, 1, NA_DH), k_g.reshape(DEPTH, 1, NA_DH), *prev)


def _na_kernel(cb_ref, cc_ref, ch_ref, q_ref, k_ref, v_ref, kc_ref, vc_ref, tw_ref, cos_ref, sin_ref,
               perm_ref, cw_ref, qg_ref, kg_ref, ya_ref, yb_ref, qs_ref, ks_ref, tws_ref):
    rows = DEC_SEQ // GRID_W
    wr = min(NA_WIN_R, rows)
    nw = wr * GRID_W
    ya_ref[...] = _conv_mix(cb_ref[...], cc_ref[...], ch_ref[...], cw_ref[...]).astype(ya_ref.dtype)
    cos, sin, perm_b = cos_ref[...], sin_ref[...], perm_ref[...].astype(BF16)

    def rope(x):
        hi, lo = _split2(x)
        return x * cos + (_mm(hi, perm_b) + _mm(lo, perm_b)) * sin

    for h in range(NA_HEADS):
        sl = slice(h * NA_DH, (h + 1) * NA_DH)
        qs_ref[...] = (rope(_head_rms(q_ref[:, sl], qg_ref[...])) * NA_SCALE).astype(BF16)
        ks_ref[...] = rope(_head_rms(k_ref[:, sl], kg_ref[...])).astype(BF16)
        kch = kc_ref[h].astype(BF16)
        vch = vc_ref[h].astype(BF16)
        for p in range(wr):
            tws_ref[p] = jnp.concatenate([tw_ref[h, j - p + NA_WIN_R - 1] for j in range(wr)], axis=1)

        def rows_step(t, carry):
            rr = [t * NA_ROWS_PER_TRIP + i for i in range(NA_ROWS_PER_TRIP)]
            rs = [jnp.clip(r - wr // 2, 0, rows - wr) for r in rr]
            q0 = [pl.multiple_of(r * GRID_W, GRID_W) for r in rr]
            k0 = [pl.multiple_of(x * GRID_W, GRID_W) for x in rs]
            q_r = [qs_ref[pl.ds(x, GRID_W), :] for x in q0]
            s_w = [_mm_nt(q, ks_ref[pl.ds(k, nw), :]) + tws_ref[r - x] for q, k, r, x in zip(q_r, k0, rr, rs)]
            s_c = [_mm_nt(q, kch) for q in q_r]
            m = [jnp.maximum(jnp.max(a, axis=-1, keepdims=True), jnp.max(b, axis=-1, keepdims=True))
                 for a, b in zip(s_w, s_c)]
            p_w = [jnp.exp(a - x) for a, x in zip(s_w, m)]
            p_c = [jnp.exp(b - x) for b, x in zip(s_c, m)]
            den = [jnp.sum(a, axis=-1, keepdims=True) + jnp.sum(b, axis=-1, keepdims=True) for a, b in zip(p_w, p_c)]
            v_w = [v_ref[pl.ds(k, nw), sl].astype(BF16) for k in k0]
            o = [(_mm(a.astype(BF16), v) + _mm(b.astype(BF16), vch)) / d for a, b, v, d in zip(p_w, p_c, v_w, den)]
            for x, val in zip(q0, o):
                yb_ref[pl.ds(x, GRID_W), sl] = val.astype(yb_ref.dtype)
            return carry

        lax.fori_loop(0, rows // NA_ROWS_PER_TRIP, rows_step, 0)


def _na_tables(rpb):
    qc = np.arange(GRID_W)
    kc = np.arange(GRID_W)
    wstart = np.clip(qc - NA_WIN_C // 2, 0, GRID_W - NA_WIN_C)
    colmask = (kc[None, :] >= wstart[:, None]) & (kc[None, :] < wstart[:, None] + NA_WIN_C)
    dc = np.clip(kc[None, :] - qc[:, None], -(NA_WIN_C - 1), NA_WIN_C - 1) + NA_WIN_C - 1
    pick = (np.arange(2 * NA_WIN_C - 1)[:, None] == dc.reshape(1, -1)).astype(np.float32)
    bias = jnp.einsum('lhrd,dn->lhrn', rpb, jnp.asarray(pick), precision=HI)
    return jnp.where(colmask[None, None, None], bias.reshape(rpb.shape[:3] + dc.shape), NEG_INF)


def _rope_tables():
    t = np.arange(DEC_SEQ)
    quarter = NA_DH // 4
    freq = ROPE_THETA ** (-np.arange(quarter, dtype=np.float32) / quarter)
    ang_r = (t // GRID_W).astype(np.float32)[:, None] * freq
    ang_c = (t % GRID_W).astype(np.float32)[:, None] * freq
    cos = np.concatenate([np.cos(ang_r), np.cos(ang_r), np.cos(ang_c), np.cos(ang_c)], axis=-1)
    sin = np.concatenate([-np.sin(ang_r), np.sin(ang_r), -np.sin(ang_c), np.sin(ang_c)], axis=-1)
    src = np.concatenate([np.arange(quarter) + quarter, np.arange(quarter),
                          np.arange(quarter) + 3 * quarter, np.arange(quarter) + 2 * quarter])
    perm = np.zeros((NA_DH, NA_DH), np.float32)
    perm[src, np.arange(NA_DH)] = 1.0
    return cos.astype(np.float32), sin.astype(np.float32), perm


def _na_sample(u, cache_k, cache_v, tw, conv_w, q_g, k_g, layer, n_seq=DEC_BATCH, row_off=N_PROMPT // DEC_SEQ):
    cos, sin, perm = _rope_tables()
    lw = lambda shape: pl.BlockSpec((None,) + shape, lambda b: (layer,) + (0,) * len(shape))
    y_spec = pl.BlockSpec((DEC_SEQ, MIX_W), lambda b: (b, 0))
    c_spec = pl.BlockSpec((None, None, NA_HEADS, PAST_LEN, NA_DH), lambda b: (b, layer, 0, 0, 0))
    wr = min(NA_WIN_R, DEC_SEQ // GRID_W)
    n_off = 2 * NA_WIN_R - 1
    return pl.pallas_call(
        _na_kernel,
        out_shape=(jax.ShapeDtypeStruct((n_seq * DEC_SEQ, MIX_W), BF16),) * 2,
        grid=(n_seq,),
        in_specs=[_u_spec(DEC_SEQ, cb, row_off) for cb in (CB_CVB, CB_CVC, CB_CVH, CB_NAQ, CB_NAK, CB_NAV)]
        + [c_spec, c_spec, lw((NA_HEADS, n_off, GRID_W, GRID_W)),
           _full((DEC_SEQ, NA_DH)), _full((DEC_SEQ, NA_DH)), _full((NA_DH, NA_DH)),
           lw((CONV_K, MIX_W)), lw((1, NA_DH)), lw((1, NA_DH))],
        out_specs=(y_spec, y_spec),
        scratch_shapes=[pltpu.VMEM((DEC_SEQ, NA_DH), BF16), pltpu.VMEM((DEC_SEQ, NA_DH), BF16),
                        pltpu.VMEM((wr, GRID_W, wr * GRID_W), F32)],
        compiler_params=_params(("parallel",)),
        name="na_sample",
    )(u, u, u, u, u, u, cache_k, cache_v, tw, jnp.asarray(cos), jnp.asarray(sin), jnp.asarray(perm),
      conv_w, q_g.reshape(DEPTH, 1, NA_DH), k_g.reshape(DEPTH, 1, NA_DH))


def _seg_ones(width, seg):
    a = lax.broadcasted_iota(jnp.int32, (width, width), 0) // seg
    b = lax.broadcasted_iota(jnp.int32, (width, width), 1) // seg
    return (a == b).astype(F32)


def _rwkv_kernel(*refs, seq, has_s0, emit_state):
    it = iter(refs)
    r_ref, k_ref, v_ref, sm_ref = (next(it) for _ in range(4))
    (w0_ref, wup_ref, a0_ref, aup_ref, gup_ref, kkp_ref, kap_ref, rkp_ref, lng_ref, lnb_ref) = (
        next(it) for _ in range(10))
    mask_ref = next(it)
    s0_ref = next(it) if has_s0 else None
    y_ref = next(it)
    so_ref = next(it) if emit_state else None
    kk_ref, lw_ref, ka_ref, kd_ref, coef_ref, ysp_ref = (next(it) for _ in range(6))

    dh = RW_DH
    pw = 2 * dh
    nc = seq // CHUNK
    seg_b = _seg_ones(pw, dh).astype(BF16)

    r = r_ref[...]
    k = k_ref[...]
    sm = sm_ref[...]
    wl = jnp.tanh(sm[:, SM_WL:SM_WL + RW_W_RANK])
    al = sm[:, SM_AL:SM_AL + RW_A_RANK]
    kk = k * kkp_ref[...]
    kk = kk * lax.rsqrt(_mm_exact_rhs(kk * kk, seg_b) + EPS)
    kk_ref[...] = kk
    for z in range(2):
        lw_ref[z] = -RW_DECAY_SCALE * _sigmoid(w0_ref[z:z + 1, :] + _mm3(wl, wup_ref[z]))
        a = _sigmoid(a0_ref[z:z + 1, :] + _mm3(al, aup_ref[z]))
        ka_ref[z] = kk * a
        kd_ref[z] = k * (1.0 + (a - 1.0) * kap_ref[...])

    cat = jnp.concatenate

    def phase1(c2, carry):
        incl, strict, ti, si = _tri_masks(CHUNK)
        incl_b = (mask_ref[0], mask_ref[1])
        strict2 = tuple(cat([m, m], axis=1) for m in strict)
        incl2 = tuple(cat([m, m], axis=1) for m in incl)
        diag_blk = (ti // SUB == si // SUB).astype(F32)
        eye = (ti == si).astype(F32)
        zero_pair = jnp.zeros((CHUNK, pw), F32)
        head_of_lane = lax.broadcasted_iota(jnp.int32, (CHUNK, pw), 1) // dh
        head_of_lane2 = (lax.broadcasted_iota(jnp.int32, (CHUNK, 2 * pw), 1) // dh) % 2
        r_pw = lax.broadcasted_iota(jnp.int32, (pw, pw), 0)
        c_pw = lax.broadcasted_iota(jnp.int32, (pw, pw), 1)
        same_head = r_pw // dh == c_pw // dh
        eye_pw = (r_pw == c_pw).astype(F32)
        chains = []
        for cc in range(P1_CHUNKS):
            rows = pl.ds(pl.multiple_of((c2 * P1_CHUNKS + cc) * CHUNK, CHUNK), CHUNK)
            vc, rc, kkc = v_ref[rows, :], r_ref[rows, :], kk_ref[rows, :]
            for z in range(2):
                lwc = lw_ref[z, rows, :]
                cum = _mm_exact_lhs(incl_b[z], lwc)
                tot = cum[CHUNK - 1:CHUNK] if z == 0 else cum[0:1]
                e_neg = jnp.exp(-cum)
                dec = jnp.exp(tot - cum)
                kac, kdc = ka_ref[z, rows, :], kd_ref[z, rows, :]
                chains.append((z, kkc * jnp.exp(cum - lwc), rc * jnp.exp(cum), vc, kac * e_neg, kdc * e_neg,
                               kac * dec, kdc * dec, jnp.exp(tot)))
        heads = [(q, j) for q in range(len(chains)) for j in range(2)]
        zq = [ch[0] for ch in chains]
        ymat = [cat([ch[4], ch[5]], axis=0).astype(BF16) for ch in chains]
        vz = [cat([zero_pair, ch[3]], axis=0) for ch in chains]
        nvz = [cat([zero_pair, -ch[3]], axis=1) for ch in chains]
        x_in = [cat([jnp.where(head_of_lane == j, chains[q][1], 0.0),
                     jnp.where(head_of_lane == j, chains[q][2], 0.0)], axis=0) for q, j in heads]
        aa = [_mm_nt(x.astype(BF16), ymat[q]) for x, (q, _) in zip(x_in, heads)]
        top = [a[0:CHUNK] * strict2[zq[q]] for a, (q, _) in zip(aa, heads)]
        a_r = [a[CHUNK:] * incl2[zq[q]] for a, (q, _) in zip(aa, heads)]
        akv = [_mm1(t, vz[q]) for t, (q, _) in zip(top, heads)]
        low = [t[:, 0:CHUNK] for t in top]
        ld = [x * diag_blk for x in low]
        lo = [x - y for x, y in zip(low, ld)]
        l2 = [_mm1(x, x) for x in ld]
        l4 = [_mm1(x, x) for x in l2]
        l8 = [_mm1(x, x) for x in l4]
        td = [eye - x for x in ld]
        for lp in (l2, l4, l8):
            td = [t + _mm1(t, p_) for t, p_ in zip(td, lp)]
        x0 = [_mm1(t, cat([chains[q][1], a, l], axis=1)) for t, a, l, (q, _) in zip(td, akv, lo, heads)]
        pq0 = [x[:, 0:2 * pw] for x in x0]
        wm = [x[:, 2 * pw:] for x in x0]
        pq = pq0
        for _ in range(CHUNK // SUB - 1):
            pq = [p0 - _mm1(w, p_) for p0, w, p_ in zip(pq0, wm, pq)]
        ryc = [_mm1(a, cat([p_, nvz[q]], axis=0)) for a, p_, (q, _) in zip(a_r, pq, heads)]
        for q, ch in enumerate(chains):
            pq_m = jnp.where(head_of_lane2 == 0, pq[2 * q], pq[2 * q + 1])
            ryc_m = jnp.where(head_of_lane2 == 0, ryc[2 * q], ryc[2 * q + 1])
            g1 = _mm1(ch[6].T, pq_m)
            g2 = _mm1(ch[7].T, ch[3])
            g_t = eye_pw * ch[8] - jnp.where(same_head, g1[:, 0:pw], 0.0)
            h_t = jnp.where(same_head, g2 - g1[:, pw:], 0.0)
            coef_ref[c2 * P1_CHUNKS + q // 2, q % 2] = cat(
                [g_t, ch[2] - ryc_m[:, 0:pw], h_t, -ryc_m[:, pw:]], axis=0)
        return carry

    lax.fori_loop(0, nc // P1_CHUNKS, phase1, 0)

    def block_diag(a, b):
        zero = jnp.zeros((dh, dh), F32)
        return cat([cat([a, zero], axis=1), cat([zero, b], axis=1)], axis=0)

    m_init = tuple((block_diag(s0_ref[z, 0].T, s0_ref[z, 1].T) if has_s0 else jnp.zeros((pw, pw), F32))
                   for z in range(2))
    n_lhs = pw + CHUNK

    def phase2(ci, ms):
        new_ms, ys = [], []
        for z in range(2):
            c = ci if z == 0 else nc - 1 - ci
            out = _mm3(coef_ref[c, z, 0:n_lhs, :], ms[z]) + coef_ref[c, z, n_lhs:, :]
            new_ms.append(out[0:pw])
            ys.append(out[pw:])
        ysp_ref[ci] = cat(ys, axis=1)
        return tuple(new_ms)

    m_fin = lax.fori_loop(0, nc, phase2, m_init)

    if emit_state:
        for z in range(2):
            for j in range(2):
                so_ref[z, j] = m_fin[z][j * dh:(j + 1) * dh, j * dh:(j + 1) * dh].T

    y = cat([ysp_ref[c, :, 0:pw] + ysp_ref[nc - 1 - c, :, pw:] for c in range(nc)], axis=0)
    mu = _mm_exact_rhs(y, seg_b) * (1.0 / dh)
    yc = y - mu
    var = _mm_exact_rhs(yc * yc, seg_b) * (1.0 / dh)
    yn = yc * lax.rsqrt(var + RW_GN_EPS) * lng_ref[...] + lnb_ref[...]
    bonus = _mm_exact_rhs(r * k * rkp_ref[...], seg_b) * v_ref[...]
    g = _mm3(_sigmoid(sm[:, SM_GL:SM_GL + RW_G_RANK]), gup_ref[...])
    y_ref[...] = ((yn + bonus) * g).astype(y_ref.dtype)


def _rwkv(u, p, layer, n_seq, seq, row_off, s0=None, emit_state=False, prev=()):
    n_pairs = RW_HEADS // 2
    pw = 2 * RW_DH
    bpc = MIX_W // pw
    lw = lambda shape: pl.BlockSpec((None,) + shape, lambda b, hp: (layer,) + (0,) * (len(shape) - 1) + (hp,))
    row = lambda a: a.reshape(DEPTH, 1, MIX_W)
    u_pair = lambda cb: pl.BlockSpec((seq, pw), lambda b, hp: (b + row_off, cb * bpc + hp))
    in_specs = [u_pair(CB_RWR), u_pair(CB_RWK), u_pair(CB_RWV),
                pl.BlockSpec((seq, MIX_W), lambda b, hp: (b + row_off, CB_SMALL)),
                lw((2, pw)), lw((2, RW_W_RANK, pw)), lw((2, pw)), lw((2, RW_A_RANK, pw)),
                lw((RW_G_RANK, pw))] + [lw((1, pw))] * 5 + [_full((2, CHUNK, CHUNK))]
    t_idx = np.arange(CHUNK)
    incl_masks = np.stack([t_idx[None, :] <= t_idx[:, None], t_idx[None, :] >= t_idx[:, None]])
    args = [u, u, u, u, p['rw_w0'], p['rw_w_up'], p['rw_a0'], p['rw_a_up'], p['rw_g_up'],
            row(p['rw_k_k']), row(p['rw_k_a']), row(p['rw_r_k']), row(p['rw_ln_g']), row(p['rw_ln_b']),
            jnp.asarray(incl_masks, BF16)]
    st_blk = (2, 2, RW_DH, RW_DH)
    if s0 is not None:
        in_specs.append(pl.BlockSpec((None, None) + st_blk, lambda b, hp: (b, layer, 0, hp, 0, 0)))
        args.append(s0)
    out_shape = [jax.ShapeDtypeStruct((n_seq * seq, MIX_W), BF16)]
    out_specs = [pl.BlockSpec((seq, pw), lambda b, hp: (b, hp))]
    if emit_state:
        out_shape.append(jax.ShapeDtypeStruct((n_seq, DEPTH, 2, RW_HEADS, RW_DH, RW_DH), F32))
        out_specs.append(_state_spec(st_blk, layer, not prev, lambda b, hp: (b, 0, hp, 0, 0)))
    nc = seq // CHUNK
    tok = lambda n: pltpu.VMEM((n, seq, pw) if n else (seq, pw), F32)
    body, prev_specs = _carry_through(
        functools.partial(_rwkv_kernel, seq=seq, has_s0=s0 is not None, emit_state=emit_state), len(in_specs), prev,
        1, 1 if emit_state else 0, layer)
    return pl.pallas_call(
        body,
        out_shape=tuple(out_shape), grid=(n_seq, n_pairs), in_specs=in_specs + prev_specs,
        out_specs=tuple(out_specs),
        input_output_aliases={len(in_specs) + i: 1 + i for i in range(len(prev))},
        scratch_shapes=[tok(0), tok(2), tok(2), tok(2),
                        pltpu.VMEM((nc, 2, 2 * (pw + CHUNK), pw), F32), pltpu.VMEM((nc, CHUNK, 2 * pw), F32)],
        compiler_params=_params(("parallel", "parallel")),
        name=f"rwkv_{seq}",
    )(*args, *prev)


def _log_sigmoid(x):
    return jnp.minimum(x, 0.0) - jnp.log(1.0 + jnp.exp(-jnp.abs(x)))


def _mlstm_kernel(*refs, seq, has_s0, emit_state):
    it = iter(refs)
    q_ref, k_ref, v_ref, o_ref, sm_ref, gr_ref, bc_ref, br_ref, ng_ref = (next(it) for _ in range(9))
    c0_ref, n0_ref, m0_ref = ((next(it), next(it), next(it)) if has_s0 else (None, None, None))
    y_ref = next(it)
    co_ref, no_ref, mo_ref = ((next(it), next(it), next(it)) if emit_state else (None, None, None))
    hs_ref, c_ref, n_ref, m_ref = (next(it) for _ in range(4))

    nc = seq // CHUNK
    n_st = 2 * ML_HEADS
    for i in range(n_st):
        z, h = divmod(i, ML_HEADS)
        c_ref[i] = c0_ref[z, h] if has_s0 else jnp.zeros((ML_DH, ML_DH), F32)
        n_ref[i] = n0_ref[z, h] if has_s0 else jnp.zeros((1, ML_DH), F32)
        m_ref[i] = m0_ref[z, h] if has_s0 else jnp.zeros((1, 1), F32)

    incl, _, ti, si = _tri_masks(CHUNK)
    before = ((si <= ti), (si >= ti))

    nh = ML_HEADS
    chains = [(z, h) for z in range(2) for h in range(nh)]
    each = lambda f, *xs: [f(*a) for a in zip(*xs)]
    zs = [z for z, _ in chains]

    def chunk_step(ci, carry):
        per_dir = []
        for z in range(2):
            c = ci if z == 0 else nc - 1 - ci
            rows = pl.ds(pl.multiple_of(c * CHUNK, CHUNK), CHUNK)
            g0 = SM_GATE + z * 2 * nh
            gc = sm_ref[rows, g0:g0 + 2 * nh] + bc_ref[:, z * 2 * nh:(z + 1) * 2 * nh]
            gr = gr_ref[c, z * 2 * nh:(z + 1) * 2 * nh, :] + br_ref[z * 2 * nh:(z + 1) * 2 * nh, :]
            b_cols = _mm(incl[z], _log_sigmoid(gc[:, nh:]), HI)
            b_rows = _mm(_log_sigmoid(gr[nh:]), incl[1 - z], HI)
            per_dir.append((rows, gc[:, :nh], b_cols, gr[:nh], b_rows))
        rows = [per_dir[z][0] for z, _ in chains]
        hsl = [slice(h * ML_DH, (h + 1) * ML_DH) for _, h in chains]
        i_col = [per_dir[z][1][:, h:h + 1] for z, h in chains]
        b_col = [per_dir[z][2][:, h:h + 1] for z, h in chains]
        i_row = [per_dir[z][3][h:h + 1] for z, h in chains]
        b_row = [per_dir[z][4][h:h + 1] for z, h in chains]
        b_last = each(lambda b, z: b[CHUNK - 1:CHUNK] if z == 0 else b[0:1], b_col, zs)
        m_old = [m_ref[i] for i in range(n_st)]
        cm = [c_ref[i] for i in range(n_st)]
        nv = [n_ref[i] for i in range(n_st)]
        qc = each(lambda r, s: q_ref[r, s] * (ML_DH ** -0.5), rows, hsl)
        kc = each(lambda r, s: k_ref[r, s], rows, hsl)
        vc = each(lambda r, s: v_ref[r, s], rows, hsl)
        qk = each(lambda q, k: _mm_nt(q.astype(BF16), k.astype(BF16)), qc, kc)
        qcm = each(lambda q, c_: _mm_nt(q.astype(BF16), c_.astype(BF16)), qc, cm)
        a_t = each(lambda b, m: b + m, b_col, m_old)
        dmat = each(lambda bc, brw, ir, z: jnp.where(before[z], bc - brw + ir, NEG_INF), b_col, b_row, i_row, zs)
        m_t = each(lambda a, d: jnp.maximum(a, jnp.max(d, axis=-1, keepdims=True)), a_t, dmat)
        s = each(lambda x, d, m: x * jnp.exp(d - m), qk, dmat, m_t)
        inter = each(lambda a, m: jnp.exp(a - m), a_t, m_t)
        sv = each(_mm1, s, vc)
        g_col = each(lambda bl, bc, ic: bl - bc + ic, b_last, b_col, i_col)
        a_l = each(lambda bl, m: bl + m, b_last, m_old)
        m_new = each(lambda a, g: jnp.maximum(a, jnp.max(g, axis=0, keepdims=True)), a_l, g_col)
        wgt = each(lambda g, m: jnp.exp(g - m), g_col, m_new)
        vk = each(lambda v, w, k: _mm1((v * w).T, k), vc, wgt, kc)
        decay = each(lambda a, m: jnp.exp(a - m), a_l, m_new)
        num = each(lambda i_, x, y: i_ * x + y, inter, qcm, sv)
        den = each(lambda i_, q, n_, s_: i_ * jnp.sum(q * n_, axis=-1, keepdims=True)
                   + jnp.sum(s_, axis=-1, keepdims=True), inter, qc, nv, s)
        hh = each(lambda n_, d, m: n_ / jnp.maximum(jnp.abs(d), jnp.exp(-m)), num, den, m_t)
        for z in range(2):
            hs_ref[z, per_dir[z][0], :] = jnp.concatenate(hh[z * nh:(z + 1) * nh], axis=1)
        for i in range(n_st):
            c_ref[i] = decay[i] * cm[i] + vk[i]
            n_ref[i] = decay[i] * nv[i] + jnp.sum(wgt[i] * kc[i], axis=0, keepdims=True)
            m_ref[i] = m_new[i]
        return carry

    lax.fori_loop(0, nc, chunk_step, 0)

    if emit_state:
        for i in range(n_st):
            z, h = divmod(i, ML_HEADS)
            co_ref[z, h] = c_ref[i]
            no_ref[z, h] = n_ref[i]
            mo_ref[z, h] = m_ref[i]

    for h in range(ML_HEADS):
        hsl = slice(h * ML_DH, (h + 1) * ML_DH)
        hn = _head_rms(hs_ref[0, :, hsl] + hs_ref[1, :, hsl], ng_ref[:, hsl])
        y_ref[:, hsl] = (_sigmoid(o_ref[:, hsl]) * hn).astype(y_ref.dtype)


def _mlstm(u, gate_rows, p, layer, n_seq, seq, row_off, state=None, emit_state=False, prev=()):
    lw = lambda shape: pl.BlockSpec((None,) + shape, lambda b: (layer,) + (0,) * len(shape))
    nc = seq // CHUNK
    n_gate = 4 * ML_HEADS
    in_specs = [_u_spec(seq, cb, row_off) for cb in (CB_MLQ, CB_MLK, CB_MLV, CB_MLO, CB_GATE)] + [
        pl.BlockSpec((None, nc, n_gate, CHUNK), lambda b: (b, 0, 0, 0)),
        lw((1, n_gate)), lw((n_gate, 1)), lw((1, MIX_W))]
    args = [u, u, u, u, u, gate_rows, p['ml_gate_b'].reshape(DEPTH, 1, n_gate),
            p['ml_gate_b'].reshape(DEPTH, n_gate, 1), p['ml_norm_g'].reshape(DEPTH, 1, MIX_W)]
    c_shape, n_shape, m_shape = (2, ML_HEADS, ML_DH, ML_DH), (2, ML_HEADS, 1, ML_DH), (2, ML_HEADS, 1, 1)
    if state is not None:
        c0, n0, m0 = state
        for a, shp in ((c0, c_shape), (n0, n_shape), (m0, m_shape)):
            in_specs.append(pl.BlockSpec((None, None) + shp, lambda b: (b, layer, 0, 0, 0, 0)))
            args.append(a.reshape(a.shape[:2] + shp))
    out_shape = [jax.ShapeDtypeStruct((n_seq * seq, MIX_W), BF16)]
    out_specs = [pl.BlockSpec((seq, MIX_W), lambda b: (b, 0))]
    if emit_state:
        for shp in (c_shape, n_shape, m_shape):
            out_shape.append(jax.ShapeDtypeStruct((n_seq, DEPTH) + shp, F32))
            out_specs.append(_state_spec(shp, layer, not prev, lambda b: (b, 0, 0, 0, 0)))
    n_st = 2 * ML_HEADS
    body, prev_specs = _carry_through(
        functools.partial(_mlstm_kernel, seq=seq, has_s0=state is not None, emit_state=emit_state),
        len(in_specs), prev, 1, 3 if emit_state else 0, layer)
    return pl.pallas_call(
        body,
        out_shape=tuple(out_shape), grid=(n_seq,), in_specs=in_specs + prev_specs, out_specs=tuple(out_specs),
        input_output_aliases={len(in_specs) + i: 1 + i for i in range(len(prev))},
        scratch_shapes=[pltpu.VMEM((2, seq, MIX_W), F32), pltpu.VMEM((n_st, ML_DH, ML_DH), F32),
                        pltpu.VMEM((n_st, 1, ML_DH), F32), pltpu.VMEM((n_st, 1, 1), F32)],
        compiler_params=_params(("parallel",)),
        name=f"mlstm_{seq}",
    )(*args, *prev)


def _outproj_kernel(*refs, n_prompt_tiles):
    n_mix = 4
    yp_refs, ys_refs = refs[:n_mix], refs[n_mix:2 * n_mix]
    w_ref, x_ref, g_ref, o_ref = refs[2 * n_mix:]
    is_prompt = pl.program_id(0) < n_prompt_tiles
    acc = None
    for i, (yp_ref, ys_ref) in enumerate(zip(yp_refs, ys_refs)):
        y = jnp.where(is_prompt, yp_ref[...], ys_ref[...])
        part = _mm(y, w_ref[i * MIX_W:(i + 1) * MIX_W, :].astype(BF16))
        acc = part if acc is None else acc + part
    o_ref[...] = x_ref[...] + g_ref[...] * acc


def _outproj(ys_prompt, ys_sample, w_out, x, mod, layer):
    tm, tn = 1024, 1024
    n_prompt_tiles = N_PROMPT // tm
    yp_spec = pl.BlockSpec((tm, MIX_W), lambda i, j: (jnp.minimum(i, n_prompt_tiles - 1), 0))
    ys_spec = pl.BlockSpec((tm, MIX_W), lambda i, j: (jnp.maximum(i - n_prompt_tiles, 0), 0))
    return pl.pallas_call(
        functools.partial(_outproj_kernel, n_prompt_tiles=n_prompt_tiles),
        out_shape=jax.ShapeDtypeStruct((N_TOK, D_MODEL), F32),
        grid=(N_TOK // tm, D_MODEL // tn),
        in_specs=[yp_spec] * 4 + [ys_spec] * 4 + [pl.BlockSpec((None, D_MODEL, tn), lambda i, j: (layer, 0, j)),
                                                   pl.BlockSpec((tm, tn), lambda i, j: (i, j)),
                                                   _mod_spec(layer, 2, tm, tn, with_j=True)],
        out_specs=pl.BlockSpec((tm, tn), lambda i, j: (i, j)),
        compiler_params=_params(("parallel", "parallel")),
        name="outproj",
    )(*ys_prompt, *ys_sample, w_out, x, mod)


def _router_kernel(x_ref, g_ref, sh_ref, sc_ref, rw_ref, rb_ref, xm_ref, gt_ref, ids_ref):
    xm = _modulated(x_ref[...], g_ref[...], sh_ref[...], sc_ref[...])
    xm_ref[...] = xm
    logits = _mm_nt(rw_ref[...], xm, HI)
    ex = jnp.exp(logits - jnp.max(logits, axis=0, keepdims=True))
    scores = ex / jnp.sum(ex, axis=0, keepdims=True)
    sel = scores + rb_ref[...]
    per = N_EXPERTS // N_EXPERT_GROUPS
    s = [sel[e:e + 1, :] for e in range(N_EXPERTS)]
    grp_score = []
    for g in range(N_EXPERT_GROUPS):
        a, b, c, d = s[per * g:per * (g + 1)]
        hi1, lo1, hi2, lo2 = jnp.maximum(a, b), jnp.minimum(a, b), jnp.maximum(c, d), jnp.minimum(c, d)
        grp_score.append(jnp.maximum(hi1, hi2) + jnp.maximum(jnp.minimum(hi1, hi2), jnp.maximum(lo1, lo2)))
    best = functools.reduce(jnp.maximum, grp_score)
    in_grp, taken = [], jnp.zeros_like(best)
    for g in range(N_EXPERT_GROUPS):
        hit = jnp.where(grp_score[g] == best, 1.0, 0.0) * (1.0 - taken)
        in_grp.append(hit)
        taken = taken + hit
    picked, flag = [], []
    for e in range(N_EXPERTS):
        g = e // per
        rank = jnp.zeros_like(best)
        for o in range(per * g, per * (g + 1)):
            if o < e:
                rank += jnp.where(s[o] >= s[e], 1.0, 0.0)
            elif o > e:
                rank += jnp.where(s[o] > s[e], 1.0, 0.0)
        flag.append(in_grp[g] * jnp.where(rank < 2.0, 1.0, 0.0))
        picked.append(flag[e] * scores[e:e + 1, :])
    total = functools.reduce(lambda x, y: x + y, picked)
    for e in range(N_EXPERTS):
        gt_ref[e:e + 1, :] = picked[e] / total
    lo_id = functools.reduce(
        jnp.minimum, [jnp.where(flag[e] > 0.0, float(e), float(N_EXPERTS)) for e in range(N_EXPERTS)])
    hi_id = functools.reduce(jnp.maximum, [jnp.where(flag[e] > 0.0, float(e), -1.0) for e in range(N_EXPERTS)])
    ids_ref[0:1, :] = lo_id
    ids_ref[1:2, :] = hi_id
    ids_ref[2:, :] = jnp.zeros((ids_ref.shape[0] - 2,) + lo_id.shape[1:], F32)


def _router(x, norm_g, mod, router_wt, router_b, layer):
    tm = 512
    return pl.pallas_call(
        _router_kernel,
        out_shape=(jax.ShapeDtypeStruct((N_TOK, D_MODEL), F32), jax.ShapeDtypeStruct((N_EXPERTS, N_TOK), F32),
                   jax.ShapeDtypeStruct((8, N_TOK), F32)),
        grid=(N_TOK // tm,),
        in_specs=[pl.BlockSpec((tm, D_MODEL), lambda i: (i, 0)),
                  pl.BlockSpec((None, 1, D_MODEL), lambda i: (layer, 0, 0)),
                  _mod_spec(layer, 3, tm), _mod_spec(layer, 4, tm),
                  _full((N_EXPERTS, D_MODEL)), _full((N_EXPERTS, 1))],
        out_specs=(pl.BlockSpec((tm, D_MODEL), lambda i: (i, 0)), pl.BlockSpec((N_EXPERTS, tm), lambda i: (0, i)),
                   pl.BlockSpec((8, tm), lambda i: (0, i))),
        compiler_params=_params(("parallel",)),
        name="router",
    )(x, norm_g.reshape(DEPTH, 1, D_MODEL), mod, mod, router_wt, router_b.reshape(N_EXPERTS, 1))


MOE_TM = 512
N_PAIRS = N_EXPERT_GROUPS * 6
MOE_STEPS = 2 * (N_PAIRS + N_TOK // MOE_TM - 1)


def _row_copy(src_hbm, src_row, dst_ref, slot, r, sems):
    return pltpu.make_async_copy(src_hbm.at[pl.ds(src_row, 1), :], dst_ref.at[slot, pl.ds(r, 1), :], sems.at[slot])


def _gather_start(src_hbm, dst_ref, slot, idx_ref, base, n, sems):
    def issue(r, carry):
        _row_copy(src_hbm, idx_ref[base + r], dst_ref, slot, r, sems).start()
        return carry

    lax.fori_loop(0, n, issue, 0, unroll=8)


def _gather_wait(src_hbm, dst_ref, slot, n, sems):
    def wait(r, carry):
        _row_copy(src_hbm, 0, dst_ref, slot, r, sems).wait()
        return carry

    lax.fori_loop(0, n, wait, 0, unroll=8)


def _gather_tile(src_hbm, dst_ref, idx_ref, tile, n_tiles, sems):
    slot = tile % 2

    @pl.when(tile == 0)
    def _():
        _gather_start(src_hbm, dst_ref, slot, idx_ref, 0, MOE_TM, sems)

    _gather_wait(src_hbm, dst_ref, slot, MOE_TM, sems)

    @pl.when(tile + 1 < n_tiles)
    def _():
        _gather_start(src_hbm, dst_ref, 1 - slot, idx_ref, (tile + 1) * MOE_TM, MOE_TM, sems)

    return slot


def _moe_plan(ids):
    i32 = jnp.int32
    lo, hi = ids[0].astype(i32), ids[1].astype(i32)
    src = jnp.argsort(lo * N_EXPERTS + hi).astype(i32)
    pos = jnp.argsort(src).astype(i32)
    n_tiles = N_TOK // MOE_TM
    ex = jnp.arange(N_EXPERTS, dtype=i32)
    lo_s, hi_s = lo[src].reshape(n_tiles, MOE_TM, 1), hi[src].reshape(n_tiles, MOE_TM, 1)
    used = ((lo_s == ex).any(axis=1) | (hi_s == ex).any(axis=1)).reshape(-1)
    n_valid = jnp.sum(used).astype(i32)
    idx = jnp.nonzero(used, size=MOE_STEPS, fill_value=0)[0].astype(i32)
    valid = jnp.arange(MOE_STEPS, dtype=i32) < n_valid
    idx = jnp.where(valid, idx, idx[jnp.maximum(n_valid - 1, 0)])
    tile, exp = idx // N_EXPERTS, idx % N_EXPERTS
    first = valid & (tile != jnp.concatenate([jnp.full((1,), -1, i32), tile[:-1]]))
    return src, pos, tile, exp, first.astype(i32), valid.astype(i32)


def _moe_kernel(tile_ref, exp_ref, first_ref, valid_ref, src_ref, xm_hbm, gates_ref, w1_ref, w3_ref, w2_ref,
                o_ref, xs_ref, xb_ref, sem):
    s = pl.program_id(0)

    @pl.when(first_ref[s] == 1)
    def _():
        slot = _gather_tile(xm_hbm, xs_ref, src_ref, tile_ref[s], N_TOK // MOE_TM, sem)
        xb_ref[...] = xs_ref[slot].astype(BF16)
        o_ref[...] = jnp.zeros_like(o_ref)

    @pl.when(valid_ref[s] == 1)
    def _():
        xb = xb_ref[...]
        h1 = _mm(xb, w1_ref[...].astype(BF16))
        h3 = _mm(xb, w3_ref[...].astype(BF16))
        gates = gates_ref[...]
        lane = lax.broadcasted_iota(jnp.int32, gates.shape, 1)
        gate = jnp.sum(jnp.where(lane == exp_ref[s], gates, 0.0), axis=-1, keepdims=True)
        hh = h1 * _sigmoid(h1) * h3 * gate
        o_ref[...] += _mm(hh.astype(BF16), w2_ref[...].astype(BF16))


def _moe(xm, gates_sorted, plan, w1, w3, w2, layer):
    src, _, tile, exp, first, valid = plan
    w_in = pl.BlockSpec((None, None, D_MODEL, D_EXPERT), lambda s, t, e, f, v, i: (layer, e[s], 0, 0))
    grid_spec = pltpu.PrefetchScalarGridSpec(
        num_scalar_prefetch=5, grid=(MOE_STEPS,),
        in_specs=[pl.BlockSpec(memory_space=pl.ANY),
                  pl.BlockSpec((MOE_TM, N_EXPERTS), lambda s, t, e, f, v, i: (t[s], 0)),
                  w_in, w_in,
                  pl.BlockSpec((None, None, D_EXPERT, D_MODEL), lambda s, t, e, f, v, i: (layer, e[s], 0, 0))],
        out_specs=pl.BlockSpec((MOE_TM, D_MODEL), lambda s, t, e, f, v, i: (t[s], 0)),
        scratch_shapes=[pltpu.VMEM((2, MOE_TM, D_MODEL), F32), pltpu.VMEM((MOE_TM, D_MODEL), BF16),
                        pltpu.SemaphoreType.DMA((2,))])
    return pl.pallas_call(
        _moe_kernel, out_shape=jax.ShapeDtypeStruct((N_TOK, D_MODEL), F32), grid_spec=grid_spec,
        compiler_params=_params(("arbitrary",)), name="moe",
    )(tile, exp, first, valid, src, xm, gates_sorted, w1, w3, w2)


def _combine_kernel(pos_ref, acc_hbm, x_ref, g_ref, *refs, split):
    i = pl.program_id(0)
    if split:
        *out_refs, buf_ref, sem = refs
    else:
        ng_ref, nsh_ref, nsc_ref, *out_refs, buf_ref, sem = refs
    slot = _gather_tile(acc_hbm, buf_ref, pos_ref, i, N_TOK // MOE_TM, sem)
    val = x_ref[...] + g_ref[...] * buf_ref[slot]
    if not split:
        out_refs[0][...] = val
        out_refs[1][...] = _modulated(val, ng_ref[...], nsh_ref[...], nsc_ref[...]).astype(BF16)
        return
    n_prompt_tiles = N_PROMPT // MOE_TM

    @pl.when(i < n_prompt_tiles)
    def _():
        out_refs[0][...] = val

    @pl.when(i >= n_prompt_tiles)
    def _():
        out_refs[1][...] = val


def _combine(acc_sorted, pos, x, mod, norm1_g, layer):
    tm = MOE_TM
    split = layer == DEPTH - 1
    n_prompt_tiles = N_PROMPT // tm
    row = pl.BlockSpec((tm, D_MODEL), lambda i, p: (i, 0))
    mod_row = lambda l, chunk: pl.BlockSpec((None, None, None, 1, D_MODEL),
                                            lambda i, p: (l, _mod_row(i, tm), chunk, 0, 0))
    in_specs = [pl.BlockSpec(memory_space=pl.ANY), row, mod_row(layer, 5)]
    args = [pos, acc_sorted, x, mod]
    if split:
        out_shape = (jax.ShapeDtypeStruct((N_PROMPT, D_MODEL), F32), jax.ShapeDtypeStruct((N_SAMPLE, D_MODEL), F32))
        out_specs = (pl.BlockSpec((tm, D_MODEL), lambda i, p: (jnp.minimum(i, n_prompt_tiles - 1), 0)),
                     pl.BlockSpec((tm, D_MODEL), lambda i, p: (jnp.maximum(i - n_prompt_tiles, 0), 0)))
    else:
        in_specs += [pl.BlockSpec((None, 1, D_MODEL), lambda i, p: (layer + 1, 0, 0)),
                     mod_row(layer + 1, 0), mod_row(layer + 1, 1)]
        args += [norm1_g.reshape(DEPTH, 1, D_MODEL), mod, mod]
        out_shape = (jax.ShapeDtypeStruct((N_TOK, D_MODEL), F32), jax.ShapeDtypeStruct((N_TOK, D_MODEL), BF16))
        out_specs = (row, row)
    grid_spec = pltpu.PrefetchScalarGridSpec(
        num_scalar_prefetch=1, grid=(N_TOK // tm,), in_specs=in_specs, out_specs=out_specs,
        scratch_shapes=[pltpu.VMEM((2, tm, D_MODEL), F32), pltpu.SemaphoreType.DMA((2,))])
    return pl.pallas_call(
        functools.partial(_combine_kernel, split=split), out_shape=out_shape, grid_spec=grid_spec,
        compiler_params=_params(("arbitrary",)), name="combine",
    )(*args)


def _gate_rows(gcols, n_seq, seq):
    return gcols.reshape(n_seq, seq // CHUNK, CHUNK, gcols.shape[-1]).transpose(0, 1, 3, 2)


def kernel(x_prompt, x_sample, cache_na_k, cache_na_v, state_rwkv, state_mlstm_c, state_mlstm_n, state_mlstm_m,
           c, c_ctx, norm1_g, norm2_g, w_mod, b_mod, w_in, conv_w, na_q_g, na_k_g, na_rpb, rw_w0, rw_w_up, rw_a0,
           rw_a_up, rw_g_up, rw_k_k, rw_k_a, rw_r_k, rw_ln_g, rw_ln_b, ml_gate_b, ml_norm_g, w_out, router_w,
           router_b, moe_w1, moe_w3, moe_w2):
    p = dict(rw_w0=rw_w0, rw_w_up=rw_w_up, rw_a0=rw_a0, rw_a_up=rw_a_up, rw_g_up=rw_g_up, rw_k_k=rw_k_k,
             rw_k_a=rw_k_a, rw_r_k=rw_r_k, rw_ln_g=rw_ln_g, rw_ln_b=rw_ln_b, ml_gate_b=ml_gate_b,
             ml_norm_g=ml_norm_g)
    cvecs = jnp.concatenate([c_ctx[None], c, jnp.zeros((MOD_ROWS - 1 - DEC_BATCH, D_MODEL), F32)], axis=0)
    mod = _adaln(cvecs, w_mod, b_mod).reshape(DEPTH, MOD_ROWS, 6, 1, D_MODEL)

    assert w_in.shape[-1] == P_IN
    w_in_t = jnp.swapaxes(w_in, 1, 2)
    tw = _na_tables(na_rpb)
    router_wt = router_w.T
    sample_row_off = N_PROMPT // DEC_SEQ

    x = jnp.concatenate([x_prompt.reshape(N_PROMPT, D_MODEL), x_sample.reshape(N_SAMPLE, D_MODEL)], axis=0)
    xm1 = _premod(x, norm1_g, mod, 0)
    kv_prev, rw_prev, ml_prev = (), (), ()
    for l in range(DEPTH):
        u = _inproj(xm1, w_in_t, l)
        ya_p, yb_p, *kv_prev = _attn_prompt(u, conv_w, na_q_g, na_k_g, l, prev=tuple(kv_prev))
        ya_s, yb_s = _na_sample(u, cache_na_k, cache_na_v, tw, conv_w, na_q_g, na_k_g, l)
        yc_p, *rw_prev = _rwkv(u, p, l, BATCH, SEQ, 0, emit_state=True, prev=tuple(rw_prev))
        (yc_s,) = _rwkv(u, p, l, DEC_BATCH, DEC_SEQ, sample_row_off, s0=state_rwkv)
        g0 = CB_GATE * MIX_W + SM_GATE
        gcols = u[:, g0:g0 + 4 * ML_HEADS]
        yd_p, *ml_prev = _mlstm(u, _gate_rows(gcols[:N_PROMPT], BATCH, SEQ), p, l, BATCH, SEQ, 0,
                                emit_state=True, prev=tuple(ml_prev))
        (yd_s,) = _mlstm(u, _gate_rows(gcols[N_PROMPT:], DEC_BATCH, DEC_SEQ), p, l, DEC_BATCH, DEC_SEQ,
                         sample_row_off, state=(state_mlstm_c, state_mlstm_n, state_mlstm_m))
        x = _outproj((ya_p, yb_p, yc_p, yd_p), (ya_s, yb_s, yc_s, yd_s), w_out, x, mod, l)
        xm, gates_t, ids = _router(x, norm2_g, mod, router_wt, router_b, l)
        plan = _moe_plan(ids)
        acc_sorted = _moe(xm, gates_t.T[plan[0]], plan, moe_w1, moe_w3, moe_w2, l)
        combined = _combine(acc_sorted, plan[1], x, mod, norm1_g, l)
        if l < DEPTH - 1:
            x, xm1 = combined
    y_prompt, y_sample = combined
    new_c, new_n, new_m = ml_prev
    return (y_prompt.reshape(BATCH, SEQ, D_MODEL), y_sample.reshape(DEC_BATCH, DEC_SEQ, D_MODEL),
            kv_prev[0], kv_prev[1], rw_prev[0], new_c, new_n.reshape(BATCH, DEPTH, 2, ML_HEADS, ML_DH),
            new_m.reshape(BATCH, DEPTH, 2, ML_HEADS))
```

```python
import functools

import numpy as np
import jax
import jax.numpy as jnp
from jax import lax
from jax.experimental import pallas as pl
from jax.experimental.pallas import tpu as pltpu

F32 = jnp.float32
BF16 = jnp.bfloat16
HI = lax.Precision.HIGHEST

D_MODEL = 2048
BATCH = 16
SEQ = 256
DEPTH = 2
DEC_BATCH = 2
DEC_SEQ = 1024
PAST_LEN = 512
GRID_W = 64
MIX_W = D_MODEL // 4
CONV_K = 3
NA_DH = 64
NA_HEADS = MIX_W // NA_DH
NA_WIN_R = 8
NA_WIN_C = 16
NA_SCALE = NA_DH ** -0.5
ROPE_THETA = 10000.0
RW_DH = 64
RW_HEADS = MIX_W // RW_DH
RW_W_RANK = 64
RW_A_RANK = 64
RW_G_RANK = 128
RW_DECAY_SCALE = 0.606531
RW_GN_EPS = 64e-5
ML_DH = 128
ML_HEADS = MIX_W // ML_DH
N_EXPERTS = 16
N_EXPERT_GROUPS = 4
D_EXPERT = 512
EPS = 1e-6
NEG_INF = -1e30

N_PROMPT = BATCH * SEQ
N_SAMPLE = DEC_BATCH * DEC_SEQ
N_TOK = N_PROMPT + N_SAMPLE
MOD_ROWS = 8
CHUNK = 64
SUB = 16
P1_CHUNKS = 4
NA_ROWS_PER_TRIP = 4
P_IN = 13 * MIX_W + RW_W_RANK + RW_A_RANK + RW_G_RANK + 4 * ML_HEADS
P_BLOCKS = 15
P_PAD = P_BLOCKS * MIX_W
(CB_CVB, CB_CVC, CB_CVH, CB_NAQ, CB_NAK, CB_NAV, CB_RWR, CB_RWK, CB_RWV,
 CB_MLQ, CB_MLK, CB_MLV, CB_MLO, CB_SMALL, CB_GATE) = range(P_BLOCKS)
SM_WL, SM_AL, SM_GL = 0, 64, 128
SM_GATE = MIX_W - 4 * ML_HEADS
VMEM_LIMIT = 56 * 1024 * 1024


def _mm(a, b, prec=None):
    return jnp.dot(a, b, precision=prec, preferred_element_type=F32)


def _mm_nt(a, b, prec=None):
    return lax.dot_general(a, b, (((1,), (1,)), ((), ())), precision=prec, preferred_element_type=F32)


def _sigmoid(x):
    return 1.0 / (1.0 + jnp.exp(-x))


def _full(shape):
    n = len(shape)
    return pl.BlockSpec(shape, lambda *_: (0,) * n)


def _params(sem):
    return pltpu.CompilerParams(dimension_semantics=sem, vmem_limit_bytes=VMEM_LIMIT)


def _mod_row(i, tm):
    n_prompt_tiles = N_PROMPT // tm
    tiles_per_sample = DEC_SEQ // tm
    return jnp.where(i < n_prompt_tiles, 0, 1 + (i - n_prompt_tiles) // tiles_per_sample)


def _mod_spec(layer, chunk, tm, tn=D_MODEL, with_j=False):
    if with_j:
        return pl.BlockSpec((None, None, None, 1, tn), lambda i, j: (layer, _mod_row(i, tm), chunk, 0, j))
    return pl.BlockSpec((None, None, None, 1, tn), lambda i, *_: (layer, _mod_row(i, tm), chunk, 0, 0))


def _tri_masks(n):
    t = lax.broadcasted_iota(jnp.int32, (n, n), 0)
    s = lax.broadcasted_iota(jnp.int32, (n, n), 1)
    incl = ((s <= t).astype(F32), (s >= t).astype(F32))
    strict = ((s < t).astype(F32), (s > t).astype(F32))
    return incl, strict, t, s


def _split2(x):
    hi = x.astype(BF16)
    return hi, (x - hi.astype(F32)).astype(BF16)


def _mm1(a, b):
    return _mm(a.astype(BF16), b.astype(BF16))


def _mm3(a, b):
    a_hi, a_lo = _split2(a)
    b_hi, b_lo = _split2(b)
    return _mm(a_hi, b_lo) + _mm(a_lo, b_hi) + _mm(a_hi, b_hi)


def _split3(x):
    x0 = x.astype(BF16)
    r1 = x - x0.astype(F32)
    x1 = r1.astype(BF16)
    return x0, x1, (r1 - x1.astype(F32)).astype(BF16)


def _mm_exact_lhs(mask_b, x):
    x0, x1, x2 = _split3(x)
    return _mm(mask_b, x2) + _mm(mask_b, x1) + _mm(mask_b, x0)


def _mm_exact_rhs(x, mask_b):
    x0, x1, x2 = _split3(x)
    return _mm(x2, mask_b) + _mm(x1, mask_b) + _mm(x0, mask_b)


def _adaln_kernel(cv_ref, w_ref, b_ref, o_ref):
    cv = cv_ref[...]
    a0, a1, a2 = _split3(cv * _sigmoid(cv))
    w_hi, w_lo = _split2(w_ref[...])
    o_ref[...] = ((_mm(a2, w_hi) + _mm(a1, w_lo)) + (_mm(a1, w_hi) + _mm(a0, w_lo)) + _mm(a0, w_hi)) + b_ref[...]


def _adaln(cvecs, w_mod, b_mod):
    tn = 1024
    n_out = 6 * D_MODEL
    return pl.pallas_call(
        _adaln_kernel,
        out_shape=jax.ShapeDtypeStruct((DEPTH, MOD_ROWS, n_out), F32),
        grid=(DEPTH, n_out // tn),
        in_specs=[_full((MOD_ROWS, D_MODEL)),
                  pl.BlockSpec((None, D_MODEL, tn), lambda l, j: (l, 0, j)),
                  pl.BlockSpec((None, 1, tn), lambda l, j: (l, 0, j))],
        out_specs=pl.BlockSpec((None, MOD_ROWS, tn), lambda l, j: (l, 0, j)),
        compiler_params=_params(("parallel", "parallel")),
        name="adaln",
    )(cvecs, w_mod, b_mod.reshape(DEPTH, 1, n_out))


def _modulated(x, g, sh, sc):
    y = x * lax.rsqrt(jnp.mean(x * x, axis=-1, keepdims=True) + EPS) * g
    return y * (1.0 + sc) + sh


def _premod_kernel(x_ref, g_ref, sh_ref, sc_ref, o_ref):
    o_ref[...] = _modulated(x_ref[...], g_ref[...], sh_ref[...], sc_ref[...]).astype(BF16)


def _premod(x, norm_g, mod, layer):
    tm = 512
    row = pl.BlockSpec((tm, D_MODEL), lambda i: (i, 0))
    return pl.pallas_call(
        _premod_kernel,
        out_shape=jax.ShapeDtypeStruct((N_TOK, D_MODEL), BF16),
        grid=(N_TOK // tm,),
        in_specs=[row, pl.BlockSpec((None, 1, D_MODEL), lambda i: (layer, 0, 0)),
                  _mod_spec(layer, 0, tm), _mod_spec(layer, 1, tm)],
        out_specs=row,
        compiler_params=_params(("parallel",)),
        name="premod",
    )(x, norm_g.reshape(DEPTH, 1, D_MODEL), mod, mod)


def _inproj_kernel(xm_ref, w_ref, o_ref):
    o_ref[...] = _mm_nt(xm_ref[...], w_ref[0].astype(BF16))


def _inproj_src_row(j):
    a = 9
    narrow = RW_W_RANK + RW_A_RANK + RW_G_RANK
    g = 16
    return g * jnp.where(j < a, j * (MIX_W // g),
                         jnp.where(j < CB_SMALL, j * (MIX_W // g) + narrow // g,
                                   jnp.where(j == CB_SMALL, a * MIX_W // g, (P_IN - MIX_W) // g)))


def _inproj(xm, w_in_t, layer):
    tm, tn = 3072, MIX_W
    return pl.pallas_call(
        _inproj_kernel,
        out_shape=jax.ShapeDtypeStruct((N_TOK, P_PAD), F32),
        grid=(N_TOK // tm, P_BLOCKS),
        in_specs=[pl.BlockSpec((tm, D_MODEL), lambda i, j: (i, 0)),
                  pl.BlockSpec((pl.Element(1), pl.Element(tn), pl.Element(D_MODEL)),
                               lambda i, j: (layer, _inproj_src_row(j), 0))],
        out_specs=pl.BlockSpec((tm, tn), lambda i, j: (i, j)),
        compiler_params=_params(("parallel", "parallel")),
        name="inproj",
    )(xm, w_in_t)


def _conv_mix(b, c, h, w):
    u = c * h
    n = u.shape[0]
    row = lax.broadcasted_iota(jnp.int32, u.shape, 0)
    prev = jnp.where(row == 0, 0.0, pltpu.roll(u, 1, axis=0))
    nxt = jnp.where(row == n - 1, 0.0, pltpu.roll(u, n - 1, axis=0))
    return b * (prev * w[0:1] + u * w[1:2] + nxt * w[2:3])


def _head_rms(x, g):
    return x * lax.rsqrt(jnp.mean(x * x, axis=-1, keepdims=True) + EPS) * g


def _attn_prompt_kernel(cb_ref, cc_ref, ch_ref, q_ref, k_ref, v_ref, cw_ref, qg_ref, kg_ref,
                        ya_ref, yb_ref, nk_ref, nv_ref):
    ya_ref[...] = _conv_mix(cb_ref[...], cc_ref[...], ch_ref[...], cw_ref[...]).astype(ya_ref.dtype)
    sls = [slice(h * NA_DH, (h + 1) * NA_DH) for h in range(NA_HEADS)]
    qn = [_head_rms(q_ref[:, sl], qg_ref[...]) * NA_SCALE for sl in sls]
    kn = [_head_rms(k_ref[:, sl], kg_ref[...]) for sl in sls]
    vh = [v_ref[:, sl] for sl in sls]
    s = [_mm_nt(q.astype(BF16), k.astype(BF16)) for q, k in zip(qn, kn)]
    p = [jnp.exp(x - jnp.max(x, axis=-1, keepdims=True)) for x in s]
    o = [_mm(x.astype(BF16), v.astype(BF16)) / jnp.sum(x, axis=-1, keepdims=True) for x, v in zip(p, vh)]
    yb_ref[...] = jnp.concatenate(o, axis=1).astype(yb_ref.dtype)
    for h in range(NA_HEADS):
        nk_ref[h] = kn[h]
        nv_ref[h] = vh[h]


def _u_spec(rows, col_block, row_off_blocks=0):
    return pl.BlockSpec((rows, MIX_W), lambda b: (b + row_off_blocks, col_block))


def _carry_through(kernel, n_inputs, prev, n_plain_out, n_state_out, layer):
    n_prev = len(prev)

    def body(*refs):
        ins, rest = refs[:n_inputs], list(refs[n_inputs + n_prev:])
        if not n_prev:
            for i in range(n_plain_out, n_plain_out + n_state_out):
                full = rest[i]
                for d in range(DEPTH):
                    if d != layer:
                        full[d] = jnp.zeros(full.shape[1:], full.dtype)
                rest[i] = full.at[layer]
        return kernel(*ins, *rest)

    return body, [pl.BlockSpec(memory_space=pl.ANY)] * n_prev


def _state_spec(shape, layer, first, index):
    def index_map(*g):
        b, *tail = index(*g)
        return (b, 0 if first else layer, *tail)
    return pl.BlockSpec((None, DEPTH if first else None) + tuple(shape), index_map)


def _attn_prompt(u, conv_w, q_g, k_g, layer, prev=(), n_seq=BATCH, seq=SEQ):
    lw = lambda shape: pl.BlockSpec((None,) + shape, lambda b: (layer,) + (0,) * len(shape))
    y_spec = pl.BlockSpec((seq, MIX_W), lambda b: (b, 0))
    kv_spec = _state_spec((NA_HEADS, seq, NA_DH), layer, not prev, lambda b: (b, 0, 0, 0))
    in_specs = ([_u_spec(seq, cb) for cb in (CB_CVB, CB_CVC, CB_CVH, CB_NAQ, CB_NAK, CB_NAV)]
                + [lw((CONV_K, MIX_W)), lw((1, NA_DH)), lw((1, NA_DH))])
    body, prev_specs = _carry_through(_attn_prompt_kernel, len(in_specs), prev, 2, 2, layer)
    return pl.pallas_call(
        body,
        out_shape=(jax.ShapeDtypeStruct((n_seq * seq, MIX_W), BF16),) * 2
        + (jax.ShapeDtypeStruct((n_seq, DEPTH, NA_HEADS, seq, NA_DH), F32),) * 2,
        grid=(n_seq,),
        in_specs=in_specs + prev_specs,
        out_specs=(y_spec, y_spec, kv_spec, kv_spec),
        input_output_aliases={len(in_specs) + i: 2 + i for i in range(len(prev))},
        compiler_params=_params(("parallel",)),
        name="attn_prompt",
    )(u, u, u, u, u, u, conv_w, q_g.reshape(DEPTH, 1, NA_DH), k_g.reshape(DEPTH, 1, NA_DH), *prev)


def _na_kernel(cb_ref, cc_ref, ch_ref, q_ref, k_ref, v_ref, kc_ref, vc_ref, tw_ref, cos_ref, sin_ref,
               perm_ref, cw_ref, qg_ref, kg_ref, ya_ref, yb_ref, qs_ref, ks_ref, tws_ref):
    rows = DEC_SEQ // GRID_W
    wr = min(NA_WIN_R, rows)
    nw = wr * GRID_W
    ya_ref[...] = _conv_mix(cb_ref[...], cc_ref[...], ch_ref[...], cw_ref[...]).astype(ya_ref.dtype)
    cos, sin, perm_b = cos_ref[...], sin_ref[...], perm_ref[...].astype(BF16)

    def rope(x):
        hi, lo = _split2(x)
        return x * cos + (_mm(hi, perm_b) + _mm(lo, perm_b)) * sin

    for h in range(NA_HEADS):
        sl = slice(h * NA_DH, (h + 1) * NA_DH)
        qs_ref[...] = (rope(_head_rms(q_ref[:, sl], qg_ref[...])) * NA_SCALE).astype(BF16)
        ks_ref[...] = rope(_head_rms(k_ref[:, sl], kg_ref[...])).astype(BF16)
        kch = kc_ref[h].astype(BF16)
        vch = vc_ref[h].astype(BF16)
        for p in range(wr):
            tws_ref[p] = jnp.concatenate([tw_ref[h, j - p + NA_WIN_R - 1] for j in range(wr)], axis=1)

        def rows_step(t, carry):
            rr = [t * NA_ROWS_PER_TRIP + i for i in range(NA_ROWS_PER_TRIP)]
            rs = [jnp.clip(r - wr // 2, 0, rows - wr) for r in rr]
            q0 = [pl.multiple_of(r * GRID_W, GRID_W) for r in rr]
            k0 = [pl.multiple_of(x * GRID_W, GRID_W) for x in rs]
            q_r = [qs_ref[pl.ds(x, GRID_W), :] for x in q0]
            s_w = [_mm_nt(q, ks_ref[pl.ds(k, nw), :]) + tws_ref[r - x] for q, k, r, x in zip(q_r, k0, rr, rs)]
            s_c = [_mm_nt(q, kch) for q in q_r]
            m = [jnp.maximum(jnp.max(a, axis=-1, keepdims=True), jnp.max(b, axis=-1, keepdims=True))
                 for a, b in zip(s_w, s_c)]
            p_w = [jnp.exp(a - x) for a, x in zip(s_w, m)]
            p_c = [jnp.exp(b - x) for b, x in zip(s_c, m)]
            den = [jnp.sum(a, axis=-1, keepdims=True) + jnp.sum(b, axis=-1, keepdims=True) for a, b in zip(p_w, p_c)]
            v_w = [v_ref[pl.ds(k, nw), sl].astype(BF16) for k in k0]
            o = [(_mm(a.astype(BF16), v) + _mm(b.astype(BF16), vch)) / d for a, b, v, d in zip(p_w, p_c, v_w, den)]
            for x, val in zip(q0, o):
                yb_ref[pl.ds(x, GRID_W), sl] = val.astype(yb_ref.dtype)
            return carry

        lax.fori_loop(0, rows // NA_ROWS_PER_TRIP, rows_step, 0)


def _na_tables(rpb):
    qc = np.arange(GRID_W)
    kc = np.arange(GRID_W)
    wstart = np.clip(qc - NA_WIN_C // 2, 0, GRID_W - NA_WIN_C)
    colmask = (kc[None, :] >= wstart[:, None]) & (kc[None, :] < wstart[:, None] + NA_WIN_C)
    dc = np.clip(kc[None, :] - qc[:, None], -(NA_WIN_C - 1), NA_WIN_C - 1) + NA_WIN_C - 1
    pick = (np.arange(2 * NA_WIN_C - 1)[:, None] == dc.reshape(1, -1)).astype(np.float32)
    bias = jnp.einsum('lhrd,dn->lhrn', rpb, jnp.asarray(pick), precision=HI)
    return jnp.where(colmask[None, None, None], bias.reshape(rpb.shape[:3] + dc.shape), NEG_INF)


def _rope_tables():
    t = np.arange(DEC_SEQ)
    quarter = NA_DH // 4
    freq = ROPE_THETA ** (-np.arange(quarter, dtype=np.float32) / quarter)
    ang_r = (t // GRID_W).astype(np.float32)[:, None] * freq
    ang_c = (t % GRID_W).astype(np.float32)[:, None] * freq
    cos = np.concatenate([np.cos(ang_r), np.cos(ang_r), np.cos(ang_c), np.cos(ang_c)], axis=-1)
    sin = np.concatenate([-np.sin(ang_r), np.sin(ang_r), -np.sin(ang_c), np.sin(ang_c)], axis=-1)
    src = np.concatenate([np.arange(quarter) + quarter, np.arange(quarter),
                          np.arange(quarter) + 3 * quarter, np.arange(quarter) + 2 * quarter])
    perm = np.zeros((NA_DH, NA_DH), np.float32)
    perm[src, np.arange(NA_DH)] = 1.0
    return cos.astype(np.float32), sin.astype(np.float32), perm


def _na_sample(u, cache_k, cache_v, tw, conv_w, q_g, k_g, layer, n_seq=DEC_BATCH, row_off=N_PROMPT // DEC_SEQ):
    cos, sin, perm = _rope_tables()
    lw = lambda shape: pl.BlockSpec((None,) + shape, lambda b: (layer,) + (0,) * len(shape))
    y_spec = pl.BlockSpec((DEC_SEQ, MIX_W), lambda b: (b, 0))
    c_spec = pl.BlockSpec((None, None, NA_HEADS, PAST_LEN, NA_DH), lambda b: (b, layer, 0, 0, 0))
    wr = min(NA_WIN_R, DEC_SEQ // GRID_W)
    n_off = 2 * NA_WIN_R - 1
    return pl.pallas_call(
        _na_kernel,
        out_shape=(jax.ShapeDtypeStruct((n_seq * DEC_SEQ, MIX_W), BF16),) * 2,
        grid=(n_seq,),
        in_specs=[_u_spec(DEC_SEQ, cb, row_off) for cb in (CB_CVB, CB_CVC, CB_CVH, CB_NAQ, CB_NAK, CB_NAV)]
        + [c_spec, c_spec, lw((NA_HEADS, n_off, GRID_W, GRID_W)),
           _full((DEC_SEQ, NA_DH)), _full((DEC_SEQ, NA_DH)), _full((NA_DH, NA_DH)),
           lw((CONV_K, MIX_W)), lw((1, NA_DH)), lw((1, NA_DH))],
        out_specs=(y_spec, y_spec),
        scratch_shapes=[pltpu.VMEM((DEC_SEQ, NA_DH), BF16), pltpu.VMEM((DEC_SEQ, NA_DH), BF16),
                        pltpu.VMEM((wr, GRID_W, wr * GRID_W), F32)],
        compiler_params=_params(("parallel",)),
        name="na_sample",
    )(u, u, u, u, u, u, cache_k, cache_v, tw, jnp.asarray(cos), jnp.asarray(sin), jnp.asarray(perm),
      conv_w, q_g.reshape(DEPTH, 1, NA_DH), k_g.reshape(DEPTH, 1, NA_DH))


def _seg_ones(width, seg):
    a = lax.broadcasted_iota(jnp.int32, (width, width), 0) // seg
    b = lax.broadcasted_iota(jnp.int32, (width, width), 1) // seg
    return (a == b).astype(F32)


def _rwkv_kernel(*refs, seq, has_s0, emit_state):
    it = iter(refs)
    r_ref, k_ref, v_ref, sm_ref = (next(it) for _ in range(4))
    (w0_ref, wup_ref, a0_ref, aup_ref, gup_ref, kkp_ref, kap_ref, rkp_ref, lng_ref, lnb_ref) = (
        next(it) for _ in range(10))
    mask_ref = next(it)
    s0_ref = next(it) if has_s0 else None
    y_ref = next(it)
    so_ref = next(it) if emit_state else None
    kk_ref, lw_ref, ka_ref, kd_ref, coef_ref, ysp_ref = (next(it) for _ in range(6))

    dh = RW_DH
    pw = 2 * dh
    nc = seq // CHUNK
    seg_b = _seg_ones(pw, dh).astype(BF16)

    r = r_ref[...]
    k = k_ref[...]
    sm = sm_ref[...]
    wl = jnp.tanh(sm[:, SM_WL:SM_WL + RW_W_RANK])
    al = sm[:, SM_AL:SM_AL + RW_A_RANK]
    kk = k * kkp_ref[...]
    kk = kk * lax.rsqrt(_mm_exact_rhs(kk * kk, seg_b) + EPS)
    kk_ref[...] = kk
    for z in range(2):
        lw_ref[z] = -RW_DECAY_SCALE * _sigmoid(w0_ref[z:z + 1, :] + _mm3(wl, wup_ref[z]))
        a = _sigmoid(a0_ref[z:z + 1, :] + _mm3(al, aup_ref[z]))
        ka_ref[z] = kk * a
        kd_ref[z] = k * (1.0 + (a - 1.0) * kap_ref[...])

    cat = jnp.concatenate

    def phase1(c2, carry):
        incl, strict, ti, si = _tri_masks(CHUNK)
        incl_b = (mask_ref[0], mask_ref[1])
        strict2 = tuple(cat([m, m], axis=1) for m in strict)
        incl2 = tuple(cat([m, m], axis=1) for m in incl)
        diag_blk = (ti // SUB == si // SUB).astype(F32)
        eye = (ti == si).astype(F32)
        zero_pair = jnp.zeros((CHUNK, pw), F32)
        head_of_lane = lax.broadcasted_iota(jnp.int32, (CHUNK, pw), 1) // dh
        head_of_lane2 = (lax.broadcasted_iota(jnp.int32, (CHUNK, 2 * pw), 1) // dh) % 2
        r_pw = lax.broadcasted_iota(jnp.int32, (pw, pw), 0)
        c_pw = lax.broadcasted_iota(jnp.int32, (pw, pw), 1)
        same_head = r_pw // dh == c_pw // dh
        eye_pw = (r_pw == c_pw).astype(F32)
        chains = []
        for cc in range(P1_CHUNKS):
            rows = pl.ds(pl.multiple_of((c2 * P1_CHUNKS + cc) * CHUNK, CHUNK), CHUNK)
            vc, rc, kkc = v_ref[rows, :], r_ref[rows, :], kk_ref[rows, :]
            for z in range(2):
                lwc = lw_ref[z, rows, :]
                cum = _mm_exact_lhs(incl_b[z], lwc)
                tot = cum[CHUNK - 1:CHUNK] if z == 0 else cum[0:1]
                e_neg = jnp.exp(-cum)
                dec = jnp.exp(tot - cum)
                kac, kdc = ka_ref[z, rows, :], kd_ref[z, rows, :]
                chains.append((z, kkc * jnp.exp(cum - lwc), rc * jnp.exp(cum), vc, kac * e_neg, kdc * e_neg,
                               kac * dec, kdc * dec, jnp.exp(tot)))
        heads = [(q, j) for q in range(len(chains)) for j in range(2)]
        zq = [ch[0] for ch in chains]
        ymat = [cat([ch[4], ch[5]], axis=0).astype(BF16) for ch in chains]
        vz = [cat([zero_pair, ch[3]], axis=0) for ch in chains]
        nvz = [cat([zero_pair, -ch[3]], axis=1) for ch in chains]
        x_in = [cat([jnp.where(head_of_lane == j, chains[q][1], 0.0),
                     jnp.where(head_of_lane == j, chains[q][2], 0.0)], axis=0) for q, j in heads]
        aa = [_mm_nt(x.astype(BF16), ymat[q]) for x, (q, _) in zip(x_in, heads)]
        top = [a[0:CHUNK] * strict2[zq[q]] for a, (q, _) in zip(aa, heads)]
        a_r = [a[CHUNK:] * incl2[zq[q]] for a, (q, _) in zip(aa, heads)]
        akv = [_mm1(t, vz[q]) for t, (q, _) in zip(top, heads)]
        low = [t[:, 0:CHUNK] for t in top]
        ld = [x * diag_blk for x in low]
        lo = [x - y for x, y in zip(low, ld)]
        l2 = [_mm1(x, x) for x in ld]
        l4 = [_mm1(x, x) for x in l2]
        l8 = [_mm1(x, x) for x in l4]
        td = [eye - x for x in ld]
        for lp in (l2, l4, l8):
            td = [t + _mm1(t, p_) for t, p_ in zip(td, lp)]
        x0 = [_mm1(t, cat([chains[q][1], a, l], axis=1)) for t, a, l, (q, _) in zip(td, akv, lo, heads)]
        pq0 = [x[:, 0:2 * pw] for x in x0]
        wm = [x[:, 2 * pw:] for x in x0]
        pq = pq0
        for _ in range(CHUNK // SUB - 1):
            pq = [p0 - _mm1(w, p_) for p0, w, p_ in zip(pq0, wm, pq)]
        ryc = [_mm1(a, cat([p_, nvz[q]], axis=0)) for a, p_, (q, _) in zip(a_r, pq, heads)]
        for q, ch in enumerate(chains):
            pq_m = jnp.where(head_of_lane2 == 0, pq[2 * q], pq[2 * q + 1])
            ryc_m = jnp.where(head_of_lane2 == 0, ryc[2 * q], ryc[2 * q + 1])
            g1 = _mm1(ch[6].T, pq_m)
            g2 = _mm1(ch[7].T, ch[3])
            g_t = eye_pw * ch[8] - jnp.where(same_head, g1[:, 0:pw], 0.0)
            h_t = jnp.where(same_head, g2 - g1[:, pw:], 0.0)
            coef_ref[c2 * P1_CHUNKS + q // 2, q % 2] = cat(
                [g_t, ch[2] - ryc_m[:, 0:pw], h_t, -ryc_m[:, pw:]], axis=0)
        return carry

    lax.fori_loop(0, nc // P1_CHUNKS, phase1, 0)

    def block_diag(a, b):
        zero = jnp.zeros((dh, dh), F32)
        return cat([cat([a, zero], axis=1), cat([zero, b], axis=1)], axis=0)

    m_init = tuple((block_diag(s0_ref[z, 0].T, s0_ref[z, 1].T) if has_s0 else jnp.zeros((pw, pw), F32))
                   for z in range(2))
    n_lhs = pw + CHUNK

    def phase2(ci, ms):
        new_ms, ys = [], []
        for z in range(2):
            c = ci if z == 0 else nc - 1 - ci
            out = _mm3(coef_ref[c, z, 0:n_lhs, :], ms[z]) + coef_ref[c, z, n_lhs:, :]
            new_ms.append(out[0:pw])
            ys.append(out[pw:])
        ysp_ref[ci] = cat(ys, axis=1)
        return tuple(new_ms)

    m_fin = lax.fori_loop(0, nc, phase2, m_init)

    if emit_state:
        for z in range(2):
            for j in range(2):
                so_ref[z, j] = m_fin[z][j * dh:(j + 1) * dh, j * dh:(j + 1) * dh].T

    y = cat([ysp_ref[c, :, 0:pw] + ysp_ref[nc - 1 - c, :, pw:] for c in range(nc)], axis=0)
    mu = _mm_exact_rhs(y, seg_b) * (1.0 / dh)
    yc = y - mu
    var = _mm_exact_rhs(yc * yc, seg_b) * (1.0 / dh)
    yn = yc * lax.rsqrt(var + RW_GN_EPS) * lng_ref[...] + lnb_ref[...]
    bonus = _mm_exact_rhs(r * k * rkp_ref[...], seg_b) * v_ref[...]
    g = _mm3(_sigmoid(sm[:, SM_GL:SM_GL + RW_G_RANK]), gup_ref[...])
    y_ref[...] = ((yn + bonus) * g).astype(y_ref.dtype)


def _rwkv(u, p, layer, n_seq, seq, row_off, s0=None, emit_state=False, prev=()):
    n_pairs = RW_HEADS // 2
    pw = 2 * RW_DH
    bpc = MIX_W // pw
    lw = lambda shape: pl.BlockSpec((None,) + shape, lambda b, hp: (layer,) + (0,) * (len(shape) - 1) + (hp,))
    row = lambda a: a.reshape(DEPTH, 1, MIX_W)
    u_pair = lambda cb: pl.BlockSpec((seq, pw), lambda b, hp: (b + row_off, cb * bpc + hp))
    in_specs = [u_pair(CB_RWR), u_pair(CB_RWK), u_pair(CB_RWV),
                pl.BlockSpec((seq, MIX_W), lambda b, hp: (b + row_off, CB_SMALL)),
                lw((2, pw)), lw((2, RW_W_RANK, pw)), lw((2, pw)), lw((2, RW_A_RANK, pw)),
                lw((RW_G_RANK, pw))] + [lw((1, pw))] * 5 + [_full((2, CHUNK, CHUNK))]
    t_idx = np.arange(CHUNK)
    incl_masks = np.stack([t_idx[None, :] <= t_idx[:, None], t_idx[None, :] >= t_idx[:, None]])
    args = [u, u, u, u, p['rw_w0'], p['rw_w_up'], p['rw_a0'], p['rw_a_up'], p['rw_g_up'],
            row(p['rw_k_k']), row(p['rw_k_a']), row(p['rw_r_k']), row(p['rw_ln_g']), row(p['rw_ln_b']),
            jnp.asarray(incl_masks, BF16)]
    st_blk = (2, 2, RW_DH, RW_DH)
    if s0 is not None:
        in_specs.append(pl.BlockSpec((None, None) + st_blk, lambda b, hp: (b, layer, 0, hp, 0, 0)))
        args.append(s0)
    out_shape = [jax.ShapeDtypeStruct((n_seq * seq, MIX_W), BF16)]
    out_specs = [pl.BlockSpec((seq, pw), lambda b, hp: (b, hp))]
    if emit_state:
        out_shape.append(jax.ShapeDtypeStruct((n_seq, DEPTH, 2, RW_HEADS, RW_DH, RW_DH), F32))
        out_specs.append(_state_spec(st_blk, layer, not prev, lambda b, hp: (b, 0, hp, 0, 0)))
    nc = seq // CHUNK
    tok = lambda n: pltpu.VMEM((n, seq, pw) if n else (seq, pw), F32)
    body, prev_specs = _carry_through(
        functools.partial(_rwkv_kernel, seq=seq, has_s0=s0 is not None, emit_state=emit_state), len(in_specs), prev,
        1, 1 if emit_state else 0, layer)
    return pl.pallas_call(
        body,
        out_shape=tuple(out_shape), grid=(n_seq, n_pairs), in_specs=in_specs + prev_specs,
        out_specs=tuple(out_specs),
        input_output_aliases={len(in_specs) + i: 1 + i for i in range(len(prev))},
        scratch_shapes=[tok(0), tok(2), tok(2), tok(2),
                        pltpu.VMEM((nc, 2, 2 * (pw + CHUNK), pw), F32), pltpu.VMEM((nc, CHUNK, 2 * pw), F32)],
        compiler_params=_params(("parallel", "parallel")),
        name=f"rwkv_{seq}",
    )(*args, *prev)


def _log_sigmoid(x):
    return jnp.minimum(x, 0.0) - jnp.log(1.0 + jnp.exp(-jnp.abs(x)))


def _mlstm_kernel(*refs, seq, has_s0, emit_state):
    it = iter(refs)
    q_ref, k_ref, v_ref, o_ref, sm_ref, gr_ref, bc_ref, br_ref, ng_ref = (next(it) for _ in range(9))
    c0_ref, n0_ref, m0_ref = ((next(it), next(it), next(it)) if has_s0 else (None, None, None))
    y_ref = next(it)
    co_ref, no_ref, mo_ref = ((next(it), next(it), next(it)) if emit_state else (None, None, None))
    hs_ref, c_ref, n_ref, m_ref = (next(it) for _ in range(4))

    nc = seq // CHUNK
    n_st = 2 * ML_HEADS
    for i in range(n_st):
        z, h = divmod(i, ML_HEADS)
        c_ref[i] = c0_ref[z, h] if has_s0 else jnp.zeros((ML_DH, ML_DH), F32)
        n_ref[i] = n0_ref[z, h] if has_s0 else jnp.zeros((1, ML_DH), F32)
        m_ref[i] = m0_ref[z, h] if has_s0 else jnp.zeros((1, 1), F32)

    incl, _, ti, si = _tri_masks(CHUNK)
    before = ((si <= ti), (si >= ti))

    nh = ML_HEADS
    chains = [(z, h) for z in range(2) for h in range(nh)]
    each = lambda f, *xs: [f(*a) for a in zip(*xs)]
    zs = [z for z, _ in chains]

    def chunk_step(ci, carry):
        per_dir = []
        for z in range(2):
            c = ci if z == 0 else nc - 1 - ci
            rows = pl.ds(pl.multiple_of(c * CHUNK, CHUNK), CHUNK)
            g0 = SM_GATE + z * 2 * nh
            gc = sm_ref[rows, g0:g0 + 2 * nh] + bc_ref[:, z * 2 * nh:(z + 1) * 2 * nh]
            gr = gr_ref[c, z * 2 * nh:(z + 1) * 2 * nh, :] + br_ref[z * 2 * nh:(z + 1) * 2 * nh, :]
            b_cols = _mm(incl[z], _log_sigmoid(gc[:, nh:]), HI)
            b_rows = _mm(_log_sigmoid(gr[nh:]), incl[1 - z], HI)
            per_dir.append((rows, gc[:, :nh], b_cols, gr[:nh], b_rows))
        rows = [per_dir[z][0] for z, _ in chains]
        hsl = [slice(h * ML_DH, (h + 1) * ML_DH) for _, h in chains]
        i_col = [per_dir[z][1][:, h:h + 1] for z, h in chains]
        b_col = [per_dir[z][2][:, h:h + 1] for z, h in chains]
        i_row = [per_dir[z][3][h:h + 1] for z, h in chains]
        b_row = [per_dir[z][4][h:h + 1] for z, h in chains]
        b_last = each(lambda b, z: b[CHUNK - 1:CHUNK] if z == 0 else b[0:1], b_col, zs)
        m_old = [m_ref[i] for i in range(n_st)]
        cm = [c_ref[i] for i in range(n_st)]
        nv = [n_ref[i] for i in range(n_st)]
        qc = each(lambda r, s: q_ref[r, s] * (ML_DH ** -0.5), rows, hsl)
        kc = each(lambda r, s: k_ref[r, s], rows, hsl)
        vc = each(lambda r, s: v_ref[r, s], rows, hsl)
        qk = each(lambda q, k: _mm_nt(q.astype(BF16), k.astype(BF16)), qc, kc)
        qcm = each(lambda q, c_: _mm_nt(q.astype(BF16), c_.astype(BF16)), qc, cm)
        a_t = each(lambda b, m: b + m, b_col, m_old)
        dmat = each(lambda bc, brw, ir, z: jnp.where(before[z], bc - brw + ir, NEG_INF), b_col, b_row, i_row, zs)
        m_t = each(lambda a, d: jnp.maximum(a, jnp.max(d, axis=-1, keepdims=True)), a_t, dmat)
        s = each(lambda x, d, m: x * jnp.exp(d - m), qk, dmat, m_t)
        inter = each(lambda a, m: jnp.exp(a - m), a_t, m_t)
        sv = each(_mm1, s, vc)
        g_col = each(lambda bl, bc, ic: bl - bc + ic, b_last, b_col, i_col)
        a_l = each(lambda bl, m: bl + m, b_last, m_old)
        m_new = each(lambda a, g: jnp.maximum(a, jnp.max(g, axis=0, keepdims=True)), a_l, g_col)
        wgt = each(lambda g, m: jnp.exp(g - m), g_col, m_new)
        vk = each(lambda v, w, k: _mm1((v * w).T, k), vc, wgt, kc)
        decay = each(lambda a, m: jnp.exp(a - m), a_l, m_new)
        num = each(lambda i_, x, y: i_ * x + y, inter, qcm, sv)
        den = each(lambda i_, q, n_, s_: i_ * jnp.sum(q * n_, axis=-1, keepdims=True)
                   + jnp.sum(s_, axis=-1, keepdims=True), inter, qc, nv, s)
        hh = each(lambda n_, d, m: n_ / jnp.maximum(jnp.abs(d), jnp.exp(-m)), num, den, m_t)
        for z in range(2):
            hs_ref[z, per_dir[z][0], :] = jnp.concatenate(hh[z * nh:(z + 1) * nh], axis=1)
        for i in range(n_st):
            c_ref[i] = decay[i] * cm[i] + vk[i]
            n_ref[i] = decay[i] * nv[i] + jnp.sum(wgt[i] * kc[i], axis=0, keepdims=True)
            m_ref[i] = m_new[i]
        return carry

    lax.fori_loop(0, nc, chunk_step, 0)

    if emit_state:
        for i in range(n_st):
            z, h = divmod(i, ML_HEADS)
            co_ref[z, h] = c_ref[i]
            no_ref[z, h] = n_ref[i]
            mo_ref[z, h] = m_ref[i]

    for h in range(ML_HEADS):
        hsl = slice(h * ML_DH, (h + 1) * ML_DH)
        hn = _head_rms(hs_ref[0, :, hsl] + hs_ref[1, :, hsl], ng_ref[:, hsl])
        y_ref[:, hsl] = (_sigmoid(o_ref[:, hsl]) * hn).astype(y_ref.dtype)


def _mlstm(u, gate_rows, p, layer, n_seq, seq, row_off, state=None, emit_state=False, prev=()):
    lw = lambda shape: pl.BlockSpec((None,) + shape, lambda b: (layer,) + (0,) * len(shape))
    nc = seq // CHUNK
    n_gate = 4 * ML_HEADS
    in_specs = [_u_spec(seq, cb, row_off) for cb in (CB_MLQ, CB_MLK, CB_MLV, CB_MLO, CB_GATE)] + [
        pl.BlockSpec((None, nc, n_gate, CHUNK), lambda b: (b, 0, 0, 0)),
        lw((1, n_gate)), lw((n_gate, 1)), lw((1, MIX_W))]
    args = [u, u, u, u, u, gate_rows, p['ml_gate_b'].reshape(DEPTH, 1, n_gate),
            p['ml_gate_b'].reshape(DEPTH, n_gate, 1), p['ml_norm_g'].reshape(DEPTH, 1, MIX_W)]
    c_shape, n_shape, m_shape = (2, ML_HEADS, ML_DH, ML_DH), (2, ML_HEADS, 1, ML_DH), (2, ML_HEADS, 1, 1)
    if state is not None:
        c0, n0, m0 = state
        for a, shp in ((c0, c_shape), (n0, n_shape), (m0, m_shape)):
            in_specs.append(pl.BlockSpec((None, None) + shp, lambda b: (b, layer, 0, 0, 0, 0)))
            args.append(a.reshape(a.shape[:2] + shp))
    out_shape = [jax.ShapeDtypeStruct((n_seq * seq, MIX_W), BF16)]
    out_specs = [pl.BlockSpec((seq, MIX_W), lambda b: (b, 0))]
    if emit_state:
        for shp in (c_shape, n_shape, m_shape):
            out_shape.append(jax.ShapeDtypeStruct((n_seq, DEPTH) + shp, F32))
            out_specs.append(_state_spec(shp, layer, not prev, lambda b: (b, 0, 0, 0, 0)))
    n_st = 2 * ML_HEADS
    body, prev_specs = _carry_through(
        functools.partial(_mlstm_kernel, seq=seq, has_s0=state is not None, emit_state=emit_state),
        len(in_specs), prev, 1, 3 if emit_state else 0, layer)
    return pl.pallas_call(
        body,
        out_shape=tuple(out_shape), grid=(n_seq,), in_specs=in_specs + prev_specs, out_specs=tuple(out_specs),
        input_output_aliases={len(in_specs) + i: 1 + i for i in range(len(prev))},
        scratch_shapes=[pltpu.VMEM((2, seq, MIX_W), F32), pltpu.VMEM((n_st, ML_DH, ML_DH), F32),
                        pltpu.VMEM((n_st, 1, ML_DH), F32), pltpu.VMEM((n_st, 1, 1), F32)],
        compiler_params=_params(("parallel",)),
        name=f"mlstm_{seq}",
    )(*args, *prev)


def _outproj_kernel(*refs, n_prompt_tiles):
    n_mix = 4
    yp_refs, ys_refs = refs[:n_mix], refs[n_mix:2 * n_mix]
    w_ref, x_ref, g_ref, o_ref = refs[2 * n_mix:]
    is_prompt = pl.program_id(0) < n_prompt_tiles
    acc = None
    for i, (yp_ref, ys_ref) in enumerate(zip(yp_refs, ys_refs)):
        y = jnp.where(is_prompt, yp_ref[...], ys_ref[...])
        part = _mm(y, w_ref[i * MIX_W:(i + 1) * MIX_W, :].astype(BF16))
        acc = part if acc is None else acc + part
    o_ref[...] = x_ref[...] + g_ref[...] * acc


def _outproj(ys_prompt, ys_sample, w_out, x, mod, layer):
    tm, tn = 1024, 1024
    n_prompt_tiles = N_PROMPT // tm
    yp_spec = pl.BlockSpec((tm, MIX_W), lambda i, j: (jnp.minimum(i, n_prompt_tiles - 1), 0))
    ys_spec = pl.BlockSpec((tm, MIX_W), lambda i, j: (jnp.maximum(i - n_prompt_tiles, 0), 0))
    return pl.pallas_call(
        functools.partial(_outproj_kernel, n_prompt_tiles=n_prompt_tiles),
        out_shape=jax.ShapeDtypeStruct((N_TOK, D_MODEL), F32),
        grid=(N_TOK // tm, D_MODEL // tn),
        in_specs=[yp_spec] * 4 + [ys_spec] * 4 + [pl.BlockSpec((None, D_MODEL, tn), lambda i, j: (layer, 0, j)),
                                                   pl.BlockSpec((tm, tn), lambda i, j: (i, j)),
                                                   _mod_spec(layer, 2, tm, tn, with_j=True)],
        out_specs=pl.BlockSpec((tm, tn), lambda i, j: (i, j)),
        compiler_params=_params(("parallel", "parallel")),
        name="outproj",
    )(*ys_prompt, *ys_sample, w_out, x, mod)


def _router_kernel(x_ref, g_ref, sh_ref, sc_ref, rw_ref, rb_ref, xm_ref, gt_ref, ids_ref):
    xm = _modulated(x_ref[...], g_ref[...], sh_ref[...], sc_ref[...])
    xm_ref[...] = xm
    logits = _mm_nt(rw_ref[...], xm, HI)
    ex = jnp.exp(logits - jnp.max(logits, axis=0, keepdims=True))
    scores = ex / jnp.sum(ex, axis=0, keepdims=True)
    sel = scores + rb_ref[...]
    per = N_EXPERTS // N_EXPERT_GROUPS
    s = [sel[e:e + 1, :] for e in range(N_EXPERTS)]
    grp_score = []
    for g in range(N_EXPERT_GROUPS):
        a, b, c, d = s[per * g:per * (g + 1)]
        hi1, lo1, hi2, lo2 = jnp.maximum(a, b), jnp.minimum(a, b), jnp.maximum(c, d), jnp.minimum(c, d)
        grp_score.append(jnp.maximum(hi1, hi2) + jnp.maximum(jnp.minimum(hi1, hi2), jnp.maximum(lo1, lo2)))
    best = functools.reduce(jnp.maximum, grp_score)
    in_grp, taken = [], jnp.zeros_like(best)
    for g in range(N_EXPERT_GROUPS):
        hit = jnp.where(grp_score[g] == best, 1.0, 0.0) * (1.0 - taken)
        in_grp.append(hit)
        taken = taken + hit
    picked, flag = [], []
    for e in range(N_EXPERTS):
        g = e // per
        rank = jnp.zeros_like(best)
        for o in range(per * g, per * (g + 1)):
            if o < e:
                rank += jnp.where(s[o] >= s[e], 1.0, 0.0)
            elif o > e:
                rank += jnp.where(s[o] > s[e], 1.0, 0.0)
        flag.append(in_grp[g] * jnp.where(rank < 2.0, 1.0, 0.0))
        picked.append(flag[e] * scores[e:e + 1, :])
    total = functools.reduce(lambda x, y: x + y, picked)
    for e in range(N_EXPERTS):
        gt_ref[e:e + 1, :] = picked[e] / total
    lo_id = functools.reduce(
        jnp.minimum, [jnp.where(flag[e] > 0.0, float(e), float(N_EXPERTS)) for e in range(N_EXPERTS)])
    hi_id = functools.reduce(jnp.maximum, [jnp.where(flag[e] > 0.0, float(e), -1.0) for e in range(N_EXPERTS)])
    ids_ref[0:1, :] = lo_id
    ids_ref[1:2, :] = hi_id
    ids_ref[2:, :] = jnp.zeros((ids_ref.shape[0] - 2,) + lo_id.shape[1:], F32)


def _router(x, norm_g, mod, router_wt, router_b, layer):
    tm = 512
    return pl.pallas_call(
        _router_kernel,
        out_shape=(jax.ShapeDtypeStruct((N_TOK, D_MODEL), F32), jax.ShapeDtypeStruct((N_EXPERTS, N_TOK), F32),
                   jax.ShapeDtypeStruct((8, N_TOK), F32)),
        grid=(N_TOK // tm,),
        in_specs=[pl.BlockSpec((tm, D_MODEL), lambda i: (i, 0)),
                  pl.BlockSpec((None, 1, D_MODEL), lambda i: (layer, 0, 0)),
                  _mod_spec(layer, 3, tm), _mod_spec(layer, 4, tm),
                  _full((N_EXPERTS, D_MODEL)), _full((N_EXPERTS, 1))],
        out_specs=(pl.BlockSpec((tm, D_MODEL), lambda i: (i, 0)), pl.BlockSpec((N_EXPERTS, tm), lambda i: (0, i)),
                   pl.BlockSpec((8, tm), lambda i: (0, i))),
        compiler_params=_params(("parallel",)),
        name="router",
    )(x, norm_g.reshape(DEPTH, 1, D_MODEL), mod, mod, router_wt, router_b.reshape(N_EXPERTS, 1))


MOE_TM = 512
N_PAIRS = N_EXPERT_GROUPS * 6
MOE_STEPS = 2 * (N_PAIRS + N_TOK // MOE_TM - 1)


def _row_copy(src_hbm, src_row, dst_ref, slot, r, sems):
    return pltpu.make_async_copy(src_hbm.at[pl.ds(src_row, 1), :], dst_ref.at[slot, pl.ds(r, 1), :], sems.at[slot])


def _gather_start(src_hbm, dst_ref, slot, idx_ref, base, n, sems):
    def issue(r, carry):
        _row_copy(src_hbm, idx_ref[base + r], dst_ref, slot, r, sems).start()
        return carry

    lax.fori_loop(0, n, issue, 0, unroll=8)


def _gather_wait(src_hbm, dst_ref, slot, n, sems):
    def wait(r, carry):
        _row_copy(src_hbm, 0, dst_ref, slot, r, sems).wait()
        return carry

    lax.fori_loop(0, n, wait, 0, unroll=8)


def _gather_tile(src_hbm, dst_ref, idx_ref, tile, n_tiles, sems):
    slot = tile % 2

    @pl.when(tile == 0)
    def _():
        _gather_start(src_hbm, dst_ref, slot, idx_ref, 0, MOE_TM, sems)

    _gather_wait(src_hbm, dst_ref, slot, MOE_TM, sems)

    @pl.when(tile + 1 < n_tiles)
    def _():
        _gather_start(src_hbm, dst_ref, 1 - slot, idx_ref, (tile + 1) * MOE_TM, MOE_TM, sems)

    return slot


def _moe_plan(ids):
    i32 = jnp.int32
    lo, hi = ids[0].astype(i32), ids[1].astype(i32)
    src = jnp.argsort(lo * N_EXPERTS + hi).astype(i32)
    pos = jnp.argsort(src).astype(i32)
    n_tiles = N_TOK // MOE_TM
    ex = jnp.arange(N_EXPERTS, dtype=i32)
    lo_s, hi_s = lo[src].reshape(n_tiles, MOE_TM, 1), hi[src].reshape(n_tiles, MOE_TM, 1)
    used = ((lo_s == ex).any(axis=1) | (hi_s == ex).any(axis=1)).reshape(-1)
    n_valid = jnp.sum(used).astype(i32)
    idx = jnp.nonzero(used, size=MOE_STEPS, fill_value=0)[0].astype(i32)
    valid = jnp.arange(MOE_STEPS, dtype=i32) < n_valid
    idx = jnp.where(valid, idx, idx[jnp.maximum(n_valid - 1, 0)])
    tile, exp = idx // N_EXPERTS, idx % N_EXPERTS
    first = valid & (tile != jnp.concatenate([jnp.full((1,), -1, i32), tile[:-1]]))
    return src, pos, tile, exp, first.astype(i32), valid.astype(i32)


def _moe_kernel(tile_ref, exp_ref, first_ref, valid_ref, src_ref, xm_hbm, gates_ref, w1_ref, w3_ref, w2_ref,
                o_ref, xs_ref, xb_ref, sem):
    s = pl.program_id(0)

    @pl.when(first_ref[s] == 1)
    def _():
        slot = _gather_tile(xm_hbm, xs_ref, src_ref, tile_ref[s], N_TOK // MOE_TM, sem)
        xb_ref[...] = xs_ref[slot].astype(BF16)
        o_ref[...] = jnp.zeros_like(o_ref)

    @pl.when(valid_ref[s] == 1)
    def _():
        xb = xb_ref[...]
        h1 = _mm(xb, w1_ref[...].astype(BF16))
        h3 = _mm(xb, w3_ref[...].astype(BF16))
        gates = gates_ref[...]
        lane = lax.broadcasted_iota(jnp.int32, gates.shape, 1)
        gate = jnp.sum(jnp.where(lane == exp_ref[s], gates, 0.0), axis=-1, keepdims=True)
        hh = h1 * _sigmoid(h1) * h3 * gate
        o_ref[...] += _mm(hh.astype(BF16), w2_ref[...].astype(BF16))


def _moe(xm, gates_sorted, plan, w1, w3, w2, layer):
    src, _, tile, exp, first, valid = plan
    w_in = pl.BlockSpec((None, None, D_MODEL, D_EXPERT), lambda s, t, e, f, v, i: (layer, e[s], 0, 0))
    grid_spec = pltpu.PrefetchScalarGridSpec(
        num_scalar_prefetch=5, grid=(MOE_STEPS,),
        in_specs=[pl.BlockSpec(memory_space=pl.ANY),
                  pl.BlockSpec((MOE_TM, N_EXPERTS), lambda s, t, e, f, v, i: (t[s], 0)),
                  w_in, w_in,
                  pl.BlockSpec((None, None, D_EXPERT, D_MODEL), lambda s, t, e, f, v, i: (layer, e[s], 0, 0))],
        out_specs=pl.BlockSpec((MOE_TM, D_MODEL), lambda s, t, e, f, v, i: (t[s], 0)),
        scratch_shapes=[pltpu.VMEM((2, MOE_TM, D_MODEL), F32), pltpu.VMEM((MOE_TM, D_MODEL), BF16),
                        pltpu.SemaphoreType.DMA((2,))])
    return pl.pallas_call(
        _moe_kernel, out_shape=jax.ShapeDtypeStruct((N_TOK, D_MODEL), F32), grid_spec=grid_spec,
        compiler_params=_params(("arbitrary",)), name="moe",
    )(tile, exp, first, valid, src, xm, gates_sorted, w1, w3, w2)


def _combine_kernel(pos_ref, acc_hbm, x_ref, g_ref, *refs, split):
    i = pl.program_id(0)
    if split:
        *out_refs, buf_ref, sem = refs
    else:
        ng_ref, nsh_ref, nsc_ref, *out_refs, buf_ref, sem = refs
    slot = _gather_tile(acc_hbm, buf_ref, pos_ref, i, N_TOK // MOE_TM, sem)
    val = x_ref[...] + g_ref[...] * buf_ref[slot]
    if not split:
        out_refs[0][...] = val
        out_refs[1][...] = _modulated(val, ng_ref[...], nsh_ref[...], nsc_ref[...]).astype(BF16)
        return
    n_prompt_tiles = N_PROMPT // MOE_TM

    @pl.when(i < n_prompt_tiles)
    def _():
        out_refs[0][...] = val

    @pl.when(i >= n_prompt_tiles)
    def _():
        out_refs[1][...] = val


def _combine(acc_sorted, pos, x, mod, norm1_g, layer):
    tm = MOE_TM
    split = layer == DEPTH - 1
    n_prompt_tiles = N_PROMPT // tm
    row = pl.BlockSpec((tm, D_MODEL), lambda i, p: (i, 0))
    mod_row = lambda l, chunk: pl.BlockSpec((None, None, None, 1, D_MODEL),
                                            lambda i, p: (l, _mod_row(i, tm), chunk, 0, 0))
    in_specs = [pl.BlockSpec(memory_space=pl.ANY), row, mod_row(layer, 5)]
    args = [pos, acc_sorted, x, mod]
    if split:
        out_shape = (jax.ShapeDtypeStruct((N_PROMPT, D_MODEL), F32), jax.ShapeDtypeStruct((N_SAMPLE, D_MODEL), F32))
        out_specs = (pl.BlockSpec((tm, D_MODEL), lambda i, p: (jnp.minimum(i, n_prompt_tiles - 1), 0)),
                     pl.BlockSpec((tm, D_MODEL), lambda i, p: (jnp.maximum(i - n_prompt_tiles, 0), 0)))
    else:
        in_specs += [pl.BlockSpec((None, 1, D_MODEL), lambda i, p: (layer + 1, 0, 0)),
                     mod_row(layer + 1, 0), mod_row(layer + 1, 1)]
        args += [norm1_g.reshape(DEPTH, 1, D_MODEL), mod, mod]
        out_shape = (jax.ShapeDtypeStruct((N_TOK, D_MODEL), F32), jax.ShapeDtypeStruct((N_TOK, D_MODEL), BF16))
        out_specs = (row, row)
    grid_spec = pltpu.PrefetchScalarGridSpec(
        num_scalar_prefetch=1, grid=(N_TOK // tm,), in_specs=in_specs, out_specs=out_specs,
        scratch_shapes=[pltpu.VMEM((2, tm, D_MODEL), F32), pltpu.SemaphoreType.DMA((2,))])
    return pl.pallas_call(
        functools.partial(_combine_kernel, split=split), out_shape=out_shape, grid_spec=grid_spec,
        compiler_params=_params(("arbitrary",)), name="combine",
    )(*args)


def _gate_rows(gcols, n_seq, seq):
    return gcols.reshape(n_seq, seq // CHUNK, CHUNK, gcols.shape[-1]).transpose(0, 1, 3, 2)


def kernel(x_prompt, x_sample, cache_na_k, cache_na_v, state_rwkv, state_mlstm_c, state_mlstm_n, state_mlstm_m,
           c, c_ctx, norm1_g, norm2_g, w_mod, b_mod, w_in, conv_w, na_q_g, na_k_g, na_rpb, rw_w0, rw_w_up, rw_a0,
           rw_a_up, rw_g_up, rw_k_k, rw_k_a, rw_r_k, rw_ln_g, rw_ln_b, ml_gate_b, ml_norm_g, w_out, router_w,
           router_b, moe_w1, moe_w3, moe_w2):
    p = dict(rw_w0=rw_w0, rw_w_up=rw_w_up, rw_a0=rw_a0, rw_a_up=rw_a_up, rw_g_up=rw_g_up, rw_k_k=rw_k_k,
             rw_k_a=rw_k_a, rw_r_k=rw_r_k, rw_ln_g=rw_ln_g, rw_ln_b=rw_ln_b, ml_gate_b=ml_gate_b,
             ml_norm_g=ml_norm_g)
    cvecs = jnp.concatenate([c_ctx[None], c, jnp.zeros((MOD_ROWS - 1 - DEC_BATCH, D_MODEL), F32)], axis=0)
    mod = _adaln(cvecs, w_mod, b_mod).reshape(DEPTH, MOD_ROWS, 6, 1, D_MODEL)

    assert w_in.shape[-1] == P_IN
    w_in_t = jnp.swapaxes(w_in, 1, 2)
    tw = _na_tables(na_rpb)
    router_wt = router_w.T
    sample_row_off = N_PROMPT // DEC_SEQ

    x = jnp.concatenate([x_prompt.reshape(N_PROMPT, D_MODEL), x_sample.reshape(N_SAMPLE, D_MODEL)], axis=0)
    xm1 = _premod(x, norm1_g, mod, 0)
    kv_prev, rw_prev, ml_prev = (), (), ()
    for l in range(DEPTH):
        u = _inproj(xm1, w_in_t, l)
        ya_p, yb_p, *kv_prev = _attn_prompt(u, conv_w, na_q_g, na_k_g, l, prev=tuple(kv_prev))
        ya_s, yb_s = _na_sample(u, cache_na_k, cache_na_v, tw, conv_w, na_q_g, na_k_g, l)
        yc_p, *rw_prev = _rwkv(u, p, l, BATCH, SEQ, 0, emit_state=True, prev=tuple(rw_prev))
        (yc_s,) = _rwkv(u, p, l, DEC_BATCH, DEC_SEQ, sample_row_off, s0=state_rwkv)
        g0 = CB_GATE * MIX_W + SM_GATE
        gcols = u[:, g0:g0 + 4 * ML_HEADS]
        yd_p, *ml_prev = _mlstm(u, _gate_rows(gcols[:N_PROMPT], BATCH, SEQ), p, l, BATCH, SEQ, 0,
                                emit_state=True, prev=tuple(ml_prev))
        (yd_s,) = _mlstm(u, _gate_rows(gcols[N_PROMPT:], DEC_BATCH, DEC_SEQ), p, l, DEC_BATCH, DEC_SEQ,
                         sample_row_off, state=(state_mlstm_c, state_mlstm_n, state_mlstm_m))
        x = _outproj((ya_p, yb_p, yc_p, yd_p), (ya_s, yb_s, yc_s, yd_s), w_out, x, mod, l)
        xm, gates_t, ids = _router(x, norm2_g, mod, router_wt, router_b, l)
        plan = _moe_plan(ids)
        acc_sorted = _moe(xm, gates_t.T[plan[0]], plan, moe_w1, moe_w3, moe_w2, l)
        combined = _combine(acc_sorted, plan[1], x, mod, norm1_g, l)
        if l < DEPTH - 1:
            x, xm1 = combined
    y_prompt, y_sample = combined
    new_c, new_n, new_m = ml_prev
    return (y_prompt.reshape(BATCH, SEQ, D_MODEL), y_sample.reshape(DEC_BATCH, DEC_SEQ, D_MODEL),
            kv_prev[0], kv_prev[1], rw_prev[0], new_c, new_n.reshape(BATCH, DEPTH, 2, ML_HEADS, ML_DH),
            new_m.reshape(BATCH, DEPTH, 2, ML_HEADS))
```

```python
import functools

import numpy as np
import jax
import jax.numpy as jnp
from jax import lax
from jax.experimental import pallas as pl
from jax.experimental.pallas import tpu as pltpu

F32 = jnp.float32
BF16 = jnp.bfloat16
HI = lax.Precision.HIGHEST

D_MODEL = 2048
BATCH = 16
SEQ = 256
DEPTH = 2
DEC_BATCH = 2
DEC_SEQ = 1024
PAST_LEN = 512
GRID_W = 64
MIX_W = D_MODEL // 4
CONV_K = 3
NA_DH = 64
NA_HEADS = MIX_W // NA_DH
NA_WIN_R = 8
NA_WIN_C = 16
NA_SCALE = NA_DH ** -0.5
ROPE_THETA = 10000.0
RW_DH = 64
RW_HEADS = MIX_W // RW_DH
RW_W_RANK = 64
RW_A_RANK = 64
RW_G_RANK = 128
RW_DECAY_SCALE = 0.606531
RW_GN_EPS = 64e-5
ML_DH = 128
ML_HEADS = MIX_W // ML_DH
N_EXPERTS = 16
N_EXPERT_GROUPS = 4
D_EXPERT = 512
EPS = 1e-6
NEG_INF = -1e30

N_PROMPT = BATCH * SEQ
N_SAMPLE = DEC_BATCH * DEC_SEQ
N_TOK = N_PROMPT + N_SAMPLE
MOD_ROWS = 8
CHUNK = 64
SUB = 16
P1_CHUNKS = 4
NA_ROWS_PER_TRIP = 4
P_IN = 13 * MIX_W + RW_W_RANK + RW_A_RANK + RW_G_RANK + 4 * ML_HEADS
P_BLOCKS = 15
P_PAD = P_BLOCKS * MIX_W
(CB_CVB, CB_CVC, CB_CVH, CB_NAQ, CB_NAK, CB_NAV, CB_RWR, CB_RWK, CB_RWV,
 CB_MLQ, CB_MLK, CB_MLV, CB_MLO, CB_SMALL, CB_GATE) = range(P_BLOCKS)
SM_WL, SM_AL, SM_GL = 0, 64, 128
SM_GATE = MIX_W - 4 * ML_HEADS
VMEM_LIMIT = 56 * 1024 * 1024


def _mm(a, b, prec=None):
    return jnp.dot(a, b, precision=prec, preferred_element_type=F32)


def _mm_nt(a, b, prec=None):
    return lax.dot_general(a, b, (((1,), (1,)), ((), ())), precision=prec, preferred_element_type=F32)


def _sigmoid(x):
    return 1.0 / (1.0 + jnp.exp(-x))


def _full(shape):
    n = len(shape)
    return pl.BlockSpec(shape, lambda *_: (0,) * n)


def _params(sem):
    return pltpu.CompilerParams(dimension_semantics=sem, vmem_limit_bytes=VMEM_LIMIT)


def _mod_row(i, tm):
    n_prompt_tiles = N_PROMPT // tm
    tiles_per_sample = DEC_SEQ // tm
    return jnp.where(i < n_prompt_tiles, 0, 1 + (i - n_prompt_tiles) // tiles_per_sample)


def _mod_spec(layer, chunk, tm, tn=D_MODEL, with_j=False):
    if with_j:
        return pl.BlockSpec((None, None, None, 1, tn), lambda i, j: (layer, _mod_row(i, tm), chunk, 0, j))
    return pl.BlockSpec((None, None, None, 1, tn), lambda i, *_: (layer, _mod_row(i, tm), chunk, 0, 0))


def _tri_masks(n):
    t = lax.broadcasted_iota(jnp.int32, (n, n), 0)
    s = lax.broadcasted_iota(jnp.int32, (n, n), 1)
    incl = ((s <= t).astype(F32), (s >= t).astype(F32))
    strict = ((s < t).astype(F32), (s > t).astype(F32))
    return incl, strict, t, s


def _split2(x):
    hi = x.astype(BF16)
    return hi, (x - hi.astype(F32)).astype(BF16)


def _mm1(a, b):
    return _mm(a.astype(BF16), b.astype(BF16))


def _mm3(a, b):
    a_hi, a_lo = _split2(a)
    b_hi, b_lo = _split2(b)
    return _mm(a_hi, b_lo) + _mm(a_lo, b_hi) + _mm(a_hi, b_hi)


def _split3(x):
    x0 = x.astype(BF16)
    r1 = x - x0.astype(F32)
    x1 = r1.astype(BF16)
    return x0, x1, (r1 - x1.astype(F32)).astype(BF16)


def _mm_exact_lhs(mask_b, x):
    x0, x1, x2 = _split3(x)
    return _mm(mask_b, x2) + _mm(mask_b, x1) + _mm(mask_b, x0)


def _mm_exact_rhs(x, mask_b):
    x0, x1, x2 = _split3(x)
    return _mm(x2, mask_b) + _mm(x1, mask_b) + _mm(x0, mask_b)


def _adaln_kernel(cv_ref, w_ref, b_ref, o_ref):
    cv = cv_ref[...]
    a0, a1, a2 = _split3(cv * _sigmoid(cv))
    w_hi, w_lo = _split2(w_ref[...])
    o_ref[...] = ((_mm(a2, w_hi) + _mm(a1, w_lo)) + (_mm(a1, w_hi) + _mm(a0, w_lo)) + _mm(a0, w_hi)) + b_ref[...]


def _adaln(cvecs, w_mod, b_mod):
    tn = 1024
    n_out = 6 * D_MODEL
    return pl.pallas_call(
        _adaln_kernel,
        out_shape=jax.ShapeDtypeStruct((DEPTH, MOD_ROWS, n_out), F32),
        grid=(DEPTH, n_out // tn),
        in_specs=[_full((MOD_ROWS, D_MODEL)),
                  pl.BlockSpec((None, D_MODEL, tn), lambda l, j: (l, 0, j)),
                  pl.BlockSpec((None, 1, tn), lambda l, j: (l, 0, j))],
        out_specs=pl.BlockSpec((None, MOD_ROWS, tn), lambda l, j: (l, 0, j)),
        compiler_params=_params(("parallel", "parallel")),
        name="adaln",
    )(cvecs, w_mod, b_mod.reshape(DEPTH, 1, n_out))


def _modulated(x, g, sh, sc):
    y = x * lax.rsqrt(jnp.mean(x * x, axis=-1, keepdims=True) + EPS) * g
    return y * (1.0 + sc) + sh


def _premod_kernel(xp_ref, xs_ref, g_ref, sh_ref, sc_ref, x_ref, xm_ref, *, n_prompt_tiles):
    x = jnp.where(pl.program_id(0) < n_prompt_tiles, xp_ref[...], xs_ref[...])
    x_ref[...] = x
    xm_ref[...] = _modulated(x, g_ref[...], sh_ref[...], sc_ref[...]).astype(BF16)


def _premod(x_prompt, x_sample, norm_g, mod):
    tm = 512
    n_prompt_tiles = N_PROMPT // tm
    row = pl.BlockSpec((tm, D_MODEL), lambda i: (i, 0))
    return pl.pallas_call(
        functools.partial(_premod_kernel, n_prompt_tiles=n_prompt_tiles),
        out_shape=(jax.ShapeDtypeStruct((N_TOK, D_MODEL), F32), jax.ShapeDtypeStruct((N_TOK, D_MODEL), BF16)),
        grid=(N_TOK // tm,),
        in_specs=[pl.BlockSpec((tm, D_MODEL), lambda i: (jnp.minimum(i, n_prompt_tiles - 1), 0)),
                  pl.BlockSpec((tm, D_MODEL), lambda i: (jnp.maximum(i - n_prompt_tiles, 0), 0)),
                  pl.BlockSpec((None, 1, D_MODEL), lambda i: (0, 0, 0)),
                  _mod_spec(0, 0, tm), _mod_spec(0, 1, tm)],
        out_specs=(row, row),
        compiler_params=_params(("parallel",)),
        name="premod",
    )(x_prompt, x_sample, norm_g.reshape(DEPTH, 1, D_MODEL), mod, mod)


def _inproj_kernel(xm_ref, w_ref, o_ref):
    o_ref[...] = _mm_nt(xm_ref[...], w_ref[0].astype(BF16))


def _inproj_src_row(j):
    a = 9
    narrow = RW_W_RANK + RW_A_RANK + RW_G_RANK
    g = 16
    return g * jnp.where(j < a, j * (MIX_W // g),
                         jnp.where(j < CB_SMALL, j * (MIX_W // g) + narrow // g,
                                   jnp.where(j == CB_SMALL, a * MIX_W // g, (P_IN - MIX_W) // g)))


def _inproj(xm, w_in_t, layer):
    tm, tn = 3072, MIX_W
    return pl.pallas_call(
        _inproj_kernel,
        out_shape=jax.ShapeDtypeStruct((N_TOK, P_PAD), F32),
        grid=(N_TOK // tm, P_BLOCKS),
        in_specs=[pl.BlockSpec((tm, D_MODEL), lambda i, j: (i, 0)),
                  pl.BlockSpec((pl.Element(1), pl.Element(tn), pl.Element(D_MODEL)),
                               lambda i, j: (layer, _inproj_src_row(j), 0))],
        out_specs=pl.BlockSpec((tm, tn), lambda i, j: (i, j)),
        compiler_params=_params(("parallel", "parallel")),
        name="inproj",
    )(xm, w_in_t)


def _conv_mix(b, c, h, w):
    u = c * h
    n = u.shape[0]
    row = lax.broadcasted_iota(jnp.int32, u.shape, 0)
    prev = jnp.where(row == 0, 0.0, pltpu.roll(u, 1, axis=0))
    nxt = jnp.where(row == n - 1, 0.0, pltpu.roll(u, n - 1, axis=0))
    return b * (prev * w[0:1] + u * w[1:2] + nxt * w[2:3])


def _head_rms(x, g):
    return x * lax.rsqrt(jnp.mean(x * x, axis=-1, keepdims=True) + EPS) * g


def _attn_prompt_kernel(cb_ref, cc_ref, ch_ref, q_ref, k_ref, v_ref, cw_ref, qg_ref, kg_ref,
                        ya_ref, yb_ref, nk_ref, nv_ref):
    ya_ref[...] = _conv_mix(cb_ref[...], cc_ref[...], ch_ref[...], cw_ref[...]).astype(ya_ref.dtype)
    sls = [slice(h * NA_DH, (h + 1) * NA_DH) for h in range(NA_HEADS)]
    qn = [_head_rms(q_ref[:, sl], qg_ref[...]) * NA_SCALE for sl in sls]
    kn = [_head_rms(k_ref[:, sl], kg_ref[...]) for sl in sls]
    vh = [v_ref[:, sl] for sl in sls]
    s = [_mm_nt(q.astype(BF16), k.astype(BF16)) for q, k in zip(qn, kn)]
    p = [jnp.exp(x - jnp.max(x, axis=-1, keepdims=True)) for x in s]
    o = [_mm(x.astype(BF16), v.astype(BF16)) / jnp.sum(x, axis=-1, keepdims=True) for x, v in zip(p, vh)]
    yb_ref[...] = jnp.concatenate(o, axis=1).astype(yb_ref.dtype)
    for h in range(NA_HEADS):
        nk_ref[h] = kn[h]
        nv_ref[h] = vh[h]


def _u_spec(rows, col_block, row_off_blocks=0):
    return pl.BlockSpec((rows, MIX_W), lambda b: (b + row_off_blocks, col_block))


def _carry_through(kernel, n_inputs, prev, n_plain_out, n_state_out, layer):
    n_prev = len(prev)

    def body(*refs):
        ins, rest = refs[:n_inputs], list(refs[n_inputs + n_prev:])
        if not n_prev:
            for i in range(n_plain_out, n_plain_out + n_state_out):
                full = rest[i]
                for d in range(DEPTH):
                    if d != layer:
                        full[d] = jnp.zeros(full.shape[1:], full.dtype)
                rest[i] = full.at[layer]
        return kernel(*ins, *rest)

    return body, [pl.BlockSpec(memory_space=pl.ANY)] * n_prev


def _state_spec(shape, layer, first, index):
    def index_map(*g):
        b, *tail = index(*g)
        return (b, 0 if first else layer, *tail)
    return pl.BlockSpec((None, DEPTH if first else None) + tuple(shape), index_map)


def _attn_prompt(u, conv_w, q_g, k_g, layer, prev=(), n_seq=BATCH, seq=SEQ):
    lw = lambda shape: pl.BlockSpec((None,) + shape, lambda b: (layer,) + (0,) * len(shape))
    y_spec = pl.BlockSpec((seq, MIX_W), lambda b: (b, 0))
    kv_spec = _state_spec((NA_HEADS, seq, NA_DH), layer, not prev, lambda b: (b, 0, 0, 0))
    in_specs = ([_u_spec(seq, cb) for cb in (CB_CVB, CB_CVC, CB_CVH, CB_NAQ, CB_NAK, CB_NAV)]
                + [lw((CONV_K, MIX_W)), lw((1, NA_DH)), lw((1, NA_DH))])
    body, prev_specs = _carry_through(_attn_prompt_kernel, len(in_specs), prev, 2, 2, layer)
    return pl.pallas_call(
        body,
        out_shape=(jax.ShapeDtypeStruct((n_seq * seq, MIX_W), BF16),) * 2
        + (jax.ShapeDtypeStruct((n_seq, DEPTH, NA_HEADS, seq, NA_DH), F32),) * 2,
        grid=(n_seq,),
        in_specs=in_specs + prev_specs,
        out_specs=(y_spec, y_spec, kv_spec, kv_spec),
        input_output_aliases={len(in_specs) + i: 2 + i for i in range(len(prev))},
        compiler_params=_params(("parallel",)),
        name="attn_prompt",
    )(u, u, u, u, u, u, conv_w, q_g.reshape(DEPTH, 1, NA_DH), k_g.reshape(DEPTH, 1, NA_DH), *prev)


def _na_kernel(cb_ref, cc_ref, ch_ref, q_ref, k_ref, v_ref, kc_ref, vc_ref, tw_ref, cos_ref, sin_ref,
               perm_ref, cw_ref, qg_ref, kg_ref, ya_ref, yb_ref, qs_ref, ks_ref, tws_ref):
    rows = DEC_SEQ // GRID_W
    wr = min(NA_WIN_R, rows)
    nw = wr * GRID_W
    ya_ref[...] = _conv_mix(cb_ref[...], cc_ref[...], ch_ref[...], cw_ref[...]).astype(ya_ref.dtype)
    cos, sin, perm_b = cos_ref[...], sin_ref[...], perm_ref[...].astype(BF16)

    def rope(x):
        hi, lo = _split2(x)
        return x * cos + (_mm(hi, perm_b) + _mm(lo, perm_b)) * sin

    for h in range(NA_HEADS):
        sl = slice(h * NA_DH, (h + 1) * NA_DH)
        qs_ref[...] = (rope(_head_rms(q_ref[:, sl], qg_ref[...])) * NA_SCALE).astype(BF16)
        ks_ref[...] = rope(_head_rms(k_ref[:, sl], kg_ref[...])).astype(BF16)
        kch = kc_ref[h].astype(BF16)
        vch = vc_ref[h].astype(BF16)
        for p in range(wr):
            tws_ref[p] = jnp.concatenate([tw_ref[h, j - p + NA_WIN_R - 1] for j in range(wr)], axis=1)

        def rows_step(t, carry):
            rr = [t * NA_ROWS_PER_TRIP + i for i in range(NA_ROWS_PER_TRIP)]
            rs = [jnp.clip(r - wr // 2, 0, rows - wr) for r in rr]
            q0 = [pl.multiple_of(r * GRID_W, GRID_W) for r in rr]
            k0 = [pl.multiple_of(x * GRID_W, GRID_W) for x in rs]
            q_r = [qs_ref[pl.ds(x, GRID_W), :] for x in q0]
            s_w = [_mm_nt(q, ks_ref[pl.ds(k, nw), :]) + tws_ref[r - x] for q, k, r, x in zip(q_r, k0, rr, rs)]
            s_c = [_mm_nt(q, kch) for q in q_r]
            m = [jnp.maximum(jnp.max(a, axis=-1, keepdims=True), jnp.max(b, axis=-1, keepdims=True))
                 for a, b in zip(s_w, s_c)]
            p_w = [jnp.exp(a - x) for a, x in zip(s_w, m)]
            p_c = [jnp.exp(b - x) for b, x in zip(s_c, m)]
            den = [jnp.sum(a, axis=-1, keepdims=True) + jnp.sum(b, axis=-1, keepdims=True) for a, b in zip(p_w, p_c)]
            v_w = [v_ref[pl.ds(k, nw), sl].astype(BF16) for k in k0]
            o = [(_mm(a.astype(BF16), v) + _mm(b.astype(BF16), vch)) / d for a, b, v, d in zip(p_w, p_c, v_w, den)]
            for x, val in zip(q0, o):
                yb_ref[pl.ds(x, GRID_W), sl] = val.astype(yb_ref.dtype)
            return carry

        lax.fori_loop(0, rows // NA_ROWS_PER_TRIP, rows_step, 0)


def _na_tables(rpb):
    qc = np.arange(GRID_W)
    kc = np.arange(GRID_W)
    wstart = np.clip(qc - NA_WIN_C // 2, 0, GRID_W - NA_WIN_C)
    colmask = (kc[None, :] >= wstart[:, None]) & (kc[None, :] < wstart[:, None] + NA_WIN_C)
    dc = np.clip(kc[None, :] - qc[:, None], -(NA_WIN_C - 1), NA_WIN_C - 1) + NA_WIN_C - 1
    pick = (np.arange(2 * NA_WIN_C - 1)[:, None] == dc.reshape(1, -1)).astype(np.float32)
    bias = jnp.einsum('lhrd,dn->lhrn', rpb, jnp.asarray(pick), precision=HI)
    return jnp.where(colmask[None, None, None], bias.reshape(rpb.shape[:3] + dc.shape), NEG_INF)


def _rope_tables():
    t = np.arange(DEC_SEQ)
    quarter = NA_DH // 4
    freq = ROPE_THETA ** (-np.arange(quarter, dtype=np.float32) / quarter)
    ang_r = (t // GRID_W).astype(np.float32)[:, None] * freq
    ang_c = (t % GRID_W).astype(np.float32)[:, None] * freq
    cos = np.concatenate([np.cos(ang_r), np.cos(ang_r), np.cos(ang_c), np.cos(ang_c)], axis=-1)
    sin = np.concatenate([-np.sin(ang_r), np.sin(ang_r), -np.sin(ang_c), np.sin(ang_c)], axis=-1)
    src = np.concatenate([np.arange(quarter) + quarter, np.arange(quarter),
                          np.arange(quarter) + 3 * quarter, np.arange(quarter) + 2 * quarter])
    perm = np.zeros((NA_DH, NA_DH), np.float32)
    perm[src, np.arange(NA_DH)] = 1.0
    return cos.astype(np.float32), sin.astype(np.float32), perm


def _na_sample(u, cache_k, cache_v, tw, conv_w, q_g, k_g, layer, n_seq=DEC_BATCH, row_off=N_PROMPT // DEC_SEQ):
    cos, sin, perm = _rope_tables()
    lw = lambda shape: pl.BlockSpec((None,) + shape, lambda b: (layer,) + (0,) * len(shape))
    y_spec = pl.BlockSpec((DEC_SEQ, MIX_W), lambda b: (b, 0))
    c_spec = pl.BlockSpec((None, None, NA_HEADS, PAST_LEN, NA_DH), lambda b: (b, layer, 0, 0, 0))
    wr = min(NA_WIN_R, DEC_SEQ // GRID_W)
    n_off = 2 * NA_WIN_R - 1
    return pl.pallas_call(
        _na_kernel,
        out_shape=(jax.ShapeDtypeStruct((n_seq * DEC_SEQ, MIX_W), BF16),) * 2,
        grid=(n_seq,),
        in_specs=[_u_spec(DEC_SEQ, cb, row_off) for cb in (CB_CVB, CB_CVC, CB_CVH, CB_NAQ, CB_NAK, CB_NAV)]
        + [c_spec, c_spec, lw((NA_HEADS, n_off, GRID_W, GRID_W)),
           _full((DEC_SEQ, NA_DH)), _full((DEC_SEQ, NA_DH)), _full((NA_DH, NA_DH)),
           lw((CONV_K, MIX_W)), lw((1, NA_DH)), lw((1, NA_DH))],
        out_specs=(y_spec, y_spec),
        scratch_shapes=[pltpu.VMEM((DEC_SEQ, NA_DH), BF16), pltpu.VMEM((DEC_SEQ, NA_DH), BF16),
                        pltpu.VMEM((wr, GRID_W, wr * GRID_W), F32)],
        compiler_params=_params(("parallel",)),
        name="na_sample",
    )(u, u, u, u, u, u, cache_k, cache_v, tw, jnp.asarray(cos), jnp.asarray(sin), jnp.asarray(perm),
      conv_w, q_g.reshape(DEPTH, 1, NA_DH), k_g.reshape(DEPTH, 1, NA_DH))


def _seg_ones(width, seg):
    a = lax.broadcasted_iota(jnp.int32, (width, width), 0) // seg
    b = lax.broadcasted_iota(jnp.int32, (width, width), 1) // seg
    return (a == b).astype(F32)


def _rwkv_kernel(*refs, seq, has_s0, emit_state):
    it = iter(refs)
    r_ref, k_ref, v_ref, sm_ref = (next(it) for _ in range(4))
    (w0_ref, wup_ref, a0_ref, aup_ref, gup_ref, kkp_ref, kap_ref, rkp_ref, lng_ref, lnb_ref) = (
        next(it) for _ in range(10))
    mask_ref = next(it)
    s0_ref = next(it) if has_s0 else None
    y_ref = next(it)
    so_ref = next(it) if emit_state else None
    kk_ref, lw_ref, ka_ref, kd_ref, coef_ref, ysp_ref = (next(it) for _ in range(6))

    dh = RW_DH
    pw = 2 * dh
    nc = seq // CHUNK
    seg_b = _seg_ones(pw, dh).astype(BF16)

    r = r_ref[...]
    k = k_ref[...]
    sm = sm_ref[...]
    wl = jnp.tanh(sm[:, SM_WL:SM_WL + RW_W_RANK])
    al = sm[:, SM_AL:SM_AL + RW_A_RANK]
    kk = k * kkp_ref[...]
    kk = kk * lax.rsqrt(_mm_exact_rhs(kk * kk, seg_b) + EPS)
    kk_ref[...] = kk
    for z in range(2):
        lw_ref[z] = -RW_DECAY_SCALE * _sigmoid(w0_ref[z:z + 1, :] + _mm3(wl, wup_ref[z]))
        a = _sigmoid(a0_ref[z:z + 1, :] + _mm3(al, aup_ref[z]))
        ka_ref[z] = kk * a
        kd_ref[z] = k * (1.0 + (a - 1.0) * kap_ref[...])

    cat = jnp.concatenate

    def phase1(c2, carry):
        incl, strict, ti, si = _tri_masks(CHUNK)
        incl_b = (mask_ref[0], mask_ref[1])
        strict2 = tuple(cat([m, m], axis=1) for m in strict)
        incl2 = tuple(cat([m, m], axis=1) for m in incl)
        diag_blk = (ti // SUB == si // SUB).astype(F32)
        eye = (ti == si).astype(F32)
        zero_pair = jnp.zeros((CHUNK, pw), F32)
        head_of_lane = lax.broadcasted_iota(jnp.int32, (CHUNK, pw), 1) // dh
        head_of_lane2 = (lax.broadcasted_iota(jnp.int32, (CHUNK, 2 * pw), 1) // dh) % 2
        r_pw = lax.broadcasted_iota(jnp.int32, (pw, pw), 0)
        c_pw = lax.broadcasted_iota(jnp.int32, (pw, pw), 1)
        same_head = r_pw // dh == c_pw // dh
        eye_pw = (r_pw == c_pw).astype(F32)
        chains = []
        for cc in range(P1_CHUNKS):
            rows = pl.ds(pl.multiple_of((c2 * P1_CHUNKS + cc) * CHUNK, CHUNK), CHUNK)
            vc, rc, kkc = v_ref[rows, :], r_ref[rows, :], kk_ref[rows, :]
            for z in range(2):
                lwc = lw_ref[z, rows, :]
                cum = _mm_exact_lhs(incl_b[z], lwc)
                tot = cum[CHUNK - 1:CHUNK] if z == 0 else cum[0:1]
                e_neg = jnp.exp(-cum)
                dec = jnp.exp(tot - cum)
                kac, kdc = ka_ref[z, rows, :], kd_ref[z, rows, :]
                chains.append((z, kkc * jnp.exp(cum - lwc), rc * jnp.exp(cum), vc, kac * e_neg, kdc * e_neg,
                               kac * dec, kdc * dec, jnp.exp(tot)))
        heads = [(q, j) for q in range(len(chains)) for j in range(2)]
        zq = [ch[0] for ch in chains]
        ymat = [cat([ch[4], ch[5]], axis=0).astype(BF16) for ch in chains]
        vz = [cat([zero_pair, ch[3]], axis=0) for ch in chains]
        nvz = [cat([zero_pair, -ch[3]], axis=1) for ch in chains]
        x_in = [cat([jnp.where(head_of_lane == j, chains[q][1], 0.0),
                     jnp.where(head_of_lane == j, chains[q][2], 0.0)], axis=0) for q, j in heads]
        aa = [_mm_nt(x.astype(BF16), ymat[q]) for x, (q, _) in zip(x_in, heads)]
        top = [a[0:CHUNK] * strict2[zq[q]] for a, (q, _) in zip(aa, heads)]
        a_r = [a[CHUNK:] * incl2[zq[q]] for a, (q, _) in zip(aa, heads)]
        akv = [_mm1(t, vz[q]) for t, (q, _) in zip(top, heads)]
        low = [t[:, 0:CHUNK] for t in top]
        ld = [x * diag_blk for x in low]
        lo = [x - y for x, y in zip(low, ld)]
        l2 = [_mm1(x, x) for x in ld]
        l4 = [_mm1(x, x) for x in l2]
        l8 = [_mm1(x, x) for x in l4]
        td = [eye - x for x in ld]
        for lp in (l2, l4, l8):
            td = [t + _mm1(t, p_) for t, p_ in zip(td, lp)]
        x0 = [_mm1(t, cat([chains[q][1], a, l], axis=1)) for t, a, l, (q, _) in zip(td, akv, lo, heads)]
        pq0 = [x[:, 0:2 * pw] for x in x0]
        wm = [x[:, 2 * pw:] for x in x0]
        pq = pq0
        for _ in range(CHUNK // SUB - 1):
            pq = [p0 - _mm1(w, p_) for p0, w, p_ in zip(pq0, wm, pq)]
        ryc = [_mm1(a, cat([p_, nvz[q]], axis=0)) for a, p_, (q, _) in zip(a_r, pq, heads)]
        for q, ch in enumerate(chains):
            pq_m = jnp.where(head_of_lane2 == 0, pq[2 * q], pq[2 * q + 1])
            ryc_m = jnp.where(head_of_lane2 == 0, ryc[2 * q], ryc[2 * q + 1])
            g1 = _mm1(ch[6].T, pq_m)
            g2 = _mm1(ch[7].T, ch[3])
            g_t = eye_pw * ch[8] - jnp.where(same_head, g1[:, 0:pw], 0.0)
            h_t = jnp.where(same_head, g2 - g1[:, pw:], 0.0)
            coef_ref[c2 * P1_CHUNKS + q // 2, q % 2] = cat(
                [g_t, ch[2] - ryc_m[:, 0:pw], h_t, -ryc_m[:, pw:]], axis=0)
        return carry

    lax.fori_loop(0, nc // P1_CHUNKS, phase1, 0)

    def block_diag(a, b):
        zero = jnp.zeros((dh, dh), F32)
        return cat([cat([a, zero], axis=1), cat([zero, b], axis=1)], axis=0)

    m_init = tuple((block_diag(s0_ref[z, 0].T, s0_ref[z, 1].T) if has_s0 else jnp.zeros((pw, pw), F32))
                   for z in range(2))
    n_lhs = pw + CHUNK

    def phase2(ci, ms):
        new_ms, ys = [], []
        for z in range(2):
            c = ci if z == 0 else nc - 1 - ci
            out = _mm3(coef_ref[c, z, 0:n_lhs, :], ms[z]) + coef_ref[c, z, n_lhs:, :]
            new_ms.append(out[0:pw])
            ys.append(out[pw:])
        ysp_ref[ci] = cat(ys, axis=1)
        return tuple(new_ms)

    m_fin = lax.fori_loop(0, nc, phase2, m_init)

    if emit_state:
        for z in range(2):
            for j in range(2):
                so_ref[z, j] = m_fin[z][j * dh:(j + 1) * dh, j * dh:(j + 1) * dh].T

    y = cat([ysp_ref[c, :, 0:pw] + ysp_ref[nc - 1 - c, :, pw:] for c in range(nc)], axis=0)
    mu = _mm_exact_rhs(y, seg_b) * (1.0 / dh)
    yc = y - mu
    var = _mm_exact_rhs(yc * yc, seg_b) * (1.0 / dh)
    yn = yc * lax.rsqrt(var + RW_GN_EPS) * lng_ref[...] + lnb_ref[...]
    bonus = _mm_exact_rhs(r * k * rkp_ref[...], seg_b) * v_ref[...]
    g = _mm3(_sigmoid(sm[:, SM_GL:SM_GL + RW_G_RANK]), gup_ref[...])
    y_ref[...] = ((yn + bonus) * g).astype(y_ref.dtype)


def _rwkv(u, p, layer, n_seq, seq, row_off, s0=None, emit_state=False, prev=()):
    n_pairs = RW_HEADS // 2
    pw = 2 * RW_DH
    bpc = MIX_W // pw
    lw = lambda shape: pl.BlockSpec((None,) + shape, lambda b, hp: (layer,) + (0,) * (len(shape) - 1) + (hp,))
    row = lambda a: a.reshape(DEPTH, 1, MIX_W)
    u_pair = lambda cb: pl.BlockSpec((seq, pw), lambda b, hp: (b + row_off, cb * bpc + hp))
    in_specs = [u_pair(CB_RWR), u_pair(CB_RWK), u_pair(CB_RWV),
                pl.BlockSpec((seq, MIX_W), lambda b, hp: (b + row_off, CB_SMALL)),
                lw((2, pw)), lw((2, RW_W_RANK, pw)), lw((2, pw)), lw((2, RW_A_RANK, pw)),
                lw((RW_G_RANK, pw))] + [lw((1, pw))] * 5 + [_full((2, CHUNK, CHUNK))]
    t_idx = np.arange(CHUNK)
    incl_masks = np.stack([t_idx[None, :] <= t_idx[:, None], t_idx[None, :] >= t_idx[:, None]])
    args = [u, u, u, u, p['rw_w0'], p['rw_w_up'], p['rw_a0'], p['rw_a_up'], p['rw_g_up'],
            row(p['rw_k_k']), row(p['rw_k_a']), row(p['rw_r_k']), row(p['rw_ln_g']), row(p['rw_ln_b']),
            jnp.asarray(incl_masks, BF16)]
    st_blk = (2, 2, RW_DH, RW_DH)
    if s0 is not None:
        in_specs.append(pl.BlockSpec((None, None) + st_blk, lambda b, hp: (b, layer, 0, hp, 0, 0)))
        args.append(s0)
    out_shape = [jax.ShapeDtypeStruct((n_seq * seq, MIX_W), BF16)]
    out_specs = [pl.BlockSpec((seq, pw), lambda b, hp: (b, hp))]
    if emit_state:
        out_shape.append(jax.ShapeDtypeStruct((n_seq, DEPTH, 2, RW_HEADS, RW_DH, RW_DH), F32))
        out_specs.append(_state_spec(st_blk, layer, not prev, lambda b, hp: (b, 0, hp, 0, 0)))
    nc = seq // CHUNK
    tok = lambda n: pltpu.VMEM((n, seq, pw) if n else (seq, pw), F32)
    body, prev_specs = _carry_through(
        functools.partial(_rwkv_kernel, seq=seq, has_s0=s0 is not None, emit_state=emit_state), len(in_specs), prev,
        1, 1 if emit_state else 0, layer)
    return pl.pallas_call(
        body,
        out_shape=tuple(out_shape), grid=(n_seq, n_pairs), in_specs=in_specs + prev_specs,
        out_specs=tuple(out_specs),
        input_output_aliases={len(in_specs) + i: 1 + i for i in range(len(prev))},
        scratch_shapes=[tok(0), tok(2), tok(2), tok(2),
                        pltpu.VMEM((nc, 2, 2 * (pw + CHUNK), pw), F32), pltpu.VMEM((nc, CHUNK, 2 * pw), F32)],
        compiler_params=_params(("parallel", "parallel")),
        name=f"rwkv_{seq}",
    )(*args, *prev)


def _log_sigmoid(x):
    return jnp.minimum(x, 0.0) - jnp.log(1.0 + jnp.exp(-jnp.abs(x)))


def _mlstm_kernel(*refs, seq, has_s0, emit_state):
    it = iter(refs)
    q_ref, k_ref, v_ref, o_ref, sm_ref, gr_ref, bc_ref, br_ref, ng_ref = (next(it) for _ in range(9))
    c0_ref, n0_ref, m0_ref = ((next(it), next(it), next(it)) if has_s0 else (None, None, None))
    y_ref = next(it)
    co_ref, no_ref, mo_ref = ((next(it), next(it), next(it)) if emit_state else (None, None, None))
    hs_ref, c_ref, n_ref, m_ref = (next(it) for _ in range(4))

    nc = seq // CHUNK
    n_st = 2 * ML_HEADS
    for i in range(n_st):
        z, h = divmod(i, ML_HEADS)
        c_ref[i] = c0_ref[z, h] if has_s0 else jnp.zeros((ML_DH, ML_DH), F32)
        n_ref[i] = n0_ref[z, h] if has_s0 else jnp.zeros((1, ML_DH), F32)
        m_ref[i] = m0_ref[z, h] if has_s0 else jnp.zeros((1, 1), F32)

    incl, _, ti, si = _tri_masks(CHUNK)
    before = ((si <= ti), (si >= ti))

    nh = ML_HEADS
    chains = [(z, h) for z in range(2) for h in range(nh)]
    each = lambda f, *xs: [f(*a) for a in zip(*xs)]
    zs = [z for z, _ in chains]

    def chunk_step(ci, carry):
        per_dir = []
        for z in range(2):
            c = ci if z == 0 else nc - 1 - ci
            rows = pl.ds(pl.multiple_of(c * CHUNK, CHUNK), CHUNK)
            g0 = SM_GATE + z * 2 * nh
            gc = sm_ref[rows, g0:g0 + 2 * nh] + bc_ref[:, z * 2 * nh:(z + 1) * 2 * nh]
            gr = gr_ref[c, z * 2 * nh:(z + 1) * 2 * nh, :] + br_ref[z * 2 * nh:(z + 1) * 2 * nh, :]
            b_cols = _mm(incl[z], _log_sigmoid(gc[:, nh:]), HI)
            b_rows = _mm(_log_sigmoid(gr[nh:]), incl[1 - z], HI)
            per_dir.append((rows, gc[:, :nh], b_cols, gr[:nh], b_rows))
        rows = [per_dir[z][0] for z, _ in chains]
        hsl = [slice(h * ML_DH, (h + 1) * ML_DH) for _, h in chains]
        i_col = [per_dir[z][1][:, h:h + 1] for z, h in chains]
        b_col = [per_dir[z][2][:, h:h + 1] for z, h in chains]
        i_row = [per_dir[z][3][h:h + 1] for z, h in chains]
        b_row = [per_dir[z][4][h:h + 1] for z, h in chains]
        b_last = each(lambda b, z: b[CHUNK - 1:CHUNK] if z == 0 else b[0:1], b_col, zs)
        m_old = [m_ref[i] for i in range(n_st)]
        cm = [c_ref[i] for i in range(n_st)]
        nv = [n_ref[i] for i in range(n_st)]
        qc = each(lambda r, s: q_ref[r, s] * (ML_DH ** -0.5), rows, hsl)
        kc = each(lambda r, s: k_ref[r, s], rows, hsl)
        vc = each(lambda r, s: v_ref[r, s], rows, hsl)
        qk = each(lambda q, k: _mm_nt(q.astype(BF16), k.astype(BF16)), qc, kc)
        qcm = each(lambda q, c_: _mm_nt(q.astype(BF16), c_.astype(BF16)), qc, cm)
        a_t = each(lambda b, m: b + m, b_col, m_old)
        dmat = each(lambda bc, brw, ir, z: jnp.where(before[z], bc - brw + ir, NEG_INF), b_col, b_row, i_row, zs)
        m_t = each(lambda a, d: jnp.maximum(a, jnp.max(d, axis=-1, keepdims=True)), a_t, dmat)
        s = each(lambda x, d, m: x * jnp.exp(d - m), qk, dmat, m_t)
        inter = each(lambda a, m: jnp.exp(a - m), a_t, m_t)
        sv = each(_mm1, s, vc)
        g_col = each(lambda bl, bc, ic: bl - bc + ic, b_last, b_col, i_col)
        a_l = each(lambda bl, m: bl + m, b_last, m_old)
        m_new = each(lambda a, g: jnp.maximum(a, jnp.max(g, axis=0, keepdims=True)), a_l, g_col)
        wgt = each(lambda g, m: jnp.exp(g - m), g_col, m_new)
        vk = each(lambda v, w, k: _mm1((v * w).T, k), vc, wgt, kc)
        decay = each(lambda a, m: jnp.exp(a - m), a_l, m_new)
        num = each(lambda i_, x, y: i_ * x + y, inter, qcm, sv)
        den = each(lambda i_, q, n_, s_: i_ * jnp.sum(q * n_, axis=-1, keepdims=True)
                   + jnp.sum(s_, axis=-1, keepdims=True), inter, qc, nv, s)
        hh = each(lambda n_, d, m: n_ / jnp.maximum(jnp.abs(d), jnp.exp(-m)), num, den, m_t)
        for z in range(2):
            hs_ref[z, per_dir[z][0], :] = jnp.concatenate(hh[z * nh:(z + 1) * nh], axis=1)
        for i in range(n_st):
            c_ref[i] = decay[i] * cm[i] + vk[i]
            n_ref[i] = decay[i] * nv[i] + jnp.sum(wgt[i] * kc[i], axis=0, keepdims=True)
            m_ref[i] = m_new[i]
        return carry

    lax.fori_loop(0, nc, chunk_step, 0)

    if emit_state:
        for i in range(n_st):
            z, h = divmod(i, ML_HEADS)
            co_ref[z, h] = c_ref[i]
            no_ref[z, h] = n_ref[i]
            mo_ref[z, h] = m_ref[i]

    for h in range(ML_HEADS):
        hsl = slice(h * ML_DH, (h + 1) * ML_DH)
        hn = _head_rms(hs_ref[0, :, hsl] + hs_ref[1, :, hsl], ng_ref[:, hsl])
        y_ref[:, hsl] = (_sigmoid(o_ref[:, hsl]) * hn).astype(y_ref.dtype)


def _mlstm(u, gate_rows, p, layer, n_seq, seq, row_off, state=None, emit_state=False, prev=()):
    lw = lambda shape: pl.BlockSpec((None,) + shape, lambda b: (layer,) + (0,) * len(shape))
    nc = seq // CHUNK
    n_gate = 4 * ML_HEADS
    in_specs = [_u_spec(seq, cb, row_off) for cb in (CB_MLQ, CB_MLK, CB_MLV, CB_MLO, CB_GATE)] + [
        pl.BlockSpec((None, nc, n_gate, CHUNK), lambda b: (b, 0, 0, 0)),
        lw((1, n_gate)), lw((n_gate, 1)), lw((1, MIX_W))]
    args = [u, u, u, u, u, gate_rows, p['ml_gate_b'].reshape(DEPTH, 1, n_gate),
            p['ml_gate_b'].reshape(DEPTH, n_gate, 1), p['ml_norm_g'].reshape(DEPTH, 1, MIX_W)]
    c_shape, n_shape, m_shape = (2, ML_HEADS, ML_DH, ML_DH), (2, ML_HEADS, 1, ML_DH), (2, ML_HEADS, 1, 1)
    if state is not None:
        c0, n0, m0 = state
        for a, shp in ((c0, c_shape), (n0, n_shape), (m0, m_shape)):
            in_specs.append(pl.BlockSpec((None, None) + shp, lambda b: (b, layer, 0, 0, 0, 0)))
            args.append(a.reshape(a.shape[:2] + shp))
    out_shape = [jax.ShapeDtypeStruct((n_seq * seq, MIX_W), BF16)]
    out_specs = [pl.BlockSpec((seq, MIX_W), lambda b: (b, 0))]
    if emit_state:
        for shp in (c_shape, n_shape, m_shape):
            out_shape.append(jax.ShapeDtypeStruct((n_seq, DEPTH) + shp, F32))
            out_specs.append(_state_spec(shp, layer, not prev, lambda b: (b, 0, 0, 0, 0)))
    n_st = 2 * ML_HEADS
    body, prev_specs = _carry_through(
        functools.partial(_mlstm_kernel, seq=seq, has_s0=state is not None, emit_state=emit_state),
        len(in_specs), prev, 1, 3 if emit_state else 0, layer)
    return pl.pallas_call(
        body,
        out_shape=tuple(out_shape), grid=(n_seq,), in_specs=in_specs + prev_specs, out_specs=tuple(out_specs),
        input_output_aliases={len(in_specs) + i: 1 + i for i in range(len(prev))},
        scratch_shapes=[pltpu.VMEM((2, seq, MIX_W), F32), pltpu.VMEM((n_st, ML_DH, ML_DH), F32),
                        pltpu.VMEM((n_st, 1, ML_DH), F32), pltpu.VMEM((n_st, 1, 1), F32)],
        compiler_params=_params(("parallel",)),
        name=f"mlstm_{seq}",
    )(*args, *prev)


def _outproj_kernel(*refs, n_prompt_tiles):
    n_mix = 4
    yp_refs, ys_refs = refs[:n_mix], refs[n_mix:2 * n_mix]
    w_ref, x_ref, g_ref, o_ref = refs[2 * n_mix:]
    is_prompt = pl.program_id(0) < n_prompt_tiles
    acc = None
    for i, (yp_ref, ys_ref) in enumerate(zip(yp_refs, ys_refs)):
        y = jnp.where(is_prompt, yp_ref[...], ys_ref[...])
        part = _mm(y, w_ref[i * MIX_W:(i + 1) * MIX_W, :].astype(BF16))
        acc = part if acc is None else acc + part
    o_ref[...] = x_ref[...] + g_ref[...] * acc


def _outproj(ys_prompt, ys_sample, w_out, x, mod, layer):
    tm, tn = 1024, 1024
    n_prompt_tiles = N_PROMPT // tm
    yp_spec = pl.BlockSpec((tm, MIX_W), lambda i, j: (jnp.minimum(i, n_prompt_tiles - 1), 0))
    ys_spec = pl.BlockSpec((tm, MIX_W), lambda i, j: (jnp.maximum(i - n_prompt_tiles, 0), 0))
    return pl.pallas_call(
        functools.partial(_outproj_kernel, n_prompt_tiles=n_prompt_tiles),
        out_shape=jax.ShapeDtypeStruct((N_TOK, D_MODEL), F32),
        grid=(N_TOK // tm, D_MODEL // tn),
        in_specs=[yp_spec] * 4 + [ys_spec] * 4 + [pl.BlockSpec((None, D_MODEL, tn), lambda i, j: (layer, 0, j)),
                                                   pl.BlockSpec((tm, tn), lambda i, j: (i, j)),
                                                   _mod_spec(layer, 2, tm, tn, with_j=True)],
        out_specs=pl.BlockSpec((tm, tn), lambda i, j: (i, j)),
        compiler_params=_params(("parallel", "parallel")),
        name="outproj",
    )(*ys_prompt, *ys_sample, w_out, x, mod)


def _router_kernel(x_ref, g_ref, sh_ref, sc_ref, rw_ref, rb_ref, xm_ref, gt_ref, ids_ref):
    xm = _modulated(x_ref[...], g_ref[...], sh_ref[...], sc_ref[...])
    xm_ref[...] = xm
    logits = _mm_nt(rw_ref[...], xm, HI)
    ex = jnp.exp(logits - jnp.max(logits, axis=0, keepdims=True))
    scores = ex / jnp.sum(ex, axis=0, keepdims=True)
    sel = scores + rb_ref[...]
    per = N_EXPERTS // N_EXPERT_GROUPS
    s = [sel[e:e + 1, :] for e in range(N_EXPERTS)]
    grp_score = []
    for g in range(N_EXPERT_GROUPS):
        a, b, c, d = s[per * g:per * (g + 1)]
        hi1, lo1, hi2, lo2 = jnp.maximum(a, b), jnp.minimum(a, b), jnp.maximum(c, d), jnp.minimum(c, d)
        grp_score.append(jnp.maximum(hi1, hi2) + jnp.maximum(jnp.minimum(hi1, hi2), jnp.maximum(lo1, lo2)))
    best = functools.reduce(jnp.maximum, grp_score)
    in_grp, taken = [], jnp.zeros_like(best)
    for g in range(N_EXPERT_GROUPS):
        hit = jnp.where(grp_score[g] == best, 1.0, 0.0) * (1.0 - taken)
        in_grp.append(hit)
        taken = taken + hit
    picked, flag = [], []
    for e in range(N_EXPERTS):
        g = e // per
        rank = jnp.zeros_like(best)
        for o in range(per * g, per * (g + 1)):
            if o < e:
                rank += jnp.where(s[o] >= s[e], 1.0, 0.0)
            elif o > e:
                rank += jnp.where(s[o] > s[e], 1.0, 0.0)
        flag.append(in_grp[g] * jnp.where(rank < 2.0, 1.0, 0.0))
        picked.append(flag[e] * scores[e:e + 1, :])
    total = functools.reduce(lambda x, y: x + y, picked)
    for e in range(N_EXPERTS):
        gt_ref[e:e + 1, :] = picked[e] / total
    lo_id = functools.reduce(
        jnp.minimum, [jnp.where(flag[e] > 0.0, float(e), float(N_EXPERTS)) for e in range(N_EXPERTS)])
    hi_id = functools.reduce(jnp.maximum, [jnp.where(flag[e] > 0.0, float(e), -1.0) for e in range(N_EXPERTS)])
    ids_ref[0:1, :] = lo_id
    ids_ref[1:2, :] = hi_id
    ids_ref[2:, :] = jnp.zeros((ids_ref.shape[0] - 2,) + lo_id.shape[1:], F32)


def _router(x, norm_g, mod, router_wt, router_b, layer):
    tm = 512
    return pl.pallas_call(
        _router_kernel,
        out_shape=(jax.ShapeDtypeStruct((N_TOK, D_MODEL), F32), jax.ShapeDtypeStruct((N_EXPERTS, N_TOK), F32),
                   jax.ShapeDtypeStruct((8, N_TOK), F32)),
        grid=(N_TOK // tm,),
        in_specs=[pl.BlockSpec((tm, D_MODEL), lambda i: (i, 0)),
                  pl.BlockSpec((None, 1, D_MODEL), lambda i: (layer, 0, 0)),
                  _mod_spec(layer, 3, tm), _mod_spec(layer, 4, tm),
                  _full((N_EXPERTS, D_MODEL)), _full((N_EXPERTS, 1))],
        out_specs=(pl.BlockSpec((tm, D_MODEL), lambda i: (i, 0)), pl.BlockSpec((N_EXPERTS, tm), lambda i: (0, i)),
                   pl.BlockSpec((8, tm), lambda i: (0, i))),
        compiler_params=_params(("parallel",)),
        name="router",
    )(x, norm_g.reshape(DEPTH, 1, D_MODEL), mod, mod, router_wt, router_b.reshape(N_EXPERTS, 1))


MOE_TM = 512
N_PAIRS = N_EXPERT_GROUPS * 6
MOE_STEPS = 2 * (N_PAIRS + N_TOK // MOE_TM - 1)


def _row_copy(src_hbm, src_row, dst_ref, slot, r, sems):
    return pltpu.make_async_copy(src_hbm.at[pl.ds(src_row, 1), :], dst_ref.at[slot, pl.ds(r, 1), :], sems.at[slot])


def _gather_start(src_hbm, dst_ref, slot, idx_ref, base, n, sems):
    def issue(r, carry):
        _row_copy(src_hbm, idx_ref[base + r], dst_ref, slot, r, sems).start()
        return carry

    lax.fori_loop(0, n, issue, 0, unroll=8)


def _gather_wait(src_hbm, dst_ref, slot, n, sems):
    def wait(r, carry):
        _row_copy(src_hbm, 0, dst_ref, slot, r, sems).wait()
        return carry

    lax.fori_loop(0, n, wait, 0, unroll=8)


def _gather_tile(src_hbm, dst_ref, idx_ref, tile, n_tiles, sems):
    slot = tile % 2

    @pl.when(tile == 0)
    def _():
        _gather_start(src_hbm, dst_ref, slot, idx_ref, 0, MOE_TM, sems)

    _gather_wait(src_hbm, dst_ref, slot, MOE_TM, sems)

    @pl.when(tile + 1 < n_tiles)
    def _():
        _gather_start(src_hbm, dst_ref, 1 - slot, idx_ref, (tile + 1) * MOE_TM, MOE_TM, sems)

    return slot


def _moe_plan(ids):
    i32 = jnp.int32
    lo, hi = ids[0].astype(i32), ids[1].astype(i32)
    src = jnp.argsort(lo * N_EXPERTS + hi).astype(i32)
    pos = jnp.argsort(src).astype(i32)
    n_tiles = N_TOK // MOE_TM
    ex = jnp.arange(N_EXPERTS, dtype=i32)
    lo_s, hi_s = lo[src].reshape(n_tiles, MOE_TM, 1), hi[src].reshape(n_tiles, MOE_TM, 1)
    used = ((lo_s == ex).any(axis=1) | (hi_s == ex).any(axis=1)).reshape(-1)
    n_valid = jnp.sum(used).astype(i32)
    idx = jnp.nonzero(used, size=MOE_STEPS, fill_value=0)[0].astype(i32)
    valid = jnp.arange(MOE_STEPS, dtype=i32) < n_valid
    idx = jnp.where(valid, idx, idx[jnp.maximum(n_valid - 1, 0)])
    tile, exp = idx // N_EXPERTS, idx % N_EXPERTS
    first = valid & (tile != jnp.concatenate([jnp.full((1,), -1, i32), tile[:-1]]))
    return src, pos, tile, exp, first.astype(i32), valid.astype(i32)


def _moe_kernel(tile_ref, exp_ref, first_ref, valid_ref, src_ref, xm_hbm, gates_ref, w1_ref, w3_ref, w2_ref,
                o_ref, xs_ref, xb_ref, sem):
    s = pl.program_id(0)

    @pl.when(first_ref[s] == 1)
    def _():
        slot = _gather_tile(xm_hbm, xs_ref, src_ref, tile_ref[s], N_TOK // MOE_TM, sem)
        xb_ref[...] = xs_ref[slot].astype(BF16)
        o_ref[...] = jnp.zeros_like(o_ref)

    @pl.when(valid_ref[s] == 1)
    def _():
        xb = xb_ref[...]
        h1 = _mm(xb, w1_ref[...].astype(BF16))
        h3 = _mm(xb, w3_ref[...].astype(BF16))
        gates = gates_ref[...]
        lane = lax.broadcasted_iota(jnp.int32, gates.shape, 1)
        gate = jnp.sum(jnp.where(lane == exp_ref[s], gates, 0.0), axis=-1, keepdims=True)
        hh = h1 * _sigmoid(h1) * h3 * gate
        o_ref[...] += _mm(hh.astype(BF16), w2_ref[...].astype(BF16))


def _moe(xm, gates_sorted, plan, w1, w3, w2, layer):
    src, _, tile, exp, first, valid = plan
    w_in = pl.BlockSpec((None, None, D_MODEL, D_EXPERT), lambda s, t, e, f, v, i: (layer, e[s], 0, 0))
    grid_spec = pltpu.PrefetchScalarGridSpec(
        num_scalar_prefetch=5, grid=(MOE_STEPS,),
        in_specs=[pl.BlockSpec(memory_space=pl.ANY),
                  pl.BlockSpec((MOE_TM, N_EXPERTS), lambda s, t, e, f, v, i: (t[s], 0)),
                  w_in, w_in,
                  pl.BlockSpec((None, None, D_EXPERT, D_MODEL), lambda s, t, e, f, v, i: (layer, e[s], 0, 0))],
        out_specs=pl.BlockSpec((MOE_TM, D_MODEL), lambda s, t, e, f, v, i: (t[s], 0)),
        scratch_shapes=[pltpu.VMEM((2, MOE_TM, D_MODEL), F32), pltpu.VMEM((MOE_TM, D_MODEL), BF16),
                        pltpu.SemaphoreType.DMA((2,))])
    return pl.pallas_call(
        _moe_kernel, out_shape=jax.ShapeDtypeStruct((N_TOK, D_MODEL), F32), grid_spec=grid_spec,
        compiler_params=_params(("arbitrary",)), name="moe",
    )(tile, exp, first, valid, src, xm, gates_sorted, w1, w3, w2)


def _combine_kernel(pos_ref, acc_hbm, x_ref, g_ref, *refs, split):
    i = pl.program_id(0)
    if split:
        *out_refs, buf_ref, sem = refs
    else:
        ng_ref, nsh_ref, nsc_ref, *out_refs, buf_ref, sem = refs
    slot = _gather_tile(acc_hbm, buf_ref, pos_ref, i, N_TOK // MOE_TM, sem)
    val = x_ref[...] + g_ref[...] * buf_ref[slot]
    if not split:
        out_refs[0][...] = val
        out_refs[1][...] = _modulated(val, ng_ref[...], nsh_ref[...], nsc_ref[...]).astype(BF16)
        return
    n_prompt_tiles = N_PROMPT // MOE_TM

    @pl.when(i < n_prompt_tiles)
    def _():
        out_refs[0][...] = val

    @pl.when(i >= n_prompt_tiles)
    def _():
        out_refs[1][...] = val


def _combine(acc_sorted, pos, x, mod, norm1_g, layer):
    tm = MOE_TM
    split = layer == DEPTH - 1
    n_prompt_tiles = N_PROMPT // tm
    row = pl.BlockSpec((tm, D_MODEL), lambda i, p: (i, 0))
    mod_row = lambda l, chunk: pl.BlockSpec((None, None, None, 1, D_MODEL),
                                            lambda i, p: (l, _mod_row(i, tm), chunk, 0, 0))
    in_specs = [pl.BlockSpec(memory_space=pl.ANY), row, mod_row(layer, 5)]
    args = [pos, acc_sorted, x, mod]
    if split:
        out_shape = (jax.ShapeDtypeStruct((N_PROMPT, D_MODEL), F32), jax.ShapeDtypeStruct((N_SAMPLE, D_MODEL), F32))
        out_specs = (pl.BlockSpec((tm, D_MODEL), lambda i, p: (jnp.minimum(i, n_prompt_tiles - 1), 0)),
                     pl.BlockSpec((tm, D_MODEL), lambda i, p: (jnp.maximum(i - n_prompt_tiles, 0), 0)))
    else:
        in_specs += [pl.BlockSpec((None, 1, D_MODEL), lambda i, p: (layer + 1, 0, 0)),
                     mod_row(layer + 1, 0), mod_row(layer + 1, 1)]
        args += [norm1_g.reshape(DEPTH, 1, D_MODEL), mod, mod]
        out_shape = (jax.ShapeDtypeStruct((N_TOK, D_MODEL), F32), jax.ShapeDtypeStruct((N_TOK, D_MODEL), BF16))
        out_specs = (row, row)
    grid_spec = pltpu.PrefetchScalarGridSpec(
        num_scalar_prefetch=1, grid=(N_TOK // tm,), in_specs=in_specs, out_specs=out_specs,
        scratch_shapes=[pltpu.VMEM((2, tm, D_MODEL), F32), pltpu.SemaphoreType.DMA((2,))])
    return pl.pallas_call(
        functools.partial(_combine_kernel, split=split), out_shape=out_shape, grid_spec=grid_spec,
        compiler_params=_params(("arbitrary",)), name="combine",
    )(*args)


def _gate_rows(gcols, n_seq, seq):
    return gcols.reshape(n_seq, seq // CHUNK, CHUNK, gcols.shape[-1]).transpose(0, 1, 3, 2)


def kernel(x_prompt, x_sample, cache_na_k, cache_na_v, state_rwkv, state_mlstm_c, state_mlstm_n, state_mlstm_m,
           c, c_ctx, norm1_g, norm2_g, w_mod, b_mod, w_in, conv_w, na_q_g, na_k_g, na_rpb, rw_w0, rw_w_up, rw_a0,
           rw_a_up, rw_g_up, rw_k_k, rw_k_a, rw_r_k, rw_ln_g, rw_ln_b, ml_gate_b, ml_norm_g, w_out, router_w,
           router_b, moe_w1, moe_w3, moe_w2):
    p = dict(rw_w0=rw_w0, rw_w_up=rw_w_up, rw_a0=rw_a0, rw_a_up=rw_a_up, rw_g_up=rw_g_up, rw_k_k=rw_k_k,
             rw_k_a=rw_k_a, rw_r_k=rw_r_k, rw_ln_g=rw_ln_g, rw_ln_b=rw_ln_b, ml_gate_b=ml_gate_b,
             ml_norm_g=ml_norm_g)
    cvecs = jnp.concatenate([c_ctx[None], c, jnp.zeros((MOD_ROWS - 1 - DEC_BATCH, D_MODEL), F32)], axis=0)
    mod = _adaln(cvecs, w_mod, b_mod).reshape(DEPTH, MOD_ROWS, 6, 1, D_MODEL)

    assert w_in.shape[-1] == P_IN
    w_in_t = jnp.swapaxes(w_in, 1, 2)
    tw = _na_tables(na_rpb)
    router_wt = router_w.T
    sample_row_off = N_PROMPT // DEC_SEQ

    x, xm1 = _premod(x_prompt.reshape(N_PROMPT, D_MODEL), x_sample.reshape(N_SAMPLE, D_MODEL), norm1_g, mod)
    kv_prev, rw_prev, ml_prev = (), (), ()
    for l in range(DEPTH):
        u = _inproj(xm1, w_in_t, l)
        ya_p, yb_p, *kv_prev = _attn_prompt(u, conv_w, na_q_g, na_k_g, l, prev=tuple(kv_prev))
        ya_s, yb_s = _na_sample(u, cache_na_k, cache_na_v, tw, conv_w, na_q_g, na_k_g, l)
        yc_p, *rw_prev = _rwkv(u, p, l, BATCH, SEQ, 0, emit_state=True, prev=tuple(rw_prev))
        (yc_s,) = _rwkv(u, p, l, DEC_BATCH, DEC_SEQ, sample_row_off, s0=state_rwkv)
        g0 = CB_GATE * MIX_W + SM_GATE
        gcols = u[:, g0:g0 + 4 * ML_HEADS]
        yd_p, *ml_prev = _mlstm(u, _gate_rows(gcols[:N_PROMPT], BATCH, SEQ), p, l, BATCH, SEQ, 0,
                                emit_state=True, prev=tuple(ml_prev))
        (yd_s,) = _mlstm(u, _gate_rows(gcols[N_PROMPT:], DEC_BATCH, DEC_SEQ), p, l, DEC_BATCH, DEC_SEQ,
                         sample_row_off, state=(state_mlstm_c, state_mlstm_n, state_mlstm_m))
        x = _outproj((ya_p, yb_p, yc_p, yd_p), (ya_s, yb_s, yc_s, yd_s), w_out, x, mod, l)
        xm, gates_t, ids = _router(x, norm2_g, mod, router_wt, router_b, l)
        plan = _moe_plan(ids)
        acc_sorted = _moe(xm, gates_t.T[plan[0]], plan, moe_w1, moe_w3, moe_w2, l)
        combined = _combine(acc_sorted, plan[1], x, mod, norm1_g, l)
        if l < DEPTH - 1:
            x, xm1 = combined
    y_prompt, y_sample = combined
    new_c, new_n, new_m = ml_prev
    return (y_prompt.reshape(BATCH, SEQ, D_MODEL), y_sample.reshape(DEC_BATCH, DEC_SEQ, D_MODEL),
            kv_prev[0], kv_prev[1], rw_prev[0], new_c, new_n.reshape(BATCH, DEPTH, 2, ML_HEADS, ML_DH),
            new_m.reshape(BATCH, DEPTH, 2, ML_HEADS))
```

```python
import functools

import numpy as np
import jax
import jax.numpy as jnp
from jax import lax
from jax.experimental import pallas as pl
from jax.experimental.pallas import tpu as pltpu

F32 = jnp.float32
BF16 = jnp.bfloat16
HI = lax.Precision.HIGHEST

D_MODEL = 2048
BATCH = 16
SEQ = 256
DEPTH = 2
DEC_BATCH = 2
DEC_SEQ = 1024
PAST_LEN = 512
GRID_W = 64
MIX_W = D_MODEL // 4
CONV_K = 3
NA_DH = 64
NA_HEADS = MIX_W // NA_DH
NA_WIN_R = 8
NA_WIN_C = 16
NA_SCALE = NA_DH ** -0.5
ROPE_THETA = 10000.0
RW_DH = 64
RW_HEADS = MIX_W // RW_DH
RW_W_RANK = 64
RW_A_RANK = 64
RW_G_RANK = 128
RW_DECAY_SCALE = 0.606531
RW_GN_EPS = 64e-5
ML_DH = 128
ML_HEADS = MIX_W // ML_DH
N_EXPERTS = 16
N_EXPERT_GROUPS = 4
D_EXPERT = 512
EPS = 1e-6
NEG_INF = -1e30

N_PROMPT = BATCH * SEQ
N_SAMPLE = DEC_BATCH * DEC_SEQ
N_TOK = N_PROMPT + N_SAMPLE
MOD_ROWS = 8
CHUNK = 64
SUB = 16
P1_CHUNKS = 4
NA_ROWS_PER_TRIP = 4
P_IN = 13 * MIX_W + RW_W_RANK + RW_A_RANK + RW_G_RANK + 4 * ML_HEADS
P_BLOCKS = 15
P_PAD = P_BLOCKS * MIX_W
(CB_CVB, CB_CVC, CB_CVH, CB_NAQ, CB_NAK, CB_NAV, CB_RWR, CB_RWK, CB_RWV,
 CB_MLQ, CB_MLK, CB_MLV, CB_MLO, CB_SMALL, CB_GATE) = range(P_BLOCKS)
SM_WL, SM_AL, SM_GL = 0, 64, 128
SM_GATE = MIX_W - 4 * ML_HEADS
VMEM_LIMIT = 56 * 1024 * 1024


def _mm(a, b, prec=None):
    return jnp.dot(a, b, precision=prec, preferred_element_type=F32)


def _mm_nt(a, b, prec=None):
    return lax.dot_general(a, b, (((1,), (1,)), ((), ())), precision=prec, preferred_element_type=F32)


def _sigmoid(x):
    return 1.0 / (1.0 + jnp.exp(-x))


def _full(shape):
    n = len(shape)
    return pl.BlockSpec(shape, lambda *_: (0,) * n)


def _params(sem):
    return pltpu.CompilerParams(dimension_semantics=sem, vmem_limit_bytes=VMEM_LIMIT)


def _mod_row(i, tm):
    n_prompt_tiles = N_PROMPT // tm
    tiles_per_sample = DEC_SEQ // tm
    return jnp.where(i < n_prompt_tiles, 0, 1 + (i - n_prompt_tiles) // tiles_per_sample)


def _mod_spec(layer, chunk, tm, tn=D_MODEL, with_j=False):
    if with_j:
        return pl.BlockSpec((None, None, None, 1, tn), lambda i, j: (layer, _mod_row(i, tm), chunk, 0, j))
    return pl.BlockSpec((None, None, None, 1, tn), lambda i, *_: (layer, _mod_row(i, tm), chunk, 0, 0))


def _tri_masks(n):
    t = lax.broadcasted_iota(jnp.int32, (n, n), 0)
    s = lax.broadcasted_iota(jnp.int32, (n, n), 1)
    incl = ((s <= t).astype(F32), (s >= t).astype(F32))
    strict = ((s < t).astype(F32), (s > t).astype(F32))
    return incl, strict, t, s


def _split2(x):
    hi = x.astype(BF16)
    return hi, (x - hi.astype(F32)).astype(BF16)


def _mm1(a, b):
    return _mm(a.astype(BF16), b.astype(BF16))


def _mm3(a, b):
    a_hi, a_lo = _split2(a)
    b_hi, b_lo = _split2(b)
    return _mm(a_hi, b_lo) + _mm(a_lo, b_hi) + _mm(a_hi, b_hi)


def _split3(x):
    x0 = x.astype(BF16)
    r1 = x - x0.astype(F32)
    x1 = r1.astype(BF16)
    return x0, x1, (r1 - x1.astype(F32)).astype(BF16)


def _mm_exact_lhs(mask_b, x):
    x0, x1, x2 = _split3(x)
    return _mm(mask_b, x2) + _mm(mask_b, x1) + _mm(mask_b, x0)


def _mm_exact_rhs(x, mask_b):
    x0, x1, x2 = _split3(x)
    return _mm(x2, mask_b) + _mm(x1, mask_b) + _mm(x0, mask_b)


def _adaln_kernel(cv_ref, w_ref, b_ref, o_ref):
    cv = cv_ref[...]
    a0, a1, a2 = _split3(cv * _sigmoid(cv))
    w_hi, w_lo = _split2(w_ref[...])
    o_ref[...] = ((_mm(a2, w_hi) + _mm(a1, w_lo)) + (_mm(a1, w_hi) + _mm(a0, w_lo)) + _mm(a0, w_hi)) + b_ref[...]


def _adaln(cvecs, w_mod, b_mod):
    tn = 1024
    n_out = 6 * D_MODEL
    return pl.pallas_call(
        _adaln_kernel,
        out_shape=jax.ShapeDtypeStruct((DEPTH, MOD_ROWS, n_out), F32),
        grid=(DEPTH, n_out // tn),
        in_specs=[_full((MOD_ROWS, D_MODEL)),
                  pl.BlockSpec((None, D_MODEL, tn), lambda l, j: (l, 0, j)),
                  pl.BlockSpec((None, 1, tn), lambda l, j: (l, 0, j))],
        out_specs=pl.BlockSpec((None, MOD_ROWS, tn), lambda l, j: (l, 0, j)),
        compiler_params=_params(("parallel", "parallel")),
        name="adaln",
    )(cvecs, w_mod, b_mod.reshape(DEPTH, 1, n_out))


def _modulated(x, g, sh, sc):
    y = x * lax.rsqrt(jnp.mean(x * x, axis=-1, keepdims=True) + EPS) * g
    return y * (1.0 + sc) + sh


def _premod_kernel(xp_ref, xs_ref, g_ref, sh_ref, sc_ref, x_ref, xm_ref, *, n_prompt_tiles):
    x = jnp.where(pl.program_id(0) < n_prompt_tiles, xp_ref[...], xs_ref[...])
    x_ref[...] = x
    xm_ref[...] = _modulated(x, g_ref[...], sh_ref[...], sc_ref[...]).astype(BF16)


def _premod(x_prompt, x_sample, norm_g, mod):
    tm = 512
    n_prompt_tiles = N_PROMPT // tm
    row = pl.BlockSpec((tm, D_MODEL), lambda i: (i, 0))
    return pl.pallas_call(
        functools.partial(_premod_kernel, n_prompt_tiles=n_prompt_tiles),
        out_shape=(jax.ShapeDtypeStruct((N_TOK, D_MODEL), F32), jax.ShapeDtypeStruct((N_TOK, D_MODEL), BF16)),
        grid=(N_TOK // tm,),
        in_specs=[pl.BlockSpec((tm, D_MODEL), lambda i: (jnp.minimum(i, n_prompt_tiles - 1), 0)),
                  pl.BlockSpec((tm, D_MODEL), lambda i: (jnp.maximum(i - n_prompt_tiles, 0), 0)),
                  pl.BlockSpec((None, 1, D_MODEL), lambda i: (0, 0, 0)),
                  _mod_spec(0, 0, tm), _mod_spec(0, 1, tm)],
        out_specs=(row, row),
        compiler_params=_params(("parallel",)),
        name="premod",
    )(x_prompt, x_sample, norm_g.reshape(DEPTH, 1, D_MODEL), mod, mod)


def _inproj_kernel(xm_ref, w_ref, o_ref):
    o_ref[...] = _mm_nt(xm_ref[...], w_ref[0].astype(BF16))


def _inproj_src_row(j):
    a = 9
    narrow = RW_W_RANK + RW_A_RANK + RW_G_RANK
    g = 16
    return g * jnp.where(j < a, j * (MIX_W // g),
                         jnp.where(j < CB_SMALL, j * (MIX_W // g) + narrow // g,
                                   jnp.where(j == CB_SMALL, a * MIX_W // g, (P_IN - MIX_W) // g)))


def _inproj(xm, w_in_t, layer):
    tm, tn = 3072, MIX_W
    return pl.pallas_call(
        _inproj_kernel,
        out_shape=jax.ShapeDtypeStruct((N_TOK, P_PAD), F32),
        grid=(N_TOK // tm, P_BLOCKS),
        in_specs=[pl.BlockSpec((tm, D_MODEL), lambda i, j: (i, 0)),
                  pl.BlockSpec((pl.Element(1), pl.Element(tn), pl.Element(D_MODEL)),
                               lambda i, j: (layer, _inproj_src_row(j), 0))],
        out_specs=pl.BlockSpec((tm, tn), lambda i, j: (i, j)),
        compiler_params=_params(("parallel", "parallel")),
        name="inproj",
    )(xm, w_in_t)


def _conv_mix(b, c, h, w):
    u = c * h
    n = u.shape[0]
    row = lax.broadcasted_iota(jnp.int32, u.shape, 0)
    prev = jnp.where(row == 0, 0.0, pltpu.roll(u, 1, axis=0))
    nxt = jnp.where(row == n - 1, 0.0, pltpu.roll(u, n - 1, axis=0))
    return b * (prev * w[0:1] + u * w[1:2] + nxt * w[2:3])


def _head_rms(x, g):
    return x * lax.rsqrt(jnp.mean(x * x, axis=-1, keepdims=True) + EPS) * g


def _attn_prompt_kernel(cb_ref, cc_ref, ch_ref, q_ref, k_ref, v_ref, cw_ref, qg_ref, kg_ref,
                        ya_ref, yb_ref, nk_ref, nv_ref):
    ya_ref[...] = _conv_mix(cb_ref[...], cc_ref[...], ch_ref[...], cw_ref[...]).astype(ya_ref.dtype)
    sls = [slice(h * NA_DH, (h + 1) * NA_DH) for h in range(NA_HEADS)]
    qn = [_head_rms(q_ref[:, sl], qg_ref[...]) * NA_SCALE for sl in sls]
    kn = [_head_rms(k_ref[:, sl], kg_ref[...]) for sl in sls]
    vh = [v_ref[:, sl] for sl in sls]
    s = [_mm_nt(q.astype(BF16), k.astype(BF16)) for q, k in zip(qn, kn)]
    p = [jnp.exp(x - jnp.max(x, axis=-1, keepdims=True)) for x in s]
    o = [_mm(x.astype(BF16), v.astype(BF16)) / jnp.sum(x, axis=-1, keepdims=True) for x, v in zip(p, vh)]
    yb_ref[...] = jnp.concatenate(o, axis=1).astype(yb_ref.dtype)
    for h in range(NA_HEADS):
        nk_ref[h] = kn[h]
        nv_ref[h] = vh[h]


def _u_spec(rows, col_block, row_off_blocks=0):
    return pl.BlockSpec((rows, MIX_W), lambda b: (b + row_off_blocks, col_block))


def _carry_through(kernel, n_inputs, prev, n_plain_out, n_state_out, layer):
    n_prev = len(prev)

    def body(*refs):
        ins, rest = refs[:n_inputs], list(refs[n_inputs + n_prev:])
        if not n_prev:
            for i in range(n_plain_out, n_plain_out + n_state_out):
                full = rest[i]
                for d in range(DEPTH):
                    if d != layer:
                        full[d] = jnp.zeros(full.shape[1:], full.dtype)
                rest[i] = full.at[layer]
        return kernel(*ins, *rest)

    return body, [pl.BlockSpec(memory_space=pl.ANY)] * n_prev


def _state_spec(shape, layer, first, index):
    def index_map(*g):
        b, *tail = index(*g)
        return (b, 0 if first else layer, *tail)
    return pl.BlockSpec((None, DEPTH if first else None) + tuple(shape), index_map)


def _attn_prompt(u, conv_w, q_g, k_g, layer, prev=(), n_seq=BATCH, seq=SEQ):
    lw = lambda shape: pl.BlockSpec((None,) + shape, lambda b: (layer,) + (0,) * len(shape))
    y_spec = pl.BlockSpec((seq, MIX_W), lambda b: (b, 0))
    kv_spec = _state_spec((NA_HEADS, seq, NA_DH), layer, not prev, lambda b: (b, 0, 0, 0))
    in_specs = ([_u_spec(seq, cb) for cb in (CB_CVB, CB_CVC, CB_CVH, CB_NAQ, CB_NAK, CB_NAV)]
                + [lw((CONV_K, MIX_W)), lw((1, NA_DH)), lw((1, NA_DH))])
    body, prev_specs = _carry_through(_attn_prompt_kernel, len(in_specs), prev, 2, 2, layer)
    return pl.pallas_call(
        body,
        out_shape=(jax.ShapeDtypeStruct((n_seq * seq, MIX_W), BF16),) * 2
        + (jax.ShapeDtypeStruct((n_seq, DEPTH, NA_HEADS, seq, NA_DH), F32),) * 2,
        grid=(n_seq,),
        in_specs=in_specs + prev_specs,
        out_specs=(y_spec, y_spec, kv_spec, kv_spec),
        input_output_aliases={len(in_specs) + i: 2 + i for i in range(len(prev))},
        compiler_params=_params(("parallel",)),
        name="attn_prompt",
    )(u, u, u, u, u, u, conv_w, q_g.reshape(DEPTH, 1, NA_DH), k_g.reshape(DEPTH, 1, NA_DH), *prev)


def _na_kernel(cb_ref, cc_ref, ch_ref, q_ref, k_ref, v_ref, kc_ref, vc_ref, tw_ref, cos_ref, sin_ref,
               perm_ref, cw_ref, qg_ref, kg_ref, ya_ref, yb_ref, qs_ref, ks_ref, tws_ref):
    rows = DEC_SEQ // GRID_W
    wr = min(NA_WIN_R, rows)
    nw = wr * GRID_W
    ya_ref[...] = _conv_mix(cb_ref[...], cc_ref[...], ch_ref[...], cw_ref[...]).astype(ya_ref.dtype)
    cos, sin, perm_b = cos_ref[...], sin_ref[...], perm_ref[...].astype(BF16)

    def rope(x):
        hi, lo = _split2(x)
        return x * cos + (_mm(hi, perm_b) + _mm(lo, perm_b)) * sin

    for h in range(NA_HEADS):
        sl = slice(h * NA_DH, (h + 1) * NA_DH)
        qs_ref[...] = (rope(_head_rms(q_ref[:, sl], qg_ref[...])) * NA_SCALE).astype(BF16)
        ks_ref[...] = rope(_head_rms(k_ref[:, sl], kg_ref[...])).astype(BF16)
        kch = kc_ref[h].astype(BF16)
        vch = vc_ref[h].astype(BF16)
        for p in range(wr):
            tws_ref[p] = jnp.concatenate([tw_ref[h, j - p + NA_WIN_R - 1] for j in range(wr)], axis=1)

        def rows_step(t, carry):
            rr = [t * NA_ROWS_PER_TRIP + i for i in range(NA_ROWS_PER_TRIP)]
            rs = [jnp.clip(r - wr // 2, 0, rows - wr) for r in rr]
            q0 = [pl.multiple_of(r * GRID_W, GRID_W) for r in rr]
            k0 = [pl.multiple_of(x * GRID_W, GRID_W) for x in rs]
            q_r = [qs_ref[pl.ds(x, GRID_W), :] for x in q0]
            s_w = [_mm_nt(q, ks_ref[pl.ds(k, nw), :]) + tws_ref[r - x] for q, k, r, x in zip(q_r, k0, rr, rs)]
            s_c = [_mm_nt(q, kch) for q in q_r]
            m = [jnp.maximum(jnp.max(a, axis=-1, keepdims=True), jnp.max(b, axis=-1, keepdims=True))
                 for a, b in zip(s_w, s_c)]
            p_w = [jnp.exp(a - x) for a, x in zip(s_w, m)]
            p_c = [jnp.exp(b - x) for b, x in zip(s_c, m)]
            den = [jnp.sum(a, axis=-1, keepdims=True) + jnp.sum(b, axis=-1, keepdims=True) for a, b in zip(p_w, p_c)]
            v_w = [v_ref[pl.ds(k, nw), sl].astype(BF16) for k in k0]
            o = [(_mm(a.astype(BF16), v) + _mm(b.astype(BF16), vch)) / d for a, b, v, d in zip(p_w, p_c, v_w, den)]
            for x, val in zip(q0, o):
                yb_ref[pl.ds(x, GRID_W), sl] = val.astype(yb_ref.dtype)
            return carry

        lax.fori_loop(0, rows // NA_ROWS_PER_TRIP, rows_step, 0)


def _na_tables(rpb):
    qc = np.arange(GRID_W)
    kc = np.arange(GRID_W)
    wstart = np.clip(qc - NA_WIN_C // 2, 0, GRID_W - NA_WIN_C)
    colmask = (kc[None, :] >= wstart[:, None]) & (kc[None, :] < wstart[:, None] + NA_WIN_C)
    dc = np.clip(kc[None, :] - qc[:, None], -(NA_WIN_C - 1), NA_WIN_C - 1) + NA_WIN_C - 1
    pick = (np.arange(2 * NA_WIN_C - 1)[:, None] == dc.reshape(1, -1)).astype(np.float32)
    bias = jnp.einsum('lhrd,dn->lhrn', rpb, jnp.asarray(pick), precision=HI)
    return jnp.where(colmask[None, None, None], bias.reshape(rpb.shape[:3] + dc.shape), NEG_INF)


def _rope_tables():
    t = np.arange(DEC_SEQ)
    quarter = NA_DH // 4
    freq = ROPE_THETA ** (-np.arange(quarter, dtype=np.float32) / quarter)
    ang_r = (t // GRID_W).astype(np.float32)[:, None] * freq
    ang_c = (t % GRID_W).astype(np.float32)[:, None] * freq
    cos = np.concatenate([np.cos(ang_r), np.cos(ang_r), np.cos(ang_c), np.cos(ang_c)], axis=-1)
    sin = np.concatenate([-np.sin(ang_r), np.sin(ang_r), -np.sin(ang_c), np.sin(ang_c)], axis=-1)
    src = np.concatenate([np.arange(quarter) + quarter, np.arange(quarter),
                          np.arange(quarter) + 3 * quarter, np.arange(quarter) + 2 * quarter])
    perm = np.zeros((NA_DH, NA_DH), np.float32)
    perm[src, np.arange(NA_DH)] = 1.0
    return cos.astype(np.float32), sin.astype(np.float32), perm


def _na_sample(u, cache_k, cache_v, tw, conv_w, q_g, k_g, layer, n_seq=DEC_BATCH, row_off=N_PROMPT // DEC_SEQ):
    cos, sin, perm = _rope_tables()
    lw = lambda shape: pl.BlockSpec((None,) + shape, lambda b: (layer,) + (0,) * len(shape))
    y_spec = pl.BlockSpec((DEC_SEQ, MIX_W), lambda b: (b, 0))
    c_spec = pl.BlockSpec((None, None, NA_HEADS, PAST_LEN, NA_DH), lambda b: (b, layer, 0, 0, 0))
    wr = min(NA_WIN_R, DEC_SEQ // GRID_W)
    n_off = 2 * NA_WIN_R - 1
    return pl.pallas_call(
        _na_kernel,
        out_shape=(jax.ShapeDtypeStruct((n_seq * DEC_SEQ, MIX_W), BF16),) * 2,
        grid=(n_seq,),
        in_specs=[_u_spec(DEC_SEQ, cb, row_off) for cb in (CB_CVB, CB_CVC, CB_CVH, CB_NAQ, CB_NAK, CB_NAV)]
        + [c_spec, c_spec, lw((NA_HEADS, n_off, GRID_W, GRID_W)),
           _full((DEC_SEQ, NA_DH)), _full((DEC_SEQ, NA_DH)), _full((NA_DH, NA_DH)),
           lw((CONV_K, MIX_W)), lw((1, NA_DH)), lw((1, NA_DH))],
        out_specs=(y_spec, y_spec),
        scratch_shapes=[pltpu.VMEM((DEC_SEQ, NA_DH), BF16), pltpu.VMEM((DEC_SEQ, NA_DH), BF16),
                        pltpu.VMEM((wr, GRID_W, wr * GRID_W), F32)],
        compiler_params=_params(("parallel",)),
        name="na_sample",
    )(u, u, u, u, u, u, cache_k, cache_v, tw, jnp.asarray(cos), jnp.asarray(sin), jnp.asarray(perm),
      conv_w, q_g.reshape(DEPTH, 1, NA_DH), k_g.reshape(DEPTH, 1, NA_DH))


def _seg_ones(width, seg):
    a = lax.broadcasted_iota(jnp.int32, (width, width), 0) // seg
    b = lax.broadcasted_iota(jnp.int32, (width, width), 1) // seg
    return (a == b).astype(F32)


def _rwkv_kernel(*refs, seq, has_s0, emit_state):
    it = iter(refs)
    r_ref, k_ref, v_ref, sm_ref = (next(it) for _ in range(4))
    (w0_ref, wup_ref, a0_ref, aup_ref, gup_ref, kkp_ref, kap_ref, rkp_ref, lng_ref, lnb_ref) = (
        next(it) for _ in range(10))
    mask_ref = next(it)
    s0_ref = next(it) if has_s0 else None
    y_ref = next(it)
    so_ref = next(it) if emit_state else None
    kk_ref, lw_ref, ka_ref, kd_ref, coef_ref, ysp_ref = (next(it) for _ in range(6))

    dh = RW_DH
    pw = 2 * dh
    nc = seq // CHUNK
    seg_b = _seg_ones(pw, dh).astype(BF16)

    r = r_ref[...]
    k = k_ref[...]
    sm = sm_ref[...]
    wl = jnp.tanh(sm[:, SM_WL:SM_WL + RW_W_RANK])
    al = sm[:, SM_AL:SM_AL + RW_A_RANK]
    kk = k * kkp_ref[...]
    kk = kk * lax.rsqrt(_mm_exact_rhs(kk * kk, seg_b) + EPS)
    kk_ref[...] = kk
    for z in range(2):
        lw_ref[z] = -RW_DECAY_SCALE * _sigmoid(w0_ref[z:z + 1, :] + _mm3(wl, wup_ref[z]))
        a = _sigmoid(a0_ref[z:z + 1, :] + _mm3(al, aup_ref[z]))
        ka_ref[z] = kk * a
        kd_ref[z] = k * (1.0 + (a - 1.0) * kap_ref[...])

    cat = jnp.concatenate

    def phase1(c2, carry):
        incl, strict, ti, si = _tri_masks(CHUNK)
        incl_b = (mask_ref[0], mask_ref[1])
        strict2 = tuple(cat([m, m], axis=1) for m in strict)
        incl2 = tuple(cat([m, m], axis=1) for m in incl)
        diag_blk = (ti // SUB == si // SUB).astype(F32)
        eye = (ti == si).astype(F32)
        zero_pair = jnp.zeros((CHUNK, pw), F32)
        head_of_lane = lax.broadcasted_iota(jnp.int32, (CHUNK, pw), 1) // dh
        head_of_lane2 = (lax.broadcasted_iota(jnp.int32, (CHUNK, 2 * pw), 1) // dh) % 2
        r_pw = lax.broadcasted_iota(jnp.int32, (pw, pw), 0)
        c_pw = lax.broadcasted_iota(jnp.int32, (pw, pw), 1)
        same_head = r_pw // dh == c_pw // dh
        eye_pw = (r_pw == c_pw).astype(F32)
        chains = []
        for cc in range(P1_CHUNKS):
            rows = pl.ds(pl.multiple_of((c2 * P1_CHUNKS + cc) * CHUNK, CHUNK), CHUNK)
            vc, rc, kkc = v_ref[rows, :], r_ref[rows, :], kk_ref[rows, :]
            for z in range(2):
                lwc = lw_ref[z, rows, :]
                cum = _mm_exact_lhs(incl_b[z], lwc)
                tot = cum[CHUNK - 1:CHUNK] if z == 0 else cum[0:1]
                e_neg = jnp.exp(-cum)
                dec = jnp.exp(tot - cum)
                kac, kdc = ka_ref[z, rows, :], kd_ref[z, rows, :]
                chains.append((z, kkc * jnp.exp(cum - lwc), rc * jnp.exp(cum), vc, kac * e_neg, kdc * e_neg,
                               kac * dec, kdc * dec, jnp.exp(tot)))
        heads = [(q, j) for q in range(len(chains)) for j in range(2)]
        zq = [ch[0] for ch in chains]
        ymat = [cat([ch[4], ch[5]], axis=0).astype(BF16) for ch in chains]
        vz = [cat([zero_pair, ch[3]], axis=0) for ch in chains]
        nvz = [cat([zero_pair, -ch[3]], axis=1) for ch in chains]
        x_in = [cat([jnp.where(head_of_lane == j, chains[q][1], 0.0),
                     jnp.where(head_of_lane == j, chains[q][2], 0.0)], axis=0) for q, j in heads]
        aa = [_mm_nt(x.astype(BF16), ymat[q]) for x, (q, _) in zip(x_in, heads)]
        top = [a[0:CHUNK] * strict2[zq[q]] for a, (q, _) in zip(aa, heads)]
        a_r = [a[CHUNK:] * incl2[zq[q]] for a, (q, _) in zip(aa, heads)]
        akv = [_mm1(t, vz[q]) for t, (q, _) in zip(top, heads)]
        low = [t[:, 0:CHUNK] for t in top]
        ld = [x * diag_blk for x in low]
        lo = [x - y for x, y in zip(low, ld)]
        l2 = [_mm1(x, x) for x in ld]
        l4 = [_mm1(x, x) for x in l2]
        l8 = [_mm1(x, x) for x in l4]
        td = [eye - x for x in ld]
        for lp in (l2, l4, l8):
            td = [t + _mm1(t, p_) for t, p_ in zip(td, lp)]
        x0 = [_mm1(t, cat([chains[q][1], a, l], axis=1)) for t, a, l, (q, _) in zip(td, akv, lo, heads)]
        pq0 = [x[:, 0:2 * pw] for x in x0]
        wm = [x[:, 2 * pw:] for x in x0]
        pq = pq0
        for _ in range(CHUNK // SUB - 1):
            pq = [p0 - _mm1(w, p_) for p0, w, p_ in zip(pq0, wm, pq)]
        ryc = [_mm1(a, cat([p_, nvz[q]], axis=0)) for a, p_, (q, _) in zip(a_r, pq, heads)]
        for q, ch in enumerate(chains):
            pq_m = jnp.where(head_of_lane2 == 0, pq[2 * q], pq[2 * q + 1])
            ryc_m = jnp.where(head_of_lane2 == 0, ryc[2 * q], ryc[2 * q + 1])
            g1 = _mm1(ch[6].T, pq_m)
            g2 = _mm1(ch[7].T, ch[3])
            g_t = eye_pw * ch[8] - jnp.where(same_head, g1[:, 0:pw], 0.0)
            h_t = jnp.where(same_head, g2 - g1[:, pw:], 0.0)
            coef_ref[c2 * P1_CHUNKS + q // 2, q % 2] = cat(
                [g_t, ch[2] - ryc_m[:, 0:pw], h_t, -ryc_m[:, pw:]], axis=0)
        return carry

    lax.fori_loop(0, nc // P1_CHUNKS, phase1, 0)

    def block_diag(a, b):
        zero = jnp.zeros((dh, dh), F32)
        return cat([cat([a, zero], axis=1), cat([zero, b], axis=1)], axis=0)

    m_init = tuple((block_diag(s0_ref[z, 0].T, s0_ref[z, 1].T) if has_s0 else jnp.zeros((pw, pw), F32))
                   for z in range(2))
    n_lhs = pw + CHUNK

    def phase2(ci, ms):
        new_ms, ys = [], []
        for z in range(2):
            c = ci if z == 0 else nc - 1 - ci
            out = _mm3(coef_ref[c, z, 0:n_lhs, :], ms[z]) + coef_ref[c, z, n_lhs:, :]
            new_ms.append(out[0:pw])
            ys.append(out[pw:])
        ysp_ref[ci] = cat(ys, axis=1)
        return tuple(new_ms)

    m_fin = lax.fori_loop(0, nc, phase2, m_init)

    if emit_state:
        for z in range(2):
            for j in range(2):
                so_ref[z, j] = m_fin[z][j * dh:(j + 1) * dh, j * dh:(j + 1) * dh].T

    y = cat([ysp_ref[c, :, 0:pw] + ysp_ref[nc - 1 - c, :, pw:] for c in range(nc)], axis=0)
    mu = _mm_exact_rhs(y, seg_b) * (1.0 / dh)
    yc = y - mu
    var = _mm_exact_rhs(yc * yc, seg_b) * (1.0 / dh)
    yn = yc * lax.rsqrt(var + RW_GN_EPS) * lng_ref[...] + lnb_ref[...]
    bonus = _mm_exact_rhs(r * k * rkp_ref[...], seg_b) * v_ref[...]
    g = _mm3(_sigmoid(sm[:, SM_GL:SM_GL + RW_G_RANK]), gup_ref[...])
    y_ref[...] = ((yn + bonus) * g).astype(y_ref.dtype)


def _rwkv(u, p, layer, n_seq, seq, row_off, s0=None, emit_state=False, prev=()):
    n_pairs = RW_HEADS // 2
    pw = 2 * RW_DH
    bpc = MIX_W // pw
    lw = lambda shape: pl.BlockSpec((None,) + shape, lambda b, hp: (layer,) + (0,) * (len(shape) - 1) + (hp,))
    row = lambda a: a.reshape(DEPTH, 1, MIX_W)
    u_pair = lambda cb: pl.BlockSpec((seq, pw), lambda b, hp: (b + row_off, cb * bpc + hp))
    in_specs = [u_pair(CB_RWR), u_pair(CB_RWK), u_pair(CB_RWV),
                pl.BlockSpec((seq, MIX_W), lambda b, hp: (b + row_off, CB_SMALL)),
                lw((2, pw)), lw((2, RW_W_RANK, pw)), lw((2, pw)), lw((2, RW_A_RANK, pw)),
                lw((RW_G_RANK, pw))] + [lw((1, pw))] * 5 + [_full((2, CHUNK, CHUNK))]
    t_idx = np.arange(CHUNK)
    incl_masks = np.stack([t_idx[None, :] <= t_idx[:, None], t_idx[None, :] >= t_idx[:, None]])
    args = [u, u, u, u, p['rw_w0'], p['rw_w_up'], p['rw_a0'], p['rw_a_up'], p['rw_g_up'],
            row(p['rw_k_k']), row(p['rw_k_a']), row(p['rw_r_k']), row(p['rw_ln_g']), row(p['rw_ln_b']),
            jnp.asarray(incl_masks, BF16)]
    st_blk = (2, 2, RW_DH, RW_DH)
    if s0 is not None:
        in_specs.append(pl.BlockSpec((None, None) + st_blk, lambda b, hp: (b, layer, 0, hp, 0, 0)))
        args.append(s0)
    out_shape = [jax.ShapeDtypeStruct((n_seq * seq, MIX_W), BF16)]
    out_specs = [pl.BlockSpec((seq, pw), lambda b, hp: (b, hp))]
    if emit_state:
        out_shape.append(jax.ShapeDtypeStruct((n_seq, DEPTH, 2, RW_HEADS, RW_DH, RW_DH), F32))
        out_specs.append(_state_spec(st_blk, layer, not prev, lambda b, hp: (b, 0, hp, 0, 0)))
    nc = seq // CHUNK
    tok = lambda n: pltpu.VMEM((n, seq, pw) if n else (seq, pw), F32)
    body, prev_specs = _carry_through(
        functools.partial(_rwkv_kernel, seq=seq, has_s0=s0 is not None, emit_state=emit_state), len(in_specs), prev,
        1, 1 if emit_state else 0, layer)
    return pl.pallas_call(
        body,
        out_shape=tuple(out_shape), grid=(n_seq, n_pairs), in_specs=in_specs + prev_specs,
        out_specs=tuple(out_specs),
        input_output_aliases={len(in_specs) + i: 1 + i for i in range(len(prev))},
        scratch_shapes=[tok(0), tok(2), tok(2), tok(2),
                        pltpu.VMEM((nc, 2, 2 * (pw + CHUNK), pw), F32), pltpu.VMEM((nc, CHUNK, 2 * pw), F32)],
        compiler_params=_params(("parallel", "parallel")),
        name=f"rwkv_{seq}",
    )(*args, *prev)


def _log_sigmoid(x):
    return jnp.minimum(x, 0.0) - jnp.log(1.0 + jnp.exp(-jnp.abs(x)))


def _mlstm_kernel(*refs, seq, has_s0, emit_state):
    it = iter(refs)
    q_ref, k_ref, v_ref, o_ref, sm_ref, gr_ref, bc_ref, br_ref, ng_ref = (next(it) for _ in range(9))
    c0_ref, n0_ref, m0_ref = ((next(it), next(it), next(it)) if has_s0 else (None, None, None))
    y_ref = next(it)
    co_ref, no_ref, mo_ref = ((next(it), next(it), next(it)) if emit_state else (None, None, None))
    hs_ref, c_ref, n_ref, m_ref = (next(it) for _ in range(4))

    nc = seq // CHUNK
    n_st = 2 * ML_HEADS
    for i in range(n_st):
        z, h = divmod(i, ML_HEADS)
        c_ref[i] = c0_ref[z, h] if has_s0 else jnp.zeros((ML_DH, ML_DH), F32)
        n_ref[i] = n0_ref[z, h] if has_s0 else jnp.zeros((1, ML_DH), F32)
        m_ref[i] = m0_ref[z, h] if has_s0 else jnp.zeros((1, 1), F32)

    incl, _, ti, si = _tri_masks(CHUNK)
    before = ((si <= ti), (si >= ti))

    nh = ML_HEADS
    chains = [(z, h) for z in range(2) for h in range(nh)]
    each = lambda f, *xs: [f(*a) for a in zip(*xs)]
    zs = [z for z, _ in chains]

    def chunk_step(ci, carry):
        per_dir = []
        for z in range(2):
            c = ci if z == 0 else nc - 1 - ci
            rows = pl.ds(pl.multiple_of(c * CHUNK, CHUNK), CHUNK)
            g0 = SM_GATE + z * 2 * nh
            gc = sm_ref[rows, g0:g0 + 2 * nh] + bc_ref[:, z * 2 * nh:(z + 1) * 2 * nh]
            gr = gr_ref[c, z * 2 * nh:(z + 1) * 2 * nh, :] + br_ref[z * 2 * nh:(z + 1) * 2 * nh, :]
            b_cols = _mm(incl[z], _log_sigmoid(gc[:, nh:]), HI)
            b_rows = _mm(_log_sigmoid(gr[nh:]), incl[1 - z], HI)
            per_dir.append((rows, gc[:, :nh], b_cols, gr[:nh], b_rows))
        rows = [per_dir[z][0] for z, _ in chains]
        hsl = [slice(h * ML_DH, (h + 1) * ML_DH) for _, h in chains]
        i_col = [per_dir[z][1][:, h:h + 1] for z, h in chains]
        b_col = [per_dir[z][2][:, h:h + 1] for z, h in chains]
        i_row = [per_dir[z][3][h:h + 1] for z, h in chains]
        b_row = [per_dir[z][4][h:h + 1] for z, h in chains]
        b_last = each(lambda b, z: b[CHUNK - 1:CHUNK] if z == 0 else b[0:1], b_col, zs)
        m_old = [m_ref[i] for i in range(n_st)]
        cm = [c_ref[i] for i in range(n_st)]
        nv = [n_ref[i] for i in range(n_st)]
        qc = each(lambda r, s: q_ref[r, s] * (ML_DH ** -0.5), rows, hsl)
        kc = each(lambda r, s: k_ref[r, s], rows, hsl)
        vc = each(lambda r, s: v_ref[r, s], rows, hsl)
        qk = each(lambda q, k: _mm_nt(q.astype(BF16), k.astype(BF16)), qc, kc)
        qcm = each(lambda q, c_: _mm_nt(q.astype(BF16), c_.astype(BF16)), qc, cm)
        a_t = each(lambda b, m: b + m, b_col, m_old)
        dmat = each(lambda bc, brw, ir, z: jnp.where(before[z], bc - brw + ir, NEG_INF), b_col, b_row, i_row, zs)
        m_t = each(lambda a, d: jnp.maximum(a, jnp.max(d, axis=-1, keepdims=True)), a_t, dmat)
        s = each(lambda x, d, m: x * jnp.exp(d - m), qk, dmat, m_t)
        inter = each(lambda a, m: jnp.exp(a - m), a_t, m_t)
        sv = each(_mm1, s, vc)
        g_col = each(lambda bl, bc, ic: bl - bc + ic, b_last, b_col, i_col)
        a_l = each(lambda bl, m: bl + m, b_last, m_old)
        m_new = each(lambda a, g: jnp.maximum(a, jnp.max(g, axis=0, keepdims=True)), a_l, g_col)
        wgt = each(lambda g, m: jnp.exp(g - m), g_col, m_new)
        vk = each(lambda v, w, k: _mm1((v * w).T, k), vc, wgt, kc)
        decay = each(lambda a, m: jnp.exp(a - m), a_l, m_new)
        num = each(lambda i_, x, y: i_ * x + y, inter, qcm, sv)
        den = each(lambda i_, q, n_, s_: i_ * jnp.sum(q * n_, axis=-1, keepdims=True)
                   + jnp.sum(s_, axis=-1, keepdims=True), inter, qc, nv, s)
        hh = each(lambda n_, d, m: n_ / jnp.maximum(jnp.abs(d), jnp.exp(-m)), num, den, m_t)
        for z in range(2):
            hs_ref[z, per_dir[z][0], :] = jnp.concatenate(hh[z * nh:(z + 1) * nh], axis=1)
        for i in range(n_st):
            c_ref[i] = decay[i] * cm[i] + vk[i]
            n_ref[i] = decay[i] * nv[i] + jnp.sum(wgt[i] * kc[i], axis=0, keepdims=True)
            m_ref[i] = m_new[i]
        return carry

    lax.fori_loop(0, nc, chunk_step, 0)

    if emit_state:
        for i in range(n_st):
            z, h = divmod(i, ML_HEADS)
            co_ref[z, h] = c_ref[i]
            no_ref[z, h] = n_ref[i]
            mo_ref[z, h] = m_ref[i]

    for h in range(ML_HEADS):
        hsl = slice(h * ML_DH, (h + 1) * ML_DH)
        hn = _head_rms(hs_ref[0, :, hsl] + hs_ref[1, :, hsl], ng_ref[:, hsl])
        y_ref[:, hsl] = (_sigmoid(o_ref[:, hsl]) * hn).astype(y_ref.dtype)


def _mlstm(u, gate_rows, p, layer, n_seq, seq, row_off, state=None, emit_state=False, prev=()):
    lw = lambda shape: pl.BlockSpec((None,) + shape, lambda b: (layer,) + (0,) * len(shape))
    nc = seq // CHUNK
    n_gate = 4 * ML_HEADS
    in_specs = [_u_spec(seq, cb, row_off) for cb in (CB_MLQ, CB_MLK, CB_MLV, CB_MLO, CB_GATE)] + [
        pl.BlockSpec((None, nc, n_gate, CHUNK), lambda b: (b, 0, 0, 0)),
        lw((1, n_gate)), lw((n_gate, 1)), lw((1, MIX_W))]
    args = [u, u, u, u, u, gate_rows, p['ml_gate_b'].reshape(DEPTH, 1, n_gate),
            p['ml_gate_b'].reshape(DEPTH, n_gate, 1), p['ml_norm_g'].reshape(DEPTH, 1, MIX_W)]
    c_shape, n_shape, m_shape = (2, ML_HEADS, ML_DH, ML_DH), (2, ML_HEADS, 1, ML_DH), (2, ML_HEADS, 1, 1)
    if state is not None:
        c0, n0, m0 = state
        for a, shp in ((c0, c_shape), (n0, n_shape), (m0, m_shape)):
            in_specs.append(pl.BlockSpec((None, None) + shp, lambda b: (b, layer, 0, 0, 0, 0)))
            args.append(a.reshape(a.shape[:2] + shp))
    out_shape = [jax.ShapeDtypeStruct((n_seq * seq, MIX_W), BF16)]
    out_specs = [pl.BlockSpec((seq, MIX_W), lambda b: (b, 0))]
    if emit_state:
        for shp in (c_shape, n_shape, m_shape):
            out_shape.append(jax.ShapeDtypeStruct((n_seq, DEPTH) + shp, F32))
            out_specs.append(_state_spec(shp, layer, not prev, lambda b: (b, 0, 0, 0, 0)))
    n_st = 2 * ML_HEADS
    body, prev_specs = _carry_through(
        functools.partial(_mlstm_kernel, seq=seq, has_s0=state is not None, emit_state=emit_state),
        len(in_specs), prev, 1, 3 if emit_state else 0, layer)
    return pl.pallas_call(
        body,
        out_shape=tuple(out_shape), grid=(n_seq,), in_specs=in_specs + prev_specs, out_specs=tuple(out_specs),
        input_output_aliases={len(in_specs) + i: 1 + i for i in range(len(prev))},
        scratch_shapes=[pltpu.VMEM((2, seq, MIX_W), F32), pltpu.VMEM((n_st, ML_DH, ML_DH), F32),
                        pltpu.VMEM((n_st, 1, ML_DH), F32), pltpu.VMEM((n_st, 1, 1), F32)],
        compiler_params=_params(("parallel",)),
        name=f"mlstm_{seq}",
    )(*args, *prev)


def _outproj_kernel(*refs, n_prompt_tiles):
    n_mix = 4
    yp_refs, ys_refs = refs[:n_mix], refs[n_mix:2 * n_mix]
    w_ref, x_ref, g_ref, o_ref = refs[2 * n_mix:]
    is_prompt = pl.program_id(0) < n_prompt_tiles
    acc = None
    for i, (yp_ref, ys_ref) in enumerate(zip(yp_refs, ys_refs)):
        y = jnp.where(is_prompt, yp_ref[...], ys_ref[...])
        part = _mm(y, w_ref[i * MIX_W:(i + 1) * MIX_W, :].astype(BF16))
        acc = part if acc is None else acc + part
    o_ref[...] = x_ref[...] + g_ref[...] * acc


def _outproj(ys_prompt, ys_sample, w_out, x, mod, layer):
    tm, tn = 1024, 1024
    n_prompt_tiles = N_PROMPT // tm
    yp_spec = pl.BlockSpec((tm, MIX_W), lambda i, j: (jnp.minimum(i, n_prompt_tiles - 1), 0))
    ys_spec = pl.BlockSpec((tm, MIX_W), lambda i, j: (jnp.maximum(i - n_prompt_tiles, 0), 0))
    return pl.pallas_call(
        functools.partial(_outproj_kernel, n_prompt_tiles=n_prompt_tiles),
        out_shape=jax.ShapeDtypeStruct((N_TOK, D_MODEL), F32),
        grid=(N_TOK // tm, D_MODEL // tn),
        in_specs=[yp_spec] * 4 + [ys_spec] * 4 + [pl.BlockSpec((None, D_MODEL, tn), lambda i, j: (layer, 0, j)),
                                                   pl.BlockSpec((tm, tn), lambda i, j: (i, j)),
                                                   _mod_spec(layer, 2, tm, tn, with_j=True)],
        out_specs=pl.BlockSpec((tm, tn), lambda i, j: (i, j)),
        compiler_params=_params(("parallel", "parallel")),
        name="outproj",
    )(*ys_prompt, *ys_sample, w_out, x, mod)


def _router_kernel(x_ref, g_ref, sh_ref, sc_ref, rw_ref, rb_ref, xm_ref, gt_ref, ids_ref):
    xm = _modulated(x_ref[...], g_ref[...], sh_ref[...], sc_ref[...])
    xm_ref[...] = xm
    logits = _mm_nt(rw_ref[...], xm, HI)
    ex = jnp.exp(logits - jnp.max(logits, axis=0, keepdims=True))
    scores = ex / jnp.sum(ex, axis=0, keepdims=True)
    sel = scores + rb_ref[...]
    per = N_EXPERTS // N_EXPERT_GROUPS
    s = [sel[e:e + 1, :] for e in range(N_EXPERTS)]
    grp_score = []
    for g in range(N_EXPERT_GROUPS):
        a, b, c, d = s[per * g:per * (g + 1)]
        hi1, lo1, hi2, lo2 = jnp.maximum(a, b), jnp.minimum(a, b), jnp.maximum(c, d), jnp.minimum(c, d)
        grp_score.append(jnp.maximum(hi1, hi2) + jnp.maximum(jnp.minimum(hi1, hi2), jnp.maximum(lo1, lo2)))
    best = functools.reduce(jnp.maximum, grp_score)
    in_grp, taken = [], jnp.zeros_like(best)
    for g in range(N_EXPERT_GROUPS):
        hit = jnp.where(grp_score[g] == best, 1.0, 0.0) * (1.0 - taken)
        in_grp.append(hit)
        taken = taken + hit
    picked, flag = [], []
    for e in range(N_EXPERTS):
        g = e // per
        rank = jnp.zeros_like(best)
        for o in range(per * g, per * (g + 1)):
            if o < e:
                rank += jnp.where(s[o] >= s[e], 1.0, 0.0)
            elif o > e:
                rank += jnp.where(s[o] > s[e], 1.0, 0.0)
        flag.append(in_grp[g] * jnp.where(rank < 2.0, 1.0, 0.0))
        picked.append(flag[e] * scores[e:e + 1, :])
    total = functools.reduce(lambda x, y: x + y, picked)
    for e in range(N_EXPERTS):
        gt_ref[e:e + 1, :] = picked[e] / total
    lo_id = functools.reduce(
        jnp.minimum, [jnp.where(flag[e] > 0.0, float(e), float(N_EXPERTS)) for e in range(N_EXPERTS)])
    hi_id = functools.reduce(jnp.maximum, [jnp.where(flag[e] > 0.0, float(e), -1.0) for e in range(N_EXPERTS)])
    ids_ref[0:1, :] = lo_id
    ids_ref[1:2, :] = hi_id
    ids_ref[2:, :] = jnp.zeros((ids_ref.shape[0] - 2,) + lo_id.shape[1:], F32)


def _router(x, norm_g, mod, router_wt, router_b, layer):
    tm = 512
    return pl.pallas_call(
        _router_kernel,
        out_shape=(jax.ShapeDtypeStruct((N_TOK, D_MODEL), F32), jax.ShapeDtypeStruct((N_EXPERTS, N_TOK), F32),
                   jax.ShapeDtypeStruct((8, N_TOK), F32)),
        grid=(N_TOK // tm,),
        in_specs=[pl.BlockSpec((tm, D_MODEL), lambda i: (i, 0)),
                  pl.BlockSpec((None, 1, D_MODEL), lambda i: (layer, 0, 0)),
                  _mod_spec(layer, 3, tm), _mod_spec(layer, 4, tm),
                  _full((N_EXPERTS, D_MODEL)), _full((N_EXPERTS, 1))],
        out_specs=(pl.BlockSpec((tm, D_MODEL), lambda i: (i, 0)), pl.BlockSpec((N_EXPERTS, tm), lambda i: (0, i)),
                   pl.BlockSpec((8, tm), lambda i: (0, i))),
        compiler_params=_params(("parallel",)),
        name="router",
    )(x, norm_g.reshape(DEPTH, 1, D_MODEL), mod, mod, router_wt, router_b.reshape(N_EXPERTS, 1))


MOE_TM = 512
N_PAIRS = N_EXPERT_GROUPS * 6
MOE_STEPS = 2 * (N_PAIRS + N_TOK // MOE_TM - 1)


def _row_copy(src_hbm, src_row, dst_ref, slot, r, sems):
    return pltpu.make_async_copy(src_hbm.at[pl.ds(src_row, 1), :], dst_ref.at[slot, pl.ds(r, 1), :], sems.at[slot])


def _gather_start(src_hbm, dst_ref, slot, idx_ref, base, n, sems):
    group = 8

    def issue(g, carry):
        for j in range(group):
            r = g * group + j
            _row_copy(src_hbm, idx_ref[base + r], dst_ref, slot, r, sems).start(priority=j % 2)
        return carry

    lax.fori_loop(0, n // group, issue, 0)


def _gather_wait(src_hbm, dst_ref, slot, n, sems):
    def wait(r, carry):
        _row_copy(src_hbm, 0, dst_ref, slot, r, sems).wait()
        return carry

    lax.fori_loop(0, n, wait, 0, unroll=8)


def _gather_tile(src_hbm, dst_ref, idx_ref, tile, n_tiles, sems):
    slot = tile % 2

    @pl.when(tile == 0)
    def _():
        _gather_start(src_hbm, dst_ref, slot, idx_ref, 0, MOE_TM, sems)

    _gather_wait(src_hbm, dst_ref, slot, MOE_TM, sems)

    @pl.when(tile + 1 < n_tiles)
    def _():
        _gather_start(src_hbm, dst_ref, 1 - slot, idx_ref, (tile + 1) * MOE_TM, MOE_TM, sems)

    return slot


def _moe_plan(ids):
    i32 = jnp.int32
    lo, hi = ids[0].astype(i32), ids[1].astype(i32)
    src = jnp.argsort(lo * N_EXPERTS + hi).astype(i32)
    pos = jnp.argsort(src).astype(i32)
    n_tiles = N_TOK // MOE_TM
    ex = jnp.arange(N_EXPERTS, dtype=i32)
    lo_s, hi_s = lo[src].reshape(n_tiles, MOE_TM, 1), hi[src].reshape(n_tiles, MOE_TM, 1)
    used = ((lo_s == ex).any(axis=1) | (hi_s == ex).any(axis=1)).reshape(-1)
    n_valid = jnp.sum(used).astype(i32)
    idx = jnp.nonzero(used, size=MOE_STEPS, fill_value=0)[0].astype(i32)
    valid = jnp.arange(MOE_STEPS, dtype=i32) < n_valid
    idx = jnp.where(valid, idx, idx[jnp.maximum(n_valid - 1, 0)])
    tile, exp = idx // N_EXPERTS, idx % N_EXPERTS
    first = valid & (tile != jnp.concatenate([jnp.full((1,), -1, i32), tile[:-1]]))
    return src, pos, tile, exp, first.astype(i32), valid.astype(i32)


def _moe_kernel(tile_ref, exp_ref, first_ref, valid_ref, src_ref, xm_hbm, gates_ref, w1_ref, w3_ref, w2_ref,
                o_ref, xs_ref, xb_ref, sem):
    s = pl.program_id(0)

    @pl.when(first_ref[s] == 1)
    def _():
        slot = _gather_tile(xm_hbm, xs_ref, src_ref, tile_ref[s], N_TOK // MOE_TM, sem)
        xb_ref[...] = xs_ref[slot].astype(BF16)
        o_ref[...] = jnp.zeros_like(o_ref)

    @pl.when(valid_ref[s] == 1)
    def _():
        xb = xb_ref[...]
        h1 = _mm(xb, w1_ref[...].astype(BF16))
        h3 = _mm(xb, w3_ref[...].astype(BF16))
        gates = gates_ref[...]
        lane = lax.broadcasted_iota(jnp.int32, gates.shape, 1)
        gate = jnp.sum(jnp.where(lane == exp_ref[s], gates, 0.0), axis=-1, keepdims=True)
        hh = h1 * _sigmoid(h1) * h3 * gate
        o_ref[...] += _mm(hh.astype(BF16), w2_ref[...].astype(BF16))


def _moe(xm, gates_sorted, plan, w1, w3, w2, layer):
    src, _, tile, exp, first, valid = plan
    w_in = pl.BlockSpec((None, None, D_MODEL, D_EXPERT), lambda s, t, e, f, v, i: (layer, e[s], 0, 0))
    grid_spec = pltpu.PrefetchScalarGridSpec(
        num_scalar_prefetch=5, grid=(MOE_STEPS,),
        in_specs=[pl.BlockSpec(memory_space=pl.ANY),
                  pl.BlockSpec((MOE_TM, N_EXPERTS), lambda s, t, e, f, v, i: (t[s], 0)),
                  w_in, w_in,
                  pl.BlockSpec((None, None, D_EXPERT, D_MODEL), lambda s, t, e, f, v, i: (layer, e[s], 0, 0))],
        out_specs=pl.BlockSpec((MOE_TM, D_MODEL), lambda s, t, e, f, v, i: (t[s], 0)),
        scratch_shapes=[pltpu.VMEM((2, MOE_TM, D_MODEL), F32), pltpu.VMEM((MOE_TM, D_MODEL), BF16),
                        pltpu.SemaphoreType.DMA((2,))])
    return pl.pallas_call(
        _moe_kernel, out_shape=jax.ShapeDtypeStruct((N_TOK, D_MODEL), F32), grid_spec=grid_spec,
        compiler_params=_params(("arbitrary",)), name="moe",
    )(tile, exp, first, valid, src, xm, gates_sorted, w1, w3, w2)


def _combine_kernel(pos_ref, acc_hbm, x_ref, g_ref, *refs, split):
    i = pl.program_id(0)
    if split:
        *out_refs, buf_ref, sem = refs
    else:
        ng_ref, nsh_ref, nsc_ref, *out_refs, buf_ref, sem = refs
    slot = _gather_tile(acc_hbm, buf_ref, pos_ref, i, N_TOK // MOE_TM, sem)
    val = x_ref[...] + g_ref[...] * buf_ref[slot]
    if not split:
        out_refs[0][...] = val
        out_refs[1][...] = _modulated(val, ng_ref[...], nsh_ref[...], nsc_ref[...]).astype(BF16)
        return
    n_prompt_tiles = N_PROMPT // MOE_TM

    @pl.when(i < n_prompt_tiles)
    def _():
        out_refs[0][...] = val

    @pl.when(i >= n_prompt_tiles)
    def _():
        out_refs[1][...] = val


def _combine(acc_sorted, pos, x, mod, norm1_g, layer):
    tm = MOE_TM
    split = layer == DEPTH - 1
    n_prompt_tiles = N_PROMPT // tm
    row = pl.BlockSpec((tm, D_MODEL), lambda i, p: (i, 0))
    mod_row = lambda l, chunk: pl.BlockSpec((None, None, None, 1, D_MODEL),
                                            lambda i, p: (l, _mod_row(i, tm), chunk, 0, 0))
    in_specs = [pl.BlockSpec(memory_space=pl.ANY), row, mod_row(layer, 5)]
    args = [pos, acc_sorted, x, mod]
    if split:
        out_shape = (jax.ShapeDtypeStruct((N_PROMPT, D_MODEL), F32), jax.ShapeDtypeStruct((N_SAMPLE, D_MODEL), F32))
        out_specs = (pl.BlockSpec((tm, D_MODEL), lambda i, p: (jnp.minimum(i, n_prompt_tiles - 1), 0)),
                     pl.BlockSpec((tm, D_MODEL), lambda i, p: (jnp.maximum(i - n_prompt_tiles, 0), 0)))
    else:
        in_specs += [pl.BlockSpec((None, 1, D_MODEL), lambda i, p: (layer + 1, 0, 0)),
                     mod_row(layer + 1, 0), mod_row(layer + 1, 1)]
        args += [norm1_g.reshape(DEPTH, 1, D_MODEL), mod, mod]
        out_shape = (jax.ShapeDtypeStruct((N_TOK, D_MODEL), F32), jax.ShapeDtypeStruct((N_TOK, D_MODEL), BF16))
        out_specs = (row, row)
    grid_spec = pltpu.PrefetchScalarGridSpec(
        num_scalar_prefetch=1, grid=(N_TOK // tm,), in_specs=in_specs, out_specs=out_specs,
        scratch_shapes=[pltpu.VMEM((2, tm, D_MODEL), F32), pltpu.SemaphoreType.DMA((2,))])
    return pl.pallas_call(
        functools.partial(_combine_kernel, split=split), out_shape=out_shape, grid_spec=grid_spec,
        compiler_params=_params(("arbitrary",)), name="combine",
    )(*args)


def _gate_rows(gcols, n_seq, seq):
    return gcols.reshape(n_seq, seq // CHUNK, CHUNK, gcols.shape[-1]).transpose(0, 1, 3, 2)


def kernel(x_prompt, x_sample, cache_na_k, cache_na_v, state_rwkv, state_mlstm_c, state_mlstm_n, state_mlstm_m,
           c, c_ctx, norm1_g, norm2_g, w_mod, b_mod, w_in, conv_w, na_q_g, na_k_g, na_rpb, rw_w0, rw_w_up, rw_a0,
           rw_a_up, rw_g_up, rw_k_k, rw_k_a, rw_r_k, rw_ln_g, rw_ln_b, ml_gate_b, ml_norm_g, w_out, router_w,
           router_b, moe_w1, moe_w3, moe_w2):
    p = dict(rw_w0=rw_w0, rw_w_up=rw_w_up, rw_a0=rw_a0, rw_a_up=rw_a_up, rw_g_up=rw_g_up, rw_k_k=rw_k_k,
             rw_k_a=rw_k_a, rw_r_k=rw_r_k, rw_ln_g=rw_ln_g, rw_ln_b=rw_ln_b, ml_gate_b=ml_gate_b,
             ml_norm_g=ml_norm_g)
    cvecs = jnp.concatenate([c_ctx[None], c, jnp.zeros((MOD_ROWS - 1 - DEC_BATCH, D_MODEL), F32)], axis=0)
    mod = _adaln(cvecs, w_mod, b_mod).reshape(DEPTH, MOD_ROWS, 6, 1, D_MODEL)

    assert w_in.shape[-1] == P_IN
    w_in_t = jnp.swapaxes(w_in, 1, 2)
    tw = _na_tables(na_rpb)
    router_wt = router_w.T
    sample_row_off = N_PROMPT // DEC_SEQ

    x, xm1 = _premod(x_prompt.reshape(N_PROMPT, D_MODEL), x_sample.reshape(N_SAMPLE, D_MODEL), norm1_g, mod)
    kv_prev, rw_prev, ml_prev = (), (), ()
    for l in range(DEPTH):
        u = _inproj(xm1, w_in_t, l)
        ya_p, yb_p, *kv_prev = _attn_prompt(u, conv_w, na_q_g, na_k_g, l, prev=tuple(kv_prev))
        ya_s, yb_s = _na_sample(u, cache_na_k, cache_na_v, tw, conv_w, na_q_g, na_k_g, l)
        yc_p, *rw_prev = _rwkv(u, p, l, BATCH, SEQ, 0, emit_state=True, prev=tuple(rw_prev))
        (yc_s,) = _rwkv(u, p, l, DEC_BATCH, DEC_SEQ, sample_row_off, s0=state_rwkv)
        g0 = CB_GATE * MIX_W + SM_GATE
        gcols = u[:, g0:g0 + 4 * ML_HEADS]
        yd_p, *ml_prev = _mlstm(u, _gate_rows(gcols[:N_PROMPT], BATCH, SEQ), p, l, BATCH, SEQ, 0,
                                emit_state=True, prev=tuple(ml_prev))
        (yd_s,) = _mlstm(u, _gate_rows(gcols[N_PROMPT:], DEC_BATCH, DEC_SEQ), p, l, DEC_BATCH, DEC_SEQ,
                         sample_row_off, state=(state_mlstm_c, state_mlstm_n, state_mlstm_m))
        x = _outproj((ya_p, yb_p, yc_p, yd_p), (ya_s, yb_s, yc_s, yd_s), w_out, x, mod, l)
        xm, gates_t, ids = _router(x, norm2_g, mod, router_wt, router_b, l)
        plan = _moe_plan(ids)
        acc_sorted = _moe(xm, gates_t.T[plan[0]], plan, moe_w1, moe_w3, moe_w2, l)
        combined = _combine(acc_sorted, plan[1], x, mod, norm1_g, l)
        if l < DEPTH - 1:
            x, xm1 = combined
    y_prompt, y_sample = combined
    new_c, new_n, new_m = ml_prev
    return (y_prompt.reshape(BATCH, SEQ, D_MODEL), y_sample.reshape(DEC_BATCH, DEC_SEQ, D_MODEL),
            kv_prev[0], kv_prev[1], rw_prev[0], new_c, new_n.reshape(BATCH, DEPTH, 2, ML_HEADS, ML_DH),
            new_m.reshape(BATCH, DEPTH, 2, ML_HEADS))
```

```python
import functools

import numpy as np
import jax
import jax.numpy as jnp
from jax import lax
from jax.experimental import pallas as pl
from jax.experimental.pallas import tpu as pltpu

F32 = jnp.float32
BF16 = jnp.bfloat16
HI = lax.Precision.HIGHEST

D_MODEL = 2048
BATCH = 16
SEQ = 256
DEPTH = 2
DEC_BATCH = 2
DEC_SEQ = 1024
PAST_LEN = 512
GRID_W = 64
MIX_W = D_MODEL // 4
CONV_K = 3
NA_DH = 64
NA_HEADS = MIX_W // NA_DH
NA_WIN_R = 8
NA_WIN_C = 16
NA_SCALE = NA_DH ** -0.5
ROPE_THETA = 10000.0
RW_DH = 64
RW_HEADS = MIX_W // RW_DH
RW_W_RANK = 64
RW_A_RANK = 64
RW_G_RANK = 128
RW_DECAY_SCALE = 0.606531
RW_GN_EPS = 64e-5
ML_DH = 128
ML_HEADS = MIX_W // ML_DH
N_EXPERTS = 16
N_EXPERT_GROUPS = 4
D_EXPERT = 512
EPS = 1e-6
NEG_INF = -1e30

N_PROMPT = BATCH * SEQ
N_SAMPLE = DEC_BATCH * DEC_SEQ
N_TOK = N_PROMPT + N_SAMPLE
MOD_ROWS = 8
CHUNK = 64
SUB = 16
P1_CHUNKS = 2
NA_ROWS_PER_TRIP = 4
P_IN = 13 * MIX_W + RW_W_RANK + RW_A_RANK + RW_G_RANK + 4 * ML_HEADS
P_BLOCKS = 15
P_PAD = P_BLOCKS * MIX_W
(CB_CVB, CB_CVC, CB_CVH, CB_NAQ, CB_NAK, CB_NAV, CB_RWR, CB_RWK, CB_RWV,
 CB_MLQ, CB_MLK, CB_MLV, CB_MLO, CB_SMALL, CB_GATE) = range(P_BLOCKS)
SM_WL, SM_AL, SM_GL = 0, 64, 128
SM_GATE = MIX_W - 4 * ML_HEADS
VMEM_LIMIT = 56 * 1024 * 1024


def _mm(a, b, prec=None):
    return jnp.dot(a, b, precision=prec, preferred_element_type=F32)


def _mm_nt(a, b, prec=None):
    return lax.dot_general(a, b, (((1,), (1,)), ((), ())), precision=prec, preferred_element_type=F32)


def _sigmoid(x):
    return 1.0 / (1.0 + jnp.exp(-x))


def _full(shape):
    n = len(shape)
    return pl.BlockSpec(shape, lambda *_: (0,) * n)


def _params(sem):
    return pltpu.CompilerParams(dimension_semantics=sem, vmem_limit_bytes=VMEM_LIMIT)


def _mod_row(i, tm):
    n_prompt_tiles = N_PROMPT // tm
    tiles_per_sample = DEC_SEQ // tm
    return jnp.where(i < n_prompt_tiles, 0, 1 + (i - n_prompt_tiles) // tiles_per_sample)


def _mod_spec(layer, chunk, tm, tn=D_MODEL, with_j=False):
    if with_j:
        return pl.BlockSpec((None, None, None, 1, tn), lambda i, j: (layer, _mod_row(i, tm), chunk, 0, j))
    return pl.BlockSpec((None, None, None, 1, tn), lambda i, *_: (layer, _mod_row(i, tm), chunk, 0, 0))


def _tri_masks(n):
    t = lax.broadcasted_iota(jnp.int32, (n, n), 0)
    s = lax.broadcasted_iota(jnp.int32, (n, n), 1)
    incl = ((s <= t).astype(F32), (s >= t).astype(F32))
    strict = ((s < t).astype(F32), (s > t).astype(F32))
    return incl, strict, t, s


def _split2(x):
    hi = x.astype(BF16)
    return hi, (x - hi.astype(F32)).astype(BF16)


def _mm1(a, b):
    return _mm(a.astype(BF16), b.astype(BF16))


def _mm3(a, b):
    a_hi, a_lo = _split2(a)
    b_hi, b_lo = _split2(b)
    return _mm(a_hi, b_lo) + _mm(a_lo, b_hi) + _mm(a_hi, b_hi)


def _split3(x):
    x0 = x.astype(BF16)
    r1 = x - x0.astype(F32)
    x1 = r1.astype(BF16)
    return x0, x1, (r1 - x1.astype(F32)).astype(BF16)


def _mm_exact_lhs(mask_b, x):
    x0, x1, x2 = _split3(x)
    return _mm(mask_b, x2) + _mm(mask_b, x1) + _mm(mask_b, x0)


def _mm_exact_rhs(x, mask_b):
    x0, x1, x2 = _split3(x)
    return _mm(x2, mask_b) + _mm(x1, mask_b) + _mm(x0, mask_b)


def _adaln_kernel(cv_ref, w_ref, b_ref, o_ref):
    cv = cv_ref[...]
    a0, a1, a2 = _split3(cv * _sigmoid(cv))
    w_hi, w_lo = _split2(w_ref[...])
    o_ref[...] = ((_mm(a2, w_hi) + _mm(a1, w_lo)) + (_mm(a1, w_hi) + _mm(a0, w_lo)) + _mm(a0, w_hi)) + b_ref[...]


def _adaln(cvecs, w_mod, b_mod):
    tn = 1024
    n_out = 6 * D_MODEL
    return pl.pallas_call(
        _adaln_kernel,
        out_shape=jax.ShapeDtypeStruct((DEPTH, MOD_ROWS, n_out), F32),
        grid=(DEPTH, n_out // tn),
        in_specs=[_full((MOD_ROWS, D_MODEL)),
                  pl.BlockSpec((None, D_MODEL, tn), lambda l, j: (l, 0, j)),
                  pl.BlockSpec((None, 1, tn), lambda l, j: (l, 0, j))],
        out_specs=pl.BlockSpec((None, MOD_ROWS, tn), lambda l, j: (l, 0, j)),
        compiler_params=_params(("parallel", "parallel")),
        name="adaln",
    )(cvecs, w_mod, b_mod.reshape(DEPTH, 1, n_out))


def _modulated(x, g, sh, sc):
    y = x * lax.rsqrt(jnp.mean(x * x, axis=-1, keepdims=True) + EPS) * g
    return y * (1.0 + sc) + sh


def _premod_kernel(xp_ref, xs_ref, g_ref, sh_ref, sc_ref, x_ref, xm_ref, *, n_prompt_tiles):
    x = jnp.where(pl.program_id(0) < n_prompt_tiles, xp_ref[...], xs_ref[...])
    x_ref[...] = x
    xm_ref[...] = _modulated(x, g_ref[...], sh_ref[...], sc_ref[...]).astype(BF16)


def _premod(x_prompt, x_sample, norm_g, mod):
    tm = 512
    n_prompt_tiles = N_PROMPT // tm
    row = pl.BlockSpec((tm, D_MODEL), lambda i: (i, 0))
    return pl.pallas_call(
        functools.partial(_premod_kernel, n_prompt_tiles=n_prompt_tiles),
        out_shape=(jax.ShapeDtypeStruct((N_TOK, D_MODEL), F32), jax.ShapeDtypeStruct((N_TOK, D_MODEL), BF16)),
        grid=(N_TOK // tm,),
        in_specs=[pl.BlockSpec((tm, D_MODEL), lambda i: (jnp.minimum(i, n_prompt_tiles - 1), 0)),
                  pl.BlockSpec((tm, D_MODEL), lambda i: (jnp.maximum(i - n_prompt_tiles, 0), 0)),
                  pl.BlockSpec((None, 1, D_MODEL), lambda i: (0, 0, 0)),
                  _mod_spec(0, 0, tm), _mod_spec(0, 1, tm)],
        out_specs=(row, row),
        compiler_params=_params(("parallel",)),
        name="premod",
    )(x_prompt, x_sample, norm_g.reshape(DEPTH, 1, D_MODEL), mod, mod)


def _inproj_kernel(xm_ref, w_ref, o_ref):
    o_ref[...] = _mm_nt(xm_ref[...], w_ref[0].astype(BF16))


def _inproj_src_row(j):
    a = 9
    narrow = RW_W_RANK + RW_A_RANK + RW_G_RANK
    g = 16
    return g * jnp.where(j < a, j * (MIX_W // g),
                         jnp.where(j < CB_SMALL, j * (MIX_W // g) + narrow // g,
                                   jnp.where(j == CB_SMALL, a * MIX_W // g, (P_IN - MIX_W) // g)))


def _inproj(xm, w_in_t, layer):
    tm, tn = 3072, MIX_W
    return pl.pallas_call(
        _inproj_kernel,
        out_shape=jax.ShapeDtypeStruct((N_TOK, P_PAD), F32),
        grid=(N_TOK // tm, P_BLOCKS),
        in_specs=[pl.BlockSpec((tm, D_MODEL), lambda i, j: (i, 0)),
                  pl.BlockSpec((pl.Element(1), pl.Element(tn), pl.Element(D_MODEL)),
                               lambda i, j: (layer, _inproj_src_row(j), 0))],
        out_specs=pl.BlockSpec((tm, tn), lambda i, j: (i, j)),
        compiler_params=_params(("parallel", "parallel")),
        name="inproj",
    )(xm, w_in_t)


def _conv_mix(b, c, h, w):
    u = c * h
    n = u.shape[0]
    row = lax.broadcasted_iota(jnp.int32, u.shape, 0)
    prev = jnp.where(row == 0, 0.0, pltpu.roll(u, 1, axis=0))
    nxt = jnp.where(row == n - 1, 0.0, pltpu.roll(u, n - 1, axis=0))
    return b * (prev * w[0:1] + u * w[1:2] + nxt * w[2:3])


def _head_rms(x, g):
    return x * lax.rsqrt(jnp.mean(x * x, axis=-1, keepdims=True) + EPS) * g


def _attn_prompt_kernel(cb_ref, cc_ref, ch_ref, q_ref, k_ref, v_ref, cw_ref, qg_ref, kg_ref,
                        ya_ref, yb_ref, nk_ref, nv_ref):
    ya_ref[...] = _conv_mix(cb_ref[...], cc_ref[...], ch_ref[...], cw_ref[...]).astype(ya_ref.dtype)
    sls = [slice(h * NA_DH, (h + 1) * NA_DH) for h in range(NA_HEADS)]
    qn = [_head_rms(q_ref[:, sl], qg_ref[...]) * NA_SCALE for sl in sls]
    kn = [_head_rms(k_ref[:, sl], kg_ref[...]) for sl in sls]
    vh = [v_ref[:, sl] for sl in sls]
    s = [_mm_nt(q.astype(BF16), k.astype(BF16)) for q, k in zip(qn, kn)]
    p = [jnp.exp(x - jnp.max(x, axis=-1, keepdims=True)) for x in s]
    o = [_mm(x.astype(BF16), v.astype(BF16)) / jnp.sum(x, axis=-1, keepdims=True) for x, v in zip(p, vh)]
    yb_ref[...] = jnp.concatenate(o, axis=1).astype(yb_ref.dtype)
    for h in range(NA_HEADS):
        nk_ref[h] = kn[h]
        nv_ref[h] = vh[h]


def _u_spec(rows, col_block, row_off_blocks=0):
    return pl.BlockSpec((rows, MIX_W), lambda b: (b + row_off_blocks, col_block))


def _carry_through(kernel, n_inputs, prev, n_plain_out, n_state_out, layer):
    n_prev = len(prev)

    def body(*refs):
        ins, rest = refs[:n_inputs], list(refs[n_inputs + n_prev:])
        if not n_prev:
            for i in range(n_plain_out, n_plain_out + n_state_out):
                full = rest[i]
                for d in range(DEPTH):
                    if d != layer:
                        full[d] = jnp.zeros(full.shape[1:], full.dtype)
                rest[i] = full.at[layer]
        return kernel(*ins, *rest)

    return body, [pl.BlockSpec(memory_space=pl.ANY)] * n_prev


def _state_spec(shape, layer, first, index):
    def index_map(*g):
        b, *tail = index(*g)
        return (b, 0 if first else layer, *tail)
    return pl.BlockSpec((None, DEPTH if first else None) + tuple(shape), index_map)


def _attn_prompt(u, conv_w, q_g, k_g, layer, prev=(), n_seq=BATCH, seq=SEQ):
    lw = lambda shape: pl.BlockSpec((None,) + shape, lambda b: (layer,) + (0,) * len(shape))
    y_spec = pl.BlockSpec((seq, MIX_W), lambda b: (b, 0))
    kv_spec = _state_spec((NA_HEADS, seq, NA_DH), layer, not prev, lambda b: (b, 0, 0, 0))
    in_specs = ([_u_spec(seq, cb) for cb in (CB_CVB, CB_CVC, CB_CVH, CB_NAQ, CB_NAK, CB_NAV)]
                + [lw((CONV_K, MIX_W)), lw((1, NA_DH)), lw((1, NA_DH))])
    body, prev_specs = _carry_through(_attn_prompt_kernel, len(in_specs), prev, 2, 2, layer)
    return pl.pallas_call(
        body,
        out_shape=(jax.ShapeDtypeStruct((n_seq * seq, MIX_W), BF16),) * 2
        + (jax.ShapeDtypeStruct((n_seq, DEPTH, NA_HEADS, seq, NA_DH), F32),) * 2,
        grid=(n_seq,),
        in_specs=in_specs + prev_specs,
        out_specs=(y_spec, y_spec, kv_spec, kv_spec),
        input_output_aliases={len(in_specs) + i: 2 + i for i in range(len(prev))},
        compiler_params=_params(("parallel",)),
        name="attn_prompt",
    )(u, u, u, u, u, u, conv_w, q_g.reshape(DEPTH, 1, NA_DH), k_g.reshape(DEPTH, 1, NA_DH), *prev)


def _na_kernel(cb_ref, cc_ref, ch_ref, q_ref, k_ref, v_ref, kc_ref, vc_ref, tw_ref, cos_ref, sin_ref,
               perm_ref, cw_ref, qg_ref, kg_ref, ya_ref, yb_ref, qs_ref, ks_ref, tws_ref):
    rows = DEC_SEQ // GRID_W
    wr = min(NA_WIN_R, rows)
    nw = wr * GRID_W
    ya_ref[...] = _conv_mix(cb_ref[...], cc_ref[...], ch_ref[...], cw_ref[...]).astype(ya_ref.dtype)
    cos, sin, perm_b = cos_ref[...], sin_ref[...], perm_ref[...].astype(BF16)

    def rope(x):
        hi, lo = _split2(x)
        return x * cos + (_mm(hi, perm_b) + _mm(lo, perm_b)) * sin

    for h in range(NA_HEADS):
        sl = slice(h * NA_DH, (h + 1) * NA_DH)
        qs_ref[...] = (rope(_head_rms(q_ref[:, sl], qg_ref[...])) * NA_SCALE).astype(BF16)
        ks_ref[...] = rope(_head_rms(k_ref[:, sl], kg_ref[...])).astype(BF16)
        kch = kc_ref[h].astype(BF16)
        vch = vc_ref[h].astype(BF16)
        for p in range(wr):
            tws_ref[p] = jnp.concatenate([tw_ref[h, j - p + NA_WIN_R - 1] for j in range(wr)], axis=1)

        def rows_step(t, carry):
            rr = [t * NA_ROWS_PER_TRIP + i for i in range(NA_ROWS_PER_TRIP)]
            rs = [jnp.clip(r - wr // 2, 0, rows - wr) for r in rr]
            q0 = [pl.multiple_of(r * GRID_W, GRID_W) for r in rr]
            k0 = [pl.multiple_of(x * GRID_W, GRID_W) for x in rs]
            q_r = [qs_ref[pl.ds(x, GRID_W), :] for x in q0]
            s_w = [_mm_nt(q, ks_ref[pl.ds(k, nw), :]) + tws_ref[r - x] for q, k, r, x in zip(q_r, k0, rr, rs)]
            s_c = [_mm_nt(q, kch) for q in q_r]
            m = [jnp.maximum(jnp.max(a, axis=-1, keepdims=True), jnp.max(b, axis=-1, keepdims=True))
                 for a, b in zip(s_w, s_c)]
            p_w = [jnp.exp(a - x) for a, x in zip(s_w, m)]
            p_c = [jnp.exp(b - x) for b, x in zip(s_c, m)]
            den = [jnp.sum(a, axis=-1, keepdims=True) + jnp.sum(b, axis=-1, keepdims=True) for a, b in zip(p_w, p_c)]
            v_w = [v_ref[pl.ds(k, nw), sl].astype(BF16) for k in k0]
            o = [(_mm(a.astype(BF16), v) + _mm(b.astype(BF16), vch)) / d for a, b, v, d in zip(p_w, p_c, v_w, den)]
            for x, val in zip(q0, o):
                yb_ref[pl.ds(x, GRID_W), sl] = val.astype(yb_ref.dtype)
            return carry

        lax.fori_loop(0, rows // NA_ROWS_PER_TRIP, rows_step, 0)


def _na_tables(rpb):
    qc = np.arange(GRID_W)
    kc = np.arange(GRID_W)
    wstart = np.clip(qc - NA_WIN_C // 2, 0, GRID_W - NA_WIN_C)
    colmask = (kc[None, :] >= wstart[:, None]) & (kc[None, :] < wstart[:, None] + NA_WIN_C)
    dc = np.clip(kc[None, :] - qc[:, None], -(NA_WIN_C - 1), NA_WIN_C - 1) + NA_WIN_C - 1
    pick = (np.arange(2 * NA_WIN_C - 1)[:, None] == dc.reshape(1, -1)).astype(np.float32)
    bias = jnp.einsum('lhrd,dn->lhrn', rpb, jnp.asarray(pick), precision=HI)
    return jnp.where(colmask[None, None, None], bias.reshape(rpb.shape[:3] + dc.shape), NEG_INF)


def _rope_tables():
    t = np.arange(DEC_SEQ)
    quarter = NA_DH // 4
    freq = ROPE_THETA ** (-np.arange(quarter, dtype=np.float32) / quarter)
    ang_r = (t // GRID_W).astype(np.float32)[:, None] * freq
    ang_c = (t % GRID_W).astype(np.float32)[:, None] * freq
    cos = np.concatenate([np.cos(ang_r), np.cos(ang_r), np.cos(ang_c), np.cos(ang_c)], axis=-1)
    sin = np.concatenate([-np.sin(ang_r), np.sin(ang_r), -np.sin(ang_c), np.sin(ang_c)], axis=-1)
    src = np.concatenate([np.arange(quarter) + quarter, np.arange(quarter),
                          np.arange(quarter) + 3 * quarter, np.arange(quarter) + 2 * quarter])
    perm = np.zeros((NA_DH, NA_DH), np.float32)
    perm[src, np.arange(NA_DH)] = 1.0
    return cos.astype(np.float32), sin.astype(np.float32), perm


def _na_sample(u, cache_k, cache_v, tw, conv_w, q_g, k_g, layer, n_seq=DEC_BATCH, row_off=N_PROMPT // DEC_SEQ):
    cos, sin, perm = _rope_tables()
    lw = lambda shape: pl.BlockSpec((None,) + shape, lambda b: (layer,) + (0,) * len(shape))
    y_spec = pl.BlockSpec((DEC_SEQ, MIX_W), lambda b: (b, 0))
    c_spec = pl.BlockSpec((None, None, NA_HEADS, PAST_LEN, NA_DH), lambda b: (b, layer, 0, 0, 0))
    wr = min(NA_WIN_R, DEC_SEQ // GRID_W)
    n_off = 2 * NA_WIN_R - 1
    return pl.pallas_call(
        _na_kernel,
        out_shape=(jax.ShapeDtypeStruct((n_seq * DEC_SEQ, MIX_W), BF16),) * 2,
        grid=(n_seq,),
        in_specs=[_u_spec(DEC_SEQ, cb, row_off) for cb in (CB_CVB, CB_CVC, CB_CVH, CB_NAQ, CB_NAK, CB_NAV)]
        + [c_spec, c_spec, lw((NA_HEADS, n_off, GRID_W, GRID_W)),
           _full((DEC_SEQ, NA_DH)), _full((DEC_SEQ, NA_DH)), _full((NA_DH, NA_DH)),
           lw((CONV_K, MIX_W)), lw((1, NA_DH)), lw((1, NA_DH))],
        out_specs=(y_spec, y_spec),
        scratch_shapes=[pltpu.VMEM((DEC_SEQ, NA_DH), BF16), pltpu.VMEM((DEC_SEQ, NA_DH), BF16),
                        pltpu.VMEM((wr, GRID_W, wr * GRID_W), F32)],
        compiler_params=_params(("parallel",)),
        name="na_sample",
    )(u, u, u, u, u, u, cache_k, cache_v, tw, jnp.asarray(cos), jnp.asarray(sin), jnp.asarray(perm),
      conv_w, q_g.reshape(DEPTH, 1, NA_DH), k_g.reshape(DEPTH, 1, NA_DH))


def _seg_ones(width, seg):
    a = lax.broadcasted_iota(jnp.int32, (width, width), 0) // seg
    b = lax.broadcasted_iota(jnp.int32, (width, width), 1) // seg
    return (a == b).astype(F32)


def _rwkv_kernel(*refs, seq, has_s0, emit_state):
    it = iter(refs)
    r_ref, k_ref, v_ref, sm_ref = (next(it) for _ in range(4))
    (w0_ref, wup_ref, a0_ref, aup_ref, gup_ref, kkp_ref, kap_ref, rkp_ref, lng_ref, lnb_ref) = (
        next(it) for _ in range(10))
    mask_ref = next(it)
    s0_ref = next(it) if has_s0 else None
    y_ref = next(it)
    so_ref = next(it) if emit_state else None
    kk_ref, lw_ref, ka_ref, kd_ref, coef_ref, ysp_ref = (next(it) for _ in range(6))

    dh = RW_DH
    pw = 2 * dh
    nc = seq // CHUNK
    seg_b = _seg_ones(pw, dh).astype(BF16)

    r = r_ref[...]
    k = k_ref[...]
    sm = sm_ref[...]
    wl = jnp.tanh(sm[:, SM_WL:SM_WL + RW_W_RANK])
    al = sm[:, SM_AL:SM_AL + RW_A_RANK]
    kk = k * kkp_ref[...]
    kk = kk * lax.rsqrt(_mm_exact_rhs(kk * kk, seg_b) + EPS)
    kk_ref[...] = kk
    for z in range(2):
        lw_ref[z] = -RW_DECAY_SCALE * _sigmoid(w0_ref[z:z + 1, :] + _mm3(wl, wup_ref[z]))
        a = _sigmoid(a0_ref[z:z + 1, :] + _mm3(al, aup_ref[z]))
        ka_ref[z] = kk * a
        kd_ref[z] = k * (1.0 + (a - 1.0) * kap_ref[...])

    cat = jnp.concatenate

    def phase1(c2, carry):
        incl, strict, ti, si = _tri_masks(CHUNK)
        incl_b = (mask_ref[0], mask_ref[1])
        strict2 = tuple(cat([m, m], axis=1) for m in strict)
        incl2 = tuple(cat([m, m], axis=1) for m in incl)
        diag_blk = (ti // SUB == si // SUB).astype(F32)
        eye = (ti == si).astype(F32)
        zero_pair = jnp.zeros((CHUNK, pw), F32)
        head_of_lane = lax.broadcasted_iota(jnp.int32, (CHUNK, pw), 1) // dh
        head_of_lane2 = (lax.broadcasted_iota(jnp.int32, (CHUNK, 2 * pw), 1) // dh) % 2
        r_pw = lax.broadcasted_iota(jnp.int32, (pw, pw), 0)
        c_pw = lax.broadcasted_iota(jnp.int32, (pw, pw), 1)
        same_head = r_pw // dh == c_pw // dh
        eye_pw = (r_pw == c_pw).astype(F32)
        chains = []
        for cc in range(P1_CHUNKS):
            rows = pl.ds(pl.multiple_of((c2 * P1_CHUNKS + cc) * CHUNK, CHUNK), CHUNK)
            vc, rc, kkc = v_ref[rows, :], r_ref[rows, :], kk_ref[rows, :]
            for z in range(2):
                lwc = lw_ref[z, rows, :]
                cum = _mm_exact_lhs(incl_b[z], lwc)
                tot = cum[CHUNK - 1:CHUNK] if z == 0 else cum[0:1]
                e_neg = jnp.exp(-cum)
                dec = jnp.exp(tot - cum)
                kac, kdc = ka_ref[z, rows, :], kd_ref[z, rows, :]
                chains.append((z, kkc * jnp.exp(cum - lwc), rc * jnp.exp(cum), vc, kac * e_neg, kdc * e_neg,
                               kac * dec, kdc * dec, jnp.exp(tot)))
        heads = [(q, j) for q in range(len(chains)) for j in range(2)]
        zq = [ch[0] for ch in chains]
        ymat = [cat([ch[4], ch[5]], axis=0).astype(BF16) for ch in chains]
        vz = [cat([zero_pair, ch[3]], axis=0) for ch in chains]
        nvz = [cat([zero_pair, -ch[3]], axis=1) for ch in chains]
        x_in = [cat([jnp.where(head_of_lane == j, chains[q][1], 0.0),
                     jnp.where(head_of_lane == j, chains[q][2], 0.0)], axis=0) for q, j in heads]
        aa = [_mm_nt(x.astype(BF16), ymat[q]) for x, (q, _) in zip(x_in, heads)]
        top = [a[0:CHUNK] * strict2[zq[q]] for a, (q, _) in zip(aa, heads)]
        a_r = [a[CHUNK:] * incl2[zq[q]] for a, (q, _) in zip(aa, heads)]
        akv = [_mm1(t, vz[q]) for t, (q, _) in zip(top, heads)]
        low = [t[:, 0:CHUNK] for t in top]
        ld = [x * diag_blk for x in low]
        lo = [x - y for x, y in zip(low, ld)]
        l2 = [_mm1(x, x) for x in ld]
        l4 = [_mm1(x, x) for x in l2]
        l8 = [_mm1(x, x) for x in l4]
        td = [eye - x for x in ld]
        for lp in (l2, l4, l8):
            td = [t + _mm1(t, p_) for t, p_ in zip(td, lp)]
        x0 = [_mm1(t, cat([chains[q][1], a, l], axis=1)) for t, a, l, (q, _) in zip(td, akv, lo, heads)]
        pq0 = [x[:, 0:2 * pw] for x in x0]
        wm = [x[:, 2 * pw:] for x in x0]
        pq = pq0
        for _ in range(CHUNK // SUB - 1):
            pq = [p0 - _mm1(w, p_) for p0, w, p_ in zip(pq0, wm, pq)]
        ryc = [_mm1(a, cat([p_, nvz[q]], axis=0)) for a, p_, (q, _) in zip(a_r, pq, heads)]
        for q, ch in enumerate(chains):
            pq_m = jnp.where(head_of_lane2 == 0, pq[2 * q], pq[2 * q + 1])
            ryc_m = jnp.where(head_of_lane2 == 0, ryc[2 * q], ryc[2 * q + 1])
            g1 = _mm1(ch[6].T, pq_m)
            g2 = _mm1(ch[7].T, ch[3])
            g_t = eye_pw * ch[8] - jnp.where(same_head, g1[:, 0:pw], 0.0)
            h_t = jnp.where(same_head, g2 - g1[:, pw:], 0.0)
            coef_ref[c2 * P1_CHUNKS + q // 2, q % 2] = cat(
                [g_t, ch[2] - ryc_m[:, 0:pw], h_t, -ryc_m[:, pw:]], axis=0)
        return carry

    lax.fori_loop(0, nc // P1_CHUNKS, phase1, 0)

    def block_diag(a, b):
        zero = jnp.zeros((dh, dh), F32)
        return cat([cat([a, zero], axis=1), cat([zero, b], axis=1)], axis=0)

    m_init = tuple((block_diag(s0_ref[z, 0].T, s0_ref[z, 1].T) if has_s0 else jnp.zeros((pw, pw), F32))
                   for z in range(2))
    n_lhs = pw + CHUNK

    def phase2(ci, ms):
        new_ms, ys = [], []
        for z in range(2):
            c = ci if z == 0 else nc - 1 - ci
            out = _mm3(coef_ref[c, z, 0:n_lhs, :], ms[z]) + coef_ref[c, z, n_lhs:, :]
            new_ms.append(out[0:pw])
            ys.append(out[pw:])
        ysp_ref[ci] = cat(ys, axis=1)
        return tuple(new_ms)

    m_fin = lax.fori_loop(0, nc, phase2, m_init)

    if emit_state:
        for z in range(2):
            for j in range(2):
                so_ref[z, j] = m_fin[z][j * dh:(j + 1) * dh, j * dh:(j + 1) * dh].T

    y = cat([ysp_ref[c, :, 0:pw] + ysp_ref[nc - 1 - c, :, pw:] for c in range(nc)], axis=0)
    mu = _mm_exact_rhs(y, seg_b) * (1.0 / dh)
    yc = y - mu
    var = _mm_exact_rhs(yc * yc, seg_b) * (1.0 / dh)
    yn = yc * lax.rsqrt(var + RW_GN_EPS) * lng_ref[...] + lnb_ref[...]
    bonus = _mm_exact_rhs(r * k * rkp_ref[...], seg_b) * v_ref[...]
    g = _mm3(_sigmoid(sm[:, SM_GL:SM_GL + RW_G_RANK]), gup_ref[...])
    y_ref[...] = ((yn + bonus) * g).astype(y_ref.dtype)


def _rwkv(u, p, layer, n_seq, seq, row_off, s0=None, emit_state=False, prev=()):
    n_pairs = RW_HEADS // 2
    pw = 2 * RW_DH
    bpc = MIX_W // pw
    lw = lambda shape: pl.BlockSpec((None,) + shape, lambda b, hp: (layer,) + (0,) * (len(shape) - 1) + (hp,))
    row = lambda a: a.reshape(DEPTH, 1, MIX_W)
    u_pair = lambda cb: pl.BlockSpec((seq, pw), lambda b, hp: (b + row_off, cb * bpc + hp))
    in_specs = [u_pair(CB_RWR), u_pair(CB_RWK), u_pair(CB_RWV),
                pl.BlockSpec((seq, MIX_W), lambda b, hp: (b + row_off, CB_SMALL)),
                lw((2, pw)), lw((2, RW_W_RANK, pw)), lw((2, pw)), lw((2, RW_A_RANK, pw)),
                lw((RW_G_RANK, pw))] + [lw((1, pw))] * 5 + [_full((2, CHUNK, CHUNK))]
    t_idx = np.arange(CHUNK)
    incl_masks = np.stack([t_idx[None, :] <= t_idx[:, None], t_idx[None, :] >= t_idx[:, None]])
    args = [u, u, u, u, p['rw_w0'], p['rw_w_up'], p['rw_a0'], p['rw_a_up'], p['rw_g_up'],
            row(p['rw_k_k']), row(p['rw_k_a']), row(p['rw_r_k']), row(p['rw_ln_g']), row(p['rw_ln_b']),
            jnp.asarray(incl_masks, BF16)]
    st_blk = (2, 2, RW_DH, RW_DH)
    if s0 is not None:
        in_specs.append(pl.BlockSpec((None, None) + st_blk, lambda b, hp: (b, layer, 0, hp, 0, 0)))
        args.append(s0)
    out_shape = [jax.ShapeDtypeStruct((n_seq * seq, MIX_W), BF16)]
    out_specs = [pl.BlockSpec((seq, pw), lambda b, hp: (b, hp))]
    if emit_state:
        out_shape.append(jax.ShapeDtypeStruct((n_seq, DEPTH, 2, RW_HEADS, RW_DH, RW_DH), F32))
        out_specs.append(_state_spec(st_blk, layer, not prev, lambda b, hp: (b, 0, hp, 0, 0)))
    nc = seq // CHUNK
    tok = lambda n: pltpu.VMEM((n, seq, pw) if n else (seq, pw), F32)
    body, prev_specs = _carry_through(
        functools.partial(_rwkv_kernel, seq=seq, has_s0=s0 is not None, emit_state=emit_state), len(in_specs), prev,
        1, 1 if emit_state else 0, layer)
    return pl.pallas_call(
        body,
        out_shape=tuple(out_shape), grid=(n_seq, n_pairs), in_specs=in_specs + prev_specs,
        out_specs=tuple(out_specs),
        input_output_aliases={len(in_specs) + i: 1 + i for i in range(len(prev))},
        scratch_shapes=[tok(0), tok(2), tok(2), tok(2),
                        pltpu.VMEM((nc, 2, 2 * (pw + CHUNK), pw), F32), pltpu.VMEM((nc, CHUNK, 2 * pw), F32)],
        compiler_params=_params(("parallel", "parallel")),
        name=f"rwkv_{seq}",
    )(*args, *prev)


def _log_sigmoid(x):
    return jnp.minimum(x, 0.0) - jnp.log(1.0 + jnp.exp(-jnp.abs(x)))


def _mlstm_kernel(*refs, seq, has_s0, emit_state):
    it = iter(refs)
    q_ref, k_ref, v_ref, o_ref, sm_ref, gr_ref, bc_ref, br_ref, ng_ref = (next(it) for _ in range(9))
    c0_ref, n0_ref, m0_ref = ((next(it), next(it), next(it)) if has_s0 else (None, None, None))
    y_ref = next(it)
    co_ref, no_ref, mo_ref = ((next(it), next(it), next(it)) if emit_state else (None, None, None))
    hs_ref, c_ref, n_ref, m_ref = (next(it) for _ in range(4))

    nc = seq // CHUNK
    n_st = 2 * ML_HEADS
    for i in range(n_st):
        z, h = divmod(i, ML_HEADS)
        c_ref[i] = c0_ref[z, h] if has_s0 else jnp.zeros((ML_DH, ML_DH), F32)
        n_ref[i] = n0_ref[z, h] if has_s0 else jnp.zeros((1, ML_DH), F32)
        m_ref[i] = m0_ref[z, h] if has_s0 else jnp.zeros((1, 1), F32)

    incl, _, ti, si = _tri_masks(CHUNK)
    before = ((si <= ti), (si >= ti))

    nh = ML_HEADS
    chains = [(z, h) for z in range(2) for h in range(nh)]
    each = lambda f, *xs: [f(*a) for a in zip(*xs)]
    zs = [z for z, _ in chains]

    def chunk_step(ci, carry):
        per_dir = []
        for z in range(2):
            c = ci if z == 0 else nc - 1 - ci
            rows = pl.ds(pl.multiple_of(c * CHUNK, CHUNK), CHUNK)
            g0 = SM_GATE + z * 2 * nh
            gc = sm_ref[rows, g0:g0 + 2 * nh] + bc_ref[:, z * 2 * nh:(z + 1) * 2 * nh]
            gr = gr_ref[c, z * 2 * nh:(z + 1) * 2 * nh, :] + br_ref[z * 2 * nh:(z + 1) * 2 * nh, :]
            b_cols = _mm(incl[z], _log_sigmoid(gc[:, nh:]), HI)
            b_rows = _mm(_log_sigmoid(gr[nh:]), incl[1 - z], HI)
            per_dir.append((rows, gc[:, :nh], b_cols, gr[:nh], b_rows))
        rows = [per_dir[z][0] for z, _ in chains]
        hsl = [slice(h * ML_DH, (h + 1) * ML_DH) for _, h in chains]
        i_col = [per_dir[z][1][:, h:h + 1] for z, h in chains]
        b_col = [per_dir[z][2][:, h:h + 1] for z, h in chains]
        i_row = [per_dir[z][3][h:h + 1] for z, h in chains]
        b_row = [per_dir[z][4][h:h + 1] for z, h in chains]
        b_last = each(lambda b, z: b[CHUNK - 1:CHUNK] if z == 0 else b[0:1], b_col, zs)
        m_old = [m_ref[i] for i in range(n_st)]
        cm = [c_ref[i] for i in range(n_st)]
        nv = [n_ref[i] for i in range(n_st)]
        qc = each(lambda r, s: q_ref[r, s] * (ML_DH ** -0.5), rows, hsl)
        kc = each(lambda r, s: k_ref[r, s], rows, hsl)
        vc = each(lambda r, s: v_ref[r, s], rows, hsl)
        qk = each(lambda q, k: _mm_nt(q.astype(BF16), k.astype(BF16)), qc, kc)
        qcm = each(lambda q, c_: _mm_nt(q.astype(BF16), c_.astype(BF16)), qc, cm)
        a_t = each(lambda b, m: b + m, b_col, m_old)
        dmat = each(lambda bc, brw, ir, z: jnp.where(before[z], bc - brw + ir, NEG_INF), b_col, b_row, i_row, zs)
        m_t = each(lambda a, d: jnp.maximum(a, jnp.max(d, axis=-1, keepdims=True)), a_t, dmat)
        s = each(lambda x, d, m: x * jnp.exp(d - m), qk, dmat, m_t)
        inter = each(lambda a, m: jnp.exp(a - m), a_t, m_t)
        sv = each(_mm1, s, vc)
        g_col = each(lambda bl, bc, ic: bl - bc + ic, b_last, b_col, i_col)
        a_l = each(lambda bl, m: bl + m, b_last, m_old)
        m_new = each(lambda a, g: jnp.maximum(a, jnp.max(g, axis=0, keepdims=True)), a_l, g_col)
        wgt = each(lambda g, m: jnp.exp(g - m), g_col, m_new)
        vk = each(lambda v, w, k: _mm1((v * w).T, k), vc, wgt, kc)
        decay = each(lambda a, m: jnp.exp(a - m), a_l, m_new)
        num = each(lambda i_, x, y: i_ * x + y, inter, qcm, sv)
        den = each(lambda i_, q, n_, s_: i_ * jnp.sum(q * n_, axis=-1, keepdims=True)
                   + jnp.sum(s_, axis=-1, keepdims=True), inter, qc, nv, s)
        hh = each(lambda n_, d, m: n_ / jnp.maximum(jnp.abs(d), jnp.exp(-m)), num, den, m_t)
        for z in range(2):
            hs_ref[z, per_dir[z][0], :] = jnp.concatenate(hh[z * nh:(z + 1) * nh], axis=1)
        for i in range(n_st):
            c_ref[i] = decay[i] * cm[i] + vk[i]
            n_ref[i] = decay[i] * nv[i] + jnp.sum(wgt[i] * kc[i], axis=0, keepdims=True)
            m_ref[i] = m_new[i]
        return carry

    lax.fori_loop(0, nc, chunk_step, 0)

    if emit_state:
        for i in range(n_st):
            z, h = divmod(i, ML_HEADS)
            co_ref[z, h] = c_ref[i]
            no_ref[z, h] = n_ref[i]
            mo_ref[z, h] = m_ref[i]

    for h in range(ML_HEADS):
        hsl = slice(h * ML_DH, (h + 1) * ML_DH)
        hn = _head_rms(hs_ref[0, :, hsl] + hs_ref[1, :, hsl], ng_ref[:, hsl])
        y_ref[:, hsl] = (_sigmoid(o_ref[:, hsl]) * hn).astype(y_ref.dtype)


def _mlstm(u, gate_rows, p, layer, n_seq, seq, row_off, state=None, emit_state=False, prev=()):
    lw = lambda shape: pl.BlockSpec((None,) + shape, lambda b: (layer,) + (0,) * len(shape))
    nc = seq // CHUNK
    n_gate = 4 * ML_HEADS
    in_specs = [_u_spec(seq, cb, row_off) for cb in (CB_MLQ, CB_MLK, CB_MLV, CB_MLO, CB_GATE)] + [
        pl.BlockSpec((None, nc, n_gate, CHUNK), lambda b: (b, 0, 0, 0)),
        lw((1, n_gate)), lw((n_gate, 1)), lw((1, MIX_W))]
    args = [u, u, u, u, u, gate_rows, p['ml_gate_b'].reshape(DEPTH, 1, n_gate),
            p['ml_gate_b'].reshape(DEPTH, n_gate, 1), p['ml_norm_g'].reshape(DEPTH, 1, MIX_W)]
    c_shape, n_shape, m_shape = (2, ML_HEADS, ML_DH, ML_DH), (2, ML_HEADS, 1, ML_DH), (2, ML_HEADS, 1, 1)
    if state is not None:
        c0, n0, m0 = state
        for a, shp in ((c0, c_shape), (n0, n_shape), (m0, m_shape)):
            in_specs.append(pl.BlockSpec((None, None) + shp, lambda b: (b, layer, 0, 0, 0, 0)))
            args.append(a.reshape(a.shape[:2] + shp))
    out_shape = [jax.ShapeDtypeStruct((n_seq * seq, MIX_W), BF16)]
    out_specs = [pl.BlockSpec((seq, MIX_W), lambda b: (b, 0))]
    if emit_state:
        for shp in (c_shape, n_shape, m_shape):
            out_shape.append(jax.ShapeDtypeStruct((n_seq, DEPTH) + shp, F32))
            out_specs.append(_state_spec(shp, layer, not prev, lambda b: (b, 0, 0, 0, 0)))
    n_st = 2 * ML_HEADS
    body, prev_specs = _carry_through(
        functools.partial(_mlstm_kernel, seq=seq, has_s0=state is not None, emit_state=emit_state),
        len(in_specs), prev, 1, 3 if emit_state else 0, layer)
    return pl.pallas_call(
        body,
        out_shape=tuple(out_shape), grid=(n_seq,), in_specs=in_specs + prev_specs, out_specs=tuple(out_specs),
        input_output_aliases={len(in_specs) + i: 1 + i for i in range(len(prev))},
        scratch_shapes=[pltpu.VMEM((2, seq, MIX_W), F32), pltpu.VMEM((n_st, ML_DH, ML_DH), F32),
                        pltpu.VMEM((n_st, 1, ML_DH), F32), pltpu.VMEM((n_st, 1, 1), F32)],
        compiler_params=_params(("parallel",)),
        name=f"mlstm_{seq}",
    )(*args, *prev)


def _outproj_kernel(*refs, n_prompt_tiles):
    n_mix = 4
    yp_refs, ys_refs = refs[:n_mix], refs[n_mix:2 * n_mix]
    w_ref, x_ref, g_ref, o_ref = refs[2 * n_mix:]
    is_prompt = pl.program_id(0) < n_prompt_tiles
    acc = None
    for i, (yp_ref, ys_ref) in enumerate(zip(yp_refs, ys_refs)):
        y = jnp.where(is_prompt, yp_ref[...], ys_ref[...])
        part = _mm(y, w_ref[i * MIX_W:(i + 1) * MIX_W, :].astype(BF16))
        acc = part if acc is None else acc + part
    o_ref[...] = x_ref[...] + g_ref[...] * acc


def _outproj(ys_prompt, ys_sample, w_out, x, mod, layer):
    tm, tn = 1024, 1024
    n_prompt_tiles = N_PROMPT // tm
    yp_spec = pl.BlockSpec((tm, MIX_W), lambda i, j: (jnp.minimum(i, n_prompt_tiles - 1), 0))
    ys_spec = pl.BlockSpec((tm, MIX_W), lambda i, j: (jnp.maximum(i - n_prompt_tiles, 0), 0))
    return pl.pallas_call(
        functools.partial(_outproj_kernel, n_prompt_tiles=n_prompt_tiles),
        out_shape=jax.ShapeDtypeStruct((N_TOK, D_MODEL), F32),
        grid=(N_TOK // tm, D_MODEL // tn),
        in_specs=[yp_spec] * 4 + [ys_spec] * 4 + [pl.BlockSpec((None, D_MODEL, tn), lambda i, j: (layer, 0, j)),
                                                   pl.BlockSpec((tm, tn), lambda i, j: (i, j)),
                                                   _mod_spec(layer, 2, tm, tn, with_j=True)],
        out_specs=pl.BlockSpec((tm, tn), lambda i, j: (i, j)),
        compiler_params=_params(("parallel", "parallel")),
        name="outproj",
    )(*ys_prompt, *ys_sample, w_out, x, mod)


def _router_kernel(x_ref, g_ref, sh_ref, sc_ref, rw_ref, rb_ref, xm_ref, gt_ref, ids_ref):
    xm = _modulated(x_ref[...], g_ref[...], sh_ref[...], sc_ref[...])
    xm_ref[...] = xm
    logits = _mm_nt(rw_ref[...], xm, HI)
    ex = jnp.exp(logits - jnp.max(logits, axis=0, keepdims=True))
    scores = ex / jnp.sum(ex, axis=0, keepdims=True)
    sel = scores + rb_ref[...]
    per = N_EXPERTS // N_EXPERT_GROUPS
    s = [sel[e:e + 1, :] for e in range(N_EXPERTS)]
    grp_score = []
    for g in range(N_EXPERT_GROUPS):
        a, b, c, d = s[per * g:per * (g + 1)]
        hi1, lo1, hi2, lo2 = jnp.maximum(a, b), jnp.minimum(a, b), jnp.maximum(c, d), jnp.minimum(c, d)
        grp_score.append(jnp.maximum(hi1, hi2) + jnp.maximum(jnp.minimum(hi1, hi2), jnp.maximum(lo1, lo2)))
    best = functools.reduce(jnp.maximum, grp_score)
    in_grp, taken = [], jnp.zeros_like(best)
    for g in range(N_EXPERT_GROUPS):
        hit = jnp.where(grp_score[g] == best, 1.0, 0.0) * (1.0 - taken)
        in_grp.append(hit)
        taken = taken + hit
    picked, flag = [], []
    for e in range(N_EXPERTS):
        g = e // per
        rank = jnp.zeros_like(best)
        for o in range(per * g, per * (g + 1)):
            if o < e:
                rank += jnp.where(s[o] >= s[e], 1.0, 0.0)
            elif o > e:
                rank += jnp.where(s[o] > s[e], 1.0, 0.0)
        flag.append(in_grp[g] * jnp.where(rank < 2.0, 1.0, 0.0))
        picked.append(flag[e] * scores[e:e + 1, :])
    total = functools.reduce(lambda x, y: x + y, picked)
    for e in range(N_EXPERTS):
        gt_ref[e:e + 1, :] = picked[e] / total
    lo_id = functools.reduce(
        jnp.minimum, [jnp.where(flag[e] > 0.0, float(e), float(N_EXPERTS)) for e in range(N_EXPERTS)])
    hi_id = functools.reduce(jnp.maximum, [jnp.where(flag[e] > 0.0, float(e), -1.0) for e in range(N_EXPERTS)])
    ids_ref[0:1, :] = lo_id
    ids_ref[1:2, :] = hi_id
    ids_ref[2:, :] = jnp.zeros((ids_ref.shape[0] - 2,) + lo_id.shape[1:], F32)


def _router(x, norm_g, mod, router_wt, router_b, layer):
    tm = 512
    return pl.pallas_call(
        _router_kernel,
        out_shape=(jax.ShapeDtypeStruct((N_TOK, D_MODEL), F32), jax.ShapeDtypeStruct((N_EXPERTS, N_TOK), F32),
                   jax.ShapeDtypeStruct((8, N_TOK), F32)),
        grid=(N_TOK // tm,),
        in_specs=[pl.BlockSpec((tm, D_MODEL), lambda i: (i, 0)),
                  pl.BlockSpec((None, 1, D_MODEL), lambda i: (layer, 0, 0)),
                  _mod_spec(layer, 3, tm), _mod_spec(layer, 4, tm),
                  _full((N_EXPERTS, D_MODEL)), _full((N_EXPERTS, 1))],
        out_specs=(pl.BlockSpec((tm, D_MODEL), lambda i: (i, 0)), pl.BlockSpec((N_EXPERTS, tm), lambda i: (0, i)),
                   pl.BlockSpec((8, tm), lambda i: (0, i))),
        compiler_params=_params(("parallel",)),
        name="router",
    )(x, norm_g.reshape(DEPTH, 1, D_MODEL), mod, mod, router_wt, router_b.reshape(N_EXPERTS, 1))


MOE_TM = 512
N_PAIRS = N_EXPERT_GROUPS * 6
MOE_STEPS = 2 * (N_PAIRS + N_TOK // MOE_TM - 1)


def _row_copy(src_hbm, src_row, dst_ref, slot, r, sems):
    return pltpu.make_async_copy(src_hbm.at[pl.ds(src_row, 1), :], dst_ref.at[slot, pl.ds(r, 1), :], sems.at[slot])


def _gather_start(src_hbm, dst_ref, slot, idx_ref, base, n, sems):
    def issue(r, carry):
        _row_copy(src_hbm, idx_ref[base + r], dst_ref, slot, r, sems).start()
        return carry

    lax.fori_loop(0, n, issue, 0, unroll=8)


def _gather_wait(src_hbm, dst_ref, slot, n, sems):
    def wait(r, carry):
        _row_copy(src_hbm, 0, dst_ref, slot, r, sems).wait()
        return carry

    lax.fori_loop(0, n, wait, 0, unroll=8)


def _gather_tile(src_hbm, dst_ref, idx_ref, tile, n_tiles, sems):
    slot = tile % 2

    @pl.when(tile == 0)
    def _():
        _gather_start(src_hbm, dst_ref, slot, idx_ref, 0, MOE_TM, sems)

    _gather_wait(src_hbm, dst_ref, slot, MOE_TM, sems)

    @pl.when(tile + 1 < n_tiles)
    def _():
        _gather_start(src_hbm, dst_ref, 1 - slot, idx_ref, (tile + 1) * MOE_TM, MOE_TM, sems)

    return slot


def _moe_plan(ids):
    i32 = jnp.int32
    lo, hi = ids[0].astype(i32), ids[1].astype(i32)
    src = jnp.argsort(lo * N_EXPERTS + hi).astype(i32)
    pos = jnp.argsort(src).astype(i32)
    n_tiles = N_TOK // MOE_TM
    ex = jnp.arange(N_EXPERTS, dtype=i32)
    lo_s, hi_s = lo[src].reshape(n_tiles, MOE_TM, 1), hi[src].reshape(n_tiles, MOE_TM, 1)
    used = ((lo_s == ex).any(axis=1) | (hi_s == ex).any(axis=1)).reshape(-1)
    n_valid = jnp.sum(used).astype(i32)
    idx = jnp.nonzero(used, size=MOE_STEPS, fill_value=0)[0].astype(i32)
    valid = jnp.arange(MOE_STEPS, dtype=i32) < n_valid
    idx = jnp.where(valid, idx, idx[jnp.maximum(n_valid - 1, 0)])
    tile, exp = idx // N_EXPERTS, idx % N_EXPERTS
    first = valid & (tile != jnp.concatenate([jnp.full((1,), -1, i32), tile[:-1]]))
    return src, pos, tile, exp, first.astype(i32), valid.astype(i32)


def _moe_kernel(tile_ref, exp_ref, first_ref, valid_ref, src_ref, xm_hbm, gates_ref, w1_ref, w3_ref, w2_ref,
                o_ref, xs_ref, xb_ref, sem):
    s = pl.program_id(0)

    @pl.when(first_ref[s] == 1)
    def _():
        slot = _gather_tile(xm_hbm, xs_ref, src_ref, tile_ref[s], N_TOK // MOE_TM, sem)
        xb_ref[...] = xs_ref[slot].astype(BF16)
        o_ref[...] = jnp.zeros_like(o_ref)

    @pl.when(valid_ref[s] == 1)
    def _():
        xb = xb_ref[...]
        h1 = _mm(xb, w1_ref[...].astype(BF16))
        h3 = _mm(xb, w3_ref[...].astype(BF16))
        gates = gates_ref[...]
        lane = lax.broadcasted_iota(jnp.int32, gates.shape, 1)
        gate = jnp.sum(jnp.where(lane == exp_ref[s], gates, 0.0), axis=-1, keepdims=True)
        hh = h1 * _sigmoid(h1) * h3 * gate
        o_ref[...] += _mm(hh.astype(BF16), w2_ref[...].astype(BF16))


def _moe(xm, gates_sorted, plan, w1, w3, w2, layer):
    src, _, tile, exp, first, valid = plan
    w_in = pl.BlockSpec((None, None, D_MODEL, D_EXPERT), lambda s, t, e, f, v, i: (layer, e[s], 0, 0))
    grid_spec = pltpu.PrefetchScalarGridSpec(
        num_scalar_prefetch=5, grid=(MOE_STEPS,),
        in_specs=[pl.BlockSpec(memory_space=pl.ANY),
                  pl.BlockSpec((MOE_TM, N_EXPERTS), lambda s, t, e, f, v, i: (t[s], 0)),
                  w_in, w_in,
                  pl.BlockSpec((None, None, D_EXPERT, D_MODEL), lambda s, t, e, f, v, i: (layer, e[s], 0, 0))],
        out_specs=pl.BlockSpec((MOE_TM, D_MODEL), lambda s, t, e, f, v, i: (t[s], 0)),
        scratch_shapes=[pltpu.VMEM((2, MOE_TM, D_MODEL), F32), pltpu.VMEM((MOE_TM, D_MODEL), BF16),
                        pltpu.SemaphoreType.DMA((2,))])
    return pl.pallas_call(
        _moe_kernel, out_shape=jax.ShapeDtypeStruct((N_TOK, D_MODEL), F32), grid_spec=grid_spec,
        compiler_params=_params(("arbitrary",)), name="moe",
    )(tile, exp, first, valid, src, xm, gates_sorted, w1, w3, w2)


def _combine_kernel(pos_ref, acc_hbm, x_ref, g_ref, *refs, split):
    i = pl.program_id(0)
    if split:
        *out_refs, buf_ref, sem = refs
    else:
        ng_ref, nsh_ref, nsc_ref, *out_refs, buf_ref, sem = refs
    slot = _gather_tile(acc_hbm, buf_ref, pos_ref, i, N_TOK // MOE_TM, sem)
    val = x_ref[...] + g_ref[...] * buf_ref[slot]
    if not split:
        out_refs[0][...] = val
        out_refs[1][...] = _modulated(val, ng_ref[...], nsh_ref[...], nsc_ref[...]).astype(BF16)
        return
    n_prompt_tiles = N_PROMPT // MOE_TM

    @pl.when(i < n_prompt_tiles)
    def _():
        out_refs[0][...] = val

    @pl.when(i >= n_prompt_tiles)
    def _():
        out_refs[1][...] = val


def _combine(acc_sorted, pos, x, mod, norm1_g, layer):
    tm = MOE_TM
    split = layer == DEPTH - 1
    n_prompt_tiles = N_PROMPT // tm
    row = pl.BlockSpec((tm, D_MODEL), lambda i, p: (i, 0))
    mod_row = lambda l, chunk: pl.BlockSpec((None, None, None, 1, D_MODEL),
                                            lambda i, p: (l, _mod_row(i, tm), chunk, 0, 0))
    in_specs = [pl.BlockSpec(memory_space=pl.ANY), row, mod_row(layer, 5)]
    args = [pos, acc_sorted, x, mod]
    if split:
        out_shape = (jax.ShapeDtypeStruct((N_PROMPT, D_MODEL), F32), jax.ShapeDtypeStruct((N_SAMPLE, D_MODEL), F32))
        out_specs = (pl.BlockSpec((tm, D_MODEL), lambda i, p: (jnp.minimum(i, n_prompt_tiles - 1), 0)),
                     pl.BlockSpec((tm, D_MODEL), lambda i, p: (jnp.maximum(i - n_prompt_tiles, 0), 0)))
    else:
        in_specs += [pl.BlockSpec((None, 1, D_MODEL), lambda i, p: (layer + 1, 0, 0)),
                     mod_row(layer + 1, 0), mod_row(layer + 1, 1)]
        args += [norm1_g.reshape(DEPTH, 1, D_MODEL), mod, mod]
        out_shape = (jax.ShapeDtypeStruct((N_TOK, D_MODEL), F32), jax.ShapeDtypeStruct((N_TOK, D_MODEL), BF16))
        out_specs = (row, row)
    grid_spec = pltpu.PrefetchScalarGridSpec(
        num_scalar_prefetch=1, grid=(N_TOK // tm,), in_specs=in_specs, out_specs=out_specs,
        scratch_shapes=[pltpu.VMEM((2, tm, D_MODEL), F32), pltpu.SemaphoreType.DMA((2,))])
    return pl.pallas_call(
        functools.partial(_combine_kernel, split=split), out_shape=out_shape, grid_spec=grid_spec,
        compiler_params=_params(("arbitrary",)), name="combine",
    )(*args)


def _gate_rows(gcols, n_seq, seq):
    return gcols.reshape(n_seq, seq // CHUNK, CHUNK, gcols.shape[-1]).transpose(0, 1, 3, 2)


def kernel(x_prompt, x_sample, cache_na_k, cache_na_v, state_rwkv, state_mlstm_c, state_mlstm_n, state_mlstm_m,
           c, c_ctx, norm1_g, norm2_g, w_mod, b_mod, w_in, conv_w, na_q_g, na_k_g, na_rpb, rw_w0, rw_w_up, rw_a0,
           rw_a_up, rw_g_up, rw_k_k, rw_k_a, rw_r_k, rw_ln_g, rw_ln_b, ml_gate_b, ml_norm_g, w_out, router_w,
           router_b, moe_w1, moe_w3, moe_w2):
    p = dict(rw_w0=rw_w0, rw_w_up=rw_w_up, rw_a0=rw_a0, rw_a_up=rw_a_up, rw_g_up=rw_g_up, rw_k_k=rw_k_k,
             rw_k_a=rw_k_a, rw_r_k=rw_r_k, rw_ln_g=rw_ln_g, rw_ln_b=rw_ln_b, ml_gate_b=ml_gate_b,
             ml_norm_g=ml_norm_g)
    cvecs = jnp.concatenate([c_ctx[None], c, jnp.zeros((MOD_ROWS - 1 - DEC_BATCH, D_MODEL), F32)], axis=0)
    mod = _adaln(cvecs, w_mod, b_mod).reshape(DEPTH, MOD_ROWS, 6, 1, D_MODEL)

    assert w_in.shape[-1] == P_IN
    w_in_t = jnp.swapaxes(w_in, 1, 2)
    tw = _na_tables(na_rpb)
    router_wt = router_w.T
    sample_row_off = N_PROMPT // DEC_SEQ

    x, xm1 = _premod(x_prompt.reshape(N_PROMPT, D_MODEL), x_sample.reshape(N_SAMPLE, D_MODEL), norm1_g, mod)
    kv_prev, rw_prev, ml_prev = (), (), ()
    for l in range(DEPTH):
        u = _inproj(xm1, w_in_t, l)
        ya_p, yb_p, *kv_prev = _attn_prompt(u, conv_w, na_q_g, na_k_g, l, prev=tuple(kv_prev))
        ya_s, yb_s = _na_sample(u, cache_na_k, cache_na_v, tw, conv_w, na_q_g, na_k_g, l)
        yc_p, *rw_prev = _rwkv(u, p, l, BATCH, SEQ, 0, emit_state=True, prev=tuple(rw_prev))
        (yc_s,) = _rwkv(u, p, l, DEC_BATCH, DEC_SEQ, sample_row_off, s0=state_rwkv)
        g0 = CB_GATE * MIX_W + SM_GATE
        gcols = u[:, g0:g0 + 4 * ML_HEADS]
        yd_p, *ml_prev = _mlstm(u, _gate_rows(gcols[:N_PROMPT], BATCH, SEQ), p, l, BATCH, SEQ, 0,
                                emit_state=True, prev=tuple(ml_prev))
        (yd_s,) = _mlstm(u, _gate_rows(gcols[N_PROMPT:], DEC_BATCH, DEC_SEQ), p, l, DEC_BATCH, DEC_SEQ,
                         sample_row_off, state=(state_mlstm_c, state_mlstm_n, state_mlstm_m))
        x = _outproj((ya_p, yb_p, yc_p, yd_p), (ya_s, yb_s, yc_s, yd_s), w_out, x, mod, l)
        xm, gates_t, ids = _router(x, norm2_g, mod, router_wt, router_b, l)
        plan = _moe_plan(ids)
        acc_sorted = _moe(xm, gates_t.T[plan[0]], plan, moe_w1, moe_w3, moe_w2, l)
        combined = _combine(acc_sorted, plan[1], x, mod, norm1_g, l)
        if l < DEPTH - 1:
            x, xm1 = combined
    y_prompt, y_sample = combined
    new_c, new_n, new_m = ml_prev
    return (y_prompt.reshape(BATCH, SEQ, D_MODEL), y_sample.reshape(DEC_BATCH, DEC_SEQ, D_MODEL),
            kv_prev[0], kv_prev[1], rw_prev[0], new_c, new_n.reshape(BATCH, DEPTH, 2, ML_HEADS, ML_DH),
            new_m.reshape(BATCH, DEPTH, 2, ML_HEADS))
```
